```python
import jax, jax.numpy as jnp
from jax import lax
import numpy as np

D_MODEL = 1024
BATCH = 4
SEQ = 4096
DEPTH = 1

EPS = 1e-6
D_MIX = D_MODEL
HG_HEADS = 4
HG_DK = 128
HG_DV = 128
HG_WIDTH = HG_HEADS * HG_DV
HG_CHUNK = 64
NSA_HEADS = 8
NSA_KV_GROUPS = 2
NSA_HD = 64
NSA_WIDTH = NSA_HEADS * NSA_HD
NSA_KV = NSA_KV_GROUPS * NSA_HD
CMP_BLOCK = 32
CMP_STRIDE = 16
CMP_HIDDEN = 256
SEL_BLOCK = 64
N_SEL = 16
WINDOW = 512
Q_BLOCK = 128
ROPE_THETA = 10000.0
N_EXPERTS = 64
TOP_K = 8
N_GROUPS = 8
TOPK_GROUPS = 4
EXPERT_HIDDEN = 256
SHARED_HIDDEN = 256
ROUTED_SCALE = 2.5
EXPERT_BLOCK = 128
IN_SPLITS = (HG_WIDTH,) * 4 + (NSA_WIDTH,) + (NSA_KV,) * 6 + (3 * NSA_HEADS,)
IN_COLS = 4 * HG_WIDTH + NSA_WIDTH + 6 * NSA_KV + 3 * NSA_HEADS

kernel_name = "hymba_hgrn2_nsa_moe_adaln"


def rmsnorm(x, g):
    xf = x.astype(jnp.float32)
    y = xf * lax.rsqrt(jnp.mean(xf * xf, axis=-1, keepdims=True) + EPS)
    return (y * g.astype(jnp.float32)).astype(x.dtype)


def rope(t, pos):
    half = t.shape[-1] // 2
    inv = ROPE_THETA ** (-jnp.arange(half, dtype=jnp.float32) / half)
    ang = pos.astype(jnp.float32)[:, None] * inv[None, :]
    cos, sin = jnp.cos(ang), jnp.sin(ang)
    t1 = t[..., :half].astype(jnp.float32)
    t2 = t[..., half:].astype(jnp.float32)
    return jnp.concatenate([t1 * cos - t2 * sin, t2 * cos + t1 * sin], axis=-1).astype(t.dtype)


def swiglu(t, w_gu, w_dn):
    g, u = jnp.split(t @ w_gu, 2, axis=-1)
    return (jax.nn.silu(g) * u) @ w_dn


def hgrn2_mixer(q, f_logit, i, gate, lb, norm_g):
    B, S, _ = q.shape
    f32 = jnp.float32
    f = lb + (1.0 - lb) * jax.nn.sigmoid(f_logit.astype(f32))
    log_f = jnp.log(f)
    k = 1.0 - f
    n_ch = S // HG_CHUNK

    def to_chunks(t, d):
        return t.reshape(B, n_ch, HG_CHUNK, HG_HEADS, d).transpose(1, 0, 3, 2, 4)

    qc = to_chunks(q.astype(f32) * HG_DK ** -0.5, HG_DK)
    kc = to_chunks(k, HG_DK)
    vc = to_chunks(i.astype(f32), HG_DV)
    bc = jnp.cumsum(to_chunks(log_f, HG_DK), axis=3)
    causal = jnp.tril(jnp.ones((HG_CHUNK, HG_CHUNK), bool))

    def step(state, inp):
        q_, k_, v_, b_ = inp
        diff = jnp.where(causal[:, :, None], b_[:, :, :, None, :] - b_[:, :, None, :, :], -jnp.inf)
        a = jnp.einsum('bhtd,bhtsd,bhsd->bhts', q_, jnp.exp(diff), k_)
        o = a @ v_ + jnp.einsum('bhtd,bhde->bhte', q_ * jnp.exp(b_), state)
        b_last = b_[:, :, -1]
        state = jnp.exp(b_last)[..., None] * state + jnp.einsum(
            'bhsd,bhse->bhde', k_ * jnp.exp(b_last[:, :, None, :] - b_), v_)
        return state, o

    s0 = jnp.zeros((B, HG_HEADS, HG_DK, HG_DV), f32)
    _, o = lax.scan(step, s0, (qc, kc, vc, bc))
    o = o.transpose(1, 0, 3, 2, 4).reshape(B, S, HG_HEADS, HG_DV)
    o = o * lax.rsqrt(jnp.mean(o * o, axis=-1, keepdims=True) + EPS) * norm_g.astype(f32).reshape(HG_HEADS, HG_DV)
    o = o.reshape(B, S, HG_WIDTH) * jax.nn.silu(gate.astype(f32))
    return o.astype(q.dtype)


def nsa_mixer(q, kc, vc, ks, vs, kw, vw, gate_logit,
              pos_k, w1_k, b1_k, w2_k, pos_v, w1_v, b1_v, w2_v):
    B, S, _ = q.shape
    G, HG, HD = NSA_KV_GROUPS, NSA_HEADS // NSA_KV_GROUPS, NSA_HD
    f32 = jnp.float32
    pos = jnp.arange(S)
    scale = HD ** -0.5
    qh = q.reshape(B, S, G, HG, HD).transpose(0, 2, 3, 1, 4)
    q_rot = rope(qh, pos)

    def kv_heads(t):
        return t.reshape(B, S, G, HD).transpose(0, 2, 1, 3)

    n_cmp = (S - CMP_BLOCK) // CMP_STRIDE + 1
    tok_idx = jnp.arange(n_cmp)[:, None] * CMP_STRIDE + jnp.arange(CMP_BLOCK)[None, :]

    def compress(t, pe, w1, b1, w2):
        blocks = kv_heads(t)[:, :, tok_idx] + pe
        hdn = jax.nn.gelu(blocks.reshape(B, G, n_cmp, CMP_BLOCK * HD) @ w1 + b1)
        return hdn @ w2

    k_cmp = compress(kc, pos_k, w1_k, b1_k, w2_k)
    v_cmp = compress(vc, pos_v, w1_v, b1_v, w2_v)
    s_cmp = jnp.einsum('bghsd,bgcd->bghsc', qh, k_cmp).astype(f32) * scale
    c_start = jnp.arange(n_cmp) * CMP_STRIDE
    cmp_mask = (c_start + CMP_BLOCK - 1)[None, :] <= pos[:, None]
    p_cmp = jax.nn.softmax(jnp.where(cmp_mask, s_cmp, -1e30), axis=-1) * cmp_mask
    o_cmp = jnp.einsum('bghsc,bgcd->bghsd', p_cmp, v_cmp.astype(f32))

    n_sb = S // SEL_BLOCK
    n_sel = min(N_SEL, n_sb)
    s_start = jnp.arange(n_sb) * SEL_BLOCK
    overlap = jnp.clip(jnp.minimum(c_start[:, None] + CMP_BLOCK, s_start[None, :] + SEL_BLOCK)
                       - jnp.maximum(c_start[:, None], s_start[None, :]), 0, None).astype(f32) / CMP_BLOCK
    p_slc = jnp.einsum('bghsc,cj->bgsj', p_cmp, overlap)
    cur = pos // SEL_BLOCK
    blk = jnp.arange(n_sb)
    available = blk[None, :] <= cur[:, None]
    forced = (blk[None, :] == 0) | (blk[None, :] == cur[:, None]) | (blk[None, :] == cur[:, None] - 1)
    sel_score = jnp.where(forced, 1e30, jnp.where(available, p_slc, -1e30))
    top_val, top_idx = lax.top_k(sel_score, n_sel)
    top_valid = top_val > -1e29

    n_qb = S // Q_BLOCK
    k_blocks = rope(kv_heads(ks), pos).reshape(B, G, n_sb, SEL_BLOCK * HD)
    v_blocks = kv_heads(vs).reshape(B, G, n_sb, SEL_BLOCK * HD)
    bi = jnp.arange(B)[:, None, None]
    gi = jnp.arange(G)[None, :, None]

    def by_qblock(t, axis):
        shape = t.shape[:axis] + (n_qb, Q_BLOCK) + t.shape[axis + 1:]
        return jnp.moveaxis(t.reshape(shape), axis, 0)

    def slc_block(args):
        qb, idx, valid, qpos = args
        flat = idx.reshape(B, G, Q_BLOCK * n_sel)
        kg = k_blocks[bi, gi, flat].reshape(B, G, Q_BLOCK, n_sel, SEL_BLOCK, HD)
        vg = v_blocks[bi, gi, flat].reshape(B, G, Q_BLOCK, n_sel, SEL_BLOCK, HD)
        s = jnp.einsum('bghqd,bgqnld->bghqnl', qb, kg).astype(f32) * scale
        kpos = idx[..., None] * SEL_BLOCK + jnp.arange(SEL_BLOCK)
        ok = valid[..., None] & (kpos <= qpos[:, None, None])
        s = jnp.where(ok[:, :, None], s, -1e30).reshape(B, G, HG, Q_BLOCK, n_sel * SEL_BLOCK)
        p = jax.nn.softmax(s, axis=-1).reshape(B, G, HG, Q_BLOCK, n_sel, SEL_BLOCK)
        return jnp.einsum('bghqnl,bgqnld->bghqd', p, vg.astype(f32))

    o_slc = lax.map(slc_block, (by_qblock(q_rot, 3), by_qblock(top_idx, 2),
                                by_qblock(top_valid, 2), pos.reshape(n_qb, Q_BLOCK)))
    o_slc = jnp.moveaxis(o_slc, 0, 3).reshape(B, G, HG, S, HD)

    n_back = WINDOW // Q_BLOCK

    def banded(t):
        tp = jnp.pad(t, ((0, 0), (0, 0), (WINDOW, 0), (0, 0))).reshape(B, G, n_qb + n_back, Q_BLOCK, HD)
        return jnp.concatenate([tp[:, :, o:o + n_qb] for o in range(n_back + 1)], axis=3)

    kwb = banded(rope(kv_heads(kw), pos))
    vwb = banded(kv_heads(vw))
    qw = q_rot.reshape(B, G, HG, n_qb, Q_BLOCK, HD)
    s_w = jnp.einsum('bghnqd,bgnkd->bghnqk', qw, kwb).astype(f32) * scale
    qpos = pos.reshape(n_qb, Q_BLOCK)
    kpos = (jnp.arange(n_qb)[:, None] - n_back) * Q_BLOCK + jnp.arange((n_back + 1) * Q_BLOCK)[None, :]
    d = qpos[:, :, None] - kpos[:, None, :]
    ok_w = (d >= 0) & (d < WINDOW) & (kpos[:, None, :] >= 0)
    p_w = jax.nn.softmax(jnp.where(ok_w, s_w, -1e30), axis=-1)
    o_win = jnp.einsum('bghnqk,bgnkd->bghnqd', p_w, vwb.astype(f32)).reshape(B, G, HG, S, HD)

    gates = jax.nn.sigmoid(gate_logit.astype(f32)).reshape(B, S, G, HG, 3).transpose(0, 2, 3, 1, 4)
    o = gates[..., 0:1] * o_cmp + gates[..., 1:2] * o_slc + gates[..., 2:3] * o_win
    return o.transpose(0, 3, 1, 2, 4).reshape(B, S, NSA_WIDTH).astype(q.dtype)


def routed_experts(t, top_i, top_w, w_gu, w_dn):
    T, D = t.shape
    A = T * TOP_K
    e_flat = top_i.reshape(A).astype(jnp.int32)
    w_flat = top_w.reshape(A).astype(jnp.float32)
    order = jnp.argsort(e_flat)
    e_sorted = e_flat[order]
    tok_sorted = (order // TOP_K).astype(jnp.int32)
    counts = jnp.bincount(e_flat, length=N_EXPERTS)
    start = jnp.cumsum(counts) - counts
    padded = (counts + EXPERT_BLOCK - 1) // EXPERT_BLOCK * EXPERT_BLOCK
    pad_end = jnp.cumsum(padded)
    pad_start = pad_end - padded
    dest = pad_start[e_sorted] + jnp.arange(A) - start[e_sorted]
    n_blk = -(-A // EXPERT_BLOCK) + N_EXPERTS
    P = n_blk * EXPERT_BLOCK
    rows = jnp.full((P,), T, jnp.int32).at[dest].set(tok_sorted)
    wts = jnp.zeros((P,), jnp.float32).at[dest].set(w_flat[order])
    blk_expert = jnp.minimum(jnp.searchsorted(pad_end, jnp.arange(n_blk) * EXPERT_BLOCK, side='right'),
                             N_EXPERTS - 1)
    t_pad = jnp.concatenate([t, jnp.zeros((1, D), t.dtype)], axis=0)

    def one_block(args):
        rb, e = args
        return swiglu(t_pad[rb], w_gu[e], w_dn[e]).astype(jnp.float32)

    ys = lax.map(one_block, (rows.reshape(n_blk, EXPERT_BLOCK), blk_expert)).reshape(P, D)
    out = jnp.zeros((T + 1, D), jnp.float32).at[rows].add(ys * wts[:, None])
    return out[:T]


def moe_ffn(h, router_w, router_bias, w_gu, w_dn, w_sh_gu, w_sh_dn):
    B, S, D = h.shape
    t = h.reshape(B * S, D)
    scores = jax.nn.sigmoid((t @ router_w).astype(jnp.float32))
    choice = scores + router_bias.astype(jnp.float32)
    grp = choice.reshape(B * S, N_GROUPS, N_EXPERTS // N_GROUPS)
    grp_score = lax.top_k(grp, 2)[0].sum(-1)
    top_g = lax.top_k(grp_score, TOPK_GROUPS)[1]
    gmask = jnp.any(top_g[..., None] == jnp.arange(N_GROUPS), axis=-2)
    emask = jnp.repeat(gmask, N_EXPERTS // N_GROUPS, axis=-1)
    top_i = lax.top_k(jnp.where(emask, choice, -jnp.inf), TOP_K)[1]
    top_w = jnp.take_along_axis(scores, top_i, axis=-1)
    top_w = top_w / jnp.sum(top_w, axis=-1, keepdims=True) * ROUTED_SCALE
    routed = routed_experts(t, top_i, top_w, w_gu, w_dn)
    shared = swiglu(t, w_sh_gu, w_sh_dn).astype(jnp.float32)
    return (routed + shared).reshape(B, S, D).astype(h.dtype)


def setup_inputs(seed: int = 0) -> dict:
    key = jax.random.key(seed)
    ks = jax.random.split(key, 26)
    L = DEPTH

    def nrm(k, shape, s):
        return jax.random.normal(k, shape, jnp.float32) * s

    return {
        "x": nrm(ks[0], (BATCH, SEQ, D_MODEL), 1.0),
        "c": nrm(ks[1], (BATCH, D_MODEL), 1.0),
        "w_ada": nrm(ks[2], (L, D_MODEL, 6 * D_MODEL), 0.5 * D_MODEL ** -0.5),
        "b_ada": nrm(ks[3], (L, 6 * D_MODEL), 0.02),
        "norm1_g": 1.0 + nrm(ks[4], (L, D_MODEL), 0.02),
        "w_in": nrm(ks[5], (L, D_MODEL, IN_COLS), D_MODEL ** -0.5),
        "hg_lb_logits": nrm(ks[6], (L + 1, HG_WIDTH), 0.5),
        "hg_norm_g": 1.0 + nrm(ks[7], (L, HG_WIDTH), 0.02),
        "cmp_pos_k": nrm(ks[8], (L, CMP_BLOCK, NSA_HD), 0.1),
        "cmp_w1_k": nrm(ks[9], (L, CMP_BLOCK * NSA_HD, CMP_HIDDEN), (CMP_BLOCK * NSA_HD) ** -0.5),
        "cmp_b1_k": nrm(ks[10], (L, CMP_HIDDEN), 0.02),
        "cmp_w2_k": nrm(ks[11], (L, CMP_HIDDEN, NSA_HD), CMP_HIDDEN ** -0.5),
        "cmp_pos_v": nrm(ks[12], (L, CMP_BLOCK, NSA_HD), 0.1),
        "cmp_w1_v": nrm(ks[13], (L, CMP_BLOCK * NSA_HD, CMP_HIDDEN), (CMP_BLOCK * NSA_HD) ** -0.5),
        "cmp_b1_v": nrm(ks[14], (L, CMP_HIDDEN), 0.02),
        "cmp_w2_v": nrm(ks[15], (L, CMP_HIDDEN, NSA_HD), CMP_HIDDEN ** -0.5),
        "w_out": nrm(ks[16], (L, D_MIX, D_MODEL), D_MIX ** -0.5),
        "norm2_g": 1.0 + nrm(ks[17], (L, D_MODEL), 0.02),
        "router_w": nrm(ks[18], (L, D_MODEL, N_EXPERTS), D_MODEL ** -0.5),
        "router_bias": nrm(ks[19], (L, N_EXPERTS), 0.01),
        "w_exp_gu": nrm(ks[20], (L, N_EXPERTS, D_MODEL, 2 * EXPERT_HIDDEN), D_MODEL ** -0.5),
        "w_exp_dn": nrm(ks[21], (L, N_EXPERTS, EXPERT_HIDDEN, D_MODEL), EXPERT_HIDDEN ** -0.5),
        "w_sh_gu": nrm(ks[22], (L, D_MODEL, 2 * SHARED_HIDDEN), D_MODEL ** -0.5),
        "w_sh_dn": nrm(ks[23], (L, SHARED_HIDDEN, D_MODEL), SHARED_HIDDEN ** -0.5),
        "final_g": 1.0 + nrm(ks[24], (D_MODEL,), 0.02),
    }


def reference(x, c, w_ada, b_ada, norm1_g, w_in, hg_lb_logits, hg_norm_g,
              cmp_pos_k, cmp_w1_k, cmp_b1_k, cmp_w2_k, cmp_pos_v, cmp_w1_v, cmp_b1_v, cmp_w2_v,
              w_out, norm2_g, router_w, router_bias, w_exp_gu, w_exp_dn, w_sh_gu, w_sh_dn, final_g):
    offsets = np.cumsum(IN_SPLITS)[:-1].tolist()
    lb_all = jnp.cumsum(jax.nn.softmax(hg_lb_logits.astype(jnp.float32), axis=0), axis=0)
    c_act = jax.nn.silu(c)
    for l in range(DEPTH):
        mod = (c_act @ w_ada[l] + b_ada[l])[:, None, :]
        sh1, sc1, g1, sh2, sc2, g2 = jnp.split(mod, 6, axis=-1)
        h = rmsnorm(x, norm1_g[l]) * (1.0 + sc1) + sh1
        hq, hf, hi, hgt, nq, kc, vc, ksl, vsl, kwn, vwn, ngate = jnp.split(h @ w_in[l], offsets, axis=-1)
        o_hg = hgrn2_mixer(hq, hf, hi, hgt, lb_all[l], hg_norm_g[l])
        o_nsa = nsa_mixer(nq, kc, vc, ksl, vsl, kwn, vwn, ngate,
                          cmp_pos_k[l], cmp_w1_k[l], cmp_b1_k[l], cmp_w2_k[l],
                          cmp_pos_v[l], cmp_w1_v[l], cmp_b1_v[l], cmp_w2_v[l])
        x = x + g1 * (jnp.concatenate([o_hg, o_nsa], axis=-1) @ w_out[l])
        h = rmsnorm(x, norm2_g[l]) * (1.0 + sc2) + sh2
        x = x + g2 * moe_ffn(h, router_w[l], router_bias[l], w_exp_gu[l], w_exp_dn[l], w_sh_gu[l], w_sh_dn[l])
    return rmsnorm(x, final_g)
```

```python
import functools

import numpy as np
import jax
import jax.numpy as jnp
from jax import lax
from jax.experimental import pallas as pl
from jax.experimental.pallas import tpu as pltpu

F32 = jnp.float32
BF16 = jnp.bfloat16
HIGHEST = lax.Precision.HIGHEST

D_MODEL = 1024
EPS = 1e-6
HG_HEADS = 4
HG_DK = 128
HG_DV = 128
HG_WIDTH = HG_HEADS * HG_DV
HG_CHUNK = 64
HG_SUB = 16
NSA_HEADS = 8
NSA_KV_GROUPS = 2
NSA_HG = NSA_HEADS // NSA_KV_GROUPS
NSA_HD = 64
NSA_WIDTH = NSA_HEADS * NSA_HD
NSA_KV = NSA_KV_GROUPS * NSA_HD
CMP_BLOCK = 32
CMP_STRIDE = 16
CMP_HIDDEN = 256
SEL_BLOCK = 64
N_SEL = 16
WINDOW = 512
ROPE_THETA = 10000.0
N_EXPERTS = 64
TOP_K = 8
N_GROUPS = 8
TOPK_GROUPS = 4
EXPERT_HIDDEN = 256
ROUTED_SCALE = 2.5

LANES = 128
NEG = -1e30
VMEM_LIMIT = 56 * 1024 * 1024

C_HG = 0
C_Q = 4 * HG_WIDTH
C_KV = C_Q + NSA_HEADS * LANES
C_GATE = C_KV + 6 * NSA_KV
IN_COLS_P = C_GATE + LANES

NT = (((1,), (1,)), ((), ()))
TN = (((0,), (0,)), ((), ()))


def _cparams(sem):
    return pltpu.CompilerParams(dimension_semantics=sem, vmem_limit_bytes=VMEM_LIMIT)


def _sigmoid(x):
    return 1.0 / (1.0 + jnp.exp(-x))


def _silu(x):
    return x * _sigmoid(x)


def _gelu_tanh(x):
    return 0.5 * x * (1.0 + jnp.tanh(0.7978845608028654 * (x + 0.044715 * (x * x * x))))


def _ada_kernel(c_ref, w_ref, b_ref, o_ref):
    a = _silu(c_ref[...])
    o_ref[...] = jnp.dot(a, w_ref[...], precision=HIGHEST, preferred_element_type=F32) + b_ref[...]


def _ada(c8, w, b):
    n = w.shape[1]
    tn = 1024
    return pl.pallas_call(
        _ada_kernel,
        out_shape=jax.ShapeDtypeStruct((8, n), F32),
        grid=(n // tn,),
        in_specs=[pl.BlockSpec((8, D_MODEL), lambda j: (0, 0)),
                  pl.BlockSpec((D_MODEL, tn), lambda j: (0, j)),
                  pl.BlockSpec((1, tn), lambda j: (0, j))],
        out_specs=pl.BlockSpec((8, tn), lambda j: (0, j)),
        compiler_params=_cparams(("arbitrary",)),
        name="ada",
    )(c8, w, b)


def _rope(t, cos, sin_signed, first_half):
    rot = jnp.where(first_half, pltpu.roll(t, 96, 1), pltpu.roll(t, 32, 1))
    return t * cos + rot * sin_signed


def _inproj_kernel(x_ref, sh_ref, sc_ref, g_ref, w_ref, cos_ref, sin_ref,
                   hg_ref, qraw_ref, qrot_ref, kc_ref, vc_ref, kvsw_ref, gate_ref, h_scr):
    x = x_ref[0]
    y = x * lax.rsqrt(jnp.mean(x * x, axis=-1, keepdims=True) + EPS) * g_ref[...]
    h_scr[...] = (y * (1.0 + sc_ref[0]) + sh_ref[0]).astype(BF16)

    def mm(lo, width):
        return jnp.dot(h_scr[...], w_ref[:, lo:lo + width], preferred_element_type=F32)

    cos = cos_ref[...]
    sin = sin_ref[...]
    first_half = (lax.broadcasted_iota(jnp.int32, cos.shape, 1) % NSA_HD) < (NSA_HD // 2)
    for j in range(4):
        hg_ref[0, :, j * HG_WIDTH:(j + 1) * HG_WIDTH] = mm(C_HG + j * HG_WIDTH, HG_WIDTH)
    for n in range(NSA_HEADS):
        q = mm(C_Q + n * LANES, LANES) * (NSA_HD ** -0.5)
        qraw_ref[0, :, n * LANES:(n + 1) * LANES] = q.astype(BF16)
        qrot_ref[0, :, n * LANES:(n + 1) * LANES] = _rope(q, cos, sin, first_half).astype(BF16)
    kc_ref[0] = mm(C_KV, LANES).astype(BF16)
    vc_ref[0] = mm(C_KV + LANES, LANES).astype(BF16)
    kvsw_ref[0, :, 0:LANES] = _rope(mm(C_KV + 2 * LANES, LANES), cos, sin, first_half).astype(BF16)
    kvsw_ref[0, :, LANES:2 * LANES] = mm(C_KV + 3 * LANES, LANES).astype(BF16)
    kvsw_ref[0, :, 2 * LANES:3 * LANES] = _rope(mm(C_KV + 4 * LANES, LANES), cos, sin, first_half).astype(BF16)
    kvsw_ref[0, :, 3 * LANES:4 * LANES] = mm(C_KV + 5 * LANES, LANES).astype(BF16)
    gate_ref[0] = mm(C_GATE, LANES)


def _inproj(x, mod3, norm_g, w_p, cos, sin, tm):
    B, S, D = x.shape
    blk = lambda w: pl.BlockSpec((1, tm, w), lambda b, i: (b, i, 0))
    return pl.pallas_call(
        _inproj_kernel,
        out_shape=(jax.ShapeDtypeStruct((B, S, 4 * HG_WIDTH), F32),
                   jax.ShapeDtypeStruct((B, S, NSA_HEADS * LANES), BF16),
                   jax.ShapeDtypeStruct((B, S, NSA_HEADS * LANES), BF16),
                   jax.ShapeDtypeStruct((B, S, LANES), BF16),
                   jax.ShapeDtypeStruct((B, S, LANES), BF16),
                   jax.ShapeDtypeStruct((B, S, 4 * LANES), BF16),
                   jax.ShapeDtypeStruct((B, S, LANES), F32)),
        grid=(B, S // tm),
        in_specs=[blk(D),
                  pl.BlockSpec((1, 1, D), lambda b, i: (b, 0, 0)),
                  pl.BlockSpec((1, 1, D), lambda b, i: (b, 0, 1)),
                  pl.BlockSpec((1, D), lambda b, i: (0, 0)),
                  pl.BlockSpec((D, IN_COLS_P), lambda b, i: (0, 0)),
                  pl.BlockSpec((tm, LANES), lambda b, i: (i, 0)),
                  pl.BlockSpec((tm, LANES), lambda b, i: (i, 0))],
        out_specs=(blk(4 * HG_WIDTH), blk(NSA_HEADS * LANES), blk(NSA_HEADS * LANES),
                   blk(LANES), blk(LANES), blk(4 * LANES), blk(LANES)),
        scratch_shapes=[pltpu.VMEM((tm, D), BF16)],
        compiler_params=_cparams(("arbitrary", "arbitrary")),
        name="inproj",
    )(x, mod3, mod3, norm_g, w_p, cos, sin)


def _hgrn_kernel(q_ref, f_ref, i_ref, gt_ref, lb_ref, ng_ref, o_ref):
    S = q_ref.shape[1]
    C, U = HG_CHUNK, HG_SUB
    lb = lb_ref[...]
    ng = ng_ref[...]
    ri = lax.broadcasted_iota(jnp.int32, (C, C), 0)
    ci = lax.broadcasted_iota(jnp.int32, (C, C), 1)
    tril = (ri >= ci).astype(F32)
    trow = lax.broadcasted_iota(jnp.int32, (U, 1), 0)

    def chunk(c, st_t):
        r0 = pl.multiple_of(c * C, C)
        rows = pl.ds(r0, C)
        f = lb + (1.0 - lb) * _sigmoid(f_ref[0, rows, :])
        kk = 1.0 - f
        b = jnp.dot(tril, jnp.log(f), precision=HIGHEST, preferred_element_type=F32)
        q = q_ref[0, rows, :] * (HG_DK ** -0.5)
        v = i_ref[0, rows, :]
        vb = v.astype(BF16)
        o_inter = lax.dot_general((q * jnp.exp(b)).astype(BF16), st_t.astype(BF16), NT,
                                  preferred_element_type=F32)
        parts = []
        for i in range(C // U):
            lo = i * U
            bi = b[lo:lo + U]
            qi = q[lo:lo + U]
            if i == 0:
                oi = jnp.zeros((U, HG_DV), F32)
            else:
                r = b[lo - 1:lo]
                qrel = (qi * jnp.exp(bi - r)).astype(BF16)
                kprev = (kk[:lo] * jnp.exp(r - b[:lo])).astype(BF16)
                a_off = lax.dot_general(qrel, kprev, NT, preferred_element_type=F32)
                oi = jnp.dot(a_off.astype(BF16), vb[:lo], preferred_element_type=F32)
            for s in range(U):
                valid = trow >= s
                e = jnp.exp(jnp.where(valid, bi - bi[s:s + 1], 0.0))
                a = jnp.sum(qi * e * kk[lo + s:lo + s + 1], axis=-1, keepdims=True)
                oi = oi + jnp.where(valid, a, 0.0) * v[lo + s:lo + s + 1]
            parts.append(oi)
        o = o_inter + jnp.concatenate(parts, axis=0)
        o = o * lax.rsqrt(jnp.mean(o * o, axis=-1, keepdims=True) + EPS) * ng
        o_ref[0, rows, :] = (o * _silu(gt_ref[0, rows, :])).astype(BF16)
        bl = b[C - 1:C]
        kv_t = lax.dot_general(vb, (kk * jnp.exp(bl - b)).astype(BF16), TN, preferred_element_type=F32)
        return jnp.exp(bl) * st_t + kv_t

    lax.fori_loop(0, S // C, chunk, jnp.zeros((HG_DV, HG_DK), F32))


def _hgrn(hg, lb, ng):
    B, S, _ = hg.shape
    col = lambda k: pl.BlockSpec((1, S, HG_DK), lambda b, h, k=k: (b, 0, k * HG_HEADS + h))
    vec = pl.BlockSpec((1, HG_DK), lambda b, h: (0, h))
    return pl.pallas_call(
        _hgrn_kernel,
        out_shape=jax.ShapeDtypeStruct((B, S, HG_WIDTH), BF16),
        grid=(B, HG_HEADS),
        in_specs=[col(0), col(1), col(2), col(3), vec, vec],
        out_specs=pl.BlockSpec((1, S, HG_DV), lambda b, h: (b, 0, h)),
        compiler_params=_cparams(("arbitrary", "arbitrary")),
        name="hgrn",
    )(hg, hg, hg, hg, lb, ng)


def _cmpmlp_kernel(x_ref, w1a_ref, w1b_ref, pe_ref, w1_ref, b1_ref, w2_ref, o_ref):
    x = x_ref[0]
    hb = jnp.dot(pe_ref[...], w1_ref[...], precision=HIGHEST, preferred_element_type=F32)[0:1] + b1_ref[...]
    nrow = x.shape[0]
    out = jnp.zeros((nrow, LANES), F32)
    for g in range(NSA_KV_GROUPS):
        a = jnp.dot(x, w1a_ref[g], preferred_element_type=F32)
        bm = jnp.dot(x, w1b_ref[g], preferred_element_type=F32)
        hdn = a + pltpu.roll(bm, nrow - 1, 0) + hb
        out = out + jnp.dot(_gelu_tanh(hdn).astype(BF16), w2_ref[g], preferred_element_type=F32)
    o_ref[0] = out.astype(BF16)


def _cmpmlp(x2, w1a, w1b, pe8, w1, b1, w2p):
    B, ncb, width = x2.shape
    full = lambda a: pl.BlockSpec(a.shape, lambda b: (0,) * a.ndim)
    return pl.pallas_call(
        _cmpmlp_kernel,
        out_shape=jax.ShapeDtypeStruct((B, ncb, LANES), BF16),
        grid=(B,),
        in_specs=[pl.BlockSpec((1, ncb, width), lambda b: (b, 0, 0)),
                  full(w1a), full(w1b), full(pe8), full(w1), full(b1), full(w2p)],
        out_specs=pl.BlockSpec((1, ncb, LANES), lambda b: (b, 0, 0)),
        compiler_params=_cparams(("arbitrary",)),
        name="cmpmlp",
    )(x2, w1a, w1b, pe8, w1, b1, w2p)


def _rank_before(score, nrows):
    jrow = lax.broadcasted_iota(jnp.int32, score.shape, 0)
    rank = jnp.zeros(score.shape, F32)
    for k in range(nrows):
        rk = score[k:k + 1]
        beats = (rk > score) | ((rk == score) & (jrow > k))
        rank = rank + jnp.where(beats, 1.0, 0.0)
    return rank


def _transpose_via_mxu(xt):
    c = xt.shape[1]
    eye = (lax.broadcasted_iota(jnp.int32, (c, c), 0) == lax.broadcasted_iota(jnp.int32, (c, c), 1)).astype(F32)
    return lax.dot_general(eye, xt, NT, precision=HIGHEST, preferred_element_type=F32)


def _cmpsel_kernel(q_ref, kc_ref, vc_ref, o_ref, sel_ref):
    tq = q_ref.shape[1]
    ncb = kc_ref.shape[1]
    nsb = sel_ref.shape[3]
    t = pl.program_id(2)
    kc = kc_ref[0]
    vc = vc_ref[0]
    pos = t * tq + lax.broadcasted_iota(jnp.int32, (tq, 1), 0)
    cblk = lax.broadcasted_iota(jnp.int32, (1, ncb), 1)
    vis = (cblk * CMP_STRIDE + CMP_BLOCK - 1) <= pos
    psum = jnp.zeros((tq, ncb), F32)
    for h in range(NSA_HG):
        s = lax.dot_general(q_ref[0, :, h * LANES:(h + 1) * LANES], kc, NT, preferred_element_type=F32)
        s = jnp.where(vis, s, NEG)
        p = jnp.exp(s - jnp.max(s, axis=-1, keepdims=True))
        p = jnp.where(vis, p / jnp.sum(p, axis=-1, keepdims=True), 0.0)
        o_ref[0, :, h * LANES:(h + 1) * LANES] = jnp.dot(p.astype(BF16), vc, preferred_element_type=F32)
        psum = psum + p
    jb = lax.broadcasted_iota(jnp.int32, (nsb, ncb), 0) * SEL_BLOCK
    cb = lax.broadcasted_iota(jnp.int32, (nsb, ncb), 1) * CMP_STRIDE
    ov = jnp.maximum(jnp.minimum(cb + CMP_BLOCK, jb + SEL_BLOCK) - jnp.maximum(cb, jb), 0).astype(F32) / CMP_BLOCK
    pslc_t = lax.dot_general(ov, psum, NT, precision=HIGHEST, preferred_element_type=F32)
    posl = t * tq + lax.broadcasted_iota(jnp.int32, (1, tq), 1)
    cur = posl // SEL_BLOCK
    jrow = lax.broadcasted_iota(jnp.int32, (nsb, tq), 0)
    forced = (jrow == 0) | (jrow == cur) | (jrow == cur - 1)
    score = jnp.where(forced, 1e30, jnp.where(jrow <= cur, pslc_t, NEG))
    rank = _rank_before(score, nsb)
    sel_t = jnp.where((rank < min(N_SEL, nsb)) & (score > -1e29), 1.0, 0.0)
    sel_ref[0, 0] = jnp.where(_transpose_via_mxu(sel_t) > 0.5, 0.0, NEG).astype(BF16)


def _cmpsel(qraw, kcmp, vcmp, tq):
    B, S, _ = qraw.shape
    ncb = kcmp.shape[1]
    nsb = S // SEL_BLOCK
    gw = NSA_HG * LANES
    return pl.pallas_call(
        _cmpsel_kernel,
        out_shape=(jax.ShapeDtypeStruct((B, S, NSA_HEADS * LANES), F32),
                   jax.ShapeDtypeStruct((B, NSA_KV_GROUPS, S, nsb), BF16)),
        grid=(B, NSA_KV_GROUPS, S // tq),
        in_specs=[pl.BlockSpec((1, tq, gw), lambda b, g, t: (b, t, g)),
                  pl.BlockSpec((1, ncb, LANES), lambda b, g, t: (b, 0, 0)),
                  pl.BlockSpec((1, ncb, LANES), lambda b, g, t: (b, 0, 0))],
        out_specs=(pl.BlockSpec((1, tq, gw), lambda b, g, t: (b, t, g)),
                   pl.BlockSpec((1, 1, tq, nsb), lambda b, g, t: (b, g, t, 0))),
        compiler_params=_cparams(("arbitrary", "arbitrary", "arbitrary")),
        name="cmpsel",
    )(qraw, kcmp, vcmp)


def _slcwin_kernel(q_ref, kv_ref, sel_ref, oslc_ref, owin_ref, *, tk, wk):
    tq = q_ref.shape[1]
    S = kv_ref.shape[1]
    nsb = sel_ref.shape[3]
    t = pl.program_id(2)
    q0 = t * tq
    qpos = q0 + lax.broadcasted_iota(jnp.int32, (tq, 1), 0)
    selneg = sel_ref[0, 0]
    blk_row = lax.broadcasted_iota(jnp.int32, (nsb, tk), 0)
    key_col = lax.broadcasted_iota(jnp.int32, (nsb, tk), 1)
    qs = [q_ref[0, :, h * LANES:(h + 1) * LANES] for h in range(NSA_HG)]

    def kstep(j, carry, diag):
        k0 = pl.multiple_of(j * tk, tk)
        ks = kv_ref[0, pl.ds(k0, tk), 0:LANES]
        vs = kv_ref[0, pl.ds(k0, tk), LANES:2 * LANES]
        expand = (blk_row == (k0 + key_col) // SEL_BLOCK).astype(BF16)
        bias = jnp.dot(selneg, expand, preferred_element_type=F32)
        if diag:
            kpos = k0 + lax.broadcasted_iota(jnp.int32, (1, tk), 1)
            bias = jnp.where(kpos <= qpos, bias, NEG)
        out = []
        for h in range(NSA_HG):
            m, l, acc = carry[h]
            s = lax.dot_general(qs[h], ks, NT, preferred_element_type=F32) + bias
            mn = jnp.maximum(m, jnp.max(s, axis=-1, keepdims=True))
            p = jnp.exp(s - mn)
            alpha = jnp.exp(m - mn)
            l = alpha * l + jnp.sum(p, axis=-1, keepdims=True)
            acc = alpha * acc + jnp.dot(p.astype(BF16), vs, preferred_element_type=F32)
            out.append((mn, l, acc))
        return tuple(out)

    init = tuple((jnp.full((tq, 1), NEG, F32), jnp.zeros((tq, 1), F32), jnp.zeros((tq, LANES), F32))
                 for _ in range(NSA_HG))
    jdiag = q0 // tk
    carry = lax.fori_loop(0, jdiag, lambda j, c: kstep(j, c, False), init)
    carry = kstep(jdiag, carry, True)
    for h in range(NSA_HG):
        _, l, acc = carry[h]
        oslc_ref[0, :, h * LANES:(h + 1) * LANES] = acc / l

    nblk = S // tq
    start = pl.multiple_of(jnp.clip(t - WINDOW // tq, 0, nblk - wk // tq) * tq, tq)
    kw = kv_ref[0, pl.ds(start, wk), 2 * LANES:3 * LANES]
    vw = kv_ref[0, pl.ds(start, wk), 3 * LANES:4 * LANES]
    d = qpos - (start + lax.broadcasted_iota(jnp.int32, (1, wk), 1))
    okw = (d >= 0) & (d < WINDOW)
    for h in range(NSA_HG):
        s = jnp.where(okw, lax.dot_general(qs[h], kw, NT, preferred_element_type=F32), NEG)
        p = jnp.exp(s - jnp.max(s, axis=-1, keepdims=True))
        o = jnp.dot(p.astype(BF16), vw, preferred_element_type=F32)
        owin_ref[0, :, h * LANES:(h + 1) * LANES] = o / jnp.sum(p, axis=-1, keepdims=True)


def _slcwin(qrot, kvsw, sel, tq, tk):
    B, S, _ = qrot.shape
    nsb = S // SEL_BLOCK
    gw = NSA_HG * LANES
    wk = min(WINDOW + tq, S)
    oshape = jax.ShapeDtypeStruct((B, S, NSA_HEADS * LANES), F32)
    ospec = pl.BlockSpec((1, tq, gw), lambda b, g, t: (b, t, g))
    return pl.pallas_call(
        functools.partial(_slcwin_kernel, tk=tk, wk=wk),
        out_shape=(oshape, oshape),
        grid=(B, NSA_KV_GROUPS, S // tq),
        in_specs=[pl.BlockSpec((1, tq, gw), lambda b, g, t: (b, t, g)),
                  pl.BlockSpec((1, S, 4 * LANES), lambda b, g, t: (b, 0, 0)),
                  pl.BlockSpec((1, 1, tq, nsb), lambda b, g, t: (b, g, t, 0))],
        out_specs=(ospec, ospec),
        compiler_params=_cparams(("arbitrary", "arbitrary", "arbitrary")),
        name="slcwin",
    )(qrot, kvsw, sel)


def _outproj_kernel(x_ref, ohg_ref, ocmp_ref, oslc_ref, owin_ref, gate_ref, whg_ref, wnsa_ref,
                    g1_ref, sh2_ref, sc2_ref, n2_ref, rwt_ref, rb_ref,
                    x1_ref, h2_ref, wc_ref):
    tm = x_ref.shape[1]
    gs = _sigmoid(gate_ref[0])
    mix = jnp.dot(ohg_ref[0], whg_ref[...], preferred_element_type=F32)
    for n in range(NSA_HEADS):
        cols = slice(n * LANES, (n + 1) * LANES)
        o = (gs[:, 3 * n:3 * n + 1] * ocmp_ref[0, :, cols] + gs[:, 3 * n + 1:3 * n + 2] * oslc_ref[0, :, cols]
             + gs[:, 3 * n + 2:3 * n + 3] * owin_ref[0, :, cols])
        mix = mix + jnp.dot(o.astype(BF16), wnsa_ref[cols, :], preferred_element_type=F32)
    x1 = x_ref[0] + g1_ref[0] * mix
    x1_ref[0] = x1
    h2 = (x1 * lax.rsqrt(jnp.mean(x1 * x1, axis=-1, keepdims=True) + EPS) * n2_ref[...]) * (1.0 + sc2_ref[0]) + sh2_ref[0]
    h2_ref[0] = h2.astype(BF16)
    scores = _sigmoid(lax.dot_general(rwt_ref[...], h2, NT, precision=HIGHEST, preferred_element_type=F32))
    choice = scores + rb_ref[...]
    per = N_EXPERTS // N_GROUPS
    c3 = choice.reshape(N_GROUPS, per, tm)
    erow = lax.broadcasted_iota(jnp.int32, c3.shape, 1)
    rank_in = jnp.zeros(c3.shape, F32)
    for k in range(per):
        ck = c3[:, k:k + 1, :]
        rank_in = rank_in + jnp.where((ck > c3) | ((ck == c3) & (erow > k)), 1.0, 0.0)
    grp_score = jnp.sum(jnp.where(rank_in < 2, c3, 0.0), axis=1)
    grp_keep = _rank_before(grp_score, N_GROUPS) < TOPK_GROUPS
    masked = jnp.where(grp_keep[:, None, :], c3, -jnp.inf).reshape(N_EXPERTS, tm)
    keep = _rank_before(masked, N_EXPERTS) < TOP_K
    tw = jnp.where(keep, scores, 0.0)
    tw = tw / jnp.sum(tw, axis=0, keepdims=True) * ROUTED_SCALE
    wc_ref[0] = _transpose_via_mxu(tw)


def _outproj(x, ohg, ocmp, oslc, owin, gate, whg, wnsa, mod3, n2, rwt, rb, tm):
    B, S, D = x.shape
    blk = lambda w: pl.BlockSpec((1, tm, w), lambda b, i: (b, i, 0))
    full = lambda a: pl.BlockSpec(a.shape, lambda b, i: (0,) * a.ndim)
    modc = lambda k: pl.BlockSpec((1, 1, D), lambda b, i, k=k: (b, 0, k))
    return pl.pallas_call(
        _outproj_kernel,
        out_shape=(jax.ShapeDtypeStruct((B, S, D), F32),
                   jax.ShapeDtypeStruct((B, S, D), BF16),
                   jax.ShapeDtypeStruct((B, S, N_EXPERTS), F32)),
        grid=(B, S // tm),
        in_specs=[blk(D), blk(HG_WIDTH), blk(NSA_HEADS * LANES), blk(NSA_HEADS * LANES), blk(NSA_HEADS * LANES),
                  blk(LANES), full(whg), full(wnsa), modc(2), modc(3), modc(4), full(n2), full(rwt), full(rb)],
        out_specs=(blk(D), blk(D), blk(N_EXPERTS)),
        compiler_params=_cparams(("arbitrary", "arbitrary")),
        name="outproj",
    )(x, ohg, ocmp, oslc, owin, gate, whg, wnsa, mod3, mod3, mod3, n2, rwt, rb)


def _moe_kernel(h_ref, wc_ref, wgu_ref, wdn_ref, sgu_ref, sdn_ref, x1_ref, g2_ref, fg_ref, o_ref, acc_ref, *, eg):
    e = pl.program_id(1)
    h = h_ref[...]
    hid = EXPERT_HIDDEN

    @pl.when(e == 0)
    def _():
        gu = jnp.dot(h, sgu_ref[...], preferred_element_type=F32)
        act = (_silu(gu[:, :hid]) * gu[:, hid:]).astype(BF16)
        acc_ref[...] = jnp.dot(act, sdn_ref[...], preferred_element_type=F32)

    wc = wc_ref[0]
    acts = []
    for k in range(eg):
        gu = jnp.dot(h, wgu_ref[k], preferred_element_type=F32)
        acts.append((_silu(gu[:, :hid]) * gu[:, hid:] * wc[:, k:k + 1]).astype(BF16))
    act = jnp.concatenate(acts, axis=1)
    acc_ref[...] += jnp.dot(act, wdn_ref[...].reshape(eg * hid, D_MODEL), preferred_element_type=F32)

    @pl.when(e == pl.num_programs(1) - 1)
    def _():
        x2 = x1_ref[...] + g2_ref[0] * acc_ref[...]
        o_ref[...] = x2 * lax.rsqrt(jnp.mean(x2 * x2, axis=-1, keepdims=True) + EPS) * fg_ref[...]


def _moe(h2, wc3, wgu, wdn, sgu, sdn, x1, mod3, fg, tm, eg, tiles_per_batch):
    T, D = h2.shape
    full = lambda a: pl.BlockSpec(a.shape, lambda i, e: (0,) * a.ndim)
    return pl.pallas_call(
        functools.partial(_moe_kernel, eg=eg),
        out_shape=jax.ShapeDtypeStruct((T, D), F32),
        grid=(T // tm, N_EXPERTS // eg),
        in_specs=[pl.BlockSpec((tm, D), lambda i, e: (i, 0)),
                  pl.BlockSpec((1, tm, eg), lambda i, e: (e, i, 0)),
                  pl.BlockSpec((eg, D, 2 * EXPERT_HIDDEN), lambda i, e: (e, 0, 0)),
                  pl.BlockSpec((eg, EXPERT_HIDDEN, D), lambda i, e: (e, 0, 0)),
                  full(sgu), full(sdn),
                  pl.BlockSpec((tm, D), lambda i, e: (i, 0)),
                  pl.BlockSpec((1, 1, D), lambda i, e: (i // tiles_per_batch, 0, 5)),
                  full(fg)],
        out_specs=pl.BlockSpec((tm, D), lambda i, e: (i, 0)),
        scratch_shapes=[pltpu.VMEM((tm, D), F32)],
        compiler_params=_cparams(("arbitrary", "arbitrary")),
        name="moe",
    )(h2, wc3, wgu, wdn, sgu, sdn, x1, mod3, fg)


def _pack_w_in(w_in):
    d = w_in.shape[0]
    q0 = 4 * HG_WIDTH
    kv0 = q0 + NSA_WIDTH
    g0 = kv0 + 6 * NSA_KV
    qcols = []
    for n in range(NSA_HEADS):
        g = n // NSA_HG
        head = w_in[:, q0 + n * NSA_HD:q0 + (n + 1) * NSA_HD]
        z = jnp.zeros((d, NSA_HD), w_in.dtype)
        qcols += [head, z] if g == 0 else [z, head]
    gate = jnp.pad(w_in[:, g0:], ((0, 0), (0, LANES - 3 * NSA_HEADS)))
    return jnp.concatenate([w_in[:, :q0]] + qcols + [w_in[:, kv0:g0], gate], axis=1).astype(BF16)


def _pack_w_out(w_out):
    rows = []
    for n in range(NSA_HEADS):
        g = n // NSA_HG
        head = w_out[HG_WIDTH + n * NSA_HD:HG_WIDTH + (n + 1) * NSA_HD]
        z = jnp.zeros((NSA_HD, w_out.shape[1]), w_out.dtype)
        rows += [head, z] if g == 0 else [z, head]
    return w_out[:HG_WIDTH].astype(BF16), jnp.concatenate(rows, axis=0).astype(BF16)


def _pack_cmp(pos, w1, b1, w2):
    half = CMP_STRIDE * NSA_HD
    def rows_for(wh):
        w3 = wh.reshape(CMP_STRIDE, NSA_HD, CMP_HIDDEN)
        z = jnp.zeros_like(w3)
        return jnp.stack([jnp.concatenate([w3, z], axis=1).reshape(CMP_STRIDE * LANES, CMP_HIDDEN),
                          jnp.concatenate([z, w3], axis=1).reshape(CMP_STRIDE * LANES, CMP_HIDDEN)])
    w1a = rows_for(w1[:half]).astype(BF16)
    w1b = rows_for(w1[half:]).astype(BF16)
    z2 = jnp.zeros_like(w2)
    w2p = jnp.stack([jnp.concatenate([w2, z2], axis=1), jnp.concatenate([z2, w2], axis=1)]).astype(BF16)
    pe8 = jnp.pad(pos.reshape(1, CMP_BLOCK * NSA_HD), ((0, 7), (0, 0)))
    return w1a, w1b, pe8, w1, b1.reshape(1, CMP_HIDDEN), w2p


def _rope_tables(S):
    half = NSA_HD // 2
    inv = ROPE_THETA ** (-jnp.arange(half, dtype=F32) / half)
    ang = jnp.arange(S, dtype=F32)[:, None] * inv[None, :]
    cos, sin = jnp.cos(ang), jnp.sin(ang)
    reps = LANES // NSA_HD
    return jnp.tile(jnp.concatenate([cos, cos], axis=1), (1, reps)), jnp.tile(jnp.concatenate([-sin, sin], axis=1), (1, reps))


def _tiles(S):
    return dict(inproj=min(512, S), cmpsel=min(256, S), slc_q=128, slc_k=256, outproj=min(256, S))


def kernel(x, c, w_ada, b_ada, norm1_g, w_in, hg_lb_logits, hg_norm_g, cmp_pos_k, cmp_w1_k, cmp_b1_k, cmp_w2_k,
           cmp_pos_v, cmp_w1_v, cmp_b1_v, cmp_w2_v, w_out, norm2_g, router_w, router_bias, w_exp_gu, w_exp_dn,
           w_sh_gu, w_sh_dn, final_g):
    B, S, D = x.shape
    assert D == D_MODEL and w_ada.shape[0] == 1 and S % 512 == 0
    tl = _tiles(S)
    l = 0
    lb = jnp.cumsum(jax.nn.softmax(hg_lb_logits.astype(F32), axis=0), axis=0)[l].reshape(1, HG_WIDTH)
    c8 = jnp.pad(c, ((0, 8 - B), (0, 0)))
    mod3 = _ada(c8, w_ada[l], b_ada[l].reshape(1, -1))[:B].reshape(B, 1, 6 * D)
    cos, sin = _rope_tables(S)
    hg, qraw, qrot, kc, vc, kvsw, gate = _inproj(x, mod3, norm1_g[l].reshape(1, D), _pack_w_in(w_in[l]), cos, sin,
                                                 tl["inproj"])
    ohg = _hgrn(hg, lb, hg_norm_g[l].reshape(1, HG_WIDTH))
    ncb = S // CMP_STRIDE
    kcmp = _cmpmlp(kc.reshape(B, ncb, CMP_STRIDE * LANES), *_pack_cmp(cmp_pos_k[l], cmp_w1_k[l], cmp_b1_k[l], cmp_w2_k[l]))
    vcmp = _cmpmlp(vc.reshape(B, ncb, CMP_STRIDE * LANES), *_pack_cmp(cmp_pos_v[l], cmp_w1_v[l], cmp_b1_v[l], cmp_w2_v[l]))
    ocmp, sel = _cmpsel(qraw, kcmp, vcmp, tl["cmpsel"])
    oslc, owin = _slcwin(qrot, kvsw, sel, tl["slc_q"], tl["slc_k"])
    whg, wnsa = _pack_w_out(w_out[l])
    x1, h2, wc = _outproj(x, ohg, ocmp, oslc, owin, gate, whg, wnsa, mod3, norm2_g[l].reshape(1, D),
                          router_w[l].T, router_bias[l].reshape(N_EXPERTS, 1), tl["outproj"])
    T = B * S
    eg = 4
    tm = min(1024, S)
    wc3 = wc.reshape(T, N_EXPERTS // eg, eg).transpose(1, 0, 2)
    out = _moe(h2.reshape(T, D), wc3, w_exp_gu[l].astype(BF16), w_exp_dn[l].astype(BF16),
               w_sh_gu[l].astype(BF16), w_sh_dn[l].astype(BF16), x1.reshape(T, D), mod3,
               final_g.reshape(1, D), tm, eg, S // tm)
    return out.reshape(B, S, D)
```

```python
import functools

import numpy as np
import jax
import jax.numpy as jnp
from jax import lax
from jax.experimental import pallas as pl
from jax.experimental.pallas import tpu as pltpu

F32 = jnp.float32
BF16 = jnp.bfloat16
HIGHEST = lax.Precision.HIGHEST

D_MODEL = 1024
EPS = 1e-6
HG_HEADS = 4
HG_DK = 128
HG_DV = 128
HG_WIDTH = HG_HEADS * HG_DV
HG_CHUNK = 64
HG_SUB = 16
NSA_HEADS = 8
NSA_KV_GROUPS = 2
NSA_HG = NSA_HEADS // NSA_KV_GROUPS
NSA_HD = 64
NSA_WIDTH = NSA_HEADS * NSA_HD
NSA_KV = NSA_KV_GROUPS * NSA_HD
CMP_BLOCK = 32
CMP_STRIDE = 16
CMP_HIDDEN = 256
SEL_BLOCK = 64
N_SEL = 16
WINDOW = 512
ROPE_THETA = 10000.0
N_EXPERTS = 64
TOP_K = 8
N_GROUPS = 8
TOPK_GROUPS = 4
EXPERT_HIDDEN = 256
ROUTED_SCALE = 2.5

LANES = 128
NEG = -1e30
VMEM_LIMIT = 56 * 1024 * 1024

C_HG = 0
C_Q = 4 * HG_WIDTH
C_K = C_Q + NSA_HEADS * LANES
C_V = C_K + 4 * LANES
C_GATE = C_V + 4 * LANES
IN_COLS_P = C_GATE + LANES
LOG2E = 1.4426950408889634

NT = (((1,), (1,)), ((), ()))
TN = (((0,), (0,)), ((), ()))


def _cparams(sem):
    return pltpu.CompilerParams(dimension_semantics=sem, vmem_limit_bytes=VMEM_LIMIT)


def _sigmoid(x):
    return 1.0 / (1.0 + jnp.exp(-x))


def _silu(x):
    return x * _sigmoid(x)


def _gelu_tanh(x):
    return 0.5 * x * (1.0 + jnp.tanh(0.7978845608028654 * (x + 0.044715 * (x * x * x))))


def _ada_kernel(c_ref, w_ref, b_ref, o_ref):
    a = _silu(c_ref[...])
    o_ref[...] = jnp.dot(a, w_ref[...], precision=HIGHEST, preferred_element_type=F32) + b_ref[...]


def _ada(c8, w, b):
    n = w.shape[1]
    tn = 1024
    return pl.pallas_call(
        _ada_kernel,
        out_shape=jax.ShapeDtypeStruct((8, n), F32),
        grid=(n // tn,),
        in_specs=[pl.BlockSpec((8, D_MODEL), lambda j: (0, 0)),
                  pl.BlockSpec((D_MODEL, tn), lambda j: (0, j)),
                  pl.BlockSpec((1, tn), lambda j: (0, j))],
        out_specs=pl.BlockSpec((8, tn), lambda j: (0, j)),
        compiler_params=_cparams(("arbitrary",)),
        name="ada",
    )(c8, w, b)


def _rope(t, cos, sin_signed, first_half):
    rot = jnp.where(first_half, pltpu.roll(t, 96, 1), pltpu.roll(t, 32, 1))
    return t * cos + rot * sin_signed


def _inproj_kernel(x_ref, sh_ref, sc_ref, g_ref, w_ref, cos_ref, sin_ref,
                   hg_ref, qraw_ref, qrot_ref, kc_ref, vc_ref, ks_ref, kw_ref, va_ref, gate_ref, h_scr):
    tm = x_ref.shape[1]
    x = x_ref[0]
    y = x * lax.rsqrt(jnp.mean(x * x, axis=-1, keepdims=True) + EPS) * g_ref[...]
    h_scr[...] = (y * (1.0 + sc_ref[0]) + sh_ref[0]).astype(BF16)

    def mm(lo, width):
        return jnp.dot(h_scr[...], w_ref[:, lo:lo + width], preferred_element_type=F32)

    cos = cos_ref[...]
    sin = sin_ref[...]
    first_half = (lax.broadcasted_iota(jnp.int32, cos.shape, 1) % NSA_HD) < (NSA_HD // 2)
    for j in range(4):
        hg_ref[0, :, j * HG_WIDTH:(j + 1) * HG_WIDTH] = mm(C_HG + j * HG_WIDTH, HG_WIDTH)
    for n in range(NSA_HEADS):
        q = mm(C_Q + n * LANES, LANES) * (NSA_HD ** -0.5)
        qraw_ref[0, :, n * LANES:(n + 1) * LANES] = q.astype(BF16)
        qrot_ref[0, :, n * LANES:(n + 1) * LANES] = (_rope(q, cos, sin, first_half) * LOG2E).astype(BF16)
    kc_ref[0] = mm(C_K, LANES).astype(BF16)
    vc_ref[0] = mm(C_K + LANES, LANES).astype(BF16)
    ks_ref[0, :, 0:LANES] = _rope(mm(C_K + 2 * LANES, LANES), cos, sin, first_half).astype(BF16)
    lane = lax.broadcasted_iota(jnp.int32, (tm, LANES), 1)
    pos = pl.program_id(1) * tm + lax.broadcasted_iota(jnp.int32, (tm, LANES), 0)
    ks_ref[0, :, LANES:2 * LANES] = jnp.where(lane == pos // SEL_BLOCK, 1.0, 0.0).astype(BF16)
    kw_ref[0] = _rope(mm(C_K + 3 * LANES, LANES), cos, sin, first_half).astype(BF16)
    for j in range(4):
        v = mm(C_V + j * LANES, LANES)
        va_ref[0, :, j * LANES:(j + 1) * LANES] = jnp.where(lane == NSA_HD, 1.0, v).astype(BF16)
    gate_ref[0] = mm(C_GATE, LANES)


def _inproj(x, mod3, norm_g, w_p, cos, sin, tm):
    B, S, D = x.shape
    blk = lambda w: pl.BlockSpec((1, tm, w), lambda b, i: (b, i, 0))
    return pl.pallas_call(
        _inproj_kernel,
        out_shape=(jax.ShapeDtypeStruct((B, S, 4 * HG_WIDTH), F32),
                   jax.ShapeDtypeStruct((B, S, NSA_HEADS * LANES), BF16),
                   jax.ShapeDtypeStruct((B, S, NSA_HEADS * LANES), BF16),
                   jax.ShapeDtypeStruct((B, S, LANES), BF16),
                   jax.ShapeDtypeStruct((B, S, LANES), BF16),
                   jax.ShapeDtypeStruct((B, S, 2 * LANES), BF16),
                   jax.ShapeDtypeStruct((B, S, LANES), BF16),
                   jax.ShapeDtypeStruct((B, S, 4 * LANES), BF16),
                   jax.ShapeDtypeStruct((B, S, LANES), F32)),
        grid=(B, S // tm),
        in_specs=[blk(D),
                  pl.BlockSpec((1, 1, D), lambda b, i: (b, 0, 0)),
                  pl.BlockSpec((1, 1, D), lambda b, i: (b, 0, 1)),
                  pl.BlockSpec((1, D), lambda b, i: (0, 0)),
                  pl.BlockSpec((D, IN_COLS_P), lambda b, i: (0, 0)),
                  pl.BlockSpec((tm, LANES), lambda b, i: (i, 0)),
                  pl.BlockSpec((tm, LANES), lambda b, i: (i, 0))],
        out_specs=(blk(4 * HG_WIDTH), blk(NSA_HEADS * LANES), blk(NSA_HEADS * LANES),
                   blk(LANES), blk(LANES), blk(2 * LANES), blk(LANES), blk(4 * LANES), blk(LANES)),
        scratch_shapes=[pltpu.VMEM((tm, D), BF16)],
        compiler_params=_cparams(("arbitrary", "arbitrary")),
        name="inproj",
    )(x, mod3, mod3, norm_g, w_p, cos, sin)


def _hgrn_kernel(q_ref, f_ref, i_ref, gt_ref, lb_ref, ng_ref, o_ref):
    S = q_ref.shape[1]
    C, U = HG_CHUNK, HG_SUB
    lb = lb_ref[...]
    ng = ng_ref[...]
    ri = lax.broadcasted_iota(jnp.int32, (C, C), 0)
    ci = lax.broadcasted_iota(jnp.int32, (C, C), 1)
    tril = (ri >= ci).astype(F32)
    trow = lax.broadcasted_iota(jnp.int32, (U, 1), 0)

    def chunk(c, st_t):
        r0 = pl.multiple_of(c * C, C)
        rows = pl.ds(r0, C)
        f = lb + (1.0 - lb) * _sigmoid(f_ref[0, rows, :])
        kk = 1.0 - f
        b = jnp.dot(tril, jnp.log(f), precision=HIGHEST, preferred_element_type=F32)
        q = q_ref[0, rows, :] * (HG_DK ** -0.5)
        v = i_ref[0, rows, :]
        vb = v.astype(BF16)
        o_inter = lax.dot_general((q * jnp.exp(b)).astype(BF16), st_t.astype(BF16), NT,
                                  preferred_element_type=F32)
        parts = []
        for i in range(C // U):
            lo = i * U
            bi = b[lo:lo + U]
            qi = q[lo:lo + U]
            if i == 0:
                oi = jnp.zeros((U, HG_DV), F32)
            else:
                r = b[lo - 1:lo]
                qrel = (qi * jnp.exp(bi - r)).astype(BF16)
                kprev = (kk[:lo] * jnp.exp(r - b[:lo])).astype(BF16)
                a_off = lax.dot_general(qrel, kprev, NT, preferred_element_type=F32)
                oi = jnp.dot(a_off.astype(BF16), vb[:lo], preferred_element_type=F32)
            for s in range(U):
                valid = trow >= s
                e = jnp.exp(jnp.where(valid, bi - bi[s:s + 1], 0.0))
                a = jnp.sum(qi * e * kk[lo + s:lo + s + 1], axis=-1, keepdims=True)
                oi = oi + jnp.where(valid, a, 0.0) * v[lo + s:lo + s + 1]
            parts.append(oi)
        o = o_inter + jnp.concatenate(parts, axis=0)
        o = o * lax.rsqrt(jnp.mean(o * o, axis=-1, keepdims=True) + EPS) * ng
        o_ref[0, rows, :] = (o * _silu(gt_ref[0, rows, :])).astype(BF16)
        bl = b[C - 1:C]
        kv_t = lax.dot_general(vb, (kk * jnp.exp(bl - b)).astype(BF16), TN, preferred_element_type=F32)
        return jnp.exp(bl) * st_t + kv_t

    lax.fori_loop(0, S // C, chunk, jnp.zeros((HG_DV, HG_DK), F32))


def _hgrn(hg, lb, ng):
    B, S, _ = hg.shape
    col = lambda k: pl.BlockSpec((1, S, HG_DK), lambda b, h, k=k: (b, 0, k * HG_HEADS + h))
    vec = pl.BlockSpec((1, HG_DK), lambda b, h: (0, h))
    return pl.pallas_call(
        _hgrn_kernel,
        out_shape=jax.ShapeDtypeStruct((B, S, HG_WIDTH), BF16),
        grid=(B, HG_HEADS),
        in_specs=[col(0), col(1), col(2), col(3), vec, vec],
        out_specs=pl.BlockSpec((1, S, HG_DV), lambda b, h: (b, 0, h)),
        compiler_params=_cparams(("arbitrary", "arbitrary")),
        name="hgrn",
    )(hg, hg, hg, hg, lb, ng)


def _cmpmlp_kernel(x_ref, w1a_ref, w1b_ref, pe_ref, w1_ref, b1_ref, w2_ref, o_ref):
    x = x_ref[0]
    hb = jnp.dot(pe_ref[...], w1_ref[...], precision=HIGHEST, preferred_element_type=F32)[0:1] + b1_ref[...]
    nrow = x.shape[0]
    for g in range(NSA_KV_GROUPS):
        a = jnp.dot(x, w1a_ref[g], preferred_element_type=F32)
        bm = jnp.dot(x, w1b_ref[g], preferred_element_type=F32)
        hdn = a + pltpu.roll(bm, nrow - 1, 0) + hb
        o_ref[0, g] = jnp.dot(_gelu_tanh(hdn).astype(BF16), w2_ref[g], preferred_element_type=F32).astype(BF16)


def _cmpmlp(x2, w1a, w1b, pe8, w1, b1, w2p):
    B, ncb, width = x2.shape
    full = lambda a: pl.BlockSpec(a.shape, lambda b: (0,) * a.ndim)
    return pl.pallas_call(
        _cmpmlp_kernel,
        out_shape=jax.ShapeDtypeStruct((B, NSA_KV_GROUPS, ncb, LANES), BF16),
        grid=(B,),
        in_specs=[pl.BlockSpec((1, ncb, width), lambda b: (b, 0, 0)),
                  full(w1a), full(w1b), full(pe8), full(w1), full(b1), full(w2p)],
        out_specs=pl.BlockSpec((1, NSA_KV_GROUPS, ncb, LANES), lambda b: (b, 0, 0, 0)),
        compiler_params=_cparams(("arbitrary",)),
        name="cmpmlp",
    )(x2, w1a, w1b, pe8, w1, b1, w2p)


def _rank_before(score, nrows):
    jrow = lax.broadcasted_iota(jnp.int32, score.shape, 0)
    rank = jnp.zeros(score.shape, F32)
    for k in range(nrows):
        rk = score[k:k + 1]
        beats = (rk > score) | ((rk == score) & (jrow > k))
        rank = rank + jnp.where(beats, 1.0, 0.0)
    return rank


def _transpose_via_mxu(xt):
    c = xt.shape[1]
    eye = (lax.broadcasted_iota(jnp.int32, (c, c), 0) == lax.broadcasted_iota(jnp.int32, (c, c), 1)).astype(F32)
    return lax.dot_general(eye, xt, NT, precision=HIGHEST, preferred_element_type=F32)


def _cmpsel_kernel(q_ref, kc_ref, vc_ref, o_ref, sel_ref):
    tq = q_ref.shape[1]
    ncb = kc_ref.shape[2]
    t = pl.program_id(2)
    kc = kc_ref[0, 0]
    vc = vc_ref[0, 0]
    pos = t * tq + lax.broadcasted_iota(jnp.int32, (tq, 1), 0)
    cblk = lax.broadcasted_iota(jnp.int32, (1, ncb), 1)
    vis = (cblk * CMP_STRIDE + CMP_BLOCK - 1) <= pos
    psum = jnp.zeros((tq, ncb), F32)
    for h in range(NSA_HG):
        s = lax.dot_general(q_ref[0, :, h * LANES:(h + 1) * LANES], kc, NT, preferred_element_type=F32)
        s = jnp.where(vis, s, NEG)
        p = jnp.exp(s - jnp.max(s, axis=-1, keepdims=True))
        p = jnp.where(vis, p / jnp.sum(p, axis=-1, keepdims=True), 0.0)
        o_ref[0, :, h * LANES:(h + 1) * LANES] = jnp.dot(p.astype(BF16), vc, preferred_element_type=F32)
        psum = psum + p
    nsb = ncb * CMP_STRIDE // SEL_BLOCK
    jb = lax.broadcasted_iota(jnp.int32, (nsb, ncb), 0) * SEL_BLOCK
    cb = lax.broadcasted_iota(jnp.int32, (nsb, ncb), 1) * CMP_STRIDE
    ov = jnp.maximum(jnp.minimum(cb + CMP_BLOCK, jb + SEL_BLOCK) - jnp.maximum(cb, jb), 0).astype(F32) / CMP_BLOCK
    pslc_t = lax.dot_general(ov, psum, NT, precision=HIGHEST, preferred_element_type=F32)
    posl = t * tq + lax.broadcasted_iota(jnp.int32, (1, tq), 1)
    cur = posl // SEL_BLOCK
    jrow = lax.broadcasted_iota(jnp.int32, (nsb, tq), 0)
    forced = (jrow == 0) | (jrow == cur) | (jrow == cur - 1)
    score = jnp.where(forced, 1e30, jnp.where(jrow <= cur, pslc_t, NEG))
    rank = _rank_before(score, nsb)
    drop = jnp.where((rank < min(N_SEL, nsb)) & (score > -1e29), 0.0, 1.0)
    drop = jnp.concatenate([drop, jnp.zeros((LANES - nsb, tq), F32)], axis=0)
    sel_ref[0, 0] = (_transpose_via_mxu(drop) * NEG).astype(BF16)


def _cmpsel(qraw, kcmp, vcmp, tq):
    B, S, _ = qraw.shape
    ncb = kcmp.shape[2]
    gw = NSA_HG * LANES
    assert S // SEL_BLOCK <= NSA_HD
    cmp_spec = pl.BlockSpec((1, 1, ncb, LANES), lambda b, g, t: (b, g, 0, 0))
    return pl.pallas_call(
        _cmpsel_kernel,
        out_shape=(jax.ShapeDtypeStruct((B, S, NSA_HEADS * LANES), F32),
                   jax.ShapeDtypeStruct((B, NSA_KV_GROUPS, S, LANES), BF16)),
        grid=(B, NSA_KV_GROUPS, S // tq),
        in_specs=[pl.BlockSpec((1, tq, gw), lambda b, g, t: (b, t, g)), cmp_spec, cmp_spec],
        out_specs=(pl.BlockSpec((1, tq, gw), lambda b, g, t: (b, t, g)),
                   pl.BlockSpec((1, 1, tq, LANES), lambda b, g, t: (b, g, t, 0))),
        compiler_params=_cparams(("arbitrary", "arbitrary", "arbitrary")),
        name="cmpsel",
    )(qraw, kcmp, vcmp)


def _slcwin_kernel(q_ref, sel_ref, ks_ref, kw_ref, vs_ref, vw_ref, oslc_ref, owin_ref, *, tk, wk):
    tq = q_ref.shape[1]
    S = ks_ref.shape[1]
    t = pl.program_id(2)
    q0 = t * tq
    qpos = q0 + lax.broadcasted_iota(jnp.int32, (tq, 1), 0)
    qw = [q_ref[0, :, h * LANES:(h + 1) * LANES] for h in range(NSA_HG)]
    sel = sel_ref[0, 0]
    qa = [jnp.concatenate([q, sel], axis=1) for q in qw]

    def normalise(acc):
        return acc * (1.0 / acc[:, NSA_HD:NSA_HD + 1])

    def kstep(j, carry, diag):
        k0 = pl.multiple_of(j * tk, tk)
        ks = ks_ref[0, pl.ds(k0, tk), :]
        vs = vs_ref[0, pl.ds(k0, tk), :]
        if diag:
            vis = (k0 + lax.broadcasted_iota(jnp.int32, (1, tk), 1)) <= qpos
        out = []
        for h in range(NSA_HG):
            m, acc = carry[h]
            s = lax.dot_general(qa[h], ks, NT, preferred_element_type=F32)
            if diag:
                s = jnp.where(vis, s, NEG)
            mn = jnp.maximum(m, jnp.max(s, axis=-1, keepdims=True))
            p = jnp.exp2(s - mn)
            acc = jnp.exp2(m - mn) * acc + jnp.dot(p.astype(BF16), vs, preferred_element_type=F32)
            out.append((mn, acc))
        return tuple(out)

    init = tuple((jnp.full((tq, 1), NEG, F32), jnp.zeros((tq, LANES), F32)) for _ in range(NSA_HG))
    jdiag = q0 // tk
    carry = lax.fori_loop(0, jdiag, lambda j, c: kstep(j, c, False), init)
    carry = kstep(jdiag, carry, True)
    for h in range(NSA_HG):
        oslc_ref[0, :, h * LANES:(h + 1) * LANES] = normalise(carry[h][1])

    nblk = S // tq
    start = pl.multiple_of(jnp.clip(t - WINDOW // tq, 0, nblk - wk // tq) * tq, tq)
    kw = kw_ref[0, pl.ds(start, wk), :]
    vw = vw_ref[0, pl.ds(start, wk), :]
    d = qpos - (start + lax.broadcasted_iota(jnp.int32, (1, wk), 1))
    okw = (d >= 0) & (d < WINDOW)
    for h in range(NSA_HG):
        s = jnp.where(okw, lax.dot_general(qw[h], kw, NT, preferred_element_type=F32), NEG)
        p = jnp.exp2(s - jnp.max(s, axis=-1, keepdims=True))
        owin_ref[0, :, h * LANES:(h + 1) * LANES] = normalise(jnp.dot(p.astype(BF16), vw, preferred_element_type=F32))


def _slcwin(qrot, sel, ks, kw, vaug, tq, tk):
    B, S, _ = qrot.shape
    gw = NSA_HG * LANES
    wk = min(WINDOW + tq, S)
    oshape = jax.ShapeDtypeStruct((B, S, NSA_HEADS * LANES), F32)
    ospec = pl.BlockSpec((1, tq, gw), lambda b, g, t: (b, t, g))
    return pl.pallas_call(
        functools.partial(_slcwin_kernel, tk=tk, wk=wk),
        out_shape=(oshape, oshape),
        grid=(B, NSA_KV_GROUPS, S // tq),
        in_specs=[pl.BlockSpec((1, tq, gw), lambda b, g, t: (b, t, g)),
                  pl.BlockSpec((1, 1, tq, LANES), lambda b, g, t: (b, g, t, 0)),
                  pl.BlockSpec((1, S, 2 * LANES), lambda b, g, t: (b, 0, 0)),
                  pl.BlockSpec((1, S, LANES), lambda b, g, t: (b, 0, 0)),
                  pl.BlockSpec((1, S, LANES), lambda b, g, t: (b, 0, g)),
                  pl.BlockSpec((1, S, LANES), lambda b, g, t: (b, 0, NSA_KV_GROUPS + g))],
        out_specs=(ospec, ospec),
        compiler_params=_cparams(("arbitrary", "arbitrary", "arbitrary")),
        name="slcwin",
    )(qrot, sel, ks, kw, vaug, vaug)


def _outproj_kernel(x_ref, ohg_ref, ocmp_ref, oslc_ref, owin_ref, gate_ref, whg_ref, wnsa_ref,
                    g1_ref, sh2_ref, sc2_ref, n2_ref, rwt_ref, rb_ref,
                    x1_ref, h2_ref, wc_ref):
    tm = x_ref.shape[1]
    gs = _sigmoid(gate_ref[0])
    mix = jnp.dot(ohg_ref[0], whg_ref[...], preferred_element_type=F32)
    for n in range(NSA_HEADS):
        cols = slice(n * LANES, (n + 1) * LANES)
        o = (gs[:, 3 * n:3 * n + 1] * ocmp_ref[0, :, cols] + gs[:, 3 * n + 1:3 * n + 2] * oslc_ref[0, :, cols]
             + gs[:, 3 * n + 2:3 * n + 3] * owin_ref[0, :, cols])
        mix = mix + jnp.dot(o.astype(BF16), wnsa_ref[cols, :], preferred_element_type=F32)
    x1 = x_ref[0] + g1_ref[0] * mix
    x1_ref[0] = x1
    h2 = (x1 * lax.rsqrt(jnp.mean(x1 * x1, axis=-1, keepdims=True) + EPS) * n2_ref[...]) * (1.0 + sc2_ref[0]) + sh2_ref[0]
    h2_ref[0] = h2.astype(BF16)
    scores = _sigmoid(lax.dot_general(rwt_ref[...], h2, NT, precision=HIGHEST, preferred_element_type=F32))
    choice = scores + rb_ref[...]
    per = N_EXPERTS // N_GROUPS
    c3 = choice.reshape(N_GROUPS, per, tm)
    erow = lax.broadcasted_iota(jnp.int32, c3.shape, 1)
    rank_in = jnp.zeros(c3.shape, F32)
    for k in range(per):
        ck = c3[:, k:k + 1, :]
        rank_in = rank_in + jnp.where((ck > c3) | ((ck == c3) & (erow > k)), 1.0, 0.0)
    grp_score = jnp.sum(jnp.where(rank_in < 2, c3, 0.0), axis=1)
    grp_keep = _rank_before(grp_score, N_GROUPS) < TOPK_GROUPS
    masked = jnp.where(grp_keep[:, None, :], c3, -jnp.inf).reshape(N_EXPERTS, tm)
    keep = _rank_before(masked, N_EXPERTS) < TOP_K
    tw = jnp.where(keep, scores, 0.0)
    tw = tw / jnp.sum(tw, axis=0, keepdims=True) * ROUTED_SCALE
    wc_ref[0] = _transpose_via_mxu(tw)


def _outproj(x, ohg, ocmp, oslc, owin, gate, whg, wnsa, mod3, n2, rwt, rb, tm):
    B, S, D = x.shape
    blk = lambda w: pl.BlockSpec((1, tm, w), lambda b, i: (b, i, 0))
    full = lambda a: pl.BlockSpec(a.shape, lambda b, i: (0,) * a.ndim)
    modc = lambda k: pl.BlockSpec((1, 1, D), lambda b, i, k=k: (b, 0, k))
    return pl.pallas_call(
        _outproj_kernel,
        out_shape=(jax.ShapeDtypeStruct((B, S, D), F32),
                   jax.ShapeDtypeStruct((B, S, D), BF16),
                   jax.ShapeDtypeStruct((B, S, N_EXPERTS), F32)),
        grid=(B, S // tm),
        in_specs=[blk(D), blk(HG_WIDTH), blk(NSA_HEADS * LANES), blk(NSA_HEADS * LANES), blk(NSA_HEADS * LANES),
                  blk(LANES), full(whg), full(wnsa), modc(2), modc(3), modc(4), full(n2), full(rwt), full(rb)],
        out_specs=(blk(D), blk(D), blk(N_EXPERTS)),
        compiler_params=_cparams(("arbitrary", "arbitrary")),
        name="outproj",
    )(x, ohg, ocmp, oslc, owin, gate, whg, wnsa, mod3, mod3, mod3, n2, rwt, rb)


def _moe_kernel(h_ref, wc_ref, wgu_ref, wdn_ref, sgu_ref, sdn_ref, x1_ref, g2_ref, fg_ref, o_ref, acc_ref, *, eg):
    e = pl.program_id(1)
    h = h_ref[...]
    hid = EXPERT_HIDDEN

    @pl.when(e == 0)
    def _():
        gu = jnp.dot(h, sgu_ref[...], preferred_element_type=F32)
        act = (_silu(gu[:, :hid]) * gu[:, hid:]).astype(BF16)
        acc_ref[...] = jnp.dot(act, sdn_ref[...], preferred_element_type=F32)

    wc = wc_ref[0]
    acts = []
    for k in range(eg):
        gu = jnp.dot(h, wgu_ref[k], preferred_element_type=F32)
        acts.append((_silu(gu[:, :hid]) * gu[:, hid:] * wc[:, k:k + 1]).astype(BF16))
    act = jnp.concatenate(acts, axis=1)
    acc_ref[...] += jnp.dot(act, wdn_ref[...].reshape(eg * hid, D_MODEL), preferred_element_type=F32)

    @pl.when(e == pl.num_programs(1) - 1)
    def _():
        x2 = x1_ref[...] + g2_ref[0] * acc_ref[...]
        o_ref[...] = x2 * lax.rsqrt(jnp.mean(x2 * x2, axis=-1, keepdims=True) + EPS) * fg_ref[...]


def _moe(h2, wc3, wgu, wdn, sgu, sdn, x1, mod3, fg, tm, eg, tiles_per_batch):
    T, D = h2.shape
    full = lambda a: pl.BlockSpec(a.shape, lambda i, e: (0,) * a.ndim)
    return pl.pallas_call(
        functools.partial(_moe_kernel, eg=eg),
        out_shape=jax.ShapeDtypeStruct((T, D), F32),
        grid=(T // tm, N_EXPERTS // eg),
        in_specs=[pl.BlockSpec((tm, D), lambda i, e: (i, 0)),
                  pl.BlockSpec((1, tm, eg), lambda i, e: (e, i, 0)),
                  pl.BlockSpec((eg, D, 2 * EXPERT_HIDDEN), lambda i, e: (e, 0, 0)),
                  pl.BlockSpec((eg, EXPERT_HIDDEN, D), lambda i, e: (e, 0, 0)),
                  full(sgu), full(sdn),
                  pl.BlockSpec((tm, D), lambda i, e: (i, 0)),
                  pl.BlockSpec((1, 1, D), lambda i, e: (i // tiles_per_batch, 0, 5)),
                  full(fg)],
        out_specs=pl.BlockSpec((tm, D), lambda i, e: (i, 0)),
        scratch_shapes=[pltpu.VMEM((tm, D), F32)],
        compiler_params=_cparams(("arbitrary", "arbitrary")),
        name="moe",
    )(h2, wc3, wgu, wdn, sgu, sdn, x1, mod3, fg)


def _pack_w_in(w_in):
    d = w_in.shape[0]
    q0 = 4 * HG_WIDTH
    kv0 = q0 + NSA_WIDTH
    g0 = kv0 + 6 * NSA_KV
    qcols = []
    for n in range(NSA_HEADS):
        g = n // NSA_HG
        head = w_in[:, q0 + n * NSA_HD:q0 + (n + 1) * NSA_HD]
        z = jnp.zeros((d, NSA_HD), w_in.dtype)
        qcols += [head, z] if g == 0 else [z, head]
    gate = jnp.pad(w_in[:, g0:], ((0, 0), (0, LANES - 3 * NSA_HEADS)))
    kv = lambda i: w_in[:, kv0 + i * NSA_KV:kv0 + (i + 1) * NSA_KV]
    z = jnp.zeros((d, NSA_HD), w_in.dtype)
    vcols = []
    for i in (3, 5):
        for g in range(NSA_KV_GROUPS):
            vcols += [kv(i)[:, g * NSA_HD:(g + 1) * NSA_HD], z]
    return jnp.concatenate([w_in[:, :q0]] + qcols + [kv(0), kv(1), kv(2), kv(4)] + vcols + [gate],
                           axis=1).astype(BF16)


def _pack_w_out(w_out):
    rows = []
    for n in range(NSA_HEADS):
        head = w_out[HG_WIDTH + n * NSA_HD:HG_WIDTH + (n + 1) * NSA_HD]
        rows += [head, jnp.zeros((LANES - NSA_HD, w_out.shape[1]), w_out.dtype)]
    return w_out[:HG_WIDTH].astype(BF16), jnp.concatenate(rows, axis=0).astype(BF16)


def _pack_cmp(pos, w1, b1, w2, lane_by_group):
    half = CMP_STRIDE * NSA_HD
    def rows_for(wh):
        w3 = wh.reshape(CMP_STRIDE, NSA_HD, CMP_HIDDEN)
        z = jnp.zeros_like(w3)
        return jnp.stack([jnp.concatenate([w3, z], axis=1).reshape(CMP_STRIDE * LANES, CMP_HIDDEN),
                          jnp.concatenate([z, w3], axis=1).reshape(CMP_STRIDE * LANES, CMP_HIDDEN)])
    w1a = rows_for(w1[:half]).astype(BF16)
    w1b = rows_for(w1[half:]).astype(BF16)
    z2 = jnp.zeros_like(w2)
    w2_first = jnp.concatenate([w2, z2], axis=1)
    w2p = jnp.stack([w2_first, jnp.concatenate([z2, w2], axis=1) if lane_by_group else w2_first]).astype(BF16)
    pe8 = jnp.pad(pos.reshape(1, CMP_BLOCK * NSA_HD), ((0, 7), (0, 0)))
    return w1a, w1b, pe8, w1, b1.reshape(1, CMP_HIDDEN), w2p


def _rope_tables(S):
    half = NSA_HD // 2
    inv = ROPE_THETA ** (-jnp.arange(half, dtype=F32) / half)
    ang = jnp.arange(S, dtype=F32)[:, None] * inv[None, :]
    cos, sin = jnp.cos(ang), jnp.sin(ang)
    reps = LANES // NSA_HD
    return jnp.tile(jnp.concatenate([cos, cos], axis=1), (1, reps)), jnp.tile(jnp.concatenate([-sin, sin], axis=1), (1, reps))


def _tiles(S):
    return dict(inproj=min(512, S), cmpsel=min(256, S), slc_q=512, slc_k=512, outproj=min(256, S))


def kernel(x, c, w_ada, b_ada, norm1_g, w_in, hg_lb_logits, hg_norm_g, cmp_pos_k, cmp_w1_k, cmp_b1_k, cmp_w2_k,
           cmp_pos_v, cmp_w1_v, cmp_b1_v, cmp_w2_v, w_out, norm2_g, router_w, router_bias, w_exp_gu, w_exp_dn,
           w_sh_gu, w_sh_dn, final_g):
    B, S, D = x.shape
    assert D == D_MODEL and w_ada.shape[0] == 1 and S % 512 == 0
    tl = _tiles(S)
    l = 0
    lb = jnp.cumsum(jax.nn.softmax(hg_lb_logits.astype(F32), axis=0), axis=0)[l].reshape(1, HG_WIDTH)
    c8 = jnp.pad(c, ((0, 8 - B), (0, 0)))
    mod3 = _ada(c8, w_ada[l], b_ada[l].reshape(1, -1))[:B].reshape(B, 1, 6 * D)
    cos, sin = _rope_tables(S)
    hg, qraw, qrot, kc, vc, ks, kw, vaug, gate = _inproj(x, mod3, norm1_g[l].reshape(1, D), _pack_w_in(w_in[l]),
                                                         cos, sin, tl["inproj"])
    ohg = _hgrn(hg, lb, hg_norm_g[l].reshape(1, HG_WIDTH))
    ncb = S // CMP_STRIDE
    kcmp = _cmpmlp(kc.reshape(B, ncb, CMP_STRIDE * LANES),
                   *_pack_cmp(cmp_pos_k[l], cmp_w1_k[l], cmp_b1_k[l], cmp_w2_k[l], True))
    vcmp = _cmpmlp(vc.reshape(B, ncb, CMP_STRIDE * LANES),
                   *_pack_cmp(cmp_pos_v[l], cmp_w1_v[l], cmp_b1_v[l], cmp_w2_v[l], False))
    ocmp, sel = _cmpsel(qraw, kcmp, vcmp, tl["cmpsel"])
    oslc, owin = _slcwin(qrot, sel, ks, kw, vaug, tl["slc_q"], tl["slc_k"])
    whg, wnsa = _pack_w_out(w_out[l])
    x1, h2, wc = _outproj(x, ohg, ocmp, oslc, owin, gate, whg, wnsa, mod3, norm2_g[l].reshape(1, D),
                          router_w[l].T, router_bias[l].reshape(N_EXPERTS, 1), tl["outproj"])
    T = B * S
    eg = 4
    tm = min(1024, S)
    wc3 = wc.reshape(T, N_EXPERTS // eg, eg).transpose(1, 0, 2)
    out = _moe(h2.reshape(T, D), wc3, w_exp_gu[l].astype(BF16), w_exp_dn[l].astype(BF16),
               w_sh_gu[l].astype(BF16), w_sh_dn[l].astype(BF16), x1.reshape(T, D), mod3,
               final_g.reshape(1, D), tm, eg, S // tm)
    return out.reshape(B, S, D)
```

```python
import functools

import numpy as np
import jax
import jax.numpy as jnp
from jax import lax
from jax.experimental import pallas as pl
from jax.experimental.pallas import tpu as pltpu

F32 = jnp.float32
BF16 = jnp.bfloat16
HIGHEST = lax.Precision.HIGHEST

D_MODEL = 1024
EPS = 1e-6
HG_HEADS = 4
HG_DK = 128
HG_DV = 128
HG_WIDTH = HG_HEADS * HG_DV
HG_CHUNK = 64
HG_SUB = 16
HG_SAFE_LOG_RANGE = 80.0
NSA_HEADS = 8
NSA_KV_GROUPS = 2
NSA_HG = NSA_HEADS // NSA_KV_GROUPS
NSA_HD = 64
NSA_WIDTH = NSA_HEADS * NSA_HD
NSA_KV = NSA_KV_GROUPS * NSA_HD
CMP_BLOCK = 32
CMP_STRIDE = 16
CMP_HIDDEN = 256
SEL_BLOCK = 64
N_SEL = 16
WINDOW = 512
ROPE_THETA = 10000.0
N_EXPERTS = 64
TOP_K = 8
N_GROUPS = 8
TOPK_GROUPS = 4
EXPERT_HIDDEN = 256
ROUTED_SCALE = 2.5

LANES = 128
NEG = -1e30
VMEM_LIMIT = 56 * 1024 * 1024

C_HG = 0
C_Q = 4 * HG_WIDTH
C_K = C_Q + NSA_HEADS * LANES
C_V = C_K + 4 * LANES
C_GATE = C_V + 4 * LANES
IN_COLS_P = C_GATE + LANES
LOG2E = 1.4426950408889634

NT = (((1,), (1,)), ((), ()))
TN = (((0,), (0,)), ((), ()))


def _cparams(sem):
    return pltpu.CompilerParams(dimension_semantics=sem, vmem_limit_bytes=VMEM_LIMIT)


def _sigmoid(x):
    return 1.0 / (1.0 + jnp.exp(-x))


def _silu(x):
    return x * _sigmoid(x)


def _gelu_tanh(x):
    return 0.5 * x * (1.0 + jnp.tanh(0.7978845608028654 * (x + 0.044715 * (x * x * x))))


def _ada_kernel(c_ref, w_ref, b_ref, o_ref):
    a = _silu(c_ref[...])
    o_ref[...] = jnp.dot(a, w_ref[...], precision=HIGHEST, preferred_element_type=F32) + b_ref[...]


def _ada(c8, w, b):
    n = w.shape[1]
    tn = 1024
    return pl.pallas_call(
        _ada_kernel,
        out_shape=jax.ShapeDtypeStruct((8, n), F32),
        grid=(n // tn,),
        in_specs=[pl.BlockSpec((8, D_MODEL), lambda j: (0, 0)),
                  pl.BlockSpec((D_MODEL, tn), lambda j: (0, j)),
                  pl.BlockSpec((1, tn), lambda j: (0, j))],
        out_specs=pl.BlockSpec((8, tn), lambda j: (0, j)),
        compiler_params=_cparams(("arbitrary",)),
        name="ada",
    )(c8, w, b)


def _rope(t, cos, sin_signed, first_half):
    rot = jnp.where(first_half, pltpu.roll(t, 96, 1), pltpu.roll(t, 32, 1))
    return t * cos + rot * sin_signed


def _inproj_kernel(x_ref, sh_ref, sc_ref, g_ref, w_ref, cos_ref, sin_ref,
                   hg_ref, qraw_ref, qrot_ref, kc_ref, vc_ref, ks_ref, kw_ref, va_ref, gate_ref, h_scr):
    tm = x_ref.shape[1]
    x = x_ref[0]
    y = x * lax.rsqrt(jnp.mean(x * x, axis=-1, keepdims=True) + EPS) * g_ref[...]
    h_scr[...] = (y * (1.0 + sc_ref[0]) + sh_ref[0]).astype(BF16)

    def mm(lo, width):
        return jnp.dot(h_scr[...], w_ref[:, lo:lo + width], preferred_element_type=F32)

    cos = cos_ref[...]
    sin = sin_ref[...]
    first_half = (lax.broadcasted_iota(jnp.int32, cos.shape, 1) % NSA_HD) < (NSA_HD // 2)
    for j in range(4):
        hg_ref[0, :, j * HG_WIDTH:(j + 1) * HG_WIDTH] = mm(C_HG + j * HG_WIDTH, HG_WIDTH)
    for n in range(NSA_HEADS):
        q = mm(C_Q + n * LANES, LANES) * (NSA_HD ** -0.5)
        qraw_ref[0, :, n * LANES:(n + 1) * LANES] = q.astype(BF16)
        qrot_ref[0, :, n * LANES:(n + 1) * LANES] = (_rope(q, cos, sin, first_half) * LOG2E).astype(BF16)
    kc_ref[0] = mm(C_K, LANES).astype(BF16)
    vc_ref[0] = mm(C_K + LANES, LANES).astype(BF16)
    ks_ref[0, :, 0:LANES] = _rope(mm(C_K + 2 * LANES, LANES), cos, sin, first_half).astype(BF16)
    lane = lax.broadcasted_iota(jnp.int32, (tm, LANES), 1)
    pos = pl.program_id(1) * tm + lax.broadcasted_iota(jnp.int32, (tm, LANES), 0)
    ks_ref[0, :, LANES:2 * LANES] = jnp.where(lane == pos // SEL_BLOCK, 1.0, 0.0).astype(BF16)
    kw_ref[0] = _rope(mm(C_K + 3 * LANES, LANES), cos, sin, first_half).astype(BF16)
    for j in range(4):
        v = mm(C_V + j * LANES, LANES)
        va_ref[0, :, j * LANES:(j + 1) * LANES] = jnp.where(lane == NSA_HD, 1.0, v).astype(BF16)
    gate_ref[0] = mm(C_GATE, LANES)


def _inproj(x, mod3, norm_g, w_p, cos, sin, tm):
    B, S, D = x.shape
    blk = lambda w: pl.BlockSpec((1, tm, w), lambda b, i: (b, i, 0))
    return pl.pallas_call(
        _inproj_kernel,
        out_shape=(jax.ShapeDtypeStruct((B, S, 4 * HG_WIDTH), F32),
                   jax.ShapeDtypeStruct((B, S, NSA_HEADS * LANES), BF16),
                   jax.ShapeDtypeStruct((B, S, NSA_HEADS * LANES), BF16),
                   jax.ShapeDtypeStruct((B, S, LANES), BF16),
                   jax.ShapeDtypeStruct((B, S, LANES), BF16),
                   jax.ShapeDtypeStruct((B, S, 2 * LANES), BF16),
                   jax.ShapeDtypeStruct((B, S, LANES), BF16),
                   jax.ShapeDtypeStruct((B, S, 4 * LANES), BF16),
                   jax.ShapeDtypeStruct((B, S, LANES), F32)),
        grid=(B, S // tm),
        in_specs=[blk(D),
                  pl.BlockSpec((1, 1, D), lambda b, i: (b, 0, 0)),
                  pl.BlockSpec((1, 1, D), lambda b, i: (b, 0, 1)),
                  pl.BlockSpec((1, D), lambda b, i: (0, 0)),
                  pl.BlockSpec((D, IN_COLS_P), lambda b, i: (0, 0)),
                  pl.BlockSpec((tm, LANES), lambda b, i: (i, 0)),
                  pl.BlockSpec((tm, LANES), lambda b, i: (i, 0))],
        out_specs=(blk(4 * HG_WIDTH), blk(NSA_HEADS * LANES), blk(NSA_HEADS * LANES),
                   blk(LANES), blk(LANES), blk(2 * LANES), blk(LANES), blk(4 * LANES), blk(LANES)),
        scratch_shapes=[pltpu.VMEM((tm, D), BF16)],
        compiler_params=_cparams(("arbitrary", "arbitrary")),
        name="inproj",
    )(x, mod3, mod3, norm_g, w_p, cos, sin)


def _hgrn_kernel(q_ref, f_ref, i_ref, gt_ref, lb_ref, ng_ref, o_ref):
    S = q_ref.shape[1]
    C, U = HG_CHUNK, HG_SUB
    lb = lb_ref[...]
    ng = ng_ref[...]
    ri = lax.broadcasted_iota(jnp.int32, (C, C), 0)
    ci = lax.broadcasted_iota(jnp.int32, (C, C), 1)
    tril = (ri >= ci).astype(F32)
    trow = lax.broadcasted_iota(jnp.int32, (U, 1), 0)

    def chunk(c, st_t):
        r0 = pl.multiple_of(c * C, C)
        rows = pl.ds(r0, C)
        f = lb + (1.0 - lb) * _sigmoid(f_ref[0, rows, :])
        kk = 1.0 - f
        b = jnp.dot(tril, jnp.log(f), precision=HIGHEST, preferred_element_type=F32)
        q = q_ref[0, rows, :] * (HG_DK ** -0.5)
        v = i_ref[0, rows, :]
        vb = v.astype(BF16)
        o_inter = lax.dot_general((q * jnp.exp(b)).astype(BF16), st_t.astype(BF16), NT,
                                  preferred_element_type=F32)
        parts = []
        for i in range(C // U):
            lo = i * U
            bi = b[lo:lo + U]
            qi = q[lo:lo + U]
            if i == 0:
                oi = jnp.zeros((U, HG_DV), F32)
            else:
                r = b[lo - 1:lo]
                qrel = (qi * jnp.exp(bi - r)).astype(BF16)
                kprev = (kk[:lo] * jnp.exp(r - b[:lo])).astype(BF16)
                a_off = lax.dot_general(qrel, kprev, NT, preferred_element_type=F32)
                oi = jnp.dot(a_off.astype(BF16), vb[:lo], preferred_element_type=F32)
            for s in range(U):
                valid = trow >= s
                e = jnp.exp(jnp.where(valid, bi - bi[s:s + 1], 0.0))
                a = jnp.sum(qi * e * kk[lo + s:lo + s + 1], axis=-1, keepdims=True)
                oi = oi + jnp.where(valid, a, 0.0) * v[lo + s:lo + s + 1]
            parts.append(oi)
        o = o_inter + jnp.concatenate(parts, axis=0)
        o = o * lax.rsqrt(jnp.mean(o * o, axis=-1, keepdims=True) + EPS) * ng
        o_ref[0, rows, :] = (o * _silu(gt_ref[0, rows, :])).astype(BF16)
        bl = b[C - 1:C]
        kv_t = lax.dot_general(vb, (kk * jnp.exp(bl - b)).astype(BF16), TN, preferred_element_type=F32)
        return jnp.exp(bl) * st_t + kv_t

    lax.fori_loop(0, S // C, chunk, jnp.zeros((HG_DV, HG_DK), F32))


def _hgrn_mxu_kernel(hg_ref, lb_ref, ng_ref, o_ref, st_scr):
    ts = hg_ref.shape[1]
    C, U = HG_CHUNK, HG_SUB
    ri = lax.broadcasted_iota(jnp.int32, (C, C), 0)
    ci = lax.broadcasted_iota(jnp.int32, (C, C), 1)
    tril = (ri >= ci).astype(F32)

    @pl.when(pl.program_id(1) == 0)
    def _():
        st_scr[...] = jnp.zeros(st_scr.shape, F32)

    W = HG_WIDTH
    NH = HG_HEADS
    head_of_lane = lax.broadcasted_iota(jnp.int32, (1, W), 1) // HG_DK
    hcols = [slice(h * HG_DK, (h + 1) * HG_DK) for h in range(NH)]

    def chunk(c, carry):
        rows = pl.ds(pl.multiple_of(c * C, C), C)
        lb = lb_ref[...]
        f = lb + (1.0 - lb) * _sigmoid(hg_ref[0, rows, W:2 * W])
        kk = 1.0 - f
        b = jnp.dot(tril, jnp.log(f), precision=HIGHEST, preferred_element_type=F32)
        q = hg_ref[0, rows, 0:W] * (HG_DK ** -0.5)
        vb = hg_ref[0, rows, 2 * W:3 * W].astype(BF16)
        qe = (q * jnp.exp(b)).astype(BF16)
        o_inter = jnp.concatenate(
            [lax.dot_general(qe[:, hc], st_scr[h].astype(BF16), NT, preferred_element_type=F32)
             for h, hc in enumerate(hcols)], axis=1)
        parts = []
        for i in range(C // U):
            lo, hi = i * U, (i + 1) * U
            r = b[lo - 1:lo] if i else jnp.zeros((1, W), F32)
            qrel = q[lo:hi] * jnp.exp(b[lo:hi] - r)
            kall = (kk[:hi] * jnp.exp(r - b[:hi])).astype(BF16)
            qbd = jnp.concatenate([jnp.where(head_of_lane == h, qrel, 0.0) for h in range(NH)], axis=0)
            a = lax.dot_general(qbd.astype(BF16), kall, NT, preferred_element_type=F32)
            trow = lax.broadcasted_iota(jnp.int32, (NH * U, hi), 0) % U
            a = jnp.where(lax.broadcasted_iota(jnp.int32, (NH * U, hi), 1) <= lo + trow, a, 0.0)
            oa = jnp.dot(a.astype(BF16), vb[:hi], preferred_element_type=F32)
            oi = jnp.where(head_of_lane == 0, oa[0:U], 0.0)
            for h in range(1, NH):
                oi = jnp.where(head_of_lane == h, oa[h * U:(h + 1) * U], oi)
            parts.append(oi)
        o = o_inter + jnp.concatenate(parts, axis=0)
        o = jnp.concatenate(
            [o[:, hc] * lax.rsqrt(jnp.mean(o[:, hc] * o[:, hc], axis=-1, keepdims=True) + EPS) for hc in hcols], axis=1)
        o_ref[0, rows, :] = (o * ng_ref[...] * _silu(hg_ref[0, rows, 3 * W:4 * W])).astype(BF16)
        bl = b[C - 1:C]
        ke = (kk * jnp.exp(bl - b)).astype(BF16)
        decay = jnp.exp(bl)
        for h, hc in enumerate(hcols):
            kv_t = lax.dot_general(vb[:, hc], ke[:, hc], TN, preferred_element_type=F32)
            st_scr[h] = decay[:, hc] * st_scr[h] + kv_t
        return carry

    lax.fori_loop(0, ts // C, chunk, 0)


def _hgrn_mxu(hg, lb, ng, ts):
    B, S, _ = hg.shape
    vec = pl.BlockSpec((1, HG_WIDTH), lambda b, i: (0, 0))
    return pl.pallas_call(
        _hgrn_mxu_kernel,
        out_shape=jax.ShapeDtypeStruct((B, S, HG_WIDTH), BF16),
        grid=(B, S // ts),
        in_specs=[pl.BlockSpec((1, ts, 4 * HG_WIDTH), lambda b, i: (b, i, 0)), vec, vec],
        out_specs=pl.BlockSpec((1, ts, HG_WIDTH), lambda b, i: (b, i, 0)),
        scratch_shapes=[pltpu.VMEM((HG_HEADS, HG_DV, HG_DK), F32)],
        compiler_params=_cparams(("arbitrary", "arbitrary")),
        name="hgrn_mxu",
    )(hg, lb, ng)


def _hgrn(hg, lb, ng):
    B, S, _ = hg.shape
    col = lambda k: pl.BlockSpec((1, S, HG_DK), lambda b, h, k=k: (b, 0, k * HG_HEADS + h))
    vec = pl.BlockSpec((1, HG_DK), lambda b, h: (0, h))
    return pl.pallas_call(
        _hgrn_kernel,
        out_shape=jax.ShapeDtypeStruct((B, S, HG_WIDTH), BF16),
        grid=(B, HG_HEADS),
        in_specs=[col(0), col(1), col(2), col(3), vec, vec],
        out_specs=pl.BlockSpec((1, S, HG_DV), lambda b, h: (b, 0, h)),
        compiler_params=_cparams(("arbitrary", "arbitrary")),
        name="hgrn",
    )(hg, hg, hg, hg, lb, ng)


def _cmpmlp_kernel(x_ref, w1a_ref, w1b_ref, pe_ref, w1_ref, b1_ref, w2_ref, o_ref):
    x = x_ref[0]
    hb = jnp.dot(pe_ref[...], w1_ref[...], precision=HIGHEST, preferred_element_type=F32)[0:1] + b1_ref[...]
    nrow = x.shape[0]
    for g in range(NSA_KV_GROUPS):
        a = jnp.dot(x, w1a_ref[g], preferred_element_type=F32)
        bm = jnp.dot(x, w1b_ref[g], preferred_element_type=F32)
        hdn = a + pltpu.roll(bm, nrow - 1, 0) + hb
        o_ref[0, g] = jnp.dot(_gelu_tanh(hdn).astype(BF16), w2_ref[g], preferred_element_type=F32).astype(BF16)


def _cmpmlp(x2, w1a, w1b, pe8, w1, b1, w2p):
    B, ncb, width = x2.shape
    full = lambda a: pl.BlockSpec(a.shape, lambda b: (0,) * a.ndim)
    return pl.pallas_call(
        _cmpmlp_kernel,
        out_shape=jax.ShapeDtypeStruct((B, NSA_KV_GROUPS, ncb, LANES), BF16),
        grid=(B,),
        in_specs=[pl.BlockSpec((1, ncb, width), lambda b: (b, 0, 0)),
                  full(w1a), full(w1b), full(pe8), full(w1), full(b1), full(w2p)],
        out_specs=pl.BlockSpec((1, NSA_KV_GROUPS, ncb, LANES), lambda b: (b, 0, 0, 0)),
        compiler_params=_cparams(("arbitrary",)),
        name="cmpmlp",
    )(x2, w1a, w1b, pe8, w1, b1, w2p)


def _rank_before(score, nrows):
    jrow = lax.broadcasted_iota(jnp.int32, score.shape, 0)
    rank = jnp.zeros(score.shape, F32)
    for k in range(nrows):
        rk = score[k:k + 1]
        beats = (rk > score) | ((rk == score) & (jrow > k))
        rank = rank + jnp.where(beats, 1.0, 0.0)
    return rank


def _transpose_via_mxu(xt):
    c = xt.shape[1]
    eye = (lax.broadcasted_iota(jnp.int32, (c, c), 0) == lax.broadcasted_iota(jnp.int32, (c, c), 1)).astype(F32)
    return lax.dot_general(eye, xt, NT, precision=HIGHEST, preferred_element_type=F32)


def _cmpsel_kernel(q_ref, kc_ref, vc_ref, o_ref, sel_ref):
    tq = q_ref.shape[1]
    ncb = kc_ref.shape[2]
    t = pl.program_id(2)
    kc = kc_ref[0, 0]
    vc = vc_ref[0, 0]
    pos = t * tq + lax.broadcasted_iota(jnp.int32, (tq, 1), 0)
    cblk = lax.broadcasted_iota(jnp.int32, (1, ncb), 1)
    vis = (cblk * CMP_STRIDE + CMP_BLOCK - 1) <= pos
    psum = jnp.zeros((tq, ncb), F32)
    for h in range(NSA_HG):
        s = lax.dot_general(q_ref[0, :, h * LANES:(h + 1) * LANES], kc, NT, preferred_element_type=F32)
        s = jnp.where(vis, s, NEG)
        p = jnp.exp(s - jnp.max(s, axis=-1, keepdims=True))
        p = jnp.where(vis, p / jnp.sum(p, axis=-1, keepdims=True), 0.0)
        o_ref[0, :, h * LANES:(h + 1) * LANES] = jnp.dot(p.astype(BF16), vc, preferred_element_type=F32)
        psum = psum + p
    nsb = ncb * CMP_STRIDE // SEL_BLOCK
    jb = lax.broadcasted_iota(jnp.int32, (nsb, ncb), 0) * SEL_BLOCK
    cb = lax.broadcasted_iota(jnp.int32, (nsb, ncb), 1) * CMP_STRIDE
    ov = jnp.maximum(jnp.minimum(cb + CMP_BLOCK, jb + SEL_BLOCK) - jnp.maximum(cb, jb), 0).astype(F32) / CMP_BLOCK
    pslc_t = lax.dot_general(ov, psum, NT, precision=HIGHEST, preferred_element_type=F32)
    posl = t * tq + lax.broadcasted_iota(jnp.int32, (1, tq), 1)
    cur = posl // SEL_BLOCK
    jrow = lax.broadcasted_iota(jnp.int32, (nsb, tq), 0)
    forced = (jrow == 0) | (jrow == cur) | (jrow == cur - 1)
    score = jnp.where(forced, 1e30, jnp.where(jrow <= cur, pslc_t, NEG))
    rank = _rank_before(score, nsb)
    drop = jnp.where((rank < min(N_SEL, nsb)) & (score > -1e29), 0.0, 1.0)
    drop = jnp.concatenate([drop, jnp.zeros((LANES - nsb, tq), F32)], axis=0)
    sel_ref[0, 0] = (_transpose_via_mxu(drop) * NEG).astype(BF16)


def _cmpsel(qraw, kcmp, vcmp, tq):
    B, S, _ = qraw.shape
    ncb = kcmp.shape[2]
    gw = NSA_HG * LANES
    assert S // SEL_BLOCK <= NSA_HD
    cmp_spec = pl.BlockSpec((1, 1, ncb, LANES), lambda b, g, t: (b, g, 0, 0))
    return pl.pallas_call(
        _cmpsel_kernel,
        out_shape=(jax.ShapeDtypeStruct((B, S, NSA_HEADS * LANES), F32),
                   jax.ShapeDtypeStruct((B, NSA_KV_GROUPS, S, LANES), BF16)),
        grid=(B, NSA_KV_GROUPS, S // tq),
        in_specs=[pl.BlockSpec((1, tq, gw), lambda b, g, t: (b, t, g)), cmp_spec, cmp_spec],
        out_specs=(pl.BlockSpec((1, tq, gw), lambda b, g, t: (b, t, g)),
                   pl.BlockSpec((1, 1, tq, LANES), lambda b, g, t: (b, g, t, 0))),
        compiler_params=_cparams(("arbitrary", "arbitrary", "arbitrary")),
        name="cmpsel",
    )(qraw, kcmp, vcmp)


def _slcwin_kernel(q_ref, sel_ref, ks_ref, kw_ref, vs_ref, vw_ref, oslc_ref, owin_ref, *, tk, wk):
    tq = q_ref.shape[1]
    S = ks_ref.shape[1]
    t = pl.program_id(2)
    q0 = t * tq
    qpos = q0 + lax.broadcasted_iota(jnp.int32, (tq, 1), 0)
    qw = [q_ref[0, :, h * LANES:(h + 1) * LANES] for h in range(NSA_HG)]
    sel = sel_ref[0, 0]
    qa = [jnp.concatenate([q, sel], axis=1) for q in qw]

    def normalise(acc):
        return acc * (1.0 / acc[:, NSA_HD:NSA_HD + 1])

    def kstep(j, carry, diag):
        k0 = pl.multiple_of(j * tk, tk)
        ks = ks_ref[0, pl.ds(k0, tk), :]
        vs = vs_ref[0, pl.ds(k0, tk), :]
        if diag:
            vis = (k0 + lax.broadcasted_iota(jnp.int32, (1, tk), 1)) <= qpos
        out = []
        for h in range(NSA_HG):
            m, acc = carry[h]
            s = lax.dot_general(qa[h], ks, NT, preferred_element_type=F32)
            if diag:
                s = jnp.where(vis, s, NEG)
            mn = jnp.maximum(m, jnp.max(s, axis=-1, keepdims=True))
            p = jnp.exp2(s - mn)
            acc = jnp.exp2(m - mn) * acc + jnp.dot(p.astype(BF16), vs, preferred_element_type=F32)
            out.append((mn, acc))
        return tuple(out)

    init = tuple((jnp.full((tq, 1), NEG, F32), jnp.zeros((tq, LANES), F32)) for _ in range(NSA_HG))
    jdiag = q0 // tk
    carry = lax.fori_loop(0, jdiag, lambda j, c: kstep(j, c, False), init)
    carry = kstep(jdiag, carry, True)
    for h in range(NSA_HG):
        oslc_ref[0, :, h * LANES:(h + 1) * LANES] = normalise(carry[h][1])

    nblk = S // tq
    start = pl.multiple_of(jnp.clip(t - WINDOW // tq, 0, nblk - wk // tq) * tq, tq)
    kw = kw_ref[0, pl.ds(start, wk), :]
    vw = vw_ref[0, pl.ds(start, wk), :]
    d = qpos - (start + lax.broadcasted_iota(jnp.int32, (1, wk), 1))
    okw = (d >= 0) & (d < WINDOW)
    for h in range(NSA_HG):
        s = jnp.where(okw, lax.dot_general(qw[h], kw, NT, preferred_element_type=F32), NEG)
        p = jnp.exp2(s - jnp.max(s, axis=-1, keepdims=True))
        owin_ref[0, :, h * LANES:(h + 1) * LANES] = normalise(jnp.dot(p.astype(BF16), vw, preferred_element_type=F32))


def _slcwin(qrot, sel, ks, kw, vaug, tq, tk):
    B, S, _ = qrot.shape
    gw = NSA_HG * LANES
    wk = min(WINDOW + tq, S)
    oshape = jax.ShapeDtypeStruct((B, S, NSA_HEADS * LANES), F32)
    ospec = pl.BlockSpec((1, tq, gw), lambda b, g, t: (b, t, g))
    return pl.pallas_call(
        functools.partial(_slcwin_kernel, tk=tk, wk=wk),
        out_shape=(oshape, oshape),
        grid=(B, NSA_KV_GROUPS, S // tq),
        in_specs=[pl.BlockSpec((1, tq, gw), lambda b, g, t: (b, t, g)),
                  pl.BlockSpec((1, 1, tq, LANES), lambda b, g, t: (b, g, t, 0)),
                  pl.BlockSpec((1, S, 2 * LANES), lambda b, g, t: (b, 0, 0)),
                  pl.BlockSpec((1, S, LANES), lambda b, g, t: (b, 0, 0)),
                  pl.BlockSpec((1, S, LANES), lambda b, g, t: (b, 0, g)),
                  pl.BlockSpec((1, S, LANES), lambda b, g, t: (b, 0, NSA_KV_GROUPS + g))],
        out_specs=(ospec, ospec),
        compiler_params=_cparams(("arbitrary", "arbitrary", "arbitrary")),
        name="slcwin",
    )(qrot, sel, ks, kw, vaug, vaug)


def _outproj_kernel(x_ref, ohg_ref, ocmp_ref, oslc_ref, owin_ref, gate_ref, whg_ref, wnsa_ref,
                    g1_ref, sh2_ref, sc2_ref, n2_ref, rwt_ref, rb_ref,
                    x1_ref, h2_ref, wc_ref):
    tm = x_ref.shape[1]
    gs = _sigmoid(gate_ref[0])
    mix = jnp.dot(ohg_ref[0], whg_ref[...], preferred_element_type=F32)
    for n in range(NSA_HEADS):
        cols = slice(n * LANES, (n + 1) * LANES)
        o = (gs[:, 3 * n:3 * n + 1] * ocmp_ref[0, :, cols] + gs[:, 3 * n + 1:3 * n + 2] * oslc_ref[0, :, cols]
             + gs[:, 3 * n + 2:3 * n + 3] * owin_ref[0, :, cols])
        mix = mix + jnp.dot(o.astype(BF16), wnsa_ref[cols, :], preferred_element_type=F32)
    x1 = x_ref[0] + g1_ref[0] * mix
    x1_ref[0] = x1
    h2 = (x1 * lax.rsqrt(jnp.mean(x1 * x1, axis=-1, keepdims=True) + EPS) * n2_ref[...]) * (1.0 + sc2_ref[0]) + sh2_ref[0]
    h2_ref[0] = h2.astype(BF16)
    scores = _sigmoid(lax.dot_general(rwt_ref[...], h2, NT, precision=HIGHEST, preferred_element_type=F32))
    choice = scores + rb_ref[...]
    per = N_EXPERTS // N_GROUPS
    c3 = choice.reshape(N_GROUPS, per, tm)
    erow = lax.broadcasted_iota(jnp.int32, c3.shape, 1)
    rank_in = jnp.zeros(c3.shape, F32)
    for k in range(per):
        ck = c3[:, k:k + 1, :]
        rank_in = rank_in + jnp.where((ck > c3) | ((ck == c3) & (erow > k)), 1.0, 0.0)
    grp_score = jnp.sum(jnp.where(rank_in < 2, c3, 0.0), axis=1)
    grp_keep = _rank_before(grp_score, N_GROUPS) < TOPK_GROUPS
    masked = jnp.where(grp_keep[:, None, :], c3, -jnp.inf).reshape(N_EXPERTS, tm)
    keep = _rank_before(masked, N_EXPERTS) < TOP_K
    tw = jnp.where(keep, scores, 0.0)
    tw = tw / jnp.sum(tw, axis=0, keepdims=True) * ROUTED_SCALE
    wc_ref[0] = _transpose_via_mxu(tw)


def _outproj(x, ohg, ocmp, oslc, owin, gate, whg, wnsa, mod3, n2, rwt, rb, tm):
    B, S, D = x.shape
    blk = lambda w: pl.BlockSpec((1, tm, w), lambda b, i: (b, i, 0))
    full = lambda a: pl.BlockSpec(a.shape, lambda b, i: (0,) * a.ndim)
    modc = lambda k: pl.BlockSpec((1, 1, D), lambda b, i, k=k: (b, 0, k))
    return pl.pallas_call(
        _outproj_kernel,
        out_shape=(jax.ShapeDtypeStruct((B, S, D), F32),
                   jax.ShapeDtypeStruct((B, S, D), BF16),
                   jax.ShapeDtypeStruct((B, S, N_EXPERTS), F32)),
        grid=(B, S // tm),
        in_specs=[blk(D), blk(HG_WIDTH), blk(NSA_HEADS * LANES), blk(NSA_HEADS * LANES), blk(NSA_HEADS * LANES),
                  blk(LANES), full(whg), full(wnsa), modc(2), modc(3), modc(4), full(n2), full(rwt), full(rb)],
        out_specs=(blk(D), blk(D), blk(N_EXPERTS)),
        compiler_params=_cparams(("arbitrary", "arbitrary")),
        name="outproj",
    )(x, ohg, ocmp, oslc, owin, gate, whg, wnsa, mod3, mod3, mod3, n2, rwt, rb)


def _moe_kernel(h_ref, wc_ref, wgu_ref, wdn_ref, sgu_ref, sdn_ref, x1_ref, g2_ref, fg_ref, o_ref, acc_ref, *, eg):
    e = pl.program_id(1)
    h = h_ref[...]
    hid = EXPERT_HIDDEN

    @pl.when(e == 0)
    def _():
        gu = jnp.dot(h, sgu_ref[...], preferred_element_type=F32)
        act = (_silu(gu[:, :hid]) * gu[:, hid:]).astype(BF16)
        acc_ref[...] = jnp.dot(act, sdn_ref[...], preferred_element_type=F32)

    wc = wc_ref[0]
    acts = []
    for k in range(eg):
        gu = jnp.dot(h, wgu_ref[k], preferred_element_type=F32)
        acts.append((_silu(gu[:, :hid]) * gu[:, hid:] * wc[:, k:k + 1]).astype(BF16))
    act = jnp.concatenate(acts, axis=1)
    acc_ref[...] += jnp.dot(act, wdn_ref[...].reshape(eg * hid, D_MODEL), preferred_element_type=F32)

    @pl.when(e == pl.num_programs(1) - 1)
    def _():
        x2 = x1_ref[...] + g2_ref[0] * acc_ref[...]
        o_ref[...] = x2 * lax.rsqrt(jnp.mean(x2 * x2, axis=-1, keepdims=True) + EPS) * fg_ref[...]


def _moe(h2, wc3, wgu, wdn, sgu, sdn, x1, mod3, fg, tm, eg, tiles_per_batch):
    T, D = h2.shape
    full = lambda a: pl.BlockSpec(a.shape, lambda i, e: (0,) * a.ndim)
    return pl.pallas_call(
        functools.partial(_moe_kernel, eg=eg),
        out_shape=jax.ShapeDtypeStruct((T, D), F32),
        grid=(T // tm, N_EXPERTS // eg),
        in_specs=[pl.BlockSpec((tm, D), lambda i, e: (i, 0)),
                  pl.BlockSpec((1, tm, eg), lambda i, e: (e, i, 0)),
                  pl.BlockSpec((eg, D, 2 * EXPERT_HIDDEN), lambda i, e: (e, 0, 0)),
                  pl.BlockSpec((eg, EXPERT_HIDDEN, D), lambda i, e: (e, 0, 0)),
                  full(sgu), full(sdn),
                  pl.BlockSpec((tm, D), lambda i, e: (i, 0)),
                  pl.BlockSpec((1, 1, D), lambda i, e: (i // tiles_per_batch, 0, 5)),
                  full(fg)],
        out_specs=pl.BlockSpec((tm, D), lambda i, e: (i, 0)),
        scratch_shapes=[pltpu.VMEM((tm, D), F32)],
        compiler_params=_cparams(("arbitrary", "arbitrary")),
        name="moe",
    )(h2, wc3, wgu, wdn, sgu, sdn, x1, mod3, fg)


def _pack_w_in(w_in):
    d = w_in.shape[0]
    q0 = 4 * HG_WIDTH
    kv0 = q0 + NSA_WIDTH
    g0 = kv0 + 6 * NSA_KV
    qcols = []
    for n in range(NSA_HEADS):
        g = n // NSA_HG
        head = w_in[:, q0 + n * NSA_HD:q0 + (n + 1) * NSA_HD]
        z = jnp.zeros((d, NSA_HD), w_in.dtype)
        qcols += [head, z] if g == 0 else [z, head]
    gate = jnp.pad(w_in[:, g0:], ((0, 0), (0, LANES - 3 * NSA_HEADS)))
    kv = lambda i: w_in[:, kv0 + i * NSA_KV:kv0 + (i + 1) * NSA_KV]
    z = jnp.zeros((d, NSA_HD), w_in.dtype)
    vcols = []
    for i in (3, 5):
        for g in range(NSA_KV_GROUPS):
            vcols += [kv(i)[:, g * NSA_HD:(g + 1) * NSA_HD], z]
    return jnp.concatenate([w_in[:, :q0]] + qcols + [kv(0), kv(1), kv(2), kv(4)] + vcols + [gate],
                           axis=1).astype(BF16)


def _pack_w_out(w_out):
    rows = []
    for n in range(NSA_HEADS):
        head = w_out[HG_WIDTH + n * NSA_HD:HG_WIDTH + (n + 1) * NSA_HD]
        rows += [head, jnp.zeros((LANES - NSA_HD, w_out.shape[1]), w_out.dtype)]
    return w_out[:HG_WIDTH].astype(BF16), jnp.concatenate(rows, axis=0).astype(BF16)


def _pack_cmp(pos, w1, b1, w2, lane_by_group):
    half = CMP_STRIDE * NSA_HD
    def rows_for(wh):
        w3 = wh.reshape(CMP_STRIDE, NSA_HD, CMP_HIDDEN)
        z = jnp.zeros_like(w3)
        return jnp.stack([jnp.concatenate([w3, z], axis=1).reshape(CMP_STRIDE * LANES, CMP_HIDDEN),
                          jnp.concatenate([z, w3], axis=1).reshape(CMP_STRIDE * LANES, CMP_HIDDEN)])
    w1a = rows_for(w1[:half]).astype(BF16)
    w1b = rows_for(w1[half:]).astype(BF16)
    z2 = jnp.zeros_like(w2)
    w2_first = jnp.concatenate([w2, z2], axis=1)
    w2p = jnp.stack([w2_first, jnp.concatenate([z2, w2], axis=1) if lane_by_group else w2_first]).astype(BF16)
    pe8 = jnp.pad(pos.reshape(1, CMP_BLOCK * NSA_HD), ((0, 7), (0, 0)))
    return w1a, w1b, pe8, w1, b1.reshape(1, CMP_HIDDEN), w2p


def _rope_tables(S):
    half = NSA_HD // 2
    inv = ROPE_THETA ** (-jnp.arange(half, dtype=F32) / half)
    ang = jnp.arange(S, dtype=F32)[:, None] * inv[None, :]
    cos, sin = jnp.cos(ang), jnp.sin(ang)
    reps = LANES // NSA_HD
    return jnp.tile(jnp.concatenate([cos, cos], axis=1), (1, reps)), jnp.tile(jnp.concatenate([-sin, sin], axis=1), (1, reps))


def _tiles(S):
    return dict(inproj=min(512, S), hgrn=min(512, S), cmpsel=min(256, S), slc_q=512, slc_k=512, outproj=min(256, S))


def kernel(x, c, w_ada, b_ada, norm1_g, w_in, hg_lb_logits, hg_norm_g, cmp_pos_k, cmp_w1_k, cmp_b1_k, cmp_w2_k,
           cmp_pos_v, cmp_w1_v, cmp_b1_v, cmp_w2_v, w_out, norm2_g, router_w, router_bias, w_exp_gu, w_exp_dn,
           w_sh_gu, w_sh_dn, final_g):
    B, S, D = x.shape
    assert D == D_MODEL and w_ada.shape[0] == 1 and S % 512 == 0
    tl = _tiles(S)
    l = 0
    lb = jnp.cumsum(jax.nn.softmax(hg_lb_logits.astype(F32), axis=0), axis=0)[l].reshape(1, HG_WIDTH)
    c8 = jnp.pad(c, ((0, 8 - B), (0, 0)))
    mod3 = _ada(c8, w_ada[l], b_ada[l].reshape(1, -1))[:B].reshape(B, 1, 6 * D)
    cos, sin = _rope_tables(S)
    hg, qraw, qrot, kc, vc, ks, kw, vaug, gate = _inproj(x, mod3, norm1_g[l].reshape(1, D), _pack_w_in(w_in[l]),
                                                         cos, sin, tl["inproj"])
    ng = hg_norm_g[l].reshape(1, HG_WIDTH)
    mxu_safe = -HG_SUB * jnp.log(jnp.min(lb)) <= HG_SAFE_LOG_RANGE
    ohg = lax.cond(mxu_safe, lambda: _hgrn_mxu(hg, lb, ng, tl["hgrn"]), lambda: _hgrn(hg, lb, ng))
    ncb = S // CMP_STRIDE
    kcmp = _cmpmlp(kc.reshape(B, ncb, CMP_STRIDE * LANES),
                   *_pack_cmp(cmp_pos_k[l], cmp_w1_k[l], cmp_b1_k[l], cmp_w2_k[l], True))
    vcmp = _cmpmlp(vc.reshape(B, ncb, CMP_STRIDE * LANES),
                   *_pack_cmp(cmp_pos_v[l], cmp_w1_v[l], cmp_b1_v[l], cmp_w2_v[l], False))
    ocmp, sel = _cmpsel(qraw, kcmp, vcmp, tl["cmpsel"])
    oslc, owin = _slcwin(qrot, sel, ks, kw, vaug, tl["slc_q"], tl["slc_k"])
    whg, wnsa = _pack_w_out(w_out[l])
    x1, h2, wc = _outproj(x, ohg, ocmp, oslc, owin, gate, whg, wnsa, mod3, norm2_g[l].reshape(1, D),
                          router_w[l].T, router_bias[l].reshape(N_EXPERTS, 1), tl["outproj"])
    T = B * S
    eg = 4
    tm = min(1024, S)
    wc3 = wc.reshape(T, N_EXPERTS // eg, eg).transpose(1, 0, 2)
    out = _moe(h2.reshape(T, D), wc3, w_exp_gu[l].astype(BF16), w_exp_dn[l].astype(BF16),
               w_sh_gu[l].astype(BF16), w_sh_dn[l].astype(BF16), x1.reshape(T, D), mod3,
               final_g.reshape(1, D), tm, eg, S // tm)
    return out.reshape(B, S, D)
```

```python
import functools

import numpy as np
import jax
import jax.numpy as jnp
from jax import lax
from jax.experimental import pallas as pl
from jax.experimental.pallas import tpu as pltpu

F32 = jnp.float32
BF16 = jnp.bfloat16
HIGHEST = lax.Precision.HIGHEST

D_MODEL = 1024
EPS = 1e-6
HG_HEADS = 4
HG_DK = 128
HG_DV = 128
HG_WIDTH = HG_HEADS * HG_DV
HG_CHUNK = 64
HG_SUB = 16
HG_SAFE_LOG_RANGE = 80.0
NSA_HEADS = 8
NSA_KV_GROUPS = 2
NSA_HG = NSA_HEADS // NSA_KV_GROUPS
NSA_HD = 64
NSA_WIDTH = NSA_HEADS * NSA_HD
NSA_KV = NSA_KV_GROUPS * NSA_HD
CMP_BLOCK = 32
CMP_STRIDE = 16
CMP_HIDDEN = 256
SEL_BLOCK = 64
N_SEL = 16
WINDOW = 512
ROPE_THETA = 10000.0
N_EXPERTS = 64
TOP_K = 8
N_GROUPS = 8
TOPK_GROUPS = 4
EXPERT_HIDDEN = 256
ROUTED_SCALE = 2.5
MOE_SUB = 256
MOE_CAP = 64
MOE_EG = 8

LANES = 128
NEG = -1e30
VMEM_LIMIT = 56 * 1024 * 1024

C_HG = 0
C_Q = 4 * HG_WIDTH
C_K = C_Q + NSA_HEADS * LANES
C_V = C_K + 4 * LANES
C_GATE = C_V + 4 * LANES
IN_COLS_P = C_GATE + LANES
LOG2E = 1.4426950408889634

NT = (((1,), (1,)), ((), ()))
TN = (((0,), (0,)), ((), ()))


def _cparams(sem):
    return pltpu.CompilerParams(dimension_semantics=sem, vmem_limit_bytes=VMEM_LIMIT)


def _sigmoid(x):
    return 1.0 / (1.0 + jnp.exp(-x))


def _silu(x):
    return x * _sigmoid(x)


def _gelu_tanh(x):
    return 0.5 * x * (1.0 + jnp.tanh(0.7978845608028654 * (x + 0.044715 * (x * x * x))))


def _ada_kernel(c_ref, w_ref, b_ref, o_ref):
    a = _silu(c_ref[...])
    o_ref[...] = jnp.dot(a, w_ref[...], precision=HIGHEST, preferred_element_type=F32) + b_ref[...]


def _ada(c8, w, b):
    n = w.shape[1]
    tn = 1024
    return pl.pallas_call(
        _ada_kernel,
        out_shape=jax.ShapeDtypeStruct((8, n), F32),
        grid=(n // tn,),
        in_specs=[pl.BlockSpec((8, D_MODEL), lambda j: (0, 0)),
                  pl.BlockSpec((D_MODEL, tn), lambda j: (0, j)),
                  pl.BlockSpec((1, tn), lambda j: (0, j))],
        out_specs=pl.BlockSpec((8, tn), lambda j: (0, j)),
        compiler_params=_cparams(("arbitrary",)),
        name="ada",
    )(c8, w, b)


def _rope(t, cos, sin_signed, first_half):
    rot = jnp.where(first_half, pltpu.roll(t, 96, 1), pltpu.roll(t, 32, 1))
    return t * cos + rot * sin_signed


def _inproj_kernel(x_ref, sh_ref, sc_ref, g_ref, w_ref, cos_ref, sin_ref,
                   hg_ref, qraw_ref, qrot_ref, kc_ref, vc_ref, ks_ref, kw_ref, va_ref, gate_ref, h_scr):
    tm = x_ref.shape[1]
    x = x_ref[0]
    y = x * lax.rsqrt(jnp.mean(x * x, axis=-1, keepdims=True) + EPS) * g_ref[...]
    h_scr[...] = (y * (1.0 + sc_ref[0]) + sh_ref[0]).astype(BF16)

    def mm(lo, width):
        return jnp.dot(h_scr[...], w_ref[:, lo:lo + width], preferred_element_type=F32)

    cos = cos_ref[...]
    sin = sin_ref[...]
    first_half = (lax.broadcasted_iota(jnp.int32, cos.shape, 1) % NSA_HD) < (NSA_HD // 2)
    for j in range(4):
        hg_ref[0, :, j * HG_WIDTH:(j + 1) * HG_WIDTH] = mm(C_HG + j * HG_WIDTH, HG_WIDTH)
    for n in range(NSA_HEADS):
        q = mm(C_Q + n * LANES, LANES) * (NSA_HD ** -0.5)
        qraw_ref[0, :, n * LANES:(n + 1) * LANES] = q.astype(BF16)
        qrot_ref[0, :, n * LANES:(n + 1) * LANES] = (_rope(q, cos, sin, first_half) * LOG2E).astype(BF16)
    kc_ref[0] = mm(C_K, LANES).astype(BF16)
    vc_ref[0] = mm(C_K + LANES, LANES).astype(BF16)
    ks_ref[0, :, 0:LANES] = _rope(mm(C_K + 2 * LANES, LANES), cos, sin, first_half).astype(BF16)
    lane = lax.broadcasted_iota(jnp.int32, (tm, LANES), 1)
    pos = pl.program_id(1) * tm + lax.broadcasted_iota(jnp.int32, (tm, LANES), 0)
    ks_ref[0, :, LANES:2 * LANES] = jnp.where(lane == pos // SEL_BLOCK, 1.0, 0.0).astype(BF16)
    kw_ref[0] = _rope(mm(C_K + 3 * LANES, LANES), cos, sin, first_half).astype(BF16)
    for j in range(4):
        v = mm(C_V + j * LANES, LANES)
        va_ref[0, :, j * LANES:(j + 1) * LANES] = jnp.where(lane == NSA_HD, 1.0, v).astype(BF16)
    gate_ref[0] = mm(C_GATE, LANES)


def _inproj(x, mod3, norm_g, w_p, cos, sin, tm):
    B, S, D = x.shape
    blk = lambda w: pl.BlockSpec((1, tm, w), lambda b, i: (b, i, 0))
    return pl.pallas_call(
        _inproj_kernel,
        out_shape=(jax.ShapeDtypeStruct((B, S, 4 * HG_WIDTH), F32),
                   jax.ShapeDtypeStruct((B, S, NSA_HEADS * LANES), BF16),
                   jax.ShapeDtypeStruct((B, S, NSA_HEADS * LANES), BF16),
                   jax.ShapeDtypeStruct((B, S, LANES), BF16),
                   jax.ShapeDtypeStruct((B, S, LANES), BF16),
                   jax.ShapeDtypeStruct((B, S, 2 * LANES), BF16),
                   jax.ShapeDtypeStruct((B, S, LANES), BF16),
                   jax.ShapeDtypeStruct((B, S, 4 * LANES), BF16),
                   jax.ShapeDtypeStruct((B, S, LANES), F32)),
        grid=(B, S // tm),
        in_specs=[blk(D),
                  pl.BlockSpec((1, 1, D), lambda b, i: (b, 0, 0)),
                  pl.BlockSpec((1, 1, D), lambda b, i: (b, 0, 1)),
                  pl.BlockSpec((1, D), lambda b, i: (0, 0)),
                  pl.BlockSpec((D, IN_COLS_P), lambda b, i: (0, 0)),
                  pl.BlockSpec((tm, LANES), lambda b, i: (i, 0)),
                  pl.BlockSpec((tm, LANES), lambda b, i: (i, 0))],
        out_specs=(blk(4 * HG_WIDTH), blk(NSA_HEADS * LANES), blk(NSA_HEADS * LANES),
                   blk(LANES), blk(LANES), blk(2 * LANES), blk(LANES), blk(4 * LANES), blk(LANES)),
        scratch_shapes=[pltpu.VMEM((tm, D), BF16)],
        compiler_params=_cparams(("arbitrary", "arbitrary")),
        name="inproj",
    )(x, mod3, mod3, norm_g, w_p, cos, sin)


def _hgrn_kernel(q_ref, f_ref, i_ref, gt_ref, lb_ref, ng_ref, o_ref):
    S = q_ref.shape[1]
    C, U = HG_CHUNK, HG_SUB
    lb = lb_ref[...]
    ng = ng_ref[...]
    ri = lax.broadcasted_iota(jnp.int32, (C, C), 0)
    ci = lax.broadcasted_iota(jnp.int32, (C, C), 1)
    tril = (ri >= ci).astype(F32)
    trow = lax.broadcasted_iota(jnp.int32, (U, 1), 0)

    def chunk(c, st_t):
        r0 = pl.multiple_of(c * C, C)
        rows = pl.ds(r0, C)
        f = lb + (1.0 - lb) * _sigmoid(f_ref[0, rows, :])
        kk = 1.0 - f
        b = jnp.dot(tril, jnp.log(f), precision=HIGHEST, preferred_element_type=F32)
        q = q_ref[0, rows, :] * (HG_DK ** -0.5)
        v = i_ref[0, rows, :]
        vb = v.astype(BF16)
        o_inter = lax.dot_general((q * jnp.exp(b)).astype(BF16), st_t.astype(BF16), NT,
                                  preferred_element_type=F32)
        parts = []
        for i in range(C // U):
            lo = i * U
            bi = b[lo:lo + U]
            qi = q[lo:lo + U]
            if i == 0:
                oi = jnp.zeros((U, HG_DV), F32)
            else:
                r = b[lo - 1:lo]
                qrel = (qi * jnp.exp(bi - r)).astype(BF16)
                kprev = (kk[:lo] * jnp.exp(r - b[:lo])).astype(BF16)
                a_off = lax.dot_general(qrel, kprev, NT, preferred_element_type=F32)
                oi = jnp.dot(a_off.astype(BF16), vb[:lo], preferred_element_type=F32)
            for s in range(U):
                valid = trow >= s
                e = jnp.exp(jnp.where(valid, bi - bi[s:s + 1], 0.0))
                a = jnp.sum(qi * e * kk[lo + s:lo + s + 1], axis=-1, keepdims=True)
                oi = oi + jnp.where(valid, a, 0.0) * v[lo + s:lo + s + 1]
            parts.append(oi)
        o = o_inter + jnp.concatenate(parts, axis=0)
        o = o * lax.rsqrt(jnp.mean(o * o, axis=-1, keepdims=True) + EPS) * ng
        o_ref[0, rows, :] = (o * _silu(gt_ref[0, rows, :])).astype(BF16)
        bl = b[C - 1:C]
        kv_t = lax.dot_general(vb, (kk * jnp.exp(bl - b)).astype(BF16), TN, preferred_element_type=F32)
        return jnp.exp(bl) * st_t + kv_t

    lax.fori_loop(0, S // C, chunk, jnp.zeros((HG_DV, HG_DK), F32))


def _hgrn_mxu_kernel(hg_ref, lb_ref, ng_ref, o_ref, st_scr):
    ts = hg_ref.shape[1]
    C, U = HG_CHUNK, HG_SUB
    ri = lax.broadcasted_iota(jnp.int32, (C, C), 0)
    ci = lax.broadcasted_iota(jnp.int32, (C, C), 1)
    tril = (ri >= ci).astype(F32)

    @pl.when(pl.program_id(1) == 0)
    def _():
        st_scr[...] = jnp.zeros(st_scr.shape, F32)

    W = HG_WIDTH
    NH = HG_HEADS
    head_of_lane = lax.broadcasted_iota(jnp.int32, (1, W), 1) // HG_DK
    hcols = [slice(h * HG_DK, (h + 1) * HG_DK) for h in range(NH)]

    def chunk(c, carry):
        rows = pl.ds(pl.multiple_of(c * C, C), C)
        lb = lb_ref[...]
        f = lb + (1.0 - lb) * _sigmoid(hg_ref[0, rows, W:2 * W])
        kk = 1.0 - f
        b = jnp.dot(tril, jnp.log(f), precision=HIGHEST, preferred_element_type=F32)
        q = hg_ref[0, rows, 0:W] * (HG_DK ** -0.5)
        vb = hg_ref[0, rows, 2 * W:3 * W].astype(BF16)
        qe = (q * jnp.exp(b)).astype(BF16)
        o_inter = jnp.concatenate(
            [lax.dot_general(qe[:, hc], st_scr[h].astype(BF16), NT, preferred_element_type=F32)
             for h, hc in enumerate(hcols)], axis=1)
        parts = []
        for i in range(C // U):
            lo, hi = i * U, (i + 1) * U
            r = b[lo - 1:lo] if i else jnp.zeros((1, W), F32)
            qrel = q[lo:hi] * jnp.exp(b[lo:hi] - r)
            kall = (kk[:hi] * jnp.exp(r - b[:hi])).astype(BF16)
            qbd = jnp.concatenate([jnp.where(head_of_lane == h, qrel, 0.0) for h in range(NH)], axis=0)
            a = lax.dot_general(qbd.astype(BF16), kall, NT, preferred_element_type=F32)
            trow = lax.broadcasted_iota(jnp.int32, (NH * U, hi), 0) % U
            a = jnp.where(lax.broadcasted_iota(jnp.int32, (NH * U, hi), 1) <= lo + trow, a, 0.0)
            oa = jnp.dot(a.astype(BF16), vb[:hi], preferred_element_type=F32)
            oi = jnp.where(head_of_lane == 0, oa[0:U], 0.0)
            for h in range(1, NH):
                oi = jnp.where(head_of_lane == h, oa[h * U:(h + 1) * U], oi)
            parts.append(oi)
        o = o_inter + jnp.concatenate(parts, axis=0)
        o = jnp.concatenate(
            [o[:, hc] * lax.rsqrt(jnp.mean(o[:, hc] * o[:, hc], axis=-1, keepdims=True) + EPS) for hc in hcols], axis=1)
        o_ref[0, rows, :] = (o * ng_ref[...] * _silu(hg_ref[0, rows, 3 * W:4 * W])).astype(BF16)
        bl = b[C - 1:C]
        ke = (kk * jnp.exp(bl - b)).astype(BF16)
        decay = jnp.exp(bl)
        for h, hc in enumerate(hcols):
            kv_t = lax.dot_general(vb[:, hc], ke[:, hc], TN, preferred_element_type=F32)
            st_scr[h] = decay[:, hc] * st_scr[h] + kv_t
        return carry

    lax.fori_loop(0, ts // C, chunk, 0)


def _hgrn_mxu(hg, lb, ng, ts):
    B, S, _ = hg.shape
    vec = pl.BlockSpec((1, HG_WIDTH), lambda b, i: (0, 0))
    return pl.pallas_call(
        _hgrn_mxu_kernel,
        out_shape=jax.ShapeDtypeStruct((B, S, HG_WIDTH), BF16),
        grid=(B, S // ts),
        in_specs=[pl.BlockSpec((1, ts, 4 * HG_WIDTH), lambda b, i: (b, i, 0)), vec, vec],
        out_specs=pl.BlockSpec((1, ts, HG_WIDTH), lambda b, i: (b, i, 0)),
        scratch_shapes=[pltpu.VMEM((HG_HEADS, HG_DV, HG_DK), F32)],
        compiler_params=_cparams(("arbitrary", "arbitrary")),
        name="hgrn_mxu",
    )(hg, lb, ng)


def _hgrn(hg, lb, ng):
    B, S, _ = hg.shape
    col = lambda k: pl.BlockSpec((1, S, HG_DK), lambda b, h, k=k: (b, 0, k * HG_HEADS + h))
    vec = pl.BlockSpec((1, HG_DK), lambda b, h: (0, h))
    return pl.pallas_call(
        _hgrn_kernel,
        out_shape=jax.ShapeDtypeStruct((B, S, HG_WIDTH), BF16),
        grid=(B, HG_HEADS),
        in_specs=[col(0), col(1), col(2), col(3), vec, vec],
        out_specs=pl.BlockSpec((1, S, HG_DV), lambda b, h: (b, 0, h)),
        compiler_params=_cparams(("arbitrary", "arbitrary")),
        name="hgrn",
    )(hg, hg, hg, hg, lb, ng)


def _cmpmlp_kernel(x_ref, w1a_ref, w1b_ref, pe_ref, w1_ref, b1_ref, w2_ref, o_ref):
    x = x_ref[0]
    hb = jnp.dot(pe_ref[...], w1_ref[...], precision=HIGHEST, preferred_element_type=F32)[0:1] + b1_ref[...]
    nrow = x.shape[0]
    for g in range(NSA_KV_GROUPS):
        a = jnp.dot(x, w1a_ref[g], preferred_element_type=F32)
        bm = jnp.dot(x, w1b_ref[g], preferred_element_type=F32)
        hdn = a + pltpu.roll(bm, nrow - 1, 0) + hb
        o_ref[0, g] = jnp.dot(_gelu_tanh(hdn).astype(BF16), w2_ref[g], preferred_element_type=F32).astype(BF16)


def _cmpmlp(x2, w1a, w1b, pe8, w1, b1, w2p):
    B, ncb, width = x2.shape
    full = lambda a: pl.BlockSpec(a.shape, lambda b: (0,) * a.ndim)
    return pl.pallas_call(
        _cmpmlp_kernel,
        out_shape=jax.ShapeDtypeStruct((B, NSA_KV_GROUPS, ncb, LANES), BF16),
        grid=(B,),
        in_specs=[pl.BlockSpec((1, ncb, width), lambda b: (b, 0, 0)),
                  full(w1a), full(w1b), full(pe8), full(w1), full(b1), full(w2p)],
        out_specs=pl.BlockSpec((1, NSA_KV_GROUPS, ncb, LANES), lambda b: (b, 0, 0, 0)),
        compiler_params=_cparams(("arbitrary",)),
        name="cmpmlp",
    )(x2, w1a, w1b, pe8, w1, b1, w2p)


def _rank_before(score, nrows):
    jrow = lax.broadcasted_iota(jnp.int32, score.shape, 0)
    rank = jnp.zeros(score.shape, F32)
    for k in range(nrows):
        rk = score[k:k + 1]
        beats = (rk > score) | ((rk == score) & (jrow > k))
        rank = rank + jnp.where(beats, 1.0, 0.0)
    return rank


def _transpose_via_mxu(xt):
    c = xt.shape[1]
    eye = (lax.broadcasted_iota(jnp.int32, (c, c), 0) == lax.broadcasted_iota(jnp.int32, (c, c), 1)).astype(F32)
    return lax.dot_general(eye, xt, NT, precision=HIGHEST, preferred_element_type=F32)


def _cmpsel_kernel(q_ref, kc_ref, vc_ref, o_ref, sel_ref):
    tq = q_ref.shape[1]
    ncb = kc_ref.shape[2]
    t = pl.program_id(2)
    kc = kc_ref[0, 0]
    vc = vc_ref[0, 0]
    pos = t * tq + lax.broadcasted_iota(jnp.int32, (tq, 1), 0)
    cblk = lax.broadcasted_iota(jnp.int32, (1, ncb), 1)
    vis = (cblk * CMP_STRIDE + CMP_BLOCK - 1) <= pos
    psum = jnp.zeros((tq, ncb), F32)
    for h in range(NSA_HG):
        s = lax.dot_general(q_ref[0, :, h * LANES:(h + 1) * LANES], kc, NT, preferred_element_type=F32)
        s = jnp.where(vis, s, NEG)
        p = jnp.exp(s - jnp.max(s, axis=-1, keepdims=True))
        p = jnp.where(vis, p / jnp.sum(p, axis=-1, keepdims=True), 0.0)
        o_ref[0, :, h * LANES:(h + 1) * LANES] = jnp.dot(p.astype(BF16), vc, preferred_element_type=F32)
        psum = psum + p
    nsb = ncb * CMP_STRIDE // SEL_BLOCK
    jb = lax.broadcasted_iota(jnp.int32, (nsb, ncb), 0) * SEL_BLOCK
    cb = lax.broadcasted_iota(jnp.int32, (nsb, ncb), 1) * CMP_STRIDE
    ov = jnp.maximum(jnp.minimum(cb + CMP_BLOCK, jb + SEL_BLOCK) - jnp.maximum(cb, jb), 0).astype(F32) / CMP_BLOCK
    pslc_t = lax.dot_general(ov, psum, NT, precision=HIGHEST, preferred_element_type=F32)
    posl = t * tq + lax.broadcasted_iota(jnp.int32, (1, tq), 1)
    cur = posl // SEL_BLOCK
    jrow = lax.broadcasted_iota(jnp.int32, (nsb, tq), 0)
    forced = (jrow == 0) | (jrow == cur) | (jrow == cur - 1)
    score = jnp.where(forced, 1e30, jnp.where(jrow <= cur, pslc_t, NEG))
    rank = _rank_before(score, nsb)
    drop = jnp.where((rank < min(N_SEL, nsb)) & (score > -1e29), 0.0, 1.0)
    drop = jnp.concatenate([drop, jnp.zeros((LANES - nsb, tq), F32)], axis=0)
    sel_ref[0, 0] = (_transpose_via_mxu(drop) * NEG).astype(BF16)


def _cmpsel(qraw, kcmp, vcmp, tq):
    B, S, _ = qraw.shape
    ncb = kcmp.shape[2]
    gw = NSA_HG * LANES
    assert S // SEL_BLOCK <= NSA_HD
    cmp_spec = pl.BlockSpec((1, 1, ncb, LANES), lambda b, g, t: (b, g, 0, 0))
    return pl.pallas_call(
        _cmpsel_kernel,
        out_shape=(jax.ShapeDtypeStruct((B, S, NSA_HEADS * LANES), F32),
                   jax.ShapeDtypeStruct((B, NSA_KV_GROUPS, S, LANES), BF16)),
        grid=(B, NSA_KV_GROUPS, S // tq),
        in_specs=[pl.BlockSpec((1, tq, gw), lambda b, g, t: (b, t, g)), cmp_spec, cmp_spec],
        out_specs=(pl.BlockSpec((1, tq, gw), lambda b, g, t: (b, t, g)),
                   pl.BlockSpec((1, 1, tq, LANES), lambda b, g, t: (b, g, t, 0))),
        compiler_params=_cparams(("arbitrary", "arbitrary", "arbitrary")),
        name="cmpsel",
    )(qraw, kcmp, vcmp)


def _slcwin_kernel(q_ref, sel_ref, ks_ref, kw_ref, vs_ref, vw_ref, oslc_ref, owin_ref, *, tk, wk):
    tq = q_ref.shape[1]
    S = ks_ref.shape[1]
    t = pl.program_id(2)
    q0 = t * tq
    qpos = q0 + lax.broadcasted_iota(jnp.int32, (tq, 1), 0)
    qw = [q_ref[0, :, h * LANES:(h + 1) * LANES] for h in range(NSA_HG)]
    sel = sel_ref[0, 0]
    qa = [jnp.concatenate([q, sel], axis=1) for q in qw]

    def normalise(acc):
        return acc * (1.0 / acc[:, NSA_HD:NSA_HD + 1])

    def kstep(j, carry, diag):
        k0 = pl.multiple_of(j * tk, tk)
        ks = ks_ref[0, pl.ds(k0, tk), :]
        vs = vs_ref[0, pl.ds(k0, tk), :]
        if diag:
            vis = (k0 + lax.broadcasted_iota(jnp.int32, (1, tk), 1)) <= qpos
        out = []
        for h in range(NSA_HG):
            m, acc = carry[h]
            s = lax.dot_general(qa[h], ks, NT, preferred_element_type=F32)
            if diag:
                s = jnp.where(vis, s, NEG)
            mn = jnp.maximum(m, jnp.max(s, axis=-1, keepdims=True))
            p = jnp.exp2(s - mn)
            acc = jnp.exp2(m - mn) * acc + jnp.dot(p.astype(BF16), vs, preferred_element_type=F32)
            out.append((mn, acc))
        return tuple(out)

    init = tuple((jnp.full((tq, 1), NEG, F32), jnp.zeros((tq, LANES), F32)) for _ in range(NSA_HG))
    jdiag = q0 // tk
    carry = lax.fori_loop(0, jdiag, lambda j, c: kstep(j, c, False), init)
    carry = kstep(jdiag, carry, True)
    for h in range(NSA_HG):
        oslc_ref[0, :, h * LANES:(h + 1) * LANES] = normalise(carry[h][1])

    nblk = S // tq
    start = pl.multiple_of(jnp.clip(t - WINDOW // tq, 0, nblk - wk // tq) * tq, tq)
    kw = kw_ref[0, pl.ds(start, wk), :]
    vw = vw_ref[0, pl.ds(start, wk), :]
    d = qpos - (start + lax.broadcasted_iota(jnp.int32, (1, wk), 1))
    okw = (d >= 0) & (d < WINDOW)
    for h in range(NSA_HG):
        s = jnp.where(okw, lax.dot_general(qw[h], kw, NT, preferred_element_type=F32), NEG)
        p = jnp.exp2(s - jnp.max(s, axis=-1, keepdims=True))
        owin_ref[0, :, h * LANES:(h + 1) * LANES] = normalise(jnp.dot(p.astype(BF16), vw, preferred_element_type=F32))


def _slcwin(qrot, sel, ks, kw, vaug, tq, tk):
    B, S, _ = qrot.shape
    gw = NSA_HG * LANES
    wk = min(WINDOW + tq, S)
    oshape = jax.ShapeDtypeStruct((B, S, NSA_HEADS * LANES), F32)
    ospec = pl.BlockSpec((1, tq, gw), lambda b, g, t: (b, t, g))
    return pl.pallas_call(
        functools.partial(_slcwin_kernel, tk=tk, wk=wk),
        out_shape=(oshape, oshape),
        grid=(B, NSA_KV_GROUPS, S // tq),
        in_specs=[pl.BlockSpec((1, tq, gw), lambda b, g, t: (b, t, g)),
                  pl.BlockSpec((1, 1, tq, LANES), lambda b, g, t: (b, g, t, 0)),
                  pl.BlockSpec((1, S, 2 * LANES), lambda b, g, t: (b, 0, 0)),
                  pl.BlockSpec((1, S, LANES), lambda b, g, t: (b, 0, 0)),
                  pl.BlockSpec((1, S, LANES), lambda b, g, t: (b, 0, g)),
                  pl.BlockSpec((1, S, LANES), lambda b, g, t: (b, 0, NSA_KV_GROUPS + g))],
        out_specs=(ospec, ospec),
        compiler_params=_cparams(("arbitrary", "arbitrary", "arbitrary")),
        name="slcwin",
    )(qrot, sel, ks, kw, vaug, vaug)


def _outproj_kernel(x_ref, ohg_ref, ocmp_ref, oslc_ref, owin_ref, gate_ref, whg_ref, wnsa_ref,
                    g1_ref, sh2_ref, sc2_ref, n2_ref, rwt_ref, rb_ref,
                    x1_ref, h2_ref, wc_ref, sc_ref, slot_t_ref, w_t_ref, ovf_ref):
    tm = x_ref.shape[1]
    gs = _sigmoid(gate_ref[0])
    mix = jnp.dot(ohg_ref[0], whg_ref[...], preferred_element_type=F32)
    for n in range(NSA_HEADS):
        cols = slice(n * LANES, (n + 1) * LANES)
        o = (gs[:, 3 * n:3 * n + 1] * ocmp_ref[0, :, cols] + gs[:, 3 * n + 1:3 * n + 2] * oslc_ref[0, :, cols]
             + gs[:, 3 * n + 2:3 * n + 3] * owin_ref[0, :, cols])
        mix = mix + jnp.dot(o.astype(BF16), wnsa_ref[cols, :], preferred_element_type=F32)
    x1 = x_ref[0] + g1_ref[0] * mix
    x1_ref[0] = x1
    h2 = (x1 * lax.rsqrt(jnp.mean(x1 * x1, axis=-1, keepdims=True) + EPS) * n2_ref[...]) * (1.0 + sc2_ref[0]) + sh2_ref[0]
    h2_ref[0] = h2.astype(BF16)
    scores = _sigmoid(lax.dot_general(rwt_ref[...], h2, NT, precision=HIGHEST, preferred_element_type=F32))
    choice = scores + rb_ref[...]
    per = N_EXPERTS // N_GROUPS
    c3 = choice.reshape(N_GROUPS, per, tm)
    erow = lax.broadcasted_iota(jnp.int32, c3.shape, 1)
    rank_in = jnp.zeros(c3.shape, F32)
    for k in range(per):
        ck = c3[:, k:k + 1, :]
        rank_in = rank_in + jnp.where((ck > c3) | ((ck == c3) & (erow > k)), 1.0, 0.0)
    grp_score = jnp.sum(jnp.where(rank_in < 2, c3, 0.0), axis=1)
    grp_keep = _rank_before(grp_score, N_GROUPS) < TOPK_GROUPS
    masked = jnp.where(grp_keep[:, None, :], c3, -jnp.inf).reshape(N_EXPERTS, tm)
    keep = _rank_before(masked, N_EXPERTS) < TOP_K
    tw = jnp.where(keep, scores, 0.0)
    tw = tw / jnp.sum(tw, axis=0, keepdims=True) * ROUTED_SCALE
    wc_ref[0] = _transpose_via_mxu(tw)
    earlier = (lax.broadcasted_iota(jnp.int32, (tm, tm), 0) < lax.broadcasted_iota(jnp.int32, (tm, tm), 1))
    pref = jnp.dot(jnp.where(keep, 1.0, 0.0).astype(BF16), earlier.astype(BF16), preferred_element_type=F32)
    slot = jnp.where(keep, pref, -1.0)
    sc_ref[0] = _transpose_via_mxu(slot)
    slot_t_ref[:, 0] = slot.reshape(N_EXPERTS // MOE_EG, MOE_EG, tm)
    w_t_ref[:, 0] = tw.reshape(N_EXPERTS // MOE_EG, MOE_EG, tm)
    ovf_ref[...] = jnp.zeros(ovf_ref.shape, F32) + jnp.max(jnp.where(keep & (pref >= MOE_CAP), 1.0, 0.0))


def _outproj(x, ohg, ocmp, oslc, owin, gate, whg, wnsa, mod3, n2, rwt, rb, tm):
    B, S, D = x.shape
    blk = lambda w: pl.BlockSpec((1, tm, w), lambda b, i: (b, i, 0))
    full = lambda a: pl.BlockSpec(a.shape, lambda b, i: (0,) * a.ndim)
    modc = lambda k: pl.BlockSpec((1, 1, D), lambda b, i, k=k: (b, 0, k))
    nt = S // tm
    ng = N_EXPERTS // MOE_EG
    tspec = pl.BlockSpec((ng, 1, MOE_EG, tm), lambda b, i: (0, b * nt + i, 0, 0))
    return pl.pallas_call(
        _outproj_kernel,
        out_shape=(jax.ShapeDtypeStruct((B, S, D), F32),
                   jax.ShapeDtypeStruct((B, S, D), BF16),
                   jax.ShapeDtypeStruct((B, S, N_EXPERTS), F32),
                   jax.ShapeDtypeStruct((B, S, N_EXPERTS), F32),
                   jax.ShapeDtypeStruct((ng, B * nt, MOE_EG, tm), F32),
                   jax.ShapeDtypeStruct((ng, B * nt, MOE_EG, tm), F32),
                   jax.ShapeDtypeStruct((B * nt, 8, LANES), F32)),
        grid=(B, nt),
        in_specs=[blk(D), blk(HG_WIDTH), blk(NSA_HEADS * LANES), blk(NSA_HEADS * LANES), blk(NSA_HEADS * LANES),
                  blk(LANES), full(whg), full(wnsa), modc(2), modc(3), modc(4), full(n2), full(rwt), full(rb)],
        out_specs=(blk(D), blk(D), blk(N_EXPERTS), blk(N_EXPERTS), tspec, tspec,
                   pl.BlockSpec((1, 8, LANES), lambda b, i: (b * nt + i, 0, 0))),
        compiler_params=_cparams(("arbitrary", "arbitrary")),
        name="outproj",
    )(x, ohg, ocmp, oslc, owin, gate, whg, wnsa, mod3, mod3, mod3, n2, rwt, rb)


def _swiglu_hidden(x, wgu):
    gu = jnp.dot(x, wgu, preferred_element_type=F32)
    return _silu(gu[:, :EXPERT_HIDDEN]) * gu[:, EXPERT_HIDDEN:]


def _moe_kernel(ovf_ref, h_ref, slot_t_ref, w_t_ref, slot_c_ref, w_c_ref, wgu_ref, wdn_ref, sgu_ref, sdn_ref,
                o_ref, x_scr, y_scr, p_scr):
    i = pl.program_id(0)
    e0 = pl.program_id(1)
    tm = h_ref.shape[0]
    nsub = tm // MOE_SUB
    cap, eg = MOE_CAP, MOE_EG

    @pl.when(e0 == 0)
    def _():
        act = _swiglu_hidden(h_ref[...], sgu_ref[...]).astype(BF16)
        o_ref[...] = jnp.dot(act, sdn_ref[...], preferred_element_type=F32)

    @pl.when(ovf_ref[i] == 0)
    def _():
        rslot = lax.broadcasted_iota(jnp.int32, (cap, 1), 0).astype(F32)
        for s in range(nsub):
            st = slot_t_ref[0, s]
            p = jnp.concatenate([jnp.where(rslot == st[e:e + 1, :], 1.0, 0.0) for e in range(eg)], axis=0)
            p_scr[s] = p.astype(BF16)
            x_scr[s] = jnp.dot(p_scr[s], h_ref[s * MOE_SUB:(s + 1) * MOE_SUB, :],
                               preferred_element_type=F32).astype(BF16)
        for e in range(eg):
            rows = slice(e * cap, (e + 1) * cap)
            xe = jnp.concatenate([x_scr[s, rows, :] for s in range(nsub)], axis=0)
            wslot = jnp.concatenate(
                [jnp.sum(p_scr[s, rows, :].astype(F32) * w_t_ref[0, s][e:e + 1, :], axis=-1, keepdims=True)
                 for s in range(nsub)], axis=0)
            y = jnp.dot((_swiglu_hidden(xe, wgu_ref[e]) * wslot).astype(BF16), wdn_ref[e], preferred_element_type=F32)
            for s in range(nsub):
                y_scr[s, rows, :] = y[s * cap:(s + 1) * cap].astype(BF16)
        lane_slot = (lax.broadcasted_iota(jnp.int32, (1, eg * cap), 1) % cap).astype(F32)
        spread = (lax.broadcasted_iota(jnp.int32, (eg, eg * cap), 1) // cap
                  == lax.broadcasted_iota(jnp.int32, (eg, eg * cap), 0)).astype(BF16)
        for s in range(nsub):
            toks = slice(s * MOE_SUB, (s + 1) * MOE_SUB)
            sc = jnp.dot(slot_c_ref[0, toks, :].astype(BF16), spread, preferred_element_type=F32)
            pt = jnp.where(sc == lane_slot, 1.0, 0.0).astype(BF16)
            o_ref[toks, :] += jnp.dot(pt, y_scr[s], preferred_element_type=F32)

    @pl.when(ovf_ref[i] != 0)
    def _():
        h = h_ref[...]
        wc = w_c_ref[0]
        acts = [(_swiglu_hidden(h, wgu_ref[e]) * wc[:, e:e + 1]).astype(BF16) for e in range(eg)]
        o_ref[...] += jnp.dot(jnp.concatenate(acts, axis=1), wdn_ref[...].reshape(eg * EXPERT_HIDDEN, D_MODEL),
                              preferred_element_type=F32)


def _moe(ovf, h2, slot_t, w_t, slot_c, w_c, wgu, wdn, sgu, sdn, tm):
    T, D = h2.shape
    eg, nsub = MOE_EG, tm // MOE_SUB
    full = lambda a: pl.BlockSpec(a.shape, lambda i, e, o: (0,) * a.ndim)
    tspec = pl.BlockSpec((1, nsub, eg, MOE_SUB), lambda i, e, o: (e, i, 0, 0))
    cspec = pl.BlockSpec((1, tm, eg), lambda i, e, o: (e, i, 0))
    return pl.pallas_call(
        _moe_kernel,
        out_shape=jax.ShapeDtypeStruct((T, D), F32),
        grid_spec=pltpu.PrefetchScalarGridSpec(
            num_scalar_prefetch=1,
            grid=(T // tm, N_EXPERTS // eg),
            in_specs=[pl.BlockSpec((tm, D), lambda i, e, o: (i, 0)), tspec, tspec, cspec, cspec,
                      pl.BlockSpec((eg, D, 2 * EXPERT_HIDDEN), lambda i, e, o: (e, 0, 0)),
                      pl.BlockSpec((eg, EXPERT_HIDDEN, D), lambda i, e, o: (e, 0, 0)),
                      full(sgu), full(sdn)],
            out_specs=pl.BlockSpec((tm, D), lambda i, e, o: (i, 0)),
            scratch_shapes=[pltpu.VMEM((nsub, eg * MOE_CAP, D), BF16),
                            pltpu.VMEM((nsub, eg * MOE_CAP, D), BF16),
                            pltpu.VMEM((nsub, eg * MOE_CAP, MOE_SUB), BF16)]),
        compiler_params=_cparams(("arbitrary", "arbitrary")),
        name="moe",
    )(ovf, h2, slot_t, w_t, slot_c, w_c, wgu, wdn, sgu, sdn)


def _final_kernel(x1_ref, moe_ref, g2_ref, fg_ref, o_ref):
    x2 = x1_ref[0] + g2_ref[0] * moe_ref[0]
    o_ref[0] = x2 * lax.rsqrt(jnp.mean(x2 * x2, axis=-1, keepdims=True) + EPS) * fg_ref[...]


def _final(x1, moe, mod3, fg, tm):
    B, S, D = x1.shape
    blk = pl.BlockSpec((1, tm, D), lambda b, i: (b, i, 0))
    return pl.pallas_call(
        _final_kernel,
        out_shape=jax.ShapeDtypeStruct((B, S, D), F32),
        grid=(B, S // tm),
        in_specs=[blk, blk, pl.BlockSpec((1, 1, D), lambda b, i: (b, 0, 5)), pl.BlockSpec((1, D), lambda b, i: (0, 0))],
        out_specs=blk,
        compiler_params=_cparams(("arbitrary", "arbitrary")),
        name="final",
    )(x1, moe, mod3, fg)


def _pack_w_in(w_in):
    d = w_in.shape[0]
    q0 = 4 * HG_WIDTH
    kv0 = q0 + NSA_WIDTH
    g0 = kv0 + 6 * NSA_KV
    qcols = []
    for n in range(NSA_HEADS):
        g = n // NSA_HG
        head = w_in[:, q0 + n * NSA_HD:q0 + (n + 1) * NSA_HD]
        z = jnp.zeros((d, NSA_HD), w_in.dtype)
        qcols += [head, z] if g == 0 else [z, head]
    gate = jnp.pad(w_in[:, g0:], ((0, 0), (0, LANES - 3 * NSA_HEADS)))
    kv = lambda i: w_in[:, kv0 + i * NSA_KV:kv0 + (i + 1) * NSA_KV]
    z = jnp.zeros((d, NSA_HD), w_in.dtype)
    vcols = []
    for i in (3, 5):
        for g in range(NSA_KV_GROUPS):
            vcols += [kv(i)[:, g * NSA_HD:(g + 1) * NSA_HD], z]
    return jnp.concatenate([w_in[:, :q0]] + qcols + [kv(0), kv(1), kv(2), kv(4)] + vcols + [gate],
                           axis=1).astype(BF16)


def _pack_w_out(w_out):
    rows = []
    for n in range(NSA_HEADS):
        head = w_out[HG_WIDTH + n * NSA_HD:HG_WIDTH + (n + 1) * NSA_HD]
        rows += [head, jnp.zeros((LANES - NSA_HD, w_out.shape[1]), w_out.dtype)]
    return w_out[:HG_WIDTH].astype(BF16), jnp.concatenate(rows, axis=0).astype(BF16)


def _pack_cmp(pos, w1, b1, w2, lane_by_group):
    half = CMP_STRIDE * NSA_HD
    def rows_for(wh):
        w3 = wh.reshape(CMP_STRIDE, NSA_HD, CMP_HIDDEN)
        z = jnp.zeros_like(w3)
        return jnp.stack([jnp.concatenate([w3, z], axis=1).reshape(CMP_STRIDE * LANES, CMP_HIDDEN),
                          jnp.concatenate([z, w3], axis=1).reshape(CMP_STRIDE * LANES, CMP_HIDDEN)])
    w1a = rows_for(w1[:half]).astype(BF16)
    w1b = rows_for(w1[half:]).astype(BF16)
    z2 = jnp.zeros_like(w2)
    w2_first = jnp.concatenate([w2, z2], axis=1)
    w2p = jnp.stack([w2_first, jnp.concatenate([z2, w2], axis=1) if lane_by_group else w2_first]).astype(BF16)
    pe8 = jnp.pad(pos.reshape(1, CMP_BLOCK * NSA_HD), ((0, 7), (0, 0)))
    return w1a, w1b, pe8, w1, b1.reshape(1, CMP_HIDDEN), w2p


def _rope_tables(S):
    half = NSA_HD // 2
    inv = ROPE_THETA ** (-jnp.arange(half, dtype=F32) / half)
    ang = jnp.arange(S, dtype=F32)[:, None] * inv[None, :]
    cos, sin = jnp.cos(ang), jnp.sin(ang)
    reps = LANES // NSA_HD
    return jnp.tile(jnp.concatenate([cos, cos], axis=1), (1, reps)), jnp.tile(jnp.concatenate([-sin, sin], axis=1), (1, reps))


def _tiles(S):
    return dict(inproj=min(512, S), hgrn=min(512, S), cmpsel=min(256, S), slc_q=512, slc_k=512, moe=min(1024, S), final=min(512, S))


def kernel(x, c, w_ada, b_ada, norm1_g, w_in, hg_lb_logits, hg_norm_g, cmp_pos_k, cmp_w1_k, cmp_b1_k, cmp_w2_k,
           cmp_pos_v, cmp_w1_v, cmp_b1_v, cmp_w2_v, w_out, norm2_g, router_w, router_bias, w_exp_gu, w_exp_dn,
           w_sh_gu, w_sh_dn, final_g):
    B, S, D = x.shape
    assert D == D_MODEL and w_ada.shape[0] == 1 and S % 512 == 0
    tl = _tiles(S)
    l = 0
    lb = jnp.cumsum(jax.nn.softmax(hg_lb_logits.astype(F32), axis=0), axis=0)[l].reshape(1, HG_WIDTH)
    c8 = jnp.pad(c, ((0, 8 - B), (0, 0)))
    mod3 = _ada(c8, w_ada[l], b_ada[l].reshape(1, -1))[:B].reshape(B, 1, 6 * D)
    cos, sin = _rope_tables(S)
    hg, qraw, qrot, kc, vc, ks, kw, vaug, gate = _inproj(x, mod3, norm1_g[l].reshape(1, D), _pack_w_in(w_in[l]),
                                                         cos, sin, tl["inproj"])
    ng = hg_norm_g[l].reshape(1, HG_WIDTH)
    mxu_safe = -HG_SUB * jnp.log(jnp.min(lb)) <= HG_SAFE_LOG_RANGE
    ohg = lax.cond(mxu_safe, lambda: _hgrn_mxu(hg, lb, ng, tl["hgrn"]), lambda: _hgrn(hg, lb, ng))
    ncb = S // CMP_STRIDE
    kcmp = _cmpmlp(kc.reshape(B, ncb, CMP_STRIDE * LANES),
                   *_pack_cmp(cmp_pos_k[l], cmp_w1_k[l], cmp_b1_k[l], cmp_w2_k[l], True))
    vcmp = _cmpmlp(vc.reshape(B, ncb, CMP_STRIDE * LANES),
                   *_pack_cmp(cmp_pos_v[l], cmp_w1_v[l], cmp_b1_v[l], cmp_w2_v[l], False))
    ocmp, sel = _cmpsel(qraw, kcmp, vcmp, tl["cmpsel"])
    oslc, owin = _slcwin(qrot, sel, ks, kw, vaug, tl["slc_q"], tl["slc_k"])
    whg, wnsa = _pack_w_out(w_out[l])
    x1, h2, wc, sc, slot_t, w_t, ovf = _outproj(x, ohg, ocmp, oslc, owin, gate, whg, wnsa, mod3,
                                                norm2_g[l].reshape(1, D), router_w[l].T,
                                                router_bias[l].reshape(N_EXPERTS, 1), MOE_SUB)
    T = B * S
    tm = tl["moe"]
    by_group = lambda a: a.reshape(T, N_EXPERTS // MOE_EG, MOE_EG).transpose(1, 0, 2)
    tile_ovf = jnp.max(ovf[:, 0, 0].reshape(T // tm, tm // MOE_SUB), axis=1).astype(jnp.int32)
    moe = _moe(tile_ovf, h2.reshape(T, D), slot_t, w_t, by_group(sc), by_group(wc),
               w_exp_gu[l].astype(BF16), w_exp_dn[l].astype(BF16), w_sh_gu[l].astype(BF16), w_sh_dn[l].astype(BF16), tm)
    return _final(x1, moe.reshape(B, S, D), mod3, final_g.reshape(1, D), tl["final"])
```

```python
import functools

import numpy as np
import jax
import jax.numpy as jnp
from jax import lax
from jax.experimental import pallas as pl
from jax.experimental.pallas import tpu as pltpu

F32 = jnp.float32
BF16 = jnp.bfloat16
HIGHEST = lax.Precision.HIGHEST

D_MODEL = 1024
EPS = 1e-6
HG_HEADS = 4
HG_DK = 128
HG_DV = 128
HG_WIDTH = HG_HEADS * HG_DV
HG_CHUNK = 64
HG_SUB = 16
HG_SAFE_LOG_RANGE = 80.0
NSA_HEADS = 8
NSA_KV_GROUPS = 2
NSA_HG = NSA_HEADS // NSA_KV_GROUPS
NSA_HD = 64
NSA_WIDTH = NSA_HEADS * NSA_HD
NSA_KV = NSA_KV_GROUPS * NSA_HD
CMP_BLOCK = 32
CMP_STRIDE = 16
CMP_HIDDEN = 256
SEL_BLOCK = 64
N_SEL = 16
WINDOW = 512
ROPE_THETA = 10000.0
N_EXPERTS = 64
TOP_K = 8
N_GROUPS = 8
TOPK_GROUPS = 4
EXPERT_HIDDEN = 256
ROUTED_SCALE = 2.5
MOE_SUB = 256
MOE_ROUND = 32
MOE_EG = 8

LANES = 128
NEG = -1e30
VMEM_LIMIT = 56 * 1024 * 1024

C_HG = 0
C_Q = 4 * HG_WIDTH
C_K = C_Q + NSA_HEADS * LANES
C_V = C_K + 4 * LANES
C_GATE = C_V + 4 * LANES
IN_COLS_P = C_GATE + LANES
LOG2E = 1.4426950408889634

NT = (((1,), (1,)), ((), ()))
TN = (((0,), (0,)), ((), ()))


def _cparams(sem):
    return pltpu.CompilerParams(dimension_semantics=sem, vmem_limit_bytes=VMEM_LIMIT)


def _sigmoid(x):
    return 1.0 / (1.0 + jnp.exp(-x))


def _silu(x):
    return x * _sigmoid(x)


def _gelu_tanh(x):
    return 0.5 * x * (1.0 + jnp.tanh(0.7978845608028654 * (x + 0.044715 * (x * x * x))))


def _ada_kernel(c_ref, w_ref, b_ref, o_ref):
    a = _silu(c_ref[...])
    o_ref[...] = jnp.dot(a, w_ref[...], precision=HIGHEST, preferred_element_type=F32) + b_ref[...]


def _ada(c8, w, b):
    n = w.shape[1]
    tn = 1024
    return pl.pallas_call(
        _ada_kernel,
        out_shape=jax.ShapeDtypeStruct((8, n), F32),
        grid=(n // tn,),
        in_specs=[pl.BlockSpec((8, D_MODEL), lambda j: (0, 0)),
                  pl.BlockSpec((D_MODEL, tn), lambda j: (0, j)),
                  pl.BlockSpec((1, tn), lambda j: (0, j))],
        out_specs=pl.BlockSpec((8, tn), lambda j: (0, j)),
        compiler_params=_cparams(("arbitrary",)),
        name="ada",
    )(c8, w, b)


def _rope(t, cos, sin_signed, first_half):
    rot = jnp.where(first_half, pltpu.roll(t, 96, 1), pltpu.roll(t, 32, 1))
    return t * cos + rot * sin_signed


def _inproj_kernel(x_ref, sh_ref, sc_ref, g_ref, w_ref, cos_ref, sin_ref,
                   hg_ref, qraw_ref, qrot_ref, kc_ref, vc_ref, ks_ref, kw_ref, va_ref, gate_ref, h_scr):
    tm = x_ref.shape[1]
    x = x_ref[0]
    y = x * lax.rsqrt(jnp.mean(x * x, axis=-1, keepdims=True) + EPS) * g_ref[...]
    h_scr[...] = (y * (1.0 + sc_ref[0]) + sh_ref[0]).astype(BF16)

    def mm(lo, width):
        return jnp.dot(h_scr[...], w_ref[:, lo:lo + width], preferred_element_type=F32)

    cos = cos_ref[...]
    sin = sin_ref[...]
    first_half = (lax.broadcasted_iota(jnp.int32, cos.shape, 1) % NSA_HD) < (NSA_HD // 2)
    for j in range(4):
        hg_ref[0, :, j * HG_WIDTH:(j + 1) * HG_WIDTH] = mm(C_HG + j * HG_WIDTH, HG_WIDTH)
    for n in range(NSA_HEADS):
        q = mm(C_Q + n * LANES, LANES) * (NSA_HD ** -0.5)
        qraw_ref[0, :, n * LANES:(n + 1) * LANES] = q.astype(BF16)
        qrot_ref[0, :, n * LANES:(n + 1) * LANES] = (_rope(q, cos, sin, first_half) * LOG2E).astype(BF16)
    kc_ref[0] = mm(C_K, LANES).astype(BF16)
    vc_ref[0] = mm(C_K + LANES, LANES).astype(BF16)
    ks_ref[0, :, 0:LANES] = _rope(mm(C_K + 2 * LANES, LANES), cos, sin, first_half).astype(BF16)
    lane = lax.broadcasted_iota(jnp.int32, (tm, LANES), 1)
    pos = pl.program_id(1) * tm + lax.broadcasted_iota(jnp.int32, (tm, LANES), 0)
    ks_ref[0, :, LANES:2 * LANES] = jnp.where(lane == pos // SEL_BLOCK, 1.0, 0.0).astype(BF16)
    kw_ref[0] = _rope(mm(C_K + 3 * LANES, LANES), cos, sin, first_half).astype(BF16)
    for j in range(4):
        v = mm(C_V + j * LANES, LANES)
        va_ref[0, :, j * LANES:(j + 1) * LANES] = jnp.where(lane == NSA_HD, 1.0, v).astype(BF16)
    gate_ref[0] = mm(C_GATE, LANES)


def _inproj(x, mod3, norm_g, w_p, cos, sin, tm):
    B, S, D = x.shape
    blk = lambda w: pl.BlockSpec((1, tm, w), lambda b, i: (b, i, 0))
    return pl.pallas_call(
        _inproj_kernel,
        out_shape=(jax.ShapeDtypeStruct((B, S, 4 * HG_WIDTH), F32),
                   jax.ShapeDtypeStruct((B, S, NSA_HEADS * LANES), BF16),
                   jax.ShapeDtypeStruct((B, S, NSA_HEADS * LANES), BF16),
                   jax.ShapeDtypeStruct((B, S, LANES), BF16),
                   jax.ShapeDtypeStruct((B, S, LANES), BF16),
                   jax.ShapeDtypeStruct((B, S, 2 * LANES), BF16),
                   jax.ShapeDtypeStruct((B, S, LANES), BF16),
                   jax.ShapeDtypeStruct((B, S, 4 * LANES), BF16),
                   jax.ShapeDtypeStruct((B, S, LANES), F32)),
        grid=(B, S // tm),
        in_specs=[blk(D),
                  pl.BlockSpec((1, 1, D), lambda b, i: (b, 0, 0)),
                  pl.BlockSpec((1, 1, D), lambda b, i: (b, 0, 1)),
                  pl.BlockSpec((1, D), lambda b, i: (0, 0)),
                  pl.BlockSpec((D, IN_COLS_P), lambda b, i: (0, 0)),
                  pl.BlockSpec((tm, LANES), lambda b, i: (i, 0)),
                  pl.BlockSpec((tm, LANES), lambda b, i: (i, 0))],
        out_specs=(blk(4 * HG_WIDTH), blk(NSA_HEADS * LANES), blk(NSA_HEADS * LANES),
                   blk(LANES), blk(LANES), blk(2 * LANES), blk(LANES), blk(4 * LANES), blk(LANES)),
        scratch_shapes=[pltpu.VMEM((tm, D), BF16)],
        compiler_params=_cparams(("arbitrary", "arbitrary")),
        name="inproj",
    )(x, mod3, mod3, norm_g, w_p, cos, sin)


def _hgrn_kernel(q_ref, f_ref, i_ref, gt_ref, lb_ref, ng_ref, o_ref):
    S = q_ref.shape[1]
    C, U = HG_CHUNK, HG_SUB
    lb = lb_ref[...]
    ng = ng_ref[...]
    ri = lax.broadcasted_iota(jnp.int32, (C, C), 0)
    ci = lax.broadcasted_iota(jnp.int32, (C, C), 1)
    tril = (ri >= ci).astype(F32)
    trow = lax.broadcasted_iota(jnp.int32, (U, 1), 0)

    def chunk(c, st_t):
        r0 = pl.multiple_of(c * C, C)
        rows = pl.ds(r0, C)
        f = lb + (1.0 - lb) * _sigmoid(f_ref[0, rows, :])
        kk = 1.0 - f
        b = jnp.dot(tril, jnp.log(f), precision=HIGHEST, preferred_element_type=F32)
        q = q_ref[0, rows, :] * (HG_DK ** -0.5)
        v = i_ref[0, rows, :]
        vb = v.astype(BF16)
        o_inter = lax.dot_general((q * jnp.exp(b)).astype(BF16), st_t.astype(BF16), NT,
                                  preferred_element_type=F32)
        parts = []
        for i in range(C // U):
            lo = i * U
            bi = b[lo:lo + U]
            qi = q[lo:lo + U]
            if i == 0:
                oi = jnp.zeros((U, HG_DV), F32)
            else:
                r = b[lo - 1:lo]
                qrel = (qi * jnp.exp(bi - r)).astype(BF16)
                kprev = (kk[:lo] * jnp.exp(r - b[:lo])).astype(BF16)
                a_off = lax.dot_general(qrel, kprev, NT, preferred_element_type=F32)
                oi = jnp.dot(a_off.astype(BF16), vb[:lo], preferred_element_type=F32)
            for s in range(U):
                valid = trow >= s
                e = jnp.exp(jnp.where(valid, bi - bi[s:s + 1], 0.0))
                a = jnp.sum(qi * e * kk[lo + s:lo + s + 1], axis=-1, keepdims=True)
                oi = oi + jnp.where(valid, a, 0.0) * v[lo + s:lo + s + 1]
            parts.append(oi)
        o = o_inter + jnp.concatenate(parts, axis=0)
        o = o * lax.rsqrt(jnp.mean(o * o, axis=-1, keepdims=True) + EPS) * ng
        o_ref[0, rows, :] = (o * _silu(gt_ref[0, rows, :])).astype(BF16)
        bl = b[C - 1:C]
        kv_t = lax.dot_general(vb, (kk * jnp.exp(bl - b)).astype(BF16), TN, preferred_element_type=F32)
        return jnp.exp(bl) * st_t + kv_t

    lax.fori_loop(0, S // C, chunk, jnp.zeros((HG_DV, HG_DK), F32))


def _hgrn_mxu_kernel(hg_ref, lb_ref, ng_ref, o_ref, st_scr):
    ts = hg_ref.shape[1]
    C, U = HG_CHUNK, HG_SUB
    ri = lax.broadcasted_iota(jnp.int32, (C, C), 0)
    ci = lax.broadcasted_iota(jnp.int32, (C, C), 1)
    tril = (ri >= ci).astype(F32)

    @pl.when(pl.program_id(1) == 0)
    def _():
        st_scr[...] = jnp.zeros(st_scr.shape, F32)

    W = HG_WIDTH
    NH = HG_HEADS
    head_of_lane = lax.broadcasted_iota(jnp.int32, (1, W), 1) // HG_DK
    hcols = [slice(h * HG_DK, (h + 1) * HG_DK) for h in range(NH)]

    def chunk(c, carry):
        rows = pl.ds(pl.multiple_of(c * C, C), C)
        lb = lb_ref[...]
        f = lb + (1.0 - lb) * _sigmoid(hg_ref[0, rows, W:2 * W])
        kk = 1.0 - f
        b = jnp.dot(tril, jnp.log(f), precision=HIGHEST, preferred_element_type=F32)
        q = hg_ref[0, rows, 0:W] * (HG_DK ** -0.5)
        vb = hg_ref[0, rows, 2 * W:3 * W].astype(BF16)
        qe = (q * jnp.exp(b)).astype(BF16)
        o_inter = jnp.concatenate(
            [lax.dot_general(qe[:, hc], st_scr[h].astype(BF16), NT, preferred_element_type=F32)
             for h, hc in enumerate(hcols)], axis=1)
        parts = []
        for i in range(C // U):
            lo, hi = i * U, (i + 1) * U
            r = b[lo - 1:lo] if i else jnp.zeros((1, W), F32)
            qrel = q[lo:hi] * jnp.exp(b[lo:hi] - r)
            kall = (kk[:hi] * jnp.exp(r - b[:hi])).astype(BF16)
            qbd = jnp.concatenate([jnp.where(head_of_lane == h, qrel, 0.0) for h in range(NH)], axis=0)
            a = lax.dot_general(qbd.astype(BF16), kall, NT, preferred_element_type=F32)
            trow = lax.broadcasted_iota(jnp.int32, (NH * U, hi), 0) % U
            a = jnp.where(lax.broadcasted_iota(jnp.int32, (NH * U, hi), 1) <= lo + trow, a, 0.0)
            oa = jnp.dot(a.astype(BF16), vb[:hi], preferred_element_type=F32)
            oi = jnp.where(head_of_lane == 0, oa[0:U], 0.0)
            for h in range(1, NH):
                oi = jnp.where(head_of_lane == h, oa[h * U:(h + 1) * U], oi)
            parts.append(oi)
        o = o_inter + jnp.concatenate(parts, axis=0)
        o = jnp.concatenate(
            [o[:, hc] * lax.rsqrt(jnp.mean(o[:, hc] * o[:, hc], axis=-1, keepdims=True) + EPS) for hc in hcols], axis=1)
        o_ref[0, rows, :] = (o * ng_ref[...] * _silu(hg_ref[0, rows, 3 * W:4 * W])).astype(BF16)
        bl = b[C - 1:C]
        ke = (kk * jnp.exp(bl - b)).astype(BF16)
        decay = jnp.exp(bl)
        for h, hc in enumerate(hcols):
            kv_t = lax.dot_general(vb[:, hc], ke[:, hc], TN, preferred_element_type=F32)
            st_scr[h] = decay[:, hc] * st_scr[h] + kv_t
        return carry

    lax.fori_loop(0, ts // C, chunk, 0)


def _hgrn_mxu(hg, lb, ng, ts):
    B, S, _ = hg.shape
    vec = pl.BlockSpec((1, HG_WIDTH), lambda b, i: (0, 0))
    return pl.pallas_call(
        _hgrn_mxu_kernel,
        out_shape=jax.ShapeDtypeStruct((B, S, HG_WIDTH), BF16),
        grid=(B, S // ts),
        in_specs=[pl.BlockSpec((1, ts, 4 * HG_WIDTH), lambda b, i: (b, i, 0)), vec, vec],
        out_specs=pl.BlockSpec((1, ts, HG_WIDTH), lambda b, i: (b, i, 0)),
        scratch_shapes=[pltpu.VMEM((HG_HEADS, HG_DV, HG_DK), F32)],
        compiler_params=_cparams(("arbitrary", "arbitrary")),
        name="hgrn_mxu",
    )(hg, lb, ng)


def _hgrn(hg, lb, ng):
    B, S, _ = hg.shape
    col = lambda k: pl.BlockSpec((1, S, HG_DK), lambda b, h, k=k: (b, 0, k * HG_HEADS + h))
    vec = pl.BlockSpec((1, HG_DK), lambda b, h: (0, h))
    return pl.pallas_call(
        _hgrn_kernel,
        out_shape=jax.ShapeDtypeStruct((B, S, HG_WIDTH), BF16),
        grid=(B, HG_HEADS),
        in_specs=[col(0), col(1), col(2), col(3), vec, vec],
        out_specs=pl.BlockSpec((1, S, HG_DV), lambda b, h: (b, 0, h)),
        compiler_params=_cparams(("arbitrary", "arbitrary")),
        name="hgrn",
    )(hg, hg, hg, hg, lb, ng)


def _cmpmlp_kernel(x_ref, w1a_ref, w1b_ref, pe_ref, w1_ref, b1_ref, w2_ref, o_ref):
    x = x_ref[0]
    hb = jnp.dot(pe_ref[...], w1_ref[...], precision=HIGHEST, preferred_element_type=F32)[0:1] + b1_ref[...]
    nrow = x.shape[0]
    for g in range(NSA_KV_GROUPS):
        a = jnp.dot(x, w1a_ref[g], preferred_element_type=F32)
        bm = jnp.dot(x, w1b_ref[g], preferred_element_type=F32)
        hdn = a + pltpu.roll(bm, nrow - 1, 0) + hb
        o_ref[0, g] = jnp.dot(_gelu_tanh(hdn).astype(BF16), w2_ref[g], preferred_element_type=F32).astype(BF16)


def _cmpmlp(x2, w1a, w1b, pe8, w1, b1, w2p):
    B, ncb, width = x2.shape
    full = lambda a: pl.BlockSpec(a.shape, lambda b: (0,) * a.ndim)
    return pl.pallas_call(
        _cmpmlp_kernel,
        out_shape=jax.ShapeDtypeStruct((B, NSA_KV_GROUPS, ncb, LANES), BF16),
        grid=(B,),
        in_specs=[pl.BlockSpec((1, ncb, width), lambda b: (b, 0, 0)),
                  full(w1a), full(w1b), full(pe8), full(w1), full(b1), full(w2p)],
        out_specs=pl.BlockSpec((1, NSA_KV_GROUPS, ncb, LANES), lambda b: (b, 0, 0, 0)),
        compiler_params=_cparams(("arbitrary",)),
        name="cmpmlp",
    )(x2, w1a, w1b, pe8, w1, b1, w2p)


def _rank_before(score, nrows):
    jrow = lax.broadcasted_iota(jnp.int32, score.shape, 0)
    rank = jnp.zeros(score.shape, F32)
    for k in range(nrows):
        rk = score[k:k + 1]
        beats = (rk > score) | ((rk == score) & (jrow > k))
        rank = rank + jnp.where(beats, 1.0, 0.0)
    return rank


def _transpose_via_mxu(xt):
    c = xt.shape[1]
    eye = (lax.broadcasted_iota(jnp.int32, (c, c), 0) == lax.broadcasted_iota(jnp.int32, (c, c), 1)).astype(F32)
    return lax.dot_general(eye, xt, NT, precision=HIGHEST, preferred_element_type=F32)


def _cmpsel_kernel(q_ref, kc_ref, vc_ref, o_ref, sel_ref):
    tq = q_ref.shape[1]
    ncb = kc_ref.shape[2]
    t = pl.program_id(2)
    kc = kc_ref[0, 0]
    vc = vc_ref[0, 0]
    pos = t * tq + lax.broadcasted_iota(jnp.int32, (tq, 1), 0)
    cblk = lax.broadcasted_iota(jnp.int32, (1, ncb), 1)
    vis = (cblk * CMP_STRIDE + CMP_BLOCK - 1) <= pos
    psum = jnp.zeros((tq, ncb), F32)
    for h in range(NSA_HG):
        s = lax.dot_general(q_ref[0, :, h * LANES:(h + 1) * LANES], kc, NT, preferred_element_type=F32)
        s = jnp.where(vis, s, NEG)
        p = jnp.exp(s - jnp.max(s, axis=-1, keepdims=True))
        p = jnp.where(vis, p / jnp.sum(p, axis=-1, keepdims=True), 0.0)
        o_ref[0, :, h * LANES:(h + 1) * LANES] = jnp.dot(p.astype(BF16), vc, preferred_element_type=F32)
        psum = psum + p
    nsb = ncb * CMP_STRIDE // SEL_BLOCK
    jb = lax.broadcasted_iota(jnp.int32, (nsb, ncb), 0) * SEL_BLOCK
    cb = lax.broadcasted_iota(jnp.int32, (nsb, ncb), 1) * CMP_STRIDE
    ov = jnp.maximum(jnp.minimum(cb + CMP_BLOCK, jb + SEL_BLOCK) - jnp.maximum(cb, jb), 0).astype(F32) / CMP_BLOCK
    pslc_t = lax.dot_general(ov, psum, NT, precision=HIGHEST, preferred_element_type=F32)
    posl = t * tq + lax.broadcasted_iota(jnp.int32, (1, tq), 1)
    cur = posl // SEL_BLOCK
    jrow = lax.broadcasted_iota(jnp.int32, (nsb, tq), 0)
    forced = (jrow == 0) | (jrow == cur) | (jrow == cur - 1)
    score = jnp.where(forced, 1e30, jnp.where(jrow <= cur, pslc_t, NEG))
    rank = _rank_before(score, nsb)
    drop = jnp.where((rank < min(N_SEL, nsb)) & (score > -1e29), 0.0, 1.0)
    drop = jnp.concatenate([drop, jnp.zeros((LANES - nsb, tq), F32)], axis=0)
    sel_ref[0, 0] = (_transpose_via_mxu(drop) * NEG).astype(BF16)


def _cmpsel(qraw, kcmp, vcmp, tq):
    B, S, _ = qraw.shape
    ncb = kcmp.shape[2]
    gw = NSA_HG * LANES
    assert S // SEL_BLOCK <= NSA_HD
    cmp_spec = pl.BlockSpec((1, 1, ncb, LANES), lambda b, g, t: (b, g, 0, 0))
    return pl.pallas_call(
        _cmpsel_kernel,
        out_shape=(jax.ShapeDtypeStruct((B, S, NSA_HEADS * LANES), F32),
                   jax.ShapeDtypeStruct((B, NSA_KV_GROUPS, S, LANES), BF16)),
        grid=(B, NSA_KV_GROUPS, S // tq),
        in_specs=[pl.BlockSpec((1, tq, gw), lambda b, g, t: (b, t, g)), cmp_spec, cmp_spec],
        out_specs=(pl.BlockSpec((1, tq, gw), lambda b, g, t: (b, t, g)),
                   pl.BlockSpec((1, 1, tq, LANES), lambda b, g, t: (b, g, t, 0))),
        compiler_params=_cparams(("arbitrary", "arbitrary", "arbitrary")),
        name="cmpsel",
    )(qraw, kcmp, vcmp)


def _slcwin_kernel(q_ref, sel_ref, ks_ref, kw_ref, vs_ref, vw_ref, oslc_ref, owin_ref, *, tk, wk):
    tq = q_ref.shape[1]
    S = ks_ref.shape[1]
    t = pl.program_id(2)
    q0 = t * tq
    qpos = q0 + lax.broadcasted_iota(jnp.int32, (tq, 1), 0)
    qw = [q_ref[0, :, h * LANES:(h + 1) * LANES] for h in range(NSA_HG)]
    sel = sel_ref[0, 0]
    qa = [jnp.concatenate([q, sel], axis=1) for q in qw]

    def normalise(acc):
        return acc * (1.0 / acc[:, NSA_HD:NSA_HD + 1])

    def kstep(j, carry, diag):
        k0 = pl.multiple_of(j * tk, tk)
        ks = ks_ref[0, pl.ds(k0, tk), :]
        vs = vs_ref[0, pl.ds(k0, tk), :]
        if diag:
            vis = (k0 + lax.broadcasted_iota(jnp.int32, (1, tk), 1)) <= qpos
        out = []
        for h in range(NSA_HG):
            m, acc = carry[h]
            s = lax.dot_general(qa[h], ks, NT, preferred_element_type=F32)
            if diag:
                s = jnp.where(vis, s, NEG)
            mn = jnp.maximum(m, jnp.max(s, axis=-1, keepdims=True))
            p = jnp.exp2(s - mn)
            acc = jnp.exp2(m - mn) * acc + jnp.dot(p.astype(BF16), vs, preferred_element_type=F32)
            out.append((mn, acc))
        return tuple(out)

    init = tuple((jnp.full((tq, 1), NEG, F32), jnp.zeros((tq, LANES), F32)) for _ in range(NSA_HG))
    jdiag = q0 // tk
    carry = lax.fori_loop(0, jdiag, lambda j, c: kstep(j, c, False), init)
    carry = kstep(jdiag, carry, True)
    for h in range(NSA_HG):
        oslc_ref[0, :, h * LANES:(h + 1) * LANES] = normalise(carry[h][1])

    nblk = S // tq
    start = pl.multiple_of(jnp.clip(t - WINDOW // tq, 0, nblk - wk // tq) * tq, tq)
    kw = kw_ref[0, pl.ds(start, wk), :]
    vw = vw_ref[0, pl.ds(start, wk), :]
    d = qpos - (start + lax.broadcasted_iota(jnp.int32, (1, wk), 1))
    okw = (d >= 0) & (d < WINDOW)
    for h in range(NSA_HG):
        s = jnp.where(okw, lax.dot_general(qw[h], kw, NT, preferred_element_type=F32), NEG)
        p = jnp.exp2(s - jnp.max(s, axis=-1, keepdims=True))
        owin_ref[0, :, h * LANES:(h + 1) * LANES] = normalise(jnp.dot(p.astype(BF16), vw, preferred_element_type=F32))


def _slcwin(qrot, sel, ks, kw, vaug, tq, tk):
    B, S, _ = qrot.shape
    gw = NSA_HG * LANES
    wk = min(WINDOW + tq, S)
    oshape = jax.ShapeDtypeStruct((B, S, NSA_HEADS * LANES), F32)
    ospec = pl.BlockSpec((1, tq, gw), lambda b, g, t: (b, t, g))
    return pl.pallas_call(
        functools.partial(_slcwin_kernel, tk=tk, wk=wk),
        out_shape=(oshape, oshape),
        grid=(B, NSA_KV_GROUPS, S // tq),
        in_specs=[pl.BlockSpec((1, tq, gw), lambda b, g, t: (b, t, g)),
                  pl.BlockSpec((1, 1, tq, LANES), lambda b, g, t: (b, g, t, 0)),
                  pl.BlockSpec((1, S, 2 * LANES), lambda b, g, t: (b, 0, 0)),
                  pl.BlockSpec((1, S, LANES), lambda b, g, t: (b, 0, 0)),
                  pl.BlockSpec((1, S, LANES), lambda b, g, t: (b, 0, g)),
                  pl.BlockSpec((1, S, LANES), lambda b, g, t: (b, 0, NSA_KV_GROUPS + g))],
        out_specs=(ospec, ospec),
        compiler_params=_cparams(("arbitrary", "arbitrary", "arbitrary")),
        name="slcwin",
    )(qrot, sel, ks, kw, vaug, vaug)


def _outproj_kernel(x_ref, ohg_ref, ocmp_ref, oslc_ref, owin_ref, gate_ref, whg_ref, wnsa_ref,
                    g1_ref, sh2_ref, sc2_ref, n2_ref, rwt_ref, rb_ref,
                    x1_ref, h2_ref, sc_ref, slot_t_ref, w_t_ref, cnt_ref):
    tm = x_ref.shape[1]
    gs = _sigmoid(gate_ref[0])
    mix = jnp.dot(ohg_ref[0], whg_ref[...], preferred_element_type=F32)
    for n in range(NSA_HEADS):
        cols = slice(n * LANES, (n + 1) * LANES)
        o = (gs[:, 3 * n:3 * n + 1] * ocmp_ref[0, :, cols] + gs[:, 3 * n + 1:3 * n + 2] * oslc_ref[0, :, cols]
             + gs[:, 3 * n + 2:3 * n + 3] * owin_ref[0, :, cols])
        mix = mix + jnp.dot(o.astype(BF16), wnsa_ref[cols, :], preferred_element_type=F32)
    x1 = x_ref[0] + g1_ref[0] * mix
    x1_ref[0] = x1
    h2 = (x1 * lax.rsqrt(jnp.mean(x1 * x1, axis=-1, keepdims=True) + EPS) * n2_ref[...]) * (1.0 + sc2_ref[0]) + sh2_ref[0]
    h2_ref[0] = h2.astype(BF16)
    scores = _sigmoid(lax.dot_general(rwt_ref[...], h2, NT, precision=HIGHEST, preferred_element_type=F32))
    choice = scores + rb_ref[...]
    per = N_EXPERTS // N_GROUPS
    c3 = choice.reshape(N_GROUPS, per, tm)
    erow = lax.broadcasted_iota(jnp.int32, c3.shape, 1)
    rank_in = jnp.zeros(c3.shape, F32)
    for k in range(per):
        ck = c3[:, k:k + 1, :]
        rank_in = rank_in + jnp.where((ck > c3) | ((ck == c3) & (erow > k)), 1.0, 0.0)
    grp_score = jnp.sum(jnp.where(rank_in < 2, c3, 0.0), axis=1)
    grp_keep = _rank_before(grp_score, N_GROUPS) < TOPK_GROUPS
    masked = jnp.where(grp_keep[:, None, :], c3, -jnp.inf).reshape(N_EXPERTS, tm)
    keep = _rank_before(masked, N_EXPERTS) < TOP_K
    tw = jnp.where(keep, scores, 0.0)
    tw = tw / jnp.sum(tw, axis=0, keepdims=True) * ROUTED_SCALE
    earlier = (lax.broadcasted_iota(jnp.int32, (tm, tm), 0) < lax.broadcasted_iota(jnp.int32, (tm, tm), 1))
    kept = jnp.where(keep, 1.0, 0.0)
    pref = jnp.dot(kept.astype(BF16), earlier.astype(BF16), preferred_element_type=F32)
    slot = jnp.where(keep, pref, -1.0)
    sc_ref[0] = _transpose_via_mxu(slot)
    slot_t_ref[:, 0] = slot.reshape(N_EXPERTS // MOE_EG, MOE_EG, tm)
    w_t_ref[:, 0] = tw.reshape(N_EXPERTS // MOE_EG, MOE_EG, tm)
    cnt_ref[0] = jnp.broadcast_to(jnp.sum(kept, axis=1, keepdims=True), (N_EXPERTS, LANES))


def _outproj(x, ohg, ocmp, oslc, owin, gate, whg, wnsa, mod3, n2, rwt, rb, tm):
    B, S, D = x.shape
    blk = lambda w: pl.BlockSpec((1, tm, w), lambda b, i: (b, i, 0))
    full = lambda a: pl.BlockSpec(a.shape, lambda b, i: (0,) * a.ndim)
    modc = lambda k: pl.BlockSpec((1, 1, D), lambda b, i, k=k: (b, 0, k))
    nt = S // tm
    ng = N_EXPERTS // MOE_EG
    tspec = pl.BlockSpec((ng, 1, MOE_EG, tm), lambda b, i: (0, b * nt + i, 0, 0))
    return pl.pallas_call(
        _outproj_kernel,
        out_shape=(jax.ShapeDtypeStruct((B, S, D), F32),
                   jax.ShapeDtypeStruct((B, S, D), BF16),
                   jax.ShapeDtypeStruct((B, S, N_EXPERTS), F32),
                   jax.ShapeDtypeStruct((ng, B * nt, MOE_EG, tm), F32),
                   jax.ShapeDtypeStruct((ng, B * nt, MOE_EG, tm), F32),
                   jax.ShapeDtypeStruct((B * nt, N_EXPERTS, LANES), F32)),
        grid=(B, nt),
        in_specs=[blk(D), blk(HG_WIDTH), blk(NSA_HEADS * LANES), blk(NSA_HEADS * LANES), blk(NSA_HEADS * LANES),
                  blk(LANES), full(whg), full(wnsa), modc(2), modc(3), modc(4), full(n2), full(rwt), full(rb)],
        out_specs=(blk(D), blk(D), blk(N_EXPERTS), tspec, tspec,
                   pl.BlockSpec((1, N_EXPERTS, LANES), lambda b, i: (b * nt + i, 0, 0))),
        compiler_params=_cparams(("arbitrary", "arbitrary")),
        name="outproj",
    )(x, ohg, ocmp, oslc, owin, gate, whg, wnsa, mod3, mod3, mod3, n2, rwt, rb)


def _swiglu_hidden(x, wgu):
    gu = jnp.dot(x, wgu, preferred_element_type=F32)
    return _silu(gu[:, :EXPERT_HIDDEN]) * gu[:, EXPERT_HIDDEN:]


def _moe_kernel(rounds_ref, h_ref, slot_t_ref, w_t_ref, slot_c_ref, wgu_ref, wdn_ref, sgu_ref, sdn_ref,
                o_ref, x_scr, y_scr, p_scr):
    i = pl.program_id(0)
    g = pl.program_id(1)
    tm = h_ref.shape[0]
    nsub = tm // MOE_SUB
    rnd, eg = MOE_ROUND, MOE_EG
    ng = N_EXPERTS // eg
    base = i * (N_EXPERTS + ng)

    @pl.when(g == 0)
    def _():
        act = _swiglu_hidden(h_ref[...], sgu_ref[...]).astype(BF16)
        o_ref[...] = jnp.dot(act, sdn_ref[...], preferred_element_type=F32)

    rslot = lax.broadcasted_iota(jnp.int32, (rnd, 1), 0).astype(F32)
    lane_slot = (lax.broadcasted_iota(jnp.int32, (1, eg * rnd), 1) % rnd).astype(F32)
    spread = (lax.broadcasted_iota(jnp.int32, (eg, eg * rnd), 1) // rnd
              == lax.broadcasted_iota(jnp.int32, (eg, eg * rnd), 0)).astype(BF16)

    def one_round(r, carry):
        first = (r * rnd).astype(F32)
        for s in range(nsub):
            st = slot_t_ref[0, s] - first
            p = jnp.concatenate([jnp.where(rslot == st[e:e + 1, :], 1.0, 0.0) for e in range(eg)], axis=0)
            p_scr[s] = p.astype(BF16)
            x_scr[s] = jnp.dot(p_scr[s], h_ref[s * MOE_SUB:(s + 1) * MOE_SUB, :],
                               preferred_element_type=F32).astype(BF16)
        for e in range(eg):
            rows = slice(e * rnd, (e + 1) * rnd)

            @pl.when(r < rounds_ref[base + g * eg + e])
            def _():
                xe = jnp.concatenate([x_scr[s, rows, :] for s in range(nsub)], axis=0)
                wslot = jnp.concatenate(
                    [jnp.sum(p_scr[s, rows, :].astype(F32) * w_t_ref[0, s][e:e + 1, :], axis=-1, keepdims=True)
                     for s in range(nsub)], axis=0)
                y = jnp.dot((_swiglu_hidden(xe, wgu_ref[e]) * wslot).astype(BF16), wdn_ref[e],
                            preferred_element_type=F32)
                for s in range(nsub):
                    y_scr[s, rows, :] = y[s * rnd:(s + 1) * rnd].astype(BF16)

            @pl.when(r >= rounds_ref[base + g * eg + e])
            def _():
                for s in range(nsub):
                    y_scr[s, rows, :] = jnp.zeros((rnd, D_MODEL), BF16)
        for s in range(nsub):
            toks = slice(s * MOE_SUB, (s + 1) * MOE_SUB)
            sc = jnp.dot(slot_c_ref[0, toks, :].astype(BF16), spread, preferred_element_type=F32) - first
            pt = jnp.where(sc == lane_slot, 1.0, 0.0).astype(BF16)
            o_ref[toks, :] += jnp.dot(pt, y_scr[s], preferred_element_type=F32)
        return carry

    lax.fori_loop(0, rounds_ref[base + N_EXPERTS + g], one_round, 0)


def _moe(rounds, h2, slot_t, w_t, slot_c, wgu, wdn, sgu, sdn, tm):
    T, D = h2.shape
    eg, nsub = MOE_EG, tm // MOE_SUB
    full = lambda a: pl.BlockSpec(a.shape, lambda i, e, o: (0,) * a.ndim)
    tspec = pl.BlockSpec((1, nsub, eg, MOE_SUB), lambda i, e, o: (e, i, 0, 0))
    return pl.pallas_call(
        _moe_kernel,
        out_shape=jax.ShapeDtypeStruct((T, D), F32),
        grid_spec=pltpu.PrefetchScalarGridSpec(
            num_scalar_prefetch=1,
            grid=(T // tm, N_EXPERTS // eg),
            in_specs=[pl.BlockSpec((tm, D), lambda i, e, o: (i, 0), pipeline_mode=pl.Buffered(1)), tspec, tspec,
                      pl.BlockSpec((1, tm, eg), lambda i, e, o: (e, i, 0)),
                      pl.BlockSpec((eg, D, 2 * EXPERT_HIDDEN), lambda i, e, o: (e, 0, 0)),
                      pl.BlockSpec((eg, EXPERT_HIDDEN, D), lambda i, e, o: (e, 0, 0)),
                      full(sgu), full(sdn)],
            out_specs=pl.BlockSpec((tm, D), lambda i, e, o: (i, 0), pipeline_mode=pl.Buffered(1)),
            scratch_shapes=[pltpu.VMEM((nsub, eg * MOE_ROUND, D), BF16),
                            pltpu.VMEM((nsub, eg * MOE_ROUND, D), BF16),
                            pltpu.VMEM((nsub, eg * MOE_ROUND, MOE_SUB), BF16)]),
        compiler_params=_cparams(("arbitrary", "arbitrary")),
        name="moe",
    )(rounds, h2, slot_t, w_t, slot_c, wgu, wdn, sgu, sdn)


def _final_kernel(x1_ref, moe_ref, g2_ref, fg_ref, o_ref):
    x2 = x1_ref[0] + g2_ref[0] * moe_ref[0]
    o_ref[0] = x2 * lax.rsqrt(jnp.mean(x2 * x2, axis=-1, keepdims=True) + EPS) * fg_ref[...]


def _final(x1, moe, mod3, fg, tm):
    B, S, D = x1.shape
    blk = pl.BlockSpec((1, tm, D), lambda b, i: (b, i, 0))
    return pl.pallas_call(
        _final_kernel,
        out_shape=jax.ShapeDtypeStruct((B, S, D), F32),
        grid=(B, S // tm),
        in_specs=[blk, blk, pl.BlockSpec((1, 1, D), lambda b, i: (b, 0, 5)), pl.BlockSpec((1, D), lambda b, i: (0, 0))],
        out_specs=blk,
        compiler_params=_cparams(("arbitrary", "arbitrary")),
        name="final",
    )(x1, moe, mod3, fg)


def _pack_w_in(w_in):
    d = w_in.shape[0]
    q0 = 4 * HG_WIDTH
    kv0 = q0 + NSA_WIDTH
    g0 = kv0 + 6 * NSA_KV
    qcols = []
    for n in range(NSA_HEADS):
        g = n // NSA_HG
        head = w_in[:, q0 + n * NSA_HD:q0 + (n + 1) * NSA_HD]
        z = jnp.zeros((d, NSA_HD), w_in.dtype)
        qcols += [head, z] if g == 0 else [z, head]
    gate = jnp.pad(w_in[:, g0:], ((0, 0), (0, LANES - 3 * NSA_HEADS)))
    kv = lambda i: w_in[:, kv0 + i * NSA_KV:kv0 + (i + 1) * NSA_KV]
    z = jnp.zeros((d, NSA_HD), w_in.dtype)
    vcols = []
    for i in (3, 5):
        for g in range(NSA_KV_GROUPS):
            vcols += [kv(i)[:, g * NSA_HD:(g + 1) * NSA_HD], z]
    return jnp.concatenate([w_in[:, :q0]] + qcols + [kv(0), kv(1), kv(2), kv(4)] + vcols + [gate],
                           axis=1).astype(BF16)


def _pack_w_out(w_out):
    rows = []
    for n in range(NSA_HEADS):
        head = w_out[HG_WIDTH + n * NSA_HD:HG_WIDTH + (n + 1) * NSA_HD]
        rows += [head, jnp.zeros((LANES - NSA_HD, w_out.shape[1]), w_out.dtype)]
    return w_out[:HG_WIDTH].astype(BF16), jnp.concatenate(rows, axis=0).astype(BF16)


def _pack_cmp(pos, w1, b1, w2, lane_by_group):
    half = CMP_STRIDE * NSA_HD
    def rows_for(wh):
        w3 = wh.reshape(CMP_STRIDE, NSA_HD, CMP_HIDDEN)
        z = jnp.zeros_like(w3)
        return jnp.stack([jnp.concatenate([w3, z], axis=1).reshape(CMP_STRIDE * LANES, CMP_HIDDEN),
                          jnp.concatenate([z, w3], axis=1).reshape(CMP_STRIDE * LANES, CMP_HIDDEN)])
    w1a = rows_for(w1[:half]).astype(BF16)
    w1b = rows_for(w1[half:]).astype(BF16)
    z2 = jnp.zeros_like(w2)
    w2_first = jnp.concatenate([w2, z2], axis=1)
    w2p = jnp.stack([w2_first, jnp.concatenate([z2, w2], axis=1) if lane_by_group else w2_first]).astype(BF16)
    pe8 = jnp.pad(pos.reshape(1, CMP_BLOCK * NSA_HD), ((0, 7), (0, 0)))
    return w1a, w1b, pe8, w1, b1.reshape(1, CMP_HIDDEN), w2p


def _rope_tables(S):
    half = NSA_HD // 2
    inv = ROPE_THETA ** (-jnp.arange(half, dtype=F32) / half)
    ang = jnp.arange(S, dtype=F32)[:, None] * inv[None, :]
    cos, sin = jnp.cos(ang), jnp.sin(ang)
    reps = LANES // NSA_HD
    return jnp.tile(jnp.concatenate([cos, cos], axis=1), (1, reps)), jnp.tile(jnp.concatenate([-sin, sin], axis=1), (1, reps))


def _tiles(S):
    return dict(inproj=min(512, S), hgrn=min(512, S), cmpsel=min(256, S), slc_q=512, slc_k=512, moe=min(2048, S), final=min(512, S))


def kernel(x, c, w_ada, b_ada, norm1_g, w_in, hg_lb_logits, hg_norm_g, cmp_pos_k, cmp_w1_k, cmp_b1_k, cmp_w2_k,
           cmp_pos_v, cmp_w1_v, cmp_b1_v, cmp_w2_v, w_out, norm2_g, router_w, router_bias, w_exp_gu, w_exp_dn,
           w_sh_gu, w_sh_dn, final_g):
    B, S, D = x.shape
    assert D == D_MODEL and w_ada.shape[0] == 1 and S % 512 == 0
    tl = _tiles(S)
    l = 0
    lb = jnp.cumsum(jax.nn.softmax(hg_lb_logits.astype(F32), axis=0), axis=0)[l].reshape(1, HG_WIDTH)
    c8 = jnp.pad(c, ((0, 8 - B), (0, 0)))
    mod3 = _ada(c8, w_ada[l], b_ada[l].reshape(1, -1))[:B].reshape(B, 1, 6 * D)
    cos, sin = _rope_tables(S)
    hg, qraw, qrot, kc, vc, ks, kw, vaug, gate = _inproj(x, mod3, norm1_g[l].reshape(1, D), _pack_w_in(w_in[l]),
                                                         cos, sin, tl["inproj"])
    ng = hg_norm_g[l].reshape(1, HG_WIDTH)
    mxu_safe = -HG_SUB * jnp.log(jnp.min(lb)) <= HG_SAFE_LOG_RANGE
    ohg = lax.cond(mxu_safe, lambda: _hgrn_mxu(hg, lb, ng, tl["hgrn"]), lambda: _hgrn(hg, lb, ng))
    ncb = S // CMP_STRIDE
    kcmp = _cmpmlp(kc.reshape(B, ncb, CMP_STRIDE * LANES),
                   *_pack_cmp(cmp_pos_k[l], cmp_w1_k[l], cmp_b1_k[l], cmp_w2_k[l], True))
    vcmp = _cmpmlp(vc.reshape(B, ncb, CMP_STRIDE * LANES),
                   *_pack_cmp(cmp_pos_v[l], cmp_w1_v[l], cmp_b1_v[l], cmp_w2_v[l], False))
    ocmp, sel = _cmpsel(qraw, kcmp, vcmp, tl["cmpsel"])
    oslc, owin = _slcwin(qrot, sel, ks, kw, vaug, tl["slc_q"], tl["slc_k"])
    whg, wnsa = _pack_w_out(w_out[l])
    x1, h2, sc, slot_t, w_t, cnt = _outproj(x, ohg, ocmp, oslc, owin, gate, whg, wnsa, mod3,
                                            norm2_g[l].reshape(1, D), router_w[l].T,
                                            router_bias[l].reshape(N_EXPERTS, 1), MOE_SUB)
    T = B * S
    tm = tl["moe"]
    ng = N_EXPERTS // MOE_EG
    slot_c = sc.reshape(T, ng, MOE_EG).transpose(1, 0, 2)
    load = jnp.max(cnt[:, :, 0].reshape(T // tm, tm // MOE_SUB, N_EXPERTS), axis=1).astype(jnp.int32)
    per_expert = (load + MOE_ROUND - 1) // MOE_ROUND
    rounds = jnp.concatenate([per_expert, jnp.max(per_expert.reshape(T // tm, ng, MOE_EG), axis=2)], axis=1).reshape(-1)
    moe = _moe(rounds, h2.reshape(T, D), slot_t, w_t, slot_c,
               w_exp_gu[l].astype(BF16), w_exp_dn[l].astype(BF16), w_sh_gu[l].astype(BF16), w_sh_dn[l].astype(BF16), tm)
    return _final(x1, moe.reshape(B, S, D), mod3, final_g.reshape(1, D), tl["final"])
```

```python
import functools

import numpy as np
import jax
import jax.numpy as jnp
from jax import lax
from jax.experimental import pallas as pl
from jax.experimental.pallas import tpu as pltpu

F32 = jnp.float32
BF16 = jnp.bfloat16
HIGHEST = lax.Precision.HIGHEST

D_MODEL = 1024
EPS = 1e-6
HG_HEADS = 4
HG_DK = 128
HG_DV = 128
HG_WIDTH = HG_HEADS * HG_DV
HG_CHUNK = 64
HG_SUB = 16
HG_SAFE_LOG_RANGE = 80.0
NSA_HEADS = 8
NSA_KV_GROUPS = 2
NSA_HG = NSA_HEADS // NSA_KV_GROUPS
NSA_HD = 64
NSA_WIDTH = NSA_HEADS * NSA_HD
NSA_KV = NSA_KV_GROUPS * NSA_HD
CMP_BLOCK = 32
CMP_STRIDE = 16
CMP_HIDDEN = 256
SEL_BLOCK = 64
N_SEL = 16
WINDOW = 512
ROPE_THETA = 10000.0
N_EXPERTS = 64
TOP_K = 8
N_GROUPS = 8
TOPK_GROUPS = 4
EXPERT_HIDDEN = 256
ROUTED_SCALE = 2.5
ROUTE_TM = 512
MOE_SUB = 256
MOE_ROUND = 32
MOE_EG = 8

LANES = 128
NEG = -1e30
VMEM_LIMIT = 56 * 1024 * 1024

C_HG = 0
C_Q = 4 * HG_WIDTH
C_K = C_Q + NSA_HEADS * LANES
C_V = C_K + 4 * LANES
C_GATE = C_V + 4 * LANES
IN_COLS_P = C_GATE + LANES
LOG2E = 1.4426950408889634

NT = (((1,), (1,)), ((), ()))
TN = (((0,), (0,)), ((), ()))


def _cparams(sem):
    return pltpu.CompilerParams(dimension_semantics=sem, vmem_limit_bytes=VMEM_LIMIT)


def _sigmoid(x):
    return 1.0 / (1.0 + jnp.exp(-x))


def _silu(x):
    return x * _sigmoid(x)


def _gelu_tanh(x):
    return 0.5 * x * (1.0 + jnp.tanh(0.7978845608028654 * (x + 0.044715 * (x * x * x))))


def _ada_kernel(c_ref, w_ref, b_ref, o_ref):
    a = _silu(c_ref[...])
    o_ref[...] = jnp.dot(a, w_ref[...], precision=HIGHEST, preferred_element_type=F32) + b_ref[...]


def _ada(c8, w, b):
    n = w.shape[1]
    tn = 1024
    return pl.pallas_call(
        _ada_kernel,
        out_shape=jax.ShapeDtypeStruct((8, n), F32),
        grid=(n // tn,),
        in_specs=[pl.BlockSpec((8, D_MODEL), lambda j: (0, 0)),
                  pl.BlockSpec((D_MODEL, tn), lambda j: (0, j)),
                  pl.BlockSpec((1, tn), lambda j: (0, j))],
        out_specs=pl.BlockSpec((8, tn), lambda j: (0, j)),
        compiler_params=_cparams(("arbitrary",)),
        name="ada",
    )(c8, w, b)


def _rope(t, cos, sin_signed, first_half):
    rot = jnp.where(first_half, pltpu.roll(t, 96, 1), pltpu.roll(t, 32, 1))
    return t * cos + rot * sin_signed


def _inproj_kernel(x_ref, sh_ref, sc_ref, g_ref, w_ref, cos_ref, sin_ref,
                   hg_ref, qraw_ref, qrot_ref, kc_ref, vc_ref, ks_ref, kw_ref, va_ref, gate_ref, h_scr):
    tm = x_ref.shape[1]
    x = x_ref[0]
    y = x * lax.rsqrt(jnp.mean(x * x, axis=-1, keepdims=True) + EPS) * g_ref[...]
    h_scr[...] = (y * (1.0 + sc_ref[0]) + sh_ref[0]).astype(BF16)

    def mm(lo, width):
        return jnp.dot(h_scr[...], w_ref[:, lo:lo + width], preferred_element_type=F32)

    cos = cos_ref[...]
    sin = sin_ref[...]
    first_half = (lax.broadcasted_iota(jnp.int32, cos.shape, 1) % NSA_HD) < (NSA_HD // 2)
    for j in range(4):
        hg_ref[0, :, j * HG_WIDTH:(j + 1) * HG_WIDTH] = mm(C_HG + j * HG_WIDTH, HG_WIDTH)
    for n in range(NSA_HEADS):
        q = mm(C_Q + n * LANES, LANES) * (NSA_HD ** -0.5)
        qraw_ref[0, :, n * LANES:(n + 1) * LANES] = q.astype(BF16)
        qrot_ref[0, :, n * LANES:(n + 1) * LANES] = (_rope(q, cos, sin, first_half) * LOG2E).astype(BF16)
    kc_ref[0] = mm(C_K, LANES).astype(BF16)
    vc_ref[0] = mm(C_K + LANES, LANES).astype(BF16)
    ks_ref[0, :, 0:LANES] = _rope(mm(C_K + 2 * LANES, LANES), cos, sin, first_half).astype(BF16)
    lane = lax.broadcasted_iota(jnp.int32, (tm, LANES), 1)
    pos = pl.program_id(1) * tm + lax.broadcasted_iota(jnp.int32, (tm, LANES), 0)
    ks_ref[0, :, LANES:2 * LANES] = jnp.where(lane == pos // SEL_BLOCK, 1.0, 0.0).astype(BF16)
    kw_ref[0] = _rope(mm(C_K + 3 * LANES, LANES), cos, sin, first_half).astype(BF16)
    for j in range(4):
        v = mm(C_V + j * LANES, LANES)
        va_ref[0, :, j * LANES:(j + 1) * LANES] = jnp.where(lane == NSA_HD, 1.0, v).astype(BF16)
    gate_ref[0] = mm(C_GATE, LANES)


def _inproj(x, mod3, norm_g, w_p, cos, sin, tm):
    B, S, D = x.shape
    blk = lambda w: pl.BlockSpec((1, tm, w), lambda b, i: (b, i, 0))
    return pl.pallas_call(
        _inproj_kernel,
        out_shape=(jax.ShapeDtypeStruct((B, S, 4 * HG_WIDTH), F32),
                   jax.ShapeDtypeStruct((B, S, NSA_HEADS * LANES), BF16),
                   jax.ShapeDtypeStruct((B, S, NSA_HEADS * LANES), BF16),
                   jax.ShapeDtypeStruct((B, S, LANES), BF16),
                   jax.ShapeDtypeStruct((B, S, LANES), BF16),
                   jax.ShapeDtypeStruct((B, S, 2 * LANES), BF16),
                   jax.ShapeDtypeStruct((B, S, LANES), BF16),
                   jax.ShapeDtypeStruct((B, S, 4 * LANES), BF16),
                   jax.ShapeDtypeStruct((B, S, LANES), F32)),
        grid=(B, S // tm),
        in_specs=[blk(D),
                  pl.BlockSpec((1, 1, D), lambda b, i: (b, 0, 0)),
                  pl.BlockSpec((1, 1, D), lambda b, i: (b, 0, 1)),
                  pl.BlockSpec((1, D), lambda b, i: (0, 0)),
                  pl.BlockSpec((D, IN_COLS_P), lambda b, i: (0, 0)),
                  pl.BlockSpec((tm, LANES), lambda b, i: (i, 0)),
                  pl.BlockSpec((tm, LANES), lambda b, i: (i, 0))],
        out_specs=(blk(4 * HG_WIDTH), blk(NSA_HEADS * LANES), blk(NSA_HEADS * LANES),
                   blk(LANES), blk(LANES), blk(2 * LANES), blk(LANES), blk(4 * LANES), blk(LANES)),
        scratch_shapes=[pltpu.VMEM((tm, D), BF16)],
        compiler_params=_cparams(("arbitrary", "arbitrary")),
        name="inproj",
    )(x, mod3, mod3, norm_g, w_p, cos, sin)


def _hgrn_kernel(q_ref, f_ref, i_ref, gt_ref, lb_ref, ng_ref, o_ref):
    S = q_ref.shape[1]
    C, U = HG_CHUNK, HG_SUB
    lb = lb_ref[...]
    ng = ng_ref[...]
    ri = lax.broadcasted_iota(jnp.int32, (C, C), 0)
    ci = lax.broadcasted_iota(jnp.int32, (C, C), 1)
    tril = (ri >= ci).astype(F32)
    trow = lax.broadcasted_iota(jnp.int32, (U, 1), 0)

    def chunk(c, st_t):
        r0 = pl.multiple_of(c * C, C)
        rows = pl.ds(r0, C)
        f = lb + (1.0 - lb) * _sigmoid(f_ref[0, rows, :])
        kk = 1.0 - f
        b = jnp.dot(tril, jnp.log(f), precision=HIGHEST, preferred_element_type=F32)
        q = q_ref[0, rows, :] * (HG_DK ** -0.5)
        v = i_ref[0, rows, :]
        vb = v.astype(BF16)
        o_inter = lax.dot_general((q * jnp.exp(b)).astype(BF16), st_t.astype(BF16), NT,
                                  preferred_element_type=F32)
        parts = []
        for i in range(C // U):
            lo = i * U
            bi = b[lo:lo + U]
            qi = q[lo:lo + U]
            if i == 0:
                oi = jnp.zeros((U, HG_DV), F32)
            else:
                r = b[lo - 1:lo]
                qrel = (qi * jnp.exp(bi - r)).astype(BF16)
                kprev = (kk[:lo] * jnp.exp(r - b[:lo])).astype(BF16)
                a_off = lax.dot_general(qrel, kprev, NT, preferred_element_type=F32)
                oi = jnp.dot(a_off.astype(BF16), vb[:lo], preferred_element_type=F32)
            for s in range(U):
                valid = trow >= s
                e = jnp.exp(jnp.where(valid, bi - bi[s:s + 1], 0.0))
                a = jnp.sum(qi * e * kk[lo + s:lo + s + 1], axis=-1, keepdims=True)
                oi = oi + jnp.where(valid, a, 0.0) * v[lo + s:lo + s + 1]
            parts.append(oi)
        o = o_inter + jnp.concatenate(parts, axis=0)
        o = o * lax.rsqrt(jnp.mean(o * o, axis=-1, keepdims=True) + EPS) * ng
        o_ref[0, rows, :] = (o * _silu(gt_ref[0, rows, :])).astype(BF16)
        bl = b[C - 1:C]
        kv_t = lax.dot_general(vb, (kk * jnp.exp(bl - b)).astype(BF16), TN, preferred_element_type=F32)
        return jnp.exp(bl) * st_t + kv_t

    lax.fori_loop(0, S // C, chunk, jnp.zeros((HG_DV, HG_DK), F32))


def _hgrn_mxu_kernel(hg_ref, lb_ref, ng_ref, o_ref, st_scr):
    ts = hg_ref.shape[1]
    C, U = HG_CHUNK, HG_SUB
    ri = lax.broadcasted_iota(jnp.int32, (C, C), 0)
    ci = lax.broadcasted_iota(jnp.int32, (C, C), 1)
    tril = (ri >= ci).astype(F32)

    @pl.when(pl.program_id(1) == 0)
    def _():
        st_scr[...] = jnp.zeros(st_scr.shape, F32)

    W = HG_WIDTH
    NH = HG_HEADS
    head_of_lane = lax.broadcasted_iota(jnp.int32, (1, W), 1) // HG_DK
    hcols = [slice(h * HG_DK, (h + 1) * HG_DK) for h in range(NH)]

    def chunk(c, carry):
        rows = pl.ds(pl.multiple_of(c * C, C), C)
        lb = lb_ref[...]
        f = lb + (1.0 - lb) * _sigmoid(hg_ref[0, rows, W:2 * W])
        kk = 1.0 - f
        b = jnp.dot(tril, jnp.log(f), precision=HIGHEST, preferred_element_type=F32)
        q = hg_ref[0, rows, 0:W] * (HG_DK ** -0.5)
        vb = hg_ref[0, rows, 2 * W:3 * W].astype(BF16)
        qe = (q * jnp.exp(b)).astype(BF16)
        o_inter = jnp.concatenate(
            [lax.dot_general(qe[:, hc], st_scr[h].astype(BF16), NT, preferred_element_type=F32)
             for h, hc in enumerate(hcols)], axis=1)
        parts = []
        for i in range(C // U):
            lo, hi = i * U, (i + 1) * U
            r = b[lo - 1:lo] if i else jnp.zeros((1, W), F32)
            qrel = q[lo:hi] * jnp.exp(b[lo:hi] - r)
            kall = (kk[:hi] * jnp.exp(r - b[:hi])).astype(BF16)
            qbd = jnp.concatenate([jnp.where(head_of_lane == h, qrel, 0.0) for h in range(NH)], axis=0)
            a = lax.dot_general(qbd.astype(BF16), kall, NT, preferred_element_type=F32)
            trow = lax.broadcasted_iota(jnp.int32, (NH * U, hi), 0) % U
            a = jnp.where(lax.broadcasted_iota(jnp.int32, (NH * U, hi), 1) <= lo + trow, a, 0.0)
            oa = jnp.dot(a.astype(BF16), vb[:hi], preferred_element_type=F32)
            oi = jnp.where(head_of_lane == 0, oa[0:U], 0.0)
            for h in range(1, NH):
                oi = jnp.where(head_of_lane == h, oa[h * U:(h + 1) * U], oi)
            parts.append(oi)
        o = o_inter + jnp.concatenate(parts, axis=0)
        o = jnp.concatenate(
            [o[:, hc] * lax.rsqrt(jnp.mean(o[:, hc] * o[:, hc], axis=-1, keepdims=True) + EPS) for hc in hcols], axis=1)
        o_ref[0, rows, :] = (o * ng_ref[...] * _silu(hg_ref[0, rows, 3 * W:4 * W])).astype(BF16)
        bl = b[C - 1:C]
        ke = (kk * jnp.exp(bl - b)).astype(BF16)
        decay = jnp.exp(bl)
        for h, hc in enumerate(hcols):
            kv_t = lax.dot_general(vb[:, hc], ke[:, hc], TN, preferred_element_type=F32)
            st_scr[h] = decay[:, hc] * st_scr[h] + kv_t
        return carry

    lax.fori_loop(0, ts // C, chunk, 0, unroll=4)


def _hgrn_mxu(hg, lb, ng, ts):
    B, S, _ = hg.shape
    vec = pl.BlockSpec((1, HG_WIDTH), lambda b, i: (0, 0))
    return pl.pallas_call(
        _hgrn_mxu_kernel,
        out_shape=jax.ShapeDtypeStruct((B, S, HG_WIDTH), BF16),
        grid=(B, S // ts),
        in_specs=[pl.BlockSpec((1, ts, 4 * HG_WIDTH), lambda b, i: (b, i, 0)), vec, vec],
        out_specs=pl.BlockSpec((1, ts, HG_WIDTH), lambda b, i: (b, i, 0)),
        scratch_shapes=[pltpu.VMEM((HG_HEADS, HG_DV, HG_DK), F32)],
        compiler_params=_cparams(("arbitrary", "arbitrary")),
        name="hgrn_mxu",
    )(hg, lb, ng)


def _hgrn(hg, lb, ng):
    B, S, _ = hg.shape
    col = lambda k: pl.BlockSpec((1, S, HG_DK), lambda b, h, k=k: (b, 0, k * HG_HEADS + h))
    vec = pl.BlockSpec((1, HG_DK), lambda b, h: (0, h))
    return pl.pallas_call(
        _hgrn_kernel,
        out_shape=jax.ShapeDtypeStruct((B, S, HG_WIDTH), BF16),
        grid=(B, HG_HEADS),
        in_specs=[col(0), col(1), col(2), col(3), vec, vec],
        out_specs=pl.BlockSpec((1, S, HG_DV), lambda b, h: (b, 0, h)),
        compiler_params=_cparams(("arbitrary", "arbitrary")),
        name="hgrn",
    )(hg, hg, hg, hg, lb, ng)


def _cmpmlp_kernel(x_ref, w1a_ref, w1b_ref, pe_ref, w1_ref, b1_ref, w2_ref, o_ref):
    x = x_ref[0]
    hb = jnp.dot(pe_ref[...], w1_ref[...], precision=HIGHEST, preferred_element_type=F32)[0:1] + b1_ref[...]
    nrow = x.shape[0]
    for g in range(NSA_KV_GROUPS):
        a = jnp.dot(x, w1a_ref[g], preferred_element_type=F32)
        bm = jnp.dot(x, w1b_ref[g], preferred_element_type=F32)
        hdn = a + pltpu.roll(bm, nrow - 1, 0) + hb
        o_ref[0, g] = jnp.dot(_gelu_tanh(hdn).astype(BF16), w2_ref[g], preferred_element_type=F32).astype(BF16)


def _cmpmlp(x2, w1a, w1b, pe8, w1, b1, w2p):
    B, ncb, width = x2.shape
    full = lambda a: pl.BlockSpec(a.shape, lambda b: (0,) * a.ndim)
    return pl.pallas_call(
        _cmpmlp_kernel,
        out_shape=jax.ShapeDtypeStruct((B, NSA_KV_GROUPS, ncb, LANES), BF16),
        grid=(B,),
        in_specs=[pl.BlockSpec((1, ncb, width), lambda b: (b, 0, 0)),
                  full(w1a), full(w1b), full(pe8), full(w1), full(b1), full(w2p)],
        out_specs=pl.BlockSpec((1, NSA_KV_GROUPS, ncb, LANES), lambda b: (b, 0, 0, 0)),
        compiler_params=_cparams(("arbitrary",)),
        name="cmpmlp",
    )(x2, w1a, w1b, pe8, w1, b1, w2p)


def _rank_before(score, nrows):
    jrow = lax.broadcasted_iota(jnp.int32, score.shape, 0)
    rank = jnp.zeros(score.shape, F32)
    for k in range(nrows):
        rk = score[k:k + 1]
        beats = (rk > score) | ((rk == score) & (jrow > k))
        rank = rank + jnp.where(beats, 1.0, 0.0)
    return rank


def _transpose_via_mxu(xt):
    c = xt.shape[1]
    eye = (lax.broadcasted_iota(jnp.int32, (c, c), 0) == lax.broadcasted_iota(jnp.int32, (c, c), 1)).astype(F32)
    return lax.dot_general(eye, xt, NT, precision=HIGHEST, preferred_element_type=F32)


def _cmpsel_kernel(q_ref, kc_ref, vc_ref, o_ref, sel_ref):
    tq = q_ref.shape[1]
    ncb = kc_ref.shape[2]
    t = pl.program_id(2)
    kc = kc_ref[0, 0]
    vc = vc_ref[0, 0]
    pos = t * tq + lax.broadcasted_iota(jnp.int32, (tq, 1), 0)
    cblk = lax.broadcasted_iota(jnp.int32, (1, ncb), 1)
    vis = (cblk * CMP_STRIDE + CMP_BLOCK - 1) <= pos
    psum = jnp.zeros((tq, ncb), F32)
    for h in range(NSA_HG):
        s = lax.dot_general(q_ref[0, :, h * LANES:(h + 1) * LANES], kc, NT, preferred_element_type=F32)
        s = jnp.where(vis, s, NEG)
        p = jnp.exp(s - jnp.max(s, axis=-1, keepdims=True))
        p = jnp.where(vis, p / jnp.sum(p, axis=-1, keepdims=True), 0.0)
        o_ref[0, :, h * LANES:(h + 1) * LANES] = jnp.dot(p.astype(BF16), vc, preferred_element_type=F32)
        psum = psum + p
    nsb = ncb * CMP_STRIDE // SEL_BLOCK
    jb = lax.broadcasted_iota(jnp.int32, (nsb, ncb), 0) * SEL_BLOCK
    cb = lax.broadcasted_iota(jnp.int32, (nsb, ncb), 1) * CMP_STRIDE
    ov = jnp.maximum(jnp.minimum(cb + CMP_BLOCK, jb + SEL_BLOCK) - jnp.maximum(cb, jb), 0).astype(F32) / CMP_BLOCK
    pslc_t = lax.dot_general(ov, psum, NT, precision=HIGHEST, preferred_element_type=F32)
    posl = t * tq + lax.broadcasted_iota(jnp.int32, (1, tq), 1)
    cur = posl // SEL_BLOCK
    jrow = lax.broadcasted_iota(jnp.int32, (nsb, tq), 0)
    forced = (jrow == 0) | (jrow == cur) | (jrow == cur - 1)
    score = jnp.where(forced, 1e30, jnp.where(jrow <= cur, pslc_t, NEG))
    rank = _rank_before(score, nsb)
    drop = jnp.where((rank < min(N_SEL, nsb)) & (score > -1e29), 0.0, 1.0)
    drop = jnp.concatenate([drop, jnp.zeros((LANES - nsb, tq), F32)], axis=0)
    sel_ref[0, 0] = (_transpose_via_mxu(drop) * NEG).astype(BF16)


def _cmpsel(qraw, kcmp, vcmp, tq):
    B, S, _ = qraw.shape
    ncb = kcmp.shape[2]
    gw = NSA_HG * LANES
    assert S // SEL_BLOCK <= NSA_HD
    cmp_spec = pl.BlockSpec((1, 1, ncb, LANES), lambda b, g, t: (b, g, 0, 0))
    return pl.pallas_call(
        _cmpsel_kernel,
        out_shape=(jax.ShapeDtypeStruct((B, S, NSA_HEADS * LANES), F32),
                   jax.ShapeDtypeStruct((B, NSA_KV_GROUPS, S, LANES), BF16)),
        grid=(B, NSA_KV_GROUPS, S // tq),
        in_specs=[pl.BlockSpec((1, tq, gw), lambda b, g, t: (b, t, g)), cmp_spec, cmp_spec],
        out_specs=(pl.BlockSpec((1, tq, gw), lambda b, g, t: (b, t, g)),
                   pl.BlockSpec((1, 1, tq, LANES), lambda b, g, t: (b, g, t, 0))),
        compiler_params=_cparams(("arbitrary", "arbitrary", "arbitrary")),
        name="cmpsel",
    )(qraw, kcmp, vcmp)


def _slcwin_kernel(q_ref, sel_ref, ks_ref, kw_ref, vs_ref, vw_ref, oslc_ref, owin_ref, *, tk, wk):
    tq = q_ref.shape[1]
    S = ks_ref.shape[1]
    t = pl.program_id(2)
    q0 = t * tq
    qpos = q0 + lax.broadcasted_iota(jnp.int32, (tq, 1), 0)
    qw = [q_ref[0, :, h * LANES:(h + 1) * LANES] for h in range(NSA_HG)]
    sel = sel_ref[0, 0]
    qa = [jnp.concatenate([q, sel], axis=1) for q in qw]

    def normalise(acc):
        return acc * (1.0 / acc[:, NSA_HD:NSA_HD + 1])

    def kstep(j, carry, diag):
        k0 = pl.multiple_of(j * tk, tk)
        ks = ks_ref[0, pl.ds(k0, tk), :]
        vs = vs_ref[0, pl.ds(k0, tk), :]
        if diag:
            vis = (k0 + lax.broadcasted_iota(jnp.int32, (1, tk), 1)) <= qpos
        out = []
        for h in range(NSA_HG):
            m, acc = carry[h]
            s = lax.dot_general(qa[h], ks, NT, preferred_element_type=F32)
            if diag:
                s = jnp.where(vis, s, NEG)
            mn = jnp.maximum(m, jnp.max(s, axis=-1, keepdims=True))
            p = jnp.exp2(s - mn)
            acc = jnp.exp2(m - mn) * acc + jnp.dot(p.astype(BF16), vs, preferred_element_type=F32)
            out.append((mn, acc))
        return tuple(out)

    init = tuple((jnp.full((tq, 1), NEG, F32), jnp.zeros((tq, LANES), F32)) for _ in range(NSA_HG))
    jdiag = q0 // tk
    carry = lax.fori_loop(0, jdiag, lambda j, c: kstep(j, c, False), init)
    carry = kstep(jdiag, carry, True)
    for h in range(NSA_HG):
        oslc_ref[0, :, h * LANES:(h + 1) * LANES] = normalise(carry[h][1])

    nblk = S // tq
    start = pl.multiple_of(jnp.clip(t - WINDOW // tq, 0, nblk - wk // tq) * tq, tq)
    kw = kw_ref[0, pl.ds(start, wk), :]
    vw = vw_ref[0, pl.ds(start, wk), :]
    d = qpos - (start + lax.broadcasted_iota(jnp.int32, (1, wk), 1))
    okw = (d >= 0) & (d < WINDOW)
    for h in range(NSA_HG):
        s = jnp.where(okw, lax.dot_general(qw[h], kw, NT, preferred_element_type=F32), NEG)
        p = jnp.exp2(s - jnp.max(s, axis=-1, keepdims=True))
        owin_ref[0, :, h * LANES:(h + 1) * LANES] = normalise(jnp.dot(p.astype(BF16), vw, preferred_element_type=F32))


def _slcwin(qrot, sel, ks, kw, vaug, tq, tk):
    B, S, _ = qrot.shape
    gw = NSA_HG * LANES
    wk = min(WINDOW + tq, S)
    oshape = jax.ShapeDtypeStruct((B, S, NSA_HEADS * LANES), F32)
    ospec = pl.BlockSpec((1, tq, gw), lambda b, g, t: (b, t, g))
    return pl.pallas_call(
        functools.partial(_slcwin_kernel, tk=tk, wk=wk),
        out_shape=(oshape, oshape),
        grid=(B, NSA_KV_GROUPS, S // tq),
        in_specs=[pl.BlockSpec((1, tq, gw), lambda b, g, t: (b, t, g)),
                  pl.BlockSpec((1, 1, tq, LANES), lambda b, g, t: (b, g, t, 0)),
                  pl.BlockSpec((1, S, 2 * LANES), lambda b, g, t: (b, 0, 0)),
                  pl.BlockSpec((1, S, LANES), lambda b, g, t: (b, 0, 0)),
                  pl.BlockSpec((1, S, LANES), lambda b, g, t: (b, 0, g)),
                  pl.BlockSpec((1, S, LANES), lambda b, g, t: (b, 0, NSA_KV_GROUPS + g))],
        out_specs=(ospec, ospec),
        compiler_params=_cparams(("arbitrary", "arbitrary", "arbitrary")),
        name="slcwin",
    )(qrot, sel, ks, kw, vaug, vaug)


def _outproj_kernel(x_ref, ohg_ref, ocmp_ref, oslc_ref, owin_ref, gate_ref, whg_ref, wnsa_ref,
                    g1_ref, sh2_ref, sc2_ref, n2_ref, rwt_ref, rb_ref,
                    x1_ref, h2_ref, sc_ref, slot_t_ref, w_t_ref, cnt_ref):
    tm = x_ref.shape[1]
    gs = _sigmoid(gate_ref[0])
    mix = jnp.dot(ohg_ref[0], whg_ref[...], preferred_element_type=F32)
    for n in range(NSA_HEADS):
        cols = slice(n * LANES, (n + 1) * LANES)
        o = (gs[:, 3 * n:3 * n + 1] * ocmp_ref[0, :, cols] + gs[:, 3 * n + 1:3 * n + 2] * oslc_ref[0, :, cols]
             + gs[:, 3 * n + 2:3 * n + 3] * owin_ref[0, :, cols])
        mix = mix + jnp.dot(o.astype(BF16), wnsa_ref[cols, :], preferred_element_type=F32)
    x1 = x_ref[0] + g1_ref[0] * mix
    x1_ref[0] = x1
    h2 = (x1 * lax.rsqrt(jnp.mean(x1 * x1, axis=-1, keepdims=True) + EPS) * n2_ref[...]) * (1.0 + sc2_ref[0]) + sh2_ref[0]
    h2_ref[0] = h2.astype(BF16)
    scores = _sigmoid(lax.dot_general(rwt_ref[...], h2, NT, precision=HIGHEST, preferred_element_type=F32))
    choice = scores + rb_ref[...]
    per = N_EXPERTS // N_GROUPS
    c3 = choice.reshape(N_GROUPS, per, tm)
    erow = lax.broadcasted_iota(jnp.int32, c3.shape, 1)
    rank_in = jnp.zeros(c3.shape, F32)
    for k in range(per):
        ck = c3[:, k:k + 1, :]
        rank_in = rank_in + jnp.where((ck > c3) | ((ck == c3) & (erow > k)), 1.0, 0.0)
    grp_score = jnp.sum(jnp.where(rank_in < 2, c3, 0.0), axis=1)
    grp_keep = _rank_before(grp_score, N_GROUPS) < TOPK_GROUPS
    masked = jnp.where(grp_keep[:, None, :], c3, -jnp.inf).reshape(N_EXPERTS, tm)
    keep = _rank_before(masked, N_EXPERTS) < TOP_K
    tw = jnp.where(keep, scores, 0.0)
    tw = tw / jnp.sum(tw, axis=0, keepdims=True) * ROUTED_SCALE
    before = lax.broadcasted_iota(jnp.int32, (tm, tm), 0)
    after = lax.broadcasted_iota(jnp.int32, (tm, tm), 1)
    earlier = (before < after) & (before // MOE_SUB == after // MOE_SUB)
    kept = jnp.where(keep, 1.0, 0.0)
    pref = jnp.dot(kept.astype(BF16), earlier.astype(BF16), preferred_element_type=F32)
    slot = jnp.where(keep, pref, -1.0)
    sc_ref[0] = _transpose_via_mxu(slot)
    for j in range(tm // MOE_SUB):
        sub = slice(j * MOE_SUB, (j + 1) * MOE_SUB)
        slot_t_ref[:, j] = slot[:, sub].reshape(N_EXPERTS // MOE_EG, MOE_EG, MOE_SUB)
        w_t_ref[:, j] = tw[:, sub].reshape(N_EXPERTS // MOE_EG, MOE_EG, MOE_SUB)
        cnt_ref[j] = jnp.broadcast_to(jnp.sum(kept[:, sub], axis=1, keepdims=True), (N_EXPERTS, LANES))


def _outproj(x, ohg, ocmp, oslc, owin, gate, whg, wnsa, mod3, n2, rwt, rb, tm):
    B, S, D = x.shape
    blk = lambda w: pl.BlockSpec((1, tm, w), lambda b, i: (b, i, 0))
    full = lambda a: pl.BlockSpec(a.shape, lambda b, i: (0,) * a.ndim)
    modc = lambda k: pl.BlockSpec((1, 1, D), lambda b, i, k=k: (b, 0, k))
    nt = S // tm
    ng = N_EXPERTS // MOE_EG
    nh = tm // MOE_SUB
    tspec = pl.BlockSpec((ng, nh, MOE_EG, MOE_SUB), lambda b, i: (0, b * nt + i, 0, 0))
    return pl.pallas_call(
        _outproj_kernel,
        out_shape=(jax.ShapeDtypeStruct((B, S, D), F32),
                   jax.ShapeDtypeStruct((B, S, D), BF16),
                   jax.ShapeDtypeStruct((B, S, N_EXPERTS), F32),
                   jax.ShapeDtypeStruct((ng, B * nt * nh, MOE_EG, MOE_SUB), F32),
                   jax.ShapeDtypeStruct((ng, B * nt * nh, MOE_EG, MOE_SUB), F32),
                   jax.ShapeDtypeStruct((B * nt * nh, N_EXPERTS, LANES), F32)),
        grid=(B, nt),
        in_specs=[blk(D), blk(HG_WIDTH), blk(NSA_HEADS * LANES), blk(NSA_HEADS * LANES), blk(NSA_HEADS * LANES),
                  blk(LANES), full(whg), full(wnsa), modc(2), modc(3), modc(4), full(n2), full(rwt), full(rb)],
        out_specs=(blk(D), blk(D), blk(N_EXPERTS), tspec, tspec,
                   pl.BlockSpec((nh, N_EXPERTS, LANES), lambda b, i: (b * nt + i, 0, 0))),
        compiler_params=_cparams(("arbitrary", "arbitrary")),
        name="outproj",
    )(x, ohg, ocmp, oslc, owin, gate, whg, wnsa, mod3, mod3, mod3, n2, rwt, rb)


def _swiglu_hidden(x, wgu):
    gu = jnp.dot(x, wgu, preferred_element_type=F32)
    return _silu(gu[:, :EXPERT_HIDDEN]) * gu[:, EXPERT_HIDDEN:]


def _moe_kernel(rounds_ref, h_ref, slot_t_ref, w_t_ref, slot_c_ref, wgu_ref, wdn_ref, sgu_ref, sdn_ref,
                o_ref, x_scr, y_scr, p_scr):
    i = pl.program_id(0)
    g = pl.program_id(1)
    tm = h_ref.shape[0]
    nsub = tm // MOE_SUB
    rnd, eg = MOE_ROUND, MOE_EG
    ng = N_EXPERTS // eg
    base = i * (N_EXPERTS + ng)

    @pl.when(g == 0)
    def _():
        act = _swiglu_hidden(h_ref[...], sgu_ref[...]).astype(BF16)
        o_ref[...] = jnp.dot(act, sdn_ref[...], preferred_element_type=F32)

    rslot = lax.broadcasted_iota(jnp.int32, (rnd, 1), 0).astype(F32)
    lane_slot = (lax.broadcasted_iota(jnp.int32, (1, eg * rnd), 1) % rnd).astype(F32)
    spread = (lax.broadcasted_iota(jnp.int32, (eg, eg * rnd), 1) // rnd
              == lax.broadcasted_iota(jnp.int32, (eg, eg * rnd), 0)).astype(BF16)

    def one_round(r, carry):
        first = (r * rnd).astype(F32)
        for s in range(nsub):
            st = slot_t_ref[0, s] - first
            p = jnp.concatenate([jnp.where(rslot == st[e:e + 1, :], 1.0, 0.0) for e in range(eg)], axis=0)
            p_scr[s] = p.astype(BF16)
            x_scr[s] = jnp.dot(p_scr[s], h_ref[s * MOE_SUB:(s + 1) * MOE_SUB, :],
                               preferred_element_type=F32).astype(BF16)
        for e in range(eg):
            rows = slice(e * rnd, (e + 1) * rnd)

            @pl.when(r < rounds_ref[base + g * eg + e])
            def _():
                xe = jnp.concatenate([x_scr[s, rows, :] for s in range(nsub)], axis=0)
                wslot = jnp.concatenate(
                    [jnp.sum(p_scr[s, rows, :].astype(F32) * w_t_ref[0, s][e:e + 1, :], axis=-1, keepdims=True)
                     for s in range(nsub)], axis=0)
                y = jnp.dot((_swiglu_hidden(xe, wgu_ref[e]) * wslot).astype(BF16), wdn_ref[e],
                            preferred_element_type=F32)
                for s in range(nsub):
                    y_scr[s, rows, :] = y[s * rnd:(s + 1) * rnd].astype(BF16)

            @pl.when(r >= rounds_ref[base + g * eg + e])
            def _():
                for s in range(nsub):
                    y_scr[s, rows, :] = jnp.zeros((rnd, D_MODEL), BF16)
        for s in range(nsub):
            toks = slice(s * MOE_SUB, (s + 1) * MOE_SUB)
            sc = jnp.dot(slot_c_ref[0, toks, :].astype(BF16), spread, preferred_element_type=F32) - first
            pt = jnp.where(sc == lane_slot, 1.0, 0.0).astype(BF16)
            o_ref[toks, :] += jnp.dot(pt, y_scr[s], preferred_element_type=F32)
        return carry

    lax.fori_loop(0, rounds_ref[base + N_EXPERTS + g], one_round, 0)


def _moe(rounds, h2, slot_t, w_t, slot_c, wgu, wdn, sgu, sdn, tm):
    T, D = h2.shape
    eg, nsub = MOE_EG, tm // MOE_SUB
    full = lambda a: pl.BlockSpec(a.shape, lambda i, e, o: (0,) * a.ndim)
    tspec = pl.BlockSpec((1, nsub, eg, MOE_SUB), lambda i, e, o: (e, i, 0, 0))
    return pl.pallas_call(
        _moe_kernel,
        out_shape=jax.ShapeDtypeStruct((T, D), F32),
        grid_spec=pltpu.PrefetchScalarGridSpec(
            num_scalar_prefetch=1,
            grid=(T // tm, N_EXPERTS // eg),
            in_specs=[pl.BlockSpec((tm, D), lambda i, e, o: (i, 0), pipeline_mode=pl.Buffered(1)), tspec, tspec,
                      pl.BlockSpec((1, tm, eg), lambda i, e, o: (e, i, 0)),
                      pl.BlockSpec((eg, D, 2 * EXPERT_HIDDEN), lambda i, e, o: (e, 0, 0)),
                      pl.BlockSpec((eg, EXPERT_HIDDEN, D), lambda i, e, o: (e, 0, 0)),
                      full(sgu), full(sdn)],
            out_specs=pl.BlockSpec((tm, D), lambda i, e, o: (i, 0), pipeline_mode=pl.Buffered(1)),
            scratch_shapes=[pltpu.VMEM((nsub, eg * MOE_ROUND, D), BF16),
                            pltpu.VMEM((nsub, eg * MOE_ROUND, D), BF16),
                            pltpu.VMEM((nsub, eg * MOE_ROUND, MOE_SUB), BF16)]),
        compiler_params=_cparams(("arbitrary", "arbitrary")),
        name="moe",
    )(rounds, h2, slot_t, w_t, slot_c, wgu, wdn, sgu, sdn)


def _final_kernel(x1_ref, moe_ref, g2_ref, fg_ref, o_ref):
    x2 = x1_ref[0] + g2_ref[0] * moe_ref[0]
    o_ref[0] = x2 * lax.rsqrt(jnp.mean(x2 * x2, axis=-1, keepdims=True) + EPS) * fg_ref[...]


def _final(x1, moe, mod3, fg, tm):
    B, S, D = x1.shape
    blk = pl.BlockSpec((1, tm, D), lambda b, i: (b, i, 0))
    return pl.pallas_call(
        _final_kernel,
        out_shape=jax.ShapeDtypeStruct((B, S, D), F32),
        grid=(B, S // tm),
        in_specs=[blk, blk, pl.BlockSpec((1, 1, D), lambda b, i: (b, 0, 5)), pl.BlockSpec((1, D), lambda b, i: (0, 0))],
        out_specs=blk,
        compiler_params=_cparams(("arbitrary", "arbitrary")),
        name="final",
    )(x1, moe, mod3, fg)


def _pack_w_in(w_in):
    d = w_in.shape[0]
    q0 = 4 * HG_WIDTH
    kv0 = q0 + NSA_WIDTH
    g0 = kv0 + 6 * NSA_KV
    qcols = []
    for n in range(NSA_HEADS):
        g = n // NSA_HG
        head = w_in[:, q0 + n * NSA_HD:q0 + (n + 1) * NSA_HD]
        z = jnp.zeros((d, NSA_HD), w_in.dtype)
        qcols += [head, z] if g == 0 else [z, head]
    gate = jnp.pad(w_in[:, g0:], ((0, 0), (0, LANES - 3 * NSA_HEADS)))
    kv = lambda i: w_in[:, kv0 + i * NSA_KV:kv0 + (i + 1) * NSA_KV]
    z = jnp.zeros((d, NSA_HD), w_in.dtype)
    vcols = []
    for i in (3, 5):
        for g in range(NSA_KV_GROUPS):
            vcols += [kv(i)[:, g * NSA_HD:(g + 1) * NSA_HD], z]
    return jnp.concatenate([w_in[:, :q0]] + qcols + [kv(0), kv(1), kv(2), kv(4)] + vcols + [gate],
                           axis=1).astype(BF16)


def _pack_w_out(w_out):
    rows = []
    for n in range(NSA_HEADS):
        head = w_out[HG_WIDTH + n * NSA_HD:HG_WIDTH + (n + 1) * NSA_HD]
        rows += [head, jnp.zeros((LANES - NSA_HD, w_out.shape[1]), w_out.dtype)]
    return w_out[:HG_WIDTH].astype(BF16), jnp.concatenate(rows, axis=0).astype(BF16)


def _pack_cmp(pos, w1, b1, w2, lane_by_group):
    half = CMP_STRIDE * NSA_HD
    def rows_for(wh):
        w3 = wh.reshape(CMP_STRIDE, NSA_HD, CMP_HIDDEN)
        z = jnp.zeros_like(w3)
        return jnp.stack([jnp.concatenate([w3, z], axis=1).reshape(CMP_STRIDE * LANES, CMP_HIDDEN),
                          jnp.concatenate([z, w3], axis=1).reshape(CMP_STRIDE * LANES, CMP_HIDDEN)])
    w1a = rows_for(w1[:half]).astype(BF16)
    w1b = rows_for(w1[half:]).astype(BF16)
    z2 = jnp.zeros_like(w2)
    w2_first = jnp.concatenate([w2, z2], axis=1)
    w2p = jnp.stack([w2_first, jnp.concatenate([z2, w2], axis=1) if lane_by_group else w2_first]).astype(BF16)
    pe8 = jnp.pad(pos.reshape(1, CMP_BLOCK * NSA_HD), ((0, 7), (0, 0)))
    return w1a, w1b, pe8, w1, b1.reshape(1, CMP_HIDDEN), w2p


def _rope_tables(S):
    half = NSA_HD // 2
    inv = ROPE_THETA ** (-jnp.arange(half, dtype=F32) / half)
    ang = jnp.arange(S, dtype=F32)[:, None] * inv[None, :]
    cos, sin = jnp.cos(ang), jnp.sin(ang)
    reps = LANES // NSA_HD
    return jnp.tile(jnp.concatenate([cos, cos], axis=1), (1, reps)), jnp.tile(jnp.concatenate([-sin, sin], axis=1), (1, reps))


def _tiles(S):
    return dict(inproj=min(512, S), hgrn=min(512, S), cmpsel=min(512, S), slc_q=512, slc_k=512, moe=min(2048, S), final=min(512, S))


def kernel(x, c, w_ada, b_ada, norm1_g, w_in, hg_lb_logits, hg_norm_g, cmp_pos_k, cmp_w1_k, cmp_b1_k, cmp_w2_k,
           cmp_pos_v, cmp_w1_v, cmp_b1_v, cmp_w2_v, w_out, norm2_g, router_w, router_bias, w_exp_gu, w_exp_dn,
           w_sh_gu, w_sh_dn, final_g):
    B, S, D = x.shape
    assert D == D_MODEL and w_ada.shape[0] == 1 and S % 512 == 0
    tl = _tiles(S)
    l = 0
    lb = jnp.cumsum(jax.nn.softmax(hg_lb_logits.astype(F32), axis=0), axis=0)[l].reshape(1, HG_WIDTH)
    c8 = jnp.pad(c, ((0, 8 - B), (0, 0)))
    mod3 = _ada(c8, w_ada[l], b_ada[l].reshape(1, -1))[:B].reshape(B, 1, 6 * D)
    cos, sin = _rope_tables(S)
    hg, qraw, qrot, kc, vc, ks, kw, vaug, gate = _inproj(x, mod3, norm1_g[l].reshape(1, D), _pack_w_in(w_in[l]),
                                                         cos, sin, tl["inproj"])
    ng = hg_norm_g[l].reshape(1, HG_WIDTH)
    mxu_safe = -HG_SUB * jnp.log(jnp.min(lb)) <= HG_SAFE_LOG_RANGE
    ohg = lax.cond(mxu_safe, lambda: _hgrn_mxu(hg, lb, ng, tl["hgrn"]), lambda: _hgrn(hg, lb, ng))
    ncb = S // CMP_STRIDE
    kcmp = _cmpmlp(kc.reshape(B, ncb, CMP_STRIDE * LANES),
                   *_pack_cmp(cmp_pos_k[l], cmp_w1_k[l], cmp_b1_k[l], cmp_w2_k[l], True))
    vcmp = _cmpmlp(vc.reshape(B, ncb, CMP_STRIDE * LANES),
                   *_pack_cmp(cmp_pos_v[l], cmp_w1_v[l], cmp_b1_v[l], cmp_w2_v[l], False))
    ocmp, sel = _cmpsel(qraw, kcmp, vcmp, tl["cmpsel"])
    oslc, owin = _slcwin(qrot, sel, ks, kw, vaug, tl["slc_q"], tl["slc_k"])
    whg, wnsa = _pack_w_out(w_out[l])
    x1, h2, sc, slot_t, w_t, cnt = _outproj(x, ohg, ocmp, oslc, owin, gate, whg, wnsa, mod3,
                                            norm2_g[l].reshape(1, D), router_w[l].T,
                                            router_bias[l].reshape(N_EXPERTS, 1), ROUTE_TM)
    T = B * S
    tm = tl["moe"]
    ng = N_EXPERTS // MOE_EG
    slot_c = sc.reshape(T, ng, MOE_EG).transpose(1, 0, 2)
    load = jnp.max(cnt[:, :, 0].reshape(T // tm, tm // MOE_SUB, N_EXPERTS), axis=1).astype(jnp.int32)
    per_expert = (load + MOE_ROUND - 1) // MOE_ROUND
    rounds = jnp.concatenate([per_expert, jnp.max(per_expert.reshape(T // tm, ng, MOE_EG), axis=2)], axis=1).reshape(-1)
    moe = _moe(rounds, h2.reshape(T, D), slot_t, w_t, slot_c,
               w_exp_gu[l].astype(BF16), w_exp_dn[l].astype(BF16), w_sh_gu[l].astype(BF16), w_sh_dn[l].astype(BF16), tm)
    return _final(x1, moe.reshape(B, S, D), mod3, final_g.reshape(1, D), tl["final"])
```

```python
import functools

import numpy as np
import jax
import jax.numpy as jnp
from jax import lax
from jax.experimental import pallas as pl
from jax.experimental.pallas import tpu as pltpu

F32 = jnp.float32
BF16 = jnp.bfloat16
HIGHEST = lax.Precision.HIGHEST

D_MODEL = 1024
EPS = 1e-6
HG_HEADS = 4
HG_DK = 128
HG_DV = 128
HG_WIDTH = HG_HEADS * HG_DV
HG_CHUNK = 64
HG_SUB = 16
HG_SAFE_LOG_RANGE = 80.0
NSA_HEADS = 8
NSA_KV_GROUPS = 2
NSA_HG = NSA_HEADS // NSA_KV_GROUPS
NSA_HD = 64
NSA_WIDTH = NSA_HEADS * NSA_HD
NSA_KV = NSA_KV_GROUPS * NSA_HD
CMP_BLOCK = 32
CMP_STRIDE = 16
CMP_HIDDEN = 256
SEL_BLOCK = 64
N_SEL = 16
WINDOW = 512
ROPE_THETA = 10000.0
N_EXPERTS = 64
TOP_K = 8
N_GROUPS = 8
TOPK_GROUPS = 4
EXPERT_HIDDEN = 256
ROUTED_SCALE = 2.5
ROUTE_TM = 512
MOE_SUB = 256
MOE_ROUND = 32
MOE_EG = 8

LANES = 128
NEG = -1e30
VMEM_LIMIT = 56 * 1024 * 1024

C_HG = 0
C_Q = 4 * HG_WIDTH
C_K = C_Q + NSA_WIDTH
C_GATE = C_K + 6 * NSA_KV
IN_COLS_P = C_GATE + LANES
LOG2E = 1.4426950408889634

NT = (((1,), (1,)), ((), ()))
TN = (((0,), (0,)), ((), ()))


def _cparams(sem):
    return pltpu.CompilerParams(dimension_semantics=sem, vmem_limit_bytes=VMEM_LIMIT)


def _sigmoid(x):
    return 1.0 / (1.0 + jnp.exp(-x))


def _silu(x):
    return x * _sigmoid(x)


def _gelu_tanh(x):
    return 0.5 * x * (1.0 + jnp.tanh(0.7978845608028654 * (x + 0.044715 * (x * x * x))))


def _ada_kernel(c_ref, w_ref, b_ref, o_ref):
    a = _silu(c_ref[...])
    o_ref[...] = jnp.dot(a, w_ref[...], precision=HIGHEST, preferred_element_type=F32) + b_ref[...]


def _ada(c8, w, b):
    n = w.shape[1]
    tn = 1024
    return pl.pallas_call(
        _ada_kernel,
        out_shape=jax.ShapeDtypeStruct((8, n), F32),
        grid=(n // tn,),
        in_specs=[pl.BlockSpec((8, D_MODEL), lambda j: (0, 0)),
                  pl.BlockSpec((D_MODEL, tn), lambda j: (0, j)),
                  pl.BlockSpec((1, tn), lambda j: (0, j))],
        out_specs=pl.BlockSpec((8, tn), lambda j: (0, j)),
        compiler_params=_cparams(("arbitrary",)),
        name="ada",
    )(c8, w, b)


def _rope(t, cos, sin_signed, first_half):
    rot = jnp.where(first_half, pltpu.roll(t, 96, 1), pltpu.roll(t, 32, 1))
    return t * cos + rot * sin_signed


def _inproj_kernel(x_ref, sh_ref, sc_ref, g_ref, w_ref, cos_ref, sin_ref,
                   hg_ref, qraw_ref, qrot_ref, kc_ref, vc_ref, ks_ref, kw_ref, va_ref, gate_ref, h_scr):
    tm = x_ref.shape[1]
    x = x_ref[0]
    y = x * lax.rsqrt(jnp.mean(x * x, axis=-1, keepdims=True) + EPS) * g_ref[...]
    h_scr[...] = (y * (1.0 + sc_ref[0]) + sh_ref[0]).astype(BF16)

    def mm(lo, width):
        return jnp.dot(h_scr[...], w_ref[:, lo:lo + width], preferred_element_type=F32)

    cos = cos_ref[...]
    sin = sin_ref[...]
    first_half = (lax.broadcasted_iota(jnp.int32, cos.shape, 1) % NSA_HD) < (NSA_HD // 2)
    lane = lax.broadcasted_iota(jnp.int32, (tm, LANES), 1)
    half_of_lane = lane // NSA_HD

    def own_lanes(t, src_half, dst_half):
        moved = t if src_half == dst_half else pltpu.roll(t, NSA_HD, 1)
        return jnp.where(half_of_lane == dst_half, moved, 0.0)

    for j in range(4):
        hg_ref[0, :, j * HG_WIDTH:(j + 1) * HG_WIDTH] = mm(C_HG + j * HG_WIDTH, HG_WIDTH)
    for m in range(NSA_WIDTH // (2 * LANES)):
        qq = mm(C_Q + m * 2 * LANES, 2 * LANES) * (NSA_HD ** -0.5)
        for c in range(2):
            q = qq[:, c * LANES:(c + 1) * LANES]
            qr = _rope(q, cos, sin, first_half) * LOG2E
            for half in range(2):
                n = (2 * m + c) * 2 + half
                g = n // NSA_HG
                qraw_ref[0, :, n * LANES:(n + 1) * LANES] = own_lanes(q, half, g).astype(BF16)
                qrot_ref[0, :, n * LANES:(n + 1) * LANES] = own_lanes(qr, half, g).astype(BF16)
    kv = mm(C_K, 2 * LANES)
    kc_ref[0] = kv[:, :LANES].astype(BF16)
    vc_ref[0] = kv[:, LANES:].astype(BF16)
    pos = pl.program_id(1) * tm + lax.broadcasted_iota(jnp.int32, (tm, LANES), 0)
    ks_ref[0, :, LANES:2 * LANES] = jnp.where(lane == pos // SEL_BLOCK, 1.0, 0.0).astype(BF16)
    for j in range(2):
        kv = mm(C_K + (j + 1) * 2 * LANES, 2 * LANES)
        k_rot = _rope(kv[:, :LANES], cos, sin, first_half).astype(BF16)
        if j == 0:
            ks_ref[0, :, 0:LANES] = k_rot
        else:
            kw_ref[0] = k_rot
        for g in range(NSA_KV_GROUPS):
            v = own_lanes(kv[:, LANES:], g, 0)
            col = (j * NSA_KV_GROUPS + g) * LANES
            va_ref[0, :, col:col + LANES] = jnp.where(lane == NSA_HD, 1.0, v).astype(BF16)
    gate_ref[0] = mm(C_GATE, LANES)


def _inproj(x, mod3, norm_g, w_p, cos, sin, tm):
    B, S, D = x.shape
    blk = lambda w: pl.BlockSpec((1, tm, w), lambda b, i: (b, i, 0))
    return pl.pallas_call(
        _inproj_kernel,
        out_shape=(jax.ShapeDtypeStruct((B, S, 4 * HG_WIDTH), F32),
                   jax.ShapeDtypeStruct((B, S, NSA_HEADS * LANES), BF16),
                   jax.ShapeDtypeStruct((B, S, NSA_HEADS * LANES), BF16),
                   jax.ShapeDtypeStruct((B, S, LANES), BF16),
                   jax.ShapeDtypeStruct((B, S, LANES), BF16),
                   jax.ShapeDtypeStruct((B, S, 2 * LANES), BF16),
                   jax.ShapeDtypeStruct((B, S, LANES), BF16),
                   jax.ShapeDtypeStruct((B, S, 4 * LANES), BF16),
                   jax.ShapeDtypeStruct((B, S, LANES), F32)),
        grid=(B, S // tm),
        in_specs=[blk(D),
                  pl.BlockSpec((1, 1, D), lambda b, i: (b, 0, 0)),
                  pl.BlockSpec((1, 1, D), lambda b, i: (b, 0, 1)),
                  pl.BlockSpec((1, D), lambda b, i: (0, 0)),
                  pl.BlockSpec((D, IN_COLS_P), lambda b, i: (0, 0)),
                  pl.BlockSpec((tm, LANES), lambda b, i: (i, 0)),
                  pl.BlockSpec((tm, LANES), lambda b, i: (i, 0))],
        out_specs=(blk(4 * HG_WIDTH), blk(NSA_HEADS * LANES), blk(NSA_HEADS * LANES),
                   blk(LANES), blk(LANES), blk(2 * LANES), blk(LANES), blk(4 * LANES), blk(LANES)),
        scratch_shapes=[pltpu.VMEM((tm, D), BF16)],
        compiler_params=_cparams(("arbitrary", "arbitrary")),
        name="inproj",
    )(x, mod3, mod3, norm_g, w_p, cos, sin)


def _hgrn_kernel(q_ref, f_ref, i_ref, gt_ref, lb_ref, ng_ref, o_ref):
    S = q_ref.shape[1]
    C, U = HG_CHUNK, HG_SUB
    lb = lb_ref[...]
    ng = ng_ref[...]
    ri = lax.broadcasted_iota(jnp.int32, (C, C), 0)
    ci = lax.broadcasted_iota(jnp.int32, (C, C), 1)
    tril = (ri >= ci).astype(F32)
    trow = lax.broadcasted_iota(jnp.int32, (U, 1), 0)

    def chunk(c, st_t):
        r0 = pl.multiple_of(c * C, C)
        rows = pl.ds(r0, C)
        f = lb + (1.0 - lb) * _sigmoid(f_ref[0, rows, :])
        kk = 1.0 - f
        b = jnp.dot(tril, jnp.log(f), precision=HIGHEST, preferred_element_type=F32)
        q = q_ref[0, rows, :] * (HG_DK ** -0.5)
        v = i_ref[0, rows, :]
        vb = v.astype(BF16)
        o_inter = lax.dot_general((q * jnp.exp(b)).astype(BF16), st_t.astype(BF16), NT,
                                  preferred_element_type=F32)
        parts = []
        for i in range(C // U):
            lo = i * U
            bi = b[lo:lo + U]
            qi = q[lo:lo + U]
            if i == 0:
                oi = jnp.zeros((U, HG_DV), F32)
            else:
                r = b[lo - 1:lo]
                qrel = (qi * jnp.exp(bi - r)).astype(BF16)
                kprev = (kk[:lo] * jnp.exp(r - b[:lo])).astype(BF16)
                a_off = lax.dot_general(qrel, kprev, NT, preferred_element_type=F32)
                oi = jnp.dot(a_off.astype(BF16), vb[:lo], preferred_element_type=F32)
            for s in range(U):
                valid = trow >= s
                e = jnp.exp(jnp.where(valid, bi - bi[s:s + 1], 0.0))
                a = jnp.sum(qi * e * kk[lo + s:lo + s + 1], axis=-1, keepdims=True)
                oi = oi + jnp.where(valid, a, 0.0) * v[lo + s:lo + s + 1]
            parts.append(oi)
        o = o_inter + jnp.concatenate(parts, axis=0)
        o = o * lax.rsqrt(jnp.mean(o * o, axis=-1, keepdims=True) + EPS) * ng
        o_ref[0, rows, :] = (o * _silu(gt_ref[0, rows, :])).astype(BF16)
        bl = b[C - 1:C]
        kv_t = lax.dot_general(vb, (kk * jnp.exp(bl - b)).astype(BF16), TN, preferred_element_type=F32)
        return jnp.exp(bl) * st_t + kv_t

    lax.fori_loop(0, S // C, chunk, jnp.zeros((HG_DV, HG_DK), F32))


def _hgrn_mxu_kernel(hg_ref, lb_ref, ng_ref, o_ref, st_scr):
    ts = hg_ref.shape[1]
    C, U = HG_CHUNK, HG_SUB
    ri = lax.broadcasted_iota(jnp.int32, (C, C), 0)
    ci = lax.broadcasted_iota(jnp.int32, (C, C), 1)
    tril = (ri >= ci).astype(F32)

    @pl.when(pl.program_id(1) == 0)
    def _():
        st_scr[...] = jnp.zeros(st_scr.shape, F32)

    W = HG_WIDTH
    NH = HG_HEADS
    head_of_lane = lax.broadcasted_iota(jnp.int32, (1, W), 1) // HG_DK
    hcols = [slice(h * HG_DK, (h + 1) * HG_DK) for h in range(NH)]

    def chunk(c, carry):
        rows = pl.ds(pl.multiple_of(c * C, C), C)
        lb = lb_ref[...]
        f = lb + (1.0 - lb) * _sigmoid(hg_ref[0, rows, W:2 * W])
        kk = 1.0 - f
        b = jnp.dot(tril, jnp.log(f), precision=HIGHEST, preferred_element_type=F32)
        q = hg_ref[0, rows, 0:W] * (HG_DK ** -0.5)
        vb = hg_ref[0, rows, 2 * W:3 * W].astype(BF16)
        qe = (q * jnp.exp(b)).astype(BF16)
        o_inter = jnp.concatenate(
            [lax.dot_general(qe[:, hc], st_scr[h].astype(BF16), NT, preferred_element_type=F32)
             for h, hc in enumerate(hcols)], axis=1)
        parts = []
        for i in range(C // U):
            lo, hi = i * U, (i + 1) * U
            r = b[lo - 1:lo] if i else jnp.zeros((1, W), F32)
            qrel = q[lo:hi] * jnp.exp(b[lo:hi] - r)
            kall = (kk[:hi] * jnp.exp(r - b[:hi])).astype(BF16)
            qbd = jnp.concatenate([jnp.where(head_of_lane == h, qrel, 0.0) for h in range(NH)], axis=0)
            a = lax.dot_general(qbd.astype(BF16), kall, NT, preferred_element_type=F32)
            trow = lax.broadcasted_iota(jnp.int32, (NH * U, hi), 0) % U
            a = jnp.where(lax.broadcasted_iota(jnp.int32, (NH * U, hi), 1) <= lo + trow, a, 0.0)
            oa = jnp.dot(a.astype(BF16), vb[:hi], preferred_element_type=F32)
            oi = jnp.where(head_of_lane == 0, oa[0:U], 0.0)
            for h in range(1, NH):
                oi = jnp.where(head_of_lane == h, oa[h * U:(h + 1) * U], oi)
            parts.append(oi)
        o = o_inter + jnp.concatenate(parts, axis=0)
        o = jnp.concatenate(
            [o[:, hc] * lax.rsqrt(jnp.mean(o[:, hc] * o[:, hc], axis=-1, keepdims=True) + EPS) for hc in hcols], axis=1)
        o_ref[0, rows, :] = (o * ng_ref[...] * _silu(hg_ref[0, rows, 3 * W:4 * W])).astype(BF16)
        bl = b[C - 1:C]
        ke = (kk * jnp.exp(bl - b)).astype(BF16)
        decay = jnp.exp(bl)
        for h, hc in enumerate(hcols):
            kv_t = lax.dot_general(vb[:, hc], ke[:, hc], TN, preferred_element_type=F32)
            st_scr[h] = decay[:, hc] * st_scr[h] + kv_t
        return carry

    lax.fori_loop(0, ts // C, chunk, 0, unroll=4)


def _hgrn_mxu(hg, lb, ng, ts):
    B, S, _ = hg.shape
    vec = pl.BlockSpec((1, HG_WIDTH), lambda b, i: (0, 0))
    return pl.pallas_call(
        _hgrn_mxu_kernel,
        out_shape=jax.ShapeDtypeStruct((B, S, HG_WIDTH), BF16),
        grid=(B, S // ts),
        in_specs=[pl.BlockSpec((1, ts, 4 * HG_WIDTH), lambda b, i: (b, i, 0)), vec, vec],
        out_specs=pl.BlockSpec((1, ts, HG_WIDTH), lambda b, i: (b, i, 0)),
        scratch_shapes=[pltpu.VMEM((HG_HEADS, HG_DV, HG_DK), F32)],
        compiler_params=_cparams(("arbitrary", "arbitrary")),
        name="hgrn_mxu",
    )(hg, lb, ng)


def _hgrn(hg, lb, ng):
    B, S, _ = hg.shape
    col = lambda k: pl.BlockSpec((1, S, HG_DK), lambda b, h, k=k: (b, 0, k * HG_HEADS + h))
    vec = pl.BlockSpec((1, HG_DK), lambda b, h: (0, h))
    return pl.pallas_call(
        _hgrn_kernel,
        out_shape=jax.ShapeDtypeStruct((B, S, HG_WIDTH), BF16),
        grid=(B, HG_HEADS),
        in_specs=[col(0), col(1), col(2), col(3), vec, vec],
        out_specs=pl.BlockSpec((1, S, HG_DV), lambda b, h: (b, 0, h)),
        compiler_params=_cparams(("arbitrary", "arbitrary")),
        name="hgrn",
    )(hg, hg, hg, hg, lb, ng)


def _cmpmlp_kernel(x_ref, w1a_ref, w1b_ref, pe_ref, w1_ref, b1_ref, w2_ref, o_ref):
    x = x_ref[0]
    hb = jnp.dot(pe_ref[...], w1_ref[...], precision=HIGHEST, preferred_element_type=F32)[0:1] + b1_ref[...]
    nrow = x.shape[0]
    for g in range(NSA_KV_GROUPS):
        a = jnp.dot(x, w1a_ref[g], preferred_element_type=F32)
        bm = jnp.dot(x, w1b_ref[g], preferred_element_type=F32)
        hdn = a + pltpu.roll(bm, nrow - 1, 0) + hb
        o_ref[0, g] = jnp.dot(_gelu_tanh(hdn).astype(BF16), w2_ref[g], preferred_element_type=F32).astype(BF16)


def _cmpmlp(x2, w1a, w1b, pe8, w1, b1, w2p):
    B, ncb, width = x2.shape
    full = lambda a: pl.BlockSpec(a.shape, lambda b: (0,) * a.ndim)
    return pl.pallas_call(
        _cmpmlp_kernel,
        out_shape=jax.ShapeDtypeStruct((B, NSA_KV_GROUPS, ncb, LANES), BF16),
        grid=(B,),
        in_specs=[pl.BlockSpec((1, ncb, width), lambda b: (b, 0, 0)),
                  full(w1a), full(w1b), full(pe8), full(w1), full(b1), full(w2p)],
        out_specs=pl.BlockSpec((1, NSA_KV_GROUPS, ncb, LANES), lambda b: (b, 0, 0, 0)),
        compiler_params=_cparams(("arbitrary",)),
        name="cmpmlp",
    )(x2, w1a, w1b, pe8, w1, b1, w2p)


def _rank_before(score, nrows):
    jrow = lax.broadcasted_iota(jnp.int32, score.shape, 0)
    rank = jnp.zeros(score.shape, F32)
    for k in range(nrows):
        rk = score[k:k + 1]
        beats = (rk > score) | ((rk == score) & (jrow > k))
        rank = rank + jnp.where(beats, 1.0, 0.0)
    return rank


def _transpose_via_mxu(xt):
    c = xt.shape[1]
    eye = (lax.broadcasted_iota(jnp.int32, (c, c), 0) == lax.broadcasted_iota(jnp.int32, (c, c), 1)).astype(F32)
    return lax.dot_general(eye, xt, NT, precision=HIGHEST, preferred_element_type=F32)


def _cmpsel_kernel(q_ref, kc_ref, vc_ref, o_ref, sel_ref):
    tq = q_ref.shape[1]
    ncb = kc_ref.shape[2]
    t = pl.program_id(2)
    kc = kc_ref[0, 0]
    vc = vc_ref[0, 0]
    pos = t * tq + lax.broadcasted_iota(jnp.int32, (tq, 1), 0)
    cblk = lax.broadcasted_iota(jnp.int32, (1, ncb), 1)
    vis = (cblk * CMP_STRIDE + CMP_BLOCK - 1) <= pos
    psum = jnp.zeros((tq, ncb), F32)
    for h in range(NSA_HG):
        s = lax.dot_general(q_ref[0, :, h * LANES:(h + 1) * LANES], kc, NT, preferred_element_type=F32)
        s = jnp.where(vis, s, NEG)
        p = jnp.exp(s - jnp.max(s, axis=-1, keepdims=True))
        p = jnp.where(vis, p / jnp.sum(p, axis=-1, keepdims=True), 0.0)
        o_ref[0, :, h * LANES:(h + 1) * LANES] = jnp.dot(p.astype(BF16), vc, preferred_element_type=F32)
        psum = psum + p
    nsb = ncb * CMP_STRIDE // SEL_BLOCK
    jb = lax.broadcasted_iota(jnp.int32, (nsb, ncb), 0) * SEL_BLOCK
    cb = lax.broadcasted_iota(jnp.int32, (nsb, ncb), 1) * CMP_STRIDE
    ov = jnp.maximum(jnp.minimum(cb + CMP_BLOCK, jb + SEL_BLOCK) - jnp.maximum(cb, jb), 0).astype(F32) / CMP_BLOCK
    pslc_t = lax.dot_general(ov, psum, NT, precision=HIGHEST, preferred_element_type=F32)
    posl = t * tq + lax.broadcasted_iota(jnp.int32, (1, tq), 1)
    cur = posl // SEL_BLOCK
    jrow = lax.broadcasted_iota(jnp.int32, (nsb, tq), 0)
    forced = (jrow == 0) | (jrow == cur) | (jrow == cur - 1)
    score = jnp.where(forced, 1e30, jnp.where(jrow <= cur, pslc_t, NEG))
    rank = _rank_before(score, nsb)
    drop = jnp.where((rank < min(N_SEL, nsb)) & (score > -1e29), 0.0, 1.0)
    drop = jnp.concatenate([drop, jnp.zeros((LANES - nsb, tq), F32)], axis=0)
    sel_ref[0, 0] = (_transpose_via_mxu(drop) * NEG).astype(BF16)


def _cmpsel(qraw, kcmp, vcmp, tq):
    B, S, _ = qraw.shape
    ncb = kcmp.shape[2]
    gw = NSA_HG * LANES
    assert S // SEL_BLOCK <= NSA_HD
    cmp_spec = pl.BlockSpec((1, 1, ncb, LANES), lambda b, g, t: (b, g, 0, 0))
    return pl.pallas_call(
        _cmpsel_kernel,
        out_shape=(jax.ShapeDtypeStruct((B, S, NSA_HEADS * LANES), F32),
                   jax.ShapeDtypeStruct((B, NSA_KV_GROUPS, S, LANES), BF16)),
        grid=(B, NSA_KV_GROUPS, S // tq),
        in_specs=[pl.BlockSpec((1, tq, gw), lambda b, g, t: (b, t, g)), cmp_spec, cmp_spec],
        out_specs=(pl.BlockSpec((1, tq, gw), lambda b, g, t: (b, t, g)),
                   pl.BlockSpec((1, 1, tq, LANES), lambda b, g, t: (b, g, t, 0))),
        compiler_params=_cparams(("arbitrary", "arbitrary", "arbitrary")),
        name="cmpsel",
    )(qraw, kcmp, vcmp)


def _slcwin_kernel(q_ref, sel_ref, ks_ref, kw_ref, vs_ref, vw_ref, oslc_ref, owin_ref, *, tk, wk):
    tq = q_ref.shape[1]
    S = ks_ref.shape[1]
    t = pl.program_id(2)
    q0 = t * tq
    qpos = q0 + lax.broadcasted_iota(jnp.int32, (tq, 1), 0)
    qw = [q_ref[0, :, h * LANES:(h + 1) * LANES] for h in range(NSA_HG)]
    sel = sel_ref[0, 0]
    qa = [jnp.concatenate([q, sel], axis=1) for q in qw]

    def normalise(acc):
        return acc * (1.0 / acc[:, NSA_HD:NSA_HD + 1])

    def kstep(j, carry, diag):
        k0 = pl.multiple_of(j * tk, tk)
        ks = ks_ref[0, pl.ds(k0, tk), :]
        vs = vs_ref[0, pl.ds(k0, tk), :]
        if diag:
            vis = (k0 + lax.broadcasted_iota(jnp.int32, (1, tk), 1)) <= qpos
        out = []
        for h in range(NSA_HG):
            m, acc = carry[h]
            s = lax.dot_general(qa[h], ks, NT, preferred_element_type=F32)
            if diag:
                s = jnp.where(vis, s, NEG)
            mn = jnp.maximum(m, jnp.max(s, axis=-1, keepdims=True))
            p = jnp.exp2(s - mn)
            acc = jnp.exp2(m - mn) * acc + jnp.dot(p.astype(BF16), vs, preferred_element_type=F32)
            out.append((mn, acc))
        return tuple(out)

    init = tuple((jnp.full((tq, 1), NEG, F32), jnp.zeros((tq, LANES), F32)) for _ in range(NSA_HG))
    jdiag = q0 // tk
    carry = lax.fori_loop(0, jdiag, lambda j, c: kstep(j, c, False), init)
    carry = kstep(jdiag, carry, True)
    for h in range(NSA_HG):
        oslc_ref[0, :, h * LANES:(h + 1) * LANES] = normalise(carry[h][1])

    nblk = S // tq
    start = pl.multiple_of(jnp.clip(t - WINDOW // tq, 0, nblk - wk // tq) * tq, tq)
    kw = kw_ref[0, pl.ds(start, wk), :]
    vw = vw_ref[0, pl.ds(start, wk), :]
    d = qpos - (start + lax.broadcasted_iota(jnp.int32, (1, wk), 1))
    okw = (d >= 0) & (d < WINDOW)
    for h in range(NSA_HG):
        s = jnp.where(okw, lax.dot_general(qw[h], kw, NT, preferred_element_type=F32), NEG)
        p = jnp.exp2(s - jnp.max(s, axis=-1, keepdims=True))
        owin_ref[0, :, h * LANES:(h + 1) * LANES] = normalise(jnp.dot(p.astype(BF16), vw, preferred_element_type=F32))


def _slcwin(qrot, sel, ks, kw, vaug, tq, tk):
    B, S, _ = qrot.shape
    gw = NSA_HG * LANES
    wk = min(WINDOW + tq, S)
    oshape = jax.ShapeDtypeStruct((B, S, NSA_HEADS * LANES), F32)
    ospec = pl.BlockSpec((1, tq, gw), lambda b, g, t: (b, t, g))
    return pl.pallas_call(
        functools.partial(_slcwin_kernel, tk=tk, wk=wk),
        out_shape=(oshape, oshape),
        grid=(B, NSA_KV_GROUPS, S // tq),
        in_specs=[pl.BlockSpec((1, tq, gw), lambda b, g, t: (b, t, g)),
                  pl.BlockSpec((1, 1, tq, LANES), lambda b, g, t: (b, g, t, 0)),
                  pl.BlockSpec((1, S, 2 * LANES), lambda b, g, t: (b, 0, 0)),
                  pl.BlockSpec((1, S, LANES), lambda b, g, t: (b, 0, 0)),
                  pl.BlockSpec((1, S, LANES), lambda b, g, t: (b, 0, g)),
                  pl.BlockSpec((1, S, LANES), lambda b, g, t: (b, 0, NSA_KV_GROUPS + g))],
        out_specs=(ospec, ospec),
        compiler_params=_cparams(("arbitrary", "arbitrary", "arbitrary")),
        name="slcwin",
    )(qrot, sel, ks, kw, vaug, vaug)


def _outproj_kernel(x_ref, ohg_ref, ocmp_ref, oslc_ref, owin_ref, gate_ref, whg_ref, wnsa_ref,
                    g1_ref, sh2_ref, sc2_ref, n2_ref, rwt_ref, rb_ref,
                    x1_ref, h2_ref, sc_ref, slot_t_ref, w_t_ref, cnt_ref):
    tm = x_ref.shape[1]
    gs = _sigmoid(gate_ref[0])
    mix = jnp.dot(ohg_ref[0], whg_ref[...], preferred_element_type=F32)
    for n in range(NSA_HEADS):
        cols = slice(n * LANES, (n + 1) * LANES)
        o = (gs[:, 3 * n:3 * n + 1] * ocmp_ref[0, :, cols] + gs[:, 3 * n + 1:3 * n + 2] * oslc_ref[0, :, cols]
             + gs[:, 3 * n + 2:3 * n + 3] * owin_ref[0, :, cols])
        mix = mix + jnp.dot(o.astype(BF16), wnsa_ref[cols, :], preferred_element_type=F32)
    x1 = x_ref[0] + g1_ref[0] * mix
    x1_ref[0] = x1
    h2 = (x1 * lax.rsqrt(jnp.mean(x1 * x1, axis=-1, keepdims=True) + EPS) * n2_ref[...]) * (1.0 + sc2_ref[0]) + sh2_ref[0]
    h2_ref[0] = h2.astype(BF16)
    scores = _sigmoid(lax.dot_general(rwt_ref[...], h2, NT, precision=HIGHEST, preferred_element_type=F32))
    choice = scores + rb_ref[...]
    per = N_EXPERTS // N_GROUPS
    c3 = choice.reshape(N_GROUPS, per, tm)
    erow = lax.broadcasted_iota(jnp.int32, c3.shape, 1)
    rank_in = jnp.zeros(c3.shape, F32)
    for k in range(per):
        ck = c3[:, k:k + 1, :]
        rank_in = rank_in + jnp.where((ck > c3) | ((ck == c3) & (erow > k)), 1.0, 0.0)
    grp_score = jnp.sum(jnp.where(rank_in < 2, c3, 0.0), axis=1)
    grp_keep = _rank_before(grp_score, N_GROUPS) < TOPK_GROUPS
    masked = jnp.where(grp_keep[:, None, :], c3, -jnp.inf).reshape(N_EXPERTS, tm)
    keep = _rank_before(masked, N_EXPERTS) < TOP_K
    tw = jnp.where(keep, scores, 0.0)
    tw = tw / jnp.sum(tw, axis=0, keepdims=True) * ROUTED_SCALE
    before = lax.broadcasted_iota(jnp.int32, (tm, tm), 0)
    after = lax.broadcasted_iota(jnp.int32, (tm, tm), 1)
    earlier = (before < after) & (before // MOE_SUB == after // MOE_SUB)
    kept = jnp.where(keep, 1.0, 0.0)
    pref = jnp.dot(kept.astype(BF16), earlier.astype(BF16), preferred_element_type=F32)
    slot = jnp.where(keep, pref, -1.0)
    sc_ref[0] = _transpose_via_mxu(slot)
    for j in range(tm // MOE_SUB):
        sub = slice(j * MOE_SUB, (j + 1) * MOE_SUB)
        slot_t_ref[:, j] = slot[:, sub].reshape(N_EXPERTS // MOE_EG, MOE_EG, MOE_SUB)
        w_t_ref[:, j] = tw[:, sub].reshape(N_EXPERTS // MOE_EG, MOE_EG, MOE_SUB)
        cnt_ref[j] = jnp.broadcast_to(jnp.sum(kept[:, sub], axis=1, keepdims=True), (N_EXPERTS, LANES))


def _outproj(x, ohg, ocmp, oslc, owin, gate, whg, wnsa, mod3, n2, rwt, rb, tm):
    B, S, D = x.shape
    blk = lambda w: pl.BlockSpec((1, tm, w), lambda b, i: (b, i, 0))
    full = lambda a: pl.BlockSpec(a.shape, lambda b, i: (0,) * a.ndim)
    modc = lambda k: pl.BlockSpec((1, 1, D), lambda b, i, k=k: (b, 0, k))
    nt = S // tm
    ng = N_EXPERTS // MOE_EG
    nh = tm // MOE_SUB
    tspec = pl.BlockSpec((ng, nh, MOE_EG, MOE_SUB), lambda b, i: (0, b * nt + i, 0, 0))
    return pl.pallas_call(
        _outproj_kernel,
        out_shape=(jax.ShapeDtypeStruct((B, S, D), F32),
                   jax.ShapeDtypeStruct((B, S, D), BF16),
                   jax.ShapeDtypeStruct((B, S, N_EXPERTS), F32),
                   jax.ShapeDtypeStruct((ng, B * nt * nh, MOE_EG, MOE_SUB), F32),
                   jax.ShapeDtypeStruct((ng, B * nt * nh, MOE_EG, MOE_SUB), F32),
                   jax.ShapeDtypeStruct((B * nt * nh, N_EXPERTS, LANES), F32)),
        grid=(B, nt),
        in_specs=[blk(D), blk(HG_WIDTH), blk(NSA_HEADS * LANES), blk(NSA_HEADS * LANES), blk(NSA_HEADS * LANES),
                  blk(LANES), full(whg), full(wnsa), modc(2), modc(3), modc(4), full(n2), full(rwt), full(rb)],
        out_specs=(blk(D), blk(D), blk(N_EXPERTS), tspec, tspec,
                   pl.BlockSpec((nh, N_EXPERTS, LANES), lambda b, i: (b * nt + i, 0, 0))),
        compiler_params=_cparams(("arbitrary", "arbitrary")),
        name="outproj",
    )(x, ohg, ocmp, oslc, owin, gate, whg, wnsa, mod3, mod3, mod3, n2, rwt, rb)


def _swiglu_hidden(x, wgu):
    gu = jnp.dot(x, wgu, preferred_element_type=F32)
    return _silu(gu[:, :EXPERT_HIDDEN]) * gu[:, EXPERT_HIDDEN:]


def _moe_kernel(rounds_ref, h_ref, slot_t_ref, w_t_ref, slot_c_ref, wgu_ref, wdn_ref, sgu_ref, sdn_ref,
                o_ref, x_scr, y_scr, p_scr):
    i = pl.program_id(0)
    g = pl.program_id(1)
    tm = h_ref.shape[0]
    nsub = tm // MOE_SUB
    rnd, eg = MOE_ROUND, MOE_EG
    ng = N_EXPERTS // eg
    base = i * (N_EXPERTS + ng)

    @pl.when(g == 0)
    def _():
        act = _swiglu_hidden(h_ref[...], sgu_ref[...]).astype(BF16)
        o_ref[...] = jnp.dot(act, sdn_ref[...], preferred_element_type=F32)

    rslot = lax.broadcasted_iota(jnp.int32, (rnd, 1), 0).astype(F32)
    lane_slot = (lax.broadcasted_iota(jnp.int32, (1, eg * rnd), 1) % rnd).astype(F32)
    spread = (lax.broadcasted_iota(jnp.int32, (N_EXPERTS, eg * rnd), 1) // rnd + g * eg
              == lax.broadcasted_iota(jnp.int32, (N_EXPERTS, eg * rnd), 0)).astype(BF16)

    def one_round(r, carry):
        first = (r * rnd).astype(F32)
        for s in range(nsub):
            st = slot_t_ref[0, s] - first
            p = jnp.concatenate([jnp.where(rslot == st[e:e + 1, :], 1.0, 0.0) for e in range(eg)], axis=0)
            p_scr[s] = p.astype(BF16)
            x_scr[s] = jnp.dot(p_scr[s], h_ref[s * MOE_SUB:(s + 1) * MOE_SUB, :],
                               preferred_element_type=F32).astype(BF16)
        for e in range(eg):
            rows = slice(e * rnd, (e + 1) * rnd)

            @pl.when(r < rounds_ref[base + g * eg + e])
            def _():
                xe = jnp.concatenate([x_scr[s, rows, :] for s in range(nsub)], axis=0)
                wslot = jnp.concatenate(
                    [jnp.sum(p_scr[s, rows, :].astype(F32) * w_t_ref[0, s][e:e + 1, :], axis=-1, keepdims=True)
                     for s in range(nsub)], axis=0)
                y = jnp.dot((_swiglu_hidden(xe, wgu_ref[e]) * wslot).astype(BF16), wdn_ref[e],
                            preferred_element_type=F32)
                for s in range(nsub):
                    y_scr[s, rows, :] = y[s * rnd:(s + 1) * rnd].astype(BF16)

            @pl.when(r >= rounds_ref[base + g * eg + e])
            def _():
                for s in range(nsub):
                    y_scr[s, rows, :] = jnp.zeros((rnd, D_MODEL), BF16)
        for s in range(nsub):
            toks = slice(s * MOE_SUB, (s + 1) * MOE_SUB)
            sc = jnp.dot(slot_c_ref[toks, :].astype(BF16), spread, preferred_element_type=F32) - first
            pt = jnp.where(sc == lane_slot, 1.0, 0.0).astype(BF16)
            o_ref[toks, :] += jnp.dot(pt, y_scr[s], preferred_element_type=F32)
        return carry

    lax.fori_loop(0, rounds_ref[base + N_EXPERTS + g], one_round, 0)


def _moe(rounds, h2, slot_t, w_t, slot_c, wgu, wdn, sgu, sdn, tm):
    T, D = h2.shape
    eg, nsub = MOE_EG, tm // MOE_SUB
    full = lambda a: pl.BlockSpec(a.shape, lambda i, e, o: (0,) * a.ndim)
    tspec = pl.BlockSpec((1, nsub, eg, MOE_SUB), lambda i, e, o: (e, i, 0, 0))
    return pl.pallas_call(
        _moe_kernel,
        out_shape=jax.ShapeDtypeStruct((T, D), F32),
        grid_spec=pltpu.PrefetchScalarGridSpec(
            num_scalar_prefetch=1,
            grid=(T // tm, N_EXPERTS // eg),
            in_specs=[pl.BlockSpec((tm, D), lambda i, e, o: (i, 0), pipeline_mode=pl.Buffered(1)), tspec, tspec,
                      pl.BlockSpec((tm, N_EXPERTS), lambda i, e, o: (i, 0)),
                      pl.BlockSpec((eg, D, 2 * EXPERT_HIDDEN), lambda i, e, o: (e, 0, 0)),
                      pl.BlockSpec((eg, EXPERT_HIDDEN, D), lambda i, e, o: (e, 0, 0)),
                      full(sgu), full(sdn)],
            out_specs=pl.BlockSpec((tm, D), lambda i, e, o: (i, 0), pipeline_mode=pl.Buffered(1)),
            scratch_shapes=[pltpu.VMEM((nsub, eg * MOE_ROUND, D), BF16),
                            pltpu.VMEM((nsub, eg * MOE_ROUND, D), BF16),
                            pltpu.VMEM((nsub, eg * MOE_ROUND, MOE_SUB), BF16)]),
        compiler_params=_cparams(("arbitrary", "arbitrary")),
        name="moe",
    )(rounds, h2, slot_t, w_t, slot_c, wgu, wdn, sgu, sdn)


def _final_kernel(x1_ref, moe_ref, g2_ref, fg_ref, o_ref):
    x2 = x1_ref[0] + g2_ref[0] * moe_ref[0]
    o_ref[0] = x2 * lax.rsqrt(jnp.mean(x2 * x2, axis=-1, keepdims=True) + EPS) * fg_ref[...]


def _final(x1, moe, mod3, fg, tm):
    B, S, D = x1.shape
    blk = pl.BlockSpec((1, tm, D), lambda b, i: (b, i, 0))
    return pl.pallas_call(
        _final_kernel,
        out_shape=jax.ShapeDtypeStruct((B, S, D), F32),
        grid=(B, S // tm),
        in_specs=[blk, blk, pl.BlockSpec((1, 1, D), lambda b, i: (b, 0, 5)), pl.BlockSpec((1, D), lambda b, i: (0, 0))],
        out_specs=blk,
        compiler_params=_cparams(("arbitrary", "arbitrary")),
        name="final",
    )(x1, moe, mod3, fg)


def _pack_w_in(w_in):
    return jnp.pad(w_in, ((0, 0), (0, IN_COLS_P - w_in.shape[1]))).astype(BF16)


def _pack_w_out(w_out):
    rows = []
    for n in range(NSA_HEADS):
        head = w_out[HG_WIDTH + n * NSA_HD:HG_WIDTH + (n + 1) * NSA_HD]
        rows += [head, jnp.zeros((LANES - NSA_HD, w_out.shape[1]), w_out.dtype)]
    return w_out[:HG_WIDTH].astype(BF16), jnp.concatenate(rows, axis=0).astype(BF16)


def _pack_cmp(pos, w1, b1, w2, lane_by_group):
    half = CMP_STRIDE * NSA_HD
    def rows_for(wh):
        w3 = wh.reshape(CMP_STRIDE, NSA_HD, CMP_HIDDEN)
        z = jnp.zeros_like(w3)
        return jnp.stack([jnp.concatenate([w3, z], axis=1).reshape(CMP_STRIDE * LANES, CMP_HIDDEN),
                          jnp.concatenate([z, w3], axis=1).reshape(CMP_STRIDE * LANES, CMP_HIDDEN)])
    w1a = rows_for(w1[:half]).astype(BF16)
    w1b = rows_for(w1[half:]).astype(BF16)
    z2 = jnp.zeros_like(w2)
    w2_first = jnp.concatenate([w2, z2], axis=1)
    w2p = jnp.stack([w2_first, jnp.concatenate([z2, w2], axis=1) if lane_by_group else w2_first]).astype(BF16)
    pe8 = jnp.pad(pos.reshape(1, CMP_BLOCK * NSA_HD), ((0, 7), (0, 0)))
    return w1a, w1b, pe8, w1, b1.reshape(1, CMP_HIDDEN), w2p


def _rope_tables(S):
    half = NSA_HD // 2
    inv = ROPE_THETA ** (-jnp.arange(half, dtype=F32) / half)
    ang = jnp.arange(S, dtype=F32)[:, None] * inv[None, :]
    cos, sin = jnp.cos(ang), jnp.sin(ang)
    reps = LANES // NSA_HD
    return jnp.tile(jnp.concatenate([cos, cos], axis=1), (1, reps)), jnp.tile(jnp.concatenate([-sin, sin], axis=1), (1, reps))


def _tiles(S):
    return dict(inproj=min(512, S), hgrn=min(512, S), cmpsel=min(512, S), slc_q=512, slc_k=512, moe=min(2048, S), final=min(512, S))


def kernel(x, c, w_ada, b_ada, norm1_g, w_in, hg_lb_logits, hg_norm_g, cmp_pos_k, cmp_w1_k, cmp_b1_k, cmp_w2_k,
           cmp_pos_v, cmp_w1_v, cmp_b1_v, cmp_w2_v, w_out, norm2_g, router_w, router_bias, w_exp_gu, w_exp_dn,
           w_sh_gu, w_sh_dn, final_g):
    B, S, D = x.shape
    assert D == D_MODEL and w_ada.shape[0] == 1 and S % 512 == 0
    tl = _tiles(S)
    l = 0
    lb = jnp.cumsum(jax.nn.softmax(hg_lb_logits.astype(F32), axis=0), axis=0)[l].reshape(1, HG_WIDTH)
    c8 = jnp.pad(c, ((0, 8 - B), (0, 0)))
    mod3 = _ada(c8, w_ada[l], b_ada[l].reshape(1, -1))[:B].reshape(B, 1, 6 * D)
    cos, sin = _rope_tables(S)
    hg, qraw, qrot, kc, vc, ks, kw, vaug, gate = _inproj(x, mod3, norm1_g[l].reshape(1, D), _pack_w_in(w_in[l]),
                                                         cos, sin, tl["inproj"])
    ng = hg_norm_g[l].reshape(1, HG_WIDTH)
    mxu_safe = -HG_SUB * jnp.log(jnp.min(lb)) <= HG_SAFE_LOG_RANGE
    ohg = lax.cond(mxu_safe, lambda: _hgrn_mxu(hg, lb, ng, tl["hgrn"]), lambda: _hgrn(hg, lb, ng))
    ncb = S // CMP_STRIDE
    kcmp = _cmpmlp(kc.reshape(B, ncb, CMP_STRIDE * LANES),
                   *_pack_cmp(cmp_pos_k[l], cmp_w1_k[l], cmp_b1_k[l], cmp_w2_k[l], True))
    vcmp = _cmpmlp(vc.reshape(B, ncb, CMP_STRIDE * LANES),
                   *_pack_cmp(cmp_pos_v[l], cmp_w1_v[l], cmp_b1_v[l], cmp_w2_v[l], False))
    ocmp, sel = _cmpsel(qraw, kcmp, vcmp, tl["cmpsel"])
    oslc, owin = _slcwin(qrot, sel, ks, kw, vaug, tl["slc_q"], tl["slc_k"])
    whg, wnsa = _pack_w_out(w_out[l])
    x1, h2, sc, slot_t, w_t, cnt = _outproj(x, ohg, ocmp, oslc, owin, gate, whg, wnsa, mod3,
                                            norm2_g[l].reshape(1, D), router_w[l].T,
                                            router_bias[l].reshape(N_EXPERTS, 1), ROUTE_TM)
    T = B * S
    tm = tl["moe"]
    ng = N_EXPERTS // MOE_EG
    slot_c = sc.reshape(T, N_EXPERTS)
    load = jnp.max(cnt[:, :, 0].reshape(T // tm, tm // MOE_SUB, N_EXPERTS), axis=1).astype(jnp.int32)
    per_expert = (load + MOE_ROUND - 1) // MOE_ROUND
    rounds = jnp.concatenate([per_expert, jnp.max(per_expert.reshape(T // tm, ng, MOE_EG), axis=2)], axis=1).reshape(-1)
    moe = _moe(rounds, h2.reshape(T, D), slot_t, w_t, slot_c,
               w_exp_gu[l].astype(BF16), w_exp_dn[l].astype(BF16), w_sh_gu[l].astype(BF16), w_sh_dn[l].astype(BF16), tm)
    return _final(x1, moe.reshape(B, S, D), mod3, final_g.reshape(1, D), tl["final"])
```

```python
import functools

import numpy as np
import jax
import jax.numpy as jnp
from jax import lax
from jax.experimental import pallas as pl
from jax.experimental.pallas import tpu as pltpu

F32 = jnp.float32
BF16 = jnp.bfloat16
HIGHEST = lax.Precision.HIGHEST

D_MODEL = 1024
EPS = 1e-6
HG_HEADS = 4
HG_DK = 128
HG_DV = 128
HG_WIDTH = HG_HEADS * HG_DV
HG_CHUNK = 64
HG_SUB = 16
HG_SAFE_LOG_RANGE = 80.0
NSA_HEADS = 8
NSA_KV_GROUPS = 2
NSA_HG = NSA_HEADS // NSA_KV_GROUPS
NSA_HD = 64
NSA_WIDTH = NSA_HEADS * NSA_HD
NSA_KV = NSA_KV_GROUPS * NSA_HD
CMP_BLOCK = 32
CMP_STRIDE = 16
CMP_HIDDEN = 256
SEL_BLOCK = 64
N_SEL = 16
WINDOW = 512
ROPE_THETA = 10000.0
N_EXPERTS = 64
TOP_K = 8
N_GROUPS = 8
TOPK_GROUPS = 4
EXPERT_HIDDEN = 256
ROUTED_SCALE = 2.5
ROUTE_TM = 512
MOE_SUB = 256
MOE_ROUND = 32
MOE_EG = 8

LANES = 128
NEG = -1e30
VMEM_LIMIT = 56 * 1024 * 1024

C_HG = 0
C_Q = 4 * HG_WIDTH
C_K = C_Q + NSA_WIDTH
C_GATE = C_K + 6 * NSA_KV
IN_COLS_P = C_GATE + LANES
LOG2E = 1.4426950408889634

NT = (((1,), (1,)), ((), ()))
TN = (((0,), (0,)), ((), ()))


def _cparams(sem):
    return pltpu.CompilerParams(dimension_semantics=sem, vmem_limit_bytes=VMEM_LIMIT)


def _sigmoid(x):
    return 1.0 / (1.0 + jnp.exp(-x))


def _silu(x):
    return x * _sigmoid(x)


def _gelu_tanh(x):
    return 0.5 * x * (1.0 + jnp.tanh(0.7978845608028654 * (x + 0.044715 * (x * x * x))))


def _ada_kernel(c_ref, w_ref, b_ref, o_ref):
    a = _silu(c_ref[...])
    o_ref[...] = jnp.dot(a, w_ref[...], precision=HIGHEST, preferred_element_type=F32) + b_ref[...]


def _ada(c8, w, b):
    n = w.shape[1]
    tn = 1024
    return pl.pallas_call(
        _ada_kernel,
        out_shape=jax.ShapeDtypeStruct((8, n), F32),
        grid=(n // tn,),
        in_specs=[pl.BlockSpec((8, D_MODEL), lambda j: (0, 0)),
                  pl.BlockSpec((D_MODEL, tn), lambda j: (0, j)),
                  pl.BlockSpec((1, tn), lambda j: (0, j))],
        out_specs=pl.BlockSpec((8, tn), lambda j: (0, j)),
        compiler_params=_cparams(("arbitrary",)),
        name="ada",
    )(c8, w, b)


def _rope(t, cos, sin_signed, first_half):
    rot = jnp.where(first_half, pltpu.roll(t, 96, 1), pltpu.roll(t, 32, 1))
    return t * cos + rot * sin_signed


def _inproj_kernel(x_ref, sh_ref, sc_ref, g_ref, w_ref, cos_ref, sin_ref,
                   hg_ref, qraw_ref, qrot_ref, kc_ref, vc_ref, ks_ref, kw_ref, va_ref, gate_ref, h_scr):
    tm = x_ref.shape[1]
    x = x_ref[0]
    y = x * lax.rsqrt(jnp.mean(x * x, axis=-1, keepdims=True) + EPS) * g_ref[...]
    h_scr[...] = (y * (1.0 + sc_ref[0]) + sh_ref[0]).astype(BF16)

    def mm(lo, width):
        return jnp.dot(h_scr[...], w_ref[:, lo:lo + width], preferred_element_type=F32)

    cos = cos_ref[...]
    sin = sin_ref[...]
    first_half = (lax.broadcasted_iota(jnp.int32, cos.shape, 1) % NSA_HD) < (NSA_HD // 2)
    lane = lax.broadcasted_iota(jnp.int32, (tm, LANES), 1)
    half_of_lane = lane // NSA_HD

    def own_lanes(t, src_half, dst_half):
        moved = t if src_half == dst_half else pltpu.roll(t, NSA_HD, 1)
        return jnp.where(half_of_lane == dst_half, moved, 0.0)

    for j in range(4):
        hg_ref[0, :, j * HG_WIDTH:(j + 1) * HG_WIDTH] = mm(C_HG + j * HG_WIDTH, HG_WIDTH)
    for m in range(NSA_WIDTH // (2 * LANES)):
        qq = mm(C_Q + m * 2 * LANES, 2 * LANES) * (NSA_HD ** -0.5)
        for c in range(2):
            q = qq[:, c * LANES:(c + 1) * LANES]
            qr = _rope(q, cos, sin, first_half) * LOG2E
            for half in range(2):
                n = (2 * m + c) * 2 + half
                g = n // NSA_HG
                qraw_ref[0, :, n * LANES:(n + 1) * LANES] = own_lanes(q, half, g).astype(BF16)
                qrot_ref[0, :, n * LANES:(n + 1) * LANES] = own_lanes(qr, half, g).astype(BF16)
    kv = mm(C_K, 2 * LANES)
    kc_ref[0] = kv[:, :LANES].astype(BF16)
    vc_ref[0] = kv[:, LANES:].astype(BF16)
    pos = pl.program_id(1) * tm + lax.broadcasted_iota(jnp.int32, (tm, LANES), 0)
    ks_ref[0, :, LANES:2 * LANES] = jnp.where(lane == pos // SEL_BLOCK, 1.0, 0.0).astype(BF16)
    for j in range(2):
        kv = mm(C_K + (j + 1) * 2 * LANES, 2 * LANES)
        k_rot = _rope(kv[:, :LANES], cos, sin, first_half).astype(BF16)
        if j == 0:
            ks_ref[0, :, 0:LANES] = k_rot
        else:
            kw_ref[0] = k_rot
        for g in range(NSA_KV_GROUPS):
            v = own_lanes(kv[:, LANES:], g, 0)
            col = (j * NSA_KV_GROUPS + g) * LANES
            va_ref[0, :, col:col + LANES] = jnp.where(lane == NSA_HD, 1.0, v).astype(BF16)
    gate = _sigmoid(mm(C_GATE, LANES))
    per_group = 3 * NSA_HG
    for g in range(NSA_KV_GROUPS):
        gate_ref[0, :, g * LANES:(g + 1) * LANES] = gate if g == 0 else pltpu.roll(gate, LANES - g * per_group, 1)


def _inproj(x, mod3, norm_g, w_p, cos, sin, tm):
    B, S, D = x.shape
    blk = lambda w: pl.BlockSpec((1, tm, w), lambda b, i: (b, i, 0))
    return pl.pallas_call(
        _inproj_kernel,
        out_shape=(jax.ShapeDtypeStruct((B, S, 4 * HG_WIDTH), F32),
                   jax.ShapeDtypeStruct((B, S, NSA_HEADS * LANES), BF16),
                   jax.ShapeDtypeStruct((B, S, NSA_HEADS * LANES), BF16),
                   jax.ShapeDtypeStruct((B, S, LANES), BF16),
                   jax.ShapeDtypeStruct((B, S, LANES), BF16),
                   jax.ShapeDtypeStruct((B, S, 2 * LANES), BF16),
                   jax.ShapeDtypeStruct((B, S, LANES), BF16),
                   jax.ShapeDtypeStruct((B, S, 4 * LANES), BF16),
                   jax.ShapeDtypeStruct((B, S, NSA_KV_GROUPS * LANES), F32)),
        grid=(B, S // tm),
        in_specs=[blk(D),
                  pl.BlockSpec((1, 1, D), lambda b, i: (b, 0, 0)),
                  pl.BlockSpec((1, 1, D), lambda b, i: (b, 0, 1)),
                  pl.BlockSpec((1, D), lambda b, i: (0, 0)),
                  pl.BlockSpec((D, IN_COLS_P), lambda b, i: (0, 0)),
                  pl.BlockSpec((tm, LANES), lambda b, i: (i, 0)),
                  pl.BlockSpec((tm, LANES), lambda b, i: (i, 0))],
        out_specs=(blk(4 * HG_WIDTH), blk(NSA_HEADS * LANES), blk(NSA_HEADS * LANES),
                   blk(LANES), blk(LANES), blk(2 * LANES), blk(LANES), blk(4 * LANES), blk(NSA_KV_GROUPS * LANES)),
        scratch_shapes=[pltpu.VMEM((tm, D), BF16)],
        compiler_params=_cparams(("arbitrary", "arbitrary")),
        name="inproj",
    )(x, mod3, mod3, norm_g, w_p, cos, sin)


def _hgrn_kernel(q_ref, f_ref, i_ref, gt_ref, lb_ref, ng_ref, o_ref):
    S = q_ref.shape[1]
    C, U = HG_CHUNK, HG_SUB
    lb = lb_ref[...]
    ng = ng_ref[...]
    ri = lax.broadcasted_iota(jnp.int32, (C, C), 0)
    ci = lax.broadcasted_iota(jnp.int32, (C, C), 1)
    tril = (ri >= ci).astype(F32)
    trow = lax.broadcasted_iota(jnp.int32, (U, 1), 0)

    def chunk(c, st_t):
        r0 = pl.multiple_of(c * C, C)
        rows = pl.ds(r0, C)
        f = lb + (1.0 - lb) * _sigmoid(f_ref[0, rows, :])
        kk = 1.0 - f
        b = jnp.dot(tril, jnp.log(f), precision=HIGHEST, preferred_element_type=F32)
        q = q_ref[0, rows, :] * (HG_DK ** -0.5)
        v = i_ref[0, rows, :]
        vb = v.astype(BF16)
        o_inter = lax.dot_general((q * jnp.exp(b)).astype(BF16), st_t.astype(BF16), NT,
                                  preferred_element_type=F32)
        parts = []
        for i in range(C // U):
            lo = i * U
            bi = b[lo:lo + U]
            qi = q[lo:lo + U]
            if i == 0:
                oi = jnp.zeros((U, HG_DV), F32)
            else:
                r = b[lo - 1:lo]
                qrel = (qi * jnp.exp(bi - r)).astype(BF16)
                kprev = (kk[:lo] * jnp.exp(r - b[:lo])).astype(BF16)
                a_off = lax.dot_general(qrel, kprev, NT, preferred_element_type=F32)
                oi = jnp.dot(a_off.astype(BF16), vb[:lo], preferred_element_type=F32)
            for s in range(U):
                valid = trow >= s
                e = jnp.exp(jnp.where(valid, bi - bi[s:s + 1], 0.0))
                a = jnp.sum(qi * e * kk[lo + s:lo + s + 1], axis=-1, keepdims=True)
                oi = oi + jnp.where(valid, a, 0.0) * v[lo + s:lo + s + 1]
            parts.append(oi)
        o = o_inter + jnp.concatenate(parts, axis=0)
        o = o * lax.rsqrt(jnp.mean(o * o, axis=-1, keepdims=True) + EPS) * ng
        o_ref[0, rows, :] = (o * _silu(gt_ref[0, rows, :])).astype(BF16)
        bl = b[C - 1:C]
        kv_t = lax.dot_general(vb, (kk * jnp.exp(bl - b)).astype(BF16), TN, preferred_element_type=F32)
        return jnp.exp(bl) * st_t + kv_t

    lax.fori_loop(0, S // C, chunk, jnp.zeros((HG_DV, HG_DK), F32))


def _hgrn_mxu_kernel(hg_ref, lb_ref, ng_ref, o_ref, st_scr):
    ts = hg_ref.shape[1]
    C, U = HG_CHUNK, HG_SUB
    ri = lax.broadcasted_iota(jnp.int32, (C, C), 0)
    ci = lax.broadcasted_iota(jnp.int32, (C, C), 1)
    tril = (ri >= ci).astype(F32)

    @pl.when(pl.program_id(1) == 0)
    def _():
        st_scr[...] = jnp.zeros(st_scr.shape, F32)

    W = HG_WIDTH
    NH = HG_HEADS
    head_of_lane = lax.broadcasted_iota(jnp.int32, (1, W), 1) // HG_DK
    hcols = [slice(h * HG_DK, (h + 1) * HG_DK) for h in range(NH)]

    def chunk(c, carry):
        rows = pl.ds(pl.multiple_of(c * C, C), C)
        lb = lb_ref[...]
        f = lb + (1.0 - lb) * _sigmoid(hg_ref[0, rows, W:2 * W])
        kk = 1.0 - f
        b = jnp.dot(tril, jnp.log(f), precision=HIGHEST, preferred_element_type=F32)
        q = hg_ref[0, rows, 0:W] * (HG_DK ** -0.5)
        vb = hg_ref[0, rows, 2 * W:3 * W].astype(BF16)
        qe = (q * jnp.exp(b)).astype(BF16)
        o_inter = jnp.concatenate(
            [lax.dot_general(qe[:, hc], st_scr[h].astype(BF16), NT, preferred_element_type=F32)
             for h, hc in enumerate(hcols)], axis=1)
        parts = []
        for i in range(C // U):
            lo, hi = i * U, (i + 1) * U
            r = b[lo - 1:lo] if i else jnp.zeros((1, W), F32)
            qrel = q[lo:hi] * jnp.exp(b[lo:hi] - r)
            kall = (kk[:hi] * jnp.exp(r - b[:hi])).astype(BF16)
            qbd = jnp.concatenate([jnp.where(head_of_lane == h, qrel, 0.0) for h in range(NH)], axis=0)
            a = lax.dot_general(qbd.astype(BF16), kall, NT, preferred_element_type=F32)
            trow = lax.broadcasted_iota(jnp.int32, (NH * U, hi), 0) % U
            a = jnp.where(lax.broadcasted_iota(jnp.int32, (NH * U, hi), 1) <= lo + trow, a, 0.0)
            oa = jnp.dot(a.astype(BF16), vb[:hi], preferred_element_type=F32)
            oi = jnp.where(head_of_lane == 0, oa[0:U], 0.0)
            for h in range(1, NH):
                oi = jnp.where(head_of_lane == h, oa[h * U:(h + 1) * U], oi)
            parts.append(oi)
        o = o_inter + jnp.concatenate(parts, axis=0)
        o = jnp.concatenate(
            [o[:, hc] * lax.rsqrt(jnp.mean(o[:, hc] * o[:, hc], axis=-1, keepdims=True) + EPS) for hc in hcols], axis=1)
        o_ref[0, rows, :] = (o * ng_ref[...] * _silu(hg_ref[0, rows, 3 * W:4 * W])).astype(BF16)
        bl = b[C - 1:C]
        ke = (kk * jnp.exp(bl - b)).astype(BF16)
        decay = jnp.exp(bl)
        for h, hc in enumerate(hcols):
            kv_t = lax.dot_general(vb[:, hc], ke[:, hc], TN, preferred_element_type=F32)
            st_scr[h] = decay[:, hc] * st_scr[h] + kv_t
        return carry

    lax.fori_loop(0, ts // C, chunk, 0, unroll=4)


def _hgrn_mxu(hg, lb, ng, ts):
    B, S, _ = hg.shape
    vec = pl.BlockSpec((1, HG_WIDTH), lambda b, i: (0, 0))
    return pl.pallas_call(
        _hgrn_mxu_kernel,
        out_shape=jax.ShapeDtypeStruct((B, S, HG_WIDTH), BF16),
        grid=(B, S // ts),
        in_specs=[pl.BlockSpec((1, ts, 4 * HG_WIDTH), lambda b, i: (b, i, 0)), vec, vec],
        out_specs=pl.BlockSpec((1, ts, HG_WIDTH), lambda b, i: (b, i, 0)),
        scratch_shapes=[pltpu.VMEM((HG_HEADS, HG_DV, HG_DK), F32)],
        compiler_params=_cparams(("arbitrary", "arbitrary")),
        name="hgrn_mxu",
    )(hg, lb, ng)


def _hgrn(hg, lb, ng):
    B, S, _ = hg.shape
    col = lambda k: pl.BlockSpec((1, S, HG_DK), lambda b, h, k=k: (b, 0, k * HG_HEADS + h))
    vec = pl.BlockSpec((1, HG_DK), lambda b, h: (0, h))
    return pl.pallas_call(
        _hgrn_kernel,
        out_shape=jax.ShapeDtypeStruct((B, S, HG_WIDTH), BF16),
        grid=(B, HG_HEADS),
        in_specs=[col(0), col(1), col(2), col(3), vec, vec],
        out_specs=pl.BlockSpec((1, S, HG_DV), lambda b, h: (b, 0, h)),
        compiler_params=_cparams(("arbitrary", "arbitrary")),
        name="hgrn",
    )(hg, hg, hg, hg, lb, ng)


def _cmpmlp_kernel(x_ref, w1a_ref, w1b_ref, pe_ref, w1_ref, b1_ref, w2_ref, o_ref):
    x = x_ref[0]
    hb = jnp.dot(pe_ref[...], w1_ref[...], precision=HIGHEST, preferred_element_type=F32)[0:1] + b1_ref[...]
    nrow = x.shape[0]
    for g in range(NSA_KV_GROUPS):
        a = jnp.dot(x, w1a_ref[g], preferred_element_type=F32)
        bm = jnp.dot(x, w1b_ref[g], preferred_element_type=F32)
        hdn = a + pltpu.roll(bm, nrow - 1, 0) + hb
        o_ref[0, g] = jnp.dot(_gelu_tanh(hdn).astype(BF16), w2_ref[g], preferred_element_type=F32).astype(BF16)


def _cmpmlp(x2, w1a, w1b, pe8, w1, b1, w2p):
    B, ncb, width = x2.shape
    full = lambda a: pl.BlockSpec(a.shape, lambda b: (0,) * a.ndim)
    return pl.pallas_call(
        _cmpmlp_kernel,
        out_shape=jax.ShapeDtypeStruct((B, NSA_KV_GROUPS, ncb, LANES), BF16),
        grid=(B,),
        in_specs=[pl.BlockSpec((1, ncb, width), lambda b: (b, 0, 0)),
                  full(w1a), full(w1b), full(pe8), full(w1), full(b1), full(w2p)],
        out_specs=pl.BlockSpec((1, NSA_KV_GROUPS, ncb, LANES), lambda b: (b, 0, 0, 0)),
        compiler_params=_cparams(("arbitrary",)),
        name="cmpmlp",
    )(x2, w1a, w1b, pe8, w1, b1, w2p)


def _rank_before(score, nrows):
    jrow = lax.broadcasted_iota(jnp.int32, score.shape, 0)
    rank = jnp.zeros(score.shape, F32)
    for k in range(nrows):
        rk = score[k:k + 1]
        beats = (rk > score) | ((rk == score) & (jrow > k))
        rank = rank + jnp.where(beats, 1.0, 0.0)
    return rank


def _topk_rows(score, k):
    n = score.shape[0]
    row = lax.broadcasted_iota(jnp.int32, score.shape, 0).astype(F32)
    keep = jnp.zeros(score.shape, F32)
    for _ in range(k):
        top = jnp.max(score, axis=0, keepdims=True)
        first = jnp.min(jnp.where(score == top, row, float(n)), axis=0, keepdims=True)
        pick = row == first
        keep = jnp.where(pick, 1.0, keep)
        score = jnp.where(pick, -jnp.inf, score)
    return keep


def _transpose_via_mxu(xt):
    c = xt.shape[1]
    eye = (lax.broadcasted_iota(jnp.int32, (c, c), 0) == lax.broadcasted_iota(jnp.int32, (c, c), 1)).astype(F32)
    return lax.dot_general(eye, xt, NT, precision=HIGHEST, preferred_element_type=F32)


def _pack_heads(heads):
    low = lax.broadcasted_iota(jnp.int32, heads[0].shape, 1) < NSA_HD
    return jnp.concatenate([jnp.where(low, heads[k], pltpu.roll(heads[k + 1], NSA_HD, 1))
                            for k in range(0, len(heads), 2)], axis=1)


def _cmpsel_kernel(q_ref, kc_ref, vc_ref, gate_ref, o_ref, sel_ref):
    tq = q_ref.shape[1]
    ncb = kc_ref.shape[2]
    t = pl.program_id(2)
    kc = kc_ref[0, 0]
    vc = vc_ref[0, 0]
    pos = t * tq + lax.broadcasted_iota(jnp.int32, (tq, 1), 0)
    cblk = lax.broadcasted_iota(jnp.int32, (1, ncb), 1)
    vis = (cblk * CMP_STRIDE + CMP_BLOCK - 1) <= pos
    psum = jnp.zeros((tq, ncb), F32)
    gate = gate_ref[0]
    heads = []
    for h in range(NSA_HG):
        s = lax.dot_general(q_ref[0, :, h * LANES:(h + 1) * LANES], kc, NT, preferred_element_type=F32)
        s = jnp.where(vis, s, NEG)
        p = jnp.exp(s - jnp.max(s, axis=-1, keepdims=True))
        p = jnp.where(vis, p / jnp.sum(p, axis=-1, keepdims=True), 0.0)
        heads.append(jnp.dot(p.astype(BF16), vc, preferred_element_type=F32) * gate[:, 3 * h:3 * h + 1])
        psum = psum + p
    o_ref[0] = _pack_heads(heads)
    nsb = ncb * CMP_STRIDE // SEL_BLOCK
    jb = lax.broadcasted_iota(jnp.int32, (nsb, ncb), 0) * SEL_BLOCK
    cb = lax.broadcasted_iota(jnp.int32, (nsb, ncb), 1) * CMP_STRIDE
    ov = jnp.maximum(jnp.minimum(cb + CMP_BLOCK, jb + SEL_BLOCK) - jnp.maximum(cb, jb), 0).astype(F32) / CMP_BLOCK
    pslc_t = lax.dot_general(ov, psum, NT, precision=HIGHEST, preferred_element_type=F32)
    posl = t * tq + lax.broadcasted_iota(jnp.int32, (1, tq), 1)
    cur = posl // SEL_BLOCK
    jrow = lax.broadcasted_iota(jnp.int32, (nsb, tq), 0)
    forced = (jrow == 0) | (jrow == cur) | (jrow == cur - 1)
    score = jnp.where(forced, 1e30, jnp.where(jrow <= cur, pslc_t, NEG))
    chosen = _topk_rows(score, min(N_SEL, nsb)) > 0.5
    drop = jnp.where(chosen & (score > -1e29), 0.0, 1.0)
    drop = jnp.concatenate([drop, jnp.zeros((LANES - nsb, tq), F32)], axis=0)
    sel_ref[0, 0] = (_transpose_via_mxu(drop) * NEG).astype(BF16)


def _cmpsel(qraw, kcmp, vcmp, gate, tq):
    B, S, _ = qraw.shape
    ncb = kcmp.shape[2]
    gw = NSA_HG * LANES
    assert S // SEL_BLOCK <= NSA_HD
    cmp_spec = pl.BlockSpec((1, 1, ncb, LANES), lambda b, g, t: (b, g, 0, 0))
    return pl.pallas_call(
        _cmpsel_kernel,
        out_shape=(jax.ShapeDtypeStruct((B, S, NSA_WIDTH), F32),
                   jax.ShapeDtypeStruct((B, NSA_KV_GROUPS, S, LANES), BF16)),
        grid=(B, NSA_KV_GROUPS, S // tq),
        in_specs=[pl.BlockSpec((1, tq, gw), lambda b, g, t: (b, t, g)), cmp_spec, cmp_spec,
                  pl.BlockSpec((1, tq, LANES), lambda b, g, t: (b, t, g))],
        out_specs=(pl.BlockSpec((1, tq, NSA_WIDTH // NSA_KV_GROUPS), lambda b, g, t: (b, t, g)),
                   pl.BlockSpec((1, 1, tq, LANES), lambda b, g, t: (b, g, t, 0))),
        compiler_params=_cparams(("arbitrary", "arbitrary", "arbitrary")),
        name="cmpsel",
    )(qraw, kcmp, vcmp, gate)


def _slcwin_kernel(q_ref, sel_ref, ks_ref, kw_ref, vs_ref, vw_ref, gate_ref, ocmp_ref, o_ref, *, tk, wk):
    tq = q_ref.shape[1]
    S = ks_ref.shape[1]
    t = pl.program_id(2)
    q0 = t * tq
    qpos = q0 + lax.broadcasted_iota(jnp.int32, (tq, 1), 0)
    qw = [q_ref[0, :, h * LANES:(h + 1) * LANES] for h in range(NSA_HG)]
    sel = sel_ref[0, 0]
    qa = [jnp.concatenate([q, sel], axis=1) for q in qw]

    gate = gate_ref[0]

    def normalise(acc, g):
        return acc * (g / acc[:, NSA_HD:NSA_HD + 1])

    def kstep(j, carry, diag):
        k0 = pl.multiple_of(j * tk, tk)
        ks = ks_ref[0, pl.ds(k0, tk), :]
        vs = vs_ref[0, pl.ds(k0, tk), :]
        if diag:
            vis = (k0 + lax.broadcasted_iota(jnp.int32, (1, tk), 1)) <= qpos
        out = []
        for h in range(NSA_HG):
            m, acc = carry[h]
            s = lax.dot_general(qa[h], ks, NT, preferred_element_type=F32)
            if diag:
                s = jnp.where(vis, s, NEG)
            mn = jnp.maximum(m, jnp.max(s, axis=-1, keepdims=True))
            p = jnp.exp2(s - mn)
            acc = jnp.exp2(m - mn) * acc + jnp.dot(p.astype(BF16), vs, preferred_element_type=F32)
            out.append((mn, acc))
        return tuple(out)

    init = tuple((jnp.full((tq, 1), NEG, F32), jnp.zeros((tq, LANES), F32)) for _ in range(NSA_HG))
    jdiag = q0 // tk
    carry = lax.fori_loop(0, jdiag, lambda j, c: kstep(j, c, False), init)
    carry = kstep(jdiag, carry, True)
    heads = [normalise(carry[h][1], gate[:, 3 * h + 1:3 * h + 2]) for h in range(NSA_HG)]

    nblk = S // tq
    start = pl.multiple_of(jnp.clip(t - WINDOW // tq, 0, nblk - wk // tq) * tq, tq)
    kw = kw_ref[0, pl.ds(start, wk), :]
    vw = vw_ref[0, pl.ds(start, wk), :]
    d = qpos - (start + lax.broadcasted_iota(jnp.int32, (1, wk), 1))
    okw = (d >= 0) & (d < WINDOW)
    for h in range(NSA_HG):
        s = jnp.where(okw, lax.dot_general(qw[h], kw, NT, preferred_element_type=F32), NEG)
        p = jnp.exp2(s - jnp.max(s, axis=-1, keepdims=True))
        heads[h] = heads[h] + normalise(jnp.dot(p.astype(BF16), vw, preferred_element_type=F32),
                                        gate[:, 3 * h + 2:3 * h + 3])
    o_ref[0] = (ocmp_ref[0] + _pack_heads(heads)).astype(BF16)


def _slcwin(qrot, sel, ks, kw, vaug, gate, ocmp, tq, tk):
    B, S, _ = qrot.shape
    gw = NSA_HG * LANES
    wk = min(WINDOW + tq, S)
    ospec = pl.BlockSpec((1, tq, NSA_WIDTH // NSA_KV_GROUPS), lambda b, g, t: (b, t, g))
    return pl.pallas_call(
        functools.partial(_slcwin_kernel, tk=tk, wk=wk),
        out_shape=jax.ShapeDtypeStruct((B, S, NSA_WIDTH), BF16),
        grid=(B, NSA_KV_GROUPS, S // tq),
        in_specs=[pl.BlockSpec((1, tq, gw), lambda b, g, t: (b, t, g)),
                  pl.BlockSpec((1, 1, tq, LANES), lambda b, g, t: (b, g, t, 0)),
                  pl.BlockSpec((1, S, 2 * LANES), lambda b, g, t: (b, 0, 0)),
                  pl.BlockSpec((1, S, LANES), lambda b, g, t: (b, 0, 0)),
                  pl.BlockSpec((1, S, LANES), lambda b, g, t: (b, 0, g)),
                  pl.BlockSpec((1, S, LANES), lambda b, g, t: (b, 0, NSA_KV_GROUPS + g)),
                  pl.BlockSpec((1, tq, LANES), lambda b, g, t: (b, t, g)),
                  ospec],
        out_specs=ospec,
        compiler_params=_cparams(("arbitrary", "arbitrary", "arbitrary")),
        name="slcwin",
    )(qrot, sel, ks, kw, vaug, vaug, gate, ocmp)


def _outproj_kernel(x_ref, ohg_ref, onsa_ref, wout_ref,
                    g1_ref, sh2_ref, sc2_ref, n2_ref, rwt_ref, rb_ref,
                    x1_ref, h2_ref, sc_ref, slot_t_ref, w_t_ref, cnt_ref):
    tm = x_ref.shape[1]
    mix = jnp.dot(jnp.concatenate([ohg_ref[0], onsa_ref[0]], axis=1), wout_ref[...], preferred_element_type=F32)
    x1 = x_ref[0] + g1_ref[0] * mix
    x1_ref[0] = x1
    h2 = (x1 * lax.rsqrt(jnp.mean(x1 * x1, axis=-1, keepdims=True) + EPS) * n2_ref[...]) * (1.0 + sc2_ref[0]) + sh2_ref[0]
    h2_ref[0] = h2.astype(BF16)
    scores = _sigmoid(lax.dot_general(rwt_ref[...], h2, NT, precision=HIGHEST, preferred_element_type=F32))
    choice = scores + rb_ref[...]
    per = N_EXPERTS // N_GROUPS
    c3 = choice.reshape(N_GROUPS, per, tm)
    erow = lax.broadcasted_iota(jnp.int32, c3.shape, 1)
    rank_in = jnp.zeros(c3.shape, F32)
    for k in range(per):
        ck = c3[:, k:k + 1, :]
        rank_in = rank_in + jnp.where((ck > c3) | ((ck == c3) & (erow > k)), 1.0, 0.0)
    grp_score = jnp.sum(jnp.where(rank_in < 2, c3, 0.0), axis=1)
    grp_keep = _rank_before(grp_score, N_GROUPS) < TOPK_GROUPS
    masked = jnp.where(grp_keep[:, None, :], c3, -jnp.inf).reshape(N_EXPERTS, tm)
    keep = _topk_rows(masked, TOP_K) > 0.5
    tw = jnp.where(keep, scores, 0.0)
    tw = tw / jnp.sum(tw, axis=0, keepdims=True) * ROUTED_SCALE
    before = lax.broadcasted_iota(jnp.int32, (tm, tm), 0)
    after = lax.broadcasted_iota(jnp.int32, (tm, tm), 1)
    earlier = (before < after) & (before // MOE_SUB == after // MOE_SUB)
    kept = jnp.where(keep, 1.0, 0.0)
    pref = jnp.dot(kept.astype(BF16), earlier.astype(BF16), preferred_element_type=F32)
    slot = jnp.where(keep, pref, -1.0)
    sc_ref[0] = _transpose_via_mxu(slot)
    for j in range(tm // MOE_SUB):
        sub = slice(j * MOE_SUB, (j + 1) * MOE_SUB)
        slot_t_ref[:, j] = slot[:, sub].reshape(N_EXPERTS // MOE_EG, MOE_EG, MOE_SUB)
        w_t_ref[:, j] = tw[:, sub].reshape(N_EXPERTS // MOE_EG, MOE_EG, MOE_SUB)
        cnt_ref[j] = jnp.broadcast_to(jnp.sum(kept[:, sub], axis=1, keepdims=True), (N_EXPERTS, LANES))


def _outproj(x, ohg, onsa, wout, mod3, n2, rwt, rb, tm):
    B, S, D = x.shape
    blk = lambda w: pl.BlockSpec((1, tm, w), lambda b, i: (b, i, 0))
    full = lambda a: pl.BlockSpec(a.shape, lambda b, i: (0,) * a.ndim)
    modc = lambda k: pl.BlockSpec((1, 1, D), lambda b, i, k=k: (b, 0, k))
    nt = S // tm
    ng = N_EXPERTS // MOE_EG
    nh = tm // MOE_SUB
    tspec = pl.BlockSpec((ng, nh, MOE_EG, MOE_SUB), lambda b, i: (0, b * nt + i, 0, 0))
    return pl.pallas_call(
        _outproj_kernel,
        out_shape=(jax.ShapeDtypeStruct((B, S, D), F32),
                   jax.ShapeDtypeStruct((B, S, D), BF16),
                   jax.ShapeDtypeStruct((B, S, N_EXPERTS), F32),
                   jax.ShapeDtypeStruct((ng, B * nt * nh, MOE_EG, MOE_SUB), F32),
                   jax.ShapeDtypeStruct((ng, B * nt * nh, MOE_EG, MOE_SUB), F32),
                   jax.ShapeDtypeStruct((B * nt * nh, N_EXPERTS, LANES), F32)),
        grid=(B, nt),
        in_specs=[blk(D), blk(HG_WIDTH), blk(NSA_WIDTH), full(wout),
                  modc(2), modc(3), modc(4), full(n2), full(rwt), full(rb)],
        out_specs=(blk(D), blk(D), blk(N_EXPERTS), tspec, tspec,
                   pl.BlockSpec((nh, N_EXPERTS, LANES), lambda b, i: (b * nt + i, 0, 0))),
        compiler_params=_cparams(("arbitrary", "arbitrary")),
        name="outproj",
    )(x, ohg, onsa, wout, mod3, mod3, mod3, n2, rwt, rb)


def _swiglu_hidden(x, wgu):
    gu = jnp.dot(x, wgu, preferred_element_type=F32)
    return _silu(gu[:, :EXPERT_HIDDEN]) * gu[:, EXPERT_HIDDEN:]


def _moe_kernel(rounds_ref, h_ref, slot_t_ref, w_t_ref, slot_c_ref, wgu_ref, wdn_ref, sgu_ref, sdn_ref,
                o_ref, x_scr, y_scr, p_scr):
    i = pl.program_id(0)
    g = pl.program_id(1)
    tm = h_ref.shape[0]
    nsub = tm // MOE_SUB
    rnd, eg = MOE_ROUND, MOE_EG
    ng = N_EXPERTS // eg
    base = i * (N_EXPERTS + ng)

    @pl.when(g == 0)
    def _():
        act = _swiglu_hidden(h_ref[...], sgu_ref[...]).astype(BF16)
        o_ref[...] = jnp.dot(act, sdn_ref[...], preferred_element_type=F32)

    rslot = lax.broadcasted_iota(jnp.int32, (rnd, 1), 0).astype(F32)
    lane_slot = (lax.broadcasted_iota(jnp.int32, (1, eg * rnd), 1) % rnd).astype(F32)
    spread = (lax.broadcasted_iota(jnp.int32, (N_EXPERTS, eg * rnd), 1) // rnd + g * eg
              == lax.broadcasted_iota(jnp.int32, (N_EXPERTS, eg * rnd), 0)).astype(BF16)

    def one_round(r, carry):
        first = (r * rnd).astype(F32)
        for s in range(nsub):
            st = slot_t_ref[0, s] - first
            p = jnp.concatenate([jnp.where(rslot == st[e:e + 1, :], 1.0, 0.0) for e in range(eg)], axis=0)
            p_scr[s] = p.astype(BF16)
            x_scr[s] = jnp.dot(p_scr[s], h_ref[s * MOE_SUB:(s + 1) * MOE_SUB, :],
                               preferred_element_type=F32).astype(BF16)
        for e in range(eg):
            rows = slice(e * rnd, (e + 1) * rnd)

            @pl.when(r < rounds_ref[base + g * eg + e])
            def _():
                xe = jnp.concatenate([x_scr[s, rows, :] for s in range(nsub)], axis=0)
                wslot = jnp.concatenate(
                    [jnp.sum(p_scr[s, rows, :].astype(F32) * w_t_ref[0, s][e:e + 1, :], axis=-1, keepdims=True)
                     for s in range(nsub)], axis=0)
                y = jnp.dot((_swiglu_hidden(xe, wgu_ref[e]) * wslot).astype(BF16), wdn_ref[e],
                            preferred_element_type=F32)
                for s in range(nsub):
                    y_scr[s, rows, :] = y[s * rnd:(s + 1) * rnd].astype(BF16)

            @pl.when(r >= rounds_ref[base + g * eg + e])
            def _():
                for s in range(nsub):
                    y_scr[s, rows, :] = jnp.zeros((rnd, D_MODEL), BF16)
        for s in range(nsub):
            toks = slice(s * MOE_SUB, (s + 1) * MOE_SUB)
            sc = jnp.dot(slot_c_ref[toks, :].astype(BF16), spread, preferred_element_type=F32) - first
            pt = jnp.where(sc == lane_slot, 1.0, 0.0).astype(BF16)
            o_ref[toks, :] += jnp.dot(pt, y_scr[s], preferred_element_type=F32)
        return carry

    lax.fori_loop(0, rounds_ref[base + N_EXPERTS + g], one_round, 0)


def _moe(rounds, h2, slot_t, w_t, slot_c, wgu, wdn, sgu, sdn, tm):
    T, D = h2.shape
    eg, nsub = MOE_EG, tm // MOE_SUB
    full = lambda a: pl.BlockSpec(a.shape, lambda i, e, o: (0,) * a.ndim)
    tspec = pl.BlockSpec((1, nsub, eg, MOE_SUB), lambda i, e, o: (e, i, 0, 0))
    return pl.pallas_call(
        _moe_kernel,
        out_shape=jax.ShapeDtypeStruct((T, D), F32),
        grid_spec=pltpu.PrefetchScalarGridSpec(
            num_scalar_prefetch=1,
            grid=(T // tm, N_EXPERTS // eg),
            in_specs=[pl.BlockSpec((tm, D), lambda i, e, o: (i, 0), pipeline_mode=pl.Buffered(1)), tspec, tspec,
                      pl.BlockSpec((tm, N_EXPERTS), lambda i, e, o: (i, 0)),
                      pl.BlockSpec((eg, D, 2 * EXPERT_HIDDEN), lambda i, e, o: (e, 0, 0)),
                      pl.BlockSpec((eg, EXPERT_HIDDEN, D), lambda i, e, o: (e, 0, 0)),
                      full(sgu), full(sdn)],
            out_specs=pl.BlockSpec((tm, D), lambda i, e, o: (i, 0), pipeline_mode=pl.Buffered(1)),
            scratch_shapes=[pltpu.VMEM((nsub, eg * MOE_ROUND, D), BF16),
                            pltpu.VMEM((nsub, eg * MOE_ROUND, D), BF16),
                            pltpu.VMEM((nsub, eg * MOE_ROUND, MOE_SUB), BF16)]),
        compiler_params=_cparams(("arbitrary", "arbitrary")),
        name="moe",
    )(rounds, h2, slot_t, w_t, slot_c, wgu, wdn, sgu, sdn)


def _final_kernel(x1_ref, moe_ref, g2_ref, fg_ref, o_ref):
    x2 = x1_ref[0] + g2_ref[0] * moe_ref[0]
    o_ref[0] = x2 * lax.rsqrt(jnp.mean(x2 * x2, axis=-1, keepdims=True) + EPS) * fg_ref[...]


def _final(x1, moe, mod3, fg, tm):
    B, S, D = x1.shape
    blk = pl.BlockSpec((1, tm, D), lambda b, i: (b, i, 0))
    return pl.pallas_call(
        _final_kernel,
        out_shape=jax.ShapeDtypeStruct((B, S, D), F32),
        grid=(B, S // tm),
        in_specs=[blk, blk, pl.BlockSpec((1, 1, D), lambda b, i: (b, 0, 5)), pl.BlockSpec((1, D), lambda b, i: (0, 0))],
        out_specs=blk,
        compiler_params=_cparams(("arbitrary", "arbitrary")),
        name="final",
    )(x1, moe, mod3, fg)


def _pack_w_in(w_in):
    return jnp.pad(w_in, ((0, 0), (0, IN_COLS_P - w_in.shape[1]))).astype(BF16)


def _pack_cmp(pos, w1, b1, w2, lane_by_group):
    half = CMP_STRIDE * NSA_HD
    def rows_for(wh):
        w3 = wh.reshape(CMP_STRIDE, NSA_HD, CMP_HIDDEN)
        z = jnp.zeros_like(w3)
        return jnp.stack([jnp.concatenate([w3, z], axis=1).reshape(CMP_STRIDE * LANES, CMP_HIDDEN),
                          jnp.concatenate([z, w3], axis=1).reshape(CMP_STRIDE * LANES, CMP_HIDDEN)])
    w1a = rows_for(w1[:half]).astype(BF16)
    w1b = rows_for(w1[half:]).astype(BF16)
    z2 = jnp.zeros_like(w2)
    w2_first = jnp.concatenate([w2, z2], axis=1)
    w2p = jnp.stack([w2_first, jnp.concatenate([z2, w2], axis=1) if lane_by_group else w2_first]).astype(BF16)
    pe8 = jnp.pad(pos.reshape(1, CMP_BLOCK * NSA_HD), ((0, 7), (0, 0)))
    return w1a, w1b, pe8, w1, b1.reshape(1, CMP_HIDDEN), w2p


def _rope_tables(S):
    half = NSA_HD // 2
    inv = ROPE_THETA ** (-jnp.arange(half, dtype=F32) / half)
    ang = jnp.arange(S, dtype=F32)[:, None] * inv[None, :]
    cos, sin = jnp.cos(ang), jnp.sin(ang)
    reps = LANES // NSA_HD
    return jnp.tile(jnp.concatenate([cos, cos], axis=1), (1, reps)), jnp.tile(jnp.concatenate([-sin, sin], axis=1), (1, reps))


def _tiles(S):
    return dict(inproj=min(512, S), hgrn=min(512, S), cmpsel=min(512, S), slc_q=512, slc_k=512, moe=min(2048, S), final=min(512, S))


def kernel(x, c, w_ada, b_ada, norm1_g, w_in, hg_lb_logits, hg_norm_g, cmp_pos_k, cmp_w1_k, cmp_b1_k, cmp_w2_k,
           cmp_pos_v, cmp_w1_v, cmp_b1_v, cmp_w2_v, w_out, norm2_g, router_w, router_bias, w_exp_gu, w_exp_dn,
           w_sh_gu, w_sh_dn, final_g):
    B, S, D = x.shape
    assert D == D_MODEL and w_ada.shape[0] == 1 and S % 512 == 0
    tl = _tiles(S)
    l = 0
    lb = jnp.cumsum(jax.nn.softmax(hg_lb_logits.astype(F32), axis=0), axis=0)[l].reshape(1, HG_WIDTH)
    c8 = jnp.pad(c, ((0, 8 - B), (0, 0)))
    mod3 = _ada(c8, w_ada[l], b_ada[l].reshape(1, -1))[:B].reshape(B, 1, 6 * D)
    cos, sin = _rope_tables(S)
    hg, qraw, qrot, kc, vc, ks, kw, vaug, gate = _inproj(x, mod3, norm1_g[l].reshape(1, D), _pack_w_in(w_in[l]),
                                                         cos, sin, tl["inproj"])
    ng = hg_norm_g[l].reshape(1, HG_WIDTH)
    mxu_safe = -HG_SUB * jnp.log(jnp.min(lb)) <= HG_SAFE_LOG_RANGE
    ohg = lax.cond(mxu_safe, lambda: _hgrn_mxu(hg, lb, ng, tl["hgrn"]), lambda: _hgrn(hg, lb, ng))
    ncb = S // CMP_STRIDE
    kcmp = _cmpmlp(kc.reshape(B, ncb, CMP_STRIDE * LANES),
                   *_pack_cmp(cmp_pos_k[l], cmp_w1_k[l], cmp_b1_k[l], cmp_w2_k[l], True))
    vcmp = _cmpmlp(vc.reshape(B, ncb, CMP_STRIDE * LANES),
                   *_pack_cmp(cmp_pos_v[l], cmp_w1_v[l], cmp_b1_v[l], cmp_w2_v[l], False))
    ocmp, sel = _cmpsel(qraw, kcmp, vcmp, gate, tl["cmpsel"])
    onsa = _slcwin(qrot, sel, ks, kw, vaug, gate, ocmp, tl["slc_q"], tl["slc_k"])
    x1, h2, sc, slot_t, w_t, cnt = _outproj(x, ohg, onsa, w_out[l].astype(BF16), mod3,
                                            norm2_g[l].reshape(1, D), router_w[l].T,
                                            router_bias[l].reshape(N_EXPERTS, 1), ROUTE_TM)
    T = B * S
    tm = tl["moe"]
    ng = N_EXPERTS // MOE_EG
    slot_c = sc.reshape(T, N_EXPERTS)
    load = jnp.max(cnt[:, :, 0].reshape(T // tm, tm // MOE_SUB, N_EXPERTS), axis=1).astype(jnp.int32)
    per_expert = (load + MOE_ROUND - 1) // MOE_ROUND
    rounds = jnp.concatenate([per_expert, jnp.max(per_expert.reshape(T // tm, ng, MOE_EG), axis=2)], axis=1).reshape(-1)
    moe = _moe(rounds, h2.reshape(T, D), slot_t, w_t, slot_c,
               w_exp_gu[l].astype(BF16), w_exp_dn[l].astype(BF16), w_sh_gu[l].astype(BF16), w_sh_dn[l].astype(BF16), tm)
    return _final(x1, moe.reshape(B, S, D), mod3, final_g.reshape(1, D), tl["final"])
```

```python
import functools

import numpy as np
import jax
import jax.numpy as jnp
from jax import lax
from jax.experimental import pallas as pl
from jax.experimental.pallas import tpu as pltpu

F32 = jnp.float32
BF16 = jnp.bfloat16
HIGHEST = lax.Precision.HIGHEST

D_MODEL = 1024
EPS = 1e-6
HG_HEADS = 4
HG_DK = 128
HG_DV = 128
HG_WIDTH = HG_HEADS * HG_DV
HG_CHUNK = 64
HG_SUB = 16
HG_SAFE_LOG_RANGE = 80.0
NSA_HEADS = 8
NSA_KV_GROUPS = 2
NSA_HG = NSA_HEADS // NSA_KV_GROUPS
NSA_HD = 64
NSA_WIDTH = NSA_HEADS * NSA_HD
NSA_KV = NSA_KV_GROUPS * NSA_HD
CMP_BLOCK = 32
CMP_STRIDE = 16
CMP_HIDDEN = 256
SEL_BLOCK = 64
N_SEL = 16
WINDOW = 512
ROPE_THETA = 10000.0
N_EXPERTS = 64
TOP_K = 8
N_GROUPS = 8
TOPK_GROUPS = 4
EXPERT_HIDDEN = 256
ROUTED_SCALE = 2.5
ROUTE_TM = 512
MOE_SUB = 256
MOE_ROUND = 32
MOE_EG = 8

LANES = 128
NEG = -1e30
VMEM_LIMIT = 56 * 1024 * 1024

C_HG = 0
C_Q = 4 * HG_WIDTH
C_K = C_Q + NSA_WIDTH
C_GATE = C_K + 6 * NSA_KV
IN_COLS_P = C_GATE + LANES
LOG2E = 1.4426950408889634

NT = (((1,), (1,)), ((), ()))
TN = (((0,), (0,)), ((), ()))


def _cparams(sem):
    return pltpu.CompilerParams(dimension_semantics=sem, vmem_limit_bytes=VMEM_LIMIT)


def _sigmoid(x):
    return 1.0 / (1.0 + jnp.exp(-x))


def _silu(x):
    return x * _sigmoid(x)


def _gelu_tanh(x):
    return 0.5 * x * (1.0 + jnp.tanh(0.7978845608028654 * (x + 0.044715 * (x * x * x))))


def _ada_kernel(c_ref, w_ref, b_ref, o_ref):
    a = _silu(c_ref[...])
    o_ref[...] = jnp.dot(a, w_ref[...], precision=HIGHEST, preferred_element_type=F32) + b_ref[...]


def _ada(c8, w, b):
    n = w.shape[1]
    tn = 1024
    return pl.pallas_call(
        _ada_kernel,
        out_shape=jax.ShapeDtypeStruct((8, n), F32),
        grid=(n // tn,),
        in_specs=[pl.BlockSpec((8, D_MODEL), lambda j: (0, 0)),
                  pl.BlockSpec((D_MODEL, tn), lambda j: (0, j)),
                  pl.BlockSpec((1, tn), lambda j: (0, j))],
        out_specs=pl.BlockSpec((8, tn), lambda j: (0, j)),
        compiler_params=_cparams(("arbitrary",)),
        name="ada",
    )(c8, w, b)


def _rope(t, cos, sin_signed, first_half):
    rot = jnp.where(first_half, pltpu.roll(t, 96, 1), pltpu.roll(t, 32, 1))
    return t * cos + rot * sin_signed


def _inproj_kernel(x_ref, sh_ref, sc_ref, g_ref, w_ref, cos_ref, sin_ref,
                   hg_ref, qraw_ref, qrot_ref, kc_ref, vc_ref, ks_ref, kw_ref, va_ref, gate_ref, h_scr):
    tm = x_ref.shape[1]
    x = x_ref[0]
    y = x * lax.rsqrt(jnp.mean(x * x, axis=-1, keepdims=True) + EPS) * g_ref[...]
    h_scr[...] = (y * (1.0 + sc_ref[0]) + sh_ref[0]).astype(BF16)

    def mm(lo, width):
        return jnp.dot(h_scr[...], w_ref[:, lo:lo + width], preferred_element_type=F32)

    cos = cos_ref[...]
    sin = sin_ref[...]
    first_half = (lax.broadcasted_iota(jnp.int32, cos.shape, 1) % NSA_HD) < (NSA_HD // 2)
    lane = lax.broadcasted_iota(jnp.int32, (tm, LANES), 1)
    half_of_lane = lane // NSA_HD

    def own_lanes(t, src_half, dst_half):
        moved = t if src_half == dst_half else pltpu.roll(t, NSA_HD, 1)
        return jnp.where(half_of_lane == dst_half, moved, 0.0)

    for j in range(4):
        hg_ref[0, :, j * HG_WIDTH:(j + 1) * HG_WIDTH] = mm(C_HG + j * HG_WIDTH, HG_WIDTH)
    for m in range(NSA_WIDTH // (2 * LANES)):
        qq = mm(C_Q + m * 2 * LANES, 2 * LANES) * (NSA_HD ** -0.5)
        for c in range(2):
            q = qq[:, c * LANES:(c + 1) * LANES]
            qr = _rope(q, cos, sin, first_half) * LOG2E
            for half in range(2):
                n = (2 * m + c) * 2 + half
                g = n // NSA_HG
                qraw_ref[0, :, n * LANES:(n + 1) * LANES] = own_lanes(q, half, g).astype(BF16)
                qrot_ref[0, :, n * LANES:(n + 1) * LANES] = own_lanes(qr, half, g).astype(BF16)
    kv = mm(C_K, 2 * LANES)
    kc_ref[0] = kv[:, :LANES].astype(BF16)
    vc_ref[0] = kv[:, LANES:].astype(BF16)
    pos = pl.program_id(1) * tm + lax.broadcasted_iota(jnp.int32, (tm, LANES), 0)
    ks_ref[0, :, LANES:2 * LANES] = jnp.where(lane == pos // SEL_BLOCK, 1.0, 0.0).astype(BF16)
    for j in range(2):
        kv = mm(C_K + (j + 1) * 2 * LANES, 2 * LANES)
        k_rot = _rope(kv[:, :LANES], cos, sin, first_half).astype(BF16)
        if j == 0:
            ks_ref[0, :, 0:LANES] = k_rot
        else:
            kw_ref[0] = k_rot
        for g in range(NSA_KV_GROUPS):
            v = own_lanes(kv[:, LANES:], g, 0)
            col = (j * NSA_KV_GROUPS + g) * LANES
            va_ref[0, :, col:col + LANES] = jnp.where(lane == NSA_HD, 1.0, v).astype(BF16)
    gate = _sigmoid(mm(C_GATE, LANES))
    per_group = 3 * NSA_HG
    for g in range(NSA_KV_GROUPS):
        gate_ref[0, :, g * LANES:(g + 1) * LANES] = gate if g == 0 else pltpu.roll(gate, LANES - g * per_group, 1)


def _inproj(x, mod3, norm_g, w_p, cos, sin, tm):
    B, S, D = x.shape
    blk = lambda w: pl.BlockSpec((1, tm, w), lambda b, i: (b, i, 0))
    return pl.pallas_call(
        _inproj_kernel,
        out_shape=(jax.ShapeDtypeStruct((B, S, 4 * HG_WIDTH), F32),
                   jax.ShapeDtypeStruct((B, S, NSA_HEADS * LANES), BF16),
                   jax.ShapeDtypeStruct((B, S, NSA_HEADS * LANES), BF16),
                   jax.ShapeDtypeStruct((B, S, LANES), BF16),
                   jax.ShapeDtypeStruct((B, S, LANES), BF16),
                   jax.ShapeDtypeStruct((B, S, 2 * LANES), BF16),
                   jax.ShapeDtypeStruct((B, S, LANES), BF16),
                   jax.ShapeDtypeStruct((B, S, 4 * LANES), BF16),
                   jax.ShapeDtypeStruct((B, S, NSA_KV_GROUPS * LANES), F32)),
        grid=(B, S // tm),
        in_specs=[blk(D),
                  pl.BlockSpec((1, 1, D), lambda b, i: (b, 0, 0)),
                  pl.BlockSpec((1, 1, D), lambda b, i: (b, 0, 1)),
                  pl.BlockSpec((1, D), lambda b, i: (0, 0)),
                  pl.BlockSpec((D, IN_COLS_P), lambda b, i: (0, 0)),
                  pl.BlockSpec((tm, LANES), lambda b, i: (i, 0)),
                  pl.BlockSpec((tm, LANES), lambda b, i: (i, 0))],
        out_specs=(blk(4 * HG_WIDTH), blk(NSA_HEADS * LANES), blk(NSA_HEADS * LANES),
                   blk(LANES), blk(LANES), blk(2 * LANES), blk(LANES), blk(4 * LANES), blk(NSA_KV_GROUPS * LANES)),
        scratch_shapes=[pltpu.VMEM((tm, D), BF16)],
        compiler_params=_cparams(("arbitrary", "arbitrary")),
        name="inproj",
    )(x, mod3, mod3, norm_g, w_p, cos, sin)


def _hgrn_kernel(q_ref, f_ref, i_ref, gt_ref, lb_ref, ng_ref, o_ref):
    S = q_ref.shape[1]
    C, U = HG_CHUNK, HG_SUB
    lb = lb_ref[...]
    ng = ng_ref[...]
    ri = lax.broadcasted_iota(jnp.int32, (C, C), 0)
    ci = lax.broadcasted_iota(jnp.int32, (C, C), 1)
    tril = (ri >= ci).astype(F32)
    trow = lax.broadcasted_iota(jnp.int32, (U, 1), 0)

    def chunk(c, st_t):
        r0 = pl.multiple_of(c * C, C)
        rows = pl.ds(r0, C)
        f = lb + (1.0 - lb) * _sigmoid(f_ref[0, rows, :])
        kk = 1.0 - f
        b = jnp.dot(tril, jnp.log(f), precision=HIGHEST, preferred_element_type=F32)
        q = q_ref[0, rows, :] * (HG_DK ** -0.5)
        v = i_ref[0, rows, :]
        vb = v.astype(BF16)
        o_inter = lax.dot_general((q * jnp.exp(b)).astype(BF16), st_t.astype(BF16), NT,
                                  preferred_element_type=F32)
        parts = []
        for i in range(C // U):
            lo = i * U
            bi = b[lo:lo + U]
            qi = q[lo:lo + U]
            if i == 0:
                oi = jnp.zeros((U, HG_DV), F32)
            else:
                r = b[lo - 1:lo]
                qrel = (qi * jnp.exp(bi - r)).astype(BF16)
                kprev = (kk[:lo] * jnp.exp(r - b[:lo])).astype(BF16)
                a_off = lax.dot_general(qrel, kprev, NT, preferred_element_type=F32)
                oi = jnp.dot(a_off.astype(BF16), vb[:lo], preferred_element_type=F32)
            for s in range(U):
                valid = trow >= s
                e = jnp.exp(jnp.where(valid, bi - bi[s:s + 1], 0.0))
                a = jnp.sum(qi * e * kk[lo + s:lo + s + 1], axis=-1, keepdims=True)
                oi = oi + jnp.where(valid, a, 0.0) * v[lo + s:lo + s + 1]
            parts.append(oi)
        o = o_inter + jnp.concatenate(parts, axis=0)
        o = o * lax.rsqrt(jnp.mean(o * o, axis=-1, keepdims=True) + EPS) * ng
        o_ref[0, rows, :] = (o * _silu(gt_ref[0, rows, :])).astype(BF16)
        bl = b[C - 1:C]
        kv_t = lax.dot_general(vb, (kk * jnp.exp(bl - b)).astype(BF16), TN, preferred_element_type=F32)
        return jnp.exp(bl) * st_t + kv_t

    lax.fori_loop(0, S // C, chunk, jnp.zeros((HG_DV, HG_DK), F32))


def _cumsum_rows(x):
    n = x.shape[0]
    row = lax.broadcasted_iota(jnp.int32, x.shape, 0)
    d = 1
    while d < n:
        x = x + jnp.where(row >= d, pltpu.roll(x, d, 0), 0.0)
        d *= 2
    return x


def _hgrn_mxu_kernel(hg_ref, lb_ref, ng_ref, o_ref, st_scr):
    ts = hg_ref.shape[1]
    C, U = HG_CHUNK, HG_SUB

    @pl.when(pl.program_id(1) == 0)
    def _():
        st_scr[...] = jnp.zeros(st_scr.shape, F32)

    W = HG_WIDTH
    NH = HG_HEADS
    head_of_lane = lax.broadcasted_iota(jnp.int32, (1, W), 1) // HG_DK
    hcols = [slice(h * HG_DK, (h + 1) * HG_DK) for h in range(NH)]

    def chunk(c, carry):
        rows = pl.ds(pl.multiple_of(c * C, C), C)
        lb = lb_ref[...]
        f = lb + (1.0 - lb) * _sigmoid(hg_ref[0, rows, W:2 * W])
        kk = 1.0 - f
        b = _cumsum_rows(jnp.log(f))
        q = hg_ref[0, rows, 0:W] * (HG_DK ** -0.5)
        vb = hg_ref[0, rows, 2 * W:3 * W].astype(BF16)
        qe = (q * jnp.exp(b)).astype(BF16)
        o_inter = jnp.concatenate(
            [lax.dot_general(qe[:, hc], st_scr[h].astype(BF16), NT, preferred_element_type=F32)
             for h, hc in enumerate(hcols)], axis=1)
        parts = []
        for i in range(C // U):
            lo, hi = i * U, (i + 1) * U
            r = b[lo - 1:lo] if i else jnp.zeros((1, W), F32)
            qrel = q[lo:hi] * jnp.exp(b[lo:hi] - r)
            kall = (kk[:hi] * jnp.exp(r - b[:hi])).astype(BF16)
            qbd = jnp.concatenate([jnp.where(head_of_lane == h, qrel, 0.0) for h in range(NH)], axis=0)
            a = lax.dot_general(qbd.astype(BF16), kall, NT, preferred_element_type=F32)
            trow = lax.broadcasted_iota(jnp.int32, (NH * U, hi), 0) % U
            a = jnp.where(lax.broadcasted_iota(jnp.int32, (NH * U, hi), 1) <= lo + trow, a, 0.0)
            oa = jnp.dot(a.astype(BF16), vb[:hi], preferred_element_type=F32)
            oi = jnp.where(head_of_lane == 0, oa[0:U], 0.0)
            for h in range(1, NH):
                oi = jnp.where(head_of_lane == h, oa[h * U:(h + 1) * U], oi)
            parts.append(oi)
        o = o_inter + jnp.concatenate(parts, axis=0)
        o = jnp.concatenate(
            [o[:, hc] * lax.rsqrt(jnp.mean(o[:, hc] * o[:, hc], axis=-1, keepdims=True) + EPS) for hc in hcols], axis=1)
        o_ref[0, rows, :] = (o * ng_ref[...] * _silu(hg_ref[0, rows, 3 * W:4 * W])).astype(BF16)
        bl = b[C - 1:C]
        ke = (kk * jnp.exp(bl - b)).astype(BF16)
        decay = jnp.exp(bl)
        for h, hc in enumerate(hcols):
            kv_t = lax.dot_general(vb[:, hc], ke[:, hc], TN, preferred_element_type=F32)
            st_scr[h] = decay[:, hc] * st_scr[h] + kv_t
        return carry

    lax.fori_loop(0, ts // C, chunk, 0, unroll=8)


def _hgrn_mxu(hg, lb, ng, ts):
    B, S, _ = hg.shape
    vec = pl.BlockSpec((1, HG_WIDTH), lambda b, i: (0, 0))
    return pl.pallas_call(
        _hgrn_mxu_kernel,
        out_shape=jax.ShapeDtypeStruct((B, S, HG_WIDTH), BF16),
        grid=(B, S // ts),
        in_specs=[pl.BlockSpec((1, ts, 4 * HG_WIDTH), lambda b, i: (b, i, 0)), vec, vec],
        out_specs=pl.BlockSpec((1, ts, HG_WIDTH), lambda b, i: (b, i, 0)),
        scratch_shapes=[pltpu.VMEM((HG_HEADS, HG_DV, HG_DK), F32)],
        compiler_params=_cparams(("arbitrary", "arbitrary")),
        name="hgrn_mxu",
    )(hg, lb, ng)


def _hgrn(hg, lb, ng):
    B, S, _ = hg.shape
    col = lambda k: pl.BlockSpec((1, S, HG_DK), lambda b, h, k=k: (b, 0, k * HG_HEADS + h))
    vec = pl.BlockSpec((1, HG_DK), lambda b, h: (0, h))
    return pl.pallas_call(
        _hgrn_kernel,
        out_shape=jax.ShapeDtypeStruct((B, S, HG_WIDTH), BF16),
        grid=(B, HG_HEADS),
        in_specs=[col(0), col(1), col(2), col(3), vec, vec],
        out_specs=pl.BlockSpec((1, S, HG_DV), lambda b, h: (b, 0, h)),
        compiler_params=_cparams(("arbitrary", "arbitrary")),
        name="hgrn",
    )(hg, hg, hg, hg, lb, ng)


def _cmpmlp_kernel(x_ref, w1a_ref, w1b_ref, pe_ref, w1_ref, b1_ref, w2_ref, o_ref):
    x = x_ref[0]
    hb = jnp.dot(pe_ref[...], w1_ref[...], precision=HIGHEST, preferred_element_type=F32)[0:1] + b1_ref[...]
    nrow = x.shape[0]
    for g in range(NSA_KV_GROUPS):
        a = jnp.dot(x, w1a_ref[g], preferred_element_type=F32)
        bm = jnp.dot(x, w1b_ref[g], preferred_element_type=F32)
        hdn = a + pltpu.roll(bm, nrow - 1, 0) + hb
        o_ref[0, g] = jnp.dot(_gelu_tanh(hdn).astype(BF16), w2_ref[g], preferred_element_type=F32).astype(BF16)


def _cmpmlp(x2, w1a, w1b, pe8, w1, b1, w2p):
    B, ncb, width = x2.shape
    full = lambda a: pl.BlockSpec(a.shape, lambda b: (0,) * a.ndim)
    return pl.pallas_call(
        _cmpmlp_kernel,
        out_shape=jax.ShapeDtypeStruct((B, NSA_KV_GROUPS, ncb, LANES), BF16),
        grid=(B,),
        in_specs=[pl.BlockSpec((1, ncb, width), lambda b: (b, 0, 0)),
                  full(w1a), full(w1b), full(pe8), full(w1), full(b1), full(w2p)],
        out_specs=pl.BlockSpec((1, NSA_KV_GROUPS, ncb, LANES), lambda b: (b, 0, 0, 0)),
        compiler_params=_cparams(("arbitrary",)),
        name="cmpmlp",
    )(x2, w1a, w1b, pe8, w1, b1, w2p)


def _rank_before(score, nrows):
    jrow = lax.broadcasted_iota(jnp.int32, score.shape, 0)
    rank = jnp.zeros(score.shape, F32)
    for k in range(nrows):
        rk = score[k:k + 1]
        beats = (rk > score) | ((rk == score) & (jrow > k))
        rank = rank + jnp.where(beats, 1.0, 0.0)
    return rank


def _topk_rows(score, k):
    n = score.shape[0]
    row = lax.broadcasted_iota(jnp.int32, score.shape, 0).astype(F32)
    keep = jnp.zeros(score.shape, F32)
    for _ in range(k):
        top = jnp.max(score, axis=0, keepdims=True)
        first = jnp.min(jnp.where(score == top, row, float(n)), axis=0, keepdims=True)
        pick = row == first
        keep = jnp.where(pick, 1.0, keep)
        score = jnp.where(pick, -jnp.inf, score)
    return keep


def _transpose_small_ints(xt):
    c = xt.shape[1]
    eye = (lax.broadcasted_iota(jnp.int32, (c, c), 0) == lax.broadcasted_iota(jnp.int32, (c, c), 1)).astype(BF16)
    return lax.dot_general(eye, xt.astype(BF16), NT, preferred_element_type=F32)


def _pack_heads(heads):
    low = lax.broadcasted_iota(jnp.int32, heads[0].shape, 1) < NSA_HD
    return jnp.concatenate([jnp.where(low, heads[k], pltpu.roll(heads[k + 1], NSA_HD, 1))
                            for k in range(0, len(heads), 2)], axis=1)


def _cmpsel_kernel(q_ref, kc_ref, vc_ref, gate_ref, o_ref, sel_ref):
    tq = q_ref.shape[1]
    ncb = kc_ref.shape[2]
    t = pl.program_id(2)
    kc = kc_ref[0, 0]
    vc = vc_ref[0, 0]
    pos = t * tq + lax.broadcasted_iota(jnp.int32, (tq, 1), 0)
    cblk = lax.broadcasted_iota(jnp.int32, (1, ncb), 1)
    vis = (cblk * CMP_STRIDE + CMP_BLOCK - 1) <= pos
    psum = jnp.zeros((tq, ncb), F32)
    gate = gate_ref[0]
    heads = []
    for h in range(NSA_HG):
        s = lax.dot_general(q_ref[0, :, h * LANES:(h + 1) * LANES], kc, NT, preferred_element_type=F32)
        s = jnp.where(vis, s, NEG)
        p = jnp.exp(s - jnp.max(s, axis=-1, keepdims=True))
        p = jnp.where(vis, p / jnp.sum(p, axis=-1, keepdims=True), 0.0)
        heads.append(jnp.dot(p.astype(BF16), vc, preferred_element_type=F32) * gate[:, 3 * h:3 * h + 1])
        psum = psum + p
    o_ref[0] = _pack_heads(heads)
    nsb = ncb * CMP_STRIDE // SEL_BLOCK
    jb = lax.broadcasted_iota(jnp.int32, (nsb, ncb), 0) * SEL_BLOCK
    cb = lax.broadcasted_iota(jnp.int32, (nsb, ncb), 1) * CMP_STRIDE
    ov = jnp.maximum(jnp.minimum(cb + CMP_BLOCK, jb + SEL_BLOCK) - jnp.maximum(cb, jb), 0).astype(F32) / CMP_BLOCK
    ps_hi = psum.astype(BF16)
    ps_lo = (psum - ps_hi.astype(F32)).astype(BF16)
    ovb = ov.astype(BF16)
    pslc_t = (lax.dot_general(ovb, ps_hi, NT, preferred_element_type=F32)
              + lax.dot_general(ovb, ps_lo, NT, preferred_element_type=F32))
    posl = t * tq + lax.broadcasted_iota(jnp.int32, (1, tq), 1)
    cur = posl // SEL_BLOCK
    jrow = lax.broadcasted_iota(jnp.int32, (nsb, tq), 0)
    forced = (jrow == 0) | (jrow == cur) | (jrow == cur - 1)
    score = jnp.where(forced, 1e30, jnp.where(jrow <= cur, pslc_t, NEG))
    chosen = _topk_rows(score, min(N_SEL, nsb)) > 0.5
    drop = jnp.where(chosen & (score > -1e29), 0.0, 1.0)
    drop = jnp.concatenate([drop, jnp.zeros((LANES - nsb, tq), F32)], axis=0)
    sel_ref[0, 0] = (_transpose_small_ints(drop) * NEG).astype(BF16)


def _cmpsel(qraw, kcmp, vcmp, gate, tq):
    B, S, _ = qraw.shape
    ncb = kcmp.shape[2]
    gw = NSA_HG * LANES
    assert S // SEL_BLOCK <= NSA_HD
    cmp_spec = pl.BlockSpec((1, 1, ncb, LANES), lambda b, g, t: (b, g, 0, 0))
    return pl.pallas_call(
        _cmpsel_kernel,
        out_shape=(jax.ShapeDtypeStruct((B, S, NSA_WIDTH), F32),
                   jax.ShapeDtypeStruct((B, NSA_KV_GROUPS, S, LANES), BF16)),
        grid=(B, NSA_KV_GROUPS, S // tq),
        in_specs=[pl.BlockSpec((1, tq, gw), lambda b, g, t: (b, t, g)), cmp_spec, cmp_spec,
                  pl.BlockSpec((1, tq, LANES), lambda b, g, t: (b, t, g))],
        out_specs=(pl.BlockSpec((1, tq, NSA_WIDTH // NSA_KV_GROUPS), lambda b, g, t: (b, t, g)),
                   pl.BlockSpec((1, 1, tq, LANES), lambda b, g, t: (b, g, t, 0))),
        compiler_params=_cparams(("arbitrary", "arbitrary", "arbitrary")),
        name="cmpsel",
    )(qraw, kcmp, vcmp, gate)


def _slcwin_kernel(q_ref, sel_ref, ks_ref, kw_ref, vs_ref, vw_ref, gate_ref, ocmp_ref, o_ref, *, tk, wk):
    tq = q_ref.shape[1]
    S = ks_ref.shape[1]
    t = pl.program_id(2)
    q0 = t * tq
    qpos = q0 + lax.broadcasted_iota(jnp.int32, (tq, 1), 0)
    qw = [q_ref[0, :, h * LANES:(h + 1) * LANES] for h in range(NSA_HG)]
    sel = sel_ref[0, 0]
    qa = [jnp.concatenate([q, sel], axis=1) for q in qw]

    gate = gate_ref[0]

    def normalise(acc, g):
        return acc * (g / acc[:, NSA_HD:NSA_HD + 1])

    def kstep(j, carry, diag):
        k0 = pl.multiple_of(j * tk, tk)
        ks = ks_ref[0, pl.ds(k0, tk), :]
        vs = vs_ref[0, pl.ds(k0, tk), :]
        if diag:
            vis = (k0 + lax.broadcasted_iota(jnp.int32, (1, tk), 1)) <= qpos
        out = []
        for h in range(NSA_HG):
            m, acc = carry[h]
            s = lax.dot_general(qa[h], ks, NT, preferred_element_type=F32)
            if diag:
                s = jnp.where(vis, s, NEG)
            mn = jnp.maximum(m, jnp.max(s, axis=-1, keepdims=True))
            p = jnp.exp2(s - mn)
            acc = jnp.exp2(m - mn) * acc + jnp.dot(p.astype(BF16), vs, preferred_element_type=F32)
            out.append((mn, acc))
        return tuple(out)

    init = tuple((jnp.full((tq, 1), NEG, F32), jnp.zeros((tq, LANES), F32)) for _ in range(NSA_HG))
    jdiag = q0 // tk
    carry = lax.fori_loop(0, jdiag, lambda j, c: kstep(j, c, False), init)
    carry = kstep(jdiag, carry, True)
    heads = [normalise(carry[h][1], gate[:, 3 * h + 1:3 * h + 2]) for h in range(NSA_HG)]

    nblk = S // tq
    start = pl.multiple_of(jnp.clip(t - WINDOW // tq, 0, nblk - wk // tq) * tq, tq)
    kw = kw_ref[0, pl.ds(start, wk), :]
    vw = vw_ref[0, pl.ds(start, wk), :]
    d = qpos - (start + lax.broadcasted_iota(jnp.int32, (1, wk), 1))
    okw = (d >= 0) & (d < WINDOW)
    for h in range(NSA_HG):
        s = jnp.where(okw, lax.dot_general(qw[h], kw, NT, preferred_element_type=F32), NEG)
        p = jnp.exp2(s - jnp.max(s, axis=-1, keepdims=True))
        heads[h] = heads[h] + normalise(jnp.dot(p.astype(BF16), vw, preferred_element_type=F32),
                                        gate[:, 3 * h + 2:3 * h + 3])
    o_ref[0] = (ocmp_ref[0] + _pack_heads(heads)).astype(BF16)


def _slcwin(qrot, sel, ks, kw, vaug, gate, ocmp, tq, tk):
    B, S, _ = qrot.shape
    gw = NSA_HG * LANES
    wk = min(WINDOW + tq, S)
    ospec = pl.BlockSpec((1, tq, NSA_WIDTH // NSA_KV_GROUPS), lambda b, g, t: (b, t, g))
    return pl.pallas_call(
        functools.partial(_slcwin_kernel, tk=tk, wk=wk),
        out_shape=jax.ShapeDtypeStruct((B, S, NSA_WIDTH), BF16),
        grid=(B, NSA_KV_GROUPS, S // tq),
        in_specs=[pl.BlockSpec((1, tq, gw), lambda b, g, t: (b, t, g)),
                  pl.BlockSpec((1, 1, tq, LANES), lambda b, g, t: (b, g, t, 0)),
                  pl.BlockSpec((1, S, 2 * LANES), lambda b, g, t: (b, 0, 0)),
                  pl.BlockSpec((1, S, LANES), lambda b, g, t: (b, 0, 0)),
                  pl.BlockSpec((1, S, LANES), lambda b, g, t: (b, 0, g)),
                  pl.BlockSpec((1, S, LANES), lambda b, g, t: (b, 0, NSA_KV_GROUPS + g)),
                  pl.BlockSpec((1, tq, LANES), lambda b, g, t: (b, t, g)),
                  ospec],
        out_specs=ospec,
        compiler_params=_cparams(("arbitrary", "arbitrary", "arbitrary")),
        name="slcwin",
    )(qrot, sel, ks, kw, vaug, vaug, gate, ocmp)


def _outproj_kernel(x_ref, ohg_ref, onsa_ref, wout_ref,
                    g1_ref, sh2_ref, sc2_ref, n2_ref, rwt_ref, rb_ref,
                    x1_ref, h2_ref, sc_ref, slot_t_ref, w_t_ref, cnt_ref):
    tm = x_ref.shape[1]
    mix = jnp.dot(jnp.concatenate([ohg_ref[0], onsa_ref[0]], axis=1), wout_ref[...], preferred_element_type=F32)
    x1 = x_ref[0] + g1_ref[0] * mix
    x1_ref[0] = x1
    h2 = (x1 * lax.rsqrt(jnp.mean(x1 * x1, axis=-1, keepdims=True) + EPS) * n2_ref[...]) * (1.0 + sc2_ref[0]) + sh2_ref[0]
    h_hi = h2.astype(BF16)
    h2_ref[0] = h_hi
    h_lo = (h2 - h_hi.astype(F32)).astype(BF16)
    both = lax.dot_general(rwt_ref[...], h_hi, NT, preferred_element_type=F32)
    logits = (both[:N_EXPERTS] + both[N_EXPERTS:]
              + lax.dot_general(rwt_ref[:N_EXPERTS, :], h_lo, NT, preferred_element_type=F32))
    scores = _sigmoid(logits)
    choice = scores + rb_ref[...]
    per = N_EXPERTS // N_GROUPS
    c3 = choice.reshape(N_GROUPS, per, tm)
    erow = lax.broadcasted_iota(jnp.int32, c3.shape, 1)
    rank_in = jnp.zeros(c3.shape, F32)
    for k in range(per):
        ck = c3[:, k:k + 1, :]
        rank_in = rank_in + jnp.where((ck > c3) | ((ck == c3) & (erow > k)), 1.0, 0.0)
    grp_score = jnp.sum(jnp.where(rank_in < 2, c3, 0.0), axis=1)
    grp_keep = _rank_before(grp_score, N_GROUPS) < TOPK_GROUPS
    masked = jnp.where(grp_keep[:, None, :], c3, -jnp.inf).reshape(N_EXPERTS, tm)
    keep = _topk_rows(masked, TOP_K) > 0.5
    tw = jnp.where(keep, scores, 0.0)
    tw = tw / jnp.sum(tw, axis=0, keepdims=True) * ROUTED_SCALE
    before = lax.broadcasted_iota(jnp.int32, (tm, tm), 0)
    after = lax.broadcasted_iota(jnp.int32, (tm, tm), 1)
    earlier = (before < after) & (before // MOE_SUB == after // MOE_SUB)
    kept = jnp.where(keep, 1.0, 0.0)
    pref = jnp.dot(kept.astype(BF16), earlier.astype(BF16), preferred_element_type=F32)
    slot = jnp.where(keep, pref, -1.0)
    assert MOE_SUB <= 256
    sc_ref[0] = _transpose_small_ints(slot)
    for j in range(tm // MOE_SUB):
        sub = slice(j * MOE_SUB, (j + 1) * MOE_SUB)
        slot_t_ref[:, j] = slot[:, sub].reshape(N_EXPERTS // MOE_EG, MOE_EG, MOE_SUB)
        w_t_ref[:, j] = tw[:, sub].reshape(N_EXPERTS // MOE_EG, MOE_EG, MOE_SUB)
        cnt_ref[j] = jnp.broadcast_to(jnp.sum(kept[:, sub], axis=1, keepdims=True), (N_EXPERTS, LANES))


def _outproj(x, ohg, onsa, wout, mod3, n2, rwt, rb, tm):
    B, S, D = x.shape
    blk = lambda w: pl.BlockSpec((1, tm, w), lambda b, i: (b, i, 0))
    full = lambda a: pl.BlockSpec(a.shape, lambda b, i: (0,) * a.ndim)
    modc = lambda k: pl.BlockSpec((1, 1, D), lambda b, i, k=k: (b, 0, k))
    nt = S // tm
    ng = N_EXPERTS // MOE_EG
    nh = tm // MOE_SUB
    tspec = pl.BlockSpec((ng, nh, MOE_EG, MOE_SUB), lambda b, i: (0, b * nt + i, 0, 0))
    return pl.pallas_call(
        _outproj_kernel,
        out_shape=(jax.ShapeDtypeStruct((B, S, D), F32),
                   jax.ShapeDtypeStruct((B, S, D), BF16),
                   jax.ShapeDtypeStruct((B, S, N_EXPERTS), F32),
                   jax.ShapeDtypeStruct((ng, B * nt * nh, MOE_EG, MOE_SUB), F32),
                   jax.ShapeDtypeStruct((ng, B * nt * nh, MOE_EG, MOE_SUB), F32),
                   jax.ShapeDtypeStruct((B * nt * nh, N_EXPERTS, LANES), F32)),
        grid=(B, nt),
        in_specs=[blk(D), blk(HG_WIDTH), blk(NSA_WIDTH), full(wout),
                  modc(2), modc(3), modc(4), full(n2), full(rwt), full(rb)],
        out_specs=(blk(D), blk(D), blk(N_EXPERTS), tspec, tspec,
                   pl.BlockSpec((nh, N_EXPERTS, LANES), lambda b, i: (b * nt + i, 0, 0))),
        compiler_params=_cparams(("arbitrary", "arbitrary")),
        name="outproj",
    )(x, ohg, onsa, wout, mod3, mod3, mod3, n2, rwt, rb)


def _swiglu_hidden(x, wgu):
    gu = jnp.dot(x, wgu, preferred_element_type=F32)
    return _silu(gu[:, :EXPERT_HIDDEN]) * gu[:, EXPERT_HIDDEN:]


def _moe_kernel(rounds_ref, h_ref, slot_t_ref, w_t_ref, slot_c_ref, wgu_ref, wdn_ref, sgu_ref, sdn_ref,
                o_ref, x_scr, y_scr, p_scr):
    i = pl.program_id(0)
    g = pl.program_id(1)
    tm = h_ref.shape[0]
    nsub = tm // MOE_SUB
    rnd, eg = MOE_ROUND, MOE_EG
    ng = N_EXPERTS // eg
    base = i * (N_EXPERTS + ng)

    @pl.when(g == 0)
    def _():
        act = _swiglu_hidden(h_ref[...], sgu_ref[...]).astype(BF16)
        o_ref[...] = jnp.dot(act, sdn_ref[...], preferred_element_type=F32)

    rslot = lax.broadcasted_iota(jnp.int32, (rnd, 1), 0).astype(F32)
    lane_slot = (lax.broadcasted_iota(jnp.int32, (1, eg * rnd), 1) % rnd).astype(F32)
    spread = (lax.broadcasted_iota(jnp.int32, (N_EXPERTS, eg * rnd), 1) // rnd + g * eg
              == lax.broadcasted_iota(jnp.int32, (N_EXPERTS, eg * rnd), 0)).astype(BF16)

    def one_round(r, carry):
        first = (r * rnd).astype(F32)
        for s in range(nsub):
            st = slot_t_ref[0, s] - first
            p = jnp.concatenate([jnp.where(rslot == st[e:e + 1, :], 1.0, 0.0) for e in range(eg)], axis=0)
            p_scr[s] = p.astype(BF16)
            x_scr[s] = jnp.dot(p_scr[s], h_ref[s * MOE_SUB:(s + 1) * MOE_SUB, :],
                               preferred_element_type=F32).astype(BF16)
        for e in range(eg):
            rows = slice(e * rnd, (e + 1) * rnd)

            @pl.when(r < rounds_ref[base + g * eg + e])
            def _():
                xe = jnp.concatenate([x_scr[s, rows, :] for s in range(nsub)], axis=0)
                wslot = jnp.concatenate(
                    [jnp.sum(p_scr[s, rows, :].astype(F32) * w_t_ref[0, s][e:e + 1, :], axis=-1, keepdims=True)
                     for s in range(nsub)], axis=0)
                y = jnp.dot((_swiglu_hidden(xe, wgu_ref[e]) * wslot).astype(BF16), wdn_ref[e],
                            preferred_element_type=F32)
                for s in range(nsub):
                    y_scr[s, rows, :] = y[s * rnd:(s + 1) * rnd].astype(BF16)

            @pl.when(r >= rounds_ref[base + g * eg + e])
            def _():
                for s in range(nsub):
                    y_scr[s, rows, :] = jnp.zeros((rnd, D_MODEL), BF16)
        for s in range(nsub):
            toks = slice(s * MOE_SUB, (s + 1) * MOE_SUB)
            sc = jnp.dot(slot_c_ref[toks, :].astype(BF16), spread, preferred_element_type=F32) - first
            pt = jnp.where(sc == lane_slot, 1.0, 0.0).astype(BF16)
            o_ref[toks, :] += jnp.dot(pt, y_scr[s], preferred_element_type=F32)
        return carry

    lax.fori_loop(0, rounds_ref[base + N_EXPERTS + g], one_round, 0)


def _moe(rounds, h2, slot_t, w_t, slot_c, wgu, wdn, sgu, sdn, tm):
    T, D = h2.shape
    eg, nsub = MOE_EG, tm // MOE_SUB
    full = lambda a: pl.BlockSpec(a.shape, lambda i, e, o: (0,) * a.ndim)
    tspec = pl.BlockSpec((1, nsub, eg, MOE_SUB), lambda i, e, o: (e, i, 0, 0))
    return pl.pallas_call(
        _moe_kernel,
        out_shape=jax.ShapeDtypeStruct((T, D), F32),
        grid_spec=pltpu.PrefetchScalarGridSpec(
            num_scalar_prefetch=1,
            grid=(T // tm, N_EXPERTS // eg),
            in_specs=[pl.BlockSpec((tm, D), lambda i, e, o: (i, 0), pipeline_mode=pl.Buffered(1)), tspec, tspec,
                      pl.BlockSpec((tm, N_EXPERTS), lambda i, e, o: (i, 0)),
                      pl.BlockSpec((eg, D, 2 * EXPERT_HIDDEN), lambda i, e, o: (e, 0, 0)),
                      pl.BlockSpec((eg, EXPERT_HIDDEN, D), lambda i, e, o: (e, 0, 0)),
                      full(sgu), full(sdn)],
            out_specs=pl.BlockSpec((tm, D), lambda i, e, o: (i, 0), pipeline_mode=pl.Buffered(1)),
            scratch_shapes=[pltpu.VMEM((nsub, eg * MOE_ROUND, D), BF16),
                            pltpu.VMEM((nsub, eg * MOE_ROUND, D), BF16),
                            pltpu.VMEM((nsub, eg * MOE_ROUND, MOE_SUB), BF16)]),
        compiler_params=_cparams(("arbitrary", "arbitrary")),
        name="moe",
    )(rounds, h2, slot_t, w_t, slot_c, wgu, wdn, sgu, sdn)


def _final_kernel(x1_ref, moe_ref, g2_ref, fg_ref, o_ref):
    x2 = x1_ref[0] + g2_ref[0] * moe_ref[0]
    o_ref[0] = x2 * lax.rsqrt(jnp.mean(x2 * x2, axis=-1, keepdims=True) + EPS) * fg_ref[...]


def _final(x1, moe, mod3, fg, tm):
    B, S, D = x1.shape
    blk = pl.BlockSpec((1, tm, D), lambda b, i: (b, i, 0))
    return pl.pallas_call(
        _final_kernel,
        out_shape=jax.ShapeDtypeStruct((B, S, D), F32),
        grid=(B, S // tm),
        in_specs=[blk, blk, pl.BlockSpec((1, 1, D), lambda b, i: (b, 0, 5)), pl.BlockSpec((1, D), lambda b, i: (0, 0))],
        out_specs=blk,
        compiler_params=_cparams(("arbitrary", "arbitrary")),
        name="final",
    )(x1, moe, mod3, fg)


def _split_bf16(w):
    hi = w.astype(BF16)
    return jnp.concatenate([hi, (w - hi.astype(F32)).astype(BF16)], axis=0)


def _pack_w_in(w_in):
    return jnp.pad(w_in, ((0, 0), (0, IN_COLS_P - w_in.shape[1]))).astype(BF16)


def _pack_cmp(pos, w1, b1, w2, lane_by_group):
    half = CMP_STRIDE * NSA_HD
    def rows_for(wh):
        w3 = wh.reshape(CMP_STRIDE, NSA_HD, CMP_HIDDEN)
        z = jnp.zeros_like(w3)
        return jnp.stack([jnp.concatenate([w3, z], axis=1).reshape(CMP_STRIDE * LANES, CMP_HIDDEN),
                          jnp.concatenate([z, w3], axis=1).reshape(CMP_STRIDE * LANES, CMP_HIDDEN)])
    w1a = rows_for(w1[:half]).astype(BF16)
    w1b = rows_for(w1[half:]).astype(BF16)
    z2 = jnp.zeros_like(w2)
    w2_first = jnp.concatenate([w2, z2], axis=1)
    w2p = jnp.stack([w2_first, jnp.concatenate([z2, w2], axis=1) if lane_by_group else w2_first]).astype(BF16)
    pe8 = jnp.pad(pos.reshape(1, CMP_BLOCK * NSA_HD), ((0, 7), (0, 0)))
    return w1a, w1b, pe8, w1, b1.reshape(1, CMP_HIDDEN), w2p


def _rope_tables(S):
    half = NSA_HD // 2
    inv = ROPE_THETA ** (-jnp.arange(half, dtype=F32) / half)
    ang = jnp.arange(S, dtype=F32)[:, None] * inv[None, :]
    cos, sin = jnp.cos(ang), jnp.sin(ang)
    reps = LANES // NSA_HD
    return jnp.tile(jnp.concatenate([cos, cos], axis=1), (1, reps)), jnp.tile(jnp.concatenate([-sin, sin], axis=1), (1, reps))


def _tiles(S):
    return dict(inproj=min(512, S), hgrn=min(512, S), cmpsel=min(512, S), slc_q=512, slc_k=512, moe=min(2048, S), final=min(512, S))


def kernel(x, c, w_ada, b_ada, norm1_g, w_in, hg_lb_logits, hg_norm_g, cmp_pos_k, cmp_w1_k, cmp_b1_k, cmp_w2_k,
           cmp_pos_v, cmp_w1_v, cmp_b1_v, cmp_w2_v, w_out, norm2_g, router_w, router_bias, w_exp_gu, w_exp_dn,
           w_sh_gu, w_sh_dn, final_g):
    B, S, D = x.shape
    assert D == D_MODEL and w_ada.shape[0] == 1 and S % 512 == 0
    tl = _tiles(S)
    l = 0
    lb = jnp.cumsum(jax.nn.softmax(hg_lb_logits.astype(F32), axis=0), axis=0)[l].reshape(1, HG_WIDTH)
    c8 = jnp.pad(c, ((0, 8 - B), (0, 0)))
    mod3 = _ada(c8, w_ada[l], b_ada[l].reshape(1, -1))[:B].reshape(B, 1, 6 * D)
    cos, sin = _rope_tables(S)
    hg, qraw, qrot, kc, vc, ks, kw, vaug, gate = _inproj(x, mod3, norm1_g[l].reshape(1, D), _pack_w_in(w_in[l]),
                                                         cos, sin, tl["inproj"])
    ng = hg_norm_g[l].reshape(1, HG_WIDTH)
    mxu_safe = -HG_SUB * jnp.log(jnp.min(lb)) <= HG_SAFE_LOG_RANGE
    ohg = lax.cond(mxu_safe, lambda: _hgrn_mxu(hg, lb, ng, tl["hgrn"]), lambda: _hgrn(hg, lb, ng))
    ncb = S // CMP_STRIDE
    kcmp = _cmpmlp(kc.reshape(B, ncb, CMP_STRIDE * LANES),
                   *_pack_cmp(cmp_pos_k[l], cmp_w1_k[l], cmp_b1_k[l], cmp_w2_k[l], True))
    vcmp = _cmpmlp(vc.reshape(B, ncb, CMP_STRIDE * LANES),
                   *_pack_cmp(cmp_pos_v[l], cmp_w1_v[l], cmp_b1_v[l], cmp_w2_v[l], False))
    ocmp, sel = _cmpsel(qraw, kcmp, vcmp, gate, tl["cmpsel"])
    onsa = _slcwin(qrot, sel, ks, kw, vaug, gate, ocmp, tl["slc_q"], tl["slc_k"])
    x1, h2, sc, slot_t, w_t, cnt = _outproj(x, ohg, onsa, w_out[l].astype(BF16), mod3,
                                            norm2_g[l].reshape(1, D), _split_bf16(router_w[l].T),
                                            router_bias[l].reshape(N_EXPERTS, 1), ROUTE_TM)
    T = B * S
    tm = tl["moe"]
    ng = N_EXPERTS // MOE_EG
    slot_c = sc.reshape(T, N_EXPERTS)
    load = jnp.max(cnt[:, :, 0].reshape(T // tm, tm // MOE_SUB, N_EXPERTS), axis=1).astype(jnp.int32)
    per_expert = (load + MOE_ROUND - 1) // MOE_ROUND
    rounds = jnp.concatenate([per_expert, jnp.max(per_expert.reshape(T // tm, ng, MOE_EG), axis=2)], axis=1).reshape(-1)
    moe = _moe(rounds, h2.reshape(T, D), slot_t, w_t, slot_c,
               w_exp_gu[l].astype(BF16), w_exp_dn[l].astype(BF16), w_sh_gu[l].astype(BF16), w_sh_dn[l].astype(BF16), tm)
    return _final(x1, moe.reshape(B, S, D), mod3, final_g.reshape(1, D), tl["final"])
```

```python
import functools

import numpy as np
import jax
import jax.numpy as jnp
from jax import lax
from jax.experimental import pallas as pl
from jax.experimental.pallas import tpu as pltpu

F32 = jnp.float32
BF16 = jnp.bfloat16
HIGHEST = lax.Precision.HIGHEST

D_MODEL = 1024
EPS = 1e-6
HG_HEADS = 4
HG_DK = 128
HG_DV = 128
HG_WIDTH = HG_HEADS * HG_DV
HG_CHUNK = 64
HG_SUB = 16
HG_SAFE_LOG_RANGE = 80.0
NSA_HEADS = 8
NSA_KV_GROUPS = 2
NSA_HG = NSA_HEADS // NSA_KV_GROUPS
NSA_HD = 64
NSA_WIDTH = NSA_HEADS * NSA_HD
NSA_KV = NSA_KV_GROUPS * NSA_HD
CMP_BLOCK = 32
CMP_STRIDE = 16
CMP_HIDDEN = 256
SEL_BLOCK = 64
N_SEL = 16
WINDOW = 512
ROPE_THETA = 10000.0
SLC_STACK = 4
WIN_STACK = 1
N_EXPERTS = 64
TOP_K = 8
N_GROUPS = 8
TOPK_GROUPS = 4
EXPERT_HIDDEN = 256
ROUTED_SCALE = 2.5
ROUTE_TM = 512
MOE_SUB = 256
MOE_ROUND = 32
MOE_EG = 8

LANES = 128
NEG = -1e30
VMEM_LIMIT = 56 * 1024 * 1024

C_HG = 0
C_Q = 4 * HG_WIDTH
C_K = C_Q + NSA_WIDTH
C_GATE = C_K + 6 * NSA_KV
IN_COLS_P = C_GATE + LANES
LOG2E = 1.4426950408889634

NT = (((1,), (1,)), ((), ()))
TN = (((0,), (0,)), ((), ()))


def _cparams(sem):
    return pltpu.CompilerParams(dimension_semantics=sem, vmem_limit_bytes=VMEM_LIMIT)


def _sigmoid(x):
    return 1.0 / (1.0 + jnp.exp(-x))


def _silu(x):
    return x * _sigmoid(x)


def _gelu_tanh(x):
    return 0.5 * x * (1.0 + jnp.tanh(0.7978845608028654 * (x + 0.044715 * (x * x * x))))


def _ada_kernel(c_ref, w_ref, b_ref, o_ref):
    a = _silu(c_ref[...])
    o_ref[...] = jnp.dot(a, w_ref[...], precision=HIGHEST, preferred_element_type=F32) + b_ref[...]


def _ada(c8, w, b):
    n = w.shape[1]
    tn = 1024
    return pl.pallas_call(
        _ada_kernel,
        out_shape=jax.ShapeDtypeStruct((8, n), F32),
        grid=(n // tn,),
        in_specs=[pl.BlockSpec((8, D_MODEL), lambda j: (0, 0)),
                  pl.BlockSpec((D_MODEL, tn), lambda j: (0, j)),
                  pl.BlockSpec((1, tn), lambda j: (0, j))],
        out_specs=pl.BlockSpec((8, tn), lambda j: (0, j)),
        compiler_params=_cparams(("arbitrary",)),
        name="ada",
    )(c8, w, b)


def _rope(t, cos, sin_signed, first_half):
    rot = jnp.where(first_half, pltpu.roll(t, 96, 1), pltpu.roll(t, 32, 1))
    return t * cos + rot * sin_signed


def _inproj_kernel(x_ref, sh_ref, sc_ref, g_ref, w_ref, cos_ref, sin_ref,
                   hg_ref, qraw_ref, qrot_ref, kc_ref, vc_ref, ks_ref, kw_ref, va_ref, gate_ref, h_scr):
    tm = x_ref.shape[1]
    x = x_ref[0]
    y = x * lax.rsqrt(jnp.mean(x * x, axis=-1, keepdims=True) + EPS) * g_ref[...]
    h_scr[...] = (y * (1.0 + sc_ref[0]) + sh_ref[0]).astype(BF16)

    def mm(lo, width):
        return jnp.dot(h_scr[...], w_ref[:, lo:lo + width], preferred_element_type=F32)

    cos = cos_ref[...]
    sin = sin_ref[...]
    first_half = (lax.broadcasted_iota(jnp.int32, cos.shape, 1) % NSA_HD) < (NSA_HD // 2)
    lane = lax.broadcasted_iota(jnp.int32, (tm, LANES), 1)
    half_of_lane = lane // NSA_HD

    def own_lanes(t, src_half, dst_half):
        moved = t if src_half == dst_half else pltpu.roll(t, NSA_HD, 1)
        return jnp.where(half_of_lane == dst_half, moved, 0.0)

    for j in range(4):
        hg_ref[0, :, j * HG_WIDTH:(j + 1) * HG_WIDTH] = mm(C_HG + j * HG_WIDTH, HG_WIDTH)
    for m in range(NSA_WIDTH // (2 * LANES)):
        qq = mm(C_Q + m * 2 * LANES, 2 * LANES) * (NSA_HD ** -0.5)
        for c in range(2):
            q = qq[:, c * LANES:(c + 1) * LANES]
            qr = _rope(q, cos, sin, first_half) * LOG2E
            for half in range(2):
                n = (2 * m + c) * 2 + half
                g = n // NSA_HG
                qraw_ref[0, :, n * LANES:(n + 1) * LANES] = own_lanes(q, half, g).astype(BF16)
                qrot_ref[0, :, n * LANES:(n + 1) * LANES] = own_lanes(qr, half, g).astype(BF16)
    kv = mm(C_K, 2 * LANES)
    kc_ref[0] = kv[:, :LANES].astype(BF16)
    vc_ref[0] = kv[:, LANES:].astype(BF16)
    pos = pl.program_id(1) * tm + lax.broadcasted_iota(jnp.int32, (tm, LANES), 0)
    ks_ref[0, :, LANES:2 * LANES] = jnp.where(lane == pos // SEL_BLOCK, 1.0, 0.0).astype(BF16)
    for j in range(2):
        kv = mm(C_K + (j + 1) * 2 * LANES, 2 * LANES)
        k_rot = _rope(kv[:, :LANES], cos, sin, first_half).astype(BF16)
        if j == 0:
            ks_ref[0, :, 0:LANES] = k_rot
        else:
            kw_ref[0] = k_rot
        for g in range(NSA_KV_GROUPS):
            v = own_lanes(kv[:, LANES:], g, 0)
            col = (j * NSA_KV_GROUPS + g) * LANES
            va_ref[0, :, col:col + LANES] = jnp.where(lane == NSA_HD, 1.0, v).astype(BF16)
    gate = _sigmoid(mm(C_GATE, LANES))
    per_group = 3 * NSA_HG
    for g in range(NSA_KV_GROUPS):
        gate_ref[0, :, g * LANES:(g + 1) * LANES] = gate if g == 0 else pltpu.roll(gate, LANES - g * per_group, 1)


def _inproj(x, mod3, norm_g, w_p, cos, sin, tm):
    B, S, D = x.shape
    blk = lambda w: pl.BlockSpec((1, tm, w), lambda b, i: (b, i, 0))
    return pl.pallas_call(
        _inproj_kernel,
        out_shape=(jax.ShapeDtypeStruct((B, S, 4 * HG_WIDTH), F32),
                   jax.ShapeDtypeStruct((B, S, NSA_HEADS * LANES), BF16),
                   jax.ShapeDtypeStruct((B, S, NSA_HEADS * LANES), BF16),
                   jax.ShapeDtypeStruct((B, S, LANES), BF16),
                   jax.ShapeDtypeStruct((B, S, LANES), BF16),
                   jax.ShapeDtypeStruct((B, S, 2 * LANES), BF16),
                   jax.ShapeDtypeStruct((B, S, LANES), BF16),
                   jax.ShapeDtypeStruct((B, S, 4 * LANES), BF16),
                   jax.ShapeDtypeStruct((B, S, NSA_KV_GROUPS * LANES), F32)),
        grid=(B, S // tm),
        in_specs=[blk(D),
                  pl.BlockSpec((1, 1, D), lambda b, i: (b, 0, 0)),
                  pl.BlockSpec((1, 1, D), lambda b, i: (b, 0, 1)),
                  pl.BlockSpec((1, D), lambda b, i: (0, 0)),
                  pl.BlockSpec((D, IN_COLS_P), lambda b, i: (0, 0)),
                  pl.BlockSpec((tm, LANES), lambda b, i: (i, 0)),
                  pl.BlockSpec((tm, LANES), lambda b, i: (i, 0))],
        out_specs=(blk(4 * HG_WIDTH), blk(NSA_HEADS * LANES), blk(NSA_HEADS * LANES),
                   blk(LANES), blk(LANES), blk(2 * LANES), blk(LANES), blk(4 * LANES), blk(NSA_KV_GROUPS * LANES)),
        scratch_shapes=[pltpu.VMEM((tm, D), BF16)],
        compiler_params=_cparams(("arbitrary", "arbitrary")),
        name="inproj",
    )(x, mod3, mod3, norm_g, w_p, cos, sin)


def _hgrn_kernel(q_ref, f_ref, i_ref, gt_ref, lb_ref, ng_ref, o_ref):
    S = q_ref.shape[1]
    C, U = HG_CHUNK, HG_SUB
    lb = lb_ref[...]
    ng = ng_ref[...]
    ri = lax.broadcasted_iota(jnp.int32, (C, C), 0)
    ci = lax.broadcasted_iota(jnp.int32, (C, C), 1)
    tril = (ri >= ci).astype(F32)
    trow = lax.broadcasted_iota(jnp.int32, (U, 1), 0)

    def chunk(c, st_t):
        r0 = pl.multiple_of(c * C, C)
        rows = pl.ds(r0, C)
        f = lb + (1.0 - lb) * _sigmoid(f_ref[0, rows, :])
        kk = 1.0 - f
        b = jnp.dot(tril, jnp.log(f), precision=HIGHEST, preferred_element_type=F32)
        q = q_ref[0, rows, :] * (HG_DK ** -0.5)
        v = i_ref[0, rows, :]
        vb = v.astype(BF16)
        o_inter = lax.dot_general((q * jnp.exp(b)).astype(BF16), st_t.astype(BF16), NT,
                                  preferred_element_type=F32)
        parts = []
        for i in range(C // U):
            lo = i * U
            bi = b[lo:lo + U]
            qi = q[lo:lo + U]
            if i == 0:
                oi = jnp.zeros((U, HG_DV), F32)
            else:
                r = b[lo - 1:lo]
                qrel = (qi * jnp.exp(bi - r)).astype(BF16)
                kprev = (kk[:lo] * jnp.exp(r - b[:lo])).astype(BF16)
                a_off = lax.dot_general(qrel, kprev, NT, preferred_element_type=F32)
                oi = jnp.dot(a_off.astype(BF16), vb[:lo], preferred_element_type=F32)
            for s in range(U):
                valid = trow >= s
                e = jnp.exp(jnp.where(valid, bi - bi[s:s + 1], 0.0))
                a = jnp.sum(qi * e * kk[lo + s:lo + s + 1], axis=-1, keepdims=True)
                oi = oi + jnp.where(valid, a, 0.0) * v[lo + s:lo + s + 1]
            parts.append(oi)
        o = o_inter + jnp.concatenate(parts, axis=0)
        o = o * lax.rsqrt(jnp.mean(o * o, axis=-1, keepdims=True) + EPS) * ng
        o_ref[0, rows, :] = (o * _silu(gt_ref[0, rows, :])).astype(BF16)
        bl = b[C - 1:C]
        kv_t = lax.dot_general(vb, (kk * jnp.exp(bl - b)).astype(BF16), TN, preferred_element_type=F32)
        return jnp.exp(bl) * st_t + kv_t

    lax.fori_loop(0, S // C, chunk, jnp.zeros((HG_DV, HG_DK), F32))


def _cumsum_rows(x):
    n = x.shape[0]
    row = lax.broadcasted_iota(jnp.int32, x.shape, 0)
    d = 1
    while d < n:
        x = x + jnp.where(row >= d, pltpu.roll(x, d, 0), 0.0)
        d *= 2
    return x


def _hgrn_mxu_kernel(hg_ref, lb_ref, ng_ref, o_ref, st_scr):
    ts = hg_ref.shape[1]
    C, U = HG_CHUNK, HG_SUB

    @pl.when(pl.program_id(1) == 0)
    def _():
        st_scr[...] = jnp.zeros(st_scr.shape, F32)

    W = HG_WIDTH
    NH = HG_HEADS
    head_of_lane = lax.broadcasted_iota(jnp.int32, (1, W), 1) // HG_DK
    hcols = [slice(h * HG_DK, (h + 1) * HG_DK) for h in range(NH)]

    def chunk(c, carry):
        rows = pl.ds(pl.multiple_of(c * C, C), C)
        lb = lb_ref[...]
        f = lb + (1.0 - lb) * _sigmoid(hg_ref[0, rows, W:2 * W])
        kk = 1.0 - f
        b = _cumsum_rows(jnp.log(f))
        q = hg_ref[0, rows, 0:W] * (HG_DK ** -0.5)
        vb = hg_ref[0, rows, 2 * W:3 * W].astype(BF16)
        qe = (q * jnp.exp(b)).astype(BF16)
        o_inter = jnp.concatenate(
            [lax.dot_general(qe[:, hc], st_scr[h].astype(BF16), NT, preferred_element_type=F32)
             for h, hc in enumerate(hcols)], axis=1)
        parts = []
        for i in range(C // U):
            lo, hi = i * U, (i + 1) * U
            r = b[lo - 1:lo] if i else jnp.zeros((1, W), F32)
            qrel = q[lo:hi] * jnp.exp(b[lo:hi] - r)
            kall = (kk[:hi] * jnp.exp(r - b[:hi])).astype(BF16)
            qbd = jnp.concatenate([jnp.where(head_of_lane == h, qrel, 0.0) for h in range(NH)], axis=0)
            a = lax.dot_general(qbd.astype(BF16), kall, NT, preferred_element_type=F32)
            trow = lax.broadcasted_iota(jnp.int32, (NH * U, hi), 0) % U
            a = jnp.where(lax.broadcasted_iota(jnp.int32, (NH * U, hi), 1) <= lo + trow, a, 0.0)
            oa = jnp.dot(a.astype(BF16), vb[:hi], preferred_element_type=F32)
            oi = jnp.where(head_of_lane == 0, oa[0:U], 0.0)
            for h in range(1, NH):
                oi = jnp.where(head_of_lane == h, oa[h * U:(h + 1) * U], oi)
            parts.append(oi)
        o = o_inter + jnp.concatenate(parts, axis=0)
        o = jnp.concatenate(
            [o[:, hc] * lax.rsqrt(jnp.mean(o[:, hc] * o[:, hc], axis=-1, keepdims=True) + EPS) for hc in hcols], axis=1)
        o_ref[0, rows, :] = (o * ng_ref[...] * _silu(hg_ref[0, rows, 3 * W:4 * W])).astype(BF16)
        bl = b[C - 1:C]
        ke = (kk * jnp.exp(bl - b)).astype(BF16)
        decay = jnp.exp(bl)
        for h, hc in enumerate(hcols):
            kv_t = lax.dot_general(vb[:, hc], ke[:, hc], TN, preferred_element_type=F32)
            st_scr[h] = decay[:, hc] * st_scr[h] + kv_t
        return carry

    lax.fori_loop(0, ts // C, chunk, 0, unroll=8)


def _hgrn_mxu(hg, lb, ng, ts):
    B, S, _ = hg.shape
    vec = pl.BlockSpec((1, HG_WIDTH), lambda b, i: (0, 0))
    return pl.pallas_call(
        _hgrn_mxu_kernel,
        out_shape=jax.ShapeDtypeStruct((B, S, HG_WIDTH), BF16),
        grid=(B, S // ts),
        in_specs=[pl.BlockSpec((1, ts, 4 * HG_WIDTH), lambda b, i: (b, i, 0)), vec, vec],
        out_specs=pl.BlockSpec((1, ts, HG_WIDTH), lambda b, i: (b, i, 0)),
        scratch_shapes=[pltpu.VMEM((HG_HEADS, HG_DV, HG_DK), F32)],
        compiler_params=_cparams(("arbitrary", "arbitrary")),
        name="hgrn_mxu",
    )(hg, lb, ng)


def _hgrn(hg, lb, ng):
    B, S, _ = hg.shape
    col = lambda k: pl.BlockSpec((1, S, HG_DK), lambda b, h, k=k: (b, 0, k * HG_HEADS + h))
    vec = pl.BlockSpec((1, HG_DK), lambda b, h: (0, h))
    return pl.pallas_call(
        _hgrn_kernel,
        out_shape=jax.ShapeDtypeStruct((B, S, HG_WIDTH), BF16),
        grid=(B, HG_HEADS),
        in_specs=[col(0), col(1), col(2), col(3), vec, vec],
        out_specs=pl.BlockSpec((1, S, HG_DV), lambda b, h: (b, 0, h)),
        compiler_params=_cparams(("arbitrary", "arbitrary")),
        name="hgrn",
    )(hg, hg, hg, hg, lb, ng)


def _cmpmlp_kernel(x_ref, w1a_ref, w1b_ref, pe_ref, w1_ref, b1_ref, w2_ref, o_ref):
    x = x_ref[0]
    hb = jnp.dot(pe_ref[...], w1_ref[...], precision=HIGHEST, preferred_element_type=F32)[0:1] + b1_ref[...]
    nrow = x.shape[0]
    for g in range(NSA_KV_GROUPS):
        a = jnp.dot(x, w1a_ref[g], preferred_element_type=F32)
        bm = jnp.dot(x, w1b_ref[g], preferred_element_type=F32)
        hdn = a + pltpu.roll(bm, nrow - 1, 0) + hb
        o_ref[0, g] = jnp.dot(_gelu_tanh(hdn).astype(BF16), w2_ref[g], preferred_element_type=F32).astype(BF16)


def _cmpmlp(x2, w1a, w1b, pe8, w1, b1, w2p):
    B, ncb, width = x2.shape
    full = lambda a: pl.BlockSpec(a.shape, lambda b: (0,) * a.ndim)
    return pl.pallas_call(
        _cmpmlp_kernel,
        out_shape=jax.ShapeDtypeStruct((B, NSA_KV_GROUPS, ncb, LANES), BF16),
        grid=(B,),
        in_specs=[pl.BlockSpec((1, ncb, width), lambda b: (b, 0, 0)),
                  full(w1a), full(w1b), full(pe8), full(w1), full(b1), full(w2p)],
        out_specs=pl.BlockSpec((1, NSA_KV_GROUPS, ncb, LANES), lambda b: (b, 0, 0, 0)),
        compiler_params=_cparams(("arbitrary",)),
        name="cmpmlp",
    )(x2, w1a, w1b, pe8, w1, b1, w2p)


def _rank_before(score, nrows):
    jrow = lax.broadcasted_iota(jnp.int32, score.shape, 0)
    rank = jnp.zeros(score.shape, F32)
    for k in range(nrows):
        rk = score[k:k + 1]
        beats = (rk > score) | ((rk == score) & (jrow > k))
        rank = rank + jnp.where(beats, 1.0, 0.0)
    return rank


def _topk_rows(score, k):
    n = score.shape[0]
    row = lax.broadcasted_iota(jnp.int32, score.shape, 0).astype(F32)
    keep = jnp.zeros(score.shape, F32)
    for _ in range(k):
        top = jnp.max(score, axis=0, keepdims=True)
        first = jnp.min(jnp.where(score == top, row, float(n)), axis=0, keepdims=True)
        pick = row == first
        keep = jnp.where(pick, 1.0, keep)
        score = jnp.where(pick, -jnp.inf, score)
    return keep


def _transpose_small_ints(xt):
    c = xt.shape[1]
    eye = (lax.broadcasted_iota(jnp.int32, (c, c), 0) == lax.broadcasted_iota(jnp.int32, (c, c), 1)).astype(BF16)
    return lax.dot_general(eye, xt.astype(BF16), NT, preferred_element_type=F32)


def _pack_heads(heads):
    low = lax.broadcasted_iota(jnp.int32, heads[0].shape, 1) < NSA_HD
    return jnp.concatenate([jnp.where(low, heads[k], pltpu.roll(heads[k + 1], NSA_HD, 1))
                            for k in range(0, len(heads), 2)], axis=1)


def _cmpsel_kernel(q_ref, kc_ref, vc_ref, gate_ref, o_ref, sel_ref):
    tq = q_ref.shape[1]
    ncb = kc_ref.shape[2]
    t = pl.program_id(2)
    kc = kc_ref[0, 0]
    vc = vc_ref[0, 0]
    pos = t * tq + lax.broadcasted_iota(jnp.int32, (tq, 1), 0)
    cblk = lax.broadcasted_iota(jnp.int32, (1, ncb), 1)
    vis = (cblk * CMP_STRIDE + CMP_BLOCK - 1) <= pos
    psum = jnp.zeros((tq, ncb), F32)
    gate = gate_ref[0]
    heads = []
    for h in range(NSA_HG):
        s = lax.dot_general(q_ref[0, :, h * LANES:(h + 1) * LANES], kc, NT, preferred_element_type=F32)
        s = jnp.where(vis, s, NEG)
        p = jnp.exp(s - jnp.max(s, axis=-1, keepdims=True))
        p = jnp.where(vis, p / jnp.sum(p, axis=-1, keepdims=True), 0.0)
        heads.append(jnp.dot(p.astype(BF16), vc, preferred_element_type=F32) * gate[:, 3 * h:3 * h + 1])
        psum = psum + p
    o_ref[0] = _pack_heads(heads)
    nsb = ncb * CMP_STRIDE // SEL_BLOCK
    jb = lax.broadcasted_iota(jnp.int32, (nsb, ncb), 0) * SEL_BLOCK
    cb = lax.broadcasted_iota(jnp.int32, (nsb, ncb), 1) * CMP_STRIDE
    ov = jnp.maximum(jnp.minimum(cb + CMP_BLOCK, jb + SEL_BLOCK) - jnp.maximum(cb, jb), 0).astype(F32) / CMP_BLOCK
    ps_hi = psum.astype(BF16)
    ps_lo = (psum - ps_hi.astype(F32)).astype(BF16)
    ovb = ov.astype(BF16)
    pslc_t = (lax.dot_general(ovb, ps_hi, NT, preferred_element_type=F32)
              + lax.dot_general(ovb, ps_lo, NT, preferred_element_type=F32))
    posl = t * tq + lax.broadcasted_iota(jnp.int32, (1, tq), 1)
    cur = posl // SEL_BLOCK
    jrow = lax.broadcasted_iota(jnp.int32, (nsb, tq), 0)
    forced = (jrow == 0) | (jrow == cur) | (jrow == cur - 1)
    score = jnp.where(forced, 1e30, jnp.where(jrow <= cur, pslc_t, NEG))
    chosen = _topk_rows(score, min(N_SEL, nsb)) > 0.5
    drop = jnp.where(chosen & (score > -1e29), 0.0, 1.0)
    drop = jnp.concatenate([drop, jnp.zeros((LANES - nsb, tq), F32)], axis=0)
    sel_ref[0, 0] = (_transpose_small_ints(drop) * NEG).astype(BF16)


def _cmpsel(qraw, kcmp, vcmp, gate, tq):
    B, S, _ = qraw.shape
    ncb = kcmp.shape[2]
    gw = NSA_HG * LANES
    assert S // SEL_BLOCK <= NSA_HD
    cmp_spec = pl.BlockSpec((1, 1, ncb, LANES), lambda b, g, t: (b, g, 0, 0))
    return pl.pallas_call(
        _cmpsel_kernel,
        out_shape=(jax.ShapeDtypeStruct((B, S, NSA_WIDTH), F32),
                   jax.ShapeDtypeStruct((B, NSA_KV_GROUPS, S, LANES), BF16)),
        grid=(B, NSA_KV_GROUPS, S // tq),
        in_specs=[pl.BlockSpec((1, tq, gw), lambda b, g, t: (b, t, g)), cmp_spec, cmp_spec,
                  pl.BlockSpec((1, tq, LANES), lambda b, g, t: (b, t, g))],
        out_specs=(pl.BlockSpec((1, tq, NSA_WIDTH // NSA_KV_GROUPS), lambda b, g, t: (b, t, g)),
                   pl.BlockSpec((1, 1, tq, LANES), lambda b, g, t: (b, g, t, 0))),
        compiler_params=_cparams(("arbitrary", "arbitrary", "arbitrary")),
        name="cmpsel",
    )(qraw, kcmp, vcmp, gate)


def _slcwin_kernel(q_ref, sel_ref, ks_ref, kw_ref, vs_ref, vw_ref, gate_ref, ocmp_ref, o_ref, *, tk, wk):
    tq = q_ref.shape[1]
    S = ks_ref.shape[1]
    t = pl.program_id(2)
    q0 = t * tq
    qpos = q0 + lax.broadcasted_iota(jnp.int32, (tq, 1), 0)
    qw = [q_ref[0, :, h * LANES:(h + 1) * LANES] for h in range(NSA_HG)]
    sel = sel_ref[0, 0]
    qa = [jnp.concatenate([q, sel], axis=1) for q in qw]

    gate = gate_ref[0]

    def normalise(acc, g):
        return acc * (g / acc[:, NSA_HD:NSA_HD + 1])

    nchain = NSA_HG // SLC_STACK
    qst = [jnp.concatenate(qa[c * SLC_STACK:(c + 1) * SLC_STACK], axis=0) for c in range(nchain)]
    qpos_st = jnp.concatenate([qpos] * SLC_STACK, axis=0)

    def kstep(j, carry, diag):
        k0 = pl.multiple_of(j * tk, tk)
        ks = ks_ref[0, pl.ds(k0, tk), :]
        vs = vs_ref[0, pl.ds(k0, tk), :]
        if diag:
            vis = (k0 + lax.broadcasted_iota(jnp.int32, (1, tk), 1)) <= qpos_st
        out = []
        for c in range(nchain):
            m, acc = carry[c]
            s = lax.dot_general(qst[c], ks, NT, preferred_element_type=F32)
            if diag:
                s = jnp.where(vis, s, NEG)
            mn = jnp.maximum(m, jnp.max(s, axis=-1, keepdims=True))
            p = jnp.exp2(s - mn)
            acc = jnp.exp2(m - mn) * acc + jnp.dot(p.astype(BF16), vs, preferred_element_type=F32)
            out.append((mn, acc))
        return tuple(out)

    init = tuple((jnp.full((SLC_STACK * tq, 1), NEG, F32), jnp.zeros((SLC_STACK * tq, LANES), F32))
                 for _ in range(nchain))
    jdiag = q0 // tk
    carry = lax.fori_loop(0, jdiag, lambda j, c: kstep(j, c, False), init)
    carry = kstep(jdiag, carry, True)
    heads = [normalise(carry[h // SLC_STACK][1][(h % SLC_STACK) * tq:(h % SLC_STACK + 1) * tq],
                       gate[:, 3 * h + 1:3 * h + 2]) for h in range(NSA_HG)]

    nblk = S // tq
    start = pl.multiple_of(jnp.clip(t - WINDOW // tq, 0, nblk - wk // tq) * tq, tq)
    kw = kw_ref[0, pl.ds(start, wk), :]
    vw = vw_ref[0, pl.ds(start, wk), :]
    d = qpos - (start + lax.broadcasted_iota(jnp.int32, (1, wk), 1))
    okw = (d >= 0) & (d < WINDOW)
    okw_st = jnp.concatenate([okw] * WIN_STACK, axis=0)
    for c in range(NSA_HG // WIN_STACK):
        q_st = jnp.concatenate(qw[c * WIN_STACK:(c + 1) * WIN_STACK], axis=0)
        s = jnp.where(okw_st, lax.dot_general(q_st, kw, NT, preferred_element_type=F32), NEG)
        p = jnp.exp2(s - jnp.max(s, axis=-1, keepdims=True))
        o_st = jnp.dot(p.astype(BF16), vw, preferred_element_type=F32)
        for k in range(WIN_STACK):
            h = c * WIN_STACK + k
            heads[h] = heads[h] + normalise(o_st[k * tq:(k + 1) * tq], gate[:, 3 * h + 2:3 * h + 3])
    o_ref[0] = (ocmp_ref[0] + _pack_heads(heads)).astype(BF16)


def _slcwin(qrot, sel, ks, kw, vaug, gate, ocmp, tq, tk):
    B, S, _ = qrot.shape
    gw = NSA_HG * LANES
    wk = min(WINDOW + tq, S)
    ospec = pl.BlockSpec((1, tq, NSA_WIDTH // NSA_KV_GROUPS), lambda b, g, t: (b, t, g))
    return pl.pallas_call(
        functools.partial(_slcwin_kernel, tk=tk, wk=wk),
        out_shape=jax.ShapeDtypeStruct((B, S, NSA_WIDTH), BF16),
        grid=(B, NSA_KV_GROUPS, S // tq),
        in_specs=[pl.BlockSpec((1, tq, gw), lambda b, g, t: (b, t, g)),
                  pl.BlockSpec((1, 1, tq, LANES), lambda b, g, t: (b, g, t, 0)),
                  pl.BlockSpec((1, S, 2 * LANES), lambda b, g, t: (b, 0, 0)),
                  pl.BlockSpec((1, S, LANES), lambda b, g, t: (b, 0, 0)),
                  pl.BlockSpec((1, S, LANES), lambda b, g, t: (b, 0, g)),
                  pl.BlockSpec((1, S, LANES), lambda b, g, t: (b, 0, NSA_KV_GROUPS + g)),
                  pl.BlockSpec((1, tq, LANES), lambda b, g, t: (b, t, g)),
                  ospec],
        out_specs=ospec,
        compiler_params=_cparams(("arbitrary", "arbitrary", "arbitrary")),
        name="slcwin",
    )(qrot, sel, ks, kw, vaug, vaug, gate, ocmp)


def _outproj_kernel(x_ref, ohg_ref, onsa_ref, wout_ref,
                    g1_ref, sh2_ref, sc2_ref, n2_ref, rwt_ref, rb_ref,
                    x1_ref, h2_ref, sc_ref, slot_t_ref, w_t_ref, cnt_ref):
    tm = x_ref.shape[1]
    mix = jnp.dot(jnp.concatenate([ohg_ref[0], onsa_ref[0]], axis=1), wout_ref[...], preferred_element_type=F32)
    x1 = x_ref[0] + g1_ref[0] * mix
    x1_ref[0] = x1
    h2 = (x1 * lax.rsqrt(jnp.mean(x1 * x1, axis=-1, keepdims=True) + EPS) * n2_ref[...]) * (1.0 + sc2_ref[0]) + sh2_ref[0]
    h_hi = h2.astype(BF16)
    h2_ref[0] = h_hi
    h_lo = (h2 - h_hi.astype(F32)).astype(BF16)
    both = lax.dot_general(rwt_ref[...], h_hi, NT, preferred_element_type=F32)
    logits = (both[:N_EXPERTS] + both[N_EXPERTS:]
              + lax.dot_general(rwt_ref[:N_EXPERTS, :], h_lo, NT, preferred_element_type=F32))
    scores = _sigmoid(logits)
    choice = scores + rb_ref[...]
    per = N_EXPERTS // N_GROUPS
    c3 = choice.reshape(N_GROUPS, per, tm)
    erow = lax.broadcasted_iota(jnp.int32, c3.shape, 1)
    rank_in = jnp.zeros(c3.shape, F32)
    for k in range(per):
        ck = c3[:, k:k + 1, :]
        rank_in = rank_in + jnp.where((ck > c3) | ((ck == c3) & (erow > k)), 1.0, 0.0)
    grp_score = jnp.sum(jnp.where(rank_in < 2, c3, 0.0), axis=1)
    grp_keep = _rank_before(grp_score, N_GROUPS) < TOPK_GROUPS
    masked = jnp.where(grp_keep[:, None, :], c3, -jnp.inf).reshape(N_EXPERTS, tm)
    keep = _topk_rows(masked, TOP_K) > 0.5
    tw = jnp.where(keep, scores, 0.0)
    tw = tw / jnp.sum(tw, axis=0, keepdims=True) * ROUTED_SCALE
    before = lax.broadcasted_iota(jnp.int32, (tm, tm), 0)
    after = lax.broadcasted_iota(jnp.int32, (tm, tm), 1)
    earlier = (before < after) & (before // MOE_SUB == after // MOE_SUB)
    kept = jnp.where(keep, 1.0, 0.0)
    pref = jnp.dot(kept.astype(BF16), earlier.astype(BF16), preferred_element_type=F32)
    slot = jnp.where(keep, pref, -1.0)
    assert MOE_SUB <= 256
    sc_ref[0] = _transpose_small_ints(slot)
    for j in range(tm // MOE_SUB):
        sub = slice(j * MOE_SUB, (j + 1) * MOE_SUB)
        slot_t_ref[:, j] = slot[:, sub].reshape(N_EXPERTS // MOE_EG, MOE_EG, MOE_SUB)
        w_t_ref[:, j] = tw[:, sub].reshape(N_EXPERTS // MOE_EG, MOE_EG, MOE_SUB)
        cnt_ref[j] = jnp.broadcast_to(jnp.sum(kept[:, sub], axis=1, keepdims=True), (N_EXPERTS, LANES))


def _outproj(x, ohg, onsa, wout, mod3, n2, rwt, rb, tm):
    B, S, D = x.shape
    blk = lambda w: pl.BlockSpec((1, tm, w), lambda b, i: (b, i, 0))
    full = lambda a: pl.BlockSpec(a.shape, lambda b, i: (0,) * a.ndim)
    modc = lambda k: pl.BlockSpec((1, 1, D), lambda b, i, k=k: (b, 0, k))
    nt = S // tm
    ng = N_EXPERTS // MOE_EG
    nh = tm // MOE_SUB
    tspec = pl.BlockSpec((ng, nh, MOE_EG, MOE_SUB), lambda b, i: (0, b * nt + i, 0, 0))
    return pl.pallas_call(
        _outproj_kernel,
        out_shape=(jax.ShapeDtypeStruct((B, S, D), F32),
                   jax.ShapeDtypeStruct((B, S, D), BF16),
                   jax.ShapeDtypeStruct((B, S, N_EXPERTS), F32),
                   jax.ShapeDtypeStruct((ng, B * nt * nh, MOE_EG, MOE_SUB), F32),
                   jax.ShapeDtypeStruct((ng, B * nt * nh, MOE_EG, MOE_SUB), F32),
                   jax.ShapeDtypeStruct((B * nt * nh, N_EXPERTS, LANES), F32)),
        grid=(B, nt),
        in_specs=[blk(D), blk(HG_WIDTH), blk(NSA_WIDTH), full(wout),
                  modc(2), modc(3), modc(4), full(n2), full(rwt), full(rb)],
        out_specs=(blk(D), blk(D), blk(N_EXPERTS), tspec, tspec,
                   pl.BlockSpec((nh, N_EXPERTS, LANES), lambda b, i: (b * nt + i, 0, 0))),
        compiler_params=_cparams(("arbitrary", "arbitrary")),
        name="outproj",
    )(x, ohg, onsa, wout, mod3, mod3, mod3, n2, rwt, rb)


def _swiglu_hidden(x, wgu):
    gu = jnp.dot(x, wgu, preferred_element_type=F32)
    return _silu(gu[:, :EXPERT_HIDDEN]) * gu[:, EXPERT_HIDDEN:]


def _moe_kernel(rounds_ref, h_ref, slot_t_ref, w_t_ref, slot_c_ref, wgu_ref, wdn_ref, sgu_ref, sdn_ref,
                o_ref, x_scr, y_scr, p_scr):
    i = pl.program_id(0)
    g = pl.program_id(1)
    tm = h_ref.shape[0]
    nsub = tm // MOE_SUB
    rnd, eg = MOE_ROUND, MOE_EG
    ng = N_EXPERTS // eg
    base = i * (N_EXPERTS + ng)

    @pl.when(g == 0)
    def _():
        act = _swiglu_hidden(h_ref[...], sgu_ref[...]).astype(BF16)
        o_ref[...] = jnp.dot(act, sdn_ref[...], preferred_element_type=F32)

    rslot = lax.broadcasted_iota(jnp.int32, (rnd, 1), 0).astype(F32)
    lane_slot = (lax.broadcasted_iota(jnp.int32, (1, eg * rnd), 1) % rnd).astype(F32)
    spread = (lax.broadcasted_iota(jnp.int32, (N_EXPERTS, eg * rnd), 1) // rnd + g * eg
              == lax.broadcasted_iota(jnp.int32, (N_EXPERTS, eg * rnd), 0)).astype(BF16)

    def one_round(r, carry):
        first = (r * rnd).astype(F32)
        for s in range(nsub):
            st = slot_t_ref[0, s] - first
            p = jnp.concatenate([jnp.where(rslot == st[e:e + 1, :], 1.0, 0.0) for e in range(eg)], axis=0)
            p_scr[s] = p.astype(BF16)
            x_scr[s] = jnp.dot(p_scr[s], h_ref[s * MOE_SUB:(s + 1) * MOE_SUB, :],
                               preferred_element_type=F32).astype(BF16)
        for e in range(eg):
            rows = slice(e * rnd, (e + 1) * rnd)

            @pl.when(r < rounds_ref[base + g * eg + e])
            def _():
                xe = jnp.concatenate([x_scr[s, rows, :] for s in range(nsub)], axis=0)
                wslot = jnp.concatenate(
                    [jnp.sum(p_scr[s, rows, :].astype(F32) * w_t_ref[0, s][e:e + 1, :], axis=-1, keepdims=True)
                     for s in range(nsub)], axis=0)
                y = jnp.dot((_swiglu_hidden(xe, wgu_ref[e]) * wslot).astype(BF16), wdn_ref[e],
                            preferred_element_type=F32)
                for s in range(nsub):
                    y_scr[s, rows, :] = y[s * rnd:(s + 1) * rnd].astype(BF16)

            @pl.when(r >= rounds_ref[base + g * eg + e])
            def _():
                for s in range(nsub):
                    y_scr[s, rows, :] = jnp.zeros((rnd, D_MODEL), BF16)
        for s in range(nsub):
            toks = slice(s * MOE_SUB, (s + 1) * MOE_SUB)
            sc = jnp.dot(slot_c_ref[toks, :].astype(BF16), spread, preferred_element_type=F32) - first
            pt = jnp.where(sc == lane_slot, 1.0, 0.0).astype(BF16)
            o_ref[toks, :] += jnp.dot(pt, y_scr[s], preferred_element_type=F32)
        return carry

    lax.fori_loop(0, rounds_ref[base + N_EXPERTS + g], one_round, 0)


def _moe(rounds, h2, slot_t, w_t, slot_c, wgu, wdn, sgu, sdn, tm):
    T, D = h2.shape
    eg, nsub = MOE_EG, tm // MOE_SUB
    full = lambda a: pl.BlockSpec(a.shape, lambda i, e, o: (0,) * a.ndim)
    tspec = pl.BlockSpec((1, nsub, eg, MOE_SUB), lambda i, e, o: (e, i, 0, 0))
    return pl.pallas_call(
        _moe_kernel,
        out_shape=jax.ShapeDtypeStruct((T, D), F32),
        grid_spec=pltpu.PrefetchScalarGridSpec(
            num_scalar_prefetch=1,
            grid=(T // tm, N_EXPERTS // eg),
            in_specs=[pl.BlockSpec((tm, D), lambda i, e, o: (i, 0), pipeline_mode=pl.Buffered(1)), tspec, tspec,
                      pl.BlockSpec((tm, N_EXPERTS), lambda i, e, o: (i, 0)),
                      pl.BlockSpec((eg, D, 2 * EXPERT_HIDDEN), lambda i, e, o: (e, 0, 0)),
                      pl.BlockSpec((eg, EXPERT_HIDDEN, D), lambda i, e, o: (e, 0, 0)),
                      full(sgu), full(sdn)],
            out_specs=pl.BlockSpec((tm, D), lambda i, e, o: (i, 0), pipeline_mode=pl.Buffered(1)),
            scratch_shapes=[pltpu.VMEM((nsub, eg * MOE_ROUND, D), BF16),
                            pltpu.VMEM((nsub, eg * MOE_ROUND, D), BF16),
                            pltpu.VMEM((nsub, eg * MOE_ROUND, MOE_SUB), BF16)]),
        compiler_params=_cparams(("arbitrary", "arbitrary")),
        name="moe",
    )(rounds, h2, slot_t, w_t, slot_c, wgu, wdn, sgu, sdn)


def _final_kernel(x1_ref, moe_ref, g2_ref, fg_ref, o_ref):
    x2 = x1_ref[0] + g2_ref[0] * moe_ref[0]
    o_ref[0] = x2 * lax.rsqrt(jnp.mean(x2 * x2, axis=-1, keepdims=True) + EPS) * fg_ref[...]


def _final(x1, moe, mod3, fg, tm):
    B, S, D = x1.shape
    blk = pl.BlockSpec((1, tm, D), lambda b, i: (b, i, 0))
    return pl.pallas_call(
        _final_kernel,
        out_shape=jax.ShapeDtypeStruct((B, S, D), F32),
        grid=(B, S // tm),
        in_specs=[blk, blk, pl.BlockSpec((1, 1, D), lambda b, i: (b, 0, 5)), pl.BlockSpec((1, D), lambda b, i: (0, 0))],
        out_specs=blk,
        compiler_params=_cparams(("arbitrary", "arbitrary")),
        name="final",
    )(x1, moe, mod3, fg)


def _split_bf16(w):
    hi = w.astype(BF16)
    return jnp.concatenate([hi, (w - hi.astype(F32)).astype(BF16)], axis=0)


def _pack_w_in(w_in):
    return jnp.pad(w_in, ((0, 0), (0, IN_COLS_P - w_in.shape[1]))).astype(BF16)


def _pack_cmp(pos, w1, b1, w2, lane_by_group):
    half = CMP_STRIDE * NSA_HD
    def rows_for(wh):
        w3 = wh.reshape(CMP_STRIDE, NSA_HD, CMP_HIDDEN)
        z = jnp.zeros_like(w3)
        return jnp.stack([jnp.concatenate([w3, z], axis=1).reshape(CMP_STRIDE * LANES, CMP_HIDDEN),
                          jnp.concatenate([z, w3], axis=1).reshape(CMP_STRIDE * LANES, CMP_HIDDEN)])
    w1a = rows_for(w1[:half]).astype(BF16)
    w1b = rows_for(w1[half:]).astype(BF16)
    z2 = jnp.zeros_like(w2)
    w2_first = jnp.concatenate([w2, z2], axis=1)
    w2p = jnp.stack([w2_first, jnp.concatenate([z2, w2], axis=1) if lane_by_group else w2_first]).astype(BF16)
    pe8 = jnp.pad(pos.reshape(1, CMP_BLOCK * NSA_HD), ((0, 7), (0, 0)))
    return w1a, w1b, pe8, w1, b1.reshape(1, CMP_HIDDEN), w2p


def _rope_tables(S):
    half = NSA_HD // 2
    inv = ROPE_THETA ** (-jnp.arange(half, dtype=F32) / half)
    ang = jnp.arange(S, dtype=F32)[:, None] * inv[None, :]
    cos, sin = jnp.cos(ang), jnp.sin(ang)
    reps = LANES // NSA_HD
    return jnp.tile(jnp.concatenate([cos, cos], axis=1), (1, reps)), jnp.tile(jnp.concatenate([-sin, sin], axis=1), (1, reps))


def _tiles(S):
    return dict(inproj=min(512, S), hgrn=min(512, S), cmpsel=min(512, S), slc_q=512, slc_k=512, moe=min(2048, S), final=min(512, S))


def kernel(x, c, w_ada, b_ada, norm1_g, w_in, hg_lb_logits, hg_norm_g, cmp_pos_k, cmp_w1_k, cmp_b1_k, cmp_w2_k,
           cmp_pos_v, cmp_w1_v, cmp_b1_v, cmp_w2_v, w_out, norm2_g, router_w, router_bias, w_exp_gu, w_exp_dn,
           w_sh_gu, w_sh_dn, final_g):
    B, S, D = x.shape
    assert D == D_MODEL and w_ada.shape[0] == 1 and S % 512 == 0
    tl = _tiles(S)
    l = 0
    lb = jnp.cumsum(jax.nn.softmax(hg_lb_logits.astype(F32), axis=0), axis=0)[l].reshape(1, HG_WIDTH)
    c8 = jnp.pad(c, ((0, 8 - B), (0, 0)))
    mod3 = _ada(c8, w_ada[l], b_ada[l].reshape(1, -1))[:B].reshape(B, 1, 6 * D)
    cos, sin = _rope_tables(S)
    hg, qraw, qrot, kc, vc, ks, kw, vaug, gate = _inproj(x, mod3, norm1_g[l].reshape(1, D), _pack_w_in(w_in[l]),
                                                         cos, sin, tl["inproj"])
    ng = hg_norm_g[l].reshape(1, HG_WIDTH)
    mxu_safe = -HG_SUB * jnp.log(jnp.min(lb)) <= HG_SAFE_LOG_RANGE
    ohg = lax.cond(mxu_safe, lambda: _hgrn_mxu(hg, lb, ng, tl["hgrn"]), lambda: _hgrn(hg, lb, ng))
    ncb = S // CMP_STRIDE
    kcmp = _cmpmlp(kc.reshape(B, ncb, CMP_STRIDE * LANES),
                   *_pack_cmp(cmp_pos_k[l], cmp_w1_k[l], cmp_b1_k[l], cmp_w2_k[l], True))
    vcmp = _cmpmlp(vc.reshape(B, ncb, CMP_STRIDE * LANES),
                   *_pack_cmp(cmp_pos_v[l], cmp_w1_v[l], cmp_b1_v[l], cmp_w2_v[l], False))
    ocmp, sel = _cmpsel(qraw, kcmp, vcmp, gate, tl["cmpsel"])
    onsa = _slcwin(qrot, sel, ks, kw, vaug, gate, ocmp, tl["slc_q"], tl["slc_k"])
    x1, h2, sc, slot_t, w_t, cnt = _outproj(x, ohg, onsa, w_out[l].astype(BF16), mod3,
                                            norm2_g[l].reshape(1, D), _split_bf16(router_w[l].T),
                                            router_bias[l].reshape(N_EXPERTS, 1), ROUTE_TM)
    T = B * S
    tm = tl["moe"]
    ng = N_EXPERTS // MOE_EG
    slot_c = sc.reshape(T, N_EXPERTS)
    load = jnp.max(cnt[:, :, 0].reshape(T // tm, tm // MOE_SUB, N_EXPERTS), axis=1).astype(jnp.int32)
    per_expert = (load + MOE_ROUND - 1) // MOE_ROUND
    rounds = jnp.concatenate([per_expert, jnp.max(per_expert.reshape(T // tm, ng, MOE_EG), axis=2)], axis=1).reshape(-1)
    moe = _moe(rounds, h2.reshape(T, D), slot_t, w_t, slot_c,
               w_exp_gu[l].astype(BF16), w_exp_dn[l].astype(BF16), w_sh_gu[l].astype(BF16), w_sh_dn[l].astype(BF16), tm)
    return _final(x1, moe.reshape(B, S, D), mod3, final_g.reshape(1, D), tl["final"])
```

```python
import functools

import numpy as np
import jax
import jax.numpy as jnp
from jax import lax
from jax.experimental import pallas as pl
from jax.experimental.pallas import tpu as pltpu

F32 = jnp.float32
BF16 = jnp.bfloat16
HIGHEST = lax.Precision.HIGHEST

D_MODEL = 1024
EPS = 1e-6
HG_HEADS = 4
HG_DK = 128
HG_DV = 128
HG_WIDTH = HG_HEADS * HG_DV
HG_CHUNK = 64
HG_SUB = 16
HG_SAFE_LOG_RANGE = 80.0
NSA_HEADS = 8
NSA_KV_GROUPS = 2
NSA_HG = NSA_HEADS // NSA_KV_GROUPS
NSA_HD = 64
NSA_WIDTH = NSA_HEADS * NSA_HD
NSA_KV = NSA_KV_GROUPS * NSA_HD
CMP_BLOCK = 32
CMP_STRIDE = 16
CMP_HIDDEN = 256
SEL_BLOCK = 64
N_SEL = 16
WINDOW = 512
ROPE_THETA = 10000.0
SLC_STACK = 4
WIN_Q = 256
N_EXPERTS = 64
TOP_K = 8
N_GROUPS = 8
TOPK_GROUPS = 4
EXPERT_HIDDEN = 256
ROUTED_SCALE = 2.5
ROUTE_TM = 512
MOE_SUB = 256
MOE_ROUND = 32
MOE_EG = 8

LANES = 128
NEG = -1e30
VMEM_LIMIT = 56 * 1024 * 1024

C_HG = 0
C_Q = 4 * HG_WIDTH
C_K = C_Q + NSA_WIDTH
C_GATE = C_K + 6 * NSA_KV
IN_COLS_P = C_GATE + LANES
LOG2E = 1.4426950408889634

NT = (((1,), (1,)), ((), ()))
TN = (((0,), (0,)), ((), ()))


def _cparams(sem):
    return pltpu.CompilerParams(dimension_semantics=sem, vmem_limit_bytes=VMEM_LIMIT)


def _sigmoid(x):
    return 1.0 / (1.0 + jnp.exp(-x))


def _silu(x):
    return x * _sigmoid(x)


def _gelu_tanh(x):
    return 0.5 * x * (1.0 + jnp.tanh(0.7978845608028654 * (x + 0.044715 * (x * x * x))))


def _ada_kernel(c_ref, w_ref, b_ref, o_ref):
    a = _silu(c_ref[...])
    o_ref[...] = jnp.dot(a, w_ref[...], precision=HIGHEST, preferred_element_type=F32) + b_ref[...]


def _ada(c8, w, b):
    n = w.shape[1]
    tn = 1024
    return pl.pallas_call(
        _ada_kernel,
        out_shape=jax.ShapeDtypeStruct((8, n), F32),
        grid=(n // tn,),
        in_specs=[pl.BlockSpec((8, D_MODEL), lambda j: (0, 0)),
                  pl.BlockSpec((D_MODEL, tn), lambda j: (0, j)),
                  pl.BlockSpec((1, tn), lambda j: (0, j))],
        out_specs=pl.BlockSpec((8, tn), lambda j: (0, j)),
        compiler_params=_cparams(("arbitrary",)),
        name="ada",
    )(c8, w, b)


def _rope(t, cos, sin_signed, first_half):
    rot = jnp.where(first_half, pltpu.roll(t, 96, 1), pltpu.roll(t, 32, 1))
    return t * cos + rot * sin_signed


def _inproj_kernel(x_ref, sh_ref, sc_ref, g_ref, w_ref, cos_ref, sin_ref,
                   hg_ref, qraw_ref, qrot_ref, kc_ref, vc_ref, ks_ref, kw_ref, va_ref, gate_ref, h_scr):
    tm = x_ref.shape[1]
    x = x_ref[0]
    y = x * lax.rsqrt(jnp.mean(x * x, axis=-1, keepdims=True) + EPS) * g_ref[...]
    h_scr[...] = (y * (1.0 + sc_ref[0]) + sh_ref[0]).astype(BF16)

    def mm(lo, width):
        return jnp.dot(h_scr[...], w_ref[:, lo:lo + width], preferred_element_type=F32)

    cos = cos_ref[...]
    sin = sin_ref[...]
    first_half = (lax.broadcasted_iota(jnp.int32, cos.shape, 1) % NSA_HD) < (NSA_HD // 2)
    lane = lax.broadcasted_iota(jnp.int32, (tm, LANES), 1)
    half_of_lane = lane // NSA_HD

    def own_lanes(t, src_half, dst_half):
        moved = t if src_half == dst_half else pltpu.roll(t, NSA_HD, 1)
        return jnp.where(half_of_lane == dst_half, moved, 0.0)

    for j in range(4):
        hg_ref[0, :, j * HG_WIDTH:(j + 1) * HG_WIDTH] = mm(C_HG + j * HG_WIDTH, HG_WIDTH)
    for m in range(NSA_WIDTH // (2 * LANES)):
        qq = mm(C_Q + m * 2 * LANES, 2 * LANES) * (NSA_HD ** -0.5)
        for c in range(2):
            q = qq[:, c * LANES:(c + 1) * LANES]
            qr = _rope(q, cos, sin, first_half) * LOG2E
            for half in range(2):
                n = (2 * m + c) * 2 + half
                g = n // NSA_HG
                qraw_ref[0, :, n * LANES:(n + 1) * LANES] = own_lanes(q, half, g).astype(BF16)
                qrot_ref[0, :, n * LANES:(n + 1) * LANES] = own_lanes(qr, half, g).astype(BF16)
    kv = mm(C_K, 2 * LANES)
    kc_ref[0] = kv[:, :LANES].astype(BF16)
    vc_ref[0] = kv[:, LANES:].astype(BF16)
    pos = pl.program_id(1) * tm + lax.broadcasted_iota(jnp.int32, (tm, LANES), 0)
    ks_ref[0, :, LANES:2 * LANES] = jnp.where(lane == pos // SEL_BLOCK, 1.0, 0.0).astype(BF16)
    for j in range(2):
        kv = mm(C_K + (j + 1) * 2 * LANES, 2 * LANES)
        k_rot = _rope(kv[:, :LANES], cos, sin, first_half).astype(BF16)
        if j == 0:
            ks_ref[0, :, 0:LANES] = k_rot
        else:
            kw_ref[0] = k_rot
        for g in range(NSA_KV_GROUPS):
            v = own_lanes(kv[:, LANES:], g, 0)
            col = (j * NSA_KV_GROUPS + g) * LANES
            va_ref[0, :, col:col + LANES] = jnp.where(lane == NSA_HD, 1.0, v).astype(BF16)
    gate = _sigmoid(mm(C_GATE, LANES))
    per_group = 3 * NSA_HG
    for g in range(NSA_KV_GROUPS):
        gate_ref[0, :, g * LANES:(g + 1) * LANES] = gate if g == 0 else pltpu.roll(gate, LANES - g * per_group, 1)


def _inproj(x, mod3, norm_g, w_p, cos, sin, tm):
    B, S, D = x.shape
    blk = lambda w: pl.BlockSpec((1, tm, w), lambda b, i: (b, i, 0))
    return pl.pallas_call(
        _inproj_kernel,
        out_shape=(jax.ShapeDtypeStruct((B, S, 4 * HG_WIDTH), F32),
                   jax.ShapeDtypeStruct((B, S, NSA_HEADS * LANES), BF16),
                   jax.ShapeDtypeStruct((B, S, NSA_HEADS * LANES), BF16),
                   jax.ShapeDtypeStruct((B, S, LANES), BF16),
                   jax.ShapeDtypeStruct((B, S, LANES), BF16),
                   jax.ShapeDtypeStruct((B, S, 2 * LANES), BF16),
                   jax.ShapeDtypeStruct((B, S, LANES), BF16),
                   jax.ShapeDtypeStruct((B, S, 4 * LANES), BF16),
                   jax.ShapeDtypeStruct((B, S, NSA_KV_GROUPS * LANES), F32)),
        grid=(B, S // tm),
        in_specs=[blk(D),
                  pl.BlockSpec((1, 1, D), lambda b, i: (b, 0, 0)),
                  pl.BlockSpec((1, 1, D), lambda b, i: (b, 0, 1)),
                  pl.BlockSpec((1, D), lambda b, i: (0, 0)),
                  pl.BlockSpec((D, IN_COLS_P), lambda b, i: (0, 0)),
                  pl.BlockSpec((tm, LANES), lambda b, i: (i, 0)),
                  pl.BlockSpec((tm, LANES), lambda b, i: (i, 0))],
        out_specs=(blk(4 * HG_WIDTH), blk(NSA_HEADS * LANES), blk(NSA_HEADS * LANES),
                   blk(LANES), blk(LANES), blk(2 * LANES), blk(LANES), blk(4 * LANES), blk(NSA_KV_GROUPS * LANES)),
        scratch_shapes=[pltpu.VMEM((tm, D), BF16)],
        compiler_params=_cparams(("arbitrary", "arbitrary")),
        name="inproj",
    )(x, mod3, mod3, norm_g, w_p, cos, sin)


def _hgrn_kernel(q_ref, f_ref, i_ref, gt_ref, lb_ref, ng_ref, o_ref):
    S = q_ref.shape[1]
    C, U = HG_CHUNK, HG_SUB
    lb = lb_ref[...]
    ng = ng_ref[...]
    ri = lax.broadcasted_iota(jnp.int32, (C, C), 0)
    ci = lax.broadcasted_iota(jnp.int32, (C, C), 1)
    tril = (ri >= ci).astype(F32)
    trow = lax.broadcasted_iota(jnp.int32, (U, 1), 0)

    def chunk(c, st_t):
        r0 = pl.multiple_of(c * C, C)
        rows = pl.ds(r0, C)
        f = lb + (1.0 - lb) * _sigmoid(f_ref[0, rows, :])
        kk = 1.0 - f
        b = jnp.dot(tril, jnp.log(f), precision=HIGHEST, preferred_element_type=F32)
        q = q_ref[0, rows, :] * (HG_DK ** -0.5)
        v = i_ref[0, rows, :]
        vb = v.astype(BF16)
        o_inter = lax.dot_general((q * jnp.exp(b)).astype(BF16), st_t.astype(BF16), NT,
                                  preferred_element_type=F32)
        parts = []
        for i in range(C // U):
            lo = i * U
            bi = b[lo:lo + U]
            qi = q[lo:lo + U]
            if i == 0:
                oi = jnp.zeros((U, HG_DV), F32)
            else:
                r = b[lo - 1:lo]
                qrel = (qi * jnp.exp(bi - r)).astype(BF16)
                kprev = (kk[:lo] * jnp.exp(r - b[:lo])).astype(BF16)
                a_off = lax.dot_general(qrel, kprev, NT, preferred_element_type=F32)
                oi = jnp.dot(a_off.astype(BF16), vb[:lo], preferred_element_type=F32)
            for s in range(U):
                valid = trow >= s
                e = jnp.exp(jnp.where(valid, bi - bi[s:s + 1], 0.0))
                a = jnp.sum(qi * e * kk[lo + s:lo + s + 1], axis=-1, keepdims=True)
                oi = oi + jnp.where(valid, a, 0.0) * v[lo + s:lo + s + 1]
            parts.append(oi)
        o = o_inter + jnp.concatenate(parts, axis=0)
        o = o * lax.rsqrt(jnp.mean(o * o, axis=-1, keepdims=True) + EPS) * ng
        o_ref[0, rows, :] = (o * _silu(gt_ref[0, rows, :])).astype(BF16)
        bl = b[C - 1:C]
        kv_t = lax.dot_general(vb, (kk * jnp.exp(bl - b)).astype(BF16), TN, preferred_element_type=F32)
        return jnp.exp(bl) * st_t + kv_t

    lax.fori_loop(0, S // C, chunk, jnp.zeros((HG_DV, HG_DK), F32))


def _cumsum_rows(x):
    n = x.shape[0]
    row = lax.broadcasted_iota(jnp.int32, x.shape, 0)
    d = 1
    while d < n:
        x = x + jnp.where(row >= d, pltpu.roll(x, d, 0), 0.0)
        d *= 2
    return x


def _hgrn_mxu_kernel(hg_ref, lb_ref, ng_ref, o_ref, st_scr):
    ts = hg_ref.shape[1]
    C, U = HG_CHUNK, HG_SUB

    @pl.when(pl.program_id(1) == 0)
    def _():
        st_scr[...] = jnp.zeros(st_scr.shape, F32)

    W = HG_WIDTH
    NH = HG_HEADS
    head_of_lane = lax.broadcasted_iota(jnp.int32, (1, W), 1) // HG_DK
    hcols = [slice(h * HG_DK, (h + 1) * HG_DK) for h in range(NH)]

    def chunk(c, carry):
        rows = pl.ds(pl.multiple_of(c * C, C), C)
        lb = lb_ref[...]
        f = lb + (1.0 - lb) * _sigmoid(hg_ref[0, rows, W:2 * W])
        kk = 1.0 - f
        b = _cumsum_rows(jnp.log(f))
        q = hg_ref[0, rows, 0:W] * (HG_DK ** -0.5)
        vb = hg_ref[0, rows, 2 * W:3 * W].astype(BF16)
        qe = (q * jnp.exp(b)).astype(BF16)
        o_inter = jnp.concatenate(
            [lax.dot_general(qe[:, hc], st_scr[h].astype(BF16), NT, preferred_element_type=F32)
             for h, hc in enumerate(hcols)], axis=1)
        parts = []
        for i in range(C // U):
            lo, hi = i * U, (i + 1) * U
            r = b[lo - 1:lo] if i else jnp.zeros((1, W), F32)
            qrel = q[lo:hi] * jnp.exp(b[lo:hi] - r)
            kall = (kk[:hi] * jnp.exp(r - b[:hi])).astype(BF16)
            qbd = jnp.concatenate([jnp.where(head_of_lane == h, qrel, 0.0) for h in range(NH)], axis=0)
            a = lax.dot_general(qbd.astype(BF16), kall, NT, preferred_element_type=F32)
            trow = lax.broadcasted_iota(jnp.int32, (NH * U, hi), 0) % U
            a = jnp.where(lax.broadcasted_iota(jnp.int32, (NH * U, hi), 1) <= lo + trow, a, 0.0)
            oa = jnp.dot(a.astype(BF16), vb[:hi], preferred_element_type=F32)
            oi = jnp.where(head_of_lane == 0, oa[0:U], 0.0)
            for h in range(1, NH):
                oi = jnp.where(head_of_lane == h, oa[h * U:(h + 1) * U], oi)
            parts.append(oi)
        o = o_inter + jnp.concatenate(parts, axis=0)
        o = jnp.concatenate(
            [o[:, hc] * lax.rsqrt(jnp.mean(o[:, hc] * o[:, hc], axis=-1, keepdims=True) + EPS) for hc in hcols], axis=1)
        o_ref[0, rows, :] = (o * ng_ref[...] * _silu(hg_ref[0, rows, 3 * W:4 * W])).astype(BF16)
        bl = b[C - 1:C]
        ke = (kk * jnp.exp(bl - b)).astype(BF16)
        decay = jnp.exp(bl)
        for h, hc in enumerate(hcols):
            kv_t = lax.dot_general(vb[:, hc], ke[:, hc], TN, preferred_element_type=F32)
            st_scr[h] = decay[:, hc] * st_scr[h] + kv_t
        return carry

    lax.fori_loop(0, ts // C, chunk, 0, unroll=8)


def _hgrn_mxu(hg, lb, ng, ts):
    B, S, _ = hg.shape
    vec = pl.BlockSpec((1, HG_WIDTH), lambda b, i: (0, 0))
    return pl.pallas_call(
        _hgrn_mxu_kernel,
        out_shape=jax.ShapeDtypeStruct((B, S, HG_WIDTH), BF16),
        grid=(B, S // ts),
        in_specs=[pl.BlockSpec((1, ts, 4 * HG_WIDTH), lambda b, i: (b, i, 0)), vec, vec],
        out_specs=pl.BlockSpec((1, ts, HG_WIDTH), lambda b, i: (b, i, 0)),
        scratch_shapes=[pltpu.VMEM((HG_HEADS, HG_DV, HG_DK), F32)],
        compiler_params=_cparams(("arbitrary", "arbitrary")),
        name="hgrn_mxu",
    )(hg, lb, ng)


def _hgrn(hg, lb, ng):
    B, S, _ = hg.shape
    col = lambda k: pl.BlockSpec((1, S, HG_DK), lambda b, h, k=k: (b, 0, k * HG_HEADS + h))
    vec = pl.BlockSpec((1, HG_DK), lambda b, h: (0, h))
    return pl.pallas_call(
        _hgrn_kernel,
        out_shape=jax.ShapeDtypeStruct((B, S, HG_WIDTH), BF16),
        grid=(B, HG_HEADS),
        in_specs=[col(0), col(1), col(2), col(3), vec, vec],
        out_specs=pl.BlockSpec((1, S, HG_DV), lambda b, h: (b, 0, h)),
        compiler_params=_cparams(("arbitrary", "arbitrary")),
        name="hgrn",
    )(hg, hg, hg, hg, lb, ng)


def _cmpmlp_kernel(x_ref, w1a_ref, w1b_ref, pe_ref, w1_ref, b1_ref, w2_ref, o_ref):
    x = x_ref[0]
    hb = jnp.dot(pe_ref[...], w1_ref[...], precision=HIGHEST, preferred_element_type=F32)[0:1] + b1_ref[...]
    nrow = x.shape[0]
    for g in range(NSA_KV_GROUPS):
        a = jnp.dot(x, w1a_ref[g], preferred_element_type=F32)
        bm = jnp.dot(x, w1b_ref[g], preferred_element_type=F32)
        hdn = a + pltpu.roll(bm, nrow - 1, 0) + hb
        o_ref[0, g] = jnp.dot(_gelu_tanh(hdn).astype(BF16), w2_ref[g], preferred_element_type=F32).astype(BF16)


def _cmpmlp(x2, w1a, w1b, pe8, w1, b1, w2p):
    B, ncb, width = x2.shape
    full = lambda a: pl.BlockSpec(a.shape, lambda b: (0,) * a.ndim)
    return pl.pallas_call(
        _cmpmlp_kernel,
        out_shape=jax.ShapeDtypeStruct((B, NSA_KV_GROUPS, ncb, LANES), BF16),
        grid=(B,),
        in_specs=[pl.BlockSpec((1, ncb, width), lambda b: (b, 0, 0)),
                  full(w1a), full(w1b), full(pe8), full(w1), full(b1), full(w2p)],
        out_specs=pl.BlockSpec((1, NSA_KV_GROUPS, ncb, LANES), lambda b: (b, 0, 0, 0)),
        compiler_params=_cparams(("arbitrary",)),
        name="cmpmlp",
    )(x2, w1a, w1b, pe8, w1, b1, w2p)


def _rank_before(score, nrows):
    jrow = lax.broadcasted_iota(jnp.int32, score.shape, 0)
    rank = jnp.zeros(score.shape, F32)
    for k in range(nrows):
        rk = score[k:k + 1]
        beats = (rk > score) | ((rk == score) & (jrow > k))
        rank = rank + jnp.where(beats, 1.0, 0.0)
    return rank


def _topk_rows(score, k):
    n = score.shape[0]
    row = lax.broadcasted_iota(jnp.int32, score.shape, 0).astype(F32)
    keep = jnp.zeros(score.shape, F32)
    for _ in range(k):
        top = jnp.max(score, axis=0, keepdims=True)
        first = jnp.min(jnp.where(score == top, row, float(n)), axis=0, keepdims=True)
        pick = row == first
        keep = jnp.where(pick, 1.0, keep)
        score = jnp.where(pick, -jnp.inf, score)
    return keep


def _transpose_small_ints(xt):
    c = xt.shape[1]
    eye = (lax.broadcasted_iota(jnp.int32, (c, c), 0) == lax.broadcasted_iota(jnp.int32, (c, c), 1)).astype(BF16)
    return lax.dot_general(eye, xt.astype(BF16), NT, preferred_element_type=F32)


def _pack_heads(heads):
    low = lax.broadcasted_iota(jnp.int32, heads[0].shape, 1) < NSA_HD
    return jnp.concatenate([jnp.where(low, heads[k], pltpu.roll(heads[k + 1], NSA_HD, 1))
                            for k in range(0, len(heads), 2)], axis=1)


def _cmpsel_kernel(q_ref, kc_ref, vc_ref, gate_ref, o_ref, sel_ref):
    tq = q_ref.shape[1]
    ncb = kc_ref.shape[2]
    t = pl.program_id(2)
    kc = kc_ref[0, 0]
    vc = vc_ref[0, 0]
    pos = t * tq + lax.broadcasted_iota(jnp.int32, (tq, 1), 0)
    cblk = lax.broadcasted_iota(jnp.int32, (1, ncb), 1)
    vis = (cblk * CMP_STRIDE + CMP_BLOCK - 1) <= pos
    psum = jnp.zeros((tq, ncb), F32)
    gate = gate_ref[0]
    heads = []
    for h in range(NSA_HG):
        s = lax.dot_general(q_ref[0, :, h * LANES:(h + 1) * LANES], kc, NT, preferred_element_type=F32)
        s = jnp.where(vis, s, NEG)
        p = jnp.exp(s - jnp.max(s, axis=-1, keepdims=True))
        p = jnp.where(vis, p / jnp.sum(p, axis=-1, keepdims=True), 0.0)
        heads.append(jnp.dot(p.astype(BF16), vc, preferred_element_type=F32) * gate[:, 3 * h:3 * h + 1])
        psum = psum + p
    o_ref[0] = _pack_heads(heads)
    nsb = ncb * CMP_STRIDE // SEL_BLOCK
    jb = lax.broadcasted_iota(jnp.int32, (nsb, ncb), 0) * SEL_BLOCK
    cb = lax.broadcasted_iota(jnp.int32, (nsb, ncb), 1) * CMP_STRIDE
    ov = jnp.maximum(jnp.minimum(cb + CMP_BLOCK, jb + SEL_BLOCK) - jnp.maximum(cb, jb), 0).astype(F32) / CMP_BLOCK
    ps_hi = psum.astype(BF16)
    ps_lo = (psum - ps_hi.astype(F32)).astype(BF16)
    ovb = ov.astype(BF16)
    pslc_t = (lax.dot_general(ovb, ps_hi, NT, preferred_element_type=F32)
              + lax.dot_general(ovb, ps_lo, NT, preferred_element_type=F32))
    posl = t * tq + lax.broadcasted_iota(jnp.int32, (1, tq), 1)
    cur = posl // SEL_BLOCK
    jrow = lax.broadcasted_iota(jnp.int32, (nsb, tq), 0)
    forced = (jrow == 0) | (jrow == cur) | (jrow == cur - 1)
    score = jnp.where(forced, 1e30, jnp.where(jrow <= cur, pslc_t, NEG))
    chosen = _topk_rows(score, min(N_SEL, nsb)) > 0.5
    drop = jnp.where(chosen & (score > -1e29), 0.0, 1.0)
    drop = jnp.concatenate([drop, jnp.zeros((LANES - nsb, tq), F32)], axis=0)
    sel_ref[0, 0] = (_transpose_small_ints(drop) * NEG).astype(BF16)


def _cmpsel(qraw, kcmp, vcmp, gate, tq):
    B, S, _ = qraw.shape
    ncb = kcmp.shape[2]
    gw = NSA_HG * LANES
    assert S // SEL_BLOCK <= NSA_HD
    cmp_spec = pl.BlockSpec((1, 1, ncb, LANES), lambda b, g, t: (b, g, 0, 0))
    return pl.pallas_call(
        _cmpsel_kernel,
        out_shape=(jax.ShapeDtypeStruct((B, S, NSA_WIDTH), F32),
                   jax.ShapeDtypeStruct((B, NSA_KV_GROUPS, S, LANES), BF16)),
        grid=(B, NSA_KV_GROUPS, S // tq),
        in_specs=[pl.BlockSpec((1, tq, gw), lambda b, g, t: (b, t, g)), cmp_spec, cmp_spec,
                  pl.BlockSpec((1, tq, LANES), lambda b, g, t: (b, t, g))],
        out_specs=(pl.BlockSpec((1, tq, NSA_WIDTH // NSA_KV_GROUPS), lambda b, g, t: (b, t, g)),
                   pl.BlockSpec((1, 1, tq, LANES), lambda b, g, t: (b, g, t, 0))),
        compiler_params=_cparams(("arbitrary", "arbitrary", "arbitrary")),
        name="cmpsel",
    )(qraw, kcmp, vcmp, gate)


def _slcwin_kernel(q_ref, sel_ref, ks_ref, kw_ref, vs_ref, vw_ref, gate_ref, ocmp_ref, o_ref, *, tk, wk):
    tq = q_ref.shape[1]
    S = ks_ref.shape[1]
    t = pl.program_id(2)
    q0 = t * tq
    qpos = q0 + lax.broadcasted_iota(jnp.int32, (tq, 1), 0)
    qw = [q_ref[0, :, h * LANES:(h + 1) * LANES] for h in range(NSA_HG)]
    sel = sel_ref[0, 0]
    qa = [jnp.concatenate([q, sel], axis=1) for q in qw]

    gate = gate_ref[0]

    def normalise(acc, g):
        return acc * (g / acc[:, NSA_HD:NSA_HD + 1])

    nchain = NSA_HG // SLC_STACK
    qst = [jnp.concatenate(qa[c * SLC_STACK:(c + 1) * SLC_STACK], axis=0) for c in range(nchain)]
    qpos_st = jnp.concatenate([qpos] * SLC_STACK, axis=0)

    def kstep(j, carry, diag):
        k0 = pl.multiple_of(j * tk, tk)
        ks = ks_ref[0, pl.ds(k0, tk), :]
        vs = vs_ref[0, pl.ds(k0, tk), :]
        if diag:
            vis = (k0 + lax.broadcasted_iota(jnp.int32, (1, tk), 1)) <= qpos_st
        out = []
        for c in range(nchain):
            m, acc = carry[c]
            s = lax.dot_general(qst[c], ks, NT, preferred_element_type=F32)
            if diag:
                s = jnp.where(vis, s, NEG)
            mn = jnp.maximum(m, jnp.max(s, axis=-1, keepdims=True))
            p = jnp.exp2(s - mn)
            acc = jnp.exp2(m - mn) * acc + jnp.dot(p.astype(BF16), vs, preferred_element_type=F32)
            out.append((mn, acc))
        return tuple(out)

    init = tuple((jnp.full((SLC_STACK * tq, 1), NEG, F32), jnp.zeros((SLC_STACK * tq, LANES), F32))
                 for _ in range(nchain))
    jdiag = q0 // tk
    carry = lax.fori_loop(0, jdiag, lambda j, c: kstep(j, c, False), init)
    carry = kstep(jdiag, carry, True)
    heads = [normalise(carry[h // SLC_STACK][1][(h % SLC_STACK) * tq:(h % SLC_STACK + 1) * tq],
                       gate[:, 3 * h + 1:3 * h + 2]) for h in range(NSA_HG)]

    wq = wk - WINDOW
    wins = [[] for _ in range(NSA_HG)]
    spos = lax.broadcasted_iota(jnp.int32, (NSA_HG * wq, 1), 0) % wq
    for i in range(tq // wq):
        rows = slice(i * wq, (i + 1) * wq)
        start = pl.multiple_of(jnp.maximum(q0 + (i + 1) * wq - wk, 0), wq)
        kw = kw_ref[0, pl.ds(start, wk), :]
        vw = vw_ref[0, pl.ds(start, wk), :]
        d = (q0 + i * wq + spos) - (start + lax.broadcasted_iota(jnp.int32, (1, wk), 1))
        q_st = jnp.concatenate([q[rows] for q in qw], axis=0)
        s = jnp.where((d >= 0) & (d < WINDOW), lax.dot_general(q_st, kw, NT, preferred_element_type=F32), NEG)
        p = jnp.exp2(s - jnp.max(s, axis=-1, keepdims=True))
        o_st = jnp.dot(p.astype(BF16), vw, preferred_element_type=F32)
        for h in range(NSA_HG):
            wins[h].append(o_st[h * wq:(h + 1) * wq])
    for h in range(NSA_HG):
        heads[h] = heads[h] + normalise(jnp.concatenate(wins[h], axis=0), gate[:, 3 * h + 2:3 * h + 3])
    o_ref[0] = (ocmp_ref[0] + _pack_heads(heads)).astype(BF16)


def _slcwin(qrot, sel, ks, kw, vaug, gate, ocmp, tq, tk):
    B, S, _ = qrot.shape
    gw = NSA_HG * LANES
    wk = WINDOW + WIN_Q
    assert S >= wk and tq % WIN_Q == 0
    ospec = pl.BlockSpec((1, tq, NSA_WIDTH // NSA_KV_GROUPS), lambda b, g, t: (b, t, g))
    return pl.pallas_call(
        functools.partial(_slcwin_kernel, tk=tk, wk=wk),
        out_shape=jax.ShapeDtypeStruct((B, S, NSA_WIDTH), BF16),
        grid=(B, NSA_KV_GROUPS, S // tq),
        in_specs=[pl.BlockSpec((1, tq, gw), lambda b, g, t: (b, t, g)),
                  pl.BlockSpec((1, 1, tq, LANES), lambda b, g, t: (b, g, t, 0)),
                  pl.BlockSpec((1, S, 2 * LANES), lambda b, g, t: (b, 0, 0)),
                  pl.BlockSpec((1, S, LANES), lambda b, g, t: (b, 0, 0)),
                  pl.BlockSpec((1, S, LANES), lambda b, g, t: (b, 0, g)),
                  pl.BlockSpec((1, S, LANES), lambda b, g, t: (b, 0, NSA_KV_GROUPS + g)),
                  pl.BlockSpec((1, tq, LANES), lambda b, g, t: (b, t, g)),
                  ospec],
        out_specs=ospec,
        compiler_params=_cparams(("arbitrary", "arbitrary", "arbitrary")),
        name="slcwin",
    )(qrot, sel, ks, kw, vaug, vaug, gate, ocmp)


def _outproj_kernel(x_ref, ohg_ref, onsa_ref, wout_ref,
                    g1_ref, sh2_ref, sc2_ref, n2_ref, rwt_ref, rb_ref,
                    x1_ref, h2_ref, sc_ref, slot_t_ref, w_t_ref, cnt_ref):
    tm = x_ref.shape[1]
    mix = jnp.dot(jnp.concatenate([ohg_ref[0], onsa_ref[0]], axis=1), wout_ref[...], preferred_element_type=F32)
    x1 = x_ref[0] + g1_ref[0] * mix
    x1_ref[0] = x1
    h2 = (x1 * lax.rsqrt(jnp.mean(x1 * x1, axis=-1, keepdims=True) + EPS) * n2_ref[...]) * (1.0 + sc2_ref[0]) + sh2_ref[0]
    h_hi = h2.astype(BF16)
    h2_ref[0] = h_hi
    h_lo = (h2 - h_hi.astype(F32)).astype(BF16)
    both = lax.dot_general(rwt_ref[...], h_hi, NT, preferred_element_type=F32)
    logits = (both[:N_EXPERTS] + both[N_EXPERTS:]
              + lax.dot_general(rwt_ref[:N_EXPERTS, :], h_lo, NT, preferred_element_type=F32))
    scores = _sigmoid(logits)
    choice = scores + rb_ref[...]
    per = N_EXPERTS // N_GROUPS
    c3 = choice.reshape(N_GROUPS, per, tm)
    erow = lax.broadcasted_iota(jnp.int32, c3.shape, 1)
    rank_in = jnp.zeros(c3.shape, F32)
    for k in range(per):
        ck = c3[:, k:k + 1, :]
        rank_in = rank_in + jnp.where((ck > c3) | ((ck == c3) & (erow > k)), 1.0, 0.0)
    grp_score = jnp.sum(jnp.where(rank_in < 2, c3, 0.0), axis=1)
    grp_keep = _rank_before(grp_score, N_GROUPS) < TOPK_GROUPS
    masked = jnp.where(grp_keep[:, None, :], c3, -jnp.inf).reshape(N_EXPERTS, tm)
    keep = _topk_rows(masked, TOP_K) > 0.5
    tw = jnp.where(keep, scores, 0.0)
    tw = tw / jnp.sum(tw, axis=0, keepdims=True) * ROUTED_SCALE
    before = lax.broadcasted_iota(jnp.int32, (tm, tm), 0)
    after = lax.broadcasted_iota(jnp.int32, (tm, tm), 1)
    earlier = (before < after) & (before // MOE_SUB == after // MOE_SUB)
    kept = jnp.where(keep, 1.0, 0.0)
    pref = jnp.dot(kept.astype(BF16), earlier.astype(BF16), preferred_element_type=F32)
    slot = jnp.where(keep, pref, -1.0)
    assert MOE_SUB <= 256
    sc_ref[0] = _transpose_small_ints(slot)
    for j in range(tm // MOE_SUB):
        sub = slice(j * MOE_SUB, (j + 1) * MOE_SUB)
        slot_t_ref[:, j] = slot[:, sub].reshape(N_EXPERTS // MOE_EG, MOE_EG, MOE_SUB)
        w_t_ref[:, j] = tw[:, sub].reshape(N_EXPERTS // MOE_EG, MOE_EG, MOE_SUB)
        cnt_ref[j] = jnp.broadcast_to(jnp.sum(kept[:, sub], axis=1, keepdims=True), (N_EXPERTS, LANES))


def _outproj(x, ohg, onsa, wout, mod3, n2, rwt, rb, tm):
    B, S, D = x.shape
    blk = lambda w: pl.BlockSpec((1, tm, w), lambda b, i: (b, i, 0))
    full = lambda a: pl.BlockSpec(a.shape, lambda b, i: (0,) * a.ndim)
    modc = lambda k: pl.BlockSpec((1, 1, D), lambda b, i, k=k: (b, 0, k))
    nt = S // tm
    ng = N_EXPERTS // MOE_EG
    nh = tm // MOE_SUB
    tspec = pl.BlockSpec((ng, nh, MOE_EG, MOE_SUB), lambda b, i: (0, b * nt + i, 0, 0))
    return pl.pallas_call(
        _outproj_kernel,
        out_shape=(jax.ShapeDtypeStruct((B, S, D), F32),
                   jax.ShapeDtypeStruct((B, S, D), BF16),
                   jax.ShapeDtypeStruct((B, S, N_EXPERTS), F32),
                   jax.ShapeDtypeStruct((ng, B * nt * nh, MOE_EG, MOE_SUB), F32),
                   jax.ShapeDtypeStruct((ng, B * nt * nh, MOE_EG, MOE_SUB), F32),
                   jax.ShapeDtypeStruct((B * nt * nh, N_EXPERTS, LANES), F32)),
        grid=(B, nt),
        in_specs=[blk(D), blk(HG_WIDTH), blk(NSA_WIDTH), full(wout),
                  modc(2), modc(3), modc(4), full(n2), full(rwt), full(rb)],
        out_specs=(blk(D), blk(D), blk(N_EXPERTS), tspec, tspec,
                   pl.BlockSpec((nh, N_EXPERTS, LANES), lambda b, i: (b * nt + i, 0, 0))),
        compiler_params=_cparams(("arbitrary", "arbitrary")),
        name="outproj",
    )(x, ohg, onsa, wout, mod3, mod3, mod3, n2, rwt, rb)


def _swiglu_hidden(x, wgu):
    gu = jnp.dot(x, wgu, preferred_element_type=F32)
    return _silu(gu[:, :EXPERT_HIDDEN]) * gu[:, EXPERT_HIDDEN:]


def _moe_kernel(rounds_ref, h_ref, slot_t_ref, w_t_ref, slot_c_ref, wgu_ref, wdn_ref, sgu_ref, sdn_ref,
                o_ref, x_scr, y_scr, p_scr):
    i = pl.program_id(0)
    g = pl.program_id(1)
    tm = h_ref.shape[0]
    nsub = tm // MOE_SUB
    rnd, eg = MOE_ROUND, MOE_EG
    ng = N_EXPERTS // eg
    base = i * (N_EXPERTS + ng)

    @pl.when(g == 0)
    def _():
        act = _swiglu_hidden(h_ref[...], sgu_ref[...]).astype(BF16)
        o_ref[...] = jnp.dot(act, sdn_ref[...], preferred_element_type=F32)

    rslot = lax.broadcasted_iota(jnp.int32, (rnd, 1), 0).astype(F32)
    lane_slot = (lax.broadcasted_iota(jnp.int32, (1, eg * rnd), 1) % rnd).astype(F32)
    spread = (lax.broadcasted_iota(jnp.int32, (N_EXPERTS, eg * rnd), 1) // rnd + g * eg
              == lax.broadcasted_iota(jnp.int32, (N_EXPERTS, eg * rnd), 0)).astype(BF16)

    def one_round(r, carry):
        first = (r * rnd).astype(F32)
        for s in range(nsub):
            st = slot_t_ref[0, s] - first
            p = jnp.concatenate([jnp.where(rslot == st[e:e + 1, :], 1.0, 0.0) for e in range(eg)], axis=0)
            p_scr[s] = p.astype(BF16)
            x_scr[s] = jnp.dot(p_scr[s], h_ref[s * MOE_SUB:(s + 1) * MOE_SUB, :],
                               preferred_element_type=F32).astype(BF16)
        for e in range(eg):
            rows = slice(e * rnd, (e + 1) * rnd)

            @pl.when(r < rounds_ref[base + g * eg + e])
            def _():
                xe = jnp.concatenate([x_scr[s, rows, :] for s in range(nsub)], axis=0)
                wslot = jnp.concatenate(
                    [jnp.sum(p_scr[s, rows, :].astype(F32) * w_t_ref[0, s][e:e + 1, :], axis=-1, keepdims=True)
                     for s in range(nsub)], axis=0)
                y = jnp.dot((_swiglu_hidden(xe, wgu_ref[e]) * wslot).astype(BF16), wdn_ref[e],
                            preferred_element_type=F32)
                for s in range(nsub):
                    y_scr[s, rows, :] = y[s * rnd:(s + 1) * rnd].astype(BF16)

            @pl.when(r >= rounds_ref[base + g * eg + e])
            def _():
                for s in range(nsub):
                    y_scr[s, rows, :] = jnp.zeros((rnd, D_MODEL), BF16)
        for s in range(nsub):
            toks = slice(s * MOE_SUB, (s + 1) * MOE_SUB)
            sc = jnp.dot(slot_c_ref[toks, :].astype(BF16), spread, preferred_element_type=F32) - first
            pt = jnp.where(sc == lane_slot, 1.0, 0.0).astype(BF16)
            o_ref[toks, :] += jnp.dot(pt, y_scr[s], preferred_element_type=F32)
        return carry

    lax.fori_loop(0, rounds_ref[base + N_EXPERTS + g], one_round, 0)


def _moe(rounds, h2, slot_t, w_t, slot_c, wgu, wdn, sgu, sdn, tm):
    T, D = h2.shape
    eg, nsub = MOE_EG, tm // MOE_SUB
    full = lambda a: pl.BlockSpec(a.shape, lambda i, e, o: (0,) * a.ndim)
    tspec = pl.BlockSpec((1, nsub, eg, MOE_SUB), lambda i, e, o: (e, i, 0, 0))
    return pl.pallas_call(
        _moe_kernel,
        out_shape=jax.ShapeDtypeStruct((T, D), F32),
        grid_spec=pltpu.PrefetchScalarGridSpec(
            num_scalar_prefetch=1,
            grid=(T // tm, N_EXPERTS // eg),
            in_specs=[pl.BlockSpec((tm, D), lambda i, e, o: (i, 0), pipeline_mode=pl.Buffered(1)), tspec, tspec,
                      pl.BlockSpec((tm, N_EXPERTS), lambda i, e, o: (i, 0)),
                      pl.BlockSpec((eg, D, 2 * EXPERT_HIDDEN), lambda i, e, o: (e, 0, 0)),
                      pl.BlockSpec((eg, EXPERT_HIDDEN, D), lambda i, e, o: (e, 0, 0)),
                      full(sgu), full(sdn)],
            out_specs=pl.BlockSpec((tm, D), lambda i, e, o: (i, 0), pipeline_mode=pl.Buffered(1)),
            scratch_shapes=[pltpu.VMEM((nsub, eg * MOE_ROUND, D), BF16),
                            pltpu.VMEM((nsub, eg * MOE_ROUND, D), BF16),
                            pltpu.VMEM((nsub, eg * MOE_ROUND, MOE_SUB), BF16)]),
        compiler_params=_cparams(("arbitrary", "arbitrary")),
        name="moe",
    )(rounds, h2, slot_t, w_t, slot_c, wgu, wdn, sgu, sdn)


def _final_kernel(x1_ref, moe_ref, g2_ref, fg_ref, o_ref):
    x2 = x1_ref[0] + g2_ref[0] * moe_ref[0]
    o_ref[0] = x2 * lax.rsqrt(jnp.mean(x2 * x2, axis=-1, keepdims=True) + EPS) * fg_ref[...]


def _final(x1, moe, mod3, fg, tm):
    B, S, D = x1.shape
    blk = pl.BlockSpec((1, tm, D), lambda b, i: (b, i, 0))
    return pl.pallas_call(
        _final_kernel,
        out_shape=jax.ShapeDtypeStruct((B, S, D), F32),
        grid=(B, S // tm),
        in_specs=[blk, blk, pl.BlockSpec((1, 1, D), lambda b, i: (b, 0, 5)), pl.BlockSpec((1, D), lambda b, i: (0, 0))],
        out_specs=blk,
        compiler_params=_cparams(("arbitrary", "arbitrary")),
        name="final",
    )(x1, moe, mod3, fg)


def _split_bf16(w):
    hi = w.astype(BF16)
    return jnp.concatenate([hi, (w - hi.astype(F32)).astype(BF16)], axis=0)


def _pack_w_in(w_in):
    return jnp.pad(w_in, ((0, 0), (0, IN_COLS_P - w_in.shape[1]))).astype(BF16)


def _pack_cmp(pos, w1, b1, w2, lane_by_group):
    half = CMP_STRIDE * NSA_HD
    def rows_for(wh):
        w3 = wh.reshape(CMP_STRIDE, NSA_HD, CMP_HIDDEN)
        z = jnp.zeros_like(w3)
        return jnp.stack([jnp.concatenate([w3, z], axis=1).reshape(CMP_STRIDE * LANES, CMP_HIDDEN),
                          jnp.concatenate([z, w3], axis=1).reshape(CMP_STRIDE * LANES, CMP_HIDDEN)])
    w1a = rows_for(w1[:half]).astype(BF16)
    w1b = rows_for(w1[half:]).astype(BF16)
    z2 = jnp.zeros_like(w2)
    w2_first = jnp.concatenate([w2, z2], axis=1)
    w2p = jnp.stack([w2_first, jnp.concatenate([z2, w2], axis=1) if lane_by_group else w2_first]).astype(BF16)
    pe8 = jnp.pad(pos.reshape(1, CMP_BLOCK * NSA_HD), ((0, 7), (0, 0)))
    return w1a, w1b, pe8, w1, b1.reshape(1, CMP_HIDDEN), w2p


def _rope_tables(S):
    half = NSA_HD // 2
    inv = ROPE_THETA ** (-jnp.arange(half, dtype=F32) / half)
    ang = jnp.arange(S, dtype=F32)[:, None] * inv[None, :]
    cos, sin = jnp.cos(ang), jnp.sin(ang)
    reps = LANES // NSA_HD
    return jnp.tile(jnp.concatenate([cos, cos], axis=1), (1, reps)), jnp.tile(jnp.concatenate([-sin, sin], axis=1), (1, reps))


def _tiles(S):
    return dict(inproj=min(512, S), hgrn=min(512, S), cmpsel=min(512, S), slc_q=512, slc_k=512, moe=min(2048, S), final=min(512, S))


def kernel(x, c, w_ada, b_ada, norm1_g, w_in, hg_lb_logits, hg_norm_g, cmp_pos_k, cmp_w1_k, cmp_b1_k, cmp_w2_k,
           cmp_pos_v, cmp_w1_v, cmp_b1_v, cmp_w2_v, w_out, norm2_g, router_w, router_bias, w_exp_gu, w_exp_dn,
           w_sh_gu, w_sh_dn, final_g):
    B, S, D = x.shape
    assert D == D_MODEL and w_ada.shape[0] == 1 and S % 512 == 0
    tl = _tiles(S)
    l = 0
    lb = jnp.cumsum(jax.nn.softmax(hg_lb_logits.astype(F32), axis=0), axis=0)[l].reshape(1, HG_WIDTH)
    c8 = jnp.pad(c, ((0, 8 - B), (0, 0)))
    mod3 = _ada(c8, w_ada[l], b_ada[l].reshape(1, -1))[:B].reshape(B, 1, 6 * D)
    cos, sin = _rope_tables(S)
    hg, qraw, qrot, kc, vc, ks, kw, vaug, gate = _inproj(x, mod3, norm1_g[l].reshape(1, D), _pack_w_in(w_in[l]),
                                                         cos, sin, tl["inproj"])
    ng = hg_norm_g[l].reshape(1, HG_WIDTH)
    mxu_safe = -HG_SUB * jnp.log(jnp.min(lb)) <= HG_SAFE_LOG_RANGE
    ohg = lax.cond(mxu_safe, lambda: _hgrn_mxu(hg, lb, ng, tl["hgrn"]), lambda: _hgrn(hg, lb, ng))
    ncb = S // CMP_STRIDE
    kcmp = _cmpmlp(kc.reshape(B, ncb, CMP_STRIDE * LANES),
                   *_pack_cmp(cmp_pos_k[l], cmp_w1_k[l], cmp_b1_k[l], cmp_w2_k[l], True))
    vcmp = _cmpmlp(vc.reshape(B, ncb, CMP_STRIDE * LANES),
                   *_pack_cmp(cmp_pos_v[l], cmp_w1_v[l], cmp_b1_v[l], cmp_w2_v[l], False))
    ocmp, sel = _cmpsel(qraw, kcmp, vcmp, gate, tl["cmpsel"])
    onsa = _slcwin(qrot, sel, ks, kw, vaug, gate, ocmp, tl["slc_q"], tl["slc_k"])
    x1, h2, sc, slot_t, w_t, cnt = _outproj(x, ohg, onsa, w_out[l].astype(BF16), mod3,
                                            norm2_g[l].reshape(1, D), _split_bf16(router_w[l].T),
                                            router_bias[l].reshape(N_EXPERTS, 1), ROUTE_TM)
    T = B * S
    tm = tl["moe"]
    ng = N_EXPERTS // MOE_EG
    slot_c = sc.reshape(T, N_EXPERTS)
    load = jnp.max(cnt[:, :, 0].reshape(T // tm, tm // MOE_SUB, N_EXPERTS), axis=1).astype(jnp.int32)
    per_expert = (load + MOE_ROUND - 1) // MOE_ROUND
    rounds = jnp.concatenate([per_expert, jnp.max(per_expert.reshape(T // tm, ng, MOE_EG), axis=2)], axis=1).reshape(-1)
    moe = _moe(rounds, h2.reshape(T, D), slot_t, w_t, slot_c,
               w_exp_gu[l].astype(BF16), w_exp_dn[l].astype(BF16), w_sh_gu[l].astype(BF16), w_sh_dn[l].astype(BF16), tm)
    return _final(x1, moe.reshape(B, S, D), mod3, final_g.reshape(1, D), tl["final"])
```

```python
import functools

import jax
import jax.numpy as jnp
from jax import lax
from jax.experimental import pallas as pl
from jax.experimental.pallas import tpu as pltpu

F32 = jnp.float32
BF16 = jnp.bfloat16
HIGHEST = lax.Precision.HIGHEST

D_MODEL = 1024
EPS = 1e-6
HG_HEADS = 4
HG_DK = 128
HG_DV = 128
HG_WIDTH = HG_HEADS * HG_DV
HG_CHUNK = 64
HG_SUB = 16
HG_SAFE_LOG_RANGE = 80.0
NSA_HEADS = 8
NSA_KV_GROUPS = 2
NSA_HG = NSA_HEADS // NSA_KV_GROUPS
NSA_HD = 64
NSA_WIDTH = NSA_HEADS * NSA_HD
NSA_KV = NSA_KV_GROUPS * NSA_HD
CMP_BLOCK = 32
CMP_STRIDE = 16
CMP_HIDDEN = 256
SEL_BLOCK = 64
N_SEL = 16
WINDOW = 512
ROPE_THETA = 10000.0
SLC_STACK = 4
WIN_Q = 256
N_EXPERTS = 64
TOP_K = 8
N_GROUPS = 8
TOPK_GROUPS = 4
EXPERT_HIDDEN = 256
ROUTED_SCALE = 2.5
ROUTE_TM = 512
MOE_SUB = 256
MOE_ROUND = 32
MOE_EG = 8

LANES = 128
NEG = -1e30
VMEM_LIMIT = 56 * 1024 * 1024

C_HG = 0
C_Q = 4 * HG_WIDTH
C_K = C_Q + NSA_WIDTH
C_GATE = C_K + 6 * NSA_KV
IN_COLS_P = C_GATE + LANES
LOG2E = 1.4426950408889634

NT = (((1,), (1,)), ((), ()))
TN = (((0,), (0,)), ((), ()))


def _cparams(sem):
    return pltpu.CompilerParams(dimension_semantics=sem, vmem_limit_bytes=VMEM_LIMIT)


def _sigmoid(x):
    return 1.0 / (1.0 + jnp.exp(-x))


def _silu(x):
    return x * _sigmoid(x)


def _gelu_tanh(x):
    return 0.5 * x * (1.0 + jnp.tanh(0.7978845608028654 * (x + 0.044715 * (x * x * x))))


def _ada_kernel(c_ref, w_ref, b_ref, o_ref):
    a = _silu(c_ref[...])
    o_ref[...] = jnp.dot(a, w_ref[...], precision=HIGHEST, preferred_element_type=F32) + b_ref[...]


def _ada(c8, w, b):
    n = w.shape[1]
    tn = 1024
    return pl.pallas_call(
        _ada_kernel,
        out_shape=jax.ShapeDtypeStruct((8, n), F32),
        grid=(n // tn,),
        in_specs=[pl.BlockSpec((8, D_MODEL), lambda j: (0, 0)),
                  pl.BlockSpec((D_MODEL, tn), lambda j: (0, j)),
                  pl.BlockSpec((1, tn), lambda j: (0, j))],
        out_specs=pl.BlockSpec((8, tn), lambda j: (0, j)),
        compiler_params=_cparams(("arbitrary",)),
        name="ada",
    )(c8, w, b)


def _rope(t, cos, sin_signed, first_half):
    rot = jnp.where(first_half, pltpu.roll(t, 96, 1), pltpu.roll(t, 32, 1))
    return t * cos + rot * sin_signed


def _inproj_kernel(x_ref, sh_ref, sc_ref, g_ref, w_ref, cos_ref, sin_ref,
                   hg_ref, qraw_ref, qrot_ref, kc_ref, vc_ref, ks_ref, kw_ref, va_ref, gate_ref, h_scr):
    tm = x_ref.shape[1]
    x = x_ref[0]
    y = x * lax.rsqrt(jnp.mean(x * x, axis=-1, keepdims=True) + EPS) * g_ref[...]
    h_scr[...] = (y * (1.0 + sc_ref[0]) + sh_ref[0]).astype(BF16)

    def mm(lo, width):
        return jnp.dot(h_scr[...], w_ref[:, lo:lo + width], preferred_element_type=F32)

    cos = cos_ref[...]
    sin = sin_ref[...]
    first_half = (lax.broadcasted_iota(jnp.int32, cos.shape, 1) % NSA_HD) < (NSA_HD // 2)
    lane = lax.broadcasted_iota(jnp.int32, (tm, LANES), 1)
    half_of_lane = lane // NSA_HD

    def own_lanes(t, src_half, dst_half):
        moved = t if src_half == dst_half else pltpu.roll(t, NSA_HD, 1)
        return jnp.where(half_of_lane == dst_half, moved, 0.0)

    for j in range(4):
        hg_ref[0, :, j * HG_WIDTH:(j + 1) * HG_WIDTH] = mm(C_HG + j * HG_WIDTH, HG_WIDTH)
    for m in range(NSA_WIDTH // (2 * LANES)):
        qq = mm(C_Q + m * 2 * LANES, 2 * LANES) * (NSA_HD ** -0.5)
        for c in range(2):
            q = qq[:, c * LANES:(c + 1) * LANES]
            qr = _rope(q, cos, sin, first_half) * LOG2E
            for half in range(2):
                n = (2 * m + c) * 2 + half
                g = n // NSA_HG
                qraw_ref[0, :, n * LANES:(n + 1) * LANES] = own_lanes(q, half, g).astype(BF16)
                qrot_ref[0, :, n * LANES:(n + 1) * LANES] = own_lanes(qr, half, g).astype(BF16)
    kv = mm(C_K, 2 * LANES)
    kc_ref[0] = kv[:, :LANES].astype(BF16)
    vc_ref[0] = kv[:, LANES:].astype(BF16)
    pos = pl.program_id(1) * tm + lax.broadcasted_iota(jnp.int32, (tm, LANES), 0)
    ks_ref[0, :, LANES:2 * LANES] = jnp.where(lane == pos // SEL_BLOCK, 1.0, 0.0).astype(BF16)
    for j in range(2):
        kv = mm(C_K + (j + 1) * 2 * LANES, 2 * LANES)
        k_rot = _rope(kv[:, :LANES], cos, sin, first_half).astype(BF16)
        if j == 0:
            ks_ref[0, :, 0:LANES] = k_rot
        else:
            kw_ref[0] = k_rot
        for g in range(NSA_KV_GROUPS):
            v = own_lanes(kv[:, LANES:], g, 0)
            col = (j * NSA_KV_GROUPS + g) * LANES
            va_ref[0, :, col:col + LANES] = jnp.where(lane == NSA_HD, 1.0, v).astype(BF16)
    gate = _sigmoid(mm(C_GATE, LANES))
    per_group = 3 * NSA_HG
    for g in range(NSA_KV_GROUPS):
        gate_ref[0, :, g * LANES:(g + 1) * LANES] = gate if g == 0 else pltpu.roll(gate, LANES - g * per_group, 1)


def _inproj(x, mod3, norm_g, w_p, cos, sin, tm):
    B, S, D = x.shape
    blk = lambda w: pl.BlockSpec((1, tm, w), lambda b, i: (b, i, 0))
    return pl.pallas_call(
        _inproj_kernel,
        out_shape=(jax.ShapeDtypeStruct((B, S, 4 * HG_WIDTH), F32),
                   jax.ShapeDtypeStruct((B, S, NSA_HEADS * LANES), BF16),
                   jax.ShapeDtypeStruct((B, S, NSA_HEADS * LANES), BF16),
                   jax.ShapeDtypeStruct((B, S, LANES), BF16),
                   jax.ShapeDtypeStruct((B, S, LANES), BF16),
                   jax.ShapeDtypeStruct((B, S, 2 * LANES), BF16),
                   jax.ShapeDtypeStruct((B, S, LANES), BF16),
                   jax.ShapeDtypeStruct((B, S, 4 * LANES), BF16),
                   jax.ShapeDtypeStruct((B, S, NSA_KV_GROUPS * LANES), F32)),
        grid=(B, S // tm),
        in_specs=[blk(D),
                  pl.BlockSpec((1, 1, D), lambda b, i: (b, 0, 0)),
                  pl.BlockSpec((1, 1, D), lambda b, i: (b, 0, 1)),
                  pl.BlockSpec((1, D), lambda b, i: (0, 0)),
                  pl.BlockSpec((D, IN_COLS_P), lambda b, i: (0, 0)),
                  pl.BlockSpec((tm, LANES), lambda b, i: (i, 0)),
                  pl.BlockSpec((tm, LANES), lambda b, i: (i, 0))],
        out_specs=(blk(4 * HG_WIDTH), blk(NSA_HEADS * LANES), blk(NSA_HEADS * LANES),
                   blk(LANES), blk(LANES), blk(2 * LANES), blk(LANES), blk(4 * LANES), blk(NSA_KV_GROUPS * LANES)),
        scratch_shapes=[pltpu.VMEM((tm, D), BF16)],
        compiler_params=_cparams(("arbitrary", "arbitrary")),
        name="inproj",
    )(x, mod3, mod3, norm_g, w_p, cos, sin)


def _hgrn_kernel(q_ref, f_ref, i_ref, gt_ref, lb_ref, ng_ref, o_ref):
    S = q_ref.shape[1]
    C, U = HG_CHUNK, HG_SUB
    lb = lb_ref[...]
    ng = ng_ref[...]
    ri = lax.broadcasted_iota(jnp.int32, (C, C), 0)
    ci = lax.broadcasted_iota(jnp.int32, (C, C), 1)
    tril = (ri >= ci).astype(F32)
    trow = lax.broadcasted_iota(jnp.int32, (U, 1), 0)

    def chunk(c, st_t):
        r0 = pl.multiple_of(c * C, C)
        rows = pl.ds(r0, C)
        f = lb + (1.0 - lb) * _sigmoid(f_ref[0, rows, :])
        kk = 1.0 - f
        b = jnp.dot(tril, jnp.log(f), precision=HIGHEST, preferred_element_type=F32)
        q = q_ref[0, rows, :] * (HG_DK ** -0.5)
        v = i_ref[0, rows, :]
        vb = v.astype(BF16)
        o_inter = lax.dot_general((q * jnp.exp(b)).astype(BF16), st_t.astype(BF16), NT,
                                  preferred_element_type=F32)
        parts = []
        for i in range(C // U):
            lo = i * U
            bi = b[lo:lo + U]
            qi = q[lo:lo + U]
            if i == 0:
                oi = jnp.zeros((U, HG_DV), F32)
            else:
                r = b[lo - 1:lo]
                qrel = (qi * jnp.exp(bi - r)).astype(BF16)
                kprev = (kk[:lo] * jnp.exp(r - b[:lo])).astype(BF16)
                a_off = lax.dot_general(qrel, kprev, NT, preferred_element_type=F32)
                oi = jnp.dot(a_off.astype(BF16), vb[:lo], preferred_element_type=F32)
            for s in range(U):
                valid = trow >= s
                e = jnp.exp(jnp.where(valid, bi - bi[s:s + 1], 0.0))
                a = jnp.sum(qi * e * kk[lo + s:lo + s + 1], axis=-1, keepdims=True)
                oi = oi + jnp.where(valid, a, 0.0) * v[lo + s:lo + s + 1]
            parts.append(oi)
        o = o_inter + jnp.concatenate(parts, axis=0)
        o = o * lax.rsqrt(jnp.mean(o * o, axis=-1, keepdims=True) + EPS) * ng
        o_ref[0, rows, :] = (o * _silu(gt_ref[0, rows, :])).astype(BF16)
        bl = b[C - 1:C]
        kv_t = lax.dot_general(vb, (kk * jnp.exp(bl - b)).astype(BF16), TN, preferred_element_type=F32)
        return jnp.exp(bl) * st_t + kv_t

    lax.fori_loop(0, S // C, chunk, jnp.zeros((HG_DV, HG_DK), F32))


def _cumsum_rows(x):
    n = x.shape[0]
    row = lax.broadcasted_iota(jnp.int32, x.shape, 0)
    d = 1
    while d < n:
        x = x + jnp.where(row >= d, pltpu.roll(x, d, 0), 0.0)
        d *= 2
    return x


def _hgrn_mxu_kernel(hg_ref, lb_ref, ng_ref, o_ref, st_scr):
    ts = hg_ref.shape[1]
    C, U = HG_CHUNK, HG_SUB

    @pl.when(pl.program_id(1) == 0)
    def _():
        st_scr[...] = jnp.zeros(st_scr.shape, F32)

    W = HG_WIDTH
    NH = HG_HEADS
    head_of_lane = lax.broadcasted_iota(jnp.int32, (1, W), 1) // HG_DK
    hcols = [slice(h * HG_DK, (h + 1) * HG_DK) for h in range(NH)]

    def chunk(c, carry):
        rows = pl.ds(pl.multiple_of(c * C, C), C)
        lb = lb_ref[...]
        f = lb + (1.0 - lb) * _sigmoid(hg_ref[0, rows, W:2 * W])
        kk = 1.0 - f
        b = _cumsum_rows(jnp.log(f))
        q = hg_ref[0, rows, 0:W] * (HG_DK ** -0.5)
        vb = hg_ref[0, rows, 2 * W:3 * W].astype(BF16)
        qe = (q * jnp.exp(b)).astype(BF16)
        o_inter = jnp.concatenate(
            [lax.dot_general(qe[:, hc], st_scr[h].astype(BF16), NT, preferred_element_type=F32)
             for h, hc in enumerate(hcols)], axis=1)
        parts = []
        for i in range(C // U):
            lo, hi = i * U, (i + 1) * U
            r = b[lo - 1:lo] if i else jnp.zeros((1, W), F32)
            qrel = q[lo:hi] * jnp.exp(b[lo:hi] - r)
            kall = (kk[:hi] * jnp.exp(r - b[:hi])).astype(BF16)
            qbd = jnp.concatenate([jnp.where(head_of_lane == h, qrel, 0.0) for h in range(NH)], axis=0)
            a = lax.dot_general(qbd.astype(BF16), kall, NT, preferred_element_type=F32)
            trow = lax.broadcasted_iota(jnp.int32, (NH * U, hi), 0) % U
            a = jnp.where(lax.broadcasted_iota(jnp.int32, (NH * U, hi), 1) <= lo + trow, a, 0.0)
            oa = jnp.dot(a.astype(BF16), vb[:hi], preferred_element_type=F32)
            oi = jnp.where(head_of_lane == 0, oa[0:U], 0.0)
            for h in range(1, NH):
                oi = jnp.where(head_of_lane == h, oa[h * U:(h + 1) * U], oi)
            parts.append(oi)
        o = o_inter + jnp.concatenate(parts, axis=0)
        o = jnp.concatenate(
            [o[:, hc] * lax.rsqrt(jnp.mean(o[:, hc] * o[:, hc], axis=-1, keepdims=True) + EPS) for hc in hcols], axis=1)
        o_ref[0, rows, :] = (o * ng_ref[...] * _silu(hg_ref[0, rows, 3 * W:4 * W])).astype(BF16)
        bl = b[C - 1:C]
        ke = (kk * jnp.exp(bl - b)).astype(BF16)
        decay = jnp.exp(bl)
        for h, hc in enumerate(hcols):
            kv_t = lax.dot_general(vb[:, hc], ke[:, hc], TN, preferred_element_type=F32)
            st_scr[h] = decay[:, hc] * st_scr[h] + kv_t
        return carry

    lax.fori_loop(0, ts // C, chunk, 0, unroll=8)


def _hgrn_mxu(hg, lb, ng, ts):
    B, S, _ = hg.shape
    vec = pl.BlockSpec((1, HG_WIDTH), lambda b, i: (0, 0))
    return pl.pallas_call(
        _hgrn_mxu_kernel,
        out_shape=jax.ShapeDtypeStruct((B, S, HG_WIDTH), BF16),
        grid=(B, S // ts),
        in_specs=[pl.BlockSpec((1, ts, 4 * HG_WIDTH), lambda b, i: (b, i, 0)), vec, vec],
        out_specs=pl.BlockSpec((1, ts, HG_WIDTH), lambda b, i: (b, i, 0)),
        scratch_shapes=[pltpu.VMEM((HG_HEADS, HG_DV, HG_DK), F32)],
        compiler_params=_cparams(("arbitrary", "arbitrary")),
        name="hgrn_mxu",
    )(hg, lb, ng)


def _hgrn(hg, lb, ng):
    B, S, _ = hg.shape
    col = lambda k: pl.BlockSpec((1, S, HG_DK), lambda b, h, k=k: (b, 0, k * HG_HEADS + h))
    vec = pl.BlockSpec((1, HG_DK), lambda b, h: (0, h))
    return pl.pallas_call(
        _hgrn_kernel,
        out_shape=jax.ShapeDtypeStruct((B, S, HG_WIDTH), BF16),
        grid=(B, HG_HEADS),
        in_specs=[col(0), col(1), col(2), col(3), vec, vec],
        out_specs=pl.BlockSpec((1, S, HG_DV), lambda b, h: (b, 0, h)),
        compiler_params=_cparams(("arbitrary", "arbitrary")),
        name="hgrn",
    )(hg, hg, hg, hg, lb, ng)


def _cmpmlp_kernel(x_ref, w1a_ref, w1b_ref, pe_ref, w1_ref, b1_ref, w2_ref, o_ref):
    x = x_ref[0]
    hb = jnp.dot(pe_ref[...], w1_ref[...], precision=HIGHEST, preferred_element_type=F32)[0:1] + b1_ref[...]
    nrow = x.shape[0]
    for g in range(NSA_KV_GROUPS):
        a = jnp.dot(x, w1a_ref[g], preferred_element_type=F32)
        bm = jnp.dot(x, w1b_ref[g], preferred_element_type=F32)
        hdn = a + pltpu.roll(bm, nrow - 1, 0) + hb
        o_ref[0, g] = jnp.dot(_gelu_tanh(hdn).astype(BF16), w2_ref[g], preferred_element_type=F32).astype(BF16)


def _cmpmlp(x2, w1a, w1b, pe8, w1, b1, w2p):
    B, ncb, width = x2.shape
    full = lambda a: pl.BlockSpec(a.shape, lambda b: (0,) * a.ndim)
    return pl.pallas_call(
        _cmpmlp_kernel,
        out_shape=jax.ShapeDtypeStruct((B, NSA_KV_GROUPS, ncb, LANES), BF16),
        grid=(B,),
        in_specs=[pl.BlockSpec((1, ncb, width), lambda b: (b, 0, 0)),
                  full(w1a), full(w1b), full(pe8), full(w1), full(b1), full(w2p)],
        out_specs=pl.BlockSpec((1, NSA_KV_GROUPS, ncb, LANES), lambda b: (b, 0, 0, 0)),
        compiler_params=_cparams(("arbitrary",)),
        name="cmpmlp",
    )(x2, w1a, w1b, pe8, w1, b1, w2p)


def _rank_before(score, nrows):
    jrow = lax.broadcasted_iota(jnp.int32, score.shape, 0)
    rank = jnp.zeros(score.shape, F32)
    for k in range(nrows):
        rk = score[k:k + 1]
        beats = (rk > score) | ((rk == score) & (jrow > k))
        rank = rank + jnp.where(beats, 1.0, 0.0)
    return rank


def _topk_rows(score, k):
    n = score.shape[0]
    row = lax.broadcasted_iota(jnp.int32, score.shape, 0).astype(F32)
    keep = jnp.zeros(score.shape, F32)
    for _ in range(k):
        top = jnp.max(score, axis=0, keepdims=True)
        first = jnp.min(jnp.where(score == top, row, float(n)), axis=0, keepdims=True)
        pick = row == first
        keep = jnp.where(pick, 1.0, keep)
        score = jnp.where(pick, -jnp.inf, score)
    return keep


def _transpose_small_ints(xt):
    c = xt.shape[1]
    eye = (lax.broadcasted_iota(jnp.int32, (c, c), 0) == lax.broadcasted_iota(jnp.int32, (c, c), 1)).astype(BF16)
    return lax.dot_general(eye, xt.astype(BF16), NT, preferred_element_type=F32)


def _pack_heads(heads):
    low = lax.broadcasted_iota(jnp.int32, heads[0].shape, 1) < NSA_HD
    return jnp.concatenate([jnp.where(low, heads[k], pltpu.roll(heads[k + 1], NSA_HD, 1))
                            for k in range(0, len(heads), 2)], axis=1)


def _cmpsel_kernel(q_ref, kc_ref, vc_ref, gate_ref, o_ref, sel_ref):
    tq = q_ref.shape[1]
    ncb = kc_ref.shape[2]
    t = pl.program_id(2)
    kc = kc_ref[0, 0]
    vc = vc_ref[0, 0]
    pos = t * tq + lax.broadcasted_iota(jnp.int32, (tq, 1), 0)
    cblk = lax.broadcasted_iota(jnp.int32, (1, ncb), 1)
    vis = (cblk * CMP_STRIDE + CMP_BLOCK - 1) <= pos
    psum = jnp.zeros((tq, ncb), F32)
    gate = gate_ref[0]
    heads = []
    for h in range(NSA_HG):
        s = lax.dot_general(q_ref[0, :, h * LANES:(h + 1) * LANES], kc, NT, preferred_element_type=F32)
        s = jnp.where(vis, s, NEG)
        p = jnp.exp(s - jnp.max(s, axis=-1, keepdims=True))
        p = jnp.where(vis, p / jnp.sum(p, axis=-1, keepdims=True), 0.0)
        heads.append(jnp.dot(p.astype(BF16), vc, preferred_element_type=F32) * gate[:, 3 * h:3 * h + 1])
        psum = psum + p
    o_ref[0] = _pack_heads(heads)
    nsb = ncb * CMP_STRIDE // SEL_BLOCK
    jb = lax.broadcasted_iota(jnp.int32, (nsb, ncb), 0) * SEL_BLOCK
    cb = lax.broadcasted_iota(jnp.int32, (nsb, ncb), 1) * CMP_STRIDE
    ov = jnp.maximum(jnp.minimum(cb + CMP_BLOCK, jb + SEL_BLOCK) - jnp.maximum(cb, jb), 0).astype(F32) / CMP_BLOCK
    ps_hi = psum.astype(BF16)
    ps_lo = (psum - ps_hi.astype(F32)).astype(BF16)
    ovb = ov.astype(BF16)
    pslc_t = (lax.dot_general(ovb, ps_hi, NT, preferred_element_type=F32)
              + lax.dot_general(ovb, ps_lo, NT, preferred_element_type=F32))
    posl = t * tq + lax.broadcasted_iota(jnp.int32, (1, tq), 1)
    cur = posl // SEL_BLOCK
    jrow = lax.broadcasted_iota(jnp.int32, (nsb, tq), 0)
    forced = (jrow == 0) | (jrow == cur) | (jrow == cur - 1)
    score = jnp.where(forced, 1e30, jnp.where(jrow <= cur, pslc_t, NEG))
    chosen = _topk_rows(score, min(N_SEL, nsb)) > 0.5
    drop = jnp.where(chosen & (score > -1e29), 0.0, 1.0)
    drop = jnp.concatenate([drop, jnp.zeros((LANES - nsb, tq), F32)], axis=0)
    sel_ref[0, 0] = (_transpose_small_ints(drop) * NEG).astype(BF16)


def _cmpsel(qraw, kcmp, vcmp, gate, tq):
    B, S, _ = qraw.shape
    ncb = kcmp.shape[2]
    gw = NSA_HG * LANES
    assert S // SEL_BLOCK <= NSA_HD
    cmp_spec = pl.BlockSpec((1, 1, ncb, LANES), lambda b, g, t: (b, g, 0, 0))
    return pl.pallas_call(
        _cmpsel_kernel,
        out_shape=(jax.ShapeDtypeStruct((B, S, NSA_WIDTH), F32),
                   jax.ShapeDtypeStruct((B, NSA_KV_GROUPS, S, LANES), BF16)),
        grid=(B, NSA_KV_GROUPS, S // tq),
        in_specs=[pl.BlockSpec((1, tq, gw), lambda b, g, t: (b, t, g)), cmp_spec, cmp_spec,
                  pl.BlockSpec((1, tq, LANES), lambda b, g, t: (b, t, g))],
        out_specs=(pl.BlockSpec((1, tq, NSA_WIDTH // NSA_KV_GROUPS), lambda b, g, t: (b, t, g)),
                   pl.BlockSpec((1, 1, tq, LANES), lambda b, g, t: (b, g, t, 0))),
        compiler_params=_cparams(("arbitrary", "arbitrary", "arbitrary")),
        name="cmpsel",
    )(qraw, kcmp, vcmp, gate)


def _slcwin_kernel(q_ref, sel_ref, ks_ref, kw_ref, vs_ref, vw_ref, gate_ref, ocmp_ref, o_ref, *, tk, wk):
    tq = q_ref.shape[1]
    S = ks_ref.shape[1]
    t = pl.program_id(2)
    q0 = t * tq
    qpos = q0 + lax.broadcasted_iota(jnp.int32, (tq, 1), 0)
    qw = [q_ref[0, :, h * LANES:(h + 1) * LANES] for h in range(NSA_HG)]
    sel = sel_ref[0, 0]
    qa = [jnp.concatenate([q, sel], axis=1) for q in qw]

    gate = gate_ref[0]

    def normalise(acc, g):
        return acc * (g / acc[:, NSA_HD:NSA_HD + 1])

    nchain = NSA_HG // SLC_STACK
    qst = [jnp.concatenate(qa[c * SLC_STACK:(c + 1) * SLC_STACK], axis=0) for c in range(nchain)]
    qpos_st = jnp.concatenate([qpos] * SLC_STACK, axis=0)

    def kstep(j, carry, diag):
        k0 = pl.multiple_of(j * tk, tk)
        ks = ks_ref[0, pl.ds(k0, tk), :]
        vs = vs_ref[0, pl.ds(k0, tk), :]
        if diag:
            vis = (k0 + lax.broadcasted_iota(jnp.int32, (1, tk), 1)) <= qpos_st
        out = []
        for c in range(nchain):
            m, acc = carry[c]
            s = lax.dot_general(qst[c], ks, NT, preferred_element_type=F32)
            if diag:
                s = jnp.where(vis, s, NEG)
            mn = jnp.maximum(m, jnp.max(s, axis=-1, keepdims=True))
            p = jnp.exp2(s - mn)
            acc = jnp.exp2(m - mn) * acc + jnp.dot(p.astype(BF16), vs, preferred_element_type=F32)
            out.append((mn, acc))
        return tuple(out)

    init = tuple((jnp.full((SLC_STACK * tq, 1), NEG, F32), jnp.zeros((SLC_STACK * tq, LANES), F32))
                 for _ in range(nchain))
    jdiag = q0 // tk
    carry = lax.fori_loop(0, jdiag, lambda j, c: kstep(j, c, False), init)
    carry = kstep(jdiag, carry, True)
    heads = [normalise(carry[h // SLC_STACK][1][(h % SLC_STACK) * tq:(h % SLC_STACK + 1) * tq],
                       gate[:, 3 * h + 1:3 * h + 2]) for h in range(NSA_HG)]

    wq = wk - WINDOW
    wins = [[] for _ in range(NSA_HG)]
    spos = lax.broadcasted_iota(jnp.int32, (NSA_HG * wq, 1), 0) % wq
    for i in range(tq // wq):
        rows = slice(i * wq, (i + 1) * wq)
        start = pl.multiple_of(jnp.maximum(q0 + (i + 1) * wq - wk, 0), wq)
        kw = kw_ref[0, pl.ds(start, wk), :]
        vw = vw_ref[0, pl.ds(start, wk), :]
        d = (q0 + i * wq + spos) - (start + lax.broadcasted_iota(jnp.int32, (1, wk), 1))
        q_st = jnp.concatenate([q[rows] for q in qw], axis=0)
        s = jnp.where((d >= 0) & (d < WINDOW), lax.dot_general(q_st, kw, NT, preferred_element_type=F32), NEG)
        p = jnp.exp2(s - jnp.max(s, axis=-1, keepdims=True))
        o_st = jnp.dot(p.astype(BF16), vw, preferred_element_type=F32)
        for h in range(NSA_HG):
            wins[h].append(o_st[h * wq:(h + 1) * wq])
    for h in range(NSA_HG):
        heads[h] = heads[h] + normalise(jnp.concatenate(wins[h], axis=0), gate[:, 3 * h + 2:3 * h + 3])
    o_ref[0] = (ocmp_ref[0] + _pack_heads(heads)).astype(BF16)


def _slcwin(qrot, sel, ks, kw, vaug, gate, ocmp, tq, tk):
    B, S, _ = qrot.shape
    gw = NSA_HG * LANES
    wk = WINDOW + WIN_Q
    assert S >= wk and tq % WIN_Q == 0
    ospec = pl.BlockSpec((1, tq, NSA_WIDTH // NSA_KV_GROUPS), lambda b, g, t: (b, t, g))
    return pl.pallas_call(
        functools.partial(_slcwin_kernel, tk=tk, wk=wk),
        out_shape=jax.ShapeDtypeStruct((B, S, NSA_WIDTH), BF16),
        grid=(B, NSA_KV_GROUPS, S // tq),
        in_specs=[pl.BlockSpec((1, tq, gw), lambda b, g, t: (b, t, g)),
                  pl.BlockSpec((1, 1, tq, LANES), lambda b, g, t: (b, g, t, 0)),
                  pl.BlockSpec((1, S, 2 * LANES), lambda b, g, t: (b, 0, 0)),
                  pl.BlockSpec((1, S, LANES), lambda b, g, t: (b, 0, 0)),
                  pl.BlockSpec((1, S, LANES), lambda b, g, t: (b, 0, g)),
                  pl.BlockSpec((1, S, LANES), lambda b, g, t: (b, 0, NSA_KV_GROUPS + g)),
                  pl.BlockSpec((1, tq, LANES), lambda b, g, t: (b, t, g)),
                  ospec],
        out_specs=ospec,
        compiler_params=_cparams(("arbitrary", "arbitrary", "arbitrary")),
        name="slcwin",
    )(qrot, sel, ks, kw, vaug, vaug, gate, ocmp)


def _outproj_kernel(x_ref, ohg_ref, onsa_ref, wout_ref,
                    g1_ref, sh2_ref, sc2_ref, n2_ref, rwt_ref, rb_ref,
                    x1_ref, h2_ref, sc_ref, slot_t_ref, w_t_ref, cnt_ref):
    tm = x_ref.shape[1]
    mix = jnp.dot(jnp.concatenate([ohg_ref[0], onsa_ref[0]], axis=1), wout_ref[...], preferred_element_type=F32)
    x1 = x_ref[0] + g1_ref[0] * mix
    x1_ref[0] = x1
    h2 = (x1 * lax.rsqrt(jnp.mean(x1 * x1, axis=-1, keepdims=True) + EPS) * n2_ref[...]) * (1.0 + sc2_ref[0]) + sh2_ref[0]
    h_hi = h2.astype(BF16)
    h2_ref[0] = h_hi
    h_lo = (h2 - h_hi.astype(F32)).astype(BF16)
    both = lax.dot_general(rwt_ref[...], h_hi, NT, preferred_element_type=F32)
    logits = (both[:N_EXPERTS] + both[N_EXPERTS:]
              + lax.dot_general(rwt_ref[:N_EXPERTS, :], h_lo, NT, preferred_element_type=F32))
    scores = _sigmoid(logits)
    choice = scores + rb_ref[...]
    per = N_EXPERTS // N_GROUPS
    c3 = choice.reshape(N_GROUPS, per, tm)
    erow = lax.broadcasted_iota(jnp.int32, c3.shape, 1)
    rank_in = jnp.zeros(c3.shape, F32)
    for k in range(per):
        ck = c3[:, k:k + 1, :]
        rank_in = rank_in + jnp.where((ck > c3) | ((ck == c3) & (erow > k)), 1.0, 0.0)
    grp_score = jnp.sum(jnp.where(rank_in < 2, c3, 0.0), axis=1)
    grp_keep = _rank_before(grp_score, N_GROUPS) < TOPK_GROUPS
    masked = jnp.where(grp_keep[:, None, :], c3, -jnp.inf).reshape(N_EXPERTS, tm)
    keep = _topk_rows(masked, TOP_K) > 0.5
    tw = jnp.where(keep, scores, 0.0)
    tw = tw / jnp.sum(tw, axis=0, keepdims=True) * ROUTED_SCALE
    before = lax.broadcasted_iota(jnp.int32, (tm, tm), 0)
    after = lax.broadcasted_iota(jnp.int32, (tm, tm), 1)
    earlier = (before < after) & (before // MOE_SUB == after // MOE_SUB)
    kept = jnp.where(keep, 1.0, 0.0)
    pref = jnp.dot(kept.astype(BF16), earlier.astype(BF16), preferred_element_type=F32)
    slot = jnp.where(keep, pref, -1.0)
    assert MOE_SUB <= 256
    sc_ref[0] = _transpose_small_ints(slot)
    for j in range(tm // MOE_SUB):
        sub = slice(j * MOE_SUB, (j + 1) * MOE_SUB)
        slot_t_ref[:, j] = slot[:, sub].reshape(N_EXPERTS // MOE_EG, MOE_EG, MOE_SUB)
        w_t_ref[:, j] = tw[:, sub].reshape(N_EXPERTS // MOE_EG, MOE_EG, MOE_SUB)
        cnt_ref[j] = jnp.broadcast_to(jnp.sum(kept[:, sub], axis=1, keepdims=True), (N_EXPERTS, LANES))


def _outproj(x, ohg, onsa, wout, mod3, n2, rwt, rb, tm):
    B, S, D = x.shape
    blk = lambda w: pl.BlockSpec((1, tm, w), lambda b, i: (b, i, 0))
    full = lambda a: pl.BlockSpec(a.shape, lambda b, i: (0,) * a.ndim)
    modc = lambda k: pl.BlockSpec((1, 1, D), lambda b, i, k=k: (b, 0, k))
    nt = S // tm
    ng = N_EXPERTS // MOE_EG
    nh = tm // MOE_SUB
    tspec = pl.BlockSpec((ng, nh, MOE_EG, MOE_SUB), lambda b, i: (0, b * nt + i, 0, 0))
    return pl.pallas_call(
        _outproj_kernel,
        out_shape=(jax.ShapeDtypeStruct((B, S, D), F32),
                   jax.ShapeDtypeStruct((B, S, D), BF16),
                   jax.ShapeDtypeStruct((B, S, N_EXPERTS), F32),
                   jax.ShapeDtypeStruct((ng, B * nt * nh, MOE_EG, MOE_SUB), F32),
                   jax.ShapeDtypeStruct((ng, B * nt * nh, MOE_EG, MOE_SUB), F32),
                   jax.ShapeDtypeStruct((B * nt * nh, N_EXPERTS, LANES), F32)),
        grid=(B, nt),
        in_specs=[blk(D), blk(HG_WIDTH), blk(NSA_WIDTH), full(wout),
                  modc(2), modc(3), modc(4), full(n2), full(rwt), full(rb)],
        out_specs=(blk(D), blk(D), blk(N_EXPERTS), tspec, tspec,
                   pl.BlockSpec((nh, N_EXPERTS, LANES), lambda b, i: (b * nt + i, 0, 0))),
        compiler_params=_cparams(("arbitrary", "arbitrary")),
        name="outproj",
    )(x, ohg, onsa, wout, mod3, mod3, mod3, n2, rwt, rb)


def _swiglu_hidden(x, wgu):
    gu = jnp.dot(x, wgu, preferred_element_type=F32)
    return _silu(gu[:, :EXPERT_HIDDEN]) * gu[:, EXPERT_HIDDEN:]


def _moe_kernel(rounds_ref, h_ref, slot_t_ref, w_t_ref, slot_c_ref, wgu_ref, wdn_ref, sgu_ref, sdn_ref,
                o_ref, x_scr, y_scr, p_scr):
    i = pl.program_id(0)
    g = pl.program_id(1)
    tm = h_ref.shape[0]
    nsub = tm // MOE_SUB
    rnd, eg = MOE_ROUND, MOE_EG
    ng = N_EXPERTS // eg
    base = i * (N_EXPERTS + ng)

    @pl.when(g == 0)
    def _():
        act = _swiglu_hidden(h_ref[...], sgu_ref[...]).astype(BF16)
        o_ref[...] = jnp.dot(act, sdn_ref[...], preferred_element_type=F32)

    rslot = lax.broadcasted_iota(jnp.int32, (rnd, 1), 0).astype(F32)
    lane_slot = (lax.broadcasted_iota(jnp.int32, (1, eg * rnd), 1) % rnd).astype(F32)
    spread = (lax.broadcasted_iota(jnp.int32, (N_EXPERTS, eg * rnd), 1) // rnd + g * eg
              == lax.broadcasted_iota(jnp.int32, (N_EXPERTS, eg * rnd), 0)).astype(BF16)

    def run_expert(e, rows):
        xe = jnp.concatenate([x_scr[s, rows, :] for s in range(nsub)], axis=0)
        wslot = jnp.concatenate(
            [jnp.sum(p_scr[s, rows, :].astype(F32) * w_t_ref[0, s][e:e + 1, :], axis=-1, keepdims=True)
             for s in range(nsub)], axis=0)
        y = jnp.dot((_swiglu_hidden(xe, wgu_ref[e]) * wslot).astype(BF16), wdn_ref[e], preferred_element_type=F32)
        for s in range(nsub):
            y_scr[s, rows, :] = y[s * rnd:(s + 1) * rnd].astype(BF16)

    def one_round(r, carry, every_expert=False):
        first = (r * rnd).astype(F32) if not every_expert else 0.0
        for s in range(nsub):
            st = slot_t_ref[0, s] - first
            p = jnp.concatenate([jnp.where(rslot == st[e:e + 1, :], 1.0, 0.0) for e in range(eg)], axis=0)
            p_scr[s] = p.astype(BF16)
            x_scr[s] = jnp.dot(p_scr[s], h_ref[s * MOE_SUB:(s + 1) * MOE_SUB, :],
                               preferred_element_type=F32).astype(BF16)
        for e in range(eg):
            rows = slice(e * rnd, (e + 1) * rnd)
            if every_expert:
                run_expert(e, rows)
                continue
            pl.when(r < rounds_ref[base + g * eg + e])(functools.partial(run_expert, e, rows))

            @pl.when(r >= rounds_ref[base + g * eg + e])
            def _():
                for s in range(nsub):
                    y_scr[s, rows, :] = jnp.zeros((rnd, D_MODEL), BF16)
        for s in range(nsub):
            toks = slice(s * MOE_SUB, (s + 1) * MOE_SUB)
            sc = jnp.dot(slot_c_ref[toks, :].astype(BF16), spread, preferred_element_type=F32) - first
            pt = jnp.where(sc == lane_slot, 1.0, 0.0).astype(BF16)
            o_ref[toks, :] += jnp.dot(pt, y_scr[s], preferred_element_type=F32)
        return carry

    one_round(0, 0, every_expert=True)
    lax.fori_loop(1, rounds_ref[base + N_EXPERTS + g], one_round, 0)


def _moe(rounds, h2, slot_t, w_t, slot_c, wgu, wdn, sgu, sdn, tm):
    T, D = h2.shape
    eg, nsub = MOE_EG, tm // MOE_SUB
    full = lambda a: pl.BlockSpec(a.shape, lambda i, e, o: (0,) * a.ndim)
    tspec = pl.BlockSpec((1, nsub, eg, MOE_SUB), lambda i, e, o: (e, i, 0, 0))
    return pl.pallas_call(
        _moe_kernel,
        out_shape=jax.ShapeDtypeStruct((T, D), F32),
        grid_spec=pltpu.PrefetchScalarGridSpec(
            num_scalar_prefetch=1,
            grid=(T // tm, N_EXPERTS // eg),
            in_specs=[pl.BlockSpec((tm, D), lambda i, e, o: (i, 0), pipeline_mode=pl.Buffered(1)), tspec, tspec,
                      pl.BlockSpec((tm, N_EXPERTS), lambda i, e, o: (i, 0)),
                      pl.BlockSpec((eg, D, 2 * EXPERT_HIDDEN), lambda i, e, o: (e, 0, 0)),
                      pl.BlockSpec((eg, EXPERT_HIDDEN, D), lambda i, e, o: (e, 0, 0)),
                      full(sgu), full(sdn)],
            out_specs=pl.BlockSpec((tm, D), lambda i, e, o: (i, 0), pipeline_mode=pl.Buffered(1)),
            scratch_shapes=[pltpu.VMEM((nsub, eg * MOE_ROUND, D), BF16),
                            pltpu.VMEM((nsub, eg * MOE_ROUND, D), BF16),
                            pltpu.VMEM((nsub, eg * MOE_ROUND, MOE_SUB), BF16)]),
        compiler_params=_cparams(("arbitrary", "arbitrary")),
        name="moe",
    )(rounds, h2, slot_t, w_t, slot_c, wgu, wdn, sgu, sdn)


def _final_kernel(x1_ref, moe_ref, g2_ref, fg_ref, o_ref):
    x2 = x1_ref[0] + g2_ref[0] * moe_ref[0]
    o_ref[0] = x2 * lax.rsqrt(jnp.mean(x2 * x2, axis=-1, keepdims=True) + EPS) * fg_ref[...]


def _final(x1, moe, mod3, fg, tm):
    B, S, D = x1.shape
    blk = pl.BlockSpec((1, tm, D), lambda b, i: (b, i, 0))
    return pl.pallas_call(
        _final_kernel,
        out_shape=jax.ShapeDtypeStruct((B, S, D), F32),
        grid=(B, S // tm),
        in_specs=[blk, blk, pl.BlockSpec((1, 1, D), lambda b, i: (b, 0, 5)), pl.BlockSpec((1, D), lambda b, i: (0, 0))],
        out_specs=blk,
        compiler_params=_cparams(("arbitrary", "arbitrary")),
        name="final",
    )(x1, moe, mod3, fg)


def _split_bf16(w):
    hi = w.astype(BF16)
    return jnp.concatenate([hi, (w - hi.astype(F32)).astype(BF16)], axis=0)


def _pack_w_in(w_in):
    return jnp.pad(w_in, ((0, 0), (0, IN_COLS_P - w_in.shape[1]))).astype(BF16)


def _pack_cmp(pos, w1, b1, w2, lane_by_group):
    half = CMP_STRIDE * NSA_HD
    def rows_for(wh):
        w3 = wh.reshape(CMP_STRIDE, NSA_HD, CMP_HIDDEN)
        z = jnp.zeros_like(w3)
        return jnp.stack([jnp.concatenate([w3, z], axis=1).reshape(CMP_STRIDE * LANES, CMP_HIDDEN),
                          jnp.concatenate([z, w3], axis=1).reshape(CMP_STRIDE * LANES, CMP_HIDDEN)])
    w1a = rows_for(w1[:half]).astype(BF16)
    w1b = rows_for(w1[half:]).astype(BF16)
    z2 = jnp.zeros_like(w2)
    w2_first = jnp.concatenate([w2, z2], axis=1)
    w2p = jnp.stack([w2_first, jnp.concatenate([z2, w2], axis=1) if lane_by_group else w2_first]).astype(BF16)
    pe8 = jnp.pad(pos.reshape(1, CMP_BLOCK * NSA_HD), ((0, 7), (0, 0)))
    return w1a, w1b, pe8, w1, b1.reshape(1, CMP_HIDDEN), w2p


def _rope_tables(S):
    half = NSA_HD // 2
    inv = ROPE_THETA ** (-jnp.arange(half, dtype=F32) / half)
    ang = jnp.arange(S, dtype=F32)[:, None] * inv[None, :]
    cos, sin = jnp.cos(ang), jnp.sin(ang)
    reps = LANES // NSA_HD
    return jnp.tile(jnp.concatenate([cos, cos], axis=1), (1, reps)), jnp.tile(jnp.concatenate([-sin, sin], axis=1), (1, reps))


def _tiles(S):
    return dict(inproj=min(512, S), hgrn=min(512, S), cmpsel=min(512, S), slc_q=512, slc_k=512, moe=min(2048, S), final=min(512, S))


def kernel(x, c, w_ada, b_ada, norm1_g, w_in, hg_lb_logits, hg_norm_g, cmp_pos_k, cmp_w1_k, cmp_b1_k, cmp_w2_k,
           cmp_pos_v, cmp_w1_v, cmp_b1_v, cmp_w2_v, w_out, norm2_g, router_w, router_bias, w_exp_gu, w_exp_dn,
           w_sh_gu, w_sh_dn, final_g):
    B, S, D = x.shape
    assert D == D_MODEL and w_ada.shape[0] == 1 and S % 512 == 0
    tl = _tiles(S)
    l = 0
    lb = jnp.cumsum(jax.nn.softmax(hg_lb_logits.astype(F32), axis=0), axis=0)[l].reshape(1, HG_WIDTH)
    c8 = jnp.pad(c, ((0, 8 - B), (0, 0)))
    mod3 = _ada(c8, w_ada[l], b_ada[l].reshape(1, -1))[:B].reshape(B, 1, 6 * D)
    cos, sin = _rope_tables(S)
    hg, qraw, qrot, kc, vc, ks, kw, vaug, gate = _inproj(x, mod3, norm1_g[l].reshape(1, D), _pack_w_in(w_in[l]),
                                                         cos, sin, tl["inproj"])
    ng = hg_norm_g[l].reshape(1, HG_WIDTH)
    mxu_safe = -HG_SUB * jnp.log(jnp.min(lb)) <= HG_SAFE_LOG_RANGE
    ohg = lax.cond(mxu_safe, lambda: _hgrn_mxu(hg, lb, ng, tl["hgrn"]), lambda: _hgrn(hg, lb, ng))
    ncb = S // CMP_STRIDE
    kcmp = _cmpmlp(kc.reshape(B, ncb, CMP_STRIDE * LANES),
                   *_pack_cmp(cmp_pos_k[l], cmp_w1_k[l], cmp_b1_k[l], cmp_w2_k[l], True))
    vcmp = _cmpmlp(vc.reshape(B, ncb, CMP_STRIDE * LANES),
                   *_pack_cmp(cmp_pos_v[l], cmp_w1_v[l], cmp_b1_v[l], cmp_w2_v[l], False))
    ocmp, sel = _cmpsel(qraw, kcmp, vcmp, gate, tl["cmpsel"])
    onsa = _slcwin(qrot, sel, ks, kw, vaug, gate, ocmp, tl["slc_q"], tl["slc_k"])
    x1, h2, sc, slot_t, w_t, cnt = _outproj(x, ohg, onsa, w_out[l].astype(BF16), mod3,
                                            norm2_g[l].reshape(1, D), _split_bf16(router_w[l].T),
                                            router_bias[l].reshape(N_EXPERTS, 1), ROUTE_TM)
    T = B * S
    tm = tl["moe"]
    ng = N_EXPERTS // MOE_EG
    slot_c = sc.reshape(T, N_EXPERTS)
    load = jnp.max(cnt[:, :, 0].reshape(T // tm, tm // MOE_SUB, N_EXPERTS), axis=1).astype(jnp.int32)
    per_expert = (load + MOE_ROUND - 1) // MOE_ROUND
    rounds = jnp.concatenate([per_expert, jnp.max(per_expert.reshape(T // tm, ng, MOE_EG), axis=2)], axis=1).reshape(-1)
    moe = _moe(rounds, h2.reshape(T, D), slot_t, w_t, slot_c,
               w_exp_gu[l].astype(BF16), w_exp_dn[l].astype(BF16), w_sh_gu[l].astype(BF16), w_sh_dn[l].astype(BF16), tm)
    return _final(x1, moe.reshape(B, S, D), mod3, final_g.reshape(1, D), tl["final"])
```

```python
import functools

import jax
import jax.numpy as jnp
from jax import lax
from jax.experimental import pallas as pl
from jax.experimental.pallas import tpu as pltpu

F32 = jnp.float32
BF16 = jnp.bfloat16
HIGHEST = lax.Precision.HIGHEST

D_MODEL = 1024
EPS = 1e-6
HG_HEADS = 4
HG_DK = 128
HG_DV = 128
HG_WIDTH = HG_HEADS * HG_DV
HG_CHUNK = 64
HG_SUB = 16
HG_SAFE_LOG_RANGE = 80.0
NSA_HEADS = 8
NSA_KV_GROUPS = 2
NSA_HG = NSA_HEADS // NSA_KV_GROUPS
NSA_HD = 64
NSA_WIDTH = NSA_HEADS * NSA_HD
NSA_KV = NSA_KV_GROUPS * NSA_HD
CMP_BLOCK = 32
CMP_STRIDE = 16
CMP_HIDDEN = 256
SEL_BLOCK = 64
N_SEL = 16
WINDOW = 512
ROPE_THETA = 10000.0
SLC_STACK = 4
WIN_Q = 256
N_EXPERTS = 64
TOP_K = 8
N_GROUPS = 8
TOPK_GROUPS = 4
EXPERT_HIDDEN = 256
ROUTED_SCALE = 2.5
ROUTE_TM = 512
MOE_SUB = 256
MOE_ROUND = 32
MOE_EG = 8

LANES = 128
NEG = -1e30
VMEM_LIMIT = 56 * 1024 * 1024

C_HG = 0
C_Q = 4 * HG_WIDTH
C_K = C_Q + NSA_WIDTH
C_GATE = C_K + 6 * NSA_KV
IN_COLS_P = C_GATE + LANES
LOG2E = 1.4426950408889634

NT = (((1,), (1,)), ((), ()))
TN = (((0,), (0,)), ((), ()))


def _cparams(sem):
    return pltpu.CompilerParams(dimension_semantics=sem, vmem_limit_bytes=VMEM_LIMIT)


def _sigmoid(x):
    return 1.0 / (1.0 + jnp.exp(-x))


def _silu(x):
    return x * _sigmoid(x)


def _gelu_tanh(x):
    return 0.5 * x * (1.0 + jnp.tanh(0.7978845608028654 * (x + 0.044715 * (x * x * x))))


def _ada_kernel(c_ref, w_ref, b_ref, o_ref):
    a = _silu(c_ref[...])
    o_ref[...] = jnp.dot(a, w_ref[...], precision=HIGHEST, preferred_element_type=F32) + b_ref[...]


def _ada(c8, w, b):
    n = w.shape[1]
    tn = 1024
    return pl.pallas_call(
        _ada_kernel,
        out_shape=jax.ShapeDtypeStruct((8, n), F32),
        grid=(n // tn,),
        in_specs=[pl.BlockSpec((8, D_MODEL), lambda j: (0, 0)),
                  pl.BlockSpec((D_MODEL, tn), lambda j: (0, j)),
                  pl.BlockSpec((1, tn), lambda j: (0, j))],
        out_specs=pl.BlockSpec((8, tn), lambda j: (0, j)),
        compiler_params=_cparams(("arbitrary",)),
        name="ada",
    )(c8, w, b)


def _rope(t, cos, sin_signed, first_half):
    rot = jnp.where(first_half, pltpu.roll(t, 96, 1), pltpu.roll(t, 32, 1))
    return t * cos + rot * sin_signed


def _inproj_kernel(x_ref, sh_ref, sc_ref, g_ref, w_ref, cos_ref, sin_ref,
                   hg_ref, qraw_ref, qrot_ref, kc_ref, vc_ref, ks_ref, kw_ref, va_ref, gate_ref, h_scr):
    tm = x_ref.shape[1]
    x = x_ref[0]
    y = x * lax.rsqrt(jnp.mean(x * x, axis=-1, keepdims=True) + EPS) * g_ref[...]
    h_scr[...] = (y * (1.0 + sc_ref[0]) + sh_ref[0]).astype(BF16)

    def mm(lo, width):
        return jnp.dot(h_scr[...], w_ref[:, lo:lo + width], preferred_element_type=F32)

    cos = cos_ref[...]
    sin = sin_ref[...]
    first_half = (lax.broadcasted_iota(jnp.int32, cos.shape, 1) % NSA_HD) < (NSA_HD // 2)
    lane = lax.broadcasted_iota(jnp.int32, (tm, LANES), 1)
    half_of_lane = lane // NSA_HD

    def own_lanes(t, src_half, dst_half):
        moved = t if src_half == dst_half else pltpu.roll(t, NSA_HD, 1)
        return jnp.where(half_of_lane == dst_half, moved, 0.0)

    for j in range(4):
        hg_ref[0, :, j * HG_WIDTH:(j + 1) * HG_WIDTH] = mm(C_HG + j * HG_WIDTH, HG_WIDTH)
    for m in range(NSA_WIDTH // (2 * LANES)):
        qq = mm(C_Q + m * 2 * LANES, 2 * LANES) * (NSA_HD ** -0.5)
        for c in range(2):
            q = qq[:, c * LANES:(c + 1) * LANES]
            qr = _rope(q, cos, sin, first_half) * LOG2E
            for half in range(2):
                n = (2 * m + c) * 2 + half
                g = n // NSA_HG
                qraw_ref[0, :, n * LANES:(n + 1) * LANES] = own_lanes(q, half, g).astype(BF16)
                qrot_ref[0, :, n * LANES:(n + 1) * LANES] = own_lanes(qr, half, g).astype(BF16)
    kv = mm(C_K, 2 * LANES)
    kc_ref[0] = kv[:, :LANES].astype(BF16)
    vc_ref[0] = kv[:, LANES:].astype(BF16)
    pos = pl.program_id(1) * tm + lax.broadcasted_iota(jnp.int32, (tm, LANES), 0)
    ks_ref[0, :, LANES:2 * LANES] = jnp.where(lane == pos // SEL_BLOCK, 1.0, 0.0).astype(BF16)
    for j in range(2):
        kv = mm(C_K + (j + 1) * 2 * LANES, 2 * LANES)
        k_rot = _rope(kv[:, :LANES], cos, sin, first_half).astype(BF16)
        if j == 0:
            ks_ref[0, :, 0:LANES] = k_rot
        else:
            kw_ref[0] = k_rot
        for g in range(NSA_KV_GROUPS):
            v = own_lanes(kv[:, LANES:], g, 0)
            col = (j * NSA_KV_GROUPS + g) * LANES
            va_ref[0, :, col:col + LANES] = jnp.where(lane == NSA_HD, 1.0, v).astype(BF16)
    gate = _sigmoid(mm(C_GATE, LANES))
    per_group = 3 * NSA_HG
    for g in range(NSA_KV_GROUPS):
        gate_ref[0, :, g * LANES:(g + 1) * LANES] = gate if g == 0 else pltpu.roll(gate, LANES - g * per_group, 1)


def _inproj(x, mod3, norm_g, w_p, cos, sin, tm):
    B, S, D = x.shape
    blk = lambda w: pl.BlockSpec((1, tm, w), lambda b, i: (b, i, 0))
    return pl.pallas_call(
        _inproj_kernel,
        out_shape=(jax.ShapeDtypeStruct((B, S, 4 * HG_WIDTH), F32),
                   jax.ShapeDtypeStruct((B, S, NSA_HEADS * LANES), BF16),
                   jax.ShapeDtypeStruct((B, S, NSA_HEADS * LANES), BF16),
                   jax.ShapeDtypeStruct((B, S, LANES), BF16),
                   jax.ShapeDtypeStruct((B, S, LANES), BF16),
                   jax.ShapeDtypeStruct((B, S, 2 * LANES), BF16),
                   jax.ShapeDtypeStruct((B, S, LANES), BF16),
                   jax.ShapeDtypeStruct((B, S, 4 * LANES), BF16),
                   jax.ShapeDtypeStruct((B, S, NSA_KV_GROUPS * LANES), F32)),
        grid=(B, S // tm),
        in_specs=[blk(D),
                  pl.BlockSpec((1, 1, D), lambda b, i: (b, 0, 0)),
                  pl.BlockSpec((1, 1, D), lambda b, i: (b, 0, 1)),
                  pl.BlockSpec((1, D), lambda b, i: (0, 0)),
                  pl.BlockSpec((D, IN_COLS_P), lambda b, i: (0, 0)),
                  pl.BlockSpec((tm, LANES), lambda b, i: (i, 0)),
                  pl.BlockSpec((tm, LANES), lambda b, i: (i, 0))],
        out_specs=(blk(4 * HG_WIDTH), blk(NSA_HEADS * LANES), blk(NSA_HEADS * LANES),
                   blk(LANES), blk(LANES), blk(2 * LANES), blk(LANES), blk(4 * LANES), blk(NSA_KV_GROUPS * LANES)),
        scratch_shapes=[pltpu.VMEM((tm, D), BF16)],
        compiler_params=_cparams(("arbitrary", "arbitrary")),
        name="inproj",
    )(x, mod3, mod3, norm_g, w_p, cos, sin)


def _hgrn_kernel(q_ref, f_ref, i_ref, gt_ref, lb_ref, ng_ref, o_ref):
    S = q_ref.shape[1]
    C, U = HG_CHUNK, HG_SUB
    lb = lb_ref[...]
    ng = ng_ref[...]
    ri = lax.broadcasted_iota(jnp.int32, (C, C), 0)
    ci = lax.broadcasted_iota(jnp.int32, (C, C), 1)
    tril = (ri >= ci).astype(F32)
    trow = lax.broadcasted_iota(jnp.int32, (U, 1), 0)

    def chunk(c, st_t):
        r0 = pl.multiple_of(c * C, C)
        rows = pl.ds(r0, C)
        f = lb + (1.0 - lb) * _sigmoid(f_ref[0, rows, :])
        kk = 1.0 - f
        b = jnp.dot(tril, jnp.log(f), precision=HIGHEST, preferred_element_type=F32)
        q = q_ref[0, rows, :] * (HG_DK ** -0.5)
        v = i_ref[0, rows, :]
        vb = v.astype(BF16)
        o_inter = lax.dot_general((q * jnp.exp(b)).astype(BF16), st_t.astype(BF16), NT,
                                  preferred_element_type=F32)
        parts = []
        for i in range(C // U):
            lo = i * U
            bi = b[lo:lo + U]
            qi = q[lo:lo + U]
            if i == 0:
                oi = jnp.zeros((U, HG_DV), F32)
            else:
                r = b[lo - 1:lo]
                qrel = (qi * jnp.exp(bi - r)).astype(BF16)
                kprev = (kk[:lo] * jnp.exp(r - b[:lo])).astype(BF16)
                a_off = lax.dot_general(qrel, kprev, NT, preferred_element_type=F32)
                oi = jnp.dot(a_off.astype(BF16), vb[:lo], preferred_element_type=F32)
            for s in range(U):
                valid = trow >= s
                e = jnp.exp(jnp.where(valid, bi - bi[s:s + 1], 0.0))
                a = jnp.sum(qi * e * kk[lo + s:lo + s + 1], axis=-1, keepdims=True)
                oi = oi + jnp.where(valid, a, 0.0) * v[lo + s:lo + s + 1]
            parts.append(oi)
        o = o_inter + jnp.concatenate(parts, axis=0)
        o = o * lax.rsqrt(jnp.mean(o * o, axis=-1, keepdims=True) + EPS) * ng
        o_ref[0, rows, :] = (o * _silu(gt_ref[0, rows, :])).astype(BF16)
        bl = b[C - 1:C]
        kv_t = lax.dot_general(vb, (kk * jnp.exp(bl - b)).astype(BF16), TN, preferred_element_type=F32)
        return jnp.exp(bl) * st_t + kv_t

    lax.fori_loop(0, S // C, chunk, jnp.zeros((HG_DV, HG_DK), F32))


def _cumsum_rows(x):
    n = x.shape[0]
    row = lax.broadcasted_iota(jnp.int32, x.shape, 0)
    d = 1
    while d < n:
        x = x + jnp.where(row >= d, pltpu.roll(x, d, 0), 0.0)
        d *= 2
    return x


def _hgrn_mxu_kernel(hg_ref, lb_ref, ng_ref, o_ref, st_scr):
    ts = hg_ref.shape[1]
    C, U = HG_CHUNK, HG_SUB

    @pl.when(pl.program_id(1) == 0)
    def _():
        st_scr[...] = jnp.zeros(st_scr.shape, F32)

    W = HG_WIDTH
    NH = HG_HEADS
    head_of_lane = lax.broadcasted_iota(jnp.int32, (1, W), 1) // HG_DK
    hcols = [slice(h * HG_DK, (h + 1) * HG_DK) for h in range(NH)]

    def chunk(c, carry):
        rows = pl.ds(pl.multiple_of(c * C, C), C)
        lb = lb_ref[...]
        f = lb + (1.0 - lb) * _sigmoid(hg_ref[0, rows, W:2 * W])
        kk = 1.0 - f
        b = _cumsum_rows(jnp.log(f))
        q = hg_ref[0, rows, 0:W] * (HG_DK ** -0.5)
        vb = hg_ref[0, rows, 2 * W:3 * W].astype(BF16)
        qe = (q * jnp.exp(b)).astype(BF16)
        o_inter = jnp.concatenate(
            [lax.dot_general(qe[:, hc], st_scr[h].astype(BF16), NT, preferred_element_type=F32)
             for h, hc in enumerate(hcols)], axis=1)
        parts = []
        for i in range(C // U):
            lo, hi = i * U, (i + 1) * U
            r = b[lo - 1:lo] if i else jnp.zeros((1, W), F32)
            qrel = q[lo:hi] * jnp.exp(b[lo:hi] - r)
            kall = (kk[:hi] * jnp.exp(r - b[:hi])).astype(BF16)
            qbd = jnp.concatenate([jnp.where(head_of_lane == h, qrel, 0.0) for h in range(NH)], axis=0)
            a = lax.dot_general(qbd.astype(BF16), kall, NT, preferred_element_type=F32)
            trow = lax.broadcasted_iota(jnp.int32, (NH * U, hi), 0) % U
            a = jnp.where(lax.broadcasted_iota(jnp.int32, (NH * U, hi), 1) <= lo + trow, a, 0.0)
            oa = jnp.dot(a.astype(BF16), vb[:hi], preferred_element_type=F32)
            oi = jnp.where(head_of_lane == 0, oa[0:U], 0.0)
            for h in range(1, NH):
                oi = jnp.where(head_of_lane == h, oa[h * U:(h + 1) * U], oi)
            parts.append(oi)
        o = o_inter + jnp.concatenate(parts, axis=0)
        o = jnp.concatenate(
            [o[:, hc] * lax.rsqrt(jnp.mean(o[:, hc] * o[:, hc], axis=-1, keepdims=True) + EPS) for hc in hcols], axis=1)
        o_ref[0, rows, :] = (o * ng_ref[...] * _silu(hg_ref[0, rows, 3 * W:4 * W])).astype(BF16)
        bl = b[C - 1:C]
        ke = (kk * jnp.exp(bl - b)).astype(BF16)
        decay = jnp.exp(bl)
        for h, hc in enumerate(hcols):
            kv_t = lax.dot_general(vb[:, hc], ke[:, hc], TN, preferred_element_type=F32)
            st_scr[h] = decay[:, hc] * st_scr[h] + kv_t
        return carry

    lax.fori_loop(0, ts // C, chunk, 0, unroll=8)


def _hgrn_mxu(hg, lb, ng, ts):
    B, S, _ = hg.shape
    vec = pl.BlockSpec((1, HG_WIDTH), lambda b, i: (0, 0))
    return pl.pallas_call(
        _hgrn_mxu_kernel,
        out_shape=jax.ShapeDtypeStruct((B, S, HG_WIDTH), BF16),
        grid=(B, S // ts),
        in_specs=[pl.BlockSpec((1, ts, 4 * HG_WIDTH), lambda b, i: (b, i, 0)), vec, vec],
        out_specs=pl.BlockSpec((1, ts, HG_WIDTH), lambda b, i: (b, i, 0)),
        scratch_shapes=[pltpu.VMEM((HG_HEADS, HG_DV, HG_DK), F32)],
        compiler_params=_cparams(("arbitrary", "arbitrary")),
        name="hgrn_mxu",
    )(hg, lb, ng)


def _hgrn(hg, lb, ng):
    B, S, _ = hg.shape
    col = lambda k: pl.BlockSpec((1, S, HG_DK), lambda b, h, k=k: (b, 0, k * HG_HEADS + h))
    vec = pl.BlockSpec((1, HG_DK), lambda b, h: (0, h))
    return pl.pallas_call(
        _hgrn_kernel,
        out_shape=jax.ShapeDtypeStruct((B, S, HG_WIDTH), BF16),
        grid=(B, HG_HEADS),
        in_specs=[col(0), col(1), col(2), col(3), vec, vec],
        out_specs=pl.BlockSpec((1, S, HG_DV), lambda b, h: (b, 0, h)),
        compiler_params=_cparams(("arbitrary", "arbitrary")),
        name="hgrn",
    )(hg, hg, hg, hg, lb, ng)


def _cmpmlp_kernel(x_ref, w1a_ref, w1b_ref, pe_ref, w1_ref, b1_ref, w2_ref, o_ref):
    x = x_ref[0]
    hb = jnp.dot(pe_ref[...], w1_ref[...], precision=HIGHEST, preferred_element_type=F32)[0:1] + b1_ref[...]
    nrow = x.shape[0]
    for g in range(NSA_KV_GROUPS):
        a = jnp.dot(x, w1a_ref[g], preferred_element_type=F32)
        bm = jnp.dot(x, w1b_ref[g], preferred_element_type=F32)
        hdn = a + pltpu.roll(bm, nrow - 1, 0) + hb
        o_ref[0, g] = jnp.dot(_gelu_tanh(hdn).astype(BF16), w2_ref[g], preferred_element_type=F32).astype(BF16)


def _cmpmlp(x2, w1a, w1b, pe8, w1, b1, w2p):
    B, ncb, width = x2.shape
    full = lambda a: pl.BlockSpec(a.shape, lambda b: (0,) * a.ndim)
    return pl.pallas_call(
        _cmpmlp_kernel,
        out_shape=jax.ShapeDtypeStruct((B, NSA_KV_GROUPS, ncb, LANES), BF16),
        grid=(B,),
        in_specs=[pl.BlockSpec((1, ncb, width), lambda b: (b, 0, 0)),
                  full(w1a), full(w1b), full(pe8), full(w1), full(b1), full(w2p)],
        out_specs=pl.BlockSpec((1, NSA_KV_GROUPS, ncb, LANES), lambda b: (b, 0, 0, 0)),
        compiler_params=_cparams(("arbitrary",)),
        name="cmpmlp",
    )(x2, w1a, w1b, pe8, w1, b1, w2p)


def _rank_before(score, nrows):
    jrow = lax.broadcasted_iota(jnp.int32, score.shape, 0)
    rank = jnp.zeros(score.shape, F32)
    for k in range(nrows):
        rk = score[k:k + 1]
        beats = (rk > score) | ((rk == score) & (jrow > k))
        rank = rank + jnp.where(beats, 1.0, 0.0)
    return rank


def _topk_rows(score, k):
    n = score.shape[0]
    row = lax.broadcasted_iota(jnp.int32, score.shape, 0).astype(F32)
    keep = jnp.zeros(score.shape, F32)
    for _ in range(k):
        top = jnp.max(score, axis=0, keepdims=True)
        first = jnp.min(jnp.where(score == top, row, float(n)), axis=0, keepdims=True)
        pick = row == first
        keep = jnp.where(pick, 1.0, keep)
        score = jnp.where(pick, -jnp.inf, score)
    return keep


def _transpose_small_ints(xt):
    c = xt.shape[1]
    eye = (lax.broadcasted_iota(jnp.int32, (c, c), 0) == lax.broadcasted_iota(jnp.int32, (c, c), 1)).astype(BF16)
    return lax.dot_general(eye, xt.astype(BF16), NT, preferred_element_type=F32)


def _pack_heads(heads):
    low = lax.broadcasted_iota(jnp.int32, heads[0].shape, 1) < NSA_HD
    return jnp.concatenate([jnp.where(low, heads[k], pltpu.roll(heads[k + 1], NSA_HD, 1))
                            for k in range(0, len(heads), 2)], axis=1)


def _cmpsel_kernel(q_ref, kc_ref, vc_ref, gate_ref, o_ref, sel_ref):
    tq = q_ref.shape[1]
    ncb = kc_ref.shape[2]
    t = pl.program_id(2)
    kc = kc_ref[0, 0]
    vc = vc_ref[0, 0]
    pos = t * tq + lax.broadcasted_iota(jnp.int32, (tq, 1), 0)
    cblk = lax.broadcasted_iota(jnp.int32, (1, ncb), 1)
    vis = (cblk * CMP_STRIDE + CMP_BLOCK - 1) <= pos
    psum = jnp.zeros((tq, ncb), F32)
    gate = gate_ref[0]
    heads = []
    for h in range(NSA_HG):
        s = lax.dot_general(q_ref[0, :, h * LANES:(h + 1) * LANES], kc, NT, preferred_element_type=F32)
        s = jnp.where(vis, s, NEG)
        p = jnp.exp(s - jnp.max(s, axis=-1, keepdims=True))
        p = jnp.where(vis, p / jnp.sum(p, axis=-1, keepdims=True), 0.0)
        heads.append(jnp.dot(p.astype(BF16), vc, preferred_element_type=F32) * gate[:, 3 * h:3 * h + 1])
        psum = psum + p
    o_ref[0] = _pack_heads(heads)
    nsb = ncb * CMP_STRIDE // SEL_BLOCK
    jb = lax.broadcasted_iota(jnp.int32, (nsb, ncb), 0) * SEL_BLOCK
    cb = lax.broadcasted_iota(jnp.int32, (nsb, ncb), 1) * CMP_STRIDE
    ov = jnp.maximum(jnp.minimum(cb + CMP_BLOCK, jb + SEL_BLOCK) - jnp.maximum(cb, jb), 0).astype(F32) / CMP_BLOCK
    ps_hi = psum.astype(BF16)
    ps_lo = (psum - ps_hi.astype(F32)).astype(BF16)
    ovb = ov.astype(BF16)
    pslc_t = (lax.dot_general(ovb, ps_hi, NT, preferred_element_type=F32)
              + lax.dot_general(ovb, ps_lo, NT, preferred_element_type=F32))
    posl = t * tq + lax.broadcasted_iota(jnp.int32, (1, tq), 1)
    cur = posl // SEL_BLOCK
    jrow = lax.broadcasted_iota(jnp.int32, (nsb, tq), 0)
    forced = (jrow == 0) | (jrow == cur) | (jrow == cur - 1)
    score = jnp.where(forced, 1e30, jnp.where(jrow <= cur, pslc_t, NEG))
    chosen = _topk_rows(score, min(N_SEL, nsb)) > 0.5
    drop = jnp.where(chosen & (score > -1e29), 0.0, 1.0)
    drop = jnp.concatenate([drop, jnp.zeros((LANES - nsb, tq), F32)], axis=0)
    sel_ref[0, 0] = (_transpose_small_ints(drop) * NEG).astype(BF16)


def _cmpsel(qraw, kcmp, vcmp, gate, tq):
    B, S, _ = qraw.shape
    ncb = kcmp.shape[2]
    gw = NSA_HG * LANES
    assert S // SEL_BLOCK <= NSA_HD
    cmp_spec = pl.BlockSpec((1, 1, ncb, LANES), lambda b, g, t: (b, g, 0, 0))
    return pl.pallas_call(
        _cmpsel_kernel,
        out_shape=(jax.ShapeDtypeStruct((B, S, NSA_WIDTH), F32),
                   jax.ShapeDtypeStruct((B, NSA_KV_GROUPS, S, LANES), BF16)),
        grid=(B, NSA_KV_GROUPS, S // tq),
        in_specs=[pl.BlockSpec((1, tq, gw), lambda b, g, t: (b, t, g)), cmp_spec, cmp_spec,
                  pl.BlockSpec((1, tq, LANES), lambda b, g, t: (b, t, g))],
        out_specs=(pl.BlockSpec((1, tq, NSA_WIDTH // NSA_KV_GROUPS), lambda b, g, t: (b, t, g)),
                   pl.BlockSpec((1, 1, tq, LANES), lambda b, g, t: (b, g, t, 0))),
        compiler_params=_cparams(("arbitrary", "arbitrary", "arbitrary")),
        name="cmpsel",
    )(qraw, kcmp, vcmp, gate)


def _slcwin_kernel(q_ref, sel_ref, ks_ref, kw_ref, vs_ref, vw_ref, gate_ref, ocmp_ref, o_ref, *, tk, wk):
    tq = q_ref.shape[1]
    S = ks_ref.shape[1]
    t = pl.program_id(2)
    q0 = t * tq
    qpos = q0 + lax.broadcasted_iota(jnp.int32, (tq, 1), 0)
    qw = [q_ref[0, :, h * LANES:(h + 1) * LANES] for h in range(NSA_HG)]
    sel = sel_ref[0, 0]
    qa = [jnp.concatenate([q, sel], axis=1) for q in qw]

    gate = gate_ref[0]

    def normalise(acc, g):
        return acc * (g / acc[:, NSA_HD:NSA_HD + 1])

    nchain = NSA_HG // SLC_STACK
    qst = [jnp.concatenate(qa[c * SLC_STACK:(c + 1) * SLC_STACK], axis=0) for c in range(nchain)]
    qpos_st = jnp.concatenate([qpos] * SLC_STACK, axis=0)

    def kstep(j, carry, diag):
        k0 = pl.multiple_of(j * tk, tk)
        ks = ks_ref[0, pl.ds(k0, tk), :]
        vs = vs_ref[0, pl.ds(k0, tk), :]
        if diag:
            vis = (k0 + lax.broadcasted_iota(jnp.int32, (1, tk), 1)) <= qpos_st
        out = []
        for c in range(nchain):
            m, acc = carry[c]
            s = lax.dot_general(qst[c], ks, NT, preferred_element_type=F32)
            if diag:
                s = jnp.where(vis, s, NEG)
            mn = jnp.maximum(m, jnp.max(s, axis=-1, keepdims=True))
            p = jnp.exp2(s - mn)
            acc = jnp.exp2(m - mn) * acc + jnp.dot(p.astype(BF16), vs, preferred_element_type=F32)
            out.append((mn, acc))
        return tuple(out)

    init = tuple((jnp.full((SLC_STACK * tq, 1), NEG, F32), jnp.zeros((SLC_STACK * tq, LANES), F32))
                 for _ in range(nchain))
    jdiag = q0 // tk
    carry = lax.fori_loop(0, jdiag, lambda j, c: kstep(j, c, False), init)
    carry = kstep(jdiag, carry, True)
    heads = [normalise(carry[h // SLC_STACK][1][(h % SLC_STACK) * tq:(h % SLC_STACK + 1) * tq],
                       gate[:, 3 * h + 1:3 * h + 2]) for h in range(NSA_HG)]

    wq = wk - WINDOW
    wins = [[] for _ in range(NSA_HG)]
    spos = lax.broadcasted_iota(jnp.int32, (NSA_HG * wq, 1), 0) % wq
    for i in range(tq // wq):
        rows = slice(i * wq, (i + 1) * wq)
        start = pl.multiple_of(jnp.maximum(q0 + (i + 1) * wq - wk, 0), wq)
        kw = kw_ref[0, pl.ds(start, wk), :]
        vw = vw_ref[0, pl.ds(start, wk), :]
        d = (q0 + i * wq + spos) - (start + lax.broadcasted_iota(jnp.int32, (1, wk), 1))
        q_st = jnp.concatenate([q[rows] for q in qw], axis=0)
        s = jnp.where((d >= 0) & (d < WINDOW), lax.dot_general(q_st, kw, NT, preferred_element_type=F32), NEG)
        p = jnp.exp2(s - jnp.max(s, axis=-1, keepdims=True))
        o_st = jnp.dot(p.astype(BF16), vw, preferred_element_type=F32)
        for h in range(NSA_HG):
            wins[h].append(o_st[h * wq:(h + 1) * wq])
    for h in range(NSA_HG):
        heads[h] = heads[h] + normalise(jnp.concatenate(wins[h], axis=0), gate[:, 3 * h + 2:3 * h + 3])
    o_ref[0] = (ocmp_ref[0] + _pack_heads(heads)).astype(BF16)


def _slcwin(qrot, sel, ks, kw, vaug, gate, ocmp, tq, tk):
    B, S, _ = qrot.shape
    gw = NSA_HG * LANES
    wk = WINDOW + WIN_Q
    assert S >= wk and tq % WIN_Q == 0
    ospec = pl.BlockSpec((1, tq, NSA_WIDTH // NSA_KV_GROUPS), lambda b, g, t: (b, t, g))
    return pl.pallas_call(
        functools.partial(_slcwin_kernel, tk=tk, wk=wk),
        out_shape=jax.ShapeDtypeStruct((B, S, NSA_WIDTH), BF16),
        grid=(B, NSA_KV_GROUPS, S // tq),
        in_specs=[pl.BlockSpec((1, tq, gw), lambda b, g, t: (b, t, g)),
                  pl.BlockSpec((1, 1, tq, LANES), lambda b, g, t: (b, g, t, 0)),
                  pl.BlockSpec((1, S, 2 * LANES), lambda b, g, t: (b, 0, 0)),
                  pl.BlockSpec((1, S, LANES), lambda b, g, t: (b, 0, 0)),
                  pl.BlockSpec((1, S, LANES), lambda b, g, t: (b, 0, g)),
                  pl.BlockSpec((1, S, LANES), lambda b, g, t: (b, 0, NSA_KV_GROUPS + g)),
                  pl.BlockSpec((1, tq, LANES), lambda b, g, t: (b, t, g)),
                  ospec],
        out_specs=ospec,
        compiler_params=_cparams(("arbitrary", "arbitrary", "arbitrary")),
        name="slcwin",
    )(qrot, sel, ks, kw, vaug, vaug, gate, ocmp)


def _outproj_kernel(x_ref, ohg_ref, onsa_ref, wout_ref,
                    g1_ref, sh2_ref, sc2_ref, n2_ref, rwt_ref, rb_ref,
                    x1_ref, h2_ref, sc_ref, slot_t_ref, w_t_ref, cnt_ref):
    tm = x_ref.shape[1]
    mix = jnp.dot(jnp.concatenate([ohg_ref[0], onsa_ref[0]], axis=1), wout_ref[...], preferred_element_type=F32)
    x1 = x_ref[0] + g1_ref[0] * mix
    x1_ref[0] = x1
    h2 = (x1 * lax.rsqrt(jnp.mean(x1 * x1, axis=-1, keepdims=True) + EPS) * n2_ref[...]) * (1.0 + sc2_ref[0]) + sh2_ref[0]
    h_hi = h2.astype(BF16)
    h2_ref[0] = h_hi
    h_lo = (h2 - h_hi.astype(F32)).astype(BF16)
    both = lax.dot_general(rwt_ref[...], h_hi, NT, preferred_element_type=F32)
    logits = (both[:N_EXPERTS] + both[N_EXPERTS:]
              + lax.dot_general(rwt_ref[:N_EXPERTS, :], h_lo, NT, preferred_element_type=F32))
    scores = _sigmoid(logits)
    choice = scores + rb_ref[...]
    per = N_EXPERTS // N_GROUPS
    c3 = choice.reshape(N_GROUPS, per, tm)
    erow = lax.broadcasted_iota(jnp.int32, c3.shape, 1)
    rank_in = jnp.zeros(c3.shape, F32)
    for k in range(per):
        ck = c3[:, k:k + 1, :]
        rank_in = rank_in + jnp.where((ck > c3) | ((ck == c3) & (erow > k)), 1.0, 0.0)
    grp_score = jnp.sum(jnp.where(rank_in < 2, c3, 0.0), axis=1)
    grp_keep = _rank_before(grp_score, N_GROUPS) < TOPK_GROUPS
    masked = jnp.where(grp_keep[:, None, :], c3, -jnp.inf).reshape(N_EXPERTS, tm)
    keep = _topk_rows(masked, TOP_K) > 0.5
    tw = jnp.where(keep, scores, 0.0)
    tw = tw / jnp.sum(tw, axis=0, keepdims=True) * ROUTED_SCALE
    before = lax.broadcasted_iota(jnp.int32, (tm, tm), 0)
    after = lax.broadcasted_iota(jnp.int32, (tm, tm), 1)
    earlier = (before < after) & (before // MOE_SUB == after // MOE_SUB)
    kept = jnp.where(keep, 1.0, 0.0)
    pref = jnp.dot(kept.astype(BF16), earlier.astype(BF16), preferred_element_type=F32)
    slot = jnp.where(keep, pref, -1.0)
    assert MOE_SUB <= 256
    sc_ref[0] = _transpose_small_ints(slot)
    for j in range(tm // MOE_SUB):
        sub = slice(j * MOE_SUB, (j + 1) * MOE_SUB)
        slot_t_ref[:, j] = slot[:, sub].reshape(N_EXPERTS // MOE_EG, MOE_EG, MOE_SUB)
        w_t_ref[:, j] = tw[:, sub].reshape(N_EXPERTS // MOE_EG, MOE_EG, MOE_SUB)
        cnt_ref[j] = jnp.broadcast_to(jnp.sum(kept[:, sub], axis=1, keepdims=True), (N_EXPERTS, LANES))


def _outproj(x, ohg, onsa, wout, mod3, n2, rwt, rb, tm):
    B, S, D = x.shape
    blk = lambda w: pl.BlockSpec((1, tm, w), lambda b, i: (b, i, 0))
    full = lambda a: pl.BlockSpec(a.shape, lambda b, i: (0,) * a.ndim)
    modc = lambda k: pl.BlockSpec((1, 1, D), lambda b, i, k=k: (b, 0, k))
    nt = S // tm
    ng = N_EXPERTS // MOE_EG
    nh = tm // MOE_SUB
    tspec = pl.BlockSpec((ng, nh, MOE_EG, MOE_SUB), lambda b, i: (0, b * nt + i, 0, 0))
    return pl.pallas_call(
        _outproj_kernel,
        out_shape=(jax.ShapeDtypeStruct((B, S, D), F32),
                   jax.ShapeDtypeStruct((B, S, D), BF16),
                   jax.ShapeDtypeStruct((B, S, N_EXPERTS), F32),
                   jax.ShapeDtypeStruct((ng, B * nt * nh, MOE_EG, MOE_SUB), F32),
                   jax.ShapeDtypeStruct((ng, B * nt * nh, MOE_EG, MOE_SUB), F32),
                   jax.ShapeDtypeStruct((B * nt * nh, N_EXPERTS, LANES), F32)),
        grid=(B, nt),
        in_specs=[blk(D), blk(HG_WIDTH), blk(NSA_WIDTH), full(wout),
                  modc(2), modc(3), modc(4), full(n2), full(rwt), full(rb)],
        out_specs=(blk(D), blk(D), blk(N_EXPERTS), tspec, tspec,
                   pl.BlockSpec((nh, N_EXPERTS, LANES), lambda b, i: (b * nt + i, 0, 0))),
        compiler_params=_cparams(("arbitrary", "arbitrary")),
        name="outproj",
    )(x, ohg, onsa, wout, mod3, mod3, mod3, n2, rwt, rb)


def _swiglu_hidden(x, wgu):
    gu = jnp.dot(x, wgu, preferred_element_type=F32)
    return _silu(gu[:, :EXPERT_HIDDEN]) * gu[:, EXPERT_HIDDEN:]


def _moe_kernel(rounds_ref, h_ref, slot_t_ref, w_t_ref, slot_c_ref, wgu_ref, wdn_ref, sgu_ref, sdn_ref,
                o_ref, x_scr, y_scr, p_scr):
    i = pl.program_id(0)
    g = pl.program_id(1)
    tm = h_ref.shape[0]
    nsub = tm // MOE_SUB
    rnd, eg = MOE_ROUND, MOE_EG
    base = i * 2 * N_EXPERTS

    @pl.when(g == 0)
    def _():
        act = _swiglu_hidden(h_ref[...], sgu_ref[...]).astype(BF16)
        o_ref[...] = jnp.dot(act, sdn_ref[...], preferred_element_type=F32)

    rslot = lax.broadcasted_iota(jnp.int32, (rnd, 1), 0).astype(F32)
    erow = lax.broadcasted_iota(jnp.int32, (N_EXPERTS, rnd), 0)

    def serve(first, experts, conditions):
        width = len(experts) * rnd
        for s in range(nsub):
            p = jnp.concatenate([jnp.where(rslot == slot_t_ref[0, s, pl.ds(e, 1), :] - first, 1.0, 0.0)
                                 for e in experts], axis=0)
            p_scr[s, 0:width, :] = p.astype(BF16)
            x_scr[s, 0:width, :] = jnp.dot(p_scr[s, 0:width, :], h_ref[s * MOE_SUB:(s + 1) * MOE_SUB, :],
                                           preferred_element_type=F32).astype(BF16)
        for k, e in enumerate(experts):
            rows = slice(k * rnd, (k + 1) * rnd)

            def run_expert(e=e, rows=rows):
                xe = jnp.concatenate([x_scr[s, rows, :] for s in range(nsub)], axis=0)
                wslot = jnp.concatenate(
                    [jnp.sum(p_scr[s, rows, :].astype(F32) * w_t_ref[0, s, pl.ds(e, 1), :], axis=-1, keepdims=True)
                     for s in range(nsub)], axis=0)
                y = jnp.dot((_swiglu_hidden(xe, wgu_ref[e]) * wslot).astype(BF16), wdn_ref[e],
                            preferred_element_type=F32)
                for s in range(nsub):
                    y_scr[s, rows, :] = y[s * rnd:(s + 1) * rnd].astype(BF16)

            def idle(rows=rows):
                for s in range(nsub):
                    y_scr[s, rows, :] = jnp.zeros((rnd, D_MODEL), BF16)

            if conditions[k] is None:
                run_expert()
            else:
                pl.when(conditions[k])(run_expert)
                pl.when(jnp.logical_not(conditions[k]))(idle)
        lane_slot = (lax.broadcasted_iota(jnp.int32, (1, width), 1) % rnd).astype(F32)
        spread = jnp.concatenate([jnp.where(erow == g * eg + e, 1.0, 0.0) for e in experts], axis=1).astype(BF16)
        for s in range(nsub):
            toks = slice(s * MOE_SUB, (s + 1) * MOE_SUB)
            sc = jnp.dot(slot_c_ref[toks, :].astype(BF16), spread, preferred_element_type=F32) - first
            pt = jnp.where(sc == lane_slot, 1.0, 0.0).astype(BF16)
            o_ref[toks, :] += jnp.dot(pt, y_scr[s, 0:width, :], preferred_element_type=F32)

    serve(0.0, list(range(eg)), [None] * eg)

    order = [rounds_ref[base + N_EXPERTS + g * eg + k] for k in range(eg)]
    need = [rounds_ref[base + g * eg + k] for k in range(eg)]

    def later_round(r, carry):
        first = (r * rnd).astype(F32)
        active = sum((n > r).astype(jnp.int32) for n in need)
        for lo, width in ((0, 2), (2, 4), (4, eg)):
            @pl.when((active > lo) & (active <= width))
            def _(width=width):
                serve(first, order[:width], [need[k] > r for k in range(width)])
        return carry

    lax.fori_loop(1, need[0], later_round, 0)


def _moe(rounds, h2, slot_t, w_t, slot_c, wgu, wdn, sgu, sdn, tm):
    T, D = h2.shape
    eg, nsub = MOE_EG, tm // MOE_SUB
    full = lambda a: pl.BlockSpec(a.shape, lambda i, e, o: (0,) * a.ndim)
    tspec = pl.BlockSpec((1, nsub, eg, MOE_SUB), lambda i, e, o: (e, i, 0, 0))
    return pl.pallas_call(
        _moe_kernel,
        out_shape=jax.ShapeDtypeStruct((T, D), F32),
        grid_spec=pltpu.PrefetchScalarGridSpec(
            num_scalar_prefetch=1,
            grid=(T // tm, N_EXPERTS // eg),
            in_specs=[pl.BlockSpec((tm, D), lambda i, e, o: (i, 0), pipeline_mode=pl.Buffered(1)), tspec, tspec,
                      pl.BlockSpec((tm, N_EXPERTS), lambda i, e, o: (i, 0)),
                      pl.BlockSpec((eg, D, 2 * EXPERT_HIDDEN), lambda i, e, o: (e, 0, 0)),
                      pl.BlockSpec((eg, EXPERT_HIDDEN, D), lambda i, e, o: (e, 0, 0)),
                      full(sgu), full(sdn)],
            out_specs=pl.BlockSpec((tm, D), lambda i, e, o: (i, 0), pipeline_mode=pl.Buffered(1)),
            scratch_shapes=[pltpu.VMEM((nsub, eg * MOE_ROUND, D), BF16),
                            pltpu.VMEM((nsub, eg * MOE_ROUND, D), BF16),
                            pltpu.VMEM((nsub, eg * MOE_ROUND, MOE_SUB), BF16)]),
        compiler_params=_cparams(("arbitrary", "arbitrary")),
        name="moe",
    )(rounds, h2, slot_t, w_t, slot_c, wgu, wdn, sgu, sdn)


def _final_kernel(x1_ref, moe_ref, g2_ref, fg_ref, o_ref):
    x2 = x1_ref[0] + g2_ref[0] * moe_ref[0]
    o_ref[0] = x2 * lax.rsqrt(jnp.mean(x2 * x2, axis=-1, keepdims=True) + EPS) * fg_ref[...]


def _final(x1, moe, mod3, fg, tm):
    B, S, D = x1.shape
    blk = pl.BlockSpec((1, tm, D), lambda b, i: (b, i, 0))
    return pl.pallas_call(
        _final_kernel,
        out_shape=jax.ShapeDtypeStruct((B, S, D), F32),
        grid=(B, S // tm),
        in_specs=[blk, blk, pl.BlockSpec((1, 1, D), lambda b, i: (b, 0, 5)), pl.BlockSpec((1, D), lambda b, i: (0, 0))],
        out_specs=blk,
        compiler_params=_cparams(("arbitrary", "arbitrary")),
        name="final",
    )(x1, moe, mod3, fg)


def _split_bf16(w):
    hi = w.astype(BF16)
    return jnp.concatenate([hi, (w - hi.astype(F32)).astype(BF16)], axis=0)


def _pack_w_in(w_in):
    return jnp.pad(w_in, ((0, 0), (0, IN_COLS_P - w_in.shape[1]))).astype(BF16)


def _pack_cmp(pos, w1, b1, w2, lane_by_group):
    half = CMP_STRIDE * NSA_HD
    def rows_for(wh):
        w3 = wh.reshape(CMP_STRIDE, NSA_HD, CMP_HIDDEN)
        z = jnp.zeros_like(w3)
        return jnp.stack([jnp.concatenate([w3, z], axis=1).reshape(CMP_STRIDE * LANES, CMP_HIDDEN),
                          jnp.concatenate([z, w3], axis=1).reshape(CMP_STRIDE * LANES, CMP_HIDDEN)])
    w1a = rows_for(w1[:half]).astype(BF16)
    w1b = rows_for(w1[half:]).astype(BF16)
    z2 = jnp.zeros_like(w2)
    w2_first = jnp.concatenate([w2, z2], axis=1)
    w2p = jnp.stack([w2_first, jnp.concatenate([z2, w2], axis=1) if lane_by_group else w2_first]).astype(BF16)
    pe8 = jnp.pad(pos.reshape(1, CMP_BLOCK * NSA_HD), ((0, 7), (0, 0)))
    return w1a, w1b, pe8, w1, b1.reshape(1, CMP_HIDDEN), w2p


def _rope_tables(S):
    half = NSA_HD // 2
    inv = ROPE_THETA ** (-jnp.arange(half, dtype=F32) / half)
    ang = jnp.arange(S, dtype=F32)[:, None] * inv[None, :]
    cos, sin = jnp.cos(ang), jnp.sin(ang)
    reps = LANES // NSA_HD
    return jnp.tile(jnp.concatenate([cos, cos], axis=1), (1, reps)), jnp.tile(jnp.concatenate([-sin, sin], axis=1), (1, reps))


def _tiles(S):
    return dict(inproj=min(512, S), hgrn=min(512, S), cmpsel=min(512, S), slc_q=512, slc_k=512, moe=min(2048, S), final=min(512, S))


def kernel(x, c, w_ada, b_ada, norm1_g, w_in, hg_lb_logits, hg_norm_g, cmp_pos_k, cmp_w1_k, cmp_b1_k, cmp_w2_k,
           cmp_pos_v, cmp_w1_v, cmp_b1_v, cmp_w2_v, w_out, norm2_g, router_w, router_bias, w_exp_gu, w_exp_dn,
           w_sh_gu, w_sh_dn, final_g):
    B, S, D = x.shape
    assert D == D_MODEL and w_ada.shape[0] == 1 and S % 512 == 0
    tl = _tiles(S)
    l = 0
    lb = jnp.cumsum(jax.nn.softmax(hg_lb_logits.astype(F32), axis=0), axis=0)[l].reshape(1, HG_WIDTH)
    c8 = jnp.pad(c, ((0, 8 - B), (0, 0)))
    mod3 = _ada(c8, w_ada[l], b_ada[l].reshape(1, -1))[:B].reshape(B, 1, 6 * D)
    cos, sin = _rope_tables(S)
    hg, qraw, qrot, kc, vc, ks, kw, vaug, gate = _inproj(x, mod3, norm1_g[l].reshape(1, D), _pack_w_in(w_in[l]),
                                                         cos, sin, tl["inproj"])
    ng = hg_norm_g[l].reshape(1, HG_WIDTH)
    mxu_safe = -HG_SUB * jnp.log(jnp.min(lb)) <= HG_SAFE_LOG_RANGE
    ohg = lax.cond(mxu_safe, lambda: _hgrn_mxu(hg, lb, ng, tl["hgrn"]), lambda: _hgrn(hg, lb, ng))
    ncb = S // CMP_STRIDE
    kcmp = _cmpmlp(kc.reshape(B, ncb, CMP_STRIDE * LANES),
                   *_pack_cmp(cmp_pos_k[l], cmp_w1_k[l], cmp_b1_k[l], cmp_w2_k[l], True))
    vcmp = _cmpmlp(vc.reshape(B, ncb, CMP_STRIDE * LANES),
                   *_pack_cmp(cmp_pos_v[l], cmp_w1_v[l], cmp_b1_v[l], cmp_w2_v[l], False))
    ocmp, sel = _cmpsel(qraw, kcmp, vcmp, gate, tl["cmpsel"])
    onsa = _slcwin(qrot, sel, ks, kw, vaug, gate, ocmp, tl["slc_q"], tl["slc_k"])
    x1, h2, sc, slot_t, w_t, cnt = _outproj(x, ohg, onsa, w_out[l].astype(BF16), mod3,
                                            norm2_g[l].reshape(1, D), _split_bf16(router_w[l].T),
                                            router_bias[l].reshape(N_EXPERTS, 1), ROUTE_TM)
    T = B * S
    tm = tl["moe"]
    ng = N_EXPERTS // MOE_EG
    slot_c = sc.reshape(T, N_EXPERTS)
    load = jnp.max(cnt[:, :, 0].reshape(T // tm, tm // MOE_SUB, N_EXPERTS), axis=1).astype(jnp.int32)
    per_expert = ((load + MOE_ROUND - 1) // MOE_ROUND).reshape(T // tm, ng, MOE_EG)
    order = jnp.argsort(-per_expert, axis=-1).astype(jnp.int32)
    rounds = jnp.concatenate([jnp.take_along_axis(per_expert, order, axis=-1).reshape(T // tm, N_EXPERTS),
                              order.reshape(T // tm, N_EXPERTS)], axis=1).reshape(-1)
    moe = _moe(rounds, h2.reshape(T, D), slot_t, w_t, slot_c,
               w_exp_gu[l].astype(BF16), w_exp_dn[l].astype(BF16), w_sh_gu[l].astype(BF16), w_sh_dn[l].astype(BF16), tm)
    return _final(x1, moe.reshape(B, S, D), mod3, final_g.reshape(1, D), tl["final"])
```

```python
import functools

import jax
import jax.numpy as jnp
from jax import lax
from jax.experimental import pallas as pl
from jax.experimental.pallas import tpu as pltpu

F32 = jnp.float32
BF16 = jnp.bfloat16
HIGHEST = lax.Precision.HIGHEST

D_MODEL = 1024
EPS = 1e-6
HG_HEADS = 4
HG_DK = 128
HG_DV = 128
HG_WIDTH = HG_HEADS * HG_DV
HG_CHUNK = 64
HG_SUB = 16
HG_SAFE_LOG_RANGE = 80.0
NSA_HEADS = 8
NSA_KV_GROUPS = 2
NSA_HG = NSA_HEADS // NSA_KV_GROUPS
NSA_HD = 64
NSA_WIDTH = NSA_HEADS * NSA_HD
NSA_KV = NSA_KV_GROUPS * NSA_HD
CMP_BLOCK = 32
CMP_STRIDE = 16
CMP_HIDDEN = 256
SEL_BLOCK = 64
N_SEL = 16
WINDOW = 512
ROPE_THETA = 10000.0
SLC_STACK = 4
WIN_Q = 256
N_EXPERTS = 64
TOP_K = 8
N_GROUPS = 8
TOPK_GROUPS = 4
EXPERT_HIDDEN = 256
ROUTED_SCALE = 2.5
ROUTE_TM = 512
MOE_SUB = 256
MOE_ROUND = 32
MOE_EG = 8

LANES = 128
NEG = -1e30
VMEM_LIMIT = 56 * 1024 * 1024

C_HG = 0
C_Q = 4 * HG_WIDTH
C_K = C_Q + NSA_WIDTH
C_GATE = C_K + 6 * NSA_KV
IN_COLS_P = C_GATE + LANES
LOG2E = 1.4426950408889634

NT = (((1,), (1,)), ((), ()))
TN = (((0,), (0,)), ((), ()))


def _cparams(sem):
    return pltpu.CompilerParams(dimension_semantics=sem, vmem_limit_bytes=VMEM_LIMIT)


def _sigmoid(x):
    return 1.0 / (1.0 + jnp.exp(-x))


def _silu(x):
    return x * _sigmoid(x)


def _gelu_tanh(x):
    return 0.5 * x * (1.0 + jnp.tanh(0.7978845608028654 * (x + 0.044715 * (x * x * x))))


def _ada_kernel(c_ref, w_ref, b_ref, o_ref):
    a = _silu(c_ref[...])
    o_ref[...] = jnp.dot(a, w_ref[...], precision=HIGHEST, preferred_element_type=F32) + b_ref[...]


def _ada(c8, w, b):
    n = w.shape[1]
    tn = 1024
    return pl.pallas_call(
        _ada_kernel,
        out_shape=jax.ShapeDtypeStruct((8, n), F32),
        grid=(n // tn,),
        in_specs=[pl.BlockSpec((8, D_MODEL), lambda j: (0, 0)),
                  pl.BlockSpec((D_MODEL, tn), lambda j: (0, j)),
                  pl.BlockSpec((1, tn), lambda j: (0, j))],
        out_specs=pl.BlockSpec((8, tn), lambda j: (0, j)),
        compiler_params=_cparams(("arbitrary",)),
        name="ada",
    )(c8, w, b)


def _rope(t, cos, sin_signed, first_half):
    rot = jnp.where(first_half, pltpu.roll(t, 96, 1), pltpu.roll(t, 32, 1))
    return t * cos + rot * sin_signed


def _inproj_kernel(x_ref, sh_ref, sc_ref, g_ref, w_ref, cos_ref, sin_ref,
                   hg_ref, qraw_ref, qrot_ref, kc_ref, vc_ref, ks_ref, kw_ref, va_ref, gate_ref, h_scr):
    tm = x_ref.shape[1]
    x = x_ref[0]
    y = x * lax.rsqrt(jnp.mean(x * x, axis=-1, keepdims=True) + EPS) * g_ref[...]
    h_scr[...] = (y * (1.0 + sc_ref[0]) + sh_ref[0]).astype(BF16)

    def mm(lo, width):
        return jnp.dot(h_scr[...], w_ref[:, lo:lo + width], preferred_element_type=F32)

    cos = cos_ref[...]
    sin = sin_ref[...]
    first_half = (lax.broadcasted_iota(jnp.int32, cos.shape, 1) % NSA_HD) < (NSA_HD // 2)
    lane = lax.broadcasted_iota(jnp.int32, (tm, LANES), 1)
    half_of_lane = lane // NSA_HD

    def own_lanes(t, src_half, dst_half):
        moved = t if src_half == dst_half else pltpu.roll(t, NSA_HD, 1)
        return jnp.where(half_of_lane == dst_half, moved, 0.0)

    for j in range(4):
        hg_ref[0, :, j * HG_WIDTH:(j + 1) * HG_WIDTH] = mm(C_HG + j * HG_WIDTH, HG_WIDTH)
    for m in range(NSA_WIDTH // (2 * LANES)):
        qq = mm(C_Q + m * 2 * LANES, 2 * LANES) * (NSA_HD ** -0.5)
        for c in range(2):
            q = qq[:, c * LANES:(c + 1) * LANES]
            qr = _rope(q, cos, sin, first_half) * LOG2E
            for half in range(2):
                n = (2 * m + c) * 2 + half
                g = n // NSA_HG
                qraw_ref[0, :, n * LANES:(n + 1) * LANES] = own_lanes(q, half, g).astype(BF16)
                qrot_ref[0, :, n * LANES:(n + 1) * LANES] = own_lanes(qr, half, g).astype(BF16)
    kv = mm(C_K, 2 * LANES)
    kc_ref[0] = kv[:, :LANES].astype(BF16)
    vc_ref[0] = kv[:, LANES:].astype(BF16)
    pos = pl.program_id(1) * tm + lax.broadcasted_iota(jnp.int32, (tm, LANES), 0)
    ks_ref[0, :, LANES:2 * LANES] = jnp.where(lane == pos // SEL_BLOCK, 1.0, 0.0).astype(BF16)
    for j in range(2):
        kv = mm(C_K + (j + 1) * 2 * LANES, 2 * LANES)
        k_rot = _rope(kv[:, :LANES], cos, sin, first_half).astype(BF16)
        if j == 0:
            ks_ref[0, :, 0:LANES] = k_rot
        else:
            kw_ref[0] = k_rot
        for g in range(NSA_KV_GROUPS):
            v = own_lanes(kv[:, LANES:], g, 0)
            col = (j * NSA_KV_GROUPS + g) * LANES
            va_ref[0, :, col:col + LANES] = jnp.where(lane == NSA_HD, 1.0, v).astype(BF16)
    gate = _sigmoid(mm(C_GATE, LANES))
    per_group = 3 * NSA_HG
    for g in range(NSA_KV_GROUPS):
        gate_ref[0, :, g * LANES:(g + 1) * LANES] = gate if g == 0 else pltpu.roll(gate, LANES - g * per_group, 1)


def _inproj(x, mod3, norm_g, w_p, cos, sin, tm):
    B, S, D = x.shape
    blk = lambda w: pl.BlockSpec((1, tm, w), lambda b, i: (b, i, 0))
    return pl.pallas_call(
        _inproj_kernel,
        out_shape=(jax.ShapeDtypeStruct((B, S, 4 * HG_WIDTH), F32),
                   jax.ShapeDtypeStruct((B, S, NSA_HEADS * LANES), BF16),
                   jax.ShapeDtypeStruct((B, S, NSA_HEADS * LANES), BF16),
                   jax.ShapeDtypeStruct((B, S, LANES), BF16),
                   jax.ShapeDtypeStruct((B, S, LANES), BF16),
                   jax.ShapeDtypeStruct((B, S, 2 * LANES), BF16),
                   jax.ShapeDtypeStruct((B, S, LANES), BF16),
                   jax.ShapeDtypeStruct((B, S, 4 * LANES), BF16),
                   jax.ShapeDtypeStruct((B, S, NSA_KV_GROUPS * LANES), F32)),
        grid=(B, S // tm),
        in_specs=[blk(D),
                  pl.BlockSpec((1, 1, D), lambda b, i: (b, 0, 0)),
                  pl.BlockSpec((1, 1, D), lambda b, i: (b, 0, 1)),
                  pl.BlockSpec((1, D), lambda b, i: (0, 0)),
                  pl.BlockSpec((D, IN_COLS_P), lambda b, i: (0, 0)),
                  pl.BlockSpec((tm, LANES), lambda b, i: (i, 0)),
                  pl.BlockSpec((tm, LANES), lambda b, i: (i, 0))],
        out_specs=(blk(4 * HG_WIDTH), blk(NSA_HEADS * LANES), blk(NSA_HEADS * LANES),
                   blk(LANES), blk(LANES), blk(2 * LANES), blk(LANES), blk(4 * LANES), blk(NSA_KV_GROUPS * LANES)),
        scratch_shapes=[pltpu.VMEM((tm, D), BF16)],
        compiler_params=_cparams(("arbitrary", "arbitrary")),
        name="inproj",
    )(x, mod3, mod3, norm_g, w_p, cos, sin)


def _hgrn_kernel(q_ref, f_ref, i_ref, gt_ref, lb_ref, ng_ref, o_ref):
    S = q_ref.shape[1]
    C, U = HG_CHUNK, HG_SUB
    lb = lb_ref[...]
    ng = ng_ref[...]
    ri = lax.broadcasted_iota(jnp.int32, (C, C), 0)
    ci = lax.broadcasted_iota(jnp.int32, (C, C), 1)
    tril = (ri >= ci).astype(F32)
    trow = lax.broadcasted_iota(jnp.int32, (U, 1), 0)

    def chunk(c, st_t):
        r0 = pl.multiple_of(c * C, C)
        rows = pl.ds(r0, C)
        f = lb + (1.0 - lb) * _sigmoid(f_ref[0, rows, :])
        kk = 1.0 - f
        b = jnp.dot(tril, jnp.log(f), precision=HIGHEST, preferred_element_type=F32)
        q = q_ref[0, rows, :] * (HG_DK ** -0.5)
        v = i_ref[0, rows, :]
        vb = v.astype(BF16)
        o_inter = lax.dot_general((q * jnp.exp(b)).astype(BF16), st_t.astype(BF16), NT,
                                  preferred_element_type=F32)
        parts = []
        for i in range(C // U):
            lo = i * U
            bi = b[lo:lo + U]
            qi = q[lo:lo + U]
            if i == 0:
                oi = jnp.zeros((U, HG_DV), F32)
            else:
                r = b[lo - 1:lo]
                qrel = (qi * jnp.exp(bi - r)).astype(BF16)
                kprev = (kk[:lo] * jnp.exp(r - b[:lo])).astype(BF16)
                a_off = lax.dot_general(qrel, kprev, NT, preferred_element_type=F32)
                oi = jnp.dot(a_off.astype(BF16), vb[:lo], preferred_element_type=F32)
            for s in range(U):
                valid = trow >= s
                e = jnp.exp(jnp.where(valid, bi - bi[s:s + 1], 0.0))
                a = jnp.sum(qi * e * kk[lo + s:lo + s + 1], axis=-1, keepdims=True)
                oi = oi + jnp.where(valid, a, 0.0) * v[lo + s:lo + s + 1]
            parts.append(oi)
        o = o_inter + jnp.concatenate(parts, axis=0)
        o = o * lax.rsqrt(jnp.mean(o * o, axis=-1, keepdims=True) + EPS) * ng
        o_ref[0, rows, :] = (o * _silu(gt_ref[0, rows, :])).astype(BF16)
        bl = b[C - 1:C]
        kv_t = lax.dot_general(vb, (kk * jnp.exp(bl - b)).astype(BF16), TN, preferred_element_type=F32)
        return jnp.exp(bl) * st_t + kv_t

    lax.fori_loop(0, S // C, chunk, jnp.zeros((HG_DV, HG_DK), F32))


def _cumsum_rows(x):
    n = x.shape[0]
    row = lax.broadcasted_iota(jnp.int32, x.shape, 0)
    d = 1
    while d < n:
        x = x + jnp.where(row >= d, pltpu.roll(x, d, 0), 0.0)
        d *= 2
    return x


def _hgrn_mxu_kernel(hg_ref, lb_ref, ng_ref, o_ref, st_scr):
    ts = hg_ref.shape[1]
    C, U = HG_CHUNK, HG_SUB

    @pl.when(pl.program_id(1) == 0)
    def _():
        st_scr[...] = jnp.zeros(st_scr.shape, F32)

    W = HG_WIDTH
    NH = HG_HEADS
    head_of_lane = lax.broadcasted_iota(jnp.int32, (1, W), 1) // HG_DK
    hcols = [slice(h * HG_DK, (h + 1) * HG_DK) for h in range(NH)]

    def chunk(c, carry):
        rows = pl.ds(pl.multiple_of(c * C, C), C)
        lb = lb_ref[...]
        f = lb + (1.0 - lb) * _sigmoid(hg_ref[0, rows, W:2 * W])
        kk = 1.0 - f
        b = _cumsum_rows(jnp.log(f))
        q = hg_ref[0, rows, 0:W] * (HG_DK ** -0.5)
        vb = hg_ref[0, rows, 2 * W:3 * W].astype(BF16)
        qe = (q * jnp.exp(b)).astype(BF16)
        o_inter = jnp.concatenate(
            [lax.dot_general(qe[:, hc], st_scr[h].astype(BF16), NT, preferred_element_type=F32)
             for h, hc in enumerate(hcols)], axis=1)
        parts = []
        for i in range(C // U):
            lo, hi = i * U, (i + 1) * U
            r = b[lo - 1:lo] if i else jnp.zeros((1, W), F32)
            qrel = q[lo:hi] * jnp.exp(b[lo:hi] - r)
            kall = (kk[:hi] * jnp.exp(r - b[:hi])).astype(BF16)
            qbd = jnp.concatenate([jnp.where(head_of_lane == h, qrel, 0.0) for h in range(NH)], axis=0)
            a = lax.dot_general(qbd.astype(BF16), kall, NT, preferred_element_type=F32)
            trow = lax.broadcasted_iota(jnp.int32, (NH * U, hi), 0) % U
            a = jnp.where(lax.broadcasted_iota(jnp.int32, (NH * U, hi), 1) <= lo + trow, a, 0.0)
            oa = jnp.dot(a.astype(BF16), vb[:hi], preferred_element_type=F32)
            oi = jnp.where(head_of_lane == 0, oa[0:U], 0.0)
            for h in range(1, NH):
                oi = jnp.where(head_of_lane == h, oa[h * U:(h + 1) * U], oi)
            parts.append(oi)
        o = o_inter + jnp.concatenate(parts, axis=0)
        o = jnp.concatenate(
            [o[:, hc] * lax.rsqrt(jnp.mean(o[:, hc] * o[:, hc], axis=-1, keepdims=True) + EPS) for hc in hcols], axis=1)
        o_ref[0, rows, :] = (o * ng_ref[...] * _silu(hg_ref[0, rows, 3 * W:4 * W])).astype(BF16)
        bl = b[C - 1:C]
        ke = (kk * jnp.exp(bl - b)).astype(BF16)
        decay = jnp.exp(bl)
        for h, hc in enumerate(hcols):
            kv_t = lax.dot_general(vb[:, hc], ke[:, hc], TN, preferred_element_type=F32)
            st_scr[h] = decay[:, hc] * st_scr[h] + kv_t
        return carry

    lax.fori_loop(0, ts // C, chunk, 0, unroll=8)


def _hgrn_mxu(hg, lb, ng, ts):
    B, S, _ = hg.shape
    vec = pl.BlockSpec((1, HG_WIDTH), lambda b, i: (0, 0))
    return pl.pallas_call(
        _hgrn_mxu_kernel,
        out_shape=jax.ShapeDtypeStruct((B, S, HG_WIDTH), BF16),
        grid=(B, S // ts),
        in_specs=[pl.BlockSpec((1, ts, 4 * HG_WIDTH), lambda b, i: (b, i, 0)), vec, vec],
        out_specs=pl.BlockSpec((1, ts, HG_WIDTH), lambda b, i: (b, i, 0)),
        scratch_shapes=[pltpu.VMEM((HG_HEADS, HG_DV, HG_DK), F32)],
        compiler_params=_cparams(("arbitrary", "arbitrary")),
        name="hgrn_mxu",
    )(hg, lb, ng)


def _hgrn(hg, lb, ng):
    B, S, _ = hg.shape
    col = lambda k: pl.BlockSpec((1, S, HG_DK), lambda b, h, k=k: (b, 0, k * HG_HEADS + h))
    vec = pl.BlockSpec((1, HG_DK), lambda b, h: (0, h))
    return pl.pallas_call(
        _hgrn_kernel,
        out_shape=jax.ShapeDtypeStruct((B, S, HG_WIDTH), BF16),
        grid=(B, HG_HEADS),
        in_specs=[col(0), col(1), col(2), col(3), vec, vec],
        out_specs=pl.BlockSpec((1, S, HG_DV), lambda b, h: (b, 0, h)),
        compiler_params=_cparams(("arbitrary", "arbitrary")),
        name="hgrn",
    )(hg, hg, hg, hg, lb, ng)


def _cmpmlp_kernel(x_ref, w1a_ref, w1b_ref, pe_ref, w1_ref, b1_ref, w2_ref, o_ref):
    x = x_ref[0]
    hb = jnp.dot(pe_ref[...], w1_ref[...], precision=HIGHEST, preferred_element_type=F32)[0:1] + b1_ref[...]
    nrow = x.shape[0]
    for g in range(NSA_KV_GROUPS):
        a = jnp.dot(x, w1a_ref[g], preferred_element_type=F32)
        bm = jnp.dot(x, w1b_ref[g], preferred_element_type=F32)
        hdn = a + pltpu.roll(bm, nrow - 1, 0) + hb
        o_ref[0, g] = jnp.dot(_gelu_tanh(hdn).astype(BF16), w2_ref[g], preferred_element_type=F32).astype(BF16)


def _cmpmlp(x2, w1a, w1b, pe8, w1, b1, w2p):
    B, ncb, width = x2.shape
    full = lambda a: pl.BlockSpec(a.shape, lambda b: (0,) * a.ndim)
    return pl.pallas_call(
        _cmpmlp_kernel,
        out_shape=jax.ShapeDtypeStruct((B, NSA_KV_GROUPS, ncb, LANES), BF16),
        grid=(B,),
        in_specs=[pl.BlockSpec((1, ncb, width), lambda b: (b, 0, 0)),
                  full(w1a), full(w1b), full(pe8), full(w1), full(b1), full(w2p)],
        out_specs=pl.BlockSpec((1, NSA_KV_GROUPS, ncb, LANES), lambda b: (b, 0, 0, 0)),
        compiler_params=_cparams(("arbitrary",)),
        name="cmpmlp",
    )(x2, w1a, w1b, pe8, w1, b1, w2p)


def _rank_before(score, nrows):
    jrow = lax.broadcasted_iota(jnp.int32, score.shape, 0)
    rank = jnp.zeros(score.shape, F32)
    for k in range(nrows):
        rk = score[k:k + 1]
        beats = (rk > score) | ((rk == score) & (jrow > k))
        rank = rank + jnp.where(beats, 1.0, 0.0)
    return rank


def _topk_rows(score, k):
    n = score.shape[0]
    row = lax.broadcasted_iota(jnp.int32, score.shape, 0).astype(F32)
    keep = jnp.zeros(score.shape, F32)
    for _ in range(k):
        top = jnp.max(score, axis=0, keepdims=True)
        first = jnp.min(jnp.where(score == top, row, float(n)), axis=0, keepdims=True)
        pick = row == first
        keep = jnp.where(pick, 1.0, keep)
        score = jnp.where(pick, -jnp.inf, score)
    return keep


def _transpose_small_ints(xt):
    c = xt.shape[1]
    eye = (lax.broadcasted_iota(jnp.int32, (c, c), 0) == lax.broadcasted_iota(jnp.int32, (c, c), 1)).astype(BF16)
    return lax.dot_general(eye, xt.astype(BF16), NT, preferred_element_type=F32)


def _pack_heads(heads):
    low = lax.broadcasted_iota(jnp.int32, heads[0].shape, 1) < NSA_HD
    return jnp.concatenate([jnp.where(low, heads[k], pltpu.roll(heads[k + 1], NSA_HD, 1))
                            for k in range(0, len(heads), 2)], axis=1)


def _cmpsel_kernel(q_ref, kc_ref, vc_ref, gate_ref, o_ref, sel_ref):
    tq = q_ref.shape[1]
    ncb = kc_ref.shape[2]
    t = pl.program_id(2)
    kc = kc_ref[0, 0]
    vc = vc_ref[0, 0]
    pos = t * tq + lax.broadcasted_iota(jnp.int32, (tq, 1), 0)
    cblk = lax.broadcasted_iota(jnp.int32, (1, ncb), 1)
    vis = (cblk * CMP_STRIDE + CMP_BLOCK - 1) <= pos
    psum = jnp.zeros((tq, ncb), F32)
    gate = gate_ref[0]
    heads = []
    for h in range(NSA_HG):
        s = lax.dot_general(q_ref[0, :, h * LANES:(h + 1) * LANES], kc, NT, preferred_element_type=F32)
        s = jnp.where(vis, s, NEG)
        p = jnp.exp(s - jnp.max(s, axis=-1, keepdims=True))
        p = jnp.where(vis, p / jnp.sum(p, axis=-1, keepdims=True), 0.0)
        heads.append(jnp.dot(p.astype(BF16), vc, preferred_element_type=F32) * gate[:, 3 * h:3 * h + 1])
        psum = psum + p
    o_ref[0] = _pack_heads(heads)
    nsb = ncb * CMP_STRIDE // SEL_BLOCK
    jb = lax.broadcasted_iota(jnp.int32, (nsb, ncb), 0) * SEL_BLOCK
    cb = lax.broadcasted_iota(jnp.int32, (nsb, ncb), 1) * CMP_STRIDE
    ov = jnp.maximum(jnp.minimum(cb + CMP_BLOCK, jb + SEL_BLOCK) - jnp.maximum(cb, jb), 0).astype(F32) / CMP_BLOCK
    ps_hi = psum.astype(BF16)
    ps_lo = (psum - ps_hi.astype(F32)).astype(BF16)
    ovb = ov.astype(BF16)
    pslc_t = (lax.dot_general(ovb, ps_hi, NT, preferred_element_type=F32)
              + lax.dot_general(ovb, ps_lo, NT, preferred_element_type=F32))
    posl = t * tq + lax.broadcasted_iota(jnp.int32, (1, tq), 1)
    cur = posl // SEL_BLOCK
    jrow = lax.broadcasted_iota(jnp.int32, (nsb, tq), 0)
    forced = (jrow == 0) | (jrow == cur) | (jrow == cur - 1)
    score = jnp.where(forced, 1e30, jnp.where(jrow <= cur, pslc_t, NEG))
    chosen = _topk_rows(score, min(N_SEL, nsb)) > 0.5
    drop = jnp.where(chosen & (score > -1e29), 0.0, 1.0)
    drop = jnp.concatenate([drop, jnp.zeros((LANES - nsb, tq), F32)], axis=0)
    sel_ref[0, 0] = (_transpose_small_ints(drop) * NEG).astype(BF16)


def _cmpsel(qraw, kcmp, vcmp, gate, tq):
    B, S, _ = qraw.shape
    ncb = kcmp.shape[2]
    gw = NSA_HG * LANES
    assert S // SEL_BLOCK <= NSA_HD
    cmp_spec = pl.BlockSpec((1, 1, ncb, LANES), lambda b, g, t: (b, g, 0, 0))
    return pl.pallas_call(
        _cmpsel_kernel,
        out_shape=(jax.ShapeDtypeStruct((B, S, NSA_WIDTH), F32),
                   jax.ShapeDtypeStruct((B, NSA_KV_GROUPS, S, LANES), BF16)),
        grid=(B, NSA_KV_GROUPS, S // tq),
        in_specs=[pl.BlockSpec((1, tq, gw), lambda b, g, t: (b, t, g)), cmp_spec, cmp_spec,
                  pl.BlockSpec((1, tq, LANES), lambda b, g, t: (b, t, g))],
        out_specs=(pl.BlockSpec((1, tq, NSA_WIDTH // NSA_KV_GROUPS), lambda b, g, t: (b, t, g)),
                   pl.BlockSpec((1, 1, tq, LANES), lambda b, g, t: (b, g, t, 0))),
        compiler_params=_cparams(("arbitrary", "arbitrary", "arbitrary")),
        name="cmpsel",
    )(qraw, kcmp, vcmp, gate)


def _slcwin_kernel(q_ref, sel_ref, ks_ref, kw_ref, vs_ref, vw_ref, gate_ref, ocmp_ref, wgu_ref, wdn_ref,
                   o_ref, wgu_bf_ref, wdn_bf_ref, *, tk, wk):
    wgu_bf_ref[...] = wgu_ref[...].astype(BF16)
    wdn_bf_ref[...] = wdn_ref[...].astype(BF16)
    tq = q_ref.shape[1]
    S = ks_ref.shape[1]
    t = pl.program_id(2)
    q0 = t * tq
    qpos = q0 + lax.broadcasted_iota(jnp.int32, (tq, 1), 0)
    qw = [q_ref[0, :, h * LANES:(h + 1) * LANES] for h in range(NSA_HG)]
    sel = sel_ref[0, 0]
    qa = [jnp.concatenate([q, sel], axis=1) for q in qw]

    gate = gate_ref[0]

    def normalise(acc, g):
        return acc * (g / acc[:, NSA_HD:NSA_HD + 1])

    nchain = NSA_HG // SLC_STACK
    qst = [jnp.concatenate(qa[c * SLC_STACK:(c + 1) * SLC_STACK], axis=0) for c in range(nchain)]
    qpos_st = jnp.concatenate([qpos] * SLC_STACK, axis=0)

    def kstep(j, carry, diag):
        k0 = pl.multiple_of(j * tk, tk)
        ks = ks_ref[0, pl.ds(k0, tk), :]
        vs = vs_ref[0, pl.ds(k0, tk), :]
        if diag:
            vis = (k0 + lax.broadcasted_iota(jnp.int32, (1, tk), 1)) <= qpos_st
        out = []
        for c in range(nchain):
            m, acc = carry[c]
            s = lax.dot_general(qst[c], ks, NT, preferred_element_type=F32)
            if diag:
                s = jnp.where(vis, s, NEG)
            mn = jnp.maximum(m, jnp.max(s, axis=-1, keepdims=True))
            p = jnp.exp2(s - mn)
            acc = jnp.exp2(m - mn) * acc + jnp.dot(p.astype(BF16), vs, preferred_element_type=F32)
            out.append((mn, acc))
        return tuple(out)

    init = tuple((jnp.full((SLC_STACK * tq, 1), NEG, F32), jnp.zeros((SLC_STACK * tq, LANES), F32))
                 for _ in range(nchain))
    jdiag = q0 // tk
    carry = lax.fori_loop(0, jdiag, lambda j, c: kstep(j, c, False), init)
    carry = kstep(jdiag, carry, True)
    heads = [normalise(carry[h // SLC_STACK][1][(h % SLC_STACK) * tq:(h % SLC_STACK + 1) * tq],
                       gate[:, 3 * h + 1:3 * h + 2]) for h in range(NSA_HG)]

    wq = wk - WINDOW
    wins = [[] for _ in range(NSA_HG)]
    spos = lax.broadcasted_iota(jnp.int32, (NSA_HG * wq, 1), 0) % wq
    for i in range(tq // wq):
        rows = slice(i * wq, (i + 1) * wq)
        start = pl.multiple_of(jnp.maximum(q0 + (i + 1) * wq - wk, 0), wq)
        kw = kw_ref[0, pl.ds(start, wk), :]
        vw = vw_ref[0, pl.ds(start, wk), :]
        d = (q0 + i * wq + spos) - (start + lax.broadcasted_iota(jnp.int32, (1, wk), 1))
        q_st = jnp.concatenate([q[rows] for q in qw], axis=0)
        s = jnp.where((d >= 0) & (d < WINDOW), lax.dot_general(q_st, kw, NT, preferred_element_type=F32), NEG)
        p = jnp.exp2(s - jnp.max(s, axis=-1, keepdims=True))
        o_st = jnp.dot(p.astype(BF16), vw, preferred_element_type=F32)
        for h in range(NSA_HG):
            wins[h].append(o_st[h * wq:(h + 1) * wq])
    for h in range(NSA_HG):
        heads[h] = heads[h] + normalise(jnp.concatenate(wins[h], axis=0), gate[:, 3 * h + 2:3 * h + 3])
    o_ref[0] = (ocmp_ref[0] + _pack_heads(heads)).astype(BF16)


def _slcwin(qrot, sel, ks, kw, vaug, gate, ocmp, wgu, wdn, tq, tk):
    B, S, _ = qrot.shape
    gw = NSA_HG * LANES
    wk = WINDOW + WIN_Q
    nt = S // tq
    nstep = B * NSA_KV_GROUPS * nt
    assert S >= wk and tq % WIN_Q == 0 and N_EXPERTS % nstep == 0
    epb = N_EXPERTS // nstep
    ospec = pl.BlockSpec((1, tq, NSA_WIDTH // NSA_KV_GROUPS), lambda b, g, t: (b, t, g))
    wspec = lambda a: pl.BlockSpec((epb,) + a.shape[1:], lambda b, g, t: ((b * NSA_KV_GROUPS + g) * nt + t, 0, 0))
    return pl.pallas_call(
        functools.partial(_slcwin_kernel, tk=tk, wk=wk),
        out_shape=(jax.ShapeDtypeStruct((B, S, NSA_WIDTH), BF16),
                   jax.ShapeDtypeStruct(wgu.shape, BF16), jax.ShapeDtypeStruct(wdn.shape, BF16)),
        grid=(B, NSA_KV_GROUPS, nt),
        in_specs=[pl.BlockSpec((1, tq, gw), lambda b, g, t: (b, t, g)),
                  pl.BlockSpec((1, 1, tq, LANES), lambda b, g, t: (b, g, t, 0)),
                  pl.BlockSpec((1, S, 2 * LANES), lambda b, g, t: (b, 0, 0)),
                  pl.BlockSpec((1, S, LANES), lambda b, g, t: (b, 0, 0)),
                  pl.BlockSpec((1, S, LANES), lambda b, g, t: (b, 0, g)),
                  pl.BlockSpec((1, S, LANES), lambda b, g, t: (b, 0, NSA_KV_GROUPS + g)),
                  pl.BlockSpec((1, tq, LANES), lambda b, g, t: (b, t, g)),
                  ospec, wspec(wgu), wspec(wdn)],
        out_specs=(ospec, wspec(wgu), wspec(wdn)),
        compiler_params=_cparams(("arbitrary", "arbitrary", "arbitrary")),
        name="slcwin",
    )(qrot, sel, ks, kw, vaug, vaug, gate, ocmp, wgu, wdn)


def _outproj_kernel(x_ref, ohg_ref, onsa_ref, wout_ref,
                    g1_ref, sh2_ref, sc2_ref, n2_ref, rwt_ref, rb_ref,
                    x1_ref, h2_ref, sc_ref, slot_t_ref, w_t_ref, cnt_ref):
    tm = x_ref.shape[1]
    mix = jnp.dot(jnp.concatenate([ohg_ref[0], onsa_ref[0]], axis=1), wout_ref[...], preferred_element_type=F32)
    x1 = x_ref[0] + g1_ref[0] * mix
    x1_ref[0] = x1
    h2 = (x1 * lax.rsqrt(jnp.mean(x1 * x1, axis=-1, keepdims=True) + EPS) * n2_ref[...]) * (1.0 + sc2_ref[0]) + sh2_ref[0]
    h_hi = h2.astype(BF16)
    h2_ref[0] = h_hi
    h_lo = (h2 - h_hi.astype(F32)).astype(BF16)
    both = lax.dot_general(rwt_ref[...], h_hi, NT, preferred_element_type=F32)
    logits = (both[:N_EXPERTS] + both[N_EXPERTS:]
              + lax.dot_general(rwt_ref[:N_EXPERTS, :], h_lo, NT, preferred_element_type=F32))
    scores = _sigmoid(logits)
    choice = scores + rb_ref[...]
    per = N_EXPERTS // N_GROUPS
    c3 = choice.reshape(N_GROUPS, per, tm)
    erow = lax.broadcasted_iota(jnp.int32, c3.shape, 1)
    rank_in = jnp.zeros(c3.shape, F32)
    for k in range(per):
        ck = c3[:, k:k + 1, :]
        rank_in = rank_in + jnp.where((ck > c3) | ((ck == c3) & (erow > k)), 1.0, 0.0)
    grp_score = jnp.sum(jnp.where(rank_in < 2, c3, 0.0), axis=1)
    grp_keep = _rank_before(grp_score, N_GROUPS) < TOPK_GROUPS
    masked = jnp.where(grp_keep[:, None, :], c3, -jnp.inf).reshape(N_EXPERTS, tm)
    keep = _topk_rows(masked, TOP_K) > 0.5
    tw = jnp.where(keep, scores, 0.0)
    tw = tw / jnp.sum(tw, axis=0, keepdims=True) * ROUTED_SCALE
    before = lax.broadcasted_iota(jnp.int32, (tm, tm), 0)
    after = lax.broadcasted_iota(jnp.int32, (tm, tm), 1)
    earlier = (before < after) & (before // MOE_SUB == after // MOE_SUB)
    kept = jnp.where(keep, 1.0, 0.0)
    pref = jnp.dot(kept.astype(BF16), earlier.astype(BF16), preferred_element_type=F32)
    slot = jnp.where(keep, pref, -1.0)
    assert MOE_SUB <= 256
    sc_ref[0] = _transpose_small_ints(slot)
    for j in range(tm // MOE_SUB):
        sub = slice(j * MOE_SUB, (j + 1) * MOE_SUB)
        slot_t_ref[:, j] = slot[:, sub].reshape(N_EXPERTS // MOE_EG, MOE_EG, MOE_SUB)
        w_t_ref[:, j] = tw[:, sub].reshape(N_EXPERTS // MOE_EG, MOE_EG, MOE_SUB)
        cnt_ref[j] = jnp.broadcast_to(jnp.sum(kept[:, sub], axis=1, keepdims=True), (N_EXPERTS, LANES))


def _outproj(x, ohg, onsa, wout, mod3, n2, rwt, rb, tm):
    B, S, D = x.shape
    blk = lambda w: pl.BlockSpec((1, tm, w), lambda b, i: (b, i, 0))
    full = lambda a: pl.BlockSpec(a.shape, lambda b, i: (0,) * a.ndim)
    modc = lambda k: pl.BlockSpec((1, 1, D), lambda b, i, k=k: (b, 0, k))
    nt = S // tm
    ng = N_EXPERTS // MOE_EG
    nh = tm // MOE_SUB
    tspec = pl.BlockSpec((ng, nh, MOE_EG, MOE_SUB), lambda b, i: (0, b * nt + i, 0, 0))
    return pl.pallas_call(
        _outproj_kernel,
        out_shape=(jax.ShapeDtypeStruct((B, S, D), F32),
                   jax.ShapeDtypeStruct((B, S, D), BF16),
                   jax.ShapeDtypeStruct((B, S, N_EXPERTS), F32),
                   jax.ShapeDtypeStruct((ng, B * nt * nh, MOE_EG, MOE_SUB), F32),
                   jax.ShapeDtypeStruct((ng, B * nt * nh, MOE_EG, MOE_SUB), F32),
                   jax.ShapeDtypeStruct((B * nt * nh, N_EXPERTS, LANES), F32)),
        grid=(B, nt),
        in_specs=[blk(D), blk(HG_WIDTH), blk(NSA_WIDTH), full(wout),
                  modc(2), modc(3), modc(4), full(n2), full(rwt), full(rb)],
        out_specs=(blk(D), blk(D), blk(N_EXPERTS), tspec, tspec,
                   pl.BlockSpec((nh, N_EXPERTS, LANES), lambda b, i: (b * nt + i, 0, 0))),
        compiler_params=_cparams(("arbitrary", "arbitrary")),
        name="outproj",
    )(x, ohg, onsa, wout, mod3, mod3, mod3, n2, rwt, rb)


def _swiglu_hidden(x, wgu):
    gu = jnp.dot(x, wgu, preferred_element_type=F32)
    return _silu(gu[:, :EXPERT_HIDDEN]) * gu[:, EXPERT_HIDDEN:]


def _moe_kernel(rounds_ref, h_ref, slot_t_ref, w_t_ref, slot_c_ref, wgu_ref, wdn_ref, sgu_ref, sdn_ref,
                o_ref, x_scr, y_scr, p_scr):
    i = pl.program_id(0)
    g = pl.program_id(1)
    tm = h_ref.shape[0]
    nsub = tm // MOE_SUB
    rnd, eg = MOE_ROUND, MOE_EG
    ng = N_EXPERTS // eg
    base = i * (N_EXPERTS + ng)

    @pl.when(g == 0)
    def _():
        act = _swiglu_hidden(h_ref[...], sgu_ref[...]).astype(BF16)
        o_ref[...] = jnp.dot(act, sdn_ref[...], preferred_element_type=F32)

    rslot = lax.broadcasted_iota(jnp.int32, (rnd, 1), 0).astype(F32)
    lane_slot = (lax.broadcasted_iota(jnp.int32, (1, eg * rnd), 1) % rnd).astype(F32)
    spread = (lax.broadcasted_iota(jnp.int32, (N_EXPERTS, eg * rnd), 1) // rnd + g * eg
              == lax.broadcasted_iota(jnp.int32, (N_EXPERTS, eg * rnd), 0)).astype(BF16)

    def run_expert(e, rows):
        xe = jnp.concatenate([x_scr[s, rows, :] for s in range(nsub)], axis=0)
        wslot = jnp.concatenate(
            [jnp.sum(p_scr[s, rows, :].astype(F32) * w_t_ref[0, s][e:e + 1, :], axis=-1, keepdims=True)
             for s in range(nsub)], axis=0)
        y = jnp.dot((_swiglu_hidden(xe, wgu_ref[e]) * wslot).astype(BF16), wdn_ref[e], preferred_element_type=F32)
        for s in range(nsub):
            y_scr[s, rows, :] = y[s * rnd:(s + 1) * rnd].astype(BF16)

    def one_round(r, carry, every_expert=False):
        first = (r * rnd).astype(F32) if not every_expert else 0.0
        for s in range(nsub):
            st = slot_t_ref[0, s] - first
            p = jnp.concatenate([jnp.where(rslot == st[e:e + 1, :], 1.0, 0.0) for e in range(eg)], axis=0)
            p_scr[s] = p.astype(BF16)
            x_scr[s] = jnp.dot(p_scr[s], h_ref[s * MOE_SUB:(s + 1) * MOE_SUB, :],
                               preferred_element_type=F32).astype(BF16)
        for e in range(eg):
            rows = slice(e * rnd, (e + 1) * rnd)
            if every_expert:
                run_expert(e, rows)
                continue
            pl.when(r < rounds_ref[base + g * eg + e])(functools.partial(run_expert, e, rows))

            @pl.when(r >= rounds_ref[base + g * eg + e])
            def _():
                for s in range(nsub):
                    y_scr[s, rows, :] = jnp.zeros((rnd, D_MODEL), BF16)
        for s in range(nsub):
            toks = slice(s * MOE_SUB, (s + 1) * MOE_SUB)
            sc = jnp.dot(slot_c_ref[toks, :].astype(BF16), spread, preferred_element_type=F32) - first
            pt = jnp.where(sc == lane_slot, 1.0, 0.0).astype(BF16)
            o_ref[toks, :] += jnp.dot(pt, y_scr[s], preferred_element_type=F32)
        return carry

    one_round(0, 0, every_expert=True)
    lax.fori_loop(1, rounds_ref[base + N_EXPERTS + g], one_round, 0)


def _moe(rounds, h2, slot_t, w_t, slot_c, wgu, wdn, sgu, sdn, tm):
    T, D = h2.shape
    eg, nsub = MOE_EG, tm // MOE_SUB
    full = lambda a: pl.BlockSpec(a.shape, lambda i, e, o: (0,) * a.ndim)
    tspec = pl.BlockSpec((1, nsub, eg, MOE_SUB), lambda i, e, o: (e, i, 0, 0))
    return pl.pallas_call(
        _moe_kernel,
        out_shape=jax.ShapeDtypeStruct((T, D), F32),
        grid_spec=pltpu.PrefetchScalarGridSpec(
            num_scalar_prefetch=1,
            grid=(T // tm, N_EXPERTS // eg),
            in_specs=[pl.BlockSpec((tm, D), lambda i, e, o: (i, 0), pipeline_mode=pl.Buffered(1)), tspec, tspec,
                      pl.BlockSpec((tm, N_EXPERTS), lambda i, e, o: (i, 0)),
                      pl.BlockSpec((eg, D, 2 * EXPERT_HIDDEN), lambda i, e, o: (e, 0, 0)),
                      pl.BlockSpec((eg, EXPERT_HIDDEN, D), lambda i, e, o: (e, 0, 0)),
                      full(sgu), full(sdn)],
            out_specs=pl.BlockSpec((tm, D), lambda i, e, o: (i, 0), pipeline_mode=pl.Buffered(1)),
            scratch_shapes=[pltpu.VMEM((nsub, eg * MOE_ROUND, D), BF16),
                            pltpu.VMEM((nsub, eg * MOE_ROUND, D), BF16),
                            pltpu.VMEM((nsub, eg * MOE_ROUND, MOE_SUB), BF16)]),
        compiler_params=_cparams(("arbitrary", "arbitrary")),
        name="moe",
    )(rounds, h2, slot_t, w_t, slot_c, wgu, wdn, sgu, sdn)


def _final_kernel(x1_ref, moe_ref, g2_ref, fg_ref, o_ref):
    x2 = x1_ref[0] + g2_ref[0] * moe_ref[0]
    o_ref[0] = x2 * lax.rsqrt(jnp.mean(x2 * x2, axis=-1, keepdims=True) + EPS) * fg_ref[...]


def _final(x1, moe, mod3, fg, tm):
    B, S, D = x1.shape
    blk = pl.BlockSpec((1, tm, D), lambda b, i: (b, i, 0))
    return pl.pallas_call(
        _final_kernel,
        out_shape=jax.ShapeDtypeStruct((B, S, D), F32),
        grid=(B, S // tm),
        in_specs=[blk, blk, pl.BlockSpec((1, 1, D), lambda b, i: (b, 0, 5)), pl.BlockSpec((1, D), lambda b, i: (0, 0))],
        out_specs=blk,
        compiler_params=_cparams(("arbitrary", "arbitrary")),
        name="final",
    )(x1, moe, mod3, fg)


def _split_bf16(w):
    hi = w.astype(BF16)
    return jnp.concatenate([hi, (w - hi.astype(F32)).astype(BF16)], axis=0)


def _pack_w_in(w_in):
    return jnp.pad(w_in, ((0, 0), (0, IN_COLS_P - w_in.shape[1]))).astype(BF16)


def _pack_cmp(pos, w1, b1, w2, lane_by_group):
    half = CMP_STRIDE * NSA_HD
    def rows_for(wh):
        w3 = wh.reshape(CMP_STRIDE, NSA_HD, CMP_HIDDEN)
        z = jnp.zeros_like(w3)
        return jnp.stack([jnp.concatenate([w3, z], axis=1).reshape(CMP_STRIDE * LANES, CMP_HIDDEN),
                          jnp.concatenate([z, w3], axis=1).reshape(CMP_STRIDE * LANES, CMP_HIDDEN)])
    w1a = rows_for(w1[:half]).astype(BF16)
    w1b = rows_for(w1[half:]).astype(BF16)
    z2 = jnp.zeros_like(w2)
    w2_first = jnp.concatenate([w2, z2], axis=1)
    w2p = jnp.stack([w2_first, jnp.concatenate([z2, w2], axis=1) if lane_by_group else w2_first]).astype(BF16)
    pe8 = jnp.pad(pos.reshape(1, CMP_BLOCK * NSA_HD), ((0, 7), (0, 0)))
    return w1a, w1b, pe8, w1, b1.reshape(1, CMP_HIDDEN), w2p


def _rope_tables(S):
    half = NSA_HD // 2
    inv = ROPE_THETA ** (-jnp.arange(half, dtype=F32) / half)
    ang = jnp.arange(S, dtype=F32)[:, None] * inv[None, :]
    cos, sin = jnp.cos(ang), jnp.sin(ang)
    reps = LANES // NSA_HD
    return jnp.tile(jnp.concatenate([cos, cos], axis=1), (1, reps)), jnp.tile(jnp.concatenate([-sin, sin], axis=1), (1, reps))


def _tiles(S):
    return dict(inproj=min(512, S), hgrn=min(512, S), cmpsel=min(512, S), slc_q=512, slc_k=512, moe=min(2048, S), final=min(512, S))


def kernel(x, c, w_ada, b_ada, norm1_g, w_in, hg_lb_logits, hg_norm_g, cmp_pos_k, cmp_w1_k, cmp_b1_k, cmp_w2_k,
           cmp_pos_v, cmp_w1_v, cmp_b1_v, cmp_w2_v, w_out, norm2_g, router_w, router_bias, w_exp_gu, w_exp_dn,
           w_sh_gu, w_sh_dn, final_g):
    B, S, D = x.shape
    assert D == D_MODEL and w_ada.shape[0] == 1 and S % 512 == 0
    tl = _tiles(S)
    l = 0
    lb = jnp.cumsum(jax.nn.softmax(hg_lb_logits.astype(F32), axis=0), axis=0)[l].reshape(1, HG_WIDTH)
    c8 = jnp.pad(c, ((0, 8 - B), (0, 0)))
    mod3 = _ada(c8, w_ada[l], b_ada[l].reshape(1, -1))[:B].reshape(B, 1, 6 * D)
    cos, sin = _rope_tables(S)
    hg, qraw, qrot, kc, vc, ks, kw, vaug, gate = _inproj(x, mod3, norm1_g[l].reshape(1, D), _pack_w_in(w_in[l]),
                                                         cos, sin, tl["inproj"])
    ng = hg_norm_g[l].reshape(1, HG_WIDTH)
    mxu_safe = -HG_SUB * jnp.log(jnp.min(lb)) <= HG_SAFE_LOG_RANGE
    ohg = lax.cond(mxu_safe, lambda: _hgrn_mxu(hg, lb, ng, tl["hgrn"]), lambda: _hgrn(hg, lb, ng))
    ncb = S // CMP_STRIDE
    kcmp = _cmpmlp(kc.reshape(B, ncb, CMP_STRIDE * LANES),
                   *_pack_cmp(cmp_pos_k[l], cmp_w1_k[l], cmp_b1_k[l], cmp_w2_k[l], True))
    vcmp = _cmpmlp(vc.reshape(B, ncb, CMP_STRIDE * LANES),
                   *_pack_cmp(cmp_pos_v[l], cmp_w1_v[l], cmp_b1_v[l], cmp_w2_v[l], False))
    ocmp, sel = _cmpsel(qraw, kcmp, vcmp, gate, tl["cmpsel"])
    onsa, wgu_bf, wdn_bf = _slcwin(qrot, sel, ks, kw, vaug, gate, ocmp, w_exp_gu[l], w_exp_dn[l],
                                   tl["slc_q"], tl["slc_k"])
    x1, h2, sc, slot_t, w_t, cnt = _outproj(x, ohg, onsa, w_out[l].astype(BF16), mod3,
                                            norm2_g[l].reshape(1, D), _split_bf16(router_w[l].T),
                                            router_bias[l].reshape(N_EXPERTS, 1), ROUTE_TM)
    T = B * S
    tm = tl["moe"]
    ng = N_EXPERTS // MOE_EG
    slot_c = sc.reshape(T, N_EXPERTS)
    load = jnp.max(cnt[:, :, 0].reshape(T // tm, tm // MOE_SUB, N_EXPERTS), axis=1).astype(jnp.int32)
    per_expert = (load + MOE_ROUND - 1) // MOE_ROUND
    rounds = jnp.concatenate([per_expert, jnp.max(per_expert.reshape(T // tm, ng, MOE_EG), axis=2)], axis=1).reshape(-1)
    moe = _moe(rounds, h2.reshape(T, D), slot_t, w_t, slot_c,
               wgu_bf, wdn_bf, w_sh_gu[l].astype(BF16), w_sh_dn[l].astype(BF16), tm)
    return _final(x1, moe.reshape(B, S, D), mod3, final_g.reshape(1, D), tl["final"])
```

```python
import functools

import jax
import jax.numpy as jnp
from jax import lax
from jax.experimental import pallas as pl
from jax.experimental.pallas import tpu as pltpu

F32 = jnp.float32
BF16 = jnp.bfloat16
HIGHEST = lax.Precision.HIGHEST

D_MODEL = 1024
EPS = 1e-6
HG_HEADS = 4
HG_DK = 128
HG_DV = 128
HG_WIDTH = HG_HEADS * HG_DV
HG_CHUNK = 64
HG_SUB = 16
HG_SAFE_LOG_RANGE = 80.0
NSA_HEADS = 8
NSA_KV_GROUPS = 2
NSA_HG = NSA_HEADS // NSA_KV_GROUPS
NSA_HD = 64
NSA_WIDTH = NSA_HEADS * NSA_HD
NSA_KV = NSA_KV_GROUPS * NSA_HD
CMP_BLOCK = 32
CMP_STRIDE = 16
CMP_HIDDEN = 256
SEL_BLOCK = 64
N_SEL = 16
WINDOW = 512
ROPE_THETA = 10000.0
SLC_STACK = 4
WIN_Q = 256
N_EXPERTS = 64
TOP_K = 8
N_GROUPS = 8
TOPK_GROUPS = 4
EXPERT_HIDDEN = 256
ROUTED_SCALE = 2.5
ROUTE_TM = 512
MOE_SUB = 256
MOE_ROUND = 32
MOE_EG = 8
MOE_FIRST = 2

LANES = 128
NEG = -1e30
VMEM_LIMIT = 56 * 1024 * 1024

C_HG = 0
C_Q = 4 * HG_WIDTH
C_K = C_Q + NSA_WIDTH
C_GATE = C_K + 6 * NSA_KV
IN_COLS_P = C_GATE + LANES
LOG2E = 1.4426950408889634

NT = (((1,), (1,)), ((), ()))
TN = (((0,), (0,)), ((), ()))


def _cparams(sem):
    return pltpu.CompilerParams(dimension_semantics=sem, vmem_limit_bytes=VMEM_LIMIT)


def _sigmoid(x):
    return 1.0 / (1.0 + jnp.exp(-x))


def _silu(x):
    return x * _sigmoid(x)


def _gelu_tanh(x):
    return 0.5 * x * (1.0 + jnp.tanh(0.7978845608028654 * (x + 0.044715 * (x * x * x))))


def _ada_kernel(c_ref, w_ref, b_ref, o_ref):
    a = _silu(c_ref[...])
    o_ref[...] = jnp.dot(a, w_ref[...], precision=HIGHEST, preferred_element_type=F32) + b_ref[...]


def _ada(c8, w, b):
    n = w.shape[1]
    tn = 1024
    return pl.pallas_call(
        _ada_kernel,
        out_shape=jax.ShapeDtypeStruct((8, n), F32),
        grid=(n // tn,),
        in_specs=[pl.BlockSpec((8, D_MODEL), lambda j: (0, 0)),
                  pl.BlockSpec((D_MODEL, tn), lambda j: (0, j)),
                  pl.BlockSpec((1, tn), lambda j: (0, j))],
        out_specs=pl.BlockSpec((8, tn), lambda j: (0, j)),
        compiler_params=_cparams(("arbitrary",)),
        name="ada",
    )(c8, w, b)


def _rope(t, cos, sin_signed, first_half):
    rot = jnp.where(first_half, pltpu.roll(t, 96, 1), pltpu.roll(t, 32, 1))
    return t * cos + rot * sin_signed


def _inproj_kernel(x_ref, sh_ref, sc_ref, g_ref, w_ref, cos_ref, sin_ref,
                   hg_ref, qraw_ref, qrot_ref, kc_ref, vc_ref, ks_ref, kw_ref, va_ref, gate_ref, h_scr):
    tm = x_ref.shape[1]
    x = x_ref[0]
    y = x * lax.rsqrt(jnp.mean(x * x, axis=-1, keepdims=True) + EPS) * g_ref[...]
    h_scr[...] = (y * (1.0 + sc_ref[0]) + sh_ref[0]).astype(BF16)

    def mm(lo, width):
        return jnp.dot(h_scr[...], w_ref[:, lo:lo + width], preferred_element_type=F32)

    cos = cos_ref[...]
    sin = sin_ref[...]
    first_half = (lax.broadcasted_iota(jnp.int32, cos.shape, 1) % NSA_HD) < (NSA_HD // 2)
    lane = lax.broadcasted_iota(jnp.int32, (tm, LANES), 1)
    half_of_lane = lane // NSA_HD

    def own_lanes(t, src_half, dst_half):
        moved = t if src_half == dst_half else pltpu.roll(t, NSA_HD, 1)
        return jnp.where(half_of_lane == dst_half, moved, 0.0)

    for j in range(4):
        hg_ref[0, :, j * HG_WIDTH:(j + 1) * HG_WIDTH] = mm(C_HG + j * HG_WIDTH, HG_WIDTH)
    for m in range(NSA_WIDTH // (2 * LANES)):
        qq = mm(C_Q + m * 2 * LANES, 2 * LANES) * (NSA_HD ** -0.5)
        for c in range(2):
            q = qq[:, c * LANES:(c + 1) * LANES]
            qr = _rope(q, cos, sin, first_half) * LOG2E
            for half in range(2):
                n = (2 * m + c) * 2 + half
                g = n // NSA_HG
                qraw_ref[0, :, n * LANES:(n + 1) * LANES] = own_lanes(q, half, g).astype(BF16)
                qrot_ref[0, :, n * LANES:(n + 1) * LANES] = own_lanes(qr, half, g).astype(BF16)
    kv = mm(C_K, 2 * LANES)
    kc_ref[0] = kv[:, :LANES].astype(BF16)
    vc_ref[0] = kv[:, LANES:].astype(BF16)
    pos = pl.program_id(1) * tm + lax.broadcasted_iota(jnp.int32, (tm, LANES), 0)
    ks_ref[0, :, LANES:2 * LANES] = jnp.where(lane == pos // SEL_BLOCK, 1.0, 0.0).astype(BF16)
    for j in range(2):
        kv = mm(C_K + (j + 1) * 2 * LANES, 2 * LANES)
        k_rot = _rope(kv[:, :LANES], cos, sin, first_half).astype(BF16)
        if j == 0:
            ks_ref[0, :, 0:LANES] = k_rot
        else:
            kw_ref[0] = k_rot
        for g in range(NSA_KV_GROUPS):
            v = own_lanes(kv[:, LANES:], g, 0)
            col = (j * NSA_KV_GROUPS + g) * LANES
            va_ref[0, :, col:col + LANES] = jnp.where(lane == NSA_HD, 1.0, v).astype(BF16)
    gate = _sigmoid(mm(C_GATE, LANES))
    per_group = 3 * NSA_HG
    for g in range(NSA_KV_GROUPS):
        gate_ref[0, :, g * LANES:(g + 1) * LANES] = gate if g == 0 else pltpu.roll(gate, LANES - g * per_group, 1)


def _inproj(x, mod3, norm_g, w_p, cos, sin, tm):
    B, S, D = x.shape
    blk = lambda w: pl.BlockSpec((1, tm, w), lambda b, i: (b, i, 0))
    return pl.pallas_call(
        _inproj_kernel,
        out_shape=(jax.ShapeDtypeStruct((B, S, 4 * HG_WIDTH), F32),
                   jax.ShapeDtypeStruct((B, S, NSA_HEADS * LANES), BF16),
                   jax.ShapeDtypeStruct((B, S, NSA_HEADS * LANES), BF16),
                   jax.ShapeDtypeStruct((B, S, LANES), BF16),
                   jax.ShapeDtypeStruct((B, S, LANES), BF16),
                   jax.ShapeDtypeStruct((B, S, 2 * LANES), BF16),
                   jax.ShapeDtypeStruct((B, S, LANES), BF16),
                   jax.ShapeDtypeStruct((B, S, 4 * LANES), BF16),
                   jax.ShapeDtypeStruct((B, S, NSA_KV_GROUPS * LANES), F32)),
        grid=(B, S // tm),
        in_specs=[blk(D),
                  pl.BlockSpec((1, 1, D), lambda b, i: (b, 0, 0)),
                  pl.BlockSpec((1, 1, D), lambda b, i: (b, 0, 1)),
                  pl.BlockSpec((1, D), lambda b, i: (0, 0)),
                  pl.BlockSpec((D, IN_COLS_P), lambda b, i: (0, 0)),
                  pl.BlockSpec((tm, LANES), lambda b, i: (i, 0)),
                  pl.BlockSpec((tm, LANES), lambda b, i: (i, 0))],
        out_specs=(blk(4 * HG_WIDTH), blk(NSA_HEADS * LANES), blk(NSA_HEADS * LANES),
                   blk(LANES), blk(LANES), blk(2 * LANES), blk(LANES), blk(4 * LANES), blk(NSA_KV_GROUPS * LANES)),
        scratch_shapes=[pltpu.VMEM((tm, D), BF16)],
        compiler_params=_cparams(("arbitrary", "arbitrary")),
        name="inproj",
    )(x, mod3, mod3, norm_g, w_p, cos, sin)


def _hgrn_kernel(q_ref, f_ref, i_ref, gt_ref, lb_ref, ng_ref, o_ref):
    S = q_ref.shape[1]
    C, U = HG_CHUNK, HG_SUB
    lb = lb_ref[...]
    ng = ng_ref[...]
    ri = lax.broadcasted_iota(jnp.int32, (C, C), 0)
    ci = lax.broadcasted_iota(jnp.int32, (C, C), 1)
    tril = (ri >= ci).astype(F32)
    trow = lax.broadcasted_iota(jnp.int32, (U, 1), 0)

    def chunk(c, st_t):
        r0 = pl.multiple_of(c * C, C)
        rows = pl.ds(r0, C)
        f = lb + (1.0 - lb) * _sigmoid(f_ref[0, rows, :])
        kk = 1.0 - f
        b = jnp.dot(tril, jnp.log(f), precision=HIGHEST, preferred_element_type=F32)
        q = q_ref[0, rows, :] * (HG_DK ** -0.5)
        v = i_ref[0, rows, :]
        vb = v.astype(BF16)
        o_inter = lax.dot_general((q * jnp.exp(b)).astype(BF16), st_t.astype(BF16), NT,
                                  preferred_element_type=F32)
        parts = []
        for i in range(C // U):
            lo = i * U
            bi = b[lo:lo + U]
            qi = q[lo:lo + U]
            if i == 0:
                oi = jnp.zeros((U, HG_DV), F32)
            else:
                r = b[lo - 1:lo]
                qrel = (qi * jnp.exp(bi - r)).astype(BF16)
                kprev = (kk[:lo] * jnp.exp(r - b[:lo])).astype(BF16)
                a_off = lax.dot_general(qrel, kprev, NT, preferred_element_type=F32)
                oi = jnp.dot(a_off.astype(BF16), vb[:lo], preferred_element_type=F32)
            for s in range(U):
                valid = trow >= s
                e = jnp.exp(jnp.where(valid, bi - bi[s:s + 1], 0.0))
                a = jnp.sum(qi * e * kk[lo + s:lo + s + 1], axis=-1, keepdims=True)
                oi = oi + jnp.where(valid, a, 0.0) * v[lo + s:lo + s + 1]
            parts.append(oi)
        o = o_inter + jnp.concatenate(parts, axis=0)
        o = o * lax.rsqrt(jnp.mean(o * o, axis=-1, keepdims=True) + EPS) * ng
        o_ref[0, rows, :] = (o * _silu(gt_ref[0, rows, :])).astype(BF16)
        bl = b[C - 1:C]
        kv_t = lax.dot_general(vb, (kk * jnp.exp(bl - b)).astype(BF16), TN, preferred_element_type=F32)
        return jnp.exp(bl) * st_t + kv_t

    lax.fori_loop(0, S // C, chunk, jnp.zeros((HG_DV, HG_DK), F32))


def _cumsum_rows(x):
    n = x.shape[0]
    row = lax.broadcasted_iota(jnp.int32, x.shape, 0)
    d = 1
    while d < n:
        x = x + jnp.where(row >= d, pltpu.roll(x, d, 0), 0.0)
        d *= 2
    return x


def _hgrn_mxu_kernel(hg_ref, lb_ref, ng_ref, o_ref, st_scr):
    ts = hg_ref.shape[1]
    C, U = HG_CHUNK, HG_SUB

    @pl.when(pl.program_id(1) == 0)
    def _():
        st_scr[...] = jnp.zeros(st_scr.shape, F32)

    W = HG_WIDTH
    NH = HG_HEADS
    head_of_lane = lax.broadcasted_iota(jnp.int32, (1, W), 1) // HG_DK
    hcols = [slice(h * HG_DK, (h + 1) * HG_DK) for h in range(NH)]

    def chunk(c, carry):
        rows = pl.ds(pl.multiple_of(c * C, C), C)
        lb = lb_ref[...]
        f = lb + (1.0 - lb) * _sigmoid(hg_ref[0, rows, W:2 * W])
        kk = 1.0 - f
        b = _cumsum_rows(jnp.log(f))
        q = hg_ref[0, rows, 0:W] * (HG_DK ** -0.5)
        vb = hg_ref[0, rows, 2 * W:3 * W].astype(BF16)
        qe = (q * jnp.exp(b)).astype(BF16)
        o_inter = jnp.concatenate(
            [lax.dot_general(qe[:, hc], st_scr[h].astype(BF16), NT, preferred_element_type=F32)
             for h, hc in enumerate(hcols)], axis=1)
        parts = []
        for i in range(C // U):
            lo, hi = i * U, (i + 1) * U
            r = b[lo - 1:lo] if i else jnp.zeros((1, W), F32)
            qrel = q[lo:hi] * jnp.exp(b[lo:hi] - r)
            kall = (kk[:hi] * jnp.exp(r - b[:hi])).astype(BF16)
            qbd = jnp.concatenate([jnp.where(head_of_lane == h, qrel, 0.0) for h in range(NH)], axis=0)
            a = lax.dot_general(qbd.astype(BF16), kall, NT, preferred_element_type=F32)
            trow = lax.broadcasted_iota(jnp.int32, (NH * U, hi), 0) % U
            a = jnp.where(lax.broadcasted_iota(jnp.int32, (NH * U, hi), 1) <= lo + trow, a, 0.0)
            oa = jnp.dot(a.astype(BF16), vb[:hi], preferred_element_type=F32)
            oi = jnp.where(head_of_lane == 0, oa[0:U], 0.0)
            for h in range(1, NH):
                oi = jnp.where(head_of_lane == h, oa[h * U:(h + 1) * U], oi)
            parts.append(oi)
        o = o_inter + jnp.concatenate(parts, axis=0)
        o = jnp.concatenate(
            [o[:, hc] * lax.rsqrt(jnp.mean(o[:, hc] * o[:, hc], axis=-1, keepdims=True) + EPS) for hc in hcols], axis=1)
        o_ref[0, rows, :] = (o * ng_ref[...] * _silu(hg_ref[0, rows, 3 * W:4 * W])).astype(BF16)
        bl = b[C - 1:C]
        ke = (kk * jnp.exp(bl - b)).astype(BF16)
        decay = jnp.exp(bl)
        for h, hc in enumerate(hcols):
            kv_t = lax.dot_general(vb[:, hc], ke[:, hc], TN, preferred_element_type=F32)
            st_scr[h] = decay[:, hc] * st_scr[h] + kv_t
        return carry

    lax.fori_loop(0, ts // C, chunk, 0, unroll=8)


def _hgrn_mxu(hg, lb, ng, ts):
    B, S, _ = hg.shape
    vec = pl.BlockSpec((1, HG_WIDTH), lambda b, i: (0, 0))
    return pl.pallas_call(
        _hgrn_mxu_kernel,
        out_shape=jax.ShapeDtypeStruct((B, S, HG_WIDTH), BF16),
        grid=(B, S // ts),
        in_specs=[pl.BlockSpec((1, ts, 4 * HG_WIDTH), lambda b, i: (b, i, 0)), vec, vec],
        out_specs=pl.BlockSpec((1, ts, HG_WIDTH), lambda b, i: (b, i, 0)),
        scratch_shapes=[pltpu.VMEM((HG_HEADS, HG_DV, HG_DK), F32)],
        compiler_params=_cparams(("arbitrary", "arbitrary")),
        name="hgrn_mxu",
    )(hg, lb, ng)


def _hgrn(hg, lb, ng):
    B, S, _ = hg.shape
    col = lambda k: pl.BlockSpec((1, S, HG_DK), lambda b, h, k=k: (b, 0, k * HG_HEADS + h))
    vec = pl.BlockSpec((1, HG_DK), lambda b, h: (0, h))
    return pl.pallas_call(
        _hgrn_kernel,
        out_shape=jax.ShapeDtypeStruct((B, S, HG_WIDTH), BF16),
        grid=(B, HG_HEADS),
        in_specs=[col(0), col(1), col(2), col(3), vec, vec],
        out_specs=pl.BlockSpec((1, S, HG_DV), lambda b, h: (b, 0, h)),
        compiler_params=_cparams(("arbitrary", "arbitrary")),
        name="hgrn",
    )(hg, hg, hg, hg, lb, ng)


def _cmpmlp_kernel(x_ref, w1a_ref, w1b_ref, pe_ref, w1_ref, b1_ref, w2_ref, o_ref):
    x = x_ref[0]
    hb = jnp.dot(pe_ref[...], w1_ref[...], precision=HIGHEST, preferred_element_type=F32)[0:1] + b1_ref[...]
    nrow = x.shape[0]
    for g in range(NSA_KV_GROUPS):
        a = jnp.dot(x, w1a_ref[g], preferred_element_type=F32)
        bm = jnp.dot(x, w1b_ref[g], preferred_element_type=F32)
        hdn = a + pltpu.roll(bm, nrow - 1, 0) + hb
        o_ref[0, g] = jnp.dot(_gelu_tanh(hdn).astype(BF16), w2_ref[g], preferred_element_type=F32).astype(BF16)


def _cmpmlp(x2, w1a, w1b, pe8, w1, b1, w2p):
    B, ncb, width = x2.shape
    full = lambda a: pl.BlockSpec(a.shape, lambda b: (0,) * a.ndim)
    return pl.pallas_call(
        _cmpmlp_kernel,
        out_shape=jax.ShapeDtypeStruct((B, NSA_KV_GROUPS, ncb, LANES), BF16),
        grid=(B,),
        in_specs=[pl.BlockSpec((1, ncb, width), lambda b: (b, 0, 0)),
                  full(w1a), full(w1b), full(pe8), full(w1), full(b1), full(w2p)],
        out_specs=pl.BlockSpec((1, NSA_KV_GROUPS, ncb, LANES), lambda b: (b, 0, 0, 0)),
        compiler_params=_cparams(("arbitrary",)),
        name="cmpmlp",
    )(x2, w1a, w1b, pe8, w1, b1, w2p)


def _rank_before(score, nrows):
    jrow = lax.broadcasted_iota(jnp.int32, score.shape, 0)
    rank = jnp.zeros(score.shape, F32)
    for k in range(nrows):
        rk = score[k:k + 1]
        beats = (rk > score) | ((rk == score) & (jrow > k))
        rank = rank + jnp.where(beats, 1.0, 0.0)
    return rank


def _topk_rows(score, k):
    n = score.shape[0]
    row = lax.broadcasted_iota(jnp.int32, score.shape, 0).astype(F32)
    keep = jnp.zeros(score.shape, F32)
    for _ in range(k):
        top = jnp.max(score, axis=0, keepdims=True)
        first = jnp.min(jnp.where(score == top, row, float(n)), axis=0, keepdims=True)
        pick = row == first
        keep = jnp.where(pick, 1.0, keep)
        score = jnp.where(pick, -jnp.inf, score)
    return keep


def _transpose_small_ints(xt):
    c = xt.shape[1]
    eye = (lax.broadcasted_iota(jnp.int32, (c, c), 0) == lax.broadcasted_iota(jnp.int32, (c, c), 1)).astype(BF16)
    return lax.dot_general(eye, xt.astype(BF16), NT, preferred_element_type=F32)


def _pack_heads(heads):
    low = lax.broadcasted_iota(jnp.int32, heads[0].shape, 1) < NSA_HD
    return jnp.concatenate([jnp.where(low, heads[k], pltpu.roll(heads[k + 1], NSA_HD, 1))
                            for k in range(0, len(heads), 2)], axis=1)


def _cmpsel_kernel(q_ref, kc_ref, vc_ref, gate_ref, o_ref, sel_ref):
    tq = q_ref.shape[1]
    ncb = kc_ref.shape[2]
    t = pl.program_id(2)
    kc = kc_ref[0, 0]
    vc = vc_ref[0, 0]
    pos = t * tq + lax.broadcasted_iota(jnp.int32, (tq, 1), 0)
    cblk = lax.broadcasted_iota(jnp.int32, (1, ncb), 1)
    vis = (cblk * CMP_STRIDE + CMP_BLOCK - 1) <= pos
    psum = jnp.zeros((tq, ncb), F32)
    gate = gate_ref[0]
    heads = []
    for h in range(NSA_HG):
        s = lax.dot_general(q_ref[0, :, h * LANES:(h + 1) * LANES], kc, NT, preferred_element_type=F32)
        s = jnp.where(vis, s, NEG)
        p = jnp.exp(s - jnp.max(s, axis=-1, keepdims=True))
        p = jnp.where(vis, p / jnp.sum(p, axis=-1, keepdims=True), 0.0)
        heads.append(jnp.dot(p.astype(BF16), vc, preferred_element_type=F32) * gate[:, 3 * h:3 * h + 1])
        psum = psum + p
    o_ref[0] = _pack_heads(heads)
    nsb = ncb * CMP_STRIDE // SEL_BLOCK
    jb = lax.broadcasted_iota(jnp.int32, (nsb, ncb), 0) * SEL_BLOCK
    cb = lax.broadcasted_iota(jnp.int32, (nsb, ncb), 1) * CMP_STRIDE
    ov = jnp.maximum(jnp.minimum(cb + CMP_BLOCK, jb + SEL_BLOCK) - jnp.maximum(cb, jb), 0).astype(F32) / CMP_BLOCK
    ps_hi = psum.astype(BF16)
    ps_lo = (psum - ps_hi.astype(F32)).astype(BF16)
    ovb = ov.astype(BF16)
    pslc_t = (lax.dot_general(ovb, ps_hi, NT, preferred_element_type=F32)
              + lax.dot_general(ovb, ps_lo, NT, preferred_element_type=F32))
    posl = t * tq + lax.broadcasted_iota(jnp.int32, (1, tq), 1)
    cur = posl // SEL_BLOCK
    jrow = lax.broadcasted_iota(jnp.int32, (nsb, tq), 0)
    forced = (jrow == 0) | (jrow == cur) | (jrow == cur - 1)
    score = jnp.where(forced, 1e30, jnp.where(jrow <= cur, pslc_t, NEG))
    chosen = _topk_rows(score, min(N_SEL, nsb)) > 0.5
    drop = jnp.where(chosen & (score > -1e29), 0.0, 1.0)
    drop = jnp.concatenate([drop, jnp.zeros((LANES - nsb, tq), F32)], axis=0)
    sel_ref[0, 0] = (_transpose_small_ints(drop) * NEG).astype(BF16)


def _cmpsel(qraw, kcmp, vcmp, gate, tq):
    B, S, _ = qraw.shape
    ncb = kcmp.shape[2]
    gw = NSA_HG * LANES
    assert S // SEL_BLOCK <= NSA_HD
    cmp_spec = pl.BlockSpec((1, 1, ncb, LANES), lambda b, g, t: (b, g, 0, 0))
    return pl.pallas_call(
        _cmpsel_kernel,
        out_shape=(jax.ShapeDtypeStruct((B, S, NSA_WIDTH), F32),
                   jax.ShapeDtypeStruct((B, NSA_KV_GROUPS, S, LANES), BF16)),
        grid=(B, NSA_KV_GROUPS, S // tq),
        in_specs=[pl.BlockSpec((1, tq, gw), lambda b, g, t: (b, t, g)), cmp_spec, cmp_spec,
                  pl.BlockSpec((1, tq, LANES), lambda b, g, t: (b, t, g))],
        out_specs=(pl.BlockSpec((1, tq, NSA_WIDTH // NSA_KV_GROUPS), lambda b, g, t: (b, t, g)),
                   pl.BlockSpec((1, 1, tq, LANES), lambda b, g, t: (b, g, t, 0))),
        compiler_params=_cparams(("arbitrary", "arbitrary", "arbitrary")),
        name="cmpsel",
    )(qraw, kcmp, vcmp, gate)


def _slcwin_kernel(q_ref, sel_ref, ks_ref, kw_ref, vs_ref, vw_ref, gate_ref, ocmp_ref, wgu_ref, wdn_ref,
                   o_ref, wgu_bf_ref, wdn_bf_ref, *, tk, wk):
    wgu_bf_ref[...] = wgu_ref[...].astype(BF16)
    wdn_bf_ref[...] = wdn_ref[...].astype(BF16)
    tq = q_ref.shape[1]
    S = ks_ref.shape[1]
    t = pl.program_id(2)
    q0 = t * tq
    qpos = q0 + lax.broadcasted_iota(jnp.int32, (tq, 1), 0)
    qw = [q_ref[0, :, h * LANES:(h + 1) * LANES] for h in range(NSA_HG)]
    sel = sel_ref[0, 0]
    qa = [jnp.concatenate([q, sel], axis=1) for q in qw]

    gate = gate_ref[0]

    def normalise(acc, g):
        return acc * (g / acc[:, NSA_HD:NSA_HD + 1])

    nchain = NSA_HG // SLC_STACK
    qst = [jnp.concatenate(qa[c * SLC_STACK:(c + 1) * SLC_STACK], axis=0) for c in range(nchain)]
    qpos_st = jnp.concatenate([qpos] * SLC_STACK, axis=0)

    def kstep(j, carry, diag):
        k0 = pl.multiple_of(j * tk, tk)
        ks = ks_ref[0, pl.ds(k0, tk), :]
        vs = vs_ref[0, pl.ds(k0, tk), :]
        if diag:
            vis = (k0 + lax.broadcasted_iota(jnp.int32, (1, tk), 1)) <= qpos_st
        out = []
        for c in range(nchain):
            m, acc = carry[c]
            s = lax.dot_general(qst[c], ks, NT, preferred_element_type=F32)
            if diag:
                s = jnp.where(vis, s, NEG)
            mn = jnp.maximum(m, jnp.max(s, axis=-1, keepdims=True))
            p = jnp.exp2(s - mn)
            acc = jnp.exp2(m - mn) * acc + jnp.dot(p.astype(BF16), vs, preferred_element_type=F32)
            out.append((mn, acc))
        return tuple(out)

    init = tuple((jnp.full((SLC_STACK * tq, 1), NEG, F32), jnp.zeros((SLC_STACK * tq, LANES), F32))
                 for _ in range(nchain))
    jdiag = q0 // tk
    carry = lax.fori_loop(0, jdiag, lambda j, c: kstep(j, c, False), init)
    carry = kstep(jdiag, carry, True)
    heads = [normalise(carry[h // SLC_STACK][1][(h % SLC_STACK) * tq:(h % SLC_STACK + 1) * tq],
                       gate[:, 3 * h + 1:3 * h + 2]) for h in range(NSA_HG)]

    wq = wk - WINDOW
    wins = [[] for _ in range(NSA_HG)]
    spos = lax.broadcasted_iota(jnp.int32, (NSA_HG * wq, 1), 0) % wq
    for i in range(tq // wq):
        rows = slice(i * wq, (i + 1) * wq)
        start = pl.multiple_of(jnp.maximum(q0 + (i + 1) * wq - wk, 0), wq)
        kw = kw_ref[0, pl.ds(start, wk), :]
        vw = vw_ref[0, pl.ds(start, wk), :]
        d = (q0 + i * wq + spos) - (start + lax.broadcasted_iota(jnp.int32, (1, wk), 1))
        q_st = jnp.concatenate([q[rows] for q in qw], axis=0)
        s = jnp.where((d >= 0) & (d < WINDOW), lax.dot_general(q_st, kw, NT, preferred_element_type=F32), NEG)
        p = jnp.exp2(s - jnp.max(s, axis=-1, keepdims=True))
        o_st = jnp.dot(p.astype(BF16), vw, preferred_element_type=F32)
        for h in range(NSA_HG):
            wins[h].append(o_st[h * wq:(h + 1) * wq])
    for h in range(NSA_HG):
        heads[h] = heads[h] + normalise(jnp.concatenate(wins[h], axis=0), gate[:, 3 * h + 2:3 * h + 3])
    o_ref[0] = (ocmp_ref[0] + _pack_heads(heads)).astype(BF16)


def _slcwin(qrot, sel, ks, kw, vaug, gate, ocmp, wgu, wdn, tq, tk):
    B, S, _ = qrot.shape
    gw = NSA_HG * LANES
    wk = WINDOW + WIN_Q
    nt = S // tq
    nstep = B * NSA_KV_GROUPS * nt
    assert S >= wk and tq % WIN_Q == 0 and N_EXPERTS % nstep == 0
    epb = N_EXPERTS // nstep
    ospec = pl.BlockSpec((1, tq, NSA_WIDTH // NSA_KV_GROUPS), lambda b, g, t: (b, t, g))
    wspec = lambda a: pl.BlockSpec((epb,) + a.shape[1:], lambda b, g, t: ((b * NSA_KV_GROUPS + g) * nt + t, 0, 0))
    return pl.pallas_call(
        functools.partial(_slcwin_kernel, tk=tk, wk=wk),
        out_shape=(jax.ShapeDtypeStruct((B, S, NSA_WIDTH), BF16),
                   jax.ShapeDtypeStruct(wgu.shape, BF16), jax.ShapeDtypeStruct(wdn.shape, BF16)),
        grid=(B, NSA_KV_GROUPS, nt),
        in_specs=[pl.BlockSpec((1, tq, gw), lambda b, g, t: (b, t, g)),
                  pl.BlockSpec((1, 1, tq, LANES), lambda b, g, t: (b, g, t, 0)),
                  pl.BlockSpec((1, S, 2 * LANES), lambda b, g, t: (b, 0, 0)),
                  pl.BlockSpec((1, S, LANES), lambda b, g, t: (b, 0, 0)),
                  pl.BlockSpec((1, S, LANES), lambda b, g, t: (b, 0, g)),
                  pl.BlockSpec((1, S, LANES), lambda b, g, t: (b, 0, NSA_KV_GROUPS + g)),
                  pl.BlockSpec((1, tq, LANES), lambda b, g, t: (b, t, g)),
                  ospec, wspec(wgu), wspec(wdn)],
        out_specs=(ospec, wspec(wgu), wspec(wdn)),
        compiler_params=_cparams(("arbitrary", "arbitrary", "arbitrary")),
        name="slcwin",
    )(qrot, sel, ks, kw, vaug, vaug, gate, ocmp, wgu, wdn)


def _outproj_kernel(x_ref, ohg_ref, onsa_ref, wout_ref,
                    g1_ref, sh2_ref, sc2_ref, n2_ref, rwt_ref, rb_ref,
                    x1_ref, h2_ref, sc_ref, slot_t_ref, w_t_ref, cnt_ref):
    tm = x_ref.shape[1]
    mix = jnp.dot(jnp.concatenate([ohg_ref[0], onsa_ref[0]], axis=1), wout_ref[...], preferred_element_type=F32)
    x1 = x_ref[0] + g1_ref[0] * mix
    x1_ref[0] = x1
    h2 = (x1 * lax.rsqrt(jnp.mean(x1 * x1, axis=-1, keepdims=True) + EPS) * n2_ref[...]) * (1.0 + sc2_ref[0]) + sh2_ref[0]
    h_hi = h2.astype(BF16)
    h2_ref[0] = h_hi
    h_lo = (h2 - h_hi.astype(F32)).astype(BF16)
    both = lax.dot_general(rwt_ref[...], h_hi, NT, preferred_element_type=F32)
    logits = (both[:N_EXPERTS] + both[N_EXPERTS:]
              + lax.dot_general(rwt_ref[:N_EXPERTS, :], h_lo, NT, preferred_element_type=F32))
    scores = _sigmoid(logits)
    choice = scores + rb_ref[...]
    per = N_EXPERTS // N_GROUPS
    c3 = choice.reshape(N_GROUPS, per, tm)
    erow = lax.broadcasted_iota(jnp.int32, c3.shape, 1)
    rank_in = jnp.zeros(c3.shape, F32)
    for k in range(per):
        ck = c3[:, k:k + 1, :]
        rank_in = rank_in + jnp.where((ck > c3) | ((ck == c3) & (erow > k)), 1.0, 0.0)
    grp_score = jnp.sum(jnp.where(rank_in < 2, c3, 0.0), axis=1)
    grp_keep = _rank_before(grp_score, N_GROUPS) < TOPK_GROUPS
    masked = jnp.where(grp_keep[:, None, :], c3, -jnp.inf).reshape(N_EXPERTS, tm)
    keep = _topk_rows(masked, TOP_K) > 0.5
    tw = jnp.where(keep, scores, 0.0)
    tw = tw / jnp.sum(tw, axis=0, keepdims=True) * ROUTED_SCALE
    before = lax.broadcasted_iota(jnp.int32, (tm, tm), 0)
    after = lax.broadcasted_iota(jnp.int32, (tm, tm), 1)
    earlier = (before < after) & (before // MOE_SUB == after // MOE_SUB)
    kept = jnp.where(keep, 1.0, 0.0)
    pref = jnp.dot(kept.astype(BF16), earlier.astype(BF16), preferred_element_type=F32)
    slot = jnp.where(keep, pref, -1.0)
    assert MOE_SUB <= 256
    sc_ref[0] = _transpose_small_ints(slot)
    for j in range(tm // MOE_SUB):
        sub = slice(j * MOE_SUB, (j + 1) * MOE_SUB)
        slot_t_ref[:, j] = slot[:, sub].reshape(N_EXPERTS // MOE_EG, MOE_EG, MOE_SUB)
        w_t_ref[:, j] = tw[:, sub].reshape(N_EXPERTS // MOE_EG, MOE_EG, MOE_SUB)
        cnt_ref[j] = jnp.broadcast_to(jnp.sum(kept[:, sub], axis=1, keepdims=True), (N_EXPERTS, LANES))


def _outproj(x, ohg, onsa, wout, mod3, n2, rwt, rb, tm):
    B, S, D = x.shape
    blk = lambda w: pl.BlockSpec((1, tm, w), lambda b, i: (b, i, 0))
    full = lambda a: pl.BlockSpec(a.shape, lambda b, i: (0,) * a.ndim)
    modc = lambda k: pl.BlockSpec((1, 1, D), lambda b, i, k=k: (b, 0, k))
    nt = S // tm
    ng = N_EXPERTS // MOE_EG
    nh = tm // MOE_SUB
    tspec = pl.BlockSpec((ng, nh, MOE_EG, MOE_SUB), lambda b, i: (0, b * nt + i, 0, 0))
    return pl.pallas_call(
        _outproj_kernel,
        out_shape=(jax.ShapeDtypeStruct((B, S, D), F32),
                   jax.ShapeDtypeStruct((B, S, D), BF16),
                   jax.ShapeDtypeStruct((B, S, N_EXPERTS), F32),
                   jax.ShapeDtypeStruct((ng, B * nt * nh, MOE_EG, MOE_SUB), F32),
                   jax.ShapeDtypeStruct((ng, B * nt * nh, MOE_EG, MOE_SUB), F32),
                   jax.ShapeDtypeStruct((B * nt * nh, N_EXPERTS, LANES), F32)),
        grid=(B, nt),
        in_specs=[blk(D), blk(HG_WIDTH), blk(NSA_WIDTH), full(wout),
                  modc(2), modc(3), modc(4), full(n2), full(rwt), full(rb)],
        out_specs=(blk(D), blk(D), blk(N_EXPERTS), tspec, tspec,
                   pl.BlockSpec((nh, N_EXPERTS, LANES), lambda b, i: (b * nt + i, 0, 0))),
        compiler_params=_cparams(("arbitrary", "arbitrary")),
        name="outproj",
    )(x, ohg, onsa, wout, mod3, mod3, mod3, n2, rwt, rb)


def _swiglu_hidden(x, wgu):
    gu = jnp.dot(x, wgu, preferred_element_type=F32)
    return _silu(gu[:, :EXPERT_HIDDEN]) * gu[:, EXPERT_HIDDEN:]


def _moe_kernel(rounds_ref, h_ref, slot_t_ref, w_t_ref, slot_c_ref, wgu_ref, wdn_ref, sgu_ref, sdn_ref,
                o_ref, x_scr, p_scr):
    i = pl.program_id(0)
    g = pl.program_id(1)
    tm = h_ref.shape[0]
    nsub = tm // MOE_SUB
    rnd, eg = MOE_ROUND, MOE_EG
    ng = N_EXPERTS // eg
    base = i * (N_EXPERTS + ng)

    @pl.when(g == 0)
    def _():
        act = _swiglu_hidden(h_ref[...], sgu_ref[...]).astype(BF16)
        o_ref[...] = jnp.dot(act, sdn_ref[...], preferred_element_type=F32)

    blk = eg * rnd
    rslot = lax.broadcasted_iota(jnp.int32, (rnd, 1), 0).astype(F32)

    def gather(first, nround):
        for s in range(nsub):
            st = slot_t_ref[0, s] - first
            p = jnp.concatenate([jnp.where(rslot + float(w * rnd) == st[e:e + 1, :], 1.0, 0.0)
                                 for w in range(nround) for e in range(eg)], axis=0)
            p_scr[s, 0:nround * blk, :] = p.astype(BF16)
            x_scr[s, 0:nround * blk, :] = jnp.dot(p_scr[s, 0:nround * blk, :], h_ref[s * MOE_SUB:(s + 1) * MOE_SUB, :],
                                                  preferred_element_type=F32).astype(BF16)

    def run_expert(e, w):
        rows = slice(w * blk + e * rnd, w * blk + (e + 1) * rnd)
        xe = jnp.concatenate([x_scr[s, rows, :] for s in range(nsub)], axis=0)
        wslot = jnp.concatenate(
            [jnp.sum(p_scr[s, rows, :].astype(F32) * w_t_ref[0, s][e:e + 1, :], axis=-1, keepdims=True)
             for s in range(nsub)], axis=0)
        y = jnp.dot((_swiglu_hidden(xe, wgu_ref[e]) * wslot).astype(BF16), wdn_ref[e], preferred_element_type=F32)
        for s in range(nsub):
            x_scr[s, rows, :] = y[s * rnd:(s + 1) * rnd].astype(BF16)

    def run_if(cond, e, w):
        pl.when(cond)(functools.partial(run_expert, e, w))

    def scatter(first, nround):
        lane = lax.broadcasted_iota(jnp.int32, (1, nround * blk), 1)
        lane_slot = ((lane // blk) * rnd + lane % rnd).astype(F32)
        spread = ((lax.broadcasted_iota(jnp.int32, (N_EXPERTS, nround * blk), 1) % blk) // rnd + g * eg
                  == lax.broadcasted_iota(jnp.int32, (N_EXPERTS, nround * blk), 0)).astype(BF16)
        for s in range(nsub):
            toks = slice(s * MOE_SUB, (s + 1) * MOE_SUB)
            sc = jnp.dot(slot_c_ref[toks, :].astype(BF16), spread, preferred_element_type=F32) - first
            pt = jnp.where(sc == lane_slot, 1.0, 0.0).astype(BF16)
            o_ref[toks, :] += jnp.dot(pt, x_scr[s, 0:nround * blk, :], preferred_element_type=F32)

    gather(0.0, MOE_FIRST)
    for e in range(eg):
        run_expert(e, 0)
    for w in range(1, MOE_FIRST):
        for e in range(eg):
            run_if(rounds_ref[base + g * eg + e] > w, e, w)
    scatter(0.0, MOE_FIRST)

    def later_round(r, carry):
        first = (r * rnd).astype(F32)
        gather(first, 1)
        for e in range(eg):
            run_if(rounds_ref[base + g * eg + e] > r, e, 0)
        scatter(first, 1)
        return carry

    lax.fori_loop(MOE_FIRST, rounds_ref[base + N_EXPERTS + g], later_round, 0)


def _moe(rounds, h2, slot_t, w_t, slot_c, wgu, wdn, sgu, sdn, tm):
    T, D = h2.shape
    eg, nsub = MOE_EG, tm // MOE_SUB
    full = lambda a: pl.BlockSpec(a.shape, lambda i, e, o: (0,) * a.ndim)
    tspec = pl.BlockSpec((1, nsub, eg, MOE_SUB), lambda i, e, o: (e, i, 0, 0))
    return pl.pallas_call(
        _moe_kernel,
        out_shape=jax.ShapeDtypeStruct((T, D), F32),
        grid_spec=pltpu.PrefetchScalarGridSpec(
            num_scalar_prefetch=1,
            grid=(T // tm, N_EXPERTS // eg),
            in_specs=[pl.BlockSpec((tm, D), lambda i, e, o: (i, 0), pipeline_mode=pl.Buffered(1)), tspec, tspec,
                      pl.BlockSpec((tm, N_EXPERTS), lambda i, e, o: (i, 0)),
                      pl.BlockSpec((eg, D, 2 * EXPERT_HIDDEN), lambda i, e, o: (e, 0, 0)),
                      pl.BlockSpec((eg, EXPERT_HIDDEN, D), lambda i, e, o: (e, 0, 0)),
                      full(sgu), full(sdn)],
            out_specs=pl.BlockSpec((tm, D), lambda i, e, o: (i, 0), pipeline_mode=pl.Buffered(1)),
            scratch_shapes=[pltpu.VMEM((nsub, MOE_FIRST * eg * MOE_ROUND, D), BF16),
                            pltpu.VMEM((nsub, MOE_FIRST * eg * MOE_ROUND, MOE_SUB), BF16)]),
        compiler_params=_cparams(("arbitrary", "arbitrary")),
        name="moe",
    )(rounds, h2, slot_t, w_t, slot_c, wgu, wdn, sgu, sdn)


def _final_kernel(x1_ref, moe_ref, g2_ref, fg_ref, o_ref):
    x2 = x1_ref[0] + g2_ref[0] * moe_ref[0]
    o_ref[0] = x2 * lax.rsqrt(jnp.mean(x2 * x2, axis=-1, keepdims=True) + EPS) * fg_ref[...]


def _final(x1, moe, mod3, fg, tm):
    B, S, D = x1.shape
    blk = pl.BlockSpec((1, tm, D), lambda b, i: (b, i, 0))
    return pl.pallas_call(
        _final_kernel,
        out_shape=jax.ShapeDtypeStruct((B, S, D), F32),
        grid=(B, S // tm),
        in_specs=[blk, blk, pl.BlockSpec((1, 1, D), lambda b, i: (b, 0, 5)), pl.BlockSpec((1, D), lambda b, i: (0, 0))],
        out_specs=blk,
        compiler_params=_cparams(("arbitrary", "arbitrary")),
        name="final",
    )(x1, moe, mod3, fg)


def _split_bf16(w):
    hi = w.astype(BF16)
    return jnp.concatenate([hi, (w - hi.astype(F32)).astype(BF16)], axis=0)


def _pack_w_in(w_in):
    return jnp.pad(w_in, ((0, 0), (0, IN_COLS_P - w_in.shape[1]))).astype(BF16)


def _pack_cmp(pos, w1, b1, w2, lane_by_group):
    half = CMP_STRIDE * NSA_HD
    def rows_for(wh):
        w3 = wh.reshape(CMP_STRIDE, NSA_HD, CMP_HIDDEN)
        z = jnp.zeros_like(w3)
        return jnp.stack([jnp.concatenate([w3, z], axis=1).reshape(CMP_STRIDE * LANES, CMP_HIDDEN),
                          jnp.concatenate([z, w3], axis=1).reshape(CMP_STRIDE * LANES, CMP_HIDDEN)])
    w1a = rows_for(w1[:half]).astype(BF16)
    w1b = rows_for(w1[half:]).astype(BF16)
    z2 = jnp.zeros_like(w2)
    w2_first = jnp.concatenate([w2, z2], axis=1)
    w2p = jnp.stack([w2_first, jnp.concatenate([z2, w2], axis=1) if lane_by_group else w2_first]).astype(BF16)
    pe8 = jnp.pad(pos.reshape(1, CMP_BLOCK * NSA_HD), ((0, 7), (0, 0)))
    return w1a, w1b, pe8, w1, b1.reshape(1, CMP_HIDDEN), w2p


def _rope_tables(S):
    half = NSA_HD // 2
    inv = ROPE_THETA ** (-jnp.arange(half, dtype=F32) / half)
    ang = jnp.arange(S, dtype=F32)[:, None] * inv[None, :]
    cos, sin = jnp.cos(ang), jnp.sin(ang)
    reps = LANES // NSA_HD
    return jnp.tile(jnp.concatenate([cos, cos], axis=1), (1, reps)), jnp.tile(jnp.concatenate([-sin, sin], axis=1), (1, reps))


def _tiles(S):
    return dict(inproj=min(512, S), hgrn=min(512, S), cmpsel=min(512, S), slc_q=512, slc_k=512, moe=min(2048, S), final=min(512, S))


def kernel(x, c, w_ada, b_ada, norm1_g, w_in, hg_lb_logits, hg_norm_g, cmp_pos_k, cmp_w1_k, cmp_b1_k, cmp_w2_k,
           cmp_pos_v, cmp_w1_v, cmp_b1_v, cmp_w2_v, w_out, norm2_g, router_w, router_bias, w_exp_gu, w_exp_dn,
           w_sh_gu, w_sh_dn, final_g):
    B, S, D = x.shape
    assert D == D_MODEL and w_ada.shape[0] == 1 and S % 512 == 0
    tl = _tiles(S)
    l = 0
    lb = jnp.cumsum(jax.nn.softmax(hg_lb_logits.astype(F32), axis=0), axis=0)[l].reshape(1, HG_WIDTH)
    c8 = jnp.pad(c, ((0, 8 - B), (0, 0)))
    mod3 = _ada(c8, w_ada[l], b_ada[l].reshape(1, -1))[:B].reshape(B, 1, 6 * D)
    cos, sin = _rope_tables(S)
    hg, qraw, qrot, kc, vc, ks, kw, vaug, gate = _inproj(x, mod3, norm1_g[l].reshape(1, D), _pack_w_in(w_in[l]),
                                                         cos, sin, tl["inproj"])
    ng = hg_norm_g[l].reshape(1, HG_WIDTH)
    mxu_safe = -HG_SUB * jnp.log(jnp.min(lb)) <= HG_SAFE_LOG_RANGE
    ohg = lax.cond(mxu_safe, lambda: _hgrn_mxu(hg, lb, ng, tl["hgrn"]), lambda: _hgrn(hg, lb, ng))
    ncb = S // CMP_STRIDE
    kcmp = _cmpmlp(kc.reshape(B, ncb, CMP_STRIDE * LANES),
                   *_pack_cmp(cmp_pos_k[l], cmp_w1_k[l], cmp_b1_k[l], cmp_w2_k[l], True))
    vcmp = _cmpmlp(vc.reshape(B, ncb, CMP_STRIDE * LANES),
                   *_pack_cmp(cmp_pos_v[l], cmp_w1_v[l], cmp_b1_v[l], cmp_w2_v[l], False))
    ocmp, sel = _cmpsel(qraw, kcmp, vcmp, gate, tl["cmpsel"])
    onsa, wgu_bf, wdn_bf = _slcwin(qrot, sel, ks, kw, vaug, gate, ocmp, w_exp_gu[l], w_exp_dn[l],
                                   tl["slc_q"], tl["slc_k"])
    x1, h2, sc, slot_t, w_t, cnt = _outproj(x, ohg, onsa, w_out[l].astype(BF16), mod3,
                                            norm2_g[l].reshape(1, D), _split_bf16(router_w[l].T),
                                            router_bias[l].reshape(N_EXPERTS, 1), ROUTE_TM)
    T = B * S
    tm = tl["moe"]
    ng = N_EXPERTS // MOE_EG
    slot_c = sc.reshape(T, N_EXPERTS)
    load = jnp.max(cnt[:, :, 0].reshape(T // tm, tm // MOE_SUB, N_EXPERTS), axis=1).astype(jnp.int32)
    per_expert = (load + MOE_ROUND - 1) // MOE_ROUND
    rounds = jnp.concatenate([per_expert, jnp.max(per_expert.reshape(T // tm, ng, MOE_EG), axis=2)], axis=1).reshape(-1)
    moe = _moe(rounds, h2.reshape(T, D), slot_t, w_t, slot_c,
               wgu_bf, wdn_bf, w_sh_gu[l].astype(BF16), w_sh_dn[l].astype(BF16), tm)
    return _final(x1, moe.reshape(B, S, D), mod3, final_g.reshape(1, D), tl["final"])
```

```python
import functools

import jax
import jax.numpy as jnp
from jax import lax
from jax.experimental import pallas as pl
from jax.experimental.pallas import tpu as pltpu

F32 = jnp.float32
BF16 = jnp.bfloat16
HIGHEST = lax.Precision.HIGHEST

D_MODEL = 1024
EPS = 1e-6
HG_HEADS = 4
HG_DK = 128
HG_DV = 128
HG_WIDTH = HG_HEADS * HG_DV
HG_CHUNK = 64
HG_SUB = 16
HG_SAFE_LOG_RANGE = 80.0
NSA_HEADS = 8
NSA_KV_GROUPS = 2
NSA_HG = NSA_HEADS // NSA_KV_GROUPS
NSA_HD = 64
NSA_WIDTH = NSA_HEADS * NSA_HD
NSA_KV = NSA_KV_GROUPS * NSA_HD
CMP_BLOCK = 32
CMP_STRIDE = 16
CMP_HIDDEN = 256
SEL_BLOCK = 64
N_SEL = 16
WINDOW = 512
ROPE_THETA = 10000.0
SLC_STACK = 4
WIN_Q = 256
N_EXPERTS = 64
TOP_K = 8
N_GROUPS = 8
TOPK_GROUPS = 4
EXPERT_HIDDEN = 256
ROUTED_SCALE = 2.5
ROUTE_TM = 512
MOE_SUB = 256
MOE_ROUND = 32
MOE_EG = 8
MOE_FIRST = 3

LANES = 128
NEG = -1e30
VMEM_LIMIT = 61 * 1024 * 1024

C_HG = 0
C_Q = 4 * HG_WIDTH
C_K = C_Q + NSA_WIDTH
C_GATE = C_K + 6 * NSA_KV
IN_COLS_P = C_GATE + LANES
LOG2E = 1.4426950408889634

NT = (((1,), (1,)), ((), ()))
TN = (((0,), (0,)), ((), ()))


def _cparams(sem):
    return pltpu.CompilerParams(dimension_semantics=sem, vmem_limit_bytes=VMEM_LIMIT)


def _sigmoid(x):
    return 1.0 / (1.0 + jnp.exp(-x))


def _silu(x):
    return x * _sigmoid(x)


def _gelu_tanh(x):
    return 0.5 * x * (1.0 + jnp.tanh(0.7978845608028654 * (x + 0.044715 * (x * x * x))))


def _ada_kernel(c_ref, w_ref, b_ref, o_ref):
    a = _silu(c_ref[...])
    o_ref[...] = jnp.dot(a, w_ref[...], precision=HIGHEST, preferred_element_type=F32) + b_ref[...]


def _ada(c8, w, b):
    n = w.shape[1]
    tn = 1024
    return pl.pallas_call(
        _ada_kernel,
        out_shape=jax.ShapeDtypeStruct((8, n), F32),
        grid=(n // tn,),
        in_specs=[pl.BlockSpec((8, D_MODEL), lambda j: (0, 0)),
                  pl.BlockSpec((D_MODEL, tn), lambda j: (0, j)),
                  pl.BlockSpec((1, tn), lambda j: (0, j))],
        out_specs=pl.BlockSpec((8, tn), lambda j: (0, j)),
        compiler_params=_cparams(("arbitrary",)),
        name="ada",
    )(c8, w, b)


def _rope(t, cos, sin_signed, first_half):
    rot = jnp.where(first_half, pltpu.roll(t, 96, 1), pltpu.roll(t, 32, 1))
    return t * cos + rot * sin_signed


def _inproj_kernel(x_ref, sh_ref, sc_ref, g_ref, w_ref, cos_ref, sin_ref,
                   hg_ref, qraw_ref, qrot_ref, kc_ref, vc_ref, ks_ref, kw_ref, va_ref, gate_ref, h_scr):
    tm = x_ref.shape[1]
    x = x_ref[0]
    y = x * lax.rsqrt(jnp.mean(x * x, axis=-1, keepdims=True) + EPS) * g_ref[...]
    h_scr[...] = (y * (1.0 + sc_ref[0]) + sh_ref[0]).astype(BF16)

    def mm(lo, width):
        return jnp.dot(h_scr[...], w_ref[:, lo:lo + width], preferred_element_type=F32)

    cos = cos_ref[...]
    sin = sin_ref[...]
    first_half = (lax.broadcasted_iota(jnp.int32, cos.shape, 1) % NSA_HD) < (NSA_HD // 2)
    lane = lax.broadcasted_iota(jnp.int32, (tm, LANES), 1)
    half_of_lane = lane // NSA_HD

    def own_lanes(t, src_half, dst_half):
        moved = t if src_half == dst_half else pltpu.roll(t, NSA_HD, 1)
        return jnp.where(half_of_lane == dst_half, moved, 0.0)

    for j in range(4):
        hg_ref[0, :, j * HG_WIDTH:(j + 1) * HG_WIDTH] = mm(C_HG + j * HG_WIDTH, HG_WIDTH)
    for m in range(NSA_WIDTH // (2 * LANES)):
        qq = mm(C_Q + m * 2 * LANES, 2 * LANES) * (NSA_HD ** -0.5)
        for c in range(2):
            q = qq[:, c * LANES:(c + 1) * LANES]
            qr = _rope(q, cos, sin, first_half) * LOG2E
            for half in range(2):
                n = (2 * m + c) * 2 + half
                g = n // NSA_HG
                qraw_ref[0, :, n * LANES:(n + 1) * LANES] = own_lanes(q, half, g).astype(BF16)
                qrot_ref[0, :, n * LANES:(n + 1) * LANES] = own_lanes(qr, half, g).astype(BF16)
    kv = mm(C_K, 2 * LANES)
    kc_ref[0] = kv[:, :LANES].astype(BF16)
    vc_ref[0] = kv[:, LANES:].astype(BF16)
    pos = pl.program_id(1) * tm + lax.broadcasted_iota(jnp.int32, (tm, LANES), 0)
    ks_ref[0, :, LANES:2 * LANES] = jnp.where(lane == pos // SEL_BLOCK, 1.0, 0.0).astype(BF16)
    for j in range(2):
        kv = mm(C_K + (j + 1) * 2 * LANES, 2 * LANES)
        k_rot = _rope(kv[:, :LANES], cos, sin, first_half).astype(BF16)
        if j == 0:
            ks_ref[0, :, 0:LANES] = k_rot
        else:
            kw_ref[0] = k_rot
        for g in range(NSA_KV_GROUPS):
            v = own_lanes(kv[:, LANES:], g, 0)
            col = (j * NSA_KV_GROUPS + g) * LANES
            va_ref[0, :, col:col + LANES] = jnp.where(lane == NSA_HD, 1.0, v).astype(BF16)
    gate = _sigmoid(mm(C_GATE, LANES))
    per_group = 3 * NSA_HG
    for g in range(NSA_KV_GROUPS):
        gate_ref[0, :, g * LANES:(g + 1) * LANES] = gate if g == 0 else pltpu.roll(gate, LANES - g * per_group, 1)


def _inproj(x, mod3, norm_g, w_p, cos, sin, tm):
    B, S, D = x.shape
    blk = lambda w: pl.BlockSpec((1, tm, w), lambda b, i: (b, i, 0))
    return pl.pallas_call(
        _inproj_kernel,
        out_shape=(jax.ShapeDtypeStruct((B, S, 4 * HG_WIDTH), F32),
                   jax.ShapeDtypeStruct((B, S, NSA_HEADS * LANES), BF16),
                   jax.ShapeDtypeStruct((B, S, NSA_HEADS * LANES), BF16),
                   jax.ShapeDtypeStruct((B, S, LANES), BF16),
                   jax.ShapeDtypeStruct((B, S, LANES), BF16),
                   jax.ShapeDtypeStruct((B, S, 2 * LANES), BF16),
                   jax.ShapeDtypeStruct((B, S, LANES), BF16),
                   jax.ShapeDtypeStruct((B, S, 4 * LANES), BF16),
                   jax.ShapeDtypeStruct((B, S, NSA_KV_GROUPS * LANES), F32)),
        grid=(B, S // tm),
        in_specs=[blk(D),
                  pl.BlockSpec((1, 1, D), lambda b, i: (b, 0, 0)),
                  pl.BlockSpec((1, 1, D), lambda b, i: (b, 0, 1)),
                  pl.BlockSpec((1, D), lambda b, i: (0, 0)),
                  pl.BlockSpec((D, IN_COLS_P), lambda b, i: (0, 0)),
                  pl.BlockSpec((tm, LANES), lambda b, i: (i, 0)),
                  pl.BlockSpec((tm, LANES), lambda b, i: (i, 0))],
        out_specs=(blk(4 * HG_WIDTH), blk(NSA_HEADS * LANES), blk(NSA_HEADS * LANES),
                   blk(LANES), blk(LANES), blk(2 * LANES), blk(LANES), blk(4 * LANES), blk(NSA_KV_GROUPS * LANES)),
        scratch_shapes=[pltpu.VMEM((tm, D), BF16)],
        compiler_params=_cparams(("arbitrary", "arbitrary")),
        name="inproj",
    )(x, mod3, mod3, norm_g, w_p, cos, sin)


def _hgrn_kernel(q_ref, f_ref, i_ref, gt_ref, lb_ref, ng_ref, o_ref):
    S = q_ref.shape[1]
    C, U = HG_CHUNK, HG_SUB
    lb = lb_ref[...]
    ng = ng_ref[...]
    ri = lax.broadcasted_iota(jnp.int32, (C, C), 0)
    ci = lax.broadcasted_iota(jnp.int32, (C, C), 1)
    tril = (ri >= ci).astype(F32)
    trow = lax.broadcasted_iota(jnp.int32, (U, 1), 0)

    def chunk(c, st_t):
        r0 = pl.multiple_of(c * C, C)
        rows = pl.ds(r0, C)
        f = lb + (1.0 - lb) * _sigmoid(f_ref[0, rows, :])
        kk = 1.0 - f
        b = jnp.dot(tril, jnp.log(f), precision=HIGHEST, preferred_element_type=F32)
        q = q_ref[0, rows, :] * (HG_DK ** -0.5)
        v = i_ref[0, rows, :]
        vb = v.astype(BF16)
        o_inter = lax.dot_general((q * jnp.exp(b)).astype(BF16), st_t.astype(BF16), NT,
                                  preferred_element_type=F32)
        parts = []
        for i in range(C // U):
            lo = i * U
            bi = b[lo:lo + U]
            qi = q[lo:lo + U]
            if i == 0:
                oi = jnp.zeros((U, HG_DV), F32)
            else:
                r = b[lo - 1:lo]
                qrel = (qi * jnp.exp(bi - r)).astype(BF16)
                kprev = (kk[:lo] * jnp.exp(r - b[:lo])).astype(BF16)
                a_off = lax.dot_general(qrel, kprev, NT, preferred_element_type=F32)
                oi = jnp.dot(a_off.astype(BF16), vb[:lo], preferred_element_type=F32)
            for s in range(U):
                valid = trow >= s
                e = jnp.exp(jnp.where(valid, bi - bi[s:s + 1], 0.0))
                a = jnp.sum(qi * e * kk[lo + s:lo + s + 1], axis=-1, keepdims=True)
                oi = oi + jnp.where(valid, a, 0.0) * v[lo + s:lo + s + 1]
            parts.append(oi)
        o = o_inter + jnp.concatenate(parts, axis=0)
        o = o * lax.rsqrt(jnp.mean(o * o, axis=-1, keepdims=True) + EPS) * ng
        o_ref[0, rows, :] = (o * _silu(gt_ref[0, rows, :])).astype(BF16)
        bl = b[C - 1:C]
        kv_t = lax.dot_general(vb, (kk * jnp.exp(bl - b)).astype(BF16), TN, preferred_element_type=F32)
        return jnp.exp(bl) * st_t + kv_t

    lax.fori_loop(0, S // C, chunk, jnp.zeros((HG_DV, HG_DK), F32))


def _cumsum_rows(x):
    n = x.shape[0]
    row = lax.broadcasted_iota(jnp.int32, x.shape, 0)
    d = 1
    while d < n:
        x = x + jnp.where(row >= d, pltpu.roll(x, d, 0), 0.0)
        d *= 2
    return x


def _hgrn_mxu_kernel(hg_ref, lb_ref, ng_ref, o_ref, st_scr):
    ts = hg_ref.shape[1]
    C, U = HG_CHUNK, HG_SUB

    @pl.when(pl.program_id(1) == 0)
    def _():
        st_scr[...] = jnp.zeros(st_scr.shape, F32)

    W = HG_WIDTH
    NH = HG_HEADS
    head_of_lane = lax.broadcasted_iota(jnp.int32, (1, W), 1) // HG_DK
    hcols = [slice(h * HG_DK, (h + 1) * HG_DK) for h in range(NH)]

    def chunk(c, carry):
        rows = pl.ds(pl.multiple_of(c * C, C), C)
        lb = lb_ref[...]
        f = lb + (1.0 - lb) * _sigmoid(hg_ref[0, rows, W:2 * W])
        kk = 1.0 - f
        b = _cumsum_rows(jnp.log(f))
        q = hg_ref[0, rows, 0:W] * (HG_DK ** -0.5)
        vb = hg_ref[0, rows, 2 * W:3 * W].astype(BF16)
        qe = (q * jnp.exp(b)).astype(BF16)
        o_inter = jnp.concatenate(
            [lax.dot_general(qe[:, hc], st_scr[h].astype(BF16), NT, preferred_element_type=F32)
             for h, hc in enumerate(hcols)], axis=1)
        parts = []
        for i in range(C // U):
            lo, hi = i * U, (i + 1) * U
            r = b[lo - 1:lo] if i else jnp.zeros((1, W), F32)
            qrel = q[lo:hi] * jnp.exp(b[lo:hi] - r)
            kall = (kk[:hi] * jnp.exp(r - b[:hi])).astype(BF16)
            qbd = jnp.concatenate([jnp.where(head_of_lane == h, qrel, 0.0) for h in range(NH)], axis=0)
            a = lax.dot_general(qbd.astype(BF16), kall, NT, preferred_element_type=F32)
            trow = lax.broadcasted_iota(jnp.int32, (NH * U, hi), 0) % U
            a = jnp.where(lax.broadcasted_iota(jnp.int32, (NH * U, hi), 1) <= lo + trow, a, 0.0)
            oa = jnp.dot(a.astype(BF16), vb[:hi], preferred_element_type=F32)
            oi = jnp.where(head_of_lane == 0, oa[0:U], 0.0)
            for h in range(1, NH):
                oi = jnp.where(head_of_lane == h, oa[h * U:(h + 1) * U], oi)
            parts.append(oi)
        o = o_inter + jnp.concatenate(parts, axis=0)
        o = jnp.concatenate(
            [o[:, hc] * lax.rsqrt(jnp.mean(o[:, hc] * o[:, hc], axis=-1, keepdims=True) + EPS) for hc in hcols], axis=1)
        o_ref[0, rows, :] = (o * ng_ref[...] * _silu(hg_ref[0, rows, 3 * W:4 * W])).astype(BF16)
        bl = b[C - 1:C]
        ke = (kk * jnp.exp(bl - b)).astype(BF16)
        decay = jnp.exp(bl)
        for h, hc in enumerate(hcols):
            kv_t = lax.dot_general(vb[:, hc], ke[:, hc], TN, preferred_element_type=F32)
            st_scr[h] = decay[:, hc] * st_scr[h] + kv_t
        return carry

    lax.fori_loop(0, ts // C, chunk, 0, unroll=8)


def _hgrn_mxu(hg, lb, ng, ts):
    B, S, _ = hg.shape
    vec = pl.BlockSpec((1, HG_WIDTH), lambda b, i: (0, 0))
    return pl.pallas_call(
        _hgrn_mxu_kernel,
        out_shape=jax.ShapeDtypeStruct((B, S, HG_WIDTH), BF16),
        grid=(B, S // ts),
        in_specs=[pl.BlockSpec((1, ts, 4 * HG_WIDTH), lambda b, i: (b, i, 0)), vec, vec],
        out_specs=pl.BlockSpec((1, ts, HG_WIDTH), lambda b, i: (b, i, 0)),
        scratch_shapes=[pltpu.VMEM((HG_HEADS, HG_DV, HG_DK), F32)],
        compiler_params=_cparams(("arbitrary", "arbitrary")),
        name="hgrn_mxu",
    )(hg, lb, ng)


def _hgrn(hg, lb, ng):
    B, S, _ = hg.shape
    col = lambda k: pl.BlockSpec((1, S, HG_DK), lambda b, h, k=k: (b, 0, k * HG_HEADS + h))
    vec = pl.BlockSpec((1, HG_DK), lambda b, h: (0, h))
    return pl.pallas_call(
        _hgrn_kernel,
        out_shape=jax.ShapeDtypeStruct((B, S, HG_WIDTH), BF16),
        grid=(B, HG_HEADS),
        in_specs=[col(0), col(1), col(2), col(3), vec, vec],
        out_specs=pl.BlockSpec((1, S, HG_DV), lambda b, h: (b, 0, h)),
        compiler_params=_cparams(("arbitrary", "arbitrary")),
        name="hgrn",
    )(hg, hg, hg, hg, lb, ng)


def _cmpmlp_kernel(x_ref, w1a_ref, w1b_ref, pe_ref, w1_ref, b1_ref, w2_ref, o_ref):
    x = x_ref[0]
    hb = jnp.dot(pe_ref[...], w1_ref[...], precision=HIGHEST, preferred_element_type=F32)[0:1] + b1_ref[...]
    nrow = x.shape[0]
    for g in range(NSA_KV_GROUPS):
        a = jnp.dot(x, w1a_ref[g], preferred_element_type=F32)
        bm = jnp.dot(x, w1b_ref[g], preferred_element_type=F32)
        hdn = a + pltpu.roll(bm, nrow - 1, 0) + hb
        o_ref[0, g] = jnp.dot(_gelu_tanh(hdn).astype(BF16), w2_ref[g], preferred_element_type=F32).astype(BF16)


def _cmpmlp(x2, w1a, w1b, pe8, w1, b1, w2p):
    B, ncb, width = x2.shape
    full = lambda a: pl.BlockSpec(a.shape, lambda b: (0,) * a.ndim)
    return pl.pallas_call(
        _cmpmlp_kernel,
        out_shape=jax.ShapeDtypeStruct((B, NSA_KV_GROUPS, ncb, LANES), BF16),
        grid=(B,),
        in_specs=[pl.BlockSpec((1, ncb, width), lambda b: (b, 0, 0)),
                  full(w1a), full(w1b), full(pe8), full(w1), full(b1), full(w2p)],
        out_specs=pl.BlockSpec((1, NSA_KV_GROUPS, ncb, LANES), lambda b: (b, 0, 0, 0)),
        compiler_params=_cparams(("arbitrary",)),
        name="cmpmlp",
    )(x2, w1a, w1b, pe8, w1, b1, w2p)


def _rank_before(score, nrows):
    jrow = lax.broadcasted_iota(jnp.int32, score.shape, 0)
    rank = jnp.zeros(score.shape, F32)
    for k in range(nrows):
        rk = score[k:k + 1]
        beats = (rk > score) | ((rk == score) & (jrow > k))
        rank = rank + jnp.where(beats, 1.0, 0.0)
    return rank


def _topk_rows(score, k):
    n = score.shape[0]
    row = lax.broadcasted_iota(jnp.int32, score.shape, 0).astype(F32)
    keep = jnp.zeros(score.shape, F32)
    for _ in range(k):
        top = jnp.max(score, axis=0, keepdims=True)
        first = jnp.min(jnp.where(score == top, row, float(n)), axis=0, keepdims=True)
        pick = row == first
        keep = jnp.where(pick, 1.0, keep)
        score = jnp.where(pick, -jnp.inf, score)
    return keep


def _transpose_small_ints(xt):
    c = xt.shape[1]
    eye = (lax.broadcasted_iota(jnp.int32, (c, c), 0) == lax.broadcasted_iota(jnp.int32, (c, c), 1)).astype(BF16)
    return lax.dot_general(eye, xt.astype(BF16), NT, preferred_element_type=F32)


def _pack_heads(heads):
    low = lax.broadcasted_iota(jnp.int32, heads[0].shape, 1) < NSA_HD
    return jnp.concatenate([jnp.where(low, heads[k], pltpu.roll(heads[k + 1], NSA_HD, 1))
                            for k in range(0, len(heads), 2)], axis=1)


def _cmpsel_kernel(q_ref, kc_ref, vc_ref, gate_ref, o_ref, sel_ref):
    tq = q_ref.shape[1]
    ncb = kc_ref.shape[2]
    t = pl.program_id(2)
    kc = kc_ref[0, 0]
    vc = vc_ref[0, 0]
    pos = t * tq + lax.broadcasted_iota(jnp.int32, (tq, 1), 0)
    cblk = lax.broadcasted_iota(jnp.int32, (1, ncb), 1)
    vis = (cblk * CMP_STRIDE + CMP_BLOCK - 1) <= pos
    psum = jnp.zeros((tq, ncb), F32)
    gate = gate_ref[0]
    heads = []
    for h in range(NSA_HG):
        s = lax.dot_general(q_ref[0, :, h * LANES:(h + 1) * LANES], kc, NT, preferred_element_type=F32)
        s = jnp.where(vis, s, NEG)
        p = jnp.exp(s - jnp.max(s, axis=-1, keepdims=True))
        p = jnp.where(vis, p / jnp.sum(p, axis=-1, keepdims=True), 0.0)
        heads.append(jnp.dot(p.astype(BF16), vc, preferred_element_type=F32) * gate[:, 3 * h:3 * h + 1])
        psum = psum + p
    o_ref[0] = _pack_heads(heads)
    nsb = ncb * CMP_STRIDE // SEL_BLOCK
    jb = lax.broadcasted_iota(jnp.int32, (nsb, ncb), 0) * SEL_BLOCK
    cb = lax.broadcasted_iota(jnp.int32, (nsb, ncb), 1) * CMP_STRIDE
    ov = jnp.maximum(jnp.minimum(cb + CMP_BLOCK, jb + SEL_BLOCK) - jnp.maximum(cb, jb), 0).astype(F32) / CMP_BLOCK
    ps_hi = psum.astype(BF16)
    ps_lo = (psum - ps_hi.astype(F32)).astype(BF16)
    ovb = ov.astype(BF16)
    pslc_t = (lax.dot_general(ovb, ps_hi, NT, preferred_element_type=F32)
              + lax.dot_general(ovb, ps_lo, NT, preferred_element_type=F32))
    posl = t * tq + lax.broadcasted_iota(jnp.int32, (1, tq), 1)
    cur = posl // SEL_BLOCK
    jrow = lax.broadcasted_iota(jnp.int32, (nsb, tq), 0)
    forced = (jrow == 0) | (jrow == cur) | (jrow == cur - 1)
    score = jnp.where(forced, 1e30, jnp.where(jrow <= cur, pslc_t, NEG))
    chosen = _topk_rows(score, min(N_SEL, nsb)) > 0.5
    drop = jnp.where(chosen & (score > -1e29), 0.0, 1.0)
    drop = jnp.concatenate([drop, jnp.zeros((LANES - nsb, tq), F32)], axis=0)
    sel_ref[0, 0] = (_transpose_small_ints(drop) * NEG).astype(BF16)


def _cmpsel(qraw, kcmp, vcmp, gate, tq):
    B, S, _ = qraw.shape
    ncb = kcmp.shape[2]
    gw = NSA_HG * LANES
    assert S // SEL_BLOCK <= NSA_HD
    cmp_spec = pl.BlockSpec((1, 1, ncb, LANES), lambda b, g, t: (b, g, 0, 0))
    return pl.pallas_call(
        _cmpsel_kernel,
        out_shape=(jax.ShapeDtypeStruct((B, S, NSA_WIDTH), F32),
                   jax.ShapeDtypeStruct((B, NSA_KV_GROUPS, S, LANES), BF16)),
        grid=(B, NSA_KV_GROUPS, S // tq),
        in_specs=[pl.BlockSpec((1, tq, gw), lambda b, g, t: (b, t, g)), cmp_spec, cmp_spec,
                  pl.BlockSpec((1, tq, LANES), lambda b, g, t: (b, t, g))],
        out_specs=(pl.BlockSpec((1, tq, NSA_WIDTH // NSA_KV_GROUPS), lambda b, g, t: (b, t, g)),
                   pl.BlockSpec((1, 1, tq, LANES), lambda b, g, t: (b, g, t, 0))),
        compiler_params=_cparams(("arbitrary", "arbitrary", "arbitrary")),
        name="cmpsel",
    )(qraw, kcmp, vcmp, gate)


def _slcwin_kernel(q_ref, sel_ref, ks_ref, kw_ref, vs_ref, vw_ref, gate_ref, ocmp_ref, wgu_ref, wdn_ref,
                   o_ref, wgu_bf_ref, wdn_bf_ref, *, tk, wk):
    wgu_bf_ref[...] = wgu_ref[...].astype(BF16)
    wdn_bf_ref[...] = wdn_ref[...].astype(BF16)
    tq = q_ref.shape[1]
    S = ks_ref.shape[1]
    t = pl.program_id(2)
    q0 = t * tq
    qpos = q0 + lax.broadcasted_iota(jnp.int32, (tq, 1), 0)
    qw = [q_ref[0, :, h * LANES:(h + 1) * LANES] for h in range(NSA_HG)]
    sel = sel_ref[0, 0]
    qa = [jnp.concatenate([q, sel], axis=1) for q in qw]

    gate = gate_ref[0]

    def normalise(acc, g):
        return acc * (g / acc[:, NSA_HD:NSA_HD + 1])

    nchain = NSA_HG // SLC_STACK
    qst = [jnp.concatenate(qa[c * SLC_STACK:(c + 1) * SLC_STACK], axis=0) for c in range(nchain)]
    qpos_st = jnp.concatenate([qpos] * SLC_STACK, axis=0)

    def kstep(j, carry, diag):
        k0 = pl.multiple_of(j * tk, tk)
        ks = ks_ref[0, pl.ds(k0, tk), :]
        vs = vs_ref[0, pl.ds(k0, tk), :]
        if diag:
            vis = (k0 + lax.broadcasted_iota(jnp.int32, (1, tk), 1)) <= qpos_st
        out = []
        for c in range(nchain):
            m, acc = carry[c]
            s = lax.dot_general(qst[c], ks, NT, preferred_element_type=F32)
            if diag:
                s = jnp.where(vis, s, NEG)
            mn = jnp.maximum(m, jnp.max(s, axis=-1, keepdims=True))
            p = jnp.exp2(s - mn)
            acc = jnp.exp2(m - mn) * acc + jnp.dot(p.astype(BF16), vs, preferred_element_type=F32)
            out.append((mn, acc))
        return tuple(out)

    init = tuple((jnp.full((SLC_STACK * tq, 1), NEG, F32), jnp.zeros((SLC_STACK * tq, LANES), F32))
                 for _ in range(nchain))
    jdiag = q0 // tk
    carry = lax.fori_loop(0, jdiag, lambda j, c: kstep(j, c, False), init)
    carry = kstep(jdiag, carry, True)
    heads = [normalise(carry[h // SLC_STACK][1][(h % SLC_STACK) * tq:(h % SLC_STACK + 1) * tq],
                       gate[:, 3 * h + 1:3 * h + 2]) for h in range(NSA_HG)]

    wq = wk - WINDOW
    wins = [[] for _ in range(NSA_HG)]
    spos = lax.broadcasted_iota(jnp.int32, (NSA_HG * wq, 1), 0) % wq
    for i in range(tq // wq):
        rows = slice(i * wq, (i + 1) * wq)
        start = pl.multiple_of(jnp.maximum(q0 + (i + 1) * wq - wk, 0), wq)
        kw = kw_ref[0, pl.ds(start, wk), :]
        vw = vw_ref[0, pl.ds(start, wk), :]
        d = (q0 + i * wq + spos) - (start + lax.broadcasted_iota(jnp.int32, (1, wk), 1))
        q_st = jnp.concatenate([q[rows] for q in qw], axis=0)
        s = jnp.where((d >= 0) & (d < WINDOW), lax.dot_general(q_st, kw, NT, preferred_element_type=F32), NEG)
        p = jnp.exp2(s - jnp.max(s, axis=-1, keepdims=True))
        o_st = jnp.dot(p.astype(BF16), vw, preferred_element_type=F32)
        for h in range(NSA_HG):
            wins[h].append(o_st[h * wq:(h + 1) * wq])
    for h in range(NSA_HG):
        heads[h] = heads[h] + normalise(jnp.concatenate(wins[h], axis=0), gate[:, 3 * h + 2:3 * h + 3])
    o_ref[0] = (ocmp_ref[0] + _pack_heads(heads)).astype(BF16)


def _slcwin(qrot, sel, ks, kw, vaug, gate, ocmp, wgu, wdn, tq, tk):
    B, S, _ = qrot.shape
    gw = NSA_HG * LANES
    wk = WINDOW + WIN_Q
    nt = S // tq
    nstep = B * NSA_KV_GROUPS * nt
    assert S >= wk and tq % WIN_Q == 0 and N_EXPERTS % nstep == 0
    epb = N_EXPERTS // nstep
    ospec = pl.BlockSpec((1, tq, NSA_WIDTH // NSA_KV_GROUPS), lambda b, g, t: (b, t, g))
    wspec = lambda a: pl.BlockSpec((epb,) + a.shape[1:], lambda b, g, t: ((b * NSA_KV_GROUPS + g) * nt + t, 0, 0))
    return pl.pallas_call(
        functools.partial(_slcwin_kernel, tk=tk, wk=wk),
        out_shape=(jax.ShapeDtypeStruct((B, S, NSA_WIDTH), BF16),
                   jax.ShapeDtypeStruct(wgu.shape, BF16), jax.ShapeDtypeStruct(wdn.shape, BF16)),
        grid=(B, NSA_KV_GROUPS, nt),
        in_specs=[pl.BlockSpec((1, tq, gw), lambda b, g, t: (b, t, g)),
                  pl.BlockSpec((1, 1, tq, LANES), lambda b, g, t: (b, g, t, 0)),
                  pl.BlockSpec((1, S, 2 * LANES), lambda b, g, t: (b, 0, 0)),
                  pl.BlockSpec((1, S, LANES), lambda b, g, t: (b, 0, 0)),
                  pl.BlockSpec((1, S, LANES), lambda b, g, t: (b, 0, g)),
                  pl.BlockSpec((1, S, LANES), lambda b, g, t: (b, 0, NSA_KV_GROUPS + g)),
                  pl.BlockSpec((1, tq, LANES), lambda b, g, t: (b, t, g)),
                  ospec, wspec(wgu), wspec(wdn)],
        out_specs=(ospec, wspec(wgu), wspec(wdn)),
        compiler_params=_cparams(("arbitrary", "arbitrary", "arbitrary")),
        name="slcwin",
    )(qrot, sel, ks, kw, vaug, vaug, gate, ocmp, wgu, wdn)


def _outproj_kernel(x_ref, ohg_ref, onsa_ref, wout_ref,
                    g1_ref, sh2_ref, sc2_ref, n2_ref, rwt_ref, rb_ref,
                    x1_ref, h2_ref, sc_ref, slot_t_ref, w_t_ref, cnt_ref):
    tm = x_ref.shape[1]
    mix = jnp.dot(jnp.concatenate([ohg_ref[0], onsa_ref[0]], axis=1), wout_ref[...], preferred_element_type=F32)
    x1 = x_ref[0] + g1_ref[0] * mix
    x1_ref[0] = x1
    h2 = (x1 * lax.rsqrt(jnp.mean(x1 * x1, axis=-1, keepdims=True) + EPS) * n2_ref[...]) * (1.0 + sc2_ref[0]) + sh2_ref[0]
    h_hi = h2.astype(BF16)
    h2_ref[0] = h_hi
    h_lo = (h2 - h_hi.astype(F32)).astype(BF16)
    both = lax.dot_general(rwt_ref[...], h_hi, NT, preferred_element_type=F32)
    logits = (both[:N_EXPERTS] + both[N_EXPERTS:]
              + lax.dot_general(rwt_ref[:N_EXPERTS, :], h_lo, NT, preferred_element_type=F32))
    scores = _sigmoid(logits)
    choice = scores + rb_ref[...]
    per = N_EXPERTS // N_GROUPS
    c3 = choice.reshape(N_GROUPS, per, tm)
    erow = lax.broadcasted_iota(jnp.int32, c3.shape, 1)
    rank_in = jnp.zeros(c3.shape, F32)
    for k in range(per):
        ck = c3[:, k:k + 1, :]
        rank_in = rank_in + jnp.where((ck > c3) | ((ck == c3) & (erow > k)), 1.0, 0.0)
    grp_score = jnp.sum(jnp.where(rank_in < 2, c3, 0.0), axis=1)
    grp_keep = _rank_before(grp_score, N_GROUPS) < TOPK_GROUPS
    masked = jnp.where(grp_keep[:, None, :], c3, -jnp.inf).reshape(N_EXPERTS, tm)
    keep = _topk_rows(masked, TOP_K) > 0.5
    tw = jnp.where(keep, scores, 0.0)
    tw = tw / jnp.sum(tw, axis=0, keepdims=True) * ROUTED_SCALE
    before = lax.broadcasted_iota(jnp.int32, (tm, tm), 0)
    after = lax.broadcasted_iota(jnp.int32, (tm, tm), 1)
    earlier = (before < after) & (before // MOE_SUB == after // MOE_SUB)
    kept = jnp.where(keep, 1.0, 0.0)
    pref = jnp.dot(kept.astype(BF16), earlier.astype(BF16), preferred_element_type=F32)
    slot = jnp.where(keep, pref, -1.0)
    assert MOE_SUB <= 256
    sc_ref[0] = _transpose_small_ints(slot)
    for j in range(tm // MOE_SUB):
        sub = slice(j * MOE_SUB, (j + 1) * MOE_SUB)
        slot_t_ref[:, j] = slot[:, sub].reshape(N_EXPERTS // MOE_EG, MOE_EG, MOE_SUB)
        w_t_ref[:, j] = tw[:, sub].reshape(N_EXPERTS // MOE_EG, MOE_EG, MOE_SUB)
        cnt_ref[j] = jnp.broadcast_to(jnp.sum(kept[:, sub], axis=1, keepdims=True), (N_EXPERTS, LANES))


def _outproj(x, ohg, onsa, wout, mod3, n2, rwt, rb, tm):
    B, S, D = x.shape
    blk = lambda w: pl.BlockSpec((1, tm, w), lambda b, i: (b, i, 0))
    full = lambda a: pl.BlockSpec(a.shape, lambda b, i: (0,) * a.ndim)
    modc = lambda k: pl.BlockSpec((1, 1, D), lambda b, i, k=k: (b, 0, k))
    nt = S // tm
    ng = N_EXPERTS // MOE_EG
    nh = tm // MOE_SUB
    tspec = pl.BlockSpec((ng, nh, MOE_EG, MOE_SUB), lambda b, i: (0, b * nt + i, 0, 0))
    return pl.pallas_call(
        _outproj_kernel,
        out_shape=(jax.ShapeDtypeStruct((B, S, D), F32),
                   jax.ShapeDtypeStruct((B, S, D), BF16),
                   jax.ShapeDtypeStruct((B, S, N_EXPERTS), F32),
                   jax.ShapeDtypeStruct((ng, B * nt * nh, MOE_EG, MOE_SUB), F32),
                   jax.ShapeDtypeStruct((ng, B * nt * nh, MOE_EG, MOE_SUB), F32),
                   jax.ShapeDtypeStruct((B * nt * nh, N_EXPERTS, LANES), F32)),
        grid=(B, nt),
        in_specs=[blk(D), blk(HG_WIDTH), blk(NSA_WIDTH), full(wout),
                  modc(2), modc(3), modc(4), full(n2), full(rwt), full(rb)],
        out_specs=(blk(D), blk(D), blk(N_EXPERTS), tspec, tspec,
                   pl.BlockSpec((nh, N_EXPERTS, LANES), lambda b, i: (b * nt + i, 0, 0))),
        compiler_params=_cparams(("arbitrary", "arbitrary")),
        name="outproj",
    )(x, ohg, onsa, wout, mod3, mod3, mod3, n2, rwt, rb)


def _swiglu_hidden(x, wgu):
    gu = jnp.dot(x, wgu, preferred_element_type=F32)
    return _silu(gu[:, :EXPERT_HIDDEN]) * gu[:, EXPERT_HIDDEN:]


def _moe_kernel(rounds_ref, h_ref, slot_t_ref, w_t_ref, slot_c_ref, wgu_ref, wdn_ref, sgu_ref, sdn_ref,
                o_ref, x_scr, p_scr):
    i = pl.program_id(0)
    g = pl.program_id(1)
    tm = h_ref.shape[0]
    nsub = tm // MOE_SUB
    rnd, eg = MOE_ROUND, MOE_EG
    ng = N_EXPERTS // eg
    base = i * (N_EXPERTS + ng)

    @pl.when(g == 0)
    def _():
        act = _swiglu_hidden(h_ref[...], sgu_ref[...]).astype(BF16)
        o_ref[...] = jnp.dot(act, sdn_ref[...], preferred_element_type=F32)

    blk = eg * rnd
    rslot = lax.broadcasted_iota(jnp.int32, (rnd, 1), 0).astype(F32)

    def gather(first, nround):
        for s in range(nsub):
            st = slot_t_ref[0, s] - first
            p = jnp.concatenate([jnp.where(rslot + float(w * rnd) == st[e:e + 1, :], 1.0, 0.0)
                                 for w in range(nround) for e in range(eg)], axis=0)
            p_scr[s, 0:nround * blk, :] = p.astype(BF16)
            x_scr[s, 0:nround * blk, :] = jnp.dot(p_scr[s, 0:nround * blk, :], h_ref[s * MOE_SUB:(s + 1) * MOE_SUB, :],
                                                  preferred_element_type=F32).astype(BF16)

    def run_expert(e, w):
        rows = slice(w * blk + e * rnd, w * blk + (e + 1) * rnd)
        xe = jnp.concatenate([x_scr[s, rows, :] for s in range(nsub)], axis=0)
        wslot = jnp.concatenate(
            [jnp.sum(p_scr[s, rows, :].astype(F32) * w_t_ref[0, s][e:e + 1, :], axis=-1, keepdims=True)
             for s in range(nsub)], axis=0)
        y = jnp.dot((_swiglu_hidden(xe, wgu_ref[e]) * wslot).astype(BF16), wdn_ref[e], preferred_element_type=F32)
        for s in range(nsub):
            x_scr[s, rows, :] = y[s * rnd:(s + 1) * rnd].astype(BF16)

    def run_if(cond, e, w):
        pl.when(cond)(functools.partial(run_expert, e, w))

    def scatter(first, nround):
        lane = lax.broadcasted_iota(jnp.int32, (1, nround * blk), 1)
        lane_slot = ((lane // blk) * rnd + lane % rnd).astype(F32)
        spread = ((lax.broadcasted_iota(jnp.int32, (N_EXPERTS, nround * blk), 1) % blk) // rnd + g * eg
                  == lax.broadcasted_iota(jnp.int32, (N_EXPERTS, nround * blk), 0)).astype(BF16)
        for s in range(nsub):
            toks = slice(s * MOE_SUB, (s + 1) * MOE_SUB)
            sc = jnp.dot(slot_c_ref[toks, :].astype(BF16), spread, preferred_element_type=F32) - first
            pt = jnp.where(sc == lane_slot, 1.0, 0.0).astype(BF16)
            o_ref[toks, :] += jnp.dot(pt, x_scr[s, 0:nround * blk, :], preferred_element_type=F32)

    def first_pass(nround):
        gather(0.0, nround)
        for e in range(eg):
            run_expert(e, 0)
        for w in range(1, nround):
            for e in range(eg):
                run_if(rounds_ref[base + g * eg + e] > w, e, w)
        scatter(0.0, nround)

    group_rounds = rounds_ref[base + N_EXPERTS + g]
    pl.when(group_rounds < MOE_FIRST)(functools.partial(first_pass, MOE_FIRST - 1))
    pl.when(group_rounds >= MOE_FIRST)(functools.partial(first_pass, MOE_FIRST))

    def later_round(r, carry):
        first = (r * rnd).astype(F32)
        gather(first, 1)
        for e in range(eg):
            run_if(rounds_ref[base + g * eg + e] > r, e, 0)
        scatter(first, 1)
        return carry

    lax.fori_loop(MOE_FIRST, group_rounds, later_round, 0)


def _moe(rounds, h2, slot_t, w_t, slot_c, wgu, wdn, sgu, sdn, tm):
    T, D = h2.shape
    eg, nsub = MOE_EG, tm // MOE_SUB
    full = lambda a: pl.BlockSpec(a.shape, lambda i, e, o: (0,) * a.ndim)
    tspec = pl.BlockSpec((1, nsub, eg, MOE_SUB), lambda i, e, o: (e, i, 0, 0))
    return pl.pallas_call(
        _moe_kernel,
        out_shape=jax.ShapeDtypeStruct((T, D), F32),
        grid_spec=pltpu.PrefetchScalarGridSpec(
            num_scalar_prefetch=1,
            grid=(T // tm, N_EXPERTS // eg),
            in_specs=[pl.BlockSpec((tm, D), lambda i, e, o: (i, 0), pipeline_mode=pl.Buffered(1)), tspec, tspec,
                      pl.BlockSpec((tm, N_EXPERTS), lambda i, e, o: (i, 0)),
                      pl.BlockSpec((eg, D, 2 * EXPERT_HIDDEN), lambda i, e, o: (e, 0, 0)),
                      pl.BlockSpec((eg, EXPERT_HIDDEN, D), lambda i, e, o: (e, 0, 0)),
                      full(sgu), full(sdn)],
            out_specs=pl.BlockSpec((tm, D), lambda i, e, o: (i, 0), pipeline_mode=pl.Buffered(1)),
            scratch_shapes=[pltpu.VMEM((nsub, MOE_FIRST * eg * MOE_ROUND, D), BF16),
                            pltpu.VMEM((nsub, MOE_FIRST * eg * MOE_ROUND, MOE_SUB), BF16)]),
        compiler_params=_cparams(("arbitrary", "arbitrary")),
        name="moe",
    )(rounds, h2, slot_t, w_t, slot_c, wgu, wdn, sgu, sdn)


def _final_kernel(x1_ref, moe_ref, g2_ref, fg_ref, o_ref):
    x2 = x1_ref[0] + g2_ref[0] * moe_ref[0]
    o_ref[0] = x2 * lax.rsqrt(jnp.mean(x2 * x2, axis=-1, keepdims=True) + EPS) * fg_ref[...]


def _final(x1, moe, mod3, fg, tm):
    B, S, D = x1.shape
    blk = pl.BlockSpec((1, tm, D), lambda b, i: (b, i, 0))
    return pl.pallas_call(
        _final_kernel,
        out_shape=jax.ShapeDtypeStruct((B, S, D), F32),
        grid=(B, S // tm),
        in_specs=[blk, blk, pl.BlockSpec((1, 1, D), lambda b, i: (b, 0, 5)), pl.BlockSpec((1, D), lambda b, i: (0, 0))],
        out_specs=blk,
        compiler_params=_cparams(("arbitrary", "arbitrary")),
        name="final",
    )(x1, moe, mod3, fg)


def _split_bf16(w):
    hi = w.astype(BF16)
    return jnp.concatenate([hi, (w - hi.astype(F32)).astype(BF16)], axis=0)


def _pack_w_in(w_in):
    return jnp.pad(w_in, ((0, 0), (0, IN_COLS_P - w_in.shape[1]))).astype(BF16)


def _pack_cmp(pos, w1, b1, w2, lane_by_group):
    half = CMP_STRIDE * NSA_HD
    def rows_for(wh):
        w3 = wh.reshape(CMP_STRIDE, NSA_HD, CMP_HIDDEN)
        z = jnp.zeros_like(w3)
        return jnp.stack([jnp.concatenate([w3, z], axis=1).reshape(CMP_STRIDE * LANES, CMP_HIDDEN),
                          jnp.concatenate([z, w3], axis=1).reshape(CMP_STRIDE * LANES, CMP_HIDDEN)])
    w1a = rows_for(w1[:half]).astype(BF16)
    w1b = rows_for(w1[half:]).astype(BF16)
    z2 = jnp.zeros_like(w2)
    w2_first = jnp.concatenate([w2, z2], axis=1)
    w2p = jnp.stack([w2_first, jnp.concatenate([z2, w2], axis=1) if lane_by_group else w2_first]).astype(BF16)
    pe8 = jnp.pad(pos.reshape(1, CMP_BLOCK * NSA_HD), ((0, 7), (0, 0)))
    return w1a, w1b, pe8, w1, b1.reshape(1, CMP_HIDDEN), w2p


def _rope_tables(S):
    half = NSA_HD // 2
    inv = ROPE_THETA ** (-jnp.arange(half, dtype=F32) / half)
    ang = jnp.arange(S, dtype=F32)[:, None] * inv[None, :]
    cos, sin = jnp.cos(ang), jnp.sin(ang)
    reps = LANES // NSA_HD
    return jnp.tile(jnp.concatenate([cos, cos], axis=1), (1, reps)), jnp.tile(jnp.concatenate([-sin, sin], axis=1), (1, reps))


def _tiles(S):
    return dict(inproj=min(512, S), hgrn=min(512, S), cmpsel=min(512, S), slc_q=512, slc_k=512, moe=min(2048, S), final=min(512, S))


def kernel(x, c, w_ada, b_ada, norm1_g, w_in, hg_lb_logits, hg_norm_g, cmp_pos_k, cmp_w1_k, cmp_b1_k, cmp_w2_k,
           cmp_pos_v, cmp_w1_v, cmp_b1_v, cmp_w2_v, w_out, norm2_g, router_w, router_bias, w_exp_gu, w_exp_dn,
           w_sh_gu, w_sh_dn, final_g):
    B, S, D = x.shape
    assert D == D_MODEL and w_ada.shape[0] == 1 and S % 512 == 0
    tl = _tiles(S)
    l = 0
    lb = jnp.cumsum(jax.nn.softmax(hg_lb_logits.astype(F32), axis=0), axis=0)[l].reshape(1, HG_WIDTH)
    c8 = jnp.pad(c, ((0, 8 - B), (0, 0)))
    mod3 = _ada(c8, w_ada[l], b_ada[l].reshape(1, -1))[:B].reshape(B, 1, 6 * D)
    cos, sin = _rope_tables(S)
    hg, qraw, qrot, kc, vc, ks, kw, vaug, gate = _inproj(x, mod3, norm1_g[l].reshape(1, D), _pack_w_in(w_in[l]),
                                                         cos, sin, tl["inproj"])
    ng = hg_norm_g[l].reshape(1, HG_WIDTH)
    mxu_safe = -HG_SUB * jnp.log(jnp.min(lb)) <= HG_SAFE_LOG_RANGE
    ohg = lax.cond(mxu_safe, lambda: _hgrn_mxu(hg, lb, ng, tl["hgrn"]), lambda: _hgrn(hg, lb, ng))
    ncb = S // CMP_STRIDE
    kcmp = _cmpmlp(kc.reshape(B, ncb, CMP_STRIDE * LANES),
                   *_pack_cmp(cmp_pos_k[l], cmp_w1_k[l], cmp_b1_k[l], cmp_w2_k[l], True))
    vcmp = _cmpmlp(vc.reshape(B, ncb, CMP_STRIDE * LANES),
                   *_pack_cmp(cmp_pos_v[l], cmp_w1_v[l], cmp_b1_v[l], cmp_w2_v[l], False))
    ocmp, sel = _cmpsel(qraw, kcmp, vcmp, gate, tl["cmpsel"])
    onsa, wgu_bf, wdn_bf = _slcwin(qrot, sel, ks, kw, vaug, gate, ocmp, w_exp_gu[l], w_exp_dn[l],
                                   tl["slc_q"], tl["slc_k"])
    x1, h2, sc, slot_t, w_t, cnt = _outproj(x, ohg, onsa, w_out[l].astype(BF16), mod3,
                                            norm2_g[l].reshape(1, D), _split_bf16(router_w[l].T),
                                            router_bias[l].reshape(N_EXPERTS, 1), ROUTE_TM)
    T = B * S
    tm = tl["moe"]
    ng = N_EXPERTS // MOE_EG
    slot_c = sc.reshape(T, N_EXPERTS)
    load = jnp.max(cnt[:, :, 0].reshape(T // tm, tm // MOE_SUB, N_EXPERTS), axis=1).astype(jnp.int32)
    per_expert = (load + MOE_ROUND - 1) // MOE_ROUND
    rounds = jnp.concatenate([per_expert, jnp.max(per_expert.reshape(T // tm, ng, MOE_EG), axis=2)], axis=1).reshape(-1)
    moe = _moe(rounds, h2.reshape(T, D), slot_t, w_t, slot_c,
               wgu_bf, wdn_bf, w_sh_gu[l].astype(BF16), w_sh_dn[l].astype(BF16), tm)
    return _final(x1, moe.reshape(B, S, D), mod3, final_g.reshape(1, D), tl["final"])
```

```python
import functools

import jax
import jax.numpy as jnp
from jax import lax
from jax.experimental import pallas as pl
from jax.experimental.pallas import tpu as pltpu

F32 = jnp.float32
BF16 = jnp.bfloat16
HIGHEST = lax.Precision.HIGHEST

D_MODEL = 1024
EPS = 1e-6
HG_HEADS = 4
HG_DK = 128
HG_DV = 128
HG_WIDTH = HG_HEADS * HG_DV
HG_CHUNK = 64
HG_SUB = 16
HG_SAFE_LOG_RANGE = 80.0
NSA_HEADS = 8
NSA_KV_GROUPS = 2
NSA_HG = NSA_HEADS // NSA_KV_GROUPS
NSA_HD = 64
NSA_WIDTH = NSA_HEADS * NSA_HD
NSA_KV = NSA_KV_GROUPS * NSA_HD
CMP_BLOCK = 32
CMP_STRIDE = 16
CMP_HIDDEN = 256
SEL_BLOCK = 64
N_SEL = 16
WINDOW = 512
ROPE_THETA = 10000.0
SLC_STACK = 4
WIN_Q = 256
N_EXPERTS = 64
TOP_K = 8
N_GROUPS = 8
TOPK_GROUPS = 4
EXPERT_HIDDEN = 256
ROUTED_SCALE = 2.5
ROUTE_TM = 512
MOE_SUB = 256
MOE_ROUND = 32
MOE_EG = 8
MOE_FIRST = 2

LANES = 128
NEG = -1e30
VMEM_LIMIT = 56 * 1024 * 1024

C_HG = 0
C_Q = 4 * HG_WIDTH
C_K = C_Q + NSA_WIDTH
C_GATE = C_K + 6 * NSA_KV
IN_COLS_P = C_GATE + LANES
LOG2E = 1.4426950408889634

NT = (((1,), (1,)), ((), ()))
TN = (((0,), (0,)), ((), ()))


def _cparams(sem):
    return pltpu.CompilerParams(dimension_semantics=sem, vmem_limit_bytes=VMEM_LIMIT)


def _sigmoid(x):
    return 1.0 / (1.0 + jnp.exp(-x))


def _silu(x):
    return x * _sigmoid(x)


def _gelu_tanh(x):
    return 0.5 * x * (1.0 + jnp.tanh(0.7978845608028654 * (x + 0.044715 * (x * x * x))))


def _ada_kernel(c_ref, w_ref, b_ref, o_ref):
    a = _silu(c_ref[...])
    o_ref[...] = jnp.dot(a, w_ref[...], precision=HIGHEST, preferred_element_type=F32) + b_ref[...]


def _ada(c8, w, b):
    n = w.shape[1]
    tn = 1024
    return pl.pallas_call(
        _ada_kernel,
        out_shape=jax.ShapeDtypeStruct((8, n), F32),
        grid=(n // tn,),
        in_specs=[pl.BlockSpec((8, D_MODEL), lambda j: (0, 0)),
                  pl.BlockSpec((D_MODEL, tn), lambda j: (0, j)),
                  pl.BlockSpec((1, tn), lambda j: (0, j))],
        out_specs=pl.BlockSpec((8, tn), lambda j: (0, j)),
        compiler_params=_cparams(("arbitrary",)),
        name="ada",
    )(c8, w, b)


def _rope(t, cos, sin_signed, first_half):
    rot = jnp.where(first_half, pltpu.roll(t, 96, 1), pltpu.roll(t, 32, 1))
    return t * cos + rot * sin_signed


def _inproj_kernel(x_ref, sh_ref, sc_ref, g_ref, w_ref, cos_ref, sin_ref,
                   hg_ref, qraw_ref, qrot_ref, kc_ref, vc_ref, ks_ref, kw_ref, va_ref, gate_ref, h_scr):
    tm = x_ref.shape[1]
    x = x_ref[0]
    y = x * lax.rsqrt(jnp.mean(x * x, axis=-1, keepdims=True) + EPS) * g_ref[...]
    h_scr[...] = (y * (1.0 + sc_ref[0]) + sh_ref[0]).astype(BF16)

    def mm(lo, width):
        return jnp.dot(h_scr[...], w_ref[:, lo:lo + width], preferred_element_type=F32)

    cos = cos_ref[...]
    sin = sin_ref[...]
    first_half = (lax.broadcasted_iota(jnp.int32, cos.shape, 1) % NSA_HD) < (NSA_HD // 2)
    lane = lax.broadcasted_iota(jnp.int32, (tm, LANES), 1)
    half_of_lane = lane // NSA_HD

    def own_lanes(t, src_half, dst_half):
        moved = t if src_half == dst_half else pltpu.roll(t, NSA_HD, 1)
        return jnp.where(half_of_lane == dst_half, moved, 0.0)

    for j in range(4):
        hg_ref[0, :, j * HG_WIDTH:(j + 1) * HG_WIDTH] = mm(C_HG + j * HG_WIDTH, HG_WIDTH)
    for m in range(NSA_WIDTH // (2 * LANES)):
        qq = mm(C_Q + m * 2 * LANES, 2 * LANES) * (NSA_HD ** -0.5)
        for c in range(2):
            q = qq[:, c * LANES:(c + 1) * LANES]
            qr = _rope(q, cos, sin, first_half) * LOG2E
            for half in range(2):
                n = (2 * m + c) * 2 + half
                g = n // NSA_HG
                qraw_ref[0, :, n * LANES:(n + 1) * LANES] = own_lanes(q, half, g).astype(BF16)
                qrot_ref[0, :, n * LANES:(n + 1) * LANES] = own_lanes(qr, half, g).astype(BF16)
    kv = mm(C_K, 2 * LANES)
    kc_ref[0] = kv[:, :LANES].astype(BF16)
    vc_ref[0] = kv[:, LANES:].astype(BF16)
    pos = pl.program_id(1) * tm + lax.broadcasted_iota(jnp.int32, (tm, LANES), 0)
    ks_ref[0, :, LANES:2 * LANES] = jnp.where(lane == pos // SEL_BLOCK, 1.0, 0.0).astype(BF16)
    for j in range(2):
        kv = mm(C_K + (j + 1) * 2 * LANES, 2 * LANES)
        k_rot = _rope(kv[:, :LANES], cos, sin, first_half).astype(BF16)
        if j == 0:
            ks_ref[0, :, 0:LANES] = k_rot
        else:
            kw_ref[0] = k_rot
        for g in range(NSA_KV_GROUPS):
            v = own_lanes(kv[:, LANES:], g, 0)
            col = (j * NSA_KV_GROUPS + g) * LANES
            va_ref[0, :, col:col + LANES] = jnp.where(lane == NSA_HD, 1.0, v).astype(BF16)
    gate = _sigmoid(mm(C_GATE, LANES))
    per_group = 3 * NSA_HG
    for g in range(NSA_KV_GROUPS):
        gate_ref[0, :, g * LANES:(g + 1) * LANES] = gate if g == 0 else pltpu.roll(gate, LANES - g * per_group, 1)


def _inproj(x, mod3, norm_g, w_p, cos, sin, tm):
    B, S, D = x.shape
    blk = lambda w: pl.BlockSpec((1, tm, w), lambda b, i: (b, i, 0))
    return pl.pallas_call(
        _inproj_kernel,
        out_shape=(jax.ShapeDtypeStruct((B, S, 4 * HG_WIDTH), F32),
                   jax.ShapeDtypeStruct((B, S, NSA_HEADS * LANES), BF16),
                   jax.ShapeDtypeStruct((B, S, NSA_HEADS * LANES), BF16),
                   jax.ShapeDtypeStruct((B, S, LANES), BF16),
                   jax.ShapeDtypeStruct((B, S, LANES), BF16),
                   jax.ShapeDtypeStruct((B, S, 2 * LANES), BF16),
                   jax.ShapeDtypeStruct((B, S, LANES), BF16),
                   jax.ShapeDtypeStruct((B, S, 4 * LANES), BF16),
                   jax.ShapeDtypeStruct((B, S, NSA_KV_GROUPS * LANES), F32)),
        grid=(B, S // tm),
        in_specs=[blk(D),
                  pl.BlockSpec((1, 1, D), lambda b, i: (b, 0, 0)),
                  pl.BlockSpec((1, 1, D), lambda b, i: (b, 0, 1)),
                  pl.BlockSpec((1, D), lambda b, i: (0, 0)),
                  pl.BlockSpec((D, IN_COLS_P), lambda b, i: (0, 0)),
                  pl.BlockSpec((tm, LANES), lambda b, i: (i, 0)),
                  pl.BlockSpec((tm, LANES), lambda b, i: (i, 0))],
        out_specs=(blk(4 * HG_WIDTH), blk(NSA_HEADS * LANES), blk(NSA_HEADS * LANES),
                   blk(LANES), blk(LANES), blk(2 * LANES), blk(LANES), blk(4 * LANES), blk(NSA_KV_GROUPS * LANES)),
        scratch_shapes=[pltpu.VMEM((tm, D), BF16)],
        compiler_params=_cparams(("arbitrary", "arbitrary")),
        name="inproj",
    )(x, mod3, mod3, norm_g, w_p, cos, sin)


def _hgrn_kernel(q_ref, f_ref, i_ref, gt_ref, lb_ref, ng_ref, o_ref):
    S = q_ref.shape[1]
    C, U = HG_CHUNK, HG_SUB
    lb = lb_ref[...]
    ng = ng_ref[...]
    ri = lax.broadcasted_iota(jnp.int32, (C, C), 0)
    ci = lax.broadcasted_iota(jnp.int32, (C, C), 1)
    tril = (ri >= ci).astype(F32)
    trow = lax.broadcasted_iota(jnp.int32, (U, 1), 0)

    def chunk(c, st_t):
        r0 = pl.multiple_of(c * C, C)
        rows = pl.ds(r0, C)
        f = lb + (1.0 - lb) * _sigmoid(f_ref[0, rows, :])
        kk = 1.0 - f
        b = jnp.dot(tril, jnp.log(f), precision=HIGHEST, preferred_element_type=F32)
        q = q_ref[0, rows, :] * (HG_DK ** -0.5)
        v = i_ref[0, rows, :]
        vb = v.astype(BF16)
        o_inter = lax.dot_general((q * jnp.exp(b)).astype(BF16), st_t.astype(BF16), NT,
                                  preferred_element_type=F32)
        parts = []
        for i in range(C // U):
            lo = i * U
            bi = b[lo:lo + U]
            qi = q[lo:lo + U]
            if i == 0:
                oi = jnp.zeros((U, HG_DV), F32)
            else:
                r = b[lo - 1:lo]
                qrel = (qi * jnp.exp(bi - r)).astype(BF16)
                kprev = (kk[:lo] * jnp.exp(r - b[:lo])).astype(BF16)
                a_off = lax.dot_general(qrel, kprev, NT, preferred_element_type=F32)
                oi = jnp.dot(a_off.astype(BF16), vb[:lo], preferred_element_type=F32)
            for s in range(U):
                valid = trow >= s
                e = jnp.exp(jnp.where(valid, bi - bi[s:s + 1], 0.0))
                a = jnp.sum(qi * e * kk[lo + s:lo + s + 1], axis=-1, keepdims=True)
                oi = oi + jnp.where(valid, a, 0.0) * v[lo + s:lo + s + 1]
            parts.append(oi)
        o = o_inter + jnp.concatenate(parts, axis=0)
        o = o * lax.rsqrt(jnp.mean(o * o, axis=-1, keepdims=True) + EPS) * ng
        o_ref[0, rows, :] = (o * _silu(gt_ref[0, rows, :])).astype(BF16)
        bl = b[C - 1:C]
        kv_t = lax.dot_general(vb, (kk * jnp.exp(bl - b)).astype(BF16), TN, preferred_element_type=F32)
        return jnp.exp(bl) * st_t + kv_t

    lax.fori_loop(0, S // C, chunk, jnp.zeros((HG_DV, HG_DK), F32))


def _cumsum_rows(x):
    n = x.shape[0]
    row = lax.broadcasted_iota(jnp.int32, x.shape, 0)
    d = 1
    while d < n:
        x = x + jnp.where(row >= d, pltpu.roll(x, d, 0), 0.0)
        d *= 2
    return x


def _hgrn_mxu_kernel(hg_ref, lb_ref, ng_ref, o_ref, st_scr):
    ts = hg_ref.shape[1]
    C, U = HG_CHUNK, HG_SUB

    @pl.when(pl.program_id(1) == 0)
    def _():
        st_scr[...] = jnp.zeros(st_scr.shape, F32)

    W = HG_WIDTH
    NH = HG_HEADS
    head_of_lane = lax.broadcasted_iota(jnp.int32, (1, W), 1) // HG_DK
    hcols = [slice(h * HG_DK, (h + 1) * HG_DK) for h in range(NH)]

    def chunk(c, carry):
        rows = pl.ds(pl.multiple_of(c * C, C), C)
        lb = lb_ref[...]
        f = lb + (1.0 - lb) * _sigmoid(hg_ref[0, rows, W:2 * W])
        kk = 1.0 - f
        b = _cumsum_rows(jnp.log(f))
        q = hg_ref[0, rows, 0:W] * (HG_DK ** -0.5)
        vb = hg_ref[0, rows, 2 * W:3 * W].astype(BF16)
        qe = (q * jnp.exp(b)).astype(BF16)
        o_inter = jnp.concatenate(
            [lax.dot_general(qe[:, hc], st_scr[h].astype(BF16), NT, preferred_element_type=F32)
             for h, hc in enumerate(hcols)], axis=1)
        parts = []
        for i in range(C // U):
            lo, hi = i * U, (i + 1) * U
            r = b[lo - 1:lo] if i else jnp.zeros((1, W), F32)
            qrel = q[lo:hi] * jnp.exp(b[lo:hi] - r)
            kall = (kk[:hi] * jnp.exp(r - b[:hi])).astype(BF16)
            qbd = jnp.concatenate([jnp.where(head_of_lane == h, qrel, 0.0) for h in range(NH)], axis=0)
            a = lax.dot_general(qbd.astype(BF16), kall, NT, preferred_element_type=F32)
            trow = lax.broadcasted_iota(jnp.int32, (NH * U, hi), 0) % U
            a = jnp.where(lax.broadcasted_iota(jnp.int32, (NH * U, hi), 1) <= lo + trow, a, 0.0)
            oa = jnp.dot(a.astype(BF16), vb[:hi], preferred_element_type=F32)
            oi = jnp.where(head_of_lane == 0, oa[0:U], 0.0)
            for h in range(1, NH):
                oi = jnp.where(head_of_lane == h, oa[h * U:(h + 1) * U], oi)
            parts.append(oi)
        o = o_inter + jnp.concatenate(parts, axis=0)
        o = jnp.concatenate(
            [o[:, hc] * lax.rsqrt(jnp.mean(o[:, hc] * o[:, hc], axis=-1, keepdims=True) + EPS) for hc in hcols], axis=1)
        o_ref[0, rows, :] = (o * ng_ref[...] * _silu(hg_ref[0, rows, 3 * W:4 * W])).astype(BF16)
        bl = b[C - 1:C]
        ke = (kk * jnp.exp(bl - b)).astype(BF16)
        decay = jnp.exp(bl)
        for h, hc in enumerate(hcols):
            kv_t = lax.dot_general(vb[:, hc], ke[:, hc], TN, preferred_element_type=F32)
            st_scr[h] = decay[:, hc] * st_scr[h] + kv_t
        return carry

    lax.fori_loop(0, ts // C, chunk, 0, unroll=8)


def _hgrn_mxu(hg, lb, ng, ts):
    B, S, _ = hg.shape
    vec = pl.BlockSpec((1, HG_WIDTH), lambda b, i: (0, 0))
    return pl.pallas_call(
        _hgrn_mxu_kernel,
        out_shape=jax.ShapeDtypeStruct((B, S, HG_WIDTH), BF16),
        grid=(B, S // ts),
        in_specs=[pl.BlockSpec((1, ts, 4 * HG_WIDTH), lambda b, i: (b, i, 0)), vec, vec],
        out_specs=pl.BlockSpec((1, ts, HG_WIDTH), lambda b, i: (b, i, 0)),
        scratch_shapes=[pltpu.VMEM((HG_HEADS, HG_DV, HG_DK), F32)],
        compiler_params=_cparams(("arbitrary", "arbitrary")),
        name="hgrn_mxu",
    )(hg, lb, ng)


def _hgrn(hg, lb, ng):
    B, S, _ = hg.shape
    col = lambda k: pl.BlockSpec((1, S, HG_DK), lambda b, h, k=k: (b, 0, k * HG_HEADS + h))
    vec = pl.BlockSpec((1, HG_DK), lambda b, h: (0, h))
    return pl.pallas_call(
        _hgrn_kernel,
        out_shape=jax.ShapeDtypeStruct((B, S, HG_WIDTH), BF16),
        grid=(B, HG_HEADS),
        in_specs=[col(0), col(1), col(2), col(3), vec, vec],
        out_specs=pl.BlockSpec((1, S, HG_DV), lambda b, h: (b, 0, h)),
        compiler_params=_cparams(("arbitrary", "arbitrary")),
        name="hgrn",
    )(hg, hg, hg, hg, lb, ng)


def _cmpmlp_kernel(x_ref, w1a_ref, w1b_ref, pe_ref, w1_ref, b1_ref, w2_ref, o_ref):
    x = x_ref[0]
    hb = jnp.dot(pe_ref[...], w1_ref[...], precision=HIGHEST, preferred_element_type=F32)[0:1] + b1_ref[...]
    nrow = x.shape[0]
    for g in range(NSA_KV_GROUPS):
        a = jnp.dot(x, w1a_ref[g], preferred_element_type=F32)
        bm = jnp.dot(x, w1b_ref[g], preferred_element_type=F32)
        hdn = a + pltpu.roll(bm, nrow - 1, 0) + hb
        o_ref[0, g] = jnp.dot(_gelu_tanh(hdn).astype(BF16), w2_ref[g], preferred_element_type=F32).astype(BF16)


def _cmpmlp(x2, w1a, w1b, pe8, w1, b1, w2p):
    B, ncb, width = x2.shape
    full = lambda a: pl.BlockSpec(a.shape, lambda b: (0,) * a.ndim)
    return pl.pallas_call(
        _cmpmlp_kernel,
        out_shape=jax.ShapeDtypeStruct((B, NSA_KV_GROUPS, ncb, LANES), BF16),
        grid=(B,),
        in_specs=[pl.BlockSpec((1, ncb, width), lambda b: (b, 0, 0)),
                  full(w1a), full(w1b), full(pe8), full(w1), full(b1), full(w2p)],
        out_specs=pl.BlockSpec((1, NSA_KV_GROUPS, ncb, LANES), lambda b: (b, 0, 0, 0)),
        compiler_params=_cparams(("arbitrary",)),
        name="cmpmlp",
    )(x2, w1a, w1b, pe8, w1, b1, w2p)


def _rank_before(score, nrows):
    jrow = lax.broadcasted_iota(jnp.int32, score.shape, 0)
    rank = jnp.zeros(score.shape, F32)
    for k in range(nrows):
        rk = score[k:k + 1]
        beats = (rk > score) | ((rk == score) & (jrow > k))
        rank = rank + jnp.where(beats, 1.0, 0.0)
    return rank


def _topk_rows(score, k):
    n = score.shape[0]
    row = lax.broadcasted_iota(jnp.int32, score.shape, 0).astype(F32)
    keep = jnp.zeros(score.shape, F32)
    for _ in range(k):
        top = jnp.max(score, axis=0, keepdims=True)
        first = jnp.min(jnp.where(score == top, row, float(n)), axis=0, keepdims=True)
        pick = row == first
        keep = jnp.where(pick, 1.0, keep)
        score = jnp.where(pick, -jnp.inf, score)
    return keep


def _transpose_small_ints(xt):
    c = xt.shape[1]
    eye = (lax.broadcasted_iota(jnp.int32, (c, c), 0) == lax.broadcasted_iota(jnp.int32, (c, c), 1)).astype(BF16)
    return lax.dot_general(eye, xt.astype(BF16), NT, preferred_element_type=F32)


def _pack_heads(heads):
    low = lax.broadcasted_iota(jnp.int32, heads[0].shape, 1) < NSA_HD
    return jnp.concatenate([jnp.where(low, heads[k], pltpu.roll(heads[k + 1], NSA_HD, 1))
                            for k in range(0, len(heads), 2)], axis=1)


def _cmpsel_kernel(q_ref, kc_ref, vc_ref, gate_ref, o_ref, sel_ref):
    tq = q_ref.shape[1]
    ncb = kc_ref.shape[2]
    t = pl.program_id(2)
    kc = kc_ref[0, 0]
    vc = vc_ref[0, 0]
    pos = t * tq + lax.broadcasted_iota(jnp.int32, (tq, 1), 0)
    cblk = lax.broadcasted_iota(jnp.int32, (1, ncb), 1)
    vis = (cblk * CMP_STRIDE + CMP_BLOCK - 1) <= pos
    psum = jnp.zeros((tq, ncb), F32)
    gate = gate_ref[0]
    heads = []
    for h in range(NSA_HG):
        s = lax.dot_general(q_ref[0, :, h * LANES:(h + 1) * LANES], kc, NT, preferred_element_type=F32)
        s = jnp.where(vis, s, NEG)
        p = jnp.exp(s - jnp.max(s, axis=-1, keepdims=True))
        p = jnp.where(vis, p / jnp.sum(p, axis=-1, keepdims=True), 0.0)
        heads.append(jnp.dot(p.astype(BF16), vc, preferred_element_type=F32) * gate[:, 3 * h:3 * h + 1])
        psum = psum + p
    o_ref[0] = _pack_heads(heads)
    nsb = ncb * CMP_STRIDE // SEL_BLOCK
    jb = lax.broadcasted_iota(jnp.int32, (nsb, ncb), 0) * SEL_BLOCK
    cb = lax.broadcasted_iota(jnp.int32, (nsb, ncb), 1) * CMP_STRIDE
    ov = jnp.maximum(jnp.minimum(cb + CMP_BLOCK, jb + SEL_BLOCK) - jnp.maximum(cb, jb), 0).astype(F32) / CMP_BLOCK
    ps_hi = psum.astype(BF16)
    ps_lo = (psum - ps_hi.astype(F32)).astype(BF16)
    ovb = ov.astype(BF16)
    pslc_t = (lax.dot_general(ovb, ps_hi, NT, preferred_element_type=F32)
              + lax.dot_general(ovb, ps_lo, NT, preferred_element_type=F32))
    posl = t * tq + lax.broadcasted_iota(jnp.int32, (1, tq), 1)
    cur = posl // SEL_BLOCK
    jrow = lax.broadcasted_iota(jnp.int32, (nsb, tq), 0)
    forced = (jrow == 0) | (jrow == cur) | (jrow == cur - 1)
    score = jnp.where(forced, 1e30, jnp.where(jrow <= cur, pslc_t, NEG))
    chosen = _topk_rows(score, min(N_SEL, nsb)) > 0.5
    drop = jnp.where(chosen & (score > -1e29), 0.0, 1.0)
    drop = jnp.concatenate([drop, jnp.zeros((LANES - nsb, tq), F32)], axis=0)
    sel_ref[0, 0] = (_transpose_small_ints(drop) * NEG).astype(BF16)


def _cmpsel(qraw, kcmp, vcmp, gate, tq):
    B, S, _ = qraw.shape
    ncb = kcmp.shape[2]
    gw = NSA_HG * LANES
    assert S // SEL_BLOCK <= NSA_HD
    cmp_spec = pl.BlockSpec((1, 1, ncb, LANES), lambda b, g, t: (b, g, 0, 0))
    return pl.pallas_call(
        _cmpsel_kernel,
        out_shape=(jax.ShapeDtypeStruct((B, S, NSA_WIDTH), F32),
                   jax.ShapeDtypeStruct((B, NSA_KV_GROUPS, S, LANES), BF16)),
        grid=(B, NSA_KV_GROUPS, S // tq),
        in_specs=[pl.BlockSpec((1, tq, gw), lambda b, g, t: (b, t, g)), cmp_spec, cmp_spec,
                  pl.BlockSpec((1, tq, LANES), lambda b, g, t: (b, t, g))],
        out_specs=(pl.BlockSpec((1, tq, NSA_WIDTH // NSA_KV_GROUPS), lambda b, g, t: (b, t, g)),
                   pl.BlockSpec((1, 1, tq, LANES), lambda b, g, t: (b, g, t, 0))),
        compiler_params=_cparams(("arbitrary", "arbitrary", "arbitrary")),
        name="cmpsel",
    )(qraw, kcmp, vcmp, gate)


def _slcwin_kernel(q_ref, sel_ref, ks_ref, kw_ref, vs_ref, vw_ref, gate_ref, ocmp_ref, wgu_ref, wdn_ref,
                   o_ref, wgu_bf_ref, wdn_bf_ref, *, tk, wk):
    wgu_bf_ref[...] = wgu_ref[...].astype(BF16)
    wdn_bf_ref[...] = wdn_ref[...].astype(BF16)
    tq = q_ref.shape[1]
    S = ks_ref.shape[1]
    t = pl.program_id(2)
    q0 = t * tq
    qpos = q0 + lax.broadcasted_iota(jnp.int32, (tq, 1), 0)
    qw = [q_ref[0, :, h * LANES:(h + 1) * LANES] for h in range(NSA_HG)]
    sel = sel_ref[0, 0]
    qa = [jnp.concatenate([q, sel], axis=1) for q in qw]

    gate = gate_ref[0]

    def normalise(acc, g):
        return acc * (g / acc[:, NSA_HD:NSA_HD + 1])

    nchain = NSA_HG // SLC_STACK
    qst = [jnp.concatenate(qa[c * SLC_STACK:(c + 1) * SLC_STACK], axis=0) for c in range(nchain)]
    qpos_st = jnp.concatenate([qpos] * SLC_STACK, axis=0)

    def kstep(j, carry, diag):
        k0 = pl.multiple_of(j * tk, tk)
        ks = ks_ref[0, pl.ds(k0, tk), :]
        vs = vs_ref[0, pl.ds(k0, tk), :]
        if diag:
            vis = (k0 + lax.broadcasted_iota(jnp.int32, (1, tk), 1)) <= qpos_st
        out = []
        for c in range(nchain):
            m, acc = carry[c]
            s = lax.dot_general(qst[c], ks, NT, preferred_element_type=F32)
            if diag:
                s = jnp.where(vis, s, NEG)
            mn = jnp.maximum(m, jnp.max(s, axis=-1, keepdims=True))
            p = jnp.exp2(s - mn)
            acc = jnp.exp2(m - mn) * acc + jnp.dot(p.astype(BF16), vs, preferred_element_type=F32)
            out.append((mn, acc))
        return tuple(out)

    init = tuple((jnp.full((SLC_STACK * tq, 1), NEG, F32), jnp.zeros((SLC_STACK * tq, LANES), F32))
                 for _ in range(nchain))
    jdiag = q0 // tk
    carry = lax.fori_loop(0, jdiag, lambda j, c: kstep(j, c, False), init)
    carry = kstep(jdiag, carry, True)
    heads = [normalise(carry[h // SLC_STACK][1][(h % SLC_STACK) * tq:(h % SLC_STACK + 1) * tq],
                       gate[:, 3 * h + 1:3 * h + 2]) for h in range(NSA_HG)]

    wq = wk - WINDOW
    wins = [[] for _ in range(NSA_HG)]
    spos = lax.broadcasted_iota(jnp.int32, (NSA_HG * wq, 1), 0) % wq
    for i in range(tq // wq):
        rows = slice(i * wq, (i + 1) * wq)
        start = pl.multiple_of(jnp.maximum(q0 + (i + 1) * wq - wk, 0), wq)
        kw = kw_ref[0, pl.ds(start, wk), :]
        vw = vw_ref[0, pl.ds(start, wk), :]
        d = (q0 + i * wq + spos) - (start + lax.broadcasted_iota(jnp.int32, (1, wk), 1))
        q_st = jnp.concatenate([q[rows] for q in qw], axis=0)
        s = jnp.where((d >= 0) & (d < WINDOW), lax.dot_general(q_st, kw, NT, preferred_element_type=F32), NEG)
        p = jnp.exp2(s - jnp.max(s, axis=-1, keepdims=True))
        o_st = jnp.dot(p.astype(BF16), vw, preferred_element_type=F32)
        for h in range(NSA_HG):
            wins[h].append(o_st[h * wq:(h + 1) * wq])
    for h in range(NSA_HG):
        heads[h] = heads[h] + normalise(jnp.concatenate(wins[h], axis=0), gate[:, 3 * h + 2:3 * h + 3])
    o_ref[0] = (ocmp_ref[0] + _pack_heads(heads)).astype(BF16)


def _slcwin(qrot, sel, ks, kw, vaug, gate, ocmp, wgu, wdn, tq, tk):
    B, S, _ = qrot.shape
    gw = NSA_HG * LANES
    wk = WINDOW + WIN_Q
    nt = S // tq
    nstep = B * NSA_KV_GROUPS * nt
    assert S >= wk and tq % WIN_Q == 0 and N_EXPERTS % nstep == 0
    epb = N_EXPERTS // nstep
    ospec = pl.BlockSpec((1, tq, NSA_WIDTH // NSA_KV_GROUPS), lambda b, g, t: (b, t, g))
    wspec = lambda a: pl.BlockSpec((epb,) + a.shape[1:], lambda b, g, t: ((b * NSA_KV_GROUPS + g) * nt + t, 0, 0))
    return pl.pallas_call(
        functools.partial(_slcwin_kernel, tk=tk, wk=wk),
        out_shape=(jax.ShapeDtypeStruct((B, S, NSA_WIDTH), BF16),
                   jax.ShapeDtypeStruct(wgu.shape, BF16), jax.ShapeDtypeStruct(wdn.shape, BF16)),
        grid=(B, NSA_KV_GROUPS, nt),
        in_specs=[pl.BlockSpec((1, tq, gw), lambda b, g, t: (b, t, g)),
                  pl.BlockSpec((1, 1, tq, LANES), lambda b, g, t: (b, g, t, 0)),
                  pl.BlockSpec((1, S, 2 * LANES), lambda b, g, t: (b, 0, 0)),
                  pl.BlockSpec((1, S, LANES), lambda b, g, t: (b, 0, 0)),
                  pl.BlockSpec((1, S, LANES), lambda b, g, t: (b, 0, g)),
                  pl.BlockSpec((1, S, LANES), lambda b, g, t: (b, 0, NSA_KV_GROUPS + g)),
                  pl.BlockSpec((1, tq, LANES), lambda b, g, t: (b, t, g)),
                  ospec, wspec(wgu), wspec(wdn)],
        out_specs=(ospec, wspec(wgu), wspec(wdn)),
        compiler_params=_cparams(("arbitrary", "arbitrary", "arbitrary")),
        name="slcwin",
    )(qrot, sel, ks, kw, vaug, vaug, gate, ocmp, wgu, wdn)


def _outproj_kernel(x_ref, ohg_ref, onsa_ref, wout_ref,
                    g1_ref, sh2_ref, sc2_ref, n2_ref, rwt_ref, rb_ref,
                    x1_ref, h2_ref, sc_ref, slot_t_ref, w_t_ref, cnt_ref):
    tm = x_ref.shape[1]
    mix = jnp.dot(jnp.concatenate([ohg_ref[0], onsa_ref[0]], axis=1), wout_ref[...], preferred_element_type=F32)
    x1 = x_ref[0] + g1_ref[0] * mix
    x1_ref[0] = x1
    h2 = (x1 * lax.rsqrt(jnp.mean(x1 * x1, axis=-1, keepdims=True) + EPS) * n2_ref[...]) * (1.0 + sc2_ref[0]) + sh2_ref[0]
    h_hi = h2.astype(BF16)
    h2_ref[0] = h_hi
    h_lo = (h2 - h_hi.astype(F32)).astype(BF16)
    both = lax.dot_general(rwt_ref[...], h_hi, NT, preferred_element_type=F32)
    logits = (both[:N_EXPERTS] + both[N_EXPERTS:]
              + lax.dot_general(rwt_ref[:N_EXPERTS, :], h_lo, NT, preferred_element_type=F32))
    scores = _sigmoid(logits)
    choice = scores + rb_ref[...]
    per = N_EXPERTS // N_GROUPS
    c3 = choice.reshape(N_GROUPS, per, tm)
    erow = lax.broadcasted_iota(jnp.int32, c3.shape, 1)
    rank_in = jnp.zeros(c3.shape, F32)
    for k in range(per):
        ck = c3[:, k:k + 1, :]
        rank_in = rank_in + jnp.where((ck > c3) | ((ck == c3) & (erow > k)), 1.0, 0.0)
    grp_score = jnp.sum(jnp.where(rank_in < 2, c3, 0.0), axis=1)
    grp_keep = _rank_before(grp_score, N_GROUPS) < TOPK_GROUPS
    masked = jnp.where(grp_keep[:, None, :], c3, -jnp.inf).reshape(N_EXPERTS, tm)
    keep = _topk_rows(masked, TOP_K) > 0.5
    tw = jnp.where(keep, scores, 0.0)
    tw = tw / jnp.sum(tw, axis=0, keepdims=True) * ROUTED_SCALE
    before = lax.broadcasted_iota(jnp.int32, (tm, tm), 0)
    after = lax.broadcasted_iota(jnp.int32, (tm, tm), 1)
    earlier = (before < after) & (before // MOE_SUB == after // MOE_SUB)
    kept = jnp.where(keep, 1.0, 0.0)
    pref = jnp.dot(kept.astype(BF16), earlier.astype(BF16), preferred_element_type=F32)
    slot = jnp.where(keep, pref, -1.0)
    assert MOE_SUB <= 256
    sc_ref[0] = _transpose_small_ints(slot)
    for j in range(tm // MOE_SUB):
        sub = slice(j * MOE_SUB, (j + 1) * MOE_SUB)
        slot_t_ref[:, j] = slot[:, sub].reshape(N_EXPERTS // MOE_EG, MOE_EG, MOE_SUB)
        w_t_ref[:, j] = tw[:, sub].reshape(N_EXPERTS // MOE_EG, MOE_EG, MOE_SUB)
        cnt_ref[j] = jnp.broadcast_to(jnp.sum(kept[:, sub], axis=1, keepdims=True), (N_EXPERTS, LANES))


def _outproj(x, ohg, onsa, wout, mod3, n2, rwt, rb, tm):
    B, S, D = x.shape
    blk = lambda w: pl.BlockSpec((1, tm, w), lambda b, i: (b, i, 0))
    full = lambda a: pl.BlockSpec(a.shape, lambda b, i: (0,) * a.ndim)
    modc = lambda k: pl.BlockSpec((1, 1, D), lambda b, i, k=k: (b, 0, k))
    nt = S // tm
    ng = N_EXPERTS // MOE_EG
    nh = tm // MOE_SUB
    tspec = pl.BlockSpec((ng, nh, MOE_EG, MOE_SUB), lambda b, i: (0, b * nt + i, 0, 0))
    return pl.pallas_call(
        _outproj_kernel,
        out_shape=(jax.ShapeDtypeStruct((B, S, D), F32),
                   jax.ShapeDtypeStruct((B, S, D), BF16),
                   jax.ShapeDtypeStruct((B, S, N_EXPERTS), F32),
                   jax.ShapeDtypeStruct((ng, B * nt * nh, MOE_EG, MOE_SUB), F32),
                   jax.ShapeDtypeStruct((ng, B * nt * nh, MOE_EG, MOE_SUB), F32),
                   jax.ShapeDtypeStruct((B * nt * nh, N_EXPERTS, LANES), F32)),
        grid=(B, nt),
        in_specs=[blk(D), blk(HG_WIDTH), blk(NSA_WIDTH), full(wout),
                  modc(2), modc(3), modc(4), full(n2), full(rwt), full(rb)],
        out_specs=(blk(D), blk(D), blk(N_EXPERTS), tspec, tspec,
                   pl.BlockSpec((nh, N_EXPERTS, LANES), lambda b, i: (b * nt + i, 0, 0))),
        compiler_params=_cparams(("arbitrary", "arbitrary")),
        name="outproj",
    )(x, ohg, onsa, wout, mod3, mod3, mod3, n2, rwt, rb)


def _swiglu_hidden(x, wgu):
    gu = jnp.dot(x, wgu, preferred_element_type=F32)
    return _silu(gu[:, :EXPERT_HIDDEN]) * gu[:, EXPERT_HIDDEN:]


def _moe_kernel(rounds_ref, h_ref, slot_t_ref, w_t_ref, slot_c_ref, wgu_ref, wdn_ref, sgu_ref, sdn_ref,
                o_ref, x_scr, p_scr):
    i = pl.program_id(0)
    g = pl.program_id(1)
    tm = h_ref.shape[0]
    nsub = tm // MOE_SUB
    rnd, eg = MOE_ROUND, MOE_EG
    ng = N_EXPERTS // eg
    base = i * (N_EXPERTS + ng)

    @pl.when(g == 0)
    def _():
        act = _swiglu_hidden(h_ref[...], sgu_ref[...]).astype(BF16)
        o_ref[...] = jnp.dot(act, sdn_ref[...], preferred_element_type=F32)

    blk = eg * rnd
    rslot = lax.broadcasted_iota(jnp.int32, (rnd, 1), 0).astype(F32)

    def gather(first, nround):
        for s in range(nsub):
            st = slot_t_ref[0, s] - first
            p = jnp.concatenate([jnp.where(rslot + float(w * rnd) == st[e:e + 1, :], 1.0, 0.0)
                                 for w in range(nround) for e in range(eg)], axis=0)
            p_scr[s, 0:nround * blk, :] = p.astype(BF16)
            x_scr[s, 0:nround * blk, :] = jnp.dot(p_scr[s, 0:nround * blk, :], h_ref[s * MOE_SUB:(s + 1) * MOE_SUB, :],
                                                  preferred_element_type=F32).astype(BF16)

    def run_expert(e, w):
        start = w * blk + e * rnd
        rows = pl.ds(start if isinstance(e, int) else pl.multiple_of(start, rnd), rnd)
        xe = jnp.concatenate([x_scr[s, rows, :] for s in range(nsub)], axis=0)
        wslot = jnp.concatenate(
            [jnp.sum(p_scr[s, rows, :].astype(F32) * w_t_ref[0, s, pl.ds(e, 1), :], axis=-1, keepdims=True)
             for s in range(nsub)], axis=0)
        y = jnp.dot((_swiglu_hidden(xe, wgu_ref[e]) * wslot).astype(BF16), wdn_ref[e], preferred_element_type=F32)
        for s in range(nsub):
            x_scr[s, rows, :] = y[s * rnd:(s + 1) * rnd].astype(BF16)

    def run_active(r, w):
        def body(e, carry):
            pl.when(rounds_ref[base + g * eg + e] > r)(functools.partial(run_expert, e, w))
            return carry
        lax.fori_loop(0, eg, body, 0)

    def scatter(first, nround):
        lane = lax.broadcasted_iota(jnp.int32, (1, nround * blk), 1)
        lane_slot = ((lane // blk) * rnd + lane % rnd).astype(F32)
        spread = ((lax.broadcasted_iota(jnp.int32, (N_EXPERTS, nround * blk), 1) % blk) // rnd + g * eg
                  == lax.broadcasted_iota(jnp.int32, (N_EXPERTS, nround * blk), 0)).astype(BF16)
        for s in range(nsub):
            toks = slice(s * MOE_SUB, (s + 1) * MOE_SUB)
            sc = jnp.dot(slot_c_ref[toks, :].astype(BF16), spread, preferred_element_type=F32) - first
            pt = jnp.where(sc == lane_slot, 1.0, 0.0).astype(BF16)
            o_ref[toks, :] += jnp.dot(pt, x_scr[s, 0:nround * blk, :], preferred_element_type=F32)

    gather(0.0, MOE_FIRST)
    for e in range(eg):
        run_expert(e, 0)
    for w in range(1, MOE_FIRST):
        run_active(w, w)
    scatter(0.0, MOE_FIRST)

    def later_round(r, carry):
        first = (r * rnd).astype(F32)
        gather(first, 1)
        run_active(r, 0)
        scatter(first, 1)
        return carry

    lax.fori_loop(MOE_FIRST, rounds_ref[base + N_EXPERTS + g], later_round, 0)


def _moe(rounds, h2, slot_t, w_t, slot_c, wgu, wdn, sgu, sdn, tm):
    T, D = h2.shape
    eg, nsub = MOE_EG, tm // MOE_SUB
    full = lambda a: pl.BlockSpec(a.shape, lambda i, e, o: (0,) * a.ndim)
    tspec = pl.BlockSpec((1, nsub, eg, MOE_SUB), lambda i, e, o: (e, i, 0, 0))
    return pl.pallas_call(
        _moe_kernel,
        out_shape=jax.ShapeDtypeStruct((T, D), F32),
        grid_spec=pltpu.PrefetchScalarGridSpec(
            num_scalar_prefetch=1,
            grid=(T // tm, N_EXPERTS // eg),
            in_specs=[pl.BlockSpec((tm, D), lambda i, e, o: (i, 0), pipeline_mode=pl.Buffered(1)), tspec, tspec,
                      pl.BlockSpec((tm, N_EXPERTS), lambda i, e, o: (i, 0)),
                      pl.BlockSpec((eg, D, 2 * EXPERT_HIDDEN), lambda i, e, o: (e, 0, 0)),
                      pl.BlockSpec((eg, EXPERT_HIDDEN, D), lambda i, e, o: (e, 0, 0)),
                      full(sgu), full(sdn)],
            out_specs=pl.BlockSpec((tm, D), lambda i, e, o: (i, 0), pipeline_mode=pl.Buffered(1)),
            scratch_shapes=[pltpu.VMEM((nsub, MOE_FIRST * eg * MOE_ROUND, D), BF16),
                            pltpu.VMEM((nsub, MOE_FIRST * eg * MOE_ROUND, MOE_SUB), BF16)]),
        compiler_params=_cparams(("arbitrary", "arbitrary")),
        name="moe",
    )(rounds, h2, slot_t, w_t, slot_c, wgu, wdn, sgu, sdn)


def _final_kernel(x1_ref, moe_ref, g2_ref, fg_ref, o_ref):
    x2 = x1_ref[0] + g2_ref[0] * moe_ref[0]
    o_ref[0] = x2 * lax.rsqrt(jnp.mean(x2 * x2, axis=-1, keepdims=True) + EPS) * fg_ref[...]


def _final(x1, moe, mod3, fg, tm):
    B, S, D = x1.shape
    blk = pl.BlockSpec((1, tm, D), lambda b, i: (b, i, 0))
    return pl.pallas_call(
        _final_kernel,
        out_shape=jax.ShapeDtypeStruct((B, S, D), F32),
        grid=(B, S // tm),
        in_specs=[blk, blk, pl.BlockSpec((1, 1, D), lambda b, i: (b, 0, 5)), pl.BlockSpec((1, D), lambda b, i: (0, 0))],
        out_specs=blk,
        compiler_params=_cparams(("arbitrary", "arbitrary")),
        name="final",
    )(x1, moe, mod3, fg)


def _split_bf16(w):
    hi = w.astype(BF16)
    return jnp.concatenate([hi, (w - hi.astype(F32)).astype(BF16)], axis=0)


def _pack_w_in(w_in):
    return jnp.pad(w_in, ((0, 0), (0, IN_COLS_P - w_in.shape[1]))).astype(BF16)


def _pack_cmp(pos, w1, b1, w2, lane_by_group):
    half = CMP_STRIDE * NSA_HD
    def rows_for(wh):
        w3 = wh.reshape(CMP_STRIDE, NSA_HD, CMP_HIDDEN)
        z = jnp.zeros_like(w3)
        return jnp.stack([jnp.concatenate([w3, z], axis=1).reshape(CMP_STRIDE * LANES, CMP_HIDDEN),
                          jnp.concatenate([z, w3], axis=1).reshape(CMP_STRIDE * LANES, CMP_HIDDEN)])
    w1a = rows_for(w1[:half]).astype(BF16)
    w1b = rows_for(w1[half:]).astype(BF16)
    z2 = jnp.zeros_like(w2)
    w2_first = jnp.concatenate([w2, z2], axis=1)
    w2p = jnp.stack([w2_first, jnp.concatenate([z2, w2], axis=1) if lane_by_group else w2_first]).astype(BF16)
    pe8 = jnp.pad(pos.reshape(1, CMP_BLOCK * NSA_HD), ((0, 7), (0, 0)))
    return w1a, w1b, pe8, w1, b1.reshape(1, CMP_HIDDEN), w2p


def _rope_tables(S):
    half = NSA_HD // 2
    inv = ROPE_THETA ** (-jnp.arange(half, dtype=F32) / half)
    ang = jnp.arange(S, dtype=F32)[:, None] * inv[None, :]
    cos, sin = jnp.cos(ang), jnp.sin(ang)
    reps = LANES // NSA_HD
    return jnp.tile(jnp.concatenate([cos, cos], axis=1), (1, reps)), jnp.tile(jnp.concatenate([-sin, sin], axis=1), (1, reps))


def _tiles(S):
    return dict(inproj=min(512, S), hgrn=min(512, S), cmpsel=min(512, S), slc_q=512, slc_k=512, moe=min(2048, S), final=min(512, S))


def kernel(x, c, w_ada, b_ada, norm1_g, w_in, hg_lb_logits, hg_norm_g, cmp_pos_k, cmp_w1_k, cmp_b1_k, cmp_w2_k,
           cmp_pos_v, cmp_w1_v, cmp_b1_v, cmp_w2_v, w_out, norm2_g, router_w, router_bias, w_exp_gu, w_exp_dn,
           w_sh_gu, w_sh_dn, final_g):
    B, S, D = x.shape
    assert D == D_MODEL and w_ada.shape[0] == 1 and S % 512 == 0
    tl = _tiles(S)
    l = 0
    lb = jnp.cumsum(jax.nn.softmax(hg_lb_logits.astype(F32), axis=0), axis=0)[l].reshape(1, HG_WIDTH)
    c8 = jnp.pad(c, ((0, 8 - B), (0, 0)))
    mod3 = _ada(c8, w_ada[l], b_ada[l].reshape(1, -1))[:B].reshape(B, 1, 6 * D)
    cos, sin = _rope_tables(S)
    hg, qraw, qrot, kc, vc, ks, kw, vaug, gate = _inproj(x, mod3, norm1_g[l].reshape(1, D), _pack_w_in(w_in[l]),
                                                         cos, sin, tl["inproj"])
    ng = hg_norm_g[l].reshape(1, HG_WIDTH)
    mxu_safe = -HG_SUB * jnp.log(jnp.min(lb)) <= HG_SAFE_LOG_RANGE
    ohg = lax.cond(mxu_safe, lambda: _hgrn_mxu(hg, lb, ng, tl["hgrn"]), lambda: _hgrn(hg, lb, ng))
    ncb = S // CMP_STRIDE
    kcmp = _cmpmlp(kc.reshape(B, ncb, CMP_STRIDE * LANES),
                   *_pack_cmp(cmp_pos_k[l], cmp_w1_k[l], cmp_b1_k[l], cmp_w2_k[l], True))
    vcmp = _cmpmlp(vc.reshape(B, ncb, CMP_STRIDE * LANES),
                   *_pack_cmp(cmp_pos_v[l], cmp_w1_v[l], cmp_b1_v[l], cmp_w2_v[l], False))
    ocmp, sel = _cmpsel(qraw, kcmp, vcmp, gate, tl["cmpsel"])
    onsa, wgu_bf, wdn_bf = _slcwin(qrot, sel, ks, kw, vaug, gate, ocmp, w_exp_gu[l], w_exp_dn[l],
                                   tl["slc_q"], tl["slc_k"])
    x1, h2, sc, slot_t, w_t, cnt = _outproj(x, ohg, onsa, w_out[l].astype(BF16), mod3,
                                            norm2_g[l].reshape(1, D), _split_bf16(router_w[l].T),
                                            router_bias[l].reshape(N_EXPERTS, 1), ROUTE_TM)
    T = B * S
    tm = tl["moe"]
    ng = N_EXPERTS // MOE_EG
    slot_c = sc.reshape(T, N_EXPERTS)
    load = jnp.max(cnt[:, :, 0].reshape(T // tm, tm // MOE_SUB, N_EXPERTS), axis=1).astype(jnp.int32)
    per_expert = (load + MOE_ROUND - 1) // MOE_ROUND
    rounds = jnp.concatenate([per_expert, jnp.max(per_expert.reshape(T // tm, ng, MOE_EG), axis=2)], axis=1).reshape(-1)
    moe = _moe(rounds, h2.reshape(T, D), slot_t, w_t, slot_c,
               wgu_bf, wdn_bf, w_sh_gu[l].astype(BF16), w_sh_dn[l].astype(BF16), tm)
    return _final(x1, moe.reshape(B, S, D), mod3, final_g.reshape(1, D), tl["final"])
```

```python
import functools

import jax
import jax.numpy as jnp
from jax import lax
from jax.experimental import pallas as pl
from jax.experimental.pallas import tpu as pltpu

F32 = jnp.float32
BF16 = jnp.bfloat16
HIGHEST = lax.Precision.HIGHEST

D_MODEL = 1024
EPS = 1e-6
HG_HEADS = 4
HG_DK = 128
HG_DV = 128
HG_WIDTH = HG_HEADS * HG_DV
HG_CHUNK = 64
HG_SUB = 16
HG_SAFE_LOG_RANGE = 80.0
NSA_HEADS = 8
NSA_KV_GROUPS = 2
NSA_HG = NSA_HEADS // NSA_KV_GROUPS
NSA_HD = 64
NSA_WIDTH = NSA_HEADS * NSA_HD
NSA_KV = NSA_KV_GROUPS * NSA_HD
CMP_BLOCK = 32
CMP_STRIDE = 16
CMP_HIDDEN = 256
SEL_BLOCK = 64
N_SEL = 16
WINDOW = 512
ROPE_THETA = 10000.0
SLC_STACK = 4
WIN_Q = 256
N_EXPERTS = 64
TOP_K = 8
N_GROUPS = 8
TOPK_GROUPS = 4
EXPERT_HIDDEN = 256
ROUTED_SCALE = 2.5
ROUTE_TM = 512
MOE_SUB = 256
MOE_ROUND = 32
MOE_EG = 8
MOE_FIRST = 3

LANES = 128
NEG = -1e30
VMEM_LIMIT = 61 * 1024 * 1024

C_HG = 0
C_Q = 4 * HG_WIDTH
C_K = C_Q + NSA_WIDTH
C_GATE = C_K + 6 * NSA_KV
IN_COLS_P = C_GATE + LANES
LOG2E = 1.4426950408889634

NT = (((1,), (1,)), ((), ()))
TN = (((0,), (0,)), ((), ()))


def _cparams(sem):
    return pltpu.CompilerParams(dimension_semantics=sem, vmem_limit_bytes=VMEM_LIMIT)


def _sigmoid(x):
    return 1.0 / (1.0 + jnp.exp(-x))


def _silu(x):
    return x * _sigmoid(x)


def _gelu_tanh(x):
    return 0.5 * x * (1.0 + jnp.tanh(0.7978845608028654 * (x + 0.044715 * (x * x * x))))


def _ada_kernel(c_ref, w_ref, b_ref, o_ref):
    a = _silu(c_ref[...])
    o_ref[...] = jnp.dot(a, w_ref[...], precision=HIGHEST, preferred_element_type=F32) + b_ref[...]


def _ada(c8, w, b):
    n = w.shape[1]
    tn = 1024
    return pl.pallas_call(
        _ada_kernel,
        out_shape=jax.ShapeDtypeStruct((8, n), F32),
        grid=(n // tn,),
        in_specs=[pl.BlockSpec((8, D_MODEL), lambda j: (0, 0)),
                  pl.BlockSpec((D_MODEL, tn), lambda j: (0, j)),
                  pl.BlockSpec((1, tn), lambda j: (0, j))],
        out_specs=pl.BlockSpec((8, tn), lambda j: (0, j)),
        compiler_params=_cparams(("arbitrary",)),
        name="ada",
    )(c8, w, b)


def _rope(t, cos, sin_signed, first_half):
    rot = jnp.where(first_half, pltpu.roll(t, 96, 1), pltpu.roll(t, 32, 1))
    return t * cos + rot * sin_signed


def _inproj_kernel(x_ref, sh_ref, sc_ref, g_ref, w_ref, cos_ref, sin_ref,
                   hg_ref, qraw_ref, qrot_ref, kc_ref, vc_ref, ks_ref, kw_ref, va_ref, gate_ref, h_scr):
    tm = x_ref.shape[1]
    x = x_ref[0]
    y = x * lax.rsqrt(jnp.mean(x * x, axis=-1, keepdims=True) + EPS) * g_ref[...]
    h_scr[...] = (y * (1.0 + sc_ref[0]) + sh_ref[0]).astype(BF16)

    def mm(lo, width):
        return jnp.dot(h_scr[...], w_ref[:, lo:lo + width], preferred_element_type=F32)

    cos = cos_ref[...]
    sin = sin_ref[...]
    first_half = (lax.broadcasted_iota(jnp.int32, cos.shape, 1) % NSA_HD) < (NSA_HD // 2)
    lane = lax.broadcasted_iota(jnp.int32, (tm, LANES), 1)
    half_of_lane = lane // NSA_HD

    def own_lanes(t, src_half, dst_half):
        moved = t if src_half == dst_half else pltpu.roll(t, NSA_HD, 1)
        return jnp.where(half_of_lane == dst_half, moved, 0.0)

    for j in range(4):
        hg_ref[0, :, j * HG_WIDTH:(j + 1) * HG_WIDTH] = mm(C_HG + j * HG_WIDTH, HG_WIDTH)
    for m in range(NSA_WIDTH // (2 * LANES)):
        qq = mm(C_Q + m * 2 * LANES, 2 * LANES) * (NSA_HD ** -0.5)
        for c in range(2):
            q = qq[:, c * LANES:(c + 1) * LANES]
            qr = _rope(q, cos, sin, first_half) * LOG2E
            for half in range(2):
                n = (2 * m + c) * 2 + half
                g = n // NSA_HG
                qraw_ref[0, :, n * LANES:(n + 1) * LANES] = own_lanes(q, half, g).astype(BF16)
                qrot_ref[0, :, n * LANES:(n + 1) * LANES] = own_lanes(qr, half, g).astype(BF16)
    kv = mm(C_K, 2 * LANES)
    kc_ref[0] = kv[:, :LANES].astype(BF16)
    vc_ref[0] = kv[:, LANES:].astype(BF16)
    pos = pl.program_id(1) * tm + lax.broadcasted_iota(jnp.int32, (tm, LANES), 0)
    ks_ref[0, :, LANES:2 * LANES] = jnp.where(lane == pos // SEL_BLOCK, 1.0, 0.0).astype(BF16)
    for j in range(2):
        kv = mm(C_K + (j + 1) * 2 * LANES, 2 * LANES)
        k_rot = _rope(kv[:, :LANES], cos, sin, first_half).astype(BF16)
        if j == 0:
            ks_ref[0, :, 0:LANES] = k_rot
        else:
            kw_ref[0] = k_rot
        for g in range(NSA_KV_GROUPS):
            v = own_lanes(kv[:, LANES:], g, 0)
            col = (j * NSA_KV_GROUPS + g) * LANES
            va_ref[0, :, col:col + LANES] = jnp.where(lane == NSA_HD, 1.0, v).astype(BF16)
    gate = _sigmoid(mm(C_GATE, LANES))
    per_group = 3 * NSA_HG
    for g in range(NSA_KV_GROUPS):
        gate_ref[0, :, g * LANES:(g + 1) * LANES] = gate if g == 0 else pltpu.roll(gate, LANES - g * per_group, 1)


def _inproj(x, mod3, norm_g, w_p, cos, sin, tm):
    B, S, D = x.shape
    blk = lambda w: pl.BlockSpec((1, tm, w), lambda b, i: (b, i, 0))
    return pl.pallas_call(
        _inproj_kernel,
        out_shape=(jax.ShapeDtypeStruct((B, S, 4 * HG_WIDTH), F32),
                   jax.ShapeDtypeStruct((B, S, NSA_HEADS * LANES), BF16),
                   jax.ShapeDtypeStruct((B, S, NSA_HEADS * LANES), BF16),
                   jax.ShapeDtypeStruct((B, S, LANES), BF16),
                   jax.ShapeDtypeStruct((B, S, LANES), BF16),
                   jax.ShapeDtypeStruct((B, S, 2 * LANES), BF16),
                   jax.ShapeDtypeStruct((B, S, LANES), BF16),
                   jax.ShapeDtypeStruct((B, S, 4 * LANES), BF16),
                   jax.ShapeDtypeStruct((B, S, NSA_KV_GROUPS * LANES), F32)),
        grid=(B, S // tm),
        in_specs=[blk(D),
                  pl.BlockSpec((1, 1, D), lambda b, i: (b, 0, 0)),
                  pl.BlockSpec((1, 1, D), lambda b, i: (b, 0, 1)),
                  pl.BlockSpec((1, D), lambda b, i: (0, 0)),
                  pl.BlockSpec((D, IN_COLS_P), lambda b, i: (0, 0)),
                  pl.BlockSpec((tm, LANES), lambda b, i: (i, 0)),
                  pl.BlockSpec((tm, LANES), lambda b, i: (i, 0))],
        out_specs=(blk(4 * HG_WIDTH), blk(NSA_HEADS * LANES), blk(NSA_HEADS * LANES),
                   blk(LANES), blk(LANES), blk(2 * LANES), blk(LANES), blk(4 * LANES), blk(NSA_KV_GROUPS * LANES)),
        scratch_shapes=[pltpu.VMEM((tm, D), BF16)],
        compiler_params=_cparams(("arbitrary", "arbitrary")),
        name="inproj",
    )(x, mod3, mod3, norm_g, w_p, cos, sin)


def _hgrn_kernel(q_ref, f_ref, i_ref, gt_ref, lb_ref, ng_ref, o_ref):
    S = q_ref.shape[1]
    C, U = HG_CHUNK, HG_SUB
    lb = lb_ref[...]
    ng = ng_ref[...]
    ri = lax.broadcasted_iota(jnp.int32, (C, C), 0)
    ci = lax.broadcasted_iota(jnp.int32, (C, C), 1)
    tril = (ri >= ci).astype(F32)
    trow = lax.broadcasted_iota(jnp.int32, (U, 1), 0)

    def chunk(c, st_t):
        r0 = pl.multiple_of(c * C, C)
        rows = pl.ds(r0, C)
        f = lb + (1.0 - lb) * _sigmoid(f_ref[0, rows, :])
        kk = 1.0 - f
        b = jnp.dot(tril, jnp.log(f), precision=HIGHEST, preferred_element_type=F32)
        q = q_ref[0, rows, :] * (HG_DK ** -0.5)
        v = i_ref[0, rows, :]
        vb = v.astype(BF16)
        o_inter = lax.dot_general((q * jnp.exp(b)).astype(BF16), st_t.astype(BF16), NT,
                                  preferred_element_type=F32)
        parts = []
        for i in range(C // U):
            lo = i * U
            bi = b[lo:lo + U]
            qi = q[lo:lo + U]
            if i == 0:
                oi = jnp.zeros((U, HG_DV), F32)
            else:
                r = b[lo - 1:lo]
                qrel = (qi * jnp.exp(bi - r)).astype(BF16)
                kprev = (kk[:lo] * jnp.exp(r - b[:lo])).astype(BF16)
                a_off = lax.dot_general(qrel, kprev, NT, preferred_element_type=F32)
                oi = jnp.dot(a_off.astype(BF16), vb[:lo], preferred_element_type=F32)
            for s in range(U):
                valid = trow >= s
                e = jnp.exp(jnp.where(valid, bi - bi[s:s + 1], 0.0))
                a = jnp.sum(qi * e * kk[lo + s:lo + s + 1], axis=-1, keepdims=True)
                oi = oi + jnp.where(valid, a, 0.0) * v[lo + s:lo + s + 1]
            parts.append(oi)
        o = o_inter + jnp.concatenate(parts, axis=0)
        o = o * lax.rsqrt(jnp.mean(o * o, axis=-1, keepdims=True) + EPS) * ng
        o_ref[0, rows, :] = (o * _silu(gt_ref[0, rows, :])).astype(BF16)
        bl = b[C - 1:C]
        kv_t = lax.dot_general(vb, (kk * jnp.exp(bl - b)).astype(BF16), TN, preferred_element_type=F32)
        return jnp.exp(bl) * st_t + kv_t

    lax.fori_loop(0, S // C, chunk, jnp.zeros((HG_DV, HG_DK), F32))


def _cumsum_rows(x):
    n = x.shape[0]
    row = lax.broadcasted_iota(jnp.int32, x.shape, 0)
    d = 1
    while d < n:
        x = x + jnp.where(row >= d, pltpu.roll(x, d, 0), 0.0)
        d *= 2
    return x


def _hgrn_mxu_kernel(hg_ref, lb_ref, ng_ref, o_ref, st_scr):
    ts = hg_ref.shape[1]
    C, U = HG_CHUNK, HG_SUB

    @pl.when(pl.program_id(1) == 0)
    def _():
        st_scr[...] = jnp.zeros(st_scr.shape, F32)

    W = HG_WIDTH
    NH = HG_HEADS
    head_of_lane = lax.broadcasted_iota(jnp.int32, (1, W), 1) // HG_DK
    hcols = [slice(h * HG_DK, (h + 1) * HG_DK) for h in range(NH)]

    def chunk(c, carry):
        rows = pl.ds(pl.multiple_of(c * C, C), C)
        lb = lb_ref[...]
        f = lb + (1.0 - lb) * _sigmoid(hg_ref[0, rows, W:2 * W])
        kk = 1.0 - f
        b = _cumsum_rows(jnp.log(f))
        q = hg_ref[0, rows, 0:W] * (HG_DK ** -0.5)
        vb = hg_ref[0, rows, 2 * W:3 * W].astype(BF16)
        qe = (q * jnp.exp(b)).astype(BF16)
        o_inter = jnp.concatenate(
            [lax.dot_general(qe[:, hc], st_scr[h].astype(BF16), NT, preferred_element_type=F32)
             for h, hc in enumerate(hcols)], axis=1)
        parts = []
        for i in range(C // U):
            lo, hi = i * U, (i + 1) * U
            r = b[lo - 1:lo] if i else jnp.zeros((1, W), F32)
            qrel = q[lo:hi] * jnp.exp(b[lo:hi] - r)
            kall = (kk[:hi] * jnp.exp(r - b[:hi])).astype(BF16)
            qbd = jnp.concatenate([jnp.where(head_of_lane == h, qrel, 0.0) for h in range(NH)], axis=0)
            a = lax.dot_general(qbd.astype(BF16), kall, NT, preferred_element_type=F32)
            trow = lax.broadcasted_iota(jnp.int32, (NH * U, hi), 0) % U
            a = jnp.where(lax.broadcasted_iota(jnp.int32, (NH * U, hi), 1) <= lo + trow, a, 0.0)
            oa = jnp.dot(a.astype(BF16), vb[:hi], preferred_element_type=F32)
            oi = jnp.where(head_of_lane == 0, oa[0:U], 0.0)
            for h in range(1, NH):
                oi = jnp.where(head_of_lane == h, oa[h * U:(h + 1) * U], oi)
            parts.append(oi)
        o = o_inter + jnp.concatenate(parts, axis=0)
        o = jnp.concatenate(
            [o[:, hc] * lax.rsqrt(jnp.mean(o[:, hc] * o[:, hc], axis=-1, keepdims=True) + EPS) for hc in hcols], axis=1)
        o_ref[0, rows, :] = (o * ng_ref[...] * _silu(hg_ref[0, rows, 3 * W:4 * W])).astype(BF16)
        bl = b[C - 1:C]
        ke = (kk * jnp.exp(bl - b)).astype(BF16)
        decay = jnp.exp(bl)
        for h, hc in enumerate(hcols):
            kv_t = lax.dot_general(vb[:, hc], ke[:, hc], TN, preferred_element_type=F32)
            st_scr[h] = decay[:, hc] * st_scr[h] + kv_t
        return carry

    lax.fori_loop(0, ts // C, chunk, 0, unroll=8)


def _hgrn_mxu(hg, lb, ng, ts):
    B, S, _ = hg.shape
    vec = pl.BlockSpec((1, HG_WIDTH), lambda b, i: (0, 0))
    return pl.pallas_call(
        _hgrn_mxu_kernel,
        out_shape=jax.ShapeDtypeStruct((B, S, HG_WIDTH), BF16),
        grid=(B, S // ts),
        in_specs=[pl.BlockSpec((1, ts, 4 * HG_WIDTH), lambda b, i: (b, i, 0)), vec, vec],
        out_specs=pl.BlockSpec((1, ts, HG_WIDTH), lambda b, i: (b, i, 0)),
        scratch_shapes=[pltpu.VMEM((HG_HEADS, HG_DV, HG_DK), F32)],
        compiler_params=_cparams(("arbitrary", "arbitrary")),
        name="hgrn_mxu",
    )(hg, lb, ng)


def _hgrn(hg, lb, ng):
    B, S, _ = hg.shape
    col = lambda k: pl.BlockSpec((1, S, HG_DK), lambda b, h, k=k: (b, 0, k * HG_HEADS + h))
    vec = pl.BlockSpec((1, HG_DK), lambda b, h: (0, h))
    return pl.pallas_call(
        _hgrn_kernel,
        out_shape=jax.ShapeDtypeStruct((B, S, HG_WIDTH), BF16),
        grid=(B, HG_HEADS),
        in_specs=[col(0), col(1), col(2), col(3), vec, vec],
        out_specs=pl.BlockSpec((1, S, HG_DV), lambda b, h: (b, 0, h)),
        compiler_params=_cparams(("arbitrary", "arbitrary")),
        name="hgrn",
    )(hg, hg, hg, hg, lb, ng)


def _cmpmlp_kernel(x_ref, w1a_ref, w1b_ref, pe_ref, w1_ref, b1_ref, w2_ref, o_ref):
    x = x_ref[0]
    hb = jnp.dot(pe_ref[...], w1_ref[...], precision=HIGHEST, preferred_element_type=F32)[0:1] + b1_ref[...]
    nrow = x.shape[0]
    for g in range(NSA_KV_GROUPS):
        a = jnp.dot(x, w1a_ref[g], preferred_element_type=F32)
        bm = jnp.dot(x, w1b_ref[g], preferred_element_type=F32)
        hdn = a + pltpu.roll(bm, nrow - 1, 0) + hb
        o_ref[0, g] = jnp.dot(_gelu_tanh(hdn).astype(BF16), w2_ref[g], preferred_element_type=F32).astype(BF16)


def _cmpmlp(x2, w1a, w1b, pe8, w1, b1, w2p):
    B, ncb, width = x2.shape
    full = lambda a: pl.BlockSpec(a.shape, lambda b: (0,) * a.ndim)
    return pl.pallas_call(
        _cmpmlp_kernel,
        out_shape=jax.ShapeDtypeStruct((B, NSA_KV_GROUPS, ncb, LANES), BF16),
        grid=(B,),
        in_specs=[pl.BlockSpec((1, ncb, width), lambda b: (b, 0, 0)),
                  full(w1a), full(w1b), full(pe8), full(w1), full(b1), full(w2p)],
        out_specs=pl.BlockSpec((1, NSA_KV_GROUPS, ncb, LANES), lambda b: (b, 0, 0, 0)),
        compiler_params=_cparams(("arbitrary",)),
        name="cmpmlp",
    )(x2, w1a, w1b, pe8, w1, b1, w2p)


def _rank_before(score, nrows):
    jrow = lax.broadcasted_iota(jnp.int32, score.shape, 0)
    rank = jnp.zeros(score.shape, F32)
    for k in range(nrows):
        rk = score[k:k + 1]
        beats = (rk > score) | ((rk == score) & (jrow > k))
        rank = rank + jnp.where(beats, 1.0, 0.0)
    return rank


def _topk_rows(score, k):
    n = score.shape[0]
    row = lax.broadcasted_iota(jnp.int32, score.shape, 0).astype(F32)
    keep = jnp.zeros(score.shape, F32)
    for _ in range(k):
        top = jnp.max(score, axis=0, keepdims=True)
        first = jnp.min(jnp.where(score == top, row, float(n)), axis=0, keepdims=True)
        pick = row == first
        keep = jnp.where(pick, 1.0, keep)
        score = jnp.where(pick, -jnp.inf, score)
    return keep


def _transpose_small_ints(xt):
    c = xt.shape[1]
    eye = (lax.broadcasted_iota(jnp.int32, (c, c), 0) == lax.broadcasted_iota(jnp.int32, (c, c), 1)).astype(BF16)
    return lax.dot_general(eye, xt.astype(BF16), NT, preferred_element_type=F32)


def _pack_heads(heads):
    low = lax.broadcasted_iota(jnp.int32, heads[0].shape, 1) < NSA_HD
    return jnp.concatenate([jnp.where(low, heads[k], pltpu.roll(heads[k + 1], NSA_HD, 1))
                            for k in range(0, len(heads), 2)], axis=1)


def _cmpsel_kernel(q_ref, kc_ref, vc_ref, gate_ref, o_ref, sel_ref):
    tq = q_ref.shape[1]
    ncb = kc_ref.shape[2]
    t = pl.program_id(2)
    kc = kc_ref[0, 0]
    vc = vc_ref[0, 0]
    pos = t * tq + lax.broadcasted_iota(jnp.int32, (tq, 1), 0)
    cblk = lax.broadcasted_iota(jnp.int32, (1, ncb), 1)
    vis = (cblk * CMP_STRIDE + CMP_BLOCK - 1) <= pos
    psum = jnp.zeros((tq, ncb), F32)
    gate = gate_ref[0]
    heads = []
    for h in range(NSA_HG):
        s = lax.dot_general(q_ref[0, :, h * LANES:(h + 1) * LANES], kc, NT, preferred_element_type=F32)
        s = jnp.where(vis, s, NEG)
        p = jnp.exp(s - jnp.max(s, axis=-1, keepdims=True))
        p = jnp.where(vis, p / jnp.sum(p, axis=-1, keepdims=True), 0.0)
        heads.append(jnp.dot(p.astype(BF16), vc, preferred_element_type=F32) * gate[:, 3 * h:3 * h + 1])
        psum = psum + p
    o_ref[0] = _pack_heads(heads)
    nsb = ncb * CMP_STRIDE // SEL_BLOCK
    jb = lax.broadcasted_iota(jnp.int32, (nsb, ncb), 0) * SEL_BLOCK
    cb = lax.broadcasted_iota(jnp.int32, (nsb, ncb), 1) * CMP_STRIDE
    ov = jnp.maximum(jnp.minimum(cb + CMP_BLOCK, jb + SEL_BLOCK) - jnp.maximum(cb, jb), 0).astype(F32) / CMP_BLOCK
    ps_hi = psum.astype(BF16)
    ps_lo = (psum - ps_hi.astype(F32)).astype(BF16)
    ovb = ov.astype(BF16)
    pslc_t = (lax.dot_general(ovb, ps_hi, NT, preferred_element_type=F32)
              + lax.dot_general(ovb, ps_lo, NT, preferred_element_type=F32))
    posl = t * tq + lax.broadcasted_iota(jnp.int32, (1, tq), 1)
    cur = posl // SEL_BLOCK
    jrow = lax.broadcasted_iota(jnp.int32, (nsb, tq), 0)
    forced = (jrow == 0) | (jrow == cur) | (jrow == cur - 1)
    score = jnp.where(forced, 1e30, jnp.where(jrow <= cur, pslc_t, NEG))
    chosen = _topk_rows(score, min(N_SEL, nsb)) > 0.5
    drop = jnp.where(chosen & (score > -1e29), 0.0, 1.0)
    drop = jnp.concatenate([drop, jnp.zeros((LANES - nsb, tq), F32)], axis=0)
    sel_ref[0, 0] = (_transpose_small_ints(drop) * NEG).astype(BF16)


def _cmpsel(qraw, kcmp, vcmp, gate, tq):
    B, S, _ = qraw.shape
    ncb = kcmp.shape[2]
    gw = NSA_HG * LANES
    assert S // SEL_BLOCK <= NSA_HD
    cmp_spec = pl.BlockSpec((1, 1, ncb, LANES), lambda b, g, t: (b, g, 0, 0))
    return pl.pallas_call(
        _cmpsel_kernel,
        out_shape=(jax.ShapeDtypeStruct((B, S, NSA_WIDTH), F32),
                   jax.ShapeDtypeStruct((B, NSA_KV_GROUPS, S, LANES), BF16)),
        grid=(B, NSA_KV_GROUPS, S // tq),
        in_specs=[pl.BlockSpec((1, tq, gw), lambda b, g, t: (b, t, g)), cmp_spec, cmp_spec,
                  pl.BlockSpec((1, tq, LANES), lambda b, g, t: (b, t, g))],
        out_specs=(pl.BlockSpec((1, tq, NSA_WIDTH // NSA_KV_GROUPS), lambda b, g, t: (b, t, g)),
                   pl.BlockSpec((1, 1, tq, LANES), lambda b, g, t: (b, g, t, 0))),
        compiler_params=_cparams(("arbitrary", "arbitrary", "arbitrary")),
        name="cmpsel",
    )(qraw, kcmp, vcmp, gate)


def _slcwin_kernel(q_ref, sel_ref, ks_ref, kw_ref, vs_ref, vw_ref, gate_ref, ocmp_ref, wgu_ref, wdn_ref,
                   o_ref, wgu_bf_ref, wdn_bf_ref, *, tk, wk):
    wgu_bf_ref[...] = wgu_ref[...].astype(BF16)
    wdn_bf_ref[...] = wdn_ref[...].astype(BF16)
    tq = q_ref.shape[1]
    S = ks_ref.shape[1]
    t = pl.program_id(2)
    q0 = t * tq
    qpos = q0 + lax.broadcasted_iota(jnp.int32, (tq, 1), 0)
    qw = [q_ref[0, :, h * LANES:(h + 1) * LANES] for h in range(NSA_HG)]
    sel = sel_ref[0, 0]
    qa = [jnp.concatenate([q, sel], axis=1) for q in qw]

    gate = gate_ref[0]

    def normalise(acc, g):
        return acc * (g / acc[:, NSA_HD:NSA_HD + 1])

    nchain = NSA_HG // SLC_STACK
    qst = [jnp.concatenate(qa[c * SLC_STACK:(c + 1) * SLC_STACK], axis=0) for c in range(nchain)]
    qpos_st = jnp.concatenate([qpos] * SLC_STACK, axis=0)

    def kstep(j, carry, diag):
        k0 = pl.multiple_of(j * tk, tk)
        ks = ks_ref[0, pl.ds(k0, tk), :]
        vs = vs_ref[0, pl.ds(k0, tk), :]
        if diag:
            vis = (k0 + lax.broadcasted_iota(jnp.int32, (1, tk), 1)) <= qpos_st
        out = []
        for c in range(nchain):
            m, acc = carry[c]
            s = lax.dot_general(qst[c], ks, NT, preferred_element_type=F32)
            if diag:
                s = jnp.where(vis, s, NEG)
            mn = jnp.maximum(m, jnp.max(s, axis=-1, keepdims=True))
            p = jnp.exp2(s - mn)
            acc = jnp.exp2(m - mn) * acc + jnp.dot(p.astype(BF16), vs, preferred_element_type=F32)
            out.append((mn, acc))
        return tuple(out)

    init = tuple((jnp.full((SLC_STACK * tq, 1), NEG, F32), jnp.zeros((SLC_STACK * tq, LANES), F32))
                 for _ in range(nchain))
    jdiag = q0 // tk
    carry = lax.fori_loop(0, jdiag, lambda j, c: kstep(j, c, False), init)
    carry = kstep(jdiag, carry, True)
    heads = [normalise(carry[h // SLC_STACK][1][(h % SLC_STACK) * tq:(h % SLC_STACK + 1) * tq],
                       gate[:, 3 * h + 1:3 * h + 2]) for h in range(NSA_HG)]

    wq = wk - WINDOW
    wins = [[] for _ in range(NSA_HG)]
    spos = lax.broadcasted_iota(jnp.int32, (NSA_HG * wq, 1), 0) % wq
    for i in range(tq // wq):
        rows = slice(i * wq, (i + 1) * wq)
        start = pl.multiple_of(jnp.maximum(q0 + (i + 1) * wq - wk, 0), wq)
        kw = kw_ref[0, pl.ds(start, wk), :]
        vw = vw_ref[0, pl.ds(start, wk), :]
        d = (q0 + i * wq + spos) - (start + lax.broadcasted_iota(jnp.int32, (1, wk), 1))
        q_st = jnp.concatenate([q[rows] for q in qw], axis=0)
        s = jnp.where((d >= 0) & (d < WINDOW), lax.dot_general(q_st, kw, NT, preferred_element_type=F32), NEG)
        p = jnp.exp2(s - jnp.max(s, axis=-1, keepdims=True))
        o_st = jnp.dot(p.astype(BF16), vw, preferred_element_type=F32)
        for h in range(NSA_HG):
            wins[h].append(o_st[h * wq:(h + 1) * wq])
    for h in range(NSA_HG):
        heads[h] = heads[h] + normalise(jnp.concatenate(wins[h], axis=0), gate[:, 3 * h + 2:3 * h + 3])
    o_ref[0] = (ocmp_ref[0] + _pack_heads(heads)).astype(BF16)


def _slcwin(qrot, sel, ks, kw, vaug, gate, ocmp, wgu, wdn, tq, tk):
    B, S, _ = qrot.shape
    gw = NSA_HG * LANES
    wk = WINDOW + WIN_Q
    nt = S // tq
    nstep = B * NSA_KV_GROUPS * nt
    assert S >= wk and tq % WIN_Q == 0 and N_EXPERTS % nstep == 0
    epb = N_EXPERTS // nstep
    ospec = pl.BlockSpec((1, tq, NSA_WIDTH // NSA_KV_GROUPS), lambda b, g, t: (b, t, g))
    wspec = lambda a: pl.BlockSpec((epb,) + a.shape[1:], lambda b, g, t: ((b * NSA_KV_GROUPS + g) * nt + t, 0, 0))
    return pl.pallas_call(
        functools.partial(_slcwin_kernel, tk=tk, wk=wk),
        out_shape=(jax.ShapeDtypeStruct((B, S, NSA_WIDTH), BF16),
                   jax.ShapeDtypeStruct(wgu.shape, BF16), jax.ShapeDtypeStruct(wdn.shape, BF16)),
        grid=(B, NSA_KV_GROUPS, nt),
        in_specs=[pl.BlockSpec((1, tq, gw), lambda b, g, t: (b, t, g)),
                  pl.BlockSpec((1, 1, tq, LANES), lambda b, g, t: (b, g, t, 0)),
                  pl.BlockSpec((1, S, 2 * LANES), lambda b, g, t: (b, 0, 0)),
                  pl.BlockSpec((1, S, LANES), lambda b, g, t: (b, 0, 0)),
                  pl.BlockSpec((1, S, LANES), lambda b, g, t: (b, 0, g)),
                  pl.BlockSpec((1, S, LANES), lambda b, g, t: (b, 0, NSA_KV_GROUPS + g)),
                  pl.BlockSpec((1, tq, LANES), lambda b, g, t: (b, t, g)),
                  ospec, wspec(wgu), wspec(wdn)],
        out_specs=(ospec, wspec(wgu), wspec(wdn)),
        compiler_params=_cparams(("arbitrary", "arbitrary", "arbitrary")),
        name="slcwin",
    )(qrot, sel, ks, kw, vaug, vaug, gate, ocmp, wgu, wdn)


def _outproj_kernel(x_ref, ohg_ref, onsa_ref, wout_ref,
                    g1_ref, sh2_ref, sc2_ref, n2_ref, rwt_ref, rb_ref,
                    x1_ref, h2_ref, sc_ref, slot_t_ref, w_t_ref, cnt_ref):
    tm = x_ref.shape[1]
    mix = jnp.dot(jnp.concatenate([ohg_ref[0], onsa_ref[0]], axis=1), wout_ref[...], preferred_element_type=F32)
    x1 = x_ref[0] + g1_ref[0] * mix
    x1_ref[0] = x1
    h2 = (x1 * lax.rsqrt(jnp.mean(x1 * x1, axis=-1, keepdims=True) + EPS) * n2_ref[...]) * (1.0 + sc2_ref[0]) + sh2_ref[0]
    h_hi = h2.astype(BF16)
    h2_ref[0] = h_hi
    h_lo = (h2 - h_hi.astype(F32)).astype(BF16)
    both = lax.dot_general(rwt_ref[...], h_hi, NT, preferred_element_type=F32)
    logits = (both[:N_EXPERTS] + both[N_EXPERTS:]
              + lax.dot_general(rwt_ref[:N_EXPERTS, :], h_lo, NT, preferred_element_type=F32))
    scores = _sigmoid(logits)
    choice = scores + rb_ref[...]
    per = N_EXPERTS // N_GROUPS
    c3 = choice.reshape(N_GROUPS, per, tm)
    erow = lax.broadcasted_iota(jnp.int32, c3.shape, 1)
    rank_in = jnp.zeros(c3.shape, F32)
    for k in range(per):
        ck = c3[:, k:k + 1, :]
        rank_in = rank_in + jnp.where((ck > c3) | ((ck == c3) & (erow > k)), 1.0, 0.0)
    grp_score = jnp.sum(jnp.where(rank_in < 2, c3, 0.0), axis=1)
    grp_keep = _rank_before(grp_score, N_GROUPS) < TOPK_GROUPS
    masked = jnp.where(grp_keep[:, None, :], c3, -jnp.inf).reshape(N_EXPERTS, tm)
    keep = _topk_rows(masked, TOP_K) > 0.5
    tw = jnp.where(keep, scores, 0.0)
    tw = tw / jnp.sum(tw, axis=0, keepdims=True) * ROUTED_SCALE
    before = lax.broadcasted_iota(jnp.int32, (tm, tm), 0)
    after = lax.broadcasted_iota(jnp.int32, (tm, tm), 1)
    earlier = (before < after) & (before // MOE_SUB == after // MOE_SUB)
    kept = jnp.where(keep, 1.0, 0.0)
    pref = jnp.dot(kept.astype(BF16), earlier.astype(BF16), preferred_element_type=F32)
    slot = jnp.where(keep, pref, -1.0)
    assert MOE_SUB <= 256
    sc_ref[0] = _transpose_small_ints(slot)
    for j in range(tm // MOE_SUB):
        sub = slice(j * MOE_SUB, (j + 1) * MOE_SUB)
        slot_t_ref[:, j] = slot[:, sub].reshape(N_EXPERTS // MOE_EG, MOE_EG, MOE_SUB)
        w_t_ref[:, j] = tw[:, sub].reshape(N_EXPERTS // MOE_EG, MOE_EG, MOE_SUB)
        cnt_ref[j] = jnp.broadcast_to(jnp.sum(kept[:, sub], axis=1, keepdims=True), (N_EXPERTS, LANES))


def _outproj(x, ohg, onsa, wout, mod3, n2, rwt, rb, tm):
    B, S, D = x.shape
    blk = lambda w: pl.BlockSpec((1, tm, w), lambda b, i: (b, i, 0))
    full = lambda a: pl.BlockSpec(a.shape, lambda b, i: (0,) * a.ndim)
    modc = lambda k: pl.BlockSpec((1, 1, D), lambda b, i, k=k: (b, 0, k))
    nt = S // tm
    ng = N_EXPERTS // MOE_EG
    nh = tm // MOE_SUB
    tspec = pl.BlockSpec((ng, nh, MOE_EG, MOE_SUB), lambda b, i: (0, b * nt + i, 0, 0))
    return pl.pallas_call(
        _outproj_kernel,
        out_shape=(jax.ShapeDtypeStruct((B, S, D), F32),
                   jax.ShapeDtypeStruct((B, S, D), BF16),
                   jax.ShapeDtypeStruct((B, S, N_EXPERTS), F32),
                   jax.ShapeDtypeStruct((ng, B * nt * nh, MOE_EG, MOE_SUB), F32),
                   jax.ShapeDtypeStruct((ng, B * nt * nh, MOE_EG, MOE_SUB), F32),
                   jax.ShapeDtypeStruct((B * nt * nh, N_EXPERTS, LANES), F32)),
        grid=(B, nt),
        in_specs=[blk(D), blk(HG_WIDTH), blk(NSA_WIDTH), full(wout),
                  modc(2), modc(3), modc(4), full(n2), full(rwt), full(rb)],
        out_specs=(blk(D), blk(D), blk(N_EXPERTS), tspec, tspec,
                   pl.BlockSpec((nh, N_EXPERTS, LANES), lambda b, i: (b * nt + i, 0, 0))),
        compiler_params=_cparams(("arbitrary", "arbitrary")),
        name="outproj",
    )(x, ohg, onsa, wout, mod3, mod3, mod3, n2, rwt, rb)


def _swiglu_hidden(x, wgu):
    gu = jnp.dot(x, wgu, preferred_element_type=F32)
    return _silu(gu[:, :EXPERT_HIDDEN]) * gu[:, EXPERT_HIDDEN:]


def _moe_kernel(rounds_ref, h_ref, slot_t_ref, w_t_ref, slot_c_ref, wgu_ref, wdn_ref, sgu_ref, sdn_ref,
                o_ref, x_scr, p_scr):
    i = pl.program_id(0)
    g = pl.program_id(1)
    tm = h_ref.shape[0]
    nsub = tm // MOE_SUB
    rnd, eg = MOE_ROUND, MOE_EG
    ng = N_EXPERTS // eg
    base = i * (N_EXPERTS + ng)

    @pl.when(g == 0)
    def _():
        act = _swiglu_hidden(h_ref[...], sgu_ref[...]).astype(BF16)
        o_ref[...] = jnp.dot(act, sdn_ref[...], preferred_element_type=F32)

    blk = eg * rnd
    rslot = lax.broadcasted_iota(jnp.int32, (rnd, 1), 0).astype(F32)

    def gather(first, nround):
        for s in range(nsub):
            st = slot_t_ref[0, s] - first
            p = jnp.concatenate([jnp.where(rslot + float(w * rnd) == st[e:e + 1, :], 1.0, 0.0)
                                 for w in range(nround) for e in range(eg)], axis=0)
            p_scr[s, 0:nround * blk, :] = p.astype(BF16)
            x_scr[s, 0:nround * blk, :] = jnp.dot(p_scr[s, 0:nround * blk, :], h_ref[s * MOE_SUB:(s + 1) * MOE_SUB, :],
                                                  preferred_element_type=F32).astype(BF16)

    def run_expert(e, w):
        start = w * blk + e * rnd
        rows = pl.ds(start if isinstance(e, int) else pl.multiple_of(start, rnd), rnd)
        xe = jnp.concatenate([x_scr[s, rows, :] for s in range(nsub)], axis=0)
        wslot = jnp.concatenate(
            [jnp.sum(p_scr[s, rows, :].astype(F32) * w_t_ref[0, s, pl.ds(e, 1), :], axis=-1, keepdims=True)
             for s in range(nsub)], axis=0)
        y = jnp.dot((_swiglu_hidden(xe, wgu_ref[e]) * wslot).astype(BF16), wdn_ref[e], preferred_element_type=F32)
        for s in range(nsub):
            x_scr[s, rows, :] = y[s * rnd:(s + 1) * rnd].astype(BF16)

    def run_active(r, w):
        def body(e, carry):
            pl.when(rounds_ref[base + g * eg + e] > r)(functools.partial(run_expert, e, w))
            return carry
        lax.fori_loop(0, eg, body, 0)

    def scatter(first, nround):
        lane = lax.broadcasted_iota(jnp.int32, (1, nround * blk), 1)
        lane_slot = ((lane // blk) * rnd + lane % rnd).astype(F32)
        spread = ((lax.broadcasted_iota(jnp.int32, (N_EXPERTS, nround * blk), 1) % blk) // rnd + g * eg
                  == lax.broadcasted_iota(jnp.int32, (N_EXPERTS, nround * blk), 0)).astype(BF16)
        for s in range(nsub):
            toks = slice(s * MOE_SUB, (s + 1) * MOE_SUB)
            sc = jnp.dot(slot_c_ref[toks, :].astype(BF16), spread, preferred_element_type=F32) - first
            pt = jnp.where(sc == lane_slot, 1.0, 0.0).astype(BF16)
            o_ref[toks, :] += jnp.dot(pt, x_scr[s, 0:nround * blk, :], preferred_element_type=F32)

    def first_pass(nround):
        gather(0.0, nround)
        for e in range(eg):
            run_expert(e, 0)
        for w in range(1, nround):
            run_active(w, w)
        scatter(0.0, nround)

    group_rounds = rounds_ref[base + N_EXPERTS + g]
    pl.when(group_rounds < MOE_FIRST)(functools.partial(first_pass, MOE_FIRST - 1))
    pl.when(group_rounds >= MOE_FIRST)(functools.partial(first_pass, MOE_FIRST))

    def later_round(r, carry):
        first = (r * rnd).astype(F32)
        gather(first, 1)
        run_active(r, 0)
        scatter(first, 1)
        return carry

    lax.fori_loop(MOE_FIRST, rounds_ref[base + N_EXPERTS + g], later_round, 0)


def _moe(rounds, h2, slot_t, w_t, slot_c, wgu, wdn, sgu, sdn, tm):
    T, D = h2.shape
    eg, nsub = MOE_EG, tm // MOE_SUB
    full = lambda a: pl.BlockSpec(a.shape, lambda i, e, o: (0,) * a.ndim)
    tspec = pl.BlockSpec((1, nsub, eg, MOE_SUB), lambda i, e, o: (e, i, 0, 0))
    return pl.pallas_call(
        _moe_kernel,
        out_shape=jax.ShapeDtypeStruct((T, D), F32),
        grid_spec=pltpu.PrefetchScalarGridSpec(
            num_scalar_prefetch=1,
            grid=(T // tm, N_EXPERTS // eg),
            in_specs=[pl.BlockSpec((tm, D), lambda i, e, o: (i, 0), pipeline_mode=pl.Buffered(1)), tspec, tspec,
                      pl.BlockSpec((tm, N_EXPERTS), lambda i, e, o: (i, 0)),
                      pl.BlockSpec((eg, D, 2 * EXPERT_HIDDEN), lambda i, e, o: (e, 0, 0)),
                      pl.BlockSpec((eg, EXPERT_HIDDEN, D), lambda i, e, o: (e, 0, 0)),
                      full(sgu), full(sdn)],
            out_specs=pl.BlockSpec((tm, D), lambda i, e, o: (i, 0), pipeline_mode=pl.Buffered(1)),
            scratch_shapes=[pltpu.VMEM((nsub, MOE_FIRST * eg * MOE_ROUND, D), BF16),
                            pltpu.VMEM((nsub, MOE_FIRST * eg * MOE_ROUND, MOE_SUB), BF16)]),
        compiler_params=_cparams(("arbitrary", "arbitrary")),
        name="moe",
    )(rounds, h2, slot_t, w_t, slot_c, wgu, wdn, sgu, sdn)


def _final_kernel(x1_ref, moe_ref, g2_ref, fg_ref, o_ref):
    x2 = x1_ref[0] + g2_ref[0] * moe_ref[0]
    o_ref[0] = x2 * lax.rsqrt(jnp.mean(x2 * x2, axis=-1, keepdims=True) + EPS) * fg_ref[...]


def _final(x1, moe, mod3, fg, tm):
    B, S, D = x1.shape
    blk = pl.BlockSpec((1, tm, D), lambda b, i: (b, i, 0))
    return pl.pallas_call(
        _final_kernel,
        out_shape=jax.ShapeDtypeStruct((B, S, D), F32),
        grid=(B, S // tm),
        in_specs=[blk, blk, pl.BlockSpec((1, 1, D), lambda b, i: (b, 0, 5)), pl.BlockSpec((1, D), lambda b, i: (0, 0))],
        out_specs=blk,
        compiler_params=_cparams(("arbitrary", "arbitrary")),
        name="final",
    )(x1, moe, mod3, fg)


def _split_bf16(w):
    hi = w.astype(BF16)
    return jnp.concatenate([hi, (w - hi.astype(F32)).astype(BF16)], axis=0)


def _pack_w_in(w_in):
    return jnp.pad(w_in, ((0, 0), (0, IN_COLS_P - w_in.shape[1]))).astype(BF16)


def _pack_cmp(pos, w1, b1, w2, lane_by_group):
    half = CMP_STRIDE * NSA_HD
    def rows_for(wh):
        w3 = wh.reshape(CMP_STRIDE, NSA_HD, CMP_HIDDEN)
        z = jnp.zeros_like(w3)
        return jnp.stack([jnp.concatenate([w3, z], axis=1).reshape(CMP_STRIDE * LANES, CMP_HIDDEN),
                          jnp.concatenate([z, w3], axis=1).reshape(CMP_STRIDE * LANES, CMP_HIDDEN)])
    w1a = rows_for(w1[:half]).astype(BF16)
    w1b = rows_for(w1[half:]).astype(BF16)
    z2 = jnp.zeros_like(w2)
    w2_first = jnp.concatenate([w2, z2], axis=1)
    w2p = jnp.stack([w2_first, jnp.concatenate([z2, w2], axis=1) if lane_by_group else w2_first]).astype(BF16)
    pe8 = jnp.pad(pos.reshape(1, CMP_BLOCK * NSA_HD), ((0, 7), (0, 0)))
    return w1a, w1b, pe8, w1, b1.reshape(1, CMP_HIDDEN), w2p


def _rope_tables(S):
    half = NSA_HD // 2
    inv = ROPE_THETA ** (-jnp.arange(half, dtype=F32) / half)
    ang = jnp.arange(S, dtype=F32)[:, None] * inv[None, :]
    cos, sin = jnp.cos(ang), jnp.sin(ang)
    reps = LANES // NSA_HD
    return jnp.tile(jnp.concatenate([cos, cos], axis=1), (1, reps)), jnp.tile(jnp.concatenate([-sin, sin], axis=1), (1, reps))


def _tiles(S):
    return dict(inproj=min(512, S), hgrn=min(512, S), cmpsel=min(512, S), slc_q=512, slc_k=512, moe=min(2048, S), final=min(512, S))


def kernel(x, c, w_ada, b_ada, norm1_g, w_in, hg_lb_logits, hg_norm_g, cmp_pos_k, cmp_w1_k, cmp_b1_k, cmp_w2_k,
           cmp_pos_v, cmp_w1_v, cmp_b1_v, cmp_w2_v, w_out, norm2_g, router_w, router_bias, w_exp_gu, w_exp_dn,
           w_sh_gu, w_sh_dn, final_g):
    B, S, D = x.shape
    assert D == D_MODEL and w_ada.shape[0] == 1 and S % 512 == 0
    tl = _tiles(S)
    l = 0
    lb = jnp.cumsum(jax.nn.softmax(hg_lb_logits.astype(F32), axis=0), axis=0)[l].reshape(1, HG_WIDTH)
    c8 = jnp.pad(c, ((0, 8 - B), (0, 0)))
    mod3 = _ada(c8, w_ada[l], b_ada[l].reshape(1, -1))[:B].reshape(B, 1, 6 * D)
    cos, sin = _rope_tables(S)
    hg, qraw, qrot, kc, vc, ks, kw, vaug, gate = _inproj(x, mod3, norm1_g[l].reshape(1, D), _pack_w_in(w_in[l]),
                                                         cos, sin, tl["inproj"])
    ng = hg_norm_g[l].reshape(1, HG_WIDTH)
    mxu_safe = -HG_SUB * jnp.log(jnp.min(lb)) <= HG_SAFE_LOG_RANGE
    ohg = lax.cond(mxu_safe, lambda: _hgrn_mxu(hg, lb, ng, tl["hgrn"]), lambda: _hgrn(hg, lb, ng))
    ncb = S // CMP_STRIDE
    kcmp = _cmpmlp(kc.reshape(B, ncb, CMP_STRIDE * LANES),
                   *_pack_cmp(cmp_pos_k[l], cmp_w1_k[l], cmp_b1_k[l], cmp_w2_k[l], True))
    vcmp = _cmpmlp(vc.reshape(B, ncb, CMP_STRIDE * LANES),
                   *_pack_cmp(cmp_pos_v[l], cmp_w1_v[l], cmp_b1_v[l], cmp_w2_v[l], False))
    ocmp, sel = _cmpsel(qraw, kcmp, vcmp, gate, tl["cmpsel"])
    onsa, wgu_bf, wdn_bf = _slcwin(qrot, sel, ks, kw, vaug, gate, ocmp, w_exp_gu[l], w_exp_dn[l],
                                   tl["slc_q"], tl["slc_k"])
    x1, h2, sc, slot_t, w_t, cnt = _outproj(x, ohg, onsa, w_out[l].astype(BF16), mod3,
                                            norm2_g[l].reshape(1, D), _split_bf16(router_w[l].T),
                                            router_bias[l].reshape(N_EXPERTS, 1), ROUTE_TM)
    T = B * S
    tm = tl["moe"]
    ng = N_EXPERTS // MOE_EG
    slot_c = sc.reshape(T, N_EXPERTS)
    load = jnp.max(cnt[:, :, 0].reshape(T // tm, tm // MOE_SUB, N_EXPERTS), axis=1).astype(jnp.int32)
    per_expert = (load + MOE_ROUND - 1) // MOE_ROUND
    rounds = jnp.concatenate([per_expert, jnp.max(per_expert.reshape(T // tm, ng, MOE_EG), axis=2)], axis=1).reshape(-1)
    moe = _moe(rounds, h2.reshape(T, D), slot_t, w_t, slot_c,
               wgu_bf, wdn_bf, w_sh_gu[l].astype(BF16), w_sh_dn[l].astype(BF16), tm)
    return _final(x1, moe.reshape(B, S, D), mod3, final_g.reshape(1, D), tl["final"])
```

```python
import functools

import jax
import jax.numpy as jnp
from jax import lax
from jax.experimental import pallas as pl
from jax.experimental.pallas import tpu as pltpu

F32 = jnp.float32
BF16 = jnp.bfloat16
HIGHEST = lax.Precision.HIGHEST

D_MODEL = 1024
EPS = 1e-6
HG_HEADS = 4
HG_DK = 128
HG_DV = 128
HG_WIDTH = HG_HEADS * HG_DV
HG_CHUNK = 64
HG_SUB = 16
HG_SAFE_LOG_RANGE = 80.0
NSA_HEADS = 8
NSA_KV_GROUPS = 2
NSA_HG = NSA_HEADS // NSA_KV_GROUPS
NSA_HD = 64
NSA_WIDTH = NSA_HEADS * NSA_HD
NSA_KV = NSA_KV_GROUPS * NSA_HD
CMP_BLOCK = 32
CMP_STRIDE = 16
CMP_HIDDEN = 256
SEL_BLOCK = 64
N_SEL = 16
WINDOW = 512
ROPE_THETA = 10000.0
SLC_STACK = 4
WIN_Q = 256
N_EXPERTS = 64
TOP_K = 8
N_GROUPS = 8
TOPK_GROUPS = 4
EXPERT_HIDDEN = 256
ROUTED_SCALE = 2.5
ROUTE_TM = 512
MOE_SUB = 256
MOE_ROUND = 32
MOE_EG = 8
MOE_FIRST = 2

LANES = 128
NEG = -1e30
VMEM_LIMIT = 56 * 1024 * 1024

C_HG = 0
C_Q = 4 * HG_WIDTH
C_K = C_Q + NSA_WIDTH
C_GATE = C_K + 6 * NSA_KV
IN_COLS_P = C_GATE + LANES
LOG2E = 1.4426950408889634

NT = (((1,), (1,)), ((), ()))
TN = (((0,), (0,)), ((), ()))


def _cparams(sem):
    return pltpu.CompilerParams(dimension_semantics=sem, vmem_limit_bytes=VMEM_LIMIT)


def _sigmoid(x):
    return 1.0 / (1.0 + jnp.exp(-x))


def _silu(x):
    return x * _sigmoid(x)


def _gelu_tanh(x):
    return 0.5 * x * (1.0 + jnp.tanh(0.7978845608028654 * (x + 0.044715 * (x * x * x))))


def _dot3(a, w):
    n = a.shape[0]
    a_hi = a.astype(BF16)
    a_lo = (a - a_hi.astype(F32)).astype(BF16)
    w_hi = w.astype(BF16)
    w_lo = (w - w_hi.astype(F32)).astype(BF16)
    both = jnp.dot(jnp.concatenate([a_hi, a_lo], axis=0), w_hi, preferred_element_type=F32)
    return both[:n] + both[n:] + jnp.dot(a_hi, w_lo, preferred_element_type=F32)


def _ada_kernel(c_ref, w_ref, b_ref, o_ref):
    o_ref[...] = _dot3(_silu(c_ref[...]), w_ref[...]) + b_ref[...]


def _ada(c8, w, b):
    n = w.shape[1]
    tn = 1024
    return pl.pallas_call(
        _ada_kernel,
        out_shape=jax.ShapeDtypeStruct((8, n), F32),
        grid=(n // tn,),
        in_specs=[pl.BlockSpec((8, D_MODEL), lambda j: (0, 0)),
                  pl.BlockSpec((D_MODEL, tn), lambda j: (0, j)),
                  pl.BlockSpec((1, tn), lambda j: (0, j))],
        out_specs=pl.BlockSpec((8, tn), lambda j: (0, j)),
        compiler_params=_cparams(("arbitrary",)),
        name="ada",
    )(c8, w, b)


def _rope(t, cos, sin_signed, first_half):
    rot = jnp.where(first_half, pltpu.roll(t, 96, 1), pltpu.roll(t, 32, 1))
    return t * cos + rot * sin_signed


def _inproj_kernel(x_ref, sh_ref, sc_ref, g_ref, w_ref, cos_ref, sin_ref,
                   hg_ref, qraw_ref, qrot_ref, kc_ref, vc_ref, ks_ref, kw_ref, va_ref, gate_ref, h_scr):
    tm = x_ref.shape[1]
    x = x_ref[0]
    y = x * lax.rsqrt(jnp.mean(x * x, axis=-1, keepdims=True) + EPS) * g_ref[...]
    h_scr[...] = (y * (1.0 + sc_ref[0]) + sh_ref[0]).astype(BF16)

    def mm(lo, width):
        return jnp.dot(h_scr[...], w_ref[:, lo:lo + width], preferred_element_type=F32)

    cos = cos_ref[...]
    sin = sin_ref[...]
    first_half = (lax.broadcasted_iota(jnp.int32, cos.shape, 1) % NSA_HD) < (NSA_HD // 2)
    lane = lax.broadcasted_iota(jnp.int32, (tm, LANES), 1)
    half_of_lane = lane // NSA_HD

    def own_lanes(t, src_half, dst_half):
        moved = t if src_half == dst_half else pltpu.roll(t, NSA_HD, 1)
        return jnp.where(half_of_lane == dst_half, moved, 0.0)

    for j in range(4):
        hg_ref[0, :, j * HG_WIDTH:(j + 1) * HG_WIDTH] = mm(C_HG + j * HG_WIDTH, HG_WIDTH)
    for m in range(NSA_WIDTH // (2 * LANES)):
        qq = mm(C_Q + m * 2 * LANES, 2 * LANES) * (NSA_HD ** -0.5 * LOG2E)
        for c in range(2):
            q = qq[:, c * LANES:(c + 1) * LANES]
            qr = _rope(q, cos, sin, first_half)
            for half in range(2):
                n = (2 * m + c) * 2 + half
                g = n // NSA_HG
                qraw_ref[0, :, n * LANES:(n + 1) * LANES] = own_lanes(q, half, g).astype(BF16)
                qrot_ref[0, :, n * LANES:(n + 1) * LANES] = own_lanes(qr, half, g).astype(BF16)
    kv = mm(C_K, 2 * LANES)
    kc_ref[0] = kv[:, :LANES].astype(BF16)
    vc_ref[0] = kv[:, LANES:].astype(BF16)
    pos = pl.program_id(1) * tm + lax.broadcasted_iota(jnp.int32, (tm, LANES), 0)
    ks_ref[0, :, LANES:2 * LANES] = jnp.where(lane == pos // SEL_BLOCK, 1.0, 0.0).astype(BF16)
    for j in range(2):
        kv = mm(C_K + (j + 1) * 2 * LANES, 2 * LANES)
        k_rot = _rope(kv[:, :LANES], cos, sin, first_half).astype(BF16)
        if j == 0:
            ks_ref[0, :, 0:LANES] = k_rot
        else:
            kw_ref[0] = k_rot
        for g in range(NSA_KV_GROUPS):
            v = own_lanes(kv[:, LANES:], g, 0)
            col = (j * NSA_KV_GROUPS + g) * LANES
            va_ref[0, :, col:col + LANES] = jnp.where(lane == NSA_HD, 1.0, v).astype(BF16)
    gate = _sigmoid(mm(C_GATE, LANES))
    per_group = 3 * NSA_HG
    for g in range(NSA_KV_GROUPS):
        gate_ref[0, :, g * LANES:(g + 1) * LANES] = gate if g == 0 else pltpu.roll(gate, LANES - g * per_group, 1)


def _inproj(x, mod3, norm_g, w_p, cos, sin, tm):
    B, S, D = x.shape
    blk = lambda w: pl.BlockSpec((1, tm, w), lambda b, i: (b, i, 0))
    return pl.pallas_call(
        _inproj_kernel,
        out_shape=(jax.ShapeDtypeStruct((B, S, 4 * HG_WIDTH), F32),
                   jax.ShapeDtypeStruct((B, S, NSA_HEADS * LANES), BF16),
                   jax.ShapeDtypeStruct((B, S, NSA_HEADS * LANES), BF16),
                   jax.ShapeDtypeStruct((B, S, LANES), BF16),
                   jax.ShapeDtypeStruct((B, S, LANES), BF16),
                   jax.ShapeDtypeStruct((B, S, 2 * LANES), BF16),
                   jax.ShapeDtypeStruct((B, S, LANES), BF16),
                   jax.ShapeDtypeStruct((B, S, 4 * LANES), BF16),
                   jax.ShapeDtypeStruct((B, S, NSA_KV_GROUPS * LANES), F32)),
        grid=(B, S // tm),
        in_specs=[blk(D),
                  pl.BlockSpec((1, 1, D), lambda b, i: (b, 0, 0)),
                  pl.BlockSpec((1, 1, D), lambda b, i: (b, 0, 1)),
                  pl.BlockSpec((1, D), lambda b, i: (0, 0)),
                  pl.BlockSpec((D, IN_COLS_P), lambda b, i: (0, 0)),
                  pl.BlockSpec((tm, LANES), lambda b, i: (i, 0)),
                  pl.BlockSpec((tm, LANES), lambda b, i: (i, 0))],
        out_specs=(blk(4 * HG_WIDTH), blk(NSA_HEADS * LANES), blk(NSA_HEADS * LANES),
                   blk(LANES), blk(LANES), blk(2 * LANES), blk(LANES), blk(4 * LANES), blk(NSA_KV_GROUPS * LANES)),
        scratch_shapes=[pltpu.VMEM((tm, D), BF16)],
        compiler_params=_cparams(("arbitrary", "arbitrary")),
        name="inproj",
    )(x, mod3, mod3, norm_g, w_p, cos, sin)


def _hgrn_kernel(q_ref, f_ref, i_ref, gt_ref, lb_ref, ng_ref, o_ref):
    S = q_ref.shape[1]
    C, U = HG_CHUNK, HG_SUB
    lb = lb_ref[...]
    ng = ng_ref[...]
    ri = lax.broadcasted_iota(jnp.int32, (C, C), 0)
    ci = lax.broadcasted_iota(jnp.int32, (C, C), 1)
    tril = (ri >= ci).astype(F32)
    trow = lax.broadcasted_iota(jnp.int32, (U, 1), 0)

    def chunk(c, st_t):
        r0 = pl.multiple_of(c * C, C)
        rows = pl.ds(r0, C)
        f = lb + (1.0 - lb) * _sigmoid(f_ref[0, rows, :])
        kk = 1.0 - f
        b = jnp.dot(tril, jnp.log(f), precision=HIGHEST, preferred_element_type=F32)
        q = q_ref[0, rows, :] * (HG_DK ** -0.5)
        v = i_ref[0, rows, :]
        vb = v.astype(BF16)
        o_inter = lax.dot_general((q * jnp.exp(b)).astype(BF16), st_t.astype(BF16), NT,
                                  preferred_element_type=F32)
        parts = []
        for i in range(C // U):
            lo = i * U
            bi = b[lo:lo + U]
            qi = q[lo:lo + U]
            if i == 0:
                oi = jnp.zeros((U, HG_DV), F32)
            else:
                r = b[lo - 1:lo]
                qrel = (qi * jnp.exp(bi - r)).astype(BF16)
                kprev = (kk[:lo] * jnp.exp(r - b[:lo])).astype(BF16)
                a_off = lax.dot_general(qrel, kprev, NT, preferred_element_type=F32)
                oi = jnp.dot(a_off.astype(BF16), vb[:lo], preferred_element_type=F32)
            for s in range(U):
                valid = trow >= s
                e = jnp.exp(jnp.where(valid, bi - bi[s:s + 1], 0.0))
                a = jnp.sum(qi * e * kk[lo + s:lo + s + 1], axis=-1, keepdims=True)
                oi = oi + jnp.where(valid, a, 0.0) * v[lo + s:lo + s + 1]
            parts.append(oi)
        o = o_inter + jnp.concatenate(parts, axis=0)
        o = o * lax.rsqrt(jnp.mean(o * o, axis=-1, keepdims=True) + EPS) * ng
        o_ref[0, rows, :] = (o * _silu(gt_ref[0, rows, :])).astype(BF16)
        bl = b[C - 1:C]
        kv_t = lax.dot_general(vb, (kk * jnp.exp(bl - b)).astype(BF16), TN, preferred_element_type=F32)
        return jnp.exp(bl) * st_t + kv_t

    lax.fori_loop(0, S // C, chunk, jnp.zeros((HG_DV, HG_DK), F32))


def _cumsum_rows(x):
    n = x.shape[0]
    row = lax.broadcasted_iota(jnp.int32, x.shape, 0)
    d = 1
    while d < n:
        x = x + jnp.where(row >= d, pltpu.roll(x, d, 0), 0.0)
        d *= 2
    return x


def _hgrn_mxu_kernel(hg_ref, lb_ref, ng_ref, o_ref, st_scr):
    ts = hg_ref.shape[1]
    C, U = HG_CHUNK, HG_SUB

    @pl.when(pl.program_id(1) == 0)
    def _():
        st_scr[...] = jnp.zeros(st_scr.shape, F32)

    W = HG_WIDTH
    NH = HG_HEADS
    head_of_lane = lax.broadcasted_iota(jnp.int32, (1, W), 1) // HG_DK
    hcols = [slice(h * HG_DK, (h + 1) * HG_DK) for h in range(NH)]

    def chunk(c, carry):
        rows = pl.ds(pl.multiple_of(c * C, C), C)
        lb = lb_ref[...]
        f = lb + (1.0 - lb) * _sigmoid(hg_ref[0, rows, W:2 * W])
        kk = 1.0 - f
        b = _cumsum_rows(jnp.log(f))
        q = hg_ref[0, rows, 0:W] * (HG_DK ** -0.5)
        vb = hg_ref[0, rows, 2 * W:3 * W].astype(BF16)
        qe = (q * jnp.exp(b)).astype(BF16)
        o_inter = jnp.concatenate(
            [lax.dot_general(qe[:, hc], st_scr[h].astype(BF16), NT, preferred_element_type=F32)
             for h, hc in enumerate(hcols)], axis=1)
        parts = []
        for i in range(C // U):
            lo, hi = i * U, (i + 1) * U
            r = b[lo - 1:lo] if i else jnp.zeros((1, W), F32)
            qrel = q[lo:hi] * jnp.exp(b[lo:hi] - r)
            kall = (kk[:hi] * jnp.exp(r - b[:hi])).astype(BF16)
            qbd = jnp.concatenate([jnp.where(head_of_lane == h, qrel, 0.0) for h in range(NH)], axis=0)
            a = lax.dot_general(qbd.astype(BF16), kall, NT, preferred_element_type=F32)
            trow = lax.broadcasted_iota(jnp.int32, (NH * U, hi), 0) % U
            a = jnp.where(lax.broadcasted_iota(jnp.int32, (NH * U, hi), 1) <= lo + trow, a, 0.0)
            oa = jnp.dot(a.astype(BF16), vb[:hi], preferred_element_type=F32)
            oi = jnp.where(head_of_lane == 0, oa[0:U], 0.0)
            for h in range(1, NH):
                oi = jnp.where(head_of_lane == h, oa[h * U:(h + 1) * U], oi)
            parts.append(oi)
        o = o_inter + jnp.concatenate(parts, axis=0)
        o = jnp.concatenate(
            [o[:, hc] * lax.rsqrt(jnp.mean(o[:, hc] * o[:, hc], axis=-1, keepdims=True) + EPS) for hc in hcols], axis=1)
        o_ref[0, rows, :] = (o * ng_ref[...] * _silu(hg_ref[0, rows, 3 * W:4 * W])).astype(BF16)
        bl = b[C - 1:C]
        ke = (kk * jnp.exp(bl - b)).astype(BF16)
        decay = jnp.exp(bl)
        for h, hc in enumerate(hcols):
            kv_t = lax.dot_general(vb[:, hc], ke[:, hc], TN, preferred_element_type=F32)
            st_scr[h] = decay[:, hc] * st_scr[h] + kv_t
        return carry

    lax.fori_loop(0, ts // C, chunk, 0, unroll=8)


def _hgrn_mxu(hg, lb, ng, ts):
    B, S, _ = hg.shape
    vec = pl.BlockSpec((1, HG_WIDTH), lambda b, i: (0, 0))
    return pl.pallas_call(
        _hgrn_mxu_kernel,
        out_shape=jax.ShapeDtypeStruct((B, S, HG_WIDTH), BF16),
        grid=(B, S // ts),
        in_specs=[pl.BlockSpec((1, ts, 4 * HG_WIDTH), lambda b, i: (b, i, 0)), vec, vec],
        out_specs=pl.BlockSpec((1, ts, HG_WIDTH), lambda b, i: (b, i, 0)),
        scratch_shapes=[pltpu.VMEM((HG_HEADS, HG_DV, HG_DK), F32)],
        compiler_params=_cparams(("arbitrary", "arbitrary")),
        name="hgrn_mxu",
    )(hg, lb, ng)


def _hgrn(hg, lb, ng):
    B, S, _ = hg.shape
    col = lambda k: pl.BlockSpec((1, S, HG_DK), lambda b, h, k=k: (b, 0, k * HG_HEADS + h))
    vec = pl.BlockSpec((1, HG_DK), lambda b, h: (0, h))
    return pl.pallas_call(
        _hgrn_kernel,
        out_shape=jax.ShapeDtypeStruct((B, S, HG_WIDTH), BF16),
        grid=(B, HG_HEADS),
        in_specs=[col(0), col(1), col(2), col(3), vec, vec],
        out_specs=pl.BlockSpec((1, S, HG_DV), lambda b, h: (b, 0, h)),
        compiler_params=_cparams(("arbitrary", "arbitrary")),
        name="hgrn",
    )(hg, hg, hg, hg, lb, ng)


def _cmpmlp_kernel(x_ref, w1a_ref, w1b_ref, pe_ref, w1_ref, b1_ref, w2_ref, o_ref):
    x = x_ref[0]
    hb = _dot3(pe_ref[...], w1_ref[...])[0:1] + b1_ref[...]
    nrow = x.shape[0]
    for g in range(NSA_KV_GROUPS):
        a = jnp.dot(x, w1a_ref[g], preferred_element_type=F32)
        bm = jnp.dot(x, w1b_ref[g], preferred_element_type=F32)
        hdn = a + pltpu.roll(bm, nrow - 1, 0) + hb
        o_ref[0, g] = jnp.dot(_gelu_tanh(hdn).astype(BF16), w2_ref[g], preferred_element_type=F32).astype(BF16)


def _cmpmlp(x2, w1a, w1b, pe8, w1, b1, w2p):
    B, ncb, width = x2.shape
    full = lambda a: pl.BlockSpec(a.shape, lambda b: (0,) * a.ndim)
    return pl.pallas_call(
        _cmpmlp_kernel,
        out_shape=jax.ShapeDtypeStruct((B, NSA_KV_GROUPS, ncb, LANES), BF16),
        grid=(B,),
        in_specs=[pl.BlockSpec((1, ncb, width), lambda b: (b, 0, 0)),
                  full(w1a), full(w1b), full(pe8), full(w1), full(b1), full(w2p)],
        out_specs=pl.BlockSpec((1, NSA_KV_GROUPS, ncb, LANES), lambda b: (b, 0, 0, 0)),
        compiler_params=_cparams(("arbitrary",)),
        name="cmpmlp",
    )(x2, w1a, w1b, pe8, w1, b1, w2p)


def _rank_before(score, nrows):
    jrow = lax.broadcasted_iota(jnp.int32, score.shape, 0)
    rank = jnp.zeros(score.shape, F32)
    for k in range(nrows):
        rk = score[k:k + 1]
        beats = (rk > score) | ((rk == score) & (jrow > k))
        rank = rank + jnp.where(beats, 1.0, 0.0)
    return rank


def _topk_rows(score, k):
    n = score.shape[0]
    row = lax.broadcasted_iota(jnp.int32, score.shape, 0).astype(F32)
    keep = jnp.zeros(score.shape, F32)
    for _ in range(k):
        top = jnp.max(score, axis=0, keepdims=True)
        first = jnp.min(jnp.where(score == top, row, float(n)), axis=0, keepdims=True)
        pick = row == first
        keep = jnp.where(pick, 1.0, keep)
        score = jnp.where(pick, -jnp.inf, score)
    return keep


def _transpose_small_ints(xt):
    c = xt.shape[1]
    eye = (lax.broadcasted_iota(jnp.int32, (c, c), 0) == lax.broadcasted_iota(jnp.int32, (c, c), 1)).astype(BF16)
    return lax.dot_general(eye, xt.astype(BF16), NT, preferred_element_type=F32)


def _pack_heads(heads):
    low = lax.broadcasted_iota(jnp.int32, heads[0].shape, 1) < NSA_HD
    return jnp.concatenate([jnp.where(low, heads[k], pltpu.roll(heads[k + 1], NSA_HD, 1))
                            for k in range(0, len(heads), 2)], axis=1)


def _cmpsel_kernel(q_ref, kc_ref, vc_ref, gate_ref, o_ref, sel_ref):
    tq = q_ref.shape[1]
    ncb = kc_ref.shape[2]
    t = pl.program_id(2)
    kc = kc_ref[0, 0]
    vc = vc_ref[0, 0]
    pos = t * tq + lax.broadcasted_iota(jnp.int32, (tq, 1), 0)
    cblk = lax.broadcasted_iota(jnp.int32, (1, ncb), 1)
    vis = (cblk * CMP_STRIDE + CMP_BLOCK - 1) <= pos
    any_vis = pos >= CMP_BLOCK - 1
    psum = jnp.zeros((tq, ncb), F32)
    gate = gate_ref[0]
    heads = []
    for h in range(NSA_HG):
        s = lax.dot_general(q_ref[0, :, h * LANES:(h + 1) * LANES], kc, NT, preferred_element_type=F32)
        s = jnp.where(vis, s, NEG)
        p = jnp.exp2(s - jnp.max(s, axis=-1, keepdims=True))
        p = p * jnp.where(any_vis, 1.0 / jnp.sum(p, axis=-1, keepdims=True), 0.0)
        heads.append(jnp.dot(p.astype(BF16), vc, preferred_element_type=F32) * gate[:, 3 * h:3 * h + 1])
        psum = psum + p
    o_ref[0] = _pack_heads(heads)
    nsb = ncb * CMP_STRIDE // SEL_BLOCK
    jb = lax.broadcasted_iota(jnp.int32, (nsb, ncb), 0) * SEL_BLOCK
    cb = lax.broadcasted_iota(jnp.int32, (nsb, ncb), 1) * CMP_STRIDE
    ov = jnp.maximum(jnp.minimum(cb + CMP_BLOCK, jb + SEL_BLOCK) - jnp.maximum(cb, jb), 0).astype(F32) / CMP_BLOCK
    ps_hi = psum.astype(BF16)
    ps_lo = (psum - ps_hi.astype(F32)).astype(BF16)
    ovb = ov.astype(BF16)
    pslc_t = (lax.dot_general(ovb, ps_hi, NT, preferred_element_type=F32)
              + lax.dot_general(ovb, ps_lo, NT, preferred_element_type=F32))
    posl = t * tq + lax.broadcasted_iota(jnp.int32, (1, tq), 1)
    cur = posl // SEL_BLOCK
    jrow = lax.broadcasted_iota(jnp.int32, (nsb, tq), 0)
    forced = (jrow == 0) | (jrow == cur) | (jrow == cur - 1)
    score = jnp.where(forced, 1e30, jnp.where(jrow <= cur, pslc_t, NEG))
    chosen = _topk_rows(score, min(N_SEL, nsb)) > 0.5
    drop = jnp.where(chosen & (score > -1e29), 0.0, 1.0)
    drop = jnp.concatenate([drop, jnp.zeros((LANES - nsb, tq), F32)], axis=0)
    sel_ref[0, 0] = (_transpose_small_ints(drop) * NEG).astype(BF16)


def _cmpsel(qraw, kcmp, vcmp, gate, tq):
    B, S, _ = qraw.shape
    ncb = kcmp.shape[2]
    gw = NSA_HG * LANES
    assert S // SEL_BLOCK <= NSA_HD
    cmp_spec = pl.BlockSpec((1, 1, ncb, LANES), lambda b, g, t: (b, g, 0, 0))
    return pl.pallas_call(
        _cmpsel_kernel,
        out_shape=(jax.ShapeDtypeStruct((B, S, NSA_WIDTH), F32),
                   jax.ShapeDtypeStruct((B, NSA_KV_GROUPS, S, LANES), BF16)),
        grid=(B, NSA_KV_GROUPS, S // tq),
        in_specs=[pl.BlockSpec((1, tq, gw), lambda b, g, t: (b, t, g)), cmp_spec, cmp_spec,
                  pl.BlockSpec((1, tq, LANES), lambda b, g, t: (b, t, g))],
        out_specs=(pl.BlockSpec((1, tq, NSA_WIDTH // NSA_KV_GROUPS), lambda b, g, t: (b, t, g)),
                   pl.BlockSpec((1, 1, tq, LANES), lambda b, g, t: (b, g, t, 0))),
        compiler_params=_cparams(("arbitrary", "arbitrary", "arbitrary")),
        name="cmpsel",
    )(qraw, kcmp, vcmp, gate)


def _slcwin_kernel(q_ref, sel_ref, ks_ref, kw_ref, vs_ref, vw_ref, gate_ref, ocmp_ref, wgu_ref, wdn_ref,
                   o_ref, wgu_bf_ref, wdn_bf_ref, *, tk, wk):
    wgu_bf_ref[...] = wgu_ref[...].astype(BF16)
    wdn_bf_ref[...] = wdn_ref[...].astype(BF16)
    tq = q_ref.shape[1]
    S = ks_ref.shape[1]
    t = pl.program_id(2)
    q0 = t * tq
    qpos = q0 + lax.broadcasted_iota(jnp.int32, (tq, 1), 0)
    qw = [q_ref[0, :, h * LANES:(h + 1) * LANES] for h in range(NSA_HG)]
    sel = sel_ref[0, 0]
    qa = [jnp.concatenate([q, sel], axis=1) for q in qw]

    gate = gate_ref[0]

    def normalise(acc, g):
        return acc * (g / acc[:, NSA_HD:NSA_HD + 1])

    nchain = NSA_HG // SLC_STACK
    qst = [jnp.concatenate(qa[c * SLC_STACK:(c + 1) * SLC_STACK], axis=0) for c in range(nchain)]
    qpos_st = jnp.concatenate([qpos] * SLC_STACK, axis=0)

    def kstep(j, carry, diag):
        k0 = pl.multiple_of(j * tk, tk)
        ks = ks_ref[0, pl.ds(k0, tk), :]
        vs = vs_ref[0, pl.ds(k0, tk), :]
        if diag:
            vis = (k0 + lax.broadcasted_iota(jnp.int32, (1, tk), 1)) <= qpos_st
        out = []
        for c in range(nchain):
            m, acc = carry[c]
            s = lax.dot_general(qst[c], ks, NT, preferred_element_type=F32)
            if diag:
                s = jnp.where(vis, s, NEG)
            mn = jnp.maximum(m, jnp.max(s, axis=-1, keepdims=True))
            p = jnp.exp2(s - mn)
            acc = jnp.exp2(m - mn) * acc + jnp.dot(p.astype(BF16), vs, preferred_element_type=F32)
            out.append((mn, acc))
        return tuple(out)

    init = tuple((jnp.full((SLC_STACK * tq, 1), NEG, F32), jnp.zeros((SLC_STACK * tq, LANES), F32))
                 for _ in range(nchain))
    jdiag = q0 // tk
    carry = lax.fori_loop(0, jdiag, lambda j, c: kstep(j, c, False), init)
    carry = kstep(jdiag, carry, True)
    heads = [normalise(carry[h // SLC_STACK][1][(h % SLC_STACK) * tq:(h % SLC_STACK + 1) * tq],
                       gate[:, 3 * h + 1:3 * h + 2]) for h in range(NSA_HG)]

    wq = wk - WINDOW
    wins = [[] for _ in range(NSA_HG)]
    spos = lax.broadcasted_iota(jnp.int32, (NSA_HG * wq, 1), 0) % wq
    for i in range(tq // wq):
        rows = slice(i * wq, (i + 1) * wq)
        start = pl.multiple_of(jnp.maximum(q0 + (i + 1) * wq - wk, 0), wq)
        kw = kw_ref[0, pl.ds(start, wk), :]
        vw = vw_ref[0, pl.ds(start, wk), :]
        d = (q0 + i * wq + spos) - (start + lax.broadcasted_iota(jnp.int32, (1, wk), 1))
        q_st = jnp.concatenate([q[rows] for q in qw], axis=0)
        s = jnp.where((d >= 0) & (d < WINDOW), lax.dot_general(q_st, kw, NT, preferred_element_type=F32), NEG)
        p = jnp.exp2(s - jnp.max(s, axis=-1, keepdims=True))
        o_st = jnp.dot(p.astype(BF16), vw, preferred_element_type=F32)
        for h in range(NSA_HG):
            wins[h].append(o_st[h * wq:(h + 1) * wq])
    for h in range(NSA_HG):
        heads[h] = heads[h] + normalise(jnp.concatenate(wins[h], axis=0), gate[:, 3 * h + 2:3 * h + 3])
    o_ref[0] = (ocmp_ref[0] + _pack_heads(heads)).astype(BF16)


def _slcwin(qrot, sel, ks, kw, vaug, gate, ocmp, wgu, wdn, tq, tk):
    B, S, _ = qrot.shape
    gw = NSA_HG * LANES
    wk = WINDOW + WIN_Q
    nt = S // tq
    nstep = B * NSA_KV_GROUPS * nt
    assert S >= wk and tq % WIN_Q == 0 and N_EXPERTS % nstep == 0
    epb = N_EXPERTS // nstep
    ospec = pl.BlockSpec((1, tq, NSA_WIDTH // NSA_KV_GROUPS), lambda b, g, t: (b, t, g))
    wspec = lambda a: pl.BlockSpec((epb,) + a.shape[1:], lambda b, g, t: ((b * NSA_KV_GROUPS + g) * nt + t, 0, 0))
    return pl.pallas_call(
        functools.partial(_slcwin_kernel, tk=tk, wk=wk),
        out_shape=(jax.ShapeDtypeStruct((B, S, NSA_WIDTH), BF16),
                   jax.ShapeDtypeStruct(wgu.shape, BF16), jax.ShapeDtypeStruct(wdn.shape, BF16)),
        grid=(B, NSA_KV_GROUPS, nt),
        in_specs=[pl.BlockSpec((1, tq, gw), lambda b, g, t: (b, t, g)),
                  pl.BlockSpec((1, 1, tq, LANES), lambda b, g, t: (b, g, t, 0)),
                  pl.BlockSpec((1, S, 2 * LANES), lambda b, g, t: (b, 0, 0)),
                  pl.BlockSpec((1, S, LANES), lambda b, g, t: (b, 0, 0)),
                  pl.BlockSpec((1, S, LANES), lambda b, g, t: (b, 0, g)),
                  pl.BlockSpec((1, S, LANES), lambda b, g, t: (b, 0, NSA_KV_GROUPS + g)),
                  pl.BlockSpec((1, tq, LANES), lambda b, g, t: (b, t, g)),
                  ospec, wspec(wgu), wspec(wdn)],
        out_specs=(ospec, wspec(wgu), wspec(wdn)),
        compiler_params=_cparams(("arbitrary", "arbitrary", "arbitrary")),
        name="slcwin",
    )(qrot, sel, ks, kw, vaug, vaug, gate, ocmp, wgu, wdn)


def _outproj_kernel(x_ref, ohg_ref, onsa_ref, wout_ref,
                    g1_ref, sh2_ref, sc2_ref, n2_ref, rwt_ref, rb_ref,
                    x1_ref, h2_ref, sc_ref, slot_t_ref, w_t_ref, cnt_ref):
    tm = x_ref.shape[1]
    mix = jnp.dot(jnp.concatenate([ohg_ref[0], onsa_ref[0]], axis=1), wout_ref[...], preferred_element_type=F32)
    x1 = x_ref[0] + g1_ref[0] * mix
    x1_ref[0] = x1
    h2 = (x1 * lax.rsqrt(jnp.mean(x1 * x1, axis=-1, keepdims=True) + EPS) * n2_ref[...]) * (1.0 + sc2_ref[0]) + sh2_ref[0]
    h_hi = h2.astype(BF16)
    h2_ref[0] = h_hi
    h_lo = (h2 - h_hi.astype(F32)).astype(BF16)
    both = lax.dot_general(rwt_ref[...], h_hi, NT, preferred_element_type=F32)
    logits = (both[:N_EXPERTS] + both[N_EXPERTS:]
              + lax.dot_general(rwt_ref[:N_EXPERTS, :], h_lo, NT, preferred_element_type=F32))
    scores = _sigmoid(logits)
    choice = scores + rb_ref[...]
    per = N_EXPERTS // N_GROUPS
    c3 = choice.reshape(N_GROUPS, per, tm)
    erow = lax.broadcasted_iota(jnp.int32, c3.shape, 1)
    rank_in = jnp.zeros(c3.shape, F32)
    for k in range(per):
        ck = c3[:, k:k + 1, :]
        rank_in = rank_in + jnp.where((ck > c3) | ((ck == c3) & (erow > k)), 1.0, 0.0)
    grp_score = jnp.sum(jnp.where(rank_in < 2, c3, 0.0), axis=1)
    grp_keep = _rank_before(grp_score, N_GROUPS) < TOPK_GROUPS
    masked = jnp.where(grp_keep[:, None, :], c3, -jnp.inf).reshape(N_EXPERTS, tm)
    keep = _topk_rows(masked, TOP_K) > 0.5
    tw = jnp.where(keep, scores, 0.0)
    tw = tw / jnp.sum(tw, axis=0, keepdims=True) * ROUTED_SCALE
    before = lax.broadcasted_iota(jnp.int32, (tm, tm), 0)
    after = lax.broadcasted_iota(jnp.int32, (tm, tm), 1)
    earlier = (before < after) & (before // MOE_SUB == after // MOE_SUB)
    kept = jnp.where(keep, 1.0, 0.0)
    pref = jnp.dot(kept.astype(BF16), earlier.astype(BF16), preferred_element_type=F32)
    slot = jnp.where(keep, pref, -1.0)
    assert MOE_SUB <= 256
    sc_ref[0] = _transpose_small_ints(slot)
    for j in range(tm // MOE_SUB):
        sub = slice(j * MOE_SUB, (j + 1) * MOE_SUB)
        slot_t_ref[:, j] = slot[:, sub].reshape(N_EXPERTS // MOE_EG, MOE_EG, MOE_SUB)
        w_t_ref[:, j] = tw[:, sub].reshape(N_EXPERTS // MOE_EG, MOE_EG, MOE_SUB)
        cnt_ref[j] = jnp.broadcast_to(jnp.sum(kept[:, sub], axis=1, keepdims=True), (N_EXPERTS, LANES))


def _outproj(x, ohg, onsa, wout, mod3, n2, rwt, rb, tm):
    B, S, D = x.shape
    blk = lambda w: pl.BlockSpec((1, tm, w), lambda b, i: (b, i, 0))
    full = lambda a: pl.BlockSpec(a.shape, lambda b, i: (0,) * a.ndim)
    modc = lambda k: pl.BlockSpec((1, 1, D), lambda b, i, k=k: (b, 0, k))
    nt = S // tm
    ng = N_EXPERTS // MOE_EG
    nh = tm // MOE_SUB
    tspec = pl.BlockSpec((ng, nh, MOE_EG, MOE_SUB), lambda b, i: (0, b * nt + i, 0, 0))
    return pl.pallas_call(
        _outproj_kernel,
        out_shape=(jax.ShapeDtypeStruct((B, S, D), F32),
                   jax.ShapeDtypeStruct((B, S, D), BF16),
                   jax.ShapeDtypeStruct((B, S, N_EXPERTS), F32),
                   jax.ShapeDtypeStruct((ng, B * nt * nh, MOE_EG, MOE_SUB), F32),
                   jax.ShapeDtypeStruct((ng, B * nt * nh, MOE_EG, MOE_SUB), F32),
                   jax.ShapeDtypeStruct((B * nt * nh, N_EXPERTS, LANES), F32)),
        grid=(B, nt),
        in_specs=[blk(D), blk(HG_WIDTH), blk(NSA_WIDTH), full(wout),
                  modc(2), modc(3), modc(4), full(n2), full(rwt), full(rb)],
        out_specs=(blk(D), blk(D), blk(N_EXPERTS), tspec, tspec,
                   pl.BlockSpec((nh, N_EXPERTS, LANES), lambda b, i: (b * nt + i, 0, 0))),
        compiler_params=_cparams(("arbitrary", "arbitrary")),
        name="outproj",
    )(x, ohg, onsa, wout, mod3, mod3, mod3, n2, rwt, rb)


def _swiglu_hidden(x, wgu):
    gu = jnp.dot(x, wgu, preferred_element_type=F32)
    return _silu(gu[:, :EXPERT_HIDDEN]) * gu[:, EXPERT_HIDDEN:]


def _moe_kernel(rounds_ref, h_ref, slot_t_ref, w_t_ref, slot_c_ref, wgu_ref, wdn_ref, sgu_ref, sdn_ref,
                o_ref, x_scr, p_scr):
    i = pl.program_id(0)
    g = pl.program_id(1)
    tm = h_ref.shape[0]
    nsub = tm // MOE_SUB
    rnd, eg = MOE_ROUND, MOE_EG
    ng = N_EXPERTS // eg
    base = i * (N_EXPERTS + ng)

    @pl.when(g == 0)
    def _():
        act = _swiglu_hidden(h_ref[...], sgu_ref[...]).astype(BF16)
        o_ref[...] = jnp.dot(act, sdn_ref[...], preferred_element_type=F32)

    blk = eg * rnd
    rslot = lax.broadcasted_iota(jnp.int32, (rnd, 1), 0).astype(F32)

    def gather(first, nround):
        for s in range(nsub):
            st = slot_t_ref[0, s] - first
            p = jnp.concatenate([jnp.where(rslot + float(w * rnd) == st[e:e + 1, :], 1.0, 0.0)
                                 for w in range(nround) for e in range(eg)], axis=0)
            p_scr[s, 0:nround * blk, :] = p.astype(BF16)
            x_scr[s, 0:nround * blk, :] = jnp.dot(p_scr[s, 0:nround * blk, :], h_ref[s * MOE_SUB:(s + 1) * MOE_SUB, :],
                                                  preferred_element_type=F32).astype(BF16)

    def run_expert(e, w):
        start = w * blk + e * rnd
        rows = pl.ds(start if isinstance(e, int) else pl.multiple_of(start, rnd), rnd)
        xe = jnp.concatenate([x_scr[s, rows, :] for s in range(nsub)], axis=0)
        wslot = jnp.concatenate(
            [jnp.sum(p_scr[s, rows, :].astype(F32) * w_t_ref[0, s, pl.ds(e, 1), :], axis=-1, keepdims=True)
             for s in range(nsub)], axis=0)
        y = jnp.dot((_swiglu_hidden(xe, wgu_ref[e]) * wslot).astype(BF16), wdn_ref[e], preferred_element_type=F32)
        for s in range(nsub):
            x_scr[s, rows, :] = y[s * rnd:(s + 1) * rnd].astype(BF16)

    def run_active(r, w):
        def body(e, carry):
            pl.when(rounds_ref[base + g * eg + e] > r)(functools.partial(run_expert, e, w))
            return carry
        lax.fori_loop(0, eg, body, 0)

    def scatter(first, nround):
        lane = lax.broadcasted_iota(jnp.int32, (1, nround * blk), 1)
        lane_slot = ((lane // blk) * rnd + lane % rnd).astype(F32)
        spread = ((lax.broadcasted_iota(jnp.int32, (N_EXPERTS, nround * blk), 1) % blk) // rnd + g * eg
                  == lax.broadcasted_iota(jnp.int32, (N_EXPERTS, nround * blk), 0)).astype(BF16)
        for s in range(nsub):
            toks = slice(s * MOE_SUB, (s + 1) * MOE_SUB)
            sc = jnp.dot(slot_c_ref[toks, :].astype(BF16), spread, preferred_element_type=F32) - first
            pt = jnp.where(sc == lane_slot, 1.0, 0.0).astype(BF16)
            o_ref[toks, :] += jnp.dot(pt, x_scr[s, 0:nround * blk, :], preferred_element_type=F32)

    gather(0.0, MOE_FIRST)
    for e in range(eg):
        run_expert(e, 0)
    for w in range(1, MOE_FIRST):
        run_active(w, w)
    scatter(0.0, MOE_FIRST)

    def later_round(r, carry):
        first = (r * rnd).astype(F32)
        gather(first, 1)
        run_active(r, 0)
        scatter(first, 1)
        return carry

    lax.fori_loop(MOE_FIRST, rounds_ref[base + N_EXPERTS + g], later_round, 0)


def _moe(rounds, h2, slot_t, w_t, slot_c, wgu, wdn, sgu, sdn, tm):
    T, D = h2.shape
    eg, nsub = MOE_EG, tm // MOE_SUB
    full = lambda a: pl.BlockSpec(a.shape, lambda i, e, o: (0,) * a.ndim)
    tspec = pl.BlockSpec((1, nsub, eg, MOE_SUB), lambda i, e, o: (e, i, 0, 0))
    return pl.pallas_call(
        _moe_kernel,
        out_shape=jax.ShapeDtypeStruct((T, D), F32),
        grid_spec=pltpu.PrefetchScalarGridSpec(
            num_scalar_prefetch=1,
            grid=(T // tm, N_EXPERTS // eg),
            in_specs=[pl.BlockSpec((tm, D), lambda i, e, o: (i, 0), pipeline_mode=pl.Buffered(1)), tspec, tspec,
                      pl.BlockSpec((tm, N_EXPERTS), lambda i, e, o: (i, 0)),
                      pl.BlockSpec((eg, D, 2 * EXPERT_HIDDEN), lambda i, e, o: (e, 0, 0)),
                      pl.BlockSpec((eg, EXPERT_HIDDEN, D), lambda i, e, o: (e, 0, 0)),
                      full(sgu), full(sdn)],
            out_specs=pl.BlockSpec((tm, D), lambda i, e, o: (i, 0), pipeline_mode=pl.Buffered(1)),
            scratch_shapes=[pltpu.VMEM((nsub, MOE_FIRST * eg * MOE_ROUND, D), BF16),
                            pltpu.VMEM((nsub, MOE_FIRST * eg * MOE_ROUND, MOE_SUB), BF16)]),
        compiler_params=_cparams(("arbitrary", "arbitrary")),
        name="moe",
    )(rounds, h2, slot_t, w_t, slot_c, wgu, wdn, sgu, sdn)


def _final_kernel(x1_ref, moe_ref, g2_ref, fg_ref, o_ref):
    x2 = x1_ref[0] + g2_ref[0] * moe_ref[0]
    o_ref[0] = x2 * lax.rsqrt(jnp.mean(x2 * x2, axis=-1, keepdims=True) + EPS) * fg_ref[...]


def _final(x1, moe, mod3, fg, tm):
    B, S, D = x1.shape
    blk = pl.BlockSpec((1, tm, D), lambda b, i: (b, i, 0))
    return pl.pallas_call(
        _final_kernel,
        out_shape=jax.ShapeDtypeStruct((B, S, D), F32),
        grid=(B, S // tm),
        in_specs=[blk, blk, pl.BlockSpec((1, 1, D), lambda b, i: (b, 0, 5)), pl.BlockSpec((1, D), lambda b, i: (0, 0))],
        out_specs=blk,
        compiler_params=_cparams(("arbitrary", "arbitrary")),
        name="final",
    )(x1, moe, mod3, fg)


def _split_bf16(w):
    hi = w.astype(BF16)
    return jnp.concatenate([hi, (w - hi.astype(F32)).astype(BF16)], axis=0)


def _pack_w_in(w_in):
    return jnp.pad(w_in, ((0, 0), (0, IN_COLS_P - w_in.shape[1]))).astype(BF16)


def _pack_cmp(pos, w1, b1, w2, lane_by_group):
    half = CMP_STRIDE * NSA_HD
    def rows_for(wh):
        w3 = wh.reshape(CMP_STRIDE, NSA_HD, CMP_HIDDEN)
        z = jnp.zeros_like(w3)
        return jnp.stack([jnp.concatenate([w3, z], axis=1).reshape(CMP_STRIDE * LANES, CMP_HIDDEN),
                          jnp.concatenate([z, w3], axis=1).reshape(CMP_STRIDE * LANES, CMP_HIDDEN)])
    w1a = rows_for(w1[:half]).astype(BF16)
    w1b = rows_for(w1[half:]).astype(BF16)
    z2 = jnp.zeros_like(w2)
    w2_first = jnp.concatenate([w2, z2], axis=1)
    w2p = jnp.stack([w2_first, jnp.concatenate([z2, w2], axis=1) if lane_by_group else w2_first]).astype(BF16)
    pe8 = jnp.pad(pos.reshape(1, CMP_BLOCK * NSA_HD), ((0, 7), (0, 0)))
    return w1a, w1b, pe8, w1, b1.reshape(1, CMP_HIDDEN), w2p


def _rope_tables(S):
    half = NSA_HD // 2
    inv = ROPE_THETA ** (-jnp.arange(half, dtype=F32) / half)
    ang = jnp.arange(S, dtype=F32)[:, None] * inv[None, :]
    cos, sin = jnp.cos(ang), jnp.sin(ang)
    reps = LANES // NSA_HD
    return jnp.tile(jnp.concatenate([cos, cos], axis=1), (1, reps)), jnp.tile(jnp.concatenate([-sin, sin], axis=1), (1, reps))


def _tiles(S):
    return dict(inproj=min(512, S), hgrn=min(512, S), cmpsel=min(512, S), slc_q=512, slc_k=512, moe=min(2048, S), final=min(512, S))


def kernel(x, c, w_ada, b_ada, norm1_g, w_in, hg_lb_logits, hg_norm_g, cmp_pos_k, cmp_w1_k, cmp_b1_k, cmp_w2_k,
           cmp_pos_v, cmp_w1_v, cmp_b1_v, cmp_w2_v, w_out, norm2_g, router_w, router_bias, w_exp_gu, w_exp_dn,
           w_sh_gu, w_sh_dn, final_g):
    B, S, D = x.shape
    assert D == D_MODEL and w_ada.shape[0] == 1 and S % 512 == 0
    tl = _tiles(S)
    l = 0
    lb = jnp.cumsum(jax.nn.softmax(hg_lb_logits.astype(F32), axis=0), axis=0)[l].reshape(1, HG_WIDTH)
    c8 = jnp.pad(c, ((0, 8 - B), (0, 0)))
    mod3 = _ada(c8, w_ada[l], b_ada[l].reshape(1, -1))[:B].reshape(B, 1, 6 * D)
    cos, sin = _rope_tables(S)
    hg, qraw, qrot, kc, vc, ks, kw, vaug, gate = _inproj(x, mod3, norm1_g[l].reshape(1, D), _pack_w_in(w_in[l]),
                                                         cos, sin, tl["inproj"])
    ng = hg_norm_g[l].reshape(1, HG_WIDTH)
    mxu_safe = -HG_SUB * jnp.log(jnp.min(lb)) <= HG_SAFE_LOG_RANGE
    ohg = lax.cond(mxu_safe, lambda: _hgrn_mxu(hg, lb, ng, tl["hgrn"]), lambda: _hgrn(hg, lb, ng))
    ncb = S // CMP_STRIDE
    kcmp = _cmpmlp(kc.reshape(B, ncb, CMP_STRIDE * LANES),
                   *_pack_cmp(cmp_pos_k[l], cmp_w1_k[l], cmp_b1_k[l], cmp_w2_k[l], True))
    vcmp = _cmpmlp(vc.reshape(B, ncb, CMP_STRIDE * LANES),
                   *_pack_cmp(cmp_pos_v[l], cmp_w1_v[l], cmp_b1_v[l], cmp_w2_v[l], False))
    ocmp, sel = _cmpsel(qraw, kcmp, vcmp, gate, tl["cmpsel"])
    onsa, wgu_bf, wdn_bf = _slcwin(qrot, sel, ks, kw, vaug, gate, ocmp, w_exp_gu[l], w_exp_dn[l],
                                   tl["slc_q"], tl["slc_k"])
    x1, h2, sc, slot_t, w_t, cnt = _outproj(x, ohg, onsa, w_out[l].astype(BF16), mod3,
                                            norm2_g[l].reshape(1, D), _split_bf16(router_w[l].T),
                                            router_bias[l].reshape(N_EXPERTS, 1), ROUTE_TM)
    T = B * S
    tm = tl["moe"]
    ng = N_EXPERTS // MOE_EG
    slot_c = sc.reshape(T, N_EXPERTS)
    load = jnp.max(cnt[:, :, 0].reshape(T // tm, tm // MOE_SUB, N_EXPERTS), axis=1).astype(jnp.int32)
    per_expert = (load + MOE_ROUND - 1) // MOE_ROUND
    rounds = jnp.concatenate([per_expert, jnp.max(per_expert.reshape(T // tm, ng, MOE_EG), axis=2)], axis=1).reshape(-1)
    moe = _moe(rounds, h2.reshape(T, D), slot_t, w_t, slot_c,
               wgu_bf, wdn_bf, w_sh_gu[l].astype(BF16), w_sh_dn[l].astype(BF16), tm)
    return _final(x1, moe.reshape(B, S, D), mod3, final_g.reshape(1, D), tl["final"])
```

```python
import functools

import jax
import jax.numpy as jnp
from jax import lax
from jax.experimental import pallas as pl
from jax.experimental.pallas import tpu as pltpu

F32 = jnp.float32
BF16 = jnp.bfloat16
HIGHEST = lax.Precision.HIGHEST

D_MODEL = 1024
EPS = 1e-6
HG_HEADS = 4
HG_DK = 128
HG_DV = 128
HG_WIDTH = HG_HEADS * HG_DV
HG_CHUNK = 64
HG_SUB = 16
HG_SAFE_LOG_RANGE = 80.0
NSA_HEADS = 8
NSA_KV_GROUPS = 2
NSA_HG = NSA_HEADS // NSA_KV_GROUPS
NSA_HD = 64
NSA_WIDTH = NSA_HEADS * NSA_HD
NSA_KV = NSA_KV_GROUPS * NSA_HD
CMP_BLOCK = 32
CMP_STRIDE = 16
CMP_HIDDEN = 256
SEL_BLOCK = 64
N_SEL = 16
WINDOW = 512
ROPE_THETA = 10000.0
SLC_STACK = 4
WIN_Q = 256
N_EXPERTS = 64
TOP_K = 8
N_GROUPS = 8
TOPK_GROUPS = 4
EXPERT_HIDDEN = 256
ROUTED_SCALE = 2.5
ROUTE_TM = 512
MOE_SUB = 256
MOE_ROUND = 32
MOE_EG = 8
MOE_FIRST = 2

LANES = 128
NEG = -1e30
VMEM_LIMIT = 56 * 1024 * 1024

C_HG = 0
C_Q = 4 * HG_WIDTH
C_K = C_Q + NSA_WIDTH
C_GATE = C_K + 6 * NSA_KV
IN_COLS_P = C_GATE + LANES
LOG2E = 1.4426950408889634

NT = (((1,), (1,)), ((), ()))
TN = (((0,), (0,)), ((), ()))


def _cparams(sem):
    return pltpu.CompilerParams(dimension_semantics=sem, vmem_limit_bytes=VMEM_LIMIT)


def _sigmoid(x):
    return 1.0 / (1.0 + jnp.exp(-x))


def _silu(x):
    return x * _sigmoid(x)


def _gelu_tanh(x):
    return 0.5 * x * (1.0 + jnp.tanh(0.7978845608028654 * (x + 0.044715 * (x * x * x))))


def _dot3(a, w):
    n = a.shape[0]
    a_hi = a.astype(BF16)
    a_lo = (a - a_hi.astype(F32)).astype(BF16)
    w_hi = w.astype(BF16)
    w_lo = (w - w_hi.astype(F32)).astype(BF16)
    both = jnp.dot(jnp.concatenate([a_hi, a_lo], axis=0), w_hi, preferred_element_type=F32)
    return both[:n] + both[n:] + jnp.dot(a_hi, w_lo, preferred_element_type=F32)


def _ada_kernel(c_ref, w_ref, b_ref, o_ref):
    o_ref[...] = _dot3(_silu(c_ref[...]), w_ref[...]) + b_ref[...]


def _ada(c8, w, b):
    n = w.shape[1]
    tn = 1024
    return pl.pallas_call(
        _ada_kernel,
        out_shape=jax.ShapeDtypeStruct((8, n), F32),
        grid=(n // tn,),
        in_specs=[pl.BlockSpec((8, D_MODEL), lambda j: (0, 0)),
                  pl.BlockSpec((D_MODEL, tn), lambda j: (0, j)),
                  pl.BlockSpec((1, tn), lambda j: (0, j))],
        out_specs=pl.BlockSpec((8, tn), lambda j: (0, j)),
        compiler_params=_cparams(("arbitrary",)),
        name="ada",
    )(c8, w, b)


def _rope(t, cos, sin_signed, first_half):
    rot = jnp.where(first_half, pltpu.roll(t, 96, 1), pltpu.roll(t, 32, 1))
    return t * cos + rot * sin_signed


def _inproj_kernel(x_ref, sh_ref, sc_ref, g_ref, w_ref, cos_ref, sin_ref,
                   hg_ref, qraw_ref, qrot_ref, kc_ref, vc_ref, ks_ref, kw_ref, va_ref, gate_ref, h_scr, kv_scr):
    tm = x_ref.shape[1]
    x = x_ref[0]
    y = x * lax.rsqrt(jnp.mean(x * x, axis=-1, keepdims=True) + EPS) * g_ref[...]
    h_scr[...] = (y * (1.0 + sc_ref[0]) + sh_ref[0]).astype(BF16)

    def mm(lo, width):
        return jnp.dot(h_scr[...], w_ref[:, lo:lo + width], preferred_element_type=F32)

    cos = cos_ref[...]
    sin = sin_ref[...]
    first_half = (lax.broadcasted_iota(jnp.int32, cos.shape, 1) % NSA_HD) < (NSA_HD // 2)
    lane = lax.broadcasted_iota(jnp.int32, (tm, LANES), 1)
    half_of_lane = lane // NSA_HD

    def own_lanes(t, src_half, dst_half):
        moved = t if src_half == dst_half else pltpu.roll(t, NSA_HD, 1)
        return jnp.where(half_of_lane == dst_half, moved, 0.0)

    for j in range(4):
        hg_ref[0, :, j * HG_WIDTH:(j + 1) * HG_WIDTH] = mm(C_HG + j * HG_WIDTH, HG_WIDTH)
    for m in range(NSA_WIDTH // (2 * LANES)):
        qq = mm(C_Q + m * 2 * LANES, 2 * LANES) * (NSA_HD ** -0.5 * LOG2E)
        for c in range(2):
            q = qq[:, c * LANES:(c + 1) * LANES]
            qr = _rope(q, cos, sin, first_half)
            for half in range(2):
                n = (2 * m + c) * 2 + half
                g = n // NSA_HG
                qraw_ref[0, :, n * LANES:(n + 1) * LANES] = own_lanes(q, half, g).astype(BF16)
                qrot_ref[0, :, n * LANES:(n + 1) * LANES] = own_lanes(qr, half, g).astype(BF16)
    kv = mm(C_K, 2 * LANES)
    kv_scr[0] = kv[:, :LANES]
    kv_scr[1] = kv[:, LANES:]
    for r in range(CMP_STRIDE):
        rows = pl.ds(r, tm // CMP_STRIDE, stride=CMP_STRIDE)
        kc_ref[0, r] = kv_scr[0, rows, :].astype(BF16)
        vc_ref[0, r] = kv_scr[1, rows, :].astype(BF16)
    pos = pl.program_id(1) * tm + lax.broadcasted_iota(jnp.int32, (tm, LANES), 0)
    ks_ref[0, :, LANES:2 * LANES] = jnp.where(lane == pos // SEL_BLOCK, 1.0, 0.0).astype(BF16)
    for j in range(2):
        kv = mm(C_K + (j + 1) * 2 * LANES, 2 * LANES)
        k_rot = _rope(kv[:, :LANES], cos, sin, first_half).astype(BF16)
        if j == 0:
            ks_ref[0, :, 0:LANES] = k_rot
        else:
            kw_ref[0] = k_rot
        for g in range(NSA_KV_GROUPS):
            v = own_lanes(kv[:, LANES:], g, 0)
            col = (j * NSA_KV_GROUPS + g) * LANES
            va_ref[0, :, col:col + LANES] = jnp.where(lane == NSA_HD, 1.0, v).astype(BF16)
    gate = _sigmoid(mm(C_GATE, LANES))
    per_group = 3 * NSA_HG
    for g in range(NSA_KV_GROUPS):
        gate_ref[0, :, g * LANES:(g + 1) * LANES] = gate if g == 0 else pltpu.roll(gate, LANES - g * per_group, 1)


def _inproj(x, mod3, norm_g, w_p, cos, sin, tm):
    B, S, D = x.shape
    blk = lambda w: pl.BlockSpec((1, tm, w), lambda b, i: (b, i, 0))
    cmp_in = jax.ShapeDtypeStruct((B, CMP_STRIDE, S // CMP_STRIDE, LANES), BF16)
    cmp_blk = pl.BlockSpec((1, CMP_STRIDE, tm // CMP_STRIDE, LANES), lambda b, i: (b, 0, i, 0))
    return pl.pallas_call(
        _inproj_kernel,
        out_shape=(jax.ShapeDtypeStruct((B, S, 4 * HG_WIDTH), F32),
                   jax.ShapeDtypeStruct((B, S, NSA_HEADS * LANES), BF16),
                   jax.ShapeDtypeStruct((B, S, NSA_HEADS * LANES), BF16),
                   cmp_in,
                   cmp_in,
                   jax.ShapeDtypeStruct((B, S, 2 * LANES), BF16),
                   jax.ShapeDtypeStruct((B, S, LANES), BF16),
                   jax.ShapeDtypeStruct((B, S, 4 * LANES), BF16),
                   jax.ShapeDtypeStruct((B, S, NSA_KV_GROUPS * LANES), F32)),
        grid=(B, S // tm),
        in_specs=[blk(D),
                  pl.BlockSpec((1, 1, D), lambda b, i: (b, 0, 0)),
                  pl.BlockSpec((1, 1, D), lambda b, i: (b, 0, 1)),
                  pl.BlockSpec((1, D), lambda b, i: (0, 0)),
                  pl.BlockSpec((D, IN_COLS_P), lambda b, i: (0, 0)),
                  pl.BlockSpec((tm, LANES), lambda b, i: (i, 0)),
                  pl.BlockSpec((tm, LANES), lambda b, i: (i, 0))],
        out_specs=(blk(4 * HG_WIDTH), blk(NSA_HEADS * LANES), blk(NSA_HEADS * LANES),
                   cmp_blk, cmp_blk, blk(2 * LANES), blk(LANES), blk(4 * LANES), blk(NSA_KV_GROUPS * LANES)),
        scratch_shapes=[pltpu.VMEM((tm, D), BF16), pltpu.VMEM((2, tm, LANES), F32)],
        compiler_params=_cparams(("arbitrary", "arbitrary")),
        name="inproj",
    )(x, mod3, mod3, norm_g, w_p, cos, sin)


def _hgrn_kernel(q_ref, f_ref, i_ref, gt_ref, lb_ref, ng_ref, o_ref):
    S = q_ref.shape[1]
    C, U = HG_CHUNK, HG_SUB
    lb = lb_ref[...]
    ng = ng_ref[...]
    ri = lax.broadcasted_iota(jnp.int32, (C, C), 0)
    ci = lax.broadcasted_iota(jnp.int32, (C, C), 1)
    tril = (ri >= ci).astype(F32)
    trow = lax.broadcasted_iota(jnp.int32, (U, 1), 0)

    def chunk(c, st_t):
        r0 = pl.multiple_of(c * C, C)
        rows = pl.ds(r0, C)
        f = lb + (1.0 - lb) * _sigmoid(f_ref[0, rows, :])
        kk = 1.0 - f
        b = jnp.dot(tril, jnp.log(f), precision=HIGHEST, preferred_element_type=F32)
        q = q_ref[0, rows, :] * (HG_DK ** -0.5)
        v = i_ref[0, rows, :]
        vb = v.astype(BF16)
        o_inter = lax.dot_general((q * jnp.exp(b)).astype(BF16), st_t.astype(BF16), NT,
                                  preferred_element_type=F32)
        parts = []
        for i in range(C // U):
            lo = i * U
            bi = b[lo:lo + U]
            qi = q[lo:lo + U]
            if i == 0:
                oi = jnp.zeros((U, HG_DV), F32)
            else:
                r = b[lo - 1:lo]
                qrel = (qi * jnp.exp(bi - r)).astype(BF16)
                kprev = (kk[:lo] * jnp.exp(r - b[:lo])).astype(BF16)
                a_off = lax.dot_general(qrel, kprev, NT, preferred_element_type=F32)
                oi = jnp.dot(a_off.astype(BF16), vb[:lo], preferred_element_type=F32)
            for s in range(U):
                valid = trow >= s
                e = jnp.exp(jnp.where(valid, bi - bi[s:s + 1], 0.0))
                a = jnp.sum(qi * e * kk[lo + s:lo + s + 1], axis=-1, keepdims=True)
                oi = oi + jnp.where(valid, a, 0.0) * v[lo + s:lo + s + 1]
            parts.append(oi)
        o = o_inter + jnp.concatenate(parts, axis=0)
        o = o * lax.rsqrt(jnp.mean(o * o, axis=-1, keepdims=True) + EPS) * ng
        o_ref[0, rows, :] = (o * _silu(gt_ref[0, rows, :])).astype(BF16)
        bl = b[C - 1:C]
        kv_t = lax.dot_general(vb, (kk * jnp.exp(bl - b)).astype(BF16), TN, preferred_element_type=F32)
        return jnp.exp(bl) * st_t + kv_t

    lax.fori_loop(0, S // C, chunk, jnp.zeros((HG_DV, HG_DK), F32))


def _cumsum_rows(x):
    n = x.shape[0]
    row = lax.broadcasted_iota(jnp.int32, x.shape, 0)
    d = 1
    while d < n:
        x = x + jnp.where(row >= d, pltpu.roll(x, d, 0), 0.0)
        d *= 2
    return x


def _hgrn_mxu_kernel(hg_ref, lb_ref, ng_ref, o_ref, st_scr):
    ts = hg_ref.shape[1]
    C, U = HG_CHUNK, HG_SUB

    @pl.when(pl.program_id(1) == 0)
    def _():
        st_scr[...] = jnp.zeros(st_scr.shape, F32)

    W = HG_WIDTH
    NH = HG_HEADS
    head_of_lane = lax.broadcasted_iota(jnp.int32, (1, W), 1) // HG_DK
    hcols = [slice(h * HG_DK, (h + 1) * HG_DK) for h in range(NH)]

    def chunk(c, carry):
        rows = pl.ds(pl.multiple_of(c * C, C), C)
        lb = lb_ref[...]
        f = lb + (1.0 - lb) * _sigmoid(hg_ref[0, rows, W:2 * W])
        kk = 1.0 - f
        b = _cumsum_rows(jnp.log(f))
        q = hg_ref[0, rows, 0:W] * (HG_DK ** -0.5)
        vb = hg_ref[0, rows, 2 * W:3 * W].astype(BF16)
        qe = (q * jnp.exp(b)).astype(BF16)
        o_inter = jnp.concatenate(
            [lax.dot_general(qe[:, hc], st_scr[h].astype(BF16), NT, preferred_element_type=F32)
             for h, hc in enumerate(hcols)], axis=1)
        parts = []
        for i in range(C // U):
            lo, hi = i * U, (i + 1) * U
            r = b[lo - 1:lo] if i else jnp.zeros((1, W), F32)
            qrel = q[lo:hi] * jnp.exp(b[lo:hi] - r)
            kall = (kk[:hi] * jnp.exp(r - b[:hi])).astype(BF16)
            qbd = jnp.concatenate([jnp.where(head_of_lane == h, qrel, 0.0) for h in range(NH)], axis=0)
            a = lax.dot_general(qbd.astype(BF16), kall, NT, preferred_element_type=F32)
            trow = lax.broadcasted_iota(jnp.int32, (NH * U, hi), 0) % U
            a = jnp.where(lax.broadcasted_iota(jnp.int32, (NH * U, hi), 1) <= lo + trow, a, 0.0)
            oa = jnp.dot(a.astype(BF16), vb[:hi], preferred_element_type=F32)
            oi = jnp.where(head_of_lane == 0, oa[0:U], 0.0)
            for h in range(1, NH):
                oi = jnp.where(head_of_lane == h, oa[h * U:(h + 1) * U], oi)
            parts.append(oi)
        o = o_inter + jnp.concatenate(parts, axis=0)
        o = jnp.concatenate(
            [o[:, hc] * lax.rsqrt(jnp.mean(o[:, hc] * o[:, hc], axis=-1, keepdims=True) + EPS) for hc in hcols], axis=1)
        o_ref[0, rows, :] = (o * ng_ref[...] * _silu(hg_ref[0, rows, 3 * W:4 * W])).astype(BF16)
        bl = b[C - 1:C]
        ke = (kk * jnp.exp(bl - b)).astype(BF16)
        decay = jnp.exp(bl)
        for h, hc in enumerate(hcols):
            kv_t = lax.dot_general(vb[:, hc], ke[:, hc], TN, preferred_element_type=F32)
            st_scr[h] = decay[:, hc] * st_scr[h] + kv_t
        return carry

    lax.fori_loop(0, ts // C, chunk, 0, unroll=8)


def _hgrn_mxu(hg, lb, ng, ts):
    B, S, _ = hg.shape
    vec = pl.BlockSpec((1, HG_WIDTH), lambda b, i: (0, 0))
    return pl.pallas_call(
        _hgrn_mxu_kernel,
        out_shape=jax.ShapeDtypeStruct((B, S, HG_WIDTH), BF16),
        grid=(B, S // ts),
        in_specs=[pl.BlockSpec((1, ts, 4 * HG_WIDTH), lambda b, i: (b, i, 0)), vec, vec],
        out_specs=pl.BlockSpec((1, ts, HG_WIDTH), lambda b, i: (b, i, 0)),
        scratch_shapes=[pltpu.VMEM((HG_HEADS, HG_DV, HG_DK), F32)],
        compiler_params=_cparams(("arbitrary", "arbitrary")),
        name="hgrn_mxu",
    )(hg, lb, ng)


def _hgrn(hg, lb, ng):
    B, S, _ = hg.shape
    col = lambda k: pl.BlockSpec((1, S, HG_DK), lambda b, h, k=k: (b, 0, k * HG_HEADS + h))
    vec = pl.BlockSpec((1, HG_DK), lambda b, h: (0, h))
    return pl.pallas_call(
        _hgrn_kernel,
        out_shape=jax.ShapeDtypeStruct((B, S, HG_WIDTH), BF16),
        grid=(B, HG_HEADS),
        in_specs=[col(0), col(1), col(2), col(3), vec, vec],
        out_specs=pl.BlockSpec((1, S, HG_DV), lambda b, h: (b, 0, h)),
        compiler_params=_cparams(("arbitrary", "arbitrary")),
        name="hgrn",
    )(hg, hg, hg, hg, lb, ng)


def _cmpmlp_kernel(x_ref, w1a_ref, w1b_ref, pe_ref, w1_ref, b1_ref, w2_ref, o_ref):
    x = jnp.concatenate([x_ref[0, r] for r in range(CMP_STRIDE)], axis=1)
    hb = _dot3(pe_ref[...], w1_ref[...])[0:1] + b1_ref[...]
    nrow = x.shape[0]
    for g in range(NSA_KV_GROUPS):
        a = jnp.dot(x, w1a_ref[g], preferred_element_type=F32)
        bm = jnp.dot(x, w1b_ref[g], preferred_element_type=F32)
        hdn = a + pltpu.roll(bm, nrow - 1, 0) + hb
        o_ref[0, g] = jnp.dot(_gelu_tanh(hdn).astype(BF16), w2_ref[g], preferred_element_type=F32).astype(BF16)


def _cmpmlp(x2, w1a, w1b, pe8, w1, b1, w2p):
    B, _, ncb, _ = x2.shape
    full = lambda a: pl.BlockSpec(a.shape, lambda b: (0,) * a.ndim)
    return pl.pallas_call(
        _cmpmlp_kernel,
        out_shape=jax.ShapeDtypeStruct((B, NSA_KV_GROUPS, ncb, LANES), BF16),
        grid=(B,),
        in_specs=[pl.BlockSpec((1, CMP_STRIDE, ncb, LANES), lambda b: (b, 0, 0, 0)),
                  full(w1a), full(w1b), full(pe8), full(w1), full(b1), full(w2p)],
        out_specs=pl.BlockSpec((1, NSA_KV_GROUPS, ncb, LANES), lambda b: (b, 0, 0, 0)),
        compiler_params=_cparams(("arbitrary",)),
        name="cmpmlp",
    )(x2, w1a, w1b, pe8, w1, b1, w2p)


def _rank_before(score, nrows):
    jrow = lax.broadcasted_iota(jnp.int32, score.shape, 0)
    rank = jnp.zeros(score.shape, F32)
    for k in range(nrows):
        rk = score[k:k + 1]
        beats = (rk > score) | ((rk == score) & (jrow > k))
        rank = rank + jnp.where(beats, 1.0, 0.0)
    return rank


def _topk_rows(score, k):
    n = score.shape[0]
    row = lax.broadcasted_iota(jnp.int32, score.shape, 0).astype(F32)
    keep = jnp.zeros(score.shape, F32)
    for _ in range(k):
        top = jnp.max(score, axis=0, keepdims=True)
        first = jnp.min(jnp.where(score == top, row, float(n)), axis=0, keepdims=True)
        pick = row == first
        keep = jnp.where(pick, 1.0, keep)
        score = jnp.where(pick, -jnp.inf, score)
    return keep


def _transpose_small_ints(xt):
    c = xt.shape[1]
    eye = (lax.broadcasted_iota(jnp.int32, (c, c), 0) == lax.broadcasted_iota(jnp.int32, (c, c), 1)).astype(BF16)
    return lax.dot_general(eye, xt.astype(BF16), NT, preferred_element_type=F32)


def _pack_heads(heads):
    low = lax.broadcasted_iota(jnp.int32, heads[0].shape, 1) < NSA_HD
    return jnp.concatenate([jnp.where(low, heads[k], pltpu.roll(heads[k + 1], NSA_HD, 1))
                            for k in range(0, len(heads), 2)], axis=1)


def _cmpsel_kernel(q_ref, kc_ref, vc_ref, gate_ref, o_ref, sel_ref):
    tq = q_ref.shape[1]
    ncb = kc_ref.shape[2]
    t = pl.program_id(2)
    kc = kc_ref[0, 0]
    vc = vc_ref[0, 0]
    pos = t * tq + lax.broadcasted_iota(jnp.int32, (tq, 1), 0)
    cblk = lax.broadcasted_iota(jnp.int32, (1, ncb), 1)
    vis = (cblk * CMP_STRIDE + CMP_BLOCK - 1) <= pos
    any_vis = pos >= CMP_BLOCK - 1
    psum = jnp.zeros((tq, ncb), F32)
    gate = gate_ref[0]
    heads = []
    for h in range(NSA_HG):
        s = lax.dot_general(q_ref[0, :, h * LANES:(h + 1) * LANES], kc, NT, preferred_element_type=F32)
        s = jnp.where(vis, s, NEG)
        p = jnp.exp2(s - jnp.max(s, axis=-1, keepdims=True))
        p = p * jnp.where(any_vis, 1.0 / jnp.sum(p, axis=-1, keepdims=True), 0.0)
        heads.append(jnp.dot(p.astype(BF16), vc, preferred_element_type=F32) * gate[:, 3 * h:3 * h + 1])
        psum = psum + p
    o_ref[0] = _pack_heads(heads)
    nsb = ncb * CMP_STRIDE // SEL_BLOCK
    jb = lax.broadcasted_iota(jnp.int32, (nsb, ncb), 0) * SEL_BLOCK
    cb = lax.broadcasted_iota(jnp.int32, (nsb, ncb), 1) * CMP_STRIDE
    ov = jnp.maximum(jnp.minimum(cb + CMP_BLOCK, jb + SEL_BLOCK) - jnp.maximum(cb, jb), 0).astype(F32) / CMP_BLOCK
    ps_hi = psum.astype(BF16)
    ps_lo = (psum - ps_hi.astype(F32)).astype(BF16)
    ovb = ov.astype(BF16)
    pslc_t = (lax.dot_general(ovb, ps_hi, NT, preferred_element_type=F32)
              + lax.dot_general(ovb, ps_lo, NT, preferred_element_type=F32))
    posl = t * tq + lax.broadcasted_iota(jnp.int32, (1, tq), 1)
    cur = posl // SEL_BLOCK
    jrow = lax.broadcasted_iota(jnp.int32, (nsb, tq), 0)
    forced = (jrow == 0) | (jrow == cur) | (jrow == cur - 1)
    score = jnp.where(forced, 1e30, jnp.where(jrow <= cur, pslc_t, NEG))
    chosen = _topk_rows(score, min(N_SEL, nsb)) > 0.5
    drop = jnp.where(chosen & (score > -1e29), 0.0, 1.0)
    drop = jnp.concatenate([drop, jnp.zeros((LANES - nsb, tq), F32)], axis=0)
    sel_ref[0, 0] = (_transpose_small_ints(drop) * NEG).astype(BF16)


def _cmpsel(qraw, kcmp, vcmp, gate, tq):
    B, S, _ = qraw.shape
    ncb = kcmp.shape[2]
    gw = NSA_HG * LANES
    assert S // SEL_BLOCK <= NSA_HD
    cmp_spec = pl.BlockSpec((1, 1, ncb, LANES), lambda b, g, t: (b, g, 0, 0))
    return pl.pallas_call(
        _cmpsel_kernel,
        out_shape=(jax.ShapeDtypeStruct((B, S, NSA_WIDTH), F32),
                   jax.ShapeDtypeStruct((B, NSA_KV_GROUPS, S, LANES), BF16)),
        grid=(B, NSA_KV_GROUPS, S // tq),
        in_specs=[pl.BlockSpec((1, tq, gw), lambda b, g, t: (b, t, g)), cmp_spec, cmp_spec,
                  pl.BlockSpec((1, tq, LANES), lambda b, g, t: (b, t, g))],
        out_specs=(pl.BlockSpec((1, tq, NSA_WIDTH // NSA_KV_GROUPS), lambda b, g, t: (b, t, g)),
                   pl.BlockSpec((1, 1, tq, LANES), lambda b, g, t: (b, g, t, 0))),
        compiler_params=_cparams(("arbitrary", "arbitrary", "arbitrary")),
        name="cmpsel",
    )(qraw, kcmp, vcmp, gate)


def _slcwin_kernel(q_ref, sel_ref, ks_ref, kw_ref, vs_ref, vw_ref, gate_ref, ocmp_ref, wgu_ref, wdn_ref,
                   o_ref, wgu_bf_ref, wdn_bf_ref, *, tk, wk):
    wgu_bf_ref[...] = wgu_ref[...].astype(BF16)
    wdn_bf_ref[...] = wdn_ref[...].astype(BF16)
    tq = q_ref.shape[1]
    S = ks_ref.shape[1]
    t = pl.program_id(2)
    q0 = t * tq
    qpos = q0 + lax.broadcasted_iota(jnp.int32, (tq, 1), 0)
    qw = [q_ref[0, :, h * LANES:(h + 1) * LANES] for h in range(NSA_HG)]
    sel = sel_ref[0, 0]
    qa = [jnp.concatenate([q, sel], axis=1) for q in qw]

    gate = gate_ref[0]

    def normalise(acc, g):
        return acc * (g / acc[:, NSA_HD:NSA_HD + 1])

    nchain = NSA_HG // SLC_STACK
    qst = [jnp.concatenate(qa[c * SLC_STACK:(c + 1) * SLC_STACK], axis=0) for c in range(nchain)]
    qpos_st = jnp.concatenate([qpos] * SLC_STACK, axis=0)

    def kstep(j, carry, diag):
        k0 = pl.multiple_of(j * tk, tk)
        ks = ks_ref[0, pl.ds(k0, tk), :]
        vs = vs_ref[0, pl.ds(k0, tk), :]
        if diag:
            vis = (k0 + lax.broadcasted_iota(jnp.int32, (1, tk), 1)) <= qpos_st
        out = []
        for c in range(nchain):
            m, acc = carry[c]
            s = lax.dot_general(qst[c], ks, NT, preferred_element_type=F32)
            if diag:
                s = jnp.where(vis, s, NEG)
            mn = jnp.maximum(m, jnp.max(s, axis=-1, keepdims=True))
            p = jnp.exp2(s - mn)
            acc = jnp.exp2(m - mn) * acc + jnp.dot(p.astype(BF16), vs, preferred_element_type=F32)
            out.append((mn, acc))
        return tuple(out)

    init = tuple((jnp.full((SLC_STACK * tq, 1), NEG, F32), jnp.zeros((SLC_STACK * tq, LANES), F32))
                 for _ in range(nchain))
    jdiag = q0 // tk
    carry = lax.fori_loop(0, jdiag, lambda j, c: kstep(j, c, False), init)
    carry = kstep(jdiag, carry, True)
    heads = [normalise(carry[h // SLC_STACK][1][(h % SLC_STACK) * tq:(h % SLC_STACK + 1) * tq],
                       gate[:, 3 * h + 1:3 * h + 2]) for h in range(NSA_HG)]

    wq = wk - WINDOW
    wins = [[] for _ in range(NSA_HG)]
    spos = lax.broadcasted_iota(jnp.int32, (NSA_HG * wq, 1), 0) % wq
    for i in range(tq // wq):
        rows = slice(i * wq, (i + 1) * wq)
        start = pl.multiple_of(jnp.maximum(q0 + (i + 1) * wq - wk, 0), wq)
        kw = kw_ref[0, pl.ds(start, wk), :]
        vw = vw_ref[0, pl.ds(start, wk), :]
        d = (q0 + i * wq + spos) - (start + lax.broadcasted_iota(jnp.int32, (1, wk), 1))
        q_st = jnp.concatenate([q[rows] for q in qw], axis=0)
        s = jnp.where((d >= 0) & (d < WINDOW), lax.dot_general(q_st, kw, NT, preferred_element_type=F32), NEG)
        p = jnp.exp2(s - jnp.max(s, axis=-1, keepdims=True))
        o_st = jnp.dot(p.astype(BF16), vw, preferred_element_type=F32)
        for h in range(NSA_HG):
            wins[h].append(o_st[h * wq:(h + 1) * wq])
    for h in range(NSA_HG):
        heads[h] = heads[h] + normalise(jnp.concatenate(wins[h], axis=0), gate[:, 3 * h + 2:3 * h + 3])
    o_ref[0] = (ocmp_ref[0] + _pack_heads(heads)).astype(BF16)


def _slcwin(qrot, sel, ks, kw, vaug, gate, ocmp, wgu, wdn, tq, tk):
    B, S, _ = qrot.shape
    gw = NSA_HG * LANES
    wk = WINDOW + WIN_Q
    nt = S // tq
    nstep = B * NSA_KV_GROUPS * nt
    assert S >= wk and tq % WIN_Q == 0 and N_EXPERTS % nstep == 0
    epb = N_EXPERTS // nstep
    ospec = pl.BlockSpec((1, tq, NSA_WIDTH // NSA_KV_GROUPS), lambda b, g, t: (b, t, g))
    wspec = lambda a: pl.BlockSpec((epb,) + a.shape[1:], lambda b, g, t: ((b * NSA_KV_GROUPS + g) * nt + t, 0, 0))
    return pl.pallas_call(
        functools.partial(_slcwin_kernel, tk=tk, wk=wk),
        out_shape=(jax.ShapeDtypeStruct((B, S, NSA_WIDTH), BF16),
                   jax.ShapeDtypeStruct(wgu.shape, BF16), jax.ShapeDtypeStruct(wdn.shape, BF16)),
        grid=(B, NSA_KV_GROUPS, nt),
        in_specs=[pl.BlockSpec((1, tq, gw), lambda b, g, t: (b, t, g)),
                  pl.BlockSpec((1, 1, tq, LANES), lambda b, g, t: (b, g, t, 0)),
                  pl.BlockSpec((1, S, 2 * LANES), lambda b, g, t: (b, 0, 0)),
                  pl.BlockSpec((1, S, LANES), lambda b, g, t: (b, 0, 0)),
                  pl.BlockSpec((1, S, LANES), lambda b, g, t: (b, 0, g)),
                  pl.BlockSpec((1, S, LANES), lambda b, g, t: (b, 0, NSA_KV_GROUPS + g)),
                  pl.BlockSpec((1, tq, LANES), lambda b, g, t: (b, t, g)),
                  ospec, wspec(wgu), wspec(wdn)],
        out_specs=(ospec, wspec(wgu), wspec(wdn)),
        compiler_params=_cparams(("arbitrary", "arbitrary", "arbitrary")),
        name="slcwin",
    )(qrot, sel, ks, kw, vaug, vaug, gate, ocmp, wgu, wdn)


def _outproj_kernel(x_ref, ohg_ref, onsa_ref, wout_ref,
                    g1_ref, sh2_ref, sc2_ref, n2_ref, rwt_ref, rb_ref,
                    x1_ref, h2_ref, sc_ref, slot_t_ref, w_t_ref, cnt_ref):
    tm = x_ref.shape[1]
    mix = jnp.dot(jnp.concatenate([ohg_ref[0], onsa_ref[0]], axis=1), wout_ref[...], preferred_element_type=F32)
    x1 = x_ref[0] + g1_ref[0] * mix
    x1_ref[0] = x1
    h2 = (x1 * lax.rsqrt(jnp.mean(x1 * x1, axis=-1, keepdims=True) + EPS) * n2_ref[...]) * (1.0 + sc2_ref[0]) + sh2_ref[0]
    h_hi = h2.astype(BF16)
    h2_ref[0] = h_hi
    h_lo = (h2 - h_hi.astype(F32)).astype(BF16)
    both = lax.dot_general(rwt_ref[...], h_hi, NT, preferred_element_type=F32)
    logits = (both[:N_EXPERTS] + both[N_EXPERTS:]
              + lax.dot_general(rwt_ref[:N_EXPERTS, :], h_lo, NT, preferred_element_type=F32))
    scores = _sigmoid(logits)
    choice = scores + rb_ref[...]
    per = N_EXPERTS // N_GROUPS
    c3 = choice.reshape(N_GROUPS, per, tm)
    erow = lax.broadcasted_iota(jnp.int32, c3.shape, 1)
    rank_in = jnp.zeros(c3.shape, F32)
    for k in range(per):
        ck = c3[:, k:k + 1, :]
        rank_in = rank_in + jnp.where((ck > c3) | ((ck == c3) & (erow > k)), 1.0, 0.0)
    grp_score = jnp.sum(jnp.where(rank_in < 2, c3, 0.0), axis=1)
    grp_keep = _rank_before(grp_score, N_GROUPS) < TOPK_GROUPS
    masked = jnp.where(grp_keep[:, None, :], c3, -jnp.inf).reshape(N_EXPERTS, tm)
    keep = _topk_rows(masked, TOP_K) > 0.5
    tw = jnp.where(keep, scores, 0.0)
    tw = tw / jnp.sum(tw, axis=0, keepdims=True) * ROUTED_SCALE
    before = lax.broadcasted_iota(jnp.int32, (tm, tm), 0)
    after = lax.broadcasted_iota(jnp.int32, (tm, tm), 1)
    earlier = (before < after) & (before // MOE_SUB == after // MOE_SUB)
    kept = jnp.where(keep, 1.0, 0.0)
    pref = jnp.dot(kept.astype(BF16), earlier.astype(BF16), preferred_element_type=F32)
    slot = jnp.where(keep, pref, -1.0)
    assert MOE_SUB <= 256
    sc_ref[0] = _transpose_small_ints(slot)
    for j in range(tm // MOE_SUB):
        sub = slice(j * MOE_SUB, (j + 1) * MOE_SUB)
        slot_t_ref[:, j] = slot[:, sub].reshape(N_EXPERTS // MOE_EG, MOE_EG, MOE_SUB)
        w_t_ref[:, j] = tw[:, sub].reshape(N_EXPERTS // MOE_EG, MOE_EG, MOE_SUB)
        cnt_ref[j] = jnp.broadcast_to(jnp.sum(kept[:, sub], axis=1, keepdims=True), (N_EXPERTS, LANES))


def _outproj(x, ohg, onsa, wout, mod3, n2, rwt, rb, tm):
    B, S, D = x.shape
    blk = lambda w: pl.BlockSpec((1, tm, w), lambda b, i: (b, i, 0))
    full = lambda a: pl.BlockSpec(a.shape, lambda b, i: (0,) * a.ndim)
    modc = lambda k: pl.BlockSpec((1, 1, D), lambda b, i, k=k: (b, 0, k))
    nt = S // tm
    ng = N_EXPERTS // MOE_EG
    nh = tm // MOE_SUB
    tspec = pl.BlockSpec((ng, nh, MOE_EG, MOE_SUB), lambda b, i: (0, b * nt + i, 0, 0))
    return pl.pallas_call(
        _outproj_kernel,
        out_shape=(jax.ShapeDtypeStruct((B, S, D), F32),
                   jax.ShapeDtypeStruct((B, S, D), BF16),
                   jax.ShapeDtypeStruct((B, S, N_EXPERTS), F32),
                   jax.ShapeDtypeStruct((ng, B * nt * nh, MOE_EG, MOE_SUB), F32),
                   jax.ShapeDtypeStruct((ng, B * nt * nh, MOE_EG, MOE_SUB), F32),
                   jax.ShapeDtypeStruct((B * nt * nh, N_EXPERTS, LANES), F32)),
        grid=(B, nt),
        in_specs=[blk(D), blk(HG_WIDTH), blk(NSA_WIDTH), full(wout),
                  modc(2), modc(3), modc(4), full(n2), full(rwt), full(rb)],
        out_specs=(blk(D), blk(D), blk(N_EXPERTS), tspec, tspec,
                   pl.BlockSpec((nh, N_EXPERTS, LANES), lambda b, i: (b * nt + i, 0, 0))),
        compiler_params=_cparams(("arbitrary", "arbitrary")),
        name="outproj",
    )(x, ohg, onsa, wout, mod3, mod3, mod3, n2, rwt, rb)


def _swiglu_hidden(x, wgu):
    gu = jnp.dot(x, wgu, preferred_element_type=F32)
    return _silu(gu[:, :EXPERT_HIDDEN]) * gu[:, EXPERT_HIDDEN:]


def _moe_kernel(rounds_ref, h_ref, slot_t_ref, w_t_ref, slot_c_ref, wgu_ref, wdn_ref, sgu_ref, sdn_ref,
                o_ref, x_scr, p_scr):
    i = pl.program_id(0)
    g = pl.program_id(1)
    tm = h_ref.shape[0]
    nsub = tm // MOE_SUB
    rnd, eg = MOE_ROUND, MOE_EG
    ng = N_EXPERTS // eg
    base = i * (N_EXPERTS + ng)

    @pl.when(g == 0)
    def _():
        act = _swiglu_hidden(h_ref[...], sgu_ref[...]).astype(BF16)
        o_ref[...] = jnp.dot(act, sdn_ref[...], preferred_element_type=F32)

    blk = eg * rnd
    rslot = lax.broadcasted_iota(jnp.int32, (rnd, 1), 0).astype(F32)

    def gather(first, nround):
        for s in range(nsub):
            st = slot_t_ref[0, s] - first
            p = jnp.concatenate([jnp.where(rslot + float(w * rnd) == st[e:e + 1, :], 1.0, 0.0)
                                 for w in range(nround) for e in range(eg)], axis=0)
            p_scr[s, 0:nround * blk, :] = p.astype(BF16)
            x_scr[s, 0:nround * blk, :] = jnp.dot(p_scr[s, 0:nround * blk, :], h_ref[s * MOE_SUB:(s + 1) * MOE_SUB, :],
                                                  preferred_element_type=F32).astype(BF16)

    def run_expert(e, w):
        start = w * blk + e * rnd
        rows = pl.ds(start if isinstance(e, int) else pl.multiple_of(start, rnd), rnd)
        xe = jnp.concatenate([x_scr[s, rows, :] for s in range(nsub)], axis=0)
        wslot = jnp.concatenate(
            [jnp.sum(p_scr[s, rows, :].astype(F32) * w_t_ref[0, s, pl.ds(e, 1), :], axis=-1, keepdims=True)
             for s in range(nsub)], axis=0)
        y = jnp.dot((_swiglu_hidden(xe, wgu_ref[e]) * wslot).astype(BF16), wdn_ref[e], preferred_element_type=F32)
        for s in range(nsub):
            x_scr[s, rows, :] = y[s * rnd:(s + 1) * rnd].astype(BF16)

    def run_active(r, w):
        def body(e, carry):
            pl.when(rounds_ref[base + g * eg + e] > r)(functools.partial(run_expert, e, w))
            return carry
        lax.fori_loop(0, eg, body, 0)

    def scatter(first, nround):
        lane = lax.broadcasted_iota(jnp.int32, (1, nround * blk), 1)
        lane_slot = ((lane // blk) * rnd + lane % rnd).astype(F32)
        spread = ((lax.broadcasted_iota(jnp.int32, (N_EXPERTS, nround * blk), 1) % blk) // rnd + g * eg
                  == lax.broadcasted_iota(jnp.int32, (N_EXPERTS, nround * blk), 0)).astype(BF16)
        for s in range(nsub):
            toks = slice(s * MOE_SUB, (s + 1) * MOE_SUB)
            sc = jnp.dot(slot_c_ref[toks, :].astype(BF16), spread, preferred_element_type=F32) - first
            pt = jnp.where(sc == lane_slot, 1.0, 0.0).astype(BF16)
            o_ref[toks, :] += jnp.dot(pt, x_scr[s, 0:nround * blk, :], preferred_element_type=F32)

    gather(0.0, MOE_FIRST)
    for e in range(eg):
        run_expert(e, 0)
    for w in range(1, MOE_FIRST):
        run_active(w, w)
    scatter(0.0, MOE_FIRST)

    def later_round(r, carry):
        first = (r * rnd).astype(F32)
        gather(first, 1)
        run_active(r, 0)
        scatter(first, 1)
        return carry

    lax.fori_loop(MOE_FIRST, rounds_ref[base + N_EXPERTS + g], later_round, 0)


def _moe(rounds, h2, slot_t, w_t, slot_c, wgu, wdn, sgu, sdn, tm):
    T, D = h2.shape
    eg, nsub = MOE_EG, tm // MOE_SUB
    full = lambda a: pl.BlockSpec(a.shape, lambda i, e, o: (0,) * a.ndim)
    tspec = pl.BlockSpec((1, nsub, eg, MOE_SUB), lambda i, e, o: (e, i, 0, 0))
    return pl.pallas_call(
        _moe_kernel,
        out_shape=jax.ShapeDtypeStruct((T, D), F32),
        grid_spec=pltpu.PrefetchScalarGridSpec(
            num_scalar_prefetch=1,
            grid=(T // tm, N_EXPERTS // eg),
            in_specs=[pl.BlockSpec((tm, D), lambda i, e, o: (i, 0), pipeline_mode=pl.Buffered(1)), tspec, tspec,
                      pl.BlockSpec((tm, N_EXPERTS), lambda i, e, o: (i, 0)),
                      pl.BlockSpec((eg, D, 2 * EXPERT_HIDDEN), lambda i, e, o: (e, 0, 0)),
                      pl.BlockSpec((eg, EXPERT_HIDDEN, D), lambda i, e, o: (e, 0, 0)),
                      full(sgu), full(sdn)],
            out_specs=pl.BlockSpec((tm, D), lambda i, e, o: (i, 0), pipeline_mode=pl.Buffered(1)),
            scratch_shapes=[pltpu.VMEM((nsub, MOE_FIRST * eg * MOE_ROUND, D), BF16),
                            pltpu.VMEM((nsub, MOE_FIRST * eg * MOE_ROUND, MOE_SUB), BF16)]),
        compiler_params=_cparams(("arbitrary", "arbitrary")),
        name="moe",
    )(rounds, h2, slot_t, w_t, slot_c, wgu, wdn, sgu, sdn)


def _final_kernel(x1_ref, moe_ref, g2_ref, fg_ref, o_ref):
    x2 = x1_ref[0] + g2_ref[0] * moe_ref[0]
    o_ref[0] = x2 * lax.rsqrt(jnp.mean(x2 * x2, axis=-1, keepdims=True) + EPS) * fg_ref[...]


def _final(x1, moe, mod3, fg, tm):
    B, S, D = x1.shape
    blk = pl.BlockSpec((1, tm, D), lambda b, i: (b, i, 0))
    return pl.pallas_call(
        _final_kernel,
        out_shape=jax.ShapeDtypeStruct((B, S, D), F32),
        grid=(B, S // tm),
        in_specs=[blk, blk, pl.BlockSpec((1, 1, D), lambda b, i: (b, 0, 5)), pl.BlockSpec((1, D), lambda b, i: (0, 0))],
        out_specs=blk,
        compiler_params=_cparams(("arbitrary", "arbitrary")),
        name="final",
    )(x1, moe, mod3, fg)


def _split_bf16(w):
    hi = w.astype(BF16)
    return jnp.concatenate([hi, (w - hi.astype(F32)).astype(BF16)], axis=0)


def _pack_w_in(w_in):
    return jnp.pad(w_in, ((0, 0), (0, IN_COLS_P - w_in.shape[1]))).astype(BF16)


def _pack_cmp(pos, w1, b1, w2, lane_by_group):
    half = CMP_STRIDE * NSA_HD
    def rows_for(wh):
        w3 = wh.reshape(CMP_STRIDE, NSA_HD, CMP_HIDDEN)
        z = jnp.zeros_like(w3)
        return jnp.stack([jnp.concatenate([w3, z], axis=1).reshape(CMP_STRIDE * LANES, CMP_HIDDEN),
                          jnp.concatenate([z, w3], axis=1).reshape(CMP_STRIDE * LANES, CMP_HIDDEN)])
    w1a = rows_for(w1[:half]).astype(BF16)
    w1b = rows_for(w1[half:]).astype(BF16)
    z2 = jnp.zeros_like(w2)
    w2_first = jnp.concatenate([w2, z2], axis=1)
    w2p = jnp.stack([w2_first, jnp.concatenate([z2, w2], axis=1) if lane_by_group else w2_first]).astype(BF16)
    pe8 = jnp.pad(pos.reshape(1, CMP_BLOCK * NSA_HD), ((0, 7), (0, 0)))
    return w1a, w1b, pe8, w1, b1.reshape(1, CMP_HIDDEN), w2p


def _rope_tables(S):
    half = NSA_HD // 2
    inv = ROPE_THETA ** (-jnp.arange(half, dtype=F32) / half)
    ang = jnp.arange(S, dtype=F32)[:, None] * inv[None, :]
    cos, sin = jnp.cos(ang), jnp.sin(ang)
    reps = LANES // NSA_HD
    return jnp.tile(jnp.concatenate([cos, cos], axis=1), (1, reps)), jnp.tile(jnp.concatenate([-sin, sin], axis=1), (1, reps))


def _tiles(S):
    return dict(inproj=min(512, S), hgrn=min(512, S), cmpsel=min(512, S), slc_q=512, slc_k=512, moe=min(2048, S), final=min(512, S))


def kernel(x, c, w_ada, b_ada, norm1_g, w_in, hg_lb_logits, hg_norm_g, cmp_pos_k, cmp_w1_k, cmp_b1_k, cmp_w2_k,
           cmp_pos_v, cmp_w1_v, cmp_b1_v, cmp_w2_v, w_out, norm2_g, router_w, router_bias, w_exp_gu, w_exp_dn,
           w_sh_gu, w_sh_dn, final_g):
    B, S, D = x.shape
    assert D == D_MODEL and w_ada.shape[0] == 1 and S % 512 == 0
    tl = _tiles(S)
    l = 0
    lb = jnp.cumsum(jax.nn.softmax(hg_lb_logits.astype(F32), axis=0), axis=0)[l].reshape(1, HG_WIDTH)
    c8 = jnp.pad(c, ((0, 8 - B), (0, 0)))
    mod3 = _ada(c8, w_ada[l], b_ada[l].reshape(1, -1))[:B].reshape(B, 1, 6 * D)
    cos, sin = _rope_tables(S)
    hg, qraw, qrot, kc, vc, ks, kw, vaug, gate = _inproj(x, mod3, norm1_g[l].reshape(1, D), _pack_w_in(w_in[l]),
                                                         cos, sin, tl["inproj"])
    ng = hg_norm_g[l].reshape(1, HG_WIDTH)
    mxu_safe = -HG_SUB * jnp.log(jnp.min(lb)) <= HG_SAFE_LOG_RANGE
    ohg = lax.cond(mxu_safe, lambda: _hgrn_mxu(hg, lb, ng, tl["hgrn"]), lambda: _hgrn(hg, lb, ng))
    ncb = S // CMP_STRIDE
    kcmp = _cmpmlp(kc, *_pack_cmp(cmp_pos_k[l], cmp_w1_k[l], cmp_b1_k[l], cmp_w2_k[l], True))
    vcmp = _cmpmlp(vc, *_pack_cmp(cmp_pos_v[l], cmp_w1_v[l], cmp_b1_v[l], cmp_w2_v[l], False))
    ocmp, sel = _cmpsel(qraw, kcmp, vcmp, gate, tl["cmpsel"])
    onsa, wgu_bf, wdn_bf = _slcwin(qrot, sel, ks, kw, vaug, gate, ocmp, w_exp_gu[l], w_exp_dn[l],
                                   tl["slc_q"], tl["slc_k"])
    x1, h2, sc, slot_t, w_t, cnt = _outproj(x, ohg, onsa, w_out[l].astype(BF16), mod3,
                                            norm2_g[l].reshape(1, D), _split_bf16(router_w[l].T),
                                            router_bias[l].reshape(N_EXPERTS, 1), ROUTE_TM)
    T = B * S
    tm = tl["moe"]
    ng = N_EXPERTS // MOE_EG
    slot_c = sc.reshape(T, N_EXPERTS)
    load = jnp.max(cnt[:, :, 0].reshape(T // tm, tm // MOE_SUB, N_EXPERTS), axis=1).astype(jnp.int32)
    per_expert = (load + MOE_ROUND - 1) // MOE_ROUND
    rounds = jnp.concatenate([per_expert, jnp.max(per_expert.reshape(T // tm, ng, MOE_EG), axis=2)], axis=1).reshape(-1)
    moe = _moe(rounds, h2.reshape(T, D), slot_t, w_t, slot_c,
               wgu_bf, wdn_bf, w_sh_gu[l].astype(BF16), w_sh_dn[l].astype(BF16), tm)
    return _final(x1, moe.reshape(B, S, D), mod3, final_g.reshape(1, D), tl["final"])
```

```python
import functools

import jax
import jax.numpy as jnp
from jax import lax
from jax.experimental import pallas as pl
from jax.experimental.pallas import tpu as pltpu

F32 = jnp.float32
BF16 = jnp.bfloat16
HIGHEST = lax.Precision.HIGHEST

D_MODEL = 1024
EPS = 1e-6
HG_HEADS = 4
HG_DK = 128
HG_DV = 128
HG_WIDTH = HG_HEADS * HG_DV
HG_CHUNK = 64
HG_SUB = 16
HG_SAFE_LOG_RANGE = 80.0
NSA_HEADS = 8
NSA_KV_GROUPS = 2
NSA_HG = NSA_HEADS // NSA_KV_GROUPS
NSA_HD = 64
NSA_WIDTH = NSA_HEADS * NSA_HD
NSA_KV = NSA_KV_GROUPS * NSA_HD
CMP_BLOCK = 32
CMP_STRIDE = 16
CMP_HIDDEN = 256
SEL_BLOCK = 64
N_SEL = 16
WINDOW = 512
ROPE_THETA = 10000.0
SLC_STACK = 4
WIN_Q = 256
N_EXPERTS = 64
TOP_K = 8
N_GROUPS = 8
TOPK_GROUPS = 4
EXPERT_HIDDEN = 256
ROUTED_SCALE = 2.5
ROUTE_TM = 512
MOE_SUB = 256
MOE_ROUND = 32
MOE_EG = 8
MOE_FIRST = 2

LANES = 128
NEG = -1e30
VMEM_LIMIT = 56 * 1024 * 1024

C_HG = 0
C_Q = 4 * HG_WIDTH
C_K = C_Q + NSA_WIDTH
C_GATE = C_K + 6 * NSA_KV
IN_COLS_P = C_GATE + LANES
LOG2E = 1.4426950408889634

NT = (((1,), (1,)), ((), ()))
TN = (((0,), (0,)), ((), ()))


def _cparams(sem):
    return pltpu.CompilerParams(dimension_semantics=sem, vmem_limit_bytes=VMEM_LIMIT)


def _sigmoid(x):
    return 1.0 / (1.0 + jnp.exp(-x))


def _silu(x):
    return x * _sigmoid(x)


def _gelu_tanh(x):
    return 0.5 * x * (1.0 + jnp.tanh(0.7978845608028654 * (x + 0.044715 * (x * x * x))))


def _dot3(a, w):
    n = a.shape[0]
    a_hi = a.astype(BF16)
    a_lo = (a - a_hi.astype(F32)).astype(BF16)
    w_hi = w.astype(BF16)
    w_lo = (w - w_hi.astype(F32)).astype(BF16)
    both = jnp.dot(jnp.concatenate([a_hi, a_lo], axis=0), w_hi, preferred_element_type=F32)
    return both[:n] + both[n:] + jnp.dot(a_hi, w_lo, preferred_element_type=F32)


def _ada_kernel(c_ref, w_ref, b_ref, o_ref):
    o_ref[...] = _dot3(_silu(c_ref[...]), w_ref[...]) + b_ref[...]


def _ada(c8, w, b):
    n = w.shape[1]
    tn = 1024
    return pl.pallas_call(
        _ada_kernel,
        out_shape=jax.ShapeDtypeStruct((8, n), F32),
        grid=(n // tn,),
        in_specs=[pl.BlockSpec((8, D_MODEL), lambda j: (0, 0)),
                  pl.BlockSpec((D_MODEL, tn), lambda j: (0, j)),
                  pl.BlockSpec((1, tn), lambda j: (0, j))],
        out_specs=pl.BlockSpec((8, tn), lambda j: (0, j)),
        compiler_params=_cparams(("arbitrary",)),
        name="ada",
    )(c8, w, b)


def _rope(t, cos, sin_signed, first_half):
    rot = jnp.where(first_half, pltpu.roll(t, 96, 1), pltpu.roll(t, 32, 1))
    return t * cos + rot * sin_signed


def _inproj_kernel(x_ref, sh_ref, sc_ref, g_ref, w_ref, cos_ref, sin_ref,
                   hg_ref, qraw_ref, qrot_ref, kc_ref, vc_ref, ks_ref, kw_ref, va_ref, gate_ref, h_scr, w_scr):
    tm = x_ref.shape[1]

    @pl.when((pl.program_id(0) == 0) & (pl.program_id(1) == 0))
    def _():
        for lo in range(0, C_GATE, 2 * LANES):
            w_scr[:, lo:lo + 2 * LANES] = w_ref[:, lo:lo + 2 * LANES].astype(BF16)
        w_scr[:, C_GATE:] = jnp.zeros((w_scr.shape[0], LANES), BF16)
        w_scr[:, C_GATE:w_ref.shape[1]] = w_ref[:, C_GATE:].astype(BF16)

    x = x_ref[0]
    y = x * lax.rsqrt(jnp.mean(x * x, axis=-1, keepdims=True) + EPS) * g_ref[...]
    h_scr[...] = (y * (1.0 + sc_ref[0]) + sh_ref[0]).astype(BF16)

    def mm(lo, width):
        return jnp.dot(h_scr[...], w_scr[:, lo:lo + width], preferred_element_type=F32)

    cos = cos_ref[...]
    sin = sin_ref[...]
    first_half = (lax.broadcasted_iota(jnp.int32, cos.shape, 1) % NSA_HD) < (NSA_HD // 2)
    lane = lax.broadcasted_iota(jnp.int32, (tm, LANES), 1)
    half_of_lane = lane // NSA_HD

    def own_lanes(t, src_half, dst_half):
        moved = t if src_half == dst_half else pltpu.roll(t, NSA_HD, 1)
        return jnp.where(half_of_lane == dst_half, moved, 0.0)

    for j in range(4):
        hg_ref[0, :, j * HG_WIDTH:(j + 1) * HG_WIDTH] = mm(C_HG + j * HG_WIDTH, HG_WIDTH)
    for m in range(NSA_WIDTH // (2 * LANES)):
        qq = mm(C_Q + m * 2 * LANES, 2 * LANES) * (NSA_HD ** -0.5 * LOG2E)
        for c in range(2):
            q = qq[:, c * LANES:(c + 1) * LANES]
            qr = _rope(q, cos, sin, first_half)
            for half in range(2):
                n = (2 * m + c) * 2 + half
                g = n // NSA_HG
                qraw_ref[0, :, n * LANES:(n + 1) * LANES] = own_lanes(q, half, g).astype(BF16)
                qrot_ref[0, :, n * LANES:(n + 1) * LANES] = own_lanes(qr, half, g).astype(BF16)
    kv = mm(C_K, 2 * LANES)
    kc_ref[0] = kv[:, :LANES].astype(BF16)
    vc_ref[0] = kv[:, LANES:].astype(BF16)
    pos = pl.program_id(1) * tm + lax.broadcasted_iota(jnp.int32, (tm, LANES), 0)
    ks_ref[0, :, LANES:2 * LANES] = jnp.where(lane == pos // SEL_BLOCK, 1.0, 0.0).astype(BF16)
    for j in range(2):
        kv = mm(C_K + (j + 1) * 2 * LANES, 2 * LANES)
        k_rot = _rope(kv[:, :LANES], cos, sin, first_half).astype(BF16)
        if j == 0:
            ks_ref[0, :, 0:LANES] = k_rot
        else:
            kw_ref[0] = k_rot
        for g in range(NSA_KV_GROUPS):
            v = own_lanes(kv[:, LANES:], g, 0)
            col = (j * NSA_KV_GROUPS + g) * LANES
            va_ref[0, :, col:col + LANES] = jnp.where(lane == NSA_HD, 1.0, v).astype(BF16)
    gate = _sigmoid(mm(C_GATE, LANES))
    per_group = 3 * NSA_HG
    for g in range(NSA_KV_GROUPS):
        gate_ref[0, :, g * LANES:(g + 1) * LANES] = gate if g == 0 else pltpu.roll(gate, LANES - g * per_group, 1)


def _inproj(x, mod3, norm_g, w_in, cos, sin, tm):
    B, S, D = x.shape
    blk = lambda w: pl.BlockSpec((1, tm, w), lambda b, i: (b, i, 0))
    return pl.pallas_call(
        _inproj_kernel,
        out_shape=(jax.ShapeDtypeStruct((B, S, 4 * HG_WIDTH), F32),
                   jax.ShapeDtypeStruct((B, S, NSA_HEADS * LANES), BF16),
                   jax.ShapeDtypeStruct((B, S, NSA_HEADS * LANES), BF16),
                   jax.ShapeDtypeStruct((B, S, LANES), BF16),
                   jax.ShapeDtypeStruct((B, S, LANES), BF16),
                   jax.ShapeDtypeStruct((B, S, 2 * LANES), BF16),
                   jax.ShapeDtypeStruct((B, S, LANES), BF16),
                   jax.ShapeDtypeStruct((B, S, 4 * LANES), BF16),
                   jax.ShapeDtypeStruct((B, S, NSA_KV_GROUPS * LANES), F32)),
        grid=(B, S // tm),
        in_specs=[blk(D),
                  pl.BlockSpec((1, 1, D), lambda b, i: (b, 0, 0)),
                  pl.BlockSpec((1, 1, D), lambda b, i: (b, 0, 1)),
                  pl.BlockSpec((1, D), lambda b, i: (0, 0)),
                  pl.BlockSpec(w_in.shape, lambda b, i: (0, 0), pipeline_mode=pl.Buffered(1)),
                  pl.BlockSpec((tm, LANES), lambda b, i: (i, 0)),
                  pl.BlockSpec((tm, LANES), lambda b, i: (i, 0))],
        out_specs=(blk(4 * HG_WIDTH), blk(NSA_HEADS * LANES), blk(NSA_HEADS * LANES),
                   blk(LANES), blk(LANES), blk(2 * LANES), blk(LANES), blk(4 * LANES), blk(NSA_KV_GROUPS * LANES)),
        scratch_shapes=[pltpu.VMEM((tm, D), BF16), pltpu.VMEM((D, IN_COLS_P), BF16)],
        compiler_params=_cparams(("arbitrary", "arbitrary")),
        name="inproj",
    )(x, mod3, mod3, norm_g, w_in, cos, sin)


def _hgrn_kernel(q_ref, f_ref, i_ref, gt_ref, lb_ref, ng_ref, o_ref):
    S = q_ref.shape[1]
    C, U = HG_CHUNK, HG_SUB
    lb = lb_ref[...]
    ng = ng_ref[...]
    ri = lax.broadcasted_iota(jnp.int32, (C, C), 0)
    ci = lax.broadcasted_iota(jnp.int32, (C, C), 1)
    tril = (ri >= ci).astype(F32)
    trow = lax.broadcasted_iota(jnp.int32, (U, 1), 0)

    def chunk(c, st_t):
        r0 = pl.multiple_of(c * C, C)
        rows = pl.ds(r0, C)
        f = lb + (1.0 - lb) * _sigmoid(f_ref[0, rows, :])
        kk = 1.0 - f
        b = jnp.dot(tril, jnp.log(f), precision=HIGHEST, preferred_element_type=F32)
        q = q_ref[0, rows, :] * (HG_DK ** -0.5)
        v = i_ref[0, rows, :]
        vb = v.astype(BF16)
        o_inter = lax.dot_general((q * jnp.exp(b)).astype(BF16), st_t.astype(BF16), NT,
                                  preferred_element_type=F32)
        parts = []
        for i in range(C // U):
            lo = i * U
            bi = b[lo:lo + U]
            qi = q[lo:lo + U]
            if i == 0:
                oi = jnp.zeros((U, HG_DV), F32)
            else:
                r = b[lo - 1:lo]
                qrel = (qi * jnp.exp(bi - r)).astype(BF16)
                kprev = (kk[:lo] * jnp.exp(r - b[:lo])).astype(BF16)
                a_off = lax.dot_general(qrel, kprev, NT, preferred_element_type=F32)
                oi = jnp.dot(a_off.astype(BF16), vb[:lo], preferred_element_type=F32)
            for s in range(U):
                valid = trow >= s
                e = jnp.exp(jnp.where(valid, bi - bi[s:s + 1], 0.0))
                a = jnp.sum(qi * e * kk[lo + s:lo + s + 1], axis=-1, keepdims=True)
                oi = oi + jnp.where(valid, a, 0.0) * v[lo + s:lo + s + 1]
            parts.append(oi)
        o = o_inter + jnp.concatenate(parts, axis=0)
        o = o * lax.rsqrt(jnp.mean(o * o, axis=-1, keepdims=True) + EPS) * ng
        o_ref[0, rows, :] = (o * _silu(gt_ref[0, rows, :])).astype(BF16)
        bl = b[C - 1:C]
        kv_t = lax.dot_general(vb, (kk * jnp.exp(bl - b)).astype(BF16), TN, preferred_element_type=F32)
        return jnp.exp(bl) * st_t + kv_t

    lax.fori_loop(0, S // C, chunk, jnp.zeros((HG_DV, HG_DK), F32))


def _cumsum_rows(x):
    n = x.shape[0]
    row = lax.broadcasted_iota(jnp.int32, x.shape, 0)
    d = 1
    while d < n:
        x = x + jnp.where(row >= d, pltpu.roll(x, d, 0), 0.0)
        d *= 2
    return x


def _hgrn_mxu_kernel(hg_ref, lb_ref, ng_ref, o_ref, st_scr):
    ts = hg_ref.shape[1]
    C, U = HG_CHUNK, HG_SUB

    @pl.when(pl.program_id(1) == 0)
    def _():
        st_scr[...] = jnp.zeros(st_scr.shape, F32)

    W = HG_WIDTH
    NH = HG_HEADS
    head_of_lane = lax.broadcasted_iota(jnp.int32, (1, W), 1) // HG_DK
    hcols = [slice(h * HG_DK, (h + 1) * HG_DK) for h in range(NH)]

    def chunk(c, carry):
        rows = pl.ds(pl.multiple_of(c * C, C), C)
        lb = lb_ref[...]
        f = lb + (1.0 - lb) * _sigmoid(hg_ref[0, rows, W:2 * W])
        kk = 1.0 - f
        b = _cumsum_rows(jnp.log(f))
        q = hg_ref[0, rows, 0:W] * (HG_DK ** -0.5)
        vb = hg_ref[0, rows, 2 * W:3 * W].astype(BF16)
        qe = (q * jnp.exp(b)).astype(BF16)
        o_inter = jnp.concatenate(
            [lax.dot_general(qe[:, hc], st_scr[h].astype(BF16), NT, preferred_element_type=F32)
             for h, hc in enumerate(hcols)], axis=1)
        parts = []
        for i in range(C // U):
            lo, hi = i * U, (i + 1) * U
            r = b[lo - 1:lo] if i else jnp.zeros((1, W), F32)
            qrel = q[lo:hi] * jnp.exp(b[lo:hi] - r)
            kall = (kk[:hi] * jnp.exp(r - b[:hi])).astype(BF16)
            qbd = jnp.concatenate([jnp.where(head_of_lane == h, qrel, 0.0) for h in range(NH)], axis=0)
            a = lax.dot_general(qbd.astype(BF16), kall, NT, preferred_element_type=F32)
            trow = lax.broadcasted_iota(jnp.int32, (NH * U, hi), 0) % U
            a = jnp.where(lax.broadcasted_iota(jnp.int32, (NH * U, hi), 1) <= lo + trow, a, 0.0)
            oa = jnp.dot(a.astype(BF16), vb[:hi], preferred_element_type=F32)
            oi = jnp.where(head_of_lane == 0, oa[0:U], 0.0)
            for h in range(1, NH):
                oi = jnp.where(head_of_lane == h, oa[h * U:(h + 1) * U], oi)
            parts.append(oi)
        o = o_inter + jnp.concatenate(parts, axis=0)
        o = jnp.concatenate(
            [o[:, hc] * lax.rsqrt(jnp.mean(o[:, hc] * o[:, hc], axis=-1, keepdims=True) + EPS) for hc in hcols], axis=1)
        o_ref[0, rows, :] = (o * ng_ref[...] * _silu(hg_ref[0, rows, 3 * W:4 * W])).astype(BF16)
        bl = b[C - 1:C]
        ke = (kk * jnp.exp(bl - b)).astype(BF16)
        decay = jnp.exp(bl)
        for h, hc in enumerate(hcols):
            kv_t = lax.dot_general(vb[:, hc], ke[:, hc], TN, preferred_element_type=F32)
            st_scr[h] = decay[:, hc] * st_scr[h] + kv_t
        return carry

    lax.fori_loop(0, ts // C, chunk, 0, unroll=8)


def _hgrn_mxu(hg, lb, ng, ts):
    B, S, _ = hg.shape
    vec = pl.BlockSpec((1, HG_WIDTH), lambda b, i: (0, 0))
    return pl.pallas_call(
        _hgrn_mxu_kernel,
        out_shape=jax.ShapeDtypeStruct((B, S, HG_WIDTH), BF16),
        grid=(B, S // ts),
        in_specs=[pl.BlockSpec((1, ts, 4 * HG_WIDTH), lambda b, i: (b, i, 0)), vec, vec],
        out_specs=pl.BlockSpec((1, ts, HG_WIDTH), lambda b, i: (b, i, 0)),
        scratch_shapes=[pltpu.VMEM((HG_HEADS, HG_DV, HG_DK), F32)],
        compiler_params=_cparams(("arbitrary", "arbitrary")),
        name="hgrn_mxu",
    )(hg, lb, ng)


def _hgrn(hg, lb, ng):
    B, S, _ = hg.shape
    col = lambda k: pl.BlockSpec((1, S, HG_DK), lambda b, h, k=k: (b, 0, k * HG_HEADS + h))
    vec = pl.BlockSpec((1, HG_DK), lambda b, h: (0, h))
    return pl.pallas_call(
        _hgrn_kernel,
        out_shape=jax.ShapeDtypeStruct((B, S, HG_WIDTH), BF16),
        grid=(B, HG_HEADS),
        in_specs=[col(0), col(1), col(2), col(3), vec, vec],
        out_specs=pl.BlockSpec((1, S, HG_DV), lambda b, h: (b, 0, h)),
        compiler_params=_cparams(("arbitrary", "arbitrary")),
        name="hgrn",
    )(hg, hg, hg, hg, lb, ng)


def _cmpmlp_kernel(x_ref, w1a_ref, w1b_ref, pe_ref, w1_ref, b1_ref, w2_ref, o_ref):
    x = x_ref[0]
    hb = _dot3(pe_ref[...], w1_ref[...])[0:1] + b1_ref[...]
    nrow = x.shape[0]
    for g in range(NSA_KV_GROUPS):
        a = jnp.dot(x, w1a_ref[g], preferred_element_type=F32)
        bm = jnp.dot(x, w1b_ref[g], preferred_element_type=F32)
        hdn = a + pltpu.roll(bm, nrow - 1, 0) + hb
        o_ref[0, g] = jnp.dot(_gelu_tanh(hdn).astype(BF16), w2_ref[g], preferred_element_type=F32).astype(BF16)


def _cmpmlp(x2, w1a, w1b, pe8, w1, b1, w2p):
    B, ncb, width = x2.shape
    full = lambda a: pl.BlockSpec(a.shape, lambda b: (0,) * a.ndim)
    return pl.pallas_call(
        _cmpmlp_kernel,
        out_shape=jax.ShapeDtypeStruct((B, NSA_KV_GROUPS, ncb, LANES), BF16),
        grid=(B,),
        in_specs=[pl.BlockSpec((1, ncb, width), lambda b: (b, 0, 0)),
                  full(w1a), full(w1b), full(pe8), full(w1), full(b1), full(w2p)],
        out_specs=pl.BlockSpec((1, NSA_KV_GROUPS, ncb, LANES), lambda b: (b, 0, 0, 0)),
        compiler_params=_cparams(("arbitrary",)),
        name="cmpmlp",
    )(x2, w1a, w1b, pe8, w1, b1, w2p)


def _rank_before(score, nrows):
    jrow = lax.broadcasted_iota(jnp.int32, score.shape, 0)
    rank = jnp.zeros(score.shape, F32)
    for k in range(nrows):
        rk = score[k:k + 1]
        beats = (rk > score) | ((rk == score) & (jrow > k))
        rank = rank + jnp.where(beats, 1.0, 0.0)
    return rank


def _topk_rows(score, k):
    n = score.shape[0]
    row = lax.broadcasted_iota(jnp.int32, score.shape, 0).astype(F32)
    keep = jnp.zeros(score.shape, F32)
    for _ in range(k):
        top = jnp.max(score, axis=0, keepdims=True)
        first = jnp.min(jnp.where(score == top, row, float(n)), axis=0, keepdims=True)
        pick = row == first
        keep = jnp.where(pick, 1.0, keep)
        score = jnp.where(pick, -jnp.inf, score)
    return keep


def _transpose_small_ints(xt):
    c = xt.shape[1]
    eye = (lax.broadcasted_iota(jnp.int32, (c, c), 0) == lax.broadcasted_iota(jnp.int32, (c, c), 1)).astype(BF16)
    return lax.dot_general(eye, xt.astype(BF16), NT, preferred_element_type=F32)


def _pack_heads(heads):
    low = lax.broadcasted_iota(jnp.int32, heads[0].shape, 1) < NSA_HD
    return jnp.concatenate([jnp.where(low, heads[k], pltpu.roll(heads[k + 1], NSA_HD, 1))
                            for k in range(0, len(heads), 2)], axis=1)


def _cmpsel_kernel(q_ref, kc_ref, vc_ref, gate_ref, o_ref, sel_ref):
    tq = q_ref.shape[1]
    ncb = kc_ref.shape[2]
    t = pl.program_id(2)
    kc = kc_ref[0, 0]
    vc = vc_ref[0, 0]
    pos = t * tq + lax.broadcasted_iota(jnp.int32, (tq, 1), 0)
    cblk = lax.broadcasted_iota(jnp.int32, (1, ncb), 1)
    vis = (cblk * CMP_STRIDE + CMP_BLOCK - 1) <= pos
    any_vis = pos >= CMP_BLOCK - 1
    psum = jnp.zeros((tq, ncb), F32)
    gate = gate_ref[0]
    heads = []
    for h in range(NSA_HG):
        s = lax.dot_general(q_ref[0, :, h * LANES:(h + 1) * LANES], kc, NT, preferred_element_type=F32)
        s = jnp.where(vis, s, NEG)
        p = jnp.exp2(s - jnp.max(s, axis=-1, keepdims=True))
        p = p * jnp.where(any_vis, 1.0 / jnp.sum(p, axis=-1, keepdims=True), 0.0)
        heads.append(jnp.dot(p.astype(BF16), vc, preferred_element_type=F32) * gate[:, 3 * h:3 * h + 1])
        psum = psum + p
    o_ref[0] = _pack_heads(heads)
    nsb = ncb * CMP_STRIDE // SEL_BLOCK
    jb = lax.broadcasted_iota(jnp.int32, (nsb, ncb), 0) * SEL_BLOCK
    cb = lax.broadcasted_iota(jnp.int32, (nsb, ncb), 1) * CMP_STRIDE
    ov = jnp.maximum(jnp.minimum(cb + CMP_BLOCK, jb + SEL_BLOCK) - jnp.maximum(cb, jb), 0).astype(F32) / CMP_BLOCK
    ps_hi = psum.astype(BF16)
    ps_lo = (psum - ps_hi.astype(F32)).astype(BF16)
    ovb = ov.astype(BF16)
    pslc_t = (lax.dot_general(ovb, ps_hi, NT, preferred_element_type=F32)
              + lax.dot_general(ovb, ps_lo, NT, preferred_element_type=F32))
    posl = t * tq + lax.broadcasted_iota(jnp.int32, (1, tq), 1)
    cur = posl // SEL_BLOCK
    jrow = lax.broadcasted_iota(jnp.int32, (nsb, tq), 0)
    forced = (jrow == 0) | (jrow == cur) | (jrow == cur - 1)
    score = jnp.where(forced, 1e30, jnp.where(jrow <= cur, pslc_t, NEG))
    chosen = _topk_rows(score, min(N_SEL, nsb)) > 0.5
    drop = jnp.where(chosen & (score > -1e29), 0.0, 1.0)
    drop = jnp.concatenate([drop, jnp.zeros((LANES - nsb, tq), F32)], axis=0)
    sel_ref[0, 0] = (_transpose_small_ints(drop) * NEG).astype(BF16)


def _cmpsel(qraw, kcmp, vcmp, gate, tq):
    B, S, _ = qraw.shape
    ncb = kcmp.shape[2]
    gw = NSA_HG * LANES
    assert S // SEL_BLOCK <= NSA_HD
    cmp_spec = pl.BlockSpec((1, 1, ncb, LANES), lambda b, g, t: (b, g, 0, 0))
    return pl.pallas_call(
        _cmpsel_kernel,
        out_shape=(jax.ShapeDtypeStruct((B, S, NSA_WIDTH), F32),
                   jax.ShapeDtypeStruct((B, NSA_KV_GROUPS, S, LANES), BF16)),
        grid=(B, NSA_KV_GROUPS, S // tq),
        in_specs=[pl.BlockSpec((1, tq, gw), lambda b, g, t: (b, t, g)), cmp_spec, cmp_spec,
                  pl.BlockSpec((1, tq, LANES), lambda b, g, t: (b, t, g))],
        out_specs=(pl.BlockSpec((1, tq, NSA_WIDTH // NSA_KV_GROUPS), lambda b, g, t: (b, t, g)),
                   pl.BlockSpec((1, 1, tq, LANES), lambda b, g, t: (b, g, t, 0))),
        compiler_params=_cparams(("arbitrary", "arbitrary", "arbitrary")),
        name="cmpsel",
    )(qraw, kcmp, vcmp, gate)


def _slcwin_kernel(q_ref, sel_ref, ks_ref, kw_ref, vs_ref, vw_ref, gate_ref, ocmp_ref, wgu_ref, wdn_ref,
                   o_ref, wgu_bf_ref, wdn_bf_ref, *, tk, wk):
    wgu_bf_ref[...] = wgu_ref[...].astype(BF16)
    wdn_bf_ref[...] = wdn_ref[...].astype(BF16)
    tq = q_ref.shape[1]
    S = ks_ref.shape[1]
    t = pl.program_id(2)
    q0 = t * tq
    qpos = q0 + lax.broadcasted_iota(jnp.int32, (tq, 1), 0)
    qw = [q_ref[0, :, h * LANES:(h + 1) * LANES] for h in range(NSA_HG)]
    sel = sel_ref[0, 0]
    qa = [jnp.concatenate([q, sel], axis=1) for q in qw]

    gate = gate_ref[0]

    def normalise(acc, g):
        return acc * (g / acc[:, NSA_HD:NSA_HD + 1])

    nchain = NSA_HG // SLC_STACK
    qst = [jnp.concatenate(qa[c * SLC_STACK:(c + 1) * SLC_STACK], axis=0) for c in range(nchain)]
    qpos_st = jnp.concatenate([qpos] * SLC_STACK, axis=0)

    def kstep(j, carry, diag):
        k0 = pl.multiple_of(j * tk, tk)
        ks = ks_ref[0, pl.ds(k0, tk), :]
        vs = vs_ref[0, pl.ds(k0, tk), :]
        if diag:
            vis = (k0 + lax.broadcasted_iota(jnp.int32, (1, tk), 1)) <= qpos_st
        out = []
        for c in range(nchain):
            m, acc = carry[c]
            s = lax.dot_general(qst[c], ks, NT, preferred_element_type=F32)
            if diag:
                s = jnp.where(vis, s, NEG)
            mn = jnp.maximum(m, jnp.max(s, axis=-1, keepdims=True))
            p = jnp.exp2(s - mn)
            acc = jnp.exp2(m - mn) * acc + jnp.dot(p.astype(BF16), vs, preferred_element_type=F32)
            out.append((mn, acc))
        return tuple(out)

    init = tuple((jnp.full((SLC_STACK * tq, 1), NEG, F32), jnp.zeros((SLC_STACK * tq, LANES), F32))
                 for _ in range(nchain))
    jdiag = q0 // tk
    carry = lax.fori_loop(0, jdiag, lambda j, c: kstep(j, c, False), init)
    carry = kstep(jdiag, carry, True)
    heads = [normalise(carry[h // SLC_STACK][1][(h % SLC_STACK) * tq:(h % SLC_STACK + 1) * tq],
                       gate[:, 3 * h + 1:3 * h + 2]) for h in range(NSA_HG)]

    wq = wk - WINDOW
    wins = [[] for _ in range(NSA_HG)]
    spos = lax.broadcasted_iota(jnp.int32, (NSA_HG * wq, 1), 0) % wq
    for i in range(tq // wq):
        rows = slice(i * wq, (i + 1) * wq)
        start = pl.multiple_of(jnp.maximum(q0 + (i + 1) * wq - wk, 0), wq)
        kw = kw_ref[0, pl.ds(start, wk), :]
        vw = vw_ref[0, pl.ds(start, wk), :]
        d = (q0 + i * wq + spos) - (start + lax.broadcasted_iota(jnp.int32, (1, wk), 1))
        q_st = jnp.concatenate([q[rows] for q in qw], axis=0)
        s = jnp.where((d >= 0) & (d < WINDOW), lax.dot_general(q_st, kw, NT, preferred_element_type=F32), NEG)
        p = jnp.exp2(s - jnp.max(s, axis=-1, keepdims=True))
        o_st = jnp.dot(p.astype(BF16), vw, preferred_element_type=F32)
        for h in range(NSA_HG):
            wins[h].append(o_st[h * wq:(h + 1) * wq])
    for h in range(NSA_HG):
        heads[h] = heads[h] + normalise(jnp.concatenate(wins[h], axis=0), gate[:, 3 * h + 2:3 * h + 3])
    o_ref[0] = (ocmp_ref[0] + _pack_heads(heads)).astype(BF16)


def _slcwin(qrot, sel, ks, kw, vaug, gate, ocmp, wgu, wdn, tq, tk):
    B, S, _ = qrot.shape
    gw = NSA_HG * LANES
    wk = WINDOW + WIN_Q
    nt = S // tq
    nstep = B * NSA_KV_GROUPS * nt
    assert S >= wk and tq % WIN_Q == 0 and N_EXPERTS % nstep == 0
    epb = N_EXPERTS // nstep
    ospec = pl.BlockSpec((1, tq, NSA_WIDTH // NSA_KV_GROUPS), lambda b, g, t: (b, t, g))
    wspec = lambda a: pl.BlockSpec((epb,) + a.shape[1:], lambda b, g, t: ((b * NSA_KV_GROUPS + g) * nt + t, 0, 0))
    return pl.pallas_call(
        functools.partial(_slcwin_kernel, tk=tk, wk=wk),
        out_shape=(jax.ShapeDtypeStruct((B, S, NSA_WIDTH), BF16),
                   jax.ShapeDtypeStruct(wgu.shape, BF16), jax.ShapeDtypeStruct(wdn.shape, BF16)),
        grid=(B, NSA_KV_GROUPS, nt),
        in_specs=[pl.BlockSpec((1, tq, gw), lambda b, g, t: (b, t, g)),
                  pl.BlockSpec((1, 1, tq, LANES), lambda b, g, t: (b, g, t, 0)),
                  pl.BlockSpec((1, S, 2 * LANES), lambda b, g, t: (b, 0, 0)),
                  pl.BlockSpec((1, S, LANES), lambda b, g, t: (b, 0, 0)),
                  pl.BlockSpec((1, S, LANES), lambda b, g, t: (b, 0, g)),
                  pl.BlockSpec((1, S, LANES), lambda b, g, t: (b, 0, NSA_KV_GROUPS + g)),
                  pl.BlockSpec((1, tq, LANES), lambda b, g, t: (b, t, g)),
                  ospec, wspec(wgu), wspec(wdn)],
        out_specs=(ospec, wspec(wgu), wspec(wdn)),
        compiler_params=_cparams(("arbitrary", "arbitrary", "arbitrary")),
        name="slcwin",
    )(qrot, sel, ks, kw, vaug, vaug, gate, ocmp, wgu, wdn)


def _outproj_kernel(x_ref, ohg_ref, onsa_ref, wout_ref,
                    g1_ref, sh2_ref, sc2_ref, n2_ref, rwt_ref, rb_ref,
                    x1_ref, h2_ref, sc_ref, slot_t_ref, w_t_ref, cnt_ref):
    tm = x_ref.shape[1]
    mix = jnp.dot(jnp.concatenate([ohg_ref[0], onsa_ref[0]], axis=1), wout_ref[...], preferred_element_type=F32)
    x1 = x_ref[0] + g1_ref[0] * mix
    x1_ref[0] = x1
    h2 = (x1 * lax.rsqrt(jnp.mean(x1 * x1, axis=-1, keepdims=True) + EPS) * n2_ref[...]) * (1.0 + sc2_ref[0]) + sh2_ref[0]
    h_hi = h2.astype(BF16)
    h2_ref[0] = h_hi
    h_lo = (h2 - h_hi.astype(F32)).astype(BF16)
    both = lax.dot_general(rwt_ref[...], h_hi, NT, preferred_element_type=F32)
    logits = (both[:N_EXPERTS] + both[N_EXPERTS:]
              + lax.dot_general(rwt_ref[:N_EXPERTS, :], h_lo, NT, preferred_element_type=F32))
    scores = _sigmoid(logits)
    choice = scores + rb_ref[...]
    per = N_EXPERTS // N_GROUPS
    c3 = choice.reshape(N_GROUPS, per, tm)
    erow = lax.broadcasted_iota(jnp.int32, c3.shape, 1)
    rank_in = jnp.zeros(c3.shape, F32)
    for k in range(per):
        ck = c3[:, k:k + 1, :]
        rank_in = rank_in + jnp.where((ck > c3) | ((ck == c3) & (erow > k)), 1.0, 0.0)
    grp_score = jnp.sum(jnp.where(rank_in < 2, c3, 0.0), axis=1)
    grp_keep = _rank_before(grp_score, N_GROUPS) < TOPK_GROUPS
    masked = jnp.where(grp_keep[:, None, :], c3, -jnp.inf).reshape(N_EXPERTS, tm)
    keep = _topk_rows(masked, TOP_K) > 0.5
    tw = jnp.where(keep, scores, 0.0)
    tw = tw / jnp.sum(tw, axis=0, keepdims=True) * ROUTED_SCALE
    before = lax.broadcasted_iota(jnp.int32, (tm, tm), 0)
    after = lax.broadcasted_iota(jnp.int32, (tm, tm), 1)
    earlier = (before < after) & (before // MOE_SUB == after // MOE_SUB)
    kept = jnp.where(keep, 1.0, 0.0)
    pref = jnp.dot(kept.astype(BF16), earlier.astype(BF16), preferred_element_type=F32)
    slot = jnp.where(keep, pref, -1.0)
    assert MOE_SUB <= 256
    sc_ref[0] = _transpose_small_ints(slot)
    for j in range(tm // MOE_SUB):
        sub = slice(j * MOE_SUB, (j + 1) * MOE_SUB)
        slot_t_ref[:, j] = slot[:, sub].reshape(N_EXPERTS // MOE_EG, MOE_EG, MOE_SUB)
        w_t_ref[:, j] = tw[:, sub].reshape(N_EXPERTS // MOE_EG, MOE_EG, MOE_SUB)
        cnt_ref[j] = jnp.broadcast_to(jnp.sum(kept[:, sub], axis=1, keepdims=True), (N_EXPERTS, LANES))


def _outproj(x, ohg, onsa, wout, mod3, n2, rwt, rb, tm):
    B, S, D = x.shape
    blk = lambda w: pl.BlockSpec((1, tm, w), lambda b, i: (b, i, 0))
    full = lambda a: pl.BlockSpec(a.shape, lambda b, i: (0,) * a.ndim)
    modc = lambda k: pl.BlockSpec((1, 1, D), lambda b, i, k=k: (b, 0, k))
    nt = S // tm
    ng = N_EXPERTS // MOE_EG
    nh = tm // MOE_SUB
    tspec = pl.BlockSpec((ng, nh, MOE_EG, MOE_SUB), lambda b, i: (0, b * nt + i, 0, 0))
    return pl.pallas_call(
        _outproj_kernel,
        out_shape=(jax.ShapeDtypeStruct((B, S, D), F32),
                   jax.ShapeDtypeStruct((B, S, D), BF16),
                   jax.ShapeDtypeStruct((B, S, N_EXPERTS), F32),
                   jax.ShapeDtypeStruct((ng, B * nt * nh, MOE_EG, MOE_SUB), F32),
                   jax.ShapeDtypeStruct((ng, B * nt * nh, MOE_EG, MOE_SUB), F32),
                   jax.ShapeDtypeStruct((B * nt * nh, N_EXPERTS, LANES), F32)),
        grid=(B, nt),
        in_specs=[blk(D), blk(HG_WIDTH), blk(NSA_WIDTH), full(wout),
                  modc(2), modc(3), modc(4), full(n2), full(rwt), full(rb)],
        out_specs=(blk(D), blk(D), blk(N_EXPERTS), tspec, tspec,
                   pl.BlockSpec((nh, N_EXPERTS, LANES), lambda b, i: (b * nt + i, 0, 0))),
        compiler_params=_cparams(("arbitrary", "arbitrary")),
        name="outproj",
    )(x, ohg, onsa, wout, mod3, mod3, mod3, n2, rwt, rb)


def _swiglu_hidden(x, wgu):
    gu = jnp.dot(x, wgu, preferred_element_type=F32)
    return _silu(gu[:, :EXPERT_HIDDEN]) * gu[:, EXPERT_HIDDEN:]


def _moe_kernel(rounds_ref, h_ref, slot_t_ref, w_t_ref, slot_c_ref, wgu_ref, wdn_ref, sgu_ref, sdn_ref,
                o_ref, x_scr, p_scr):
    i = pl.program_id(0)
    g = pl.program_id(1)
    tm = h_ref.shape[0]
    nsub = tm // MOE_SUB
    rnd, eg = MOE_ROUND, MOE_EG
    ng = N_EXPERTS // eg
    base = i * (N_EXPERTS + ng)

    @pl.when(g == 0)
    def _():
        act = _swiglu_hidden(h_ref[...], sgu_ref[...]).astype(BF16)
        o_ref[...] = jnp.dot(act, sdn_ref[...], preferred_element_type=F32)

    blk = eg * rnd
    rslot = lax.broadcasted_iota(jnp.int32, (rnd, 1), 0).astype(F32)

    def gather(first, nround):
        for s in range(nsub):
            st = slot_t_ref[0, s] - first
            p = jnp.concatenate([jnp.where(rslot + float(w * rnd) == st[e:e + 1, :], 1.0, 0.0)
                                 for w in range(nround) for e in range(eg)], axis=0)
            p_scr[s, 0:nround * blk, :] = p.astype(BF16)
            x_scr[s, 0:nround * blk, :] = jnp.dot(p_scr[s, 0:nround * blk, :], h_ref[s * MOE_SUB:(s + 1) * MOE_SUB, :],
                                                  preferred_element_type=F32).astype(BF16)

    def run_expert(e, w):
        start = w * blk + e * rnd
        rows = pl.ds(start if isinstance(e, int) else pl.multiple_of(start, rnd), rnd)
        xe = jnp.concatenate([x_scr[s, rows, :] for s in range(nsub)], axis=0)
        wslot = jnp.concatenate(
            [jnp.sum(p_scr[s, rows, :].astype(F32) * w_t_ref[0, s, pl.ds(e, 1), :], axis=-1, keepdims=True)
             for s in range(nsub)], axis=0)
        y = jnp.dot((_swiglu_hidden(xe, wgu_ref[e]) * wslot).astype(BF16), wdn_ref[e], preferred_element_type=F32)
        for s in range(nsub):
            x_scr[s, rows, :] = y[s * rnd:(s + 1) * rnd].astype(BF16)

    def run_active(r, w):
        def body(e, carry):
            pl.when(rounds_ref[base + g * eg + e] > r)(functools.partial(run_expert, e, w))
            return carry
        lax.fori_loop(0, eg, body, 0)

    def scatter(first, nround):
        lane = lax.broadcasted_iota(jnp.int32, (1, nround * blk), 1)
        lane_slot = ((lane // blk) * rnd + lane % rnd).astype(F32)
        spread = ((lax.broadcasted_iota(jnp.int32, (N_EXPERTS, nround * blk), 1) % blk) // rnd + g * eg
                  == lax.broadcasted_iota(jnp.int32, (N_EXPERTS, nround * blk), 0)).astype(BF16)
        for s in range(nsub):
            toks = slice(s * MOE_SUB, (s + 1) * MOE_SUB)
            sc = jnp.dot(slot_c_ref[toks, :].astype(BF16), spread, preferred_element_type=F32) - first
            pt = jnp.where(sc == lane_slot, 1.0, 0.0).astype(BF16)
            o_ref[toks, :] += jnp.dot(pt, x_scr[s, 0:nround * blk, :], preferred_element_type=F32)

    gather(0.0, MOE_FIRST)
    for e in range(eg):
        run_expert(e, 0)
    for w in range(1, MOE_FIRST):
        run_active(w, w)
    scatter(0.0, MOE_FIRST)

    def later_round(r, carry):
        first = (r * rnd).astype(F32)
        gather(first, 1)
        run_active(r, 0)
        scatter(first, 1)
        return carry

    lax.fori_loop(MOE_FIRST, rounds_ref[base + N_EXPERTS + g], later_round, 0)


def _moe(rounds, h2, slot_t, w_t, slot_c, wgu, wdn, sgu, sdn, tm):
    T, D = h2.shape
    eg, nsub = MOE_EG, tm // MOE_SUB
    full = lambda a: pl.BlockSpec(a.shape, lambda i, e, o: (0,) * a.ndim)
    tspec = pl.BlockSpec((1, nsub, eg, MOE_SUB), lambda i, e, o: (e, i, 0, 0))
    return pl.pallas_call(
        _moe_kernel,
        out_shape=jax.ShapeDtypeStruct((T, D), F32),
        grid_spec=pltpu.PrefetchScalarGridSpec(
            num_scalar_prefetch=1,
            grid=(T // tm, N_EXPERTS // eg),
            in_specs=[pl.BlockSpec((tm, D), lambda i, e, o: (i, 0), pipeline_mode=pl.Buffered(1)), tspec, tspec,
                      pl.BlockSpec((tm, N_EXPERTS), lambda i, e, o: (i, 0)),
                      pl.BlockSpec((eg, D, 2 * EXPERT_HIDDEN), lambda i, e, o: (e, 0, 0)),
                      pl.BlockSpec((eg, EXPERT_HIDDEN, D), lambda i, e, o: (e, 0, 0)),
                      full(sgu), full(sdn)],
            out_specs=pl.BlockSpec((tm, D), lambda i, e, o: (i, 0), pipeline_mode=pl.Buffered(1)),
            scratch_shapes=[pltpu.VMEM((nsub, MOE_FIRST * eg * MOE_ROUND, D), BF16),
                            pltpu.VMEM((nsub, MOE_FIRST * eg * MOE_ROUND, MOE_SUB), BF16)]),
        compiler_params=_cparams(("arbitrary", "arbitrary")),
        name="moe",
    )(rounds, h2, slot_t, w_t, slot_c, wgu, wdn, sgu, sdn)


def _final_kernel(x1_ref, moe_ref, g2_ref, fg_ref, o_ref):
    x2 = x1_ref[0] + g2_ref[0] * moe_ref[0]
    o_ref[0] = x2 * lax.rsqrt(jnp.mean(x2 * x2, axis=-1, keepdims=True) + EPS) * fg_ref[...]


def _final(x1, moe, mod3, fg, tm):
    B, S, D = x1.shape
    blk = pl.BlockSpec((1, tm, D), lambda b, i: (b, i, 0))
    return pl.pallas_call(
        _final_kernel,
        out_shape=jax.ShapeDtypeStruct((B, S, D), F32),
        grid=(B, S // tm),
        in_specs=[blk, blk, pl.BlockSpec((1, 1, D), lambda b, i: (b, 0, 5)), pl.BlockSpec((1, D), lambda b, i: (0, 0))],
        out_specs=blk,
        compiler_params=_cparams(("arbitrary", "arbitrary")),
        name="final",
    )(x1, moe, mod3, fg)


def _split_bf16(w):
    hi = w.astype(BF16)
    return jnp.concatenate([hi, (w - hi.astype(F32)).astype(BF16)], axis=0)


def _pack_cmp(pos, w1, b1, w2, lane_by_group):
    half = CMP_STRIDE * NSA_HD
    def rows_for(wh):
        w3 = wh.reshape(CMP_STRIDE, NSA_HD, CMP_HIDDEN)
        z = jnp.zeros_like(w3)
        return jnp.stack([jnp.concatenate([w3, z], axis=1).reshape(CMP_STRIDE * LANES, CMP_HIDDEN),
                          jnp.concatenate([z, w3], axis=1).reshape(CMP_STRIDE * LANES, CMP_HIDDEN)])
    w1a = rows_for(w1[:half]).astype(BF16)
    w1b = rows_for(w1[half:]).astype(BF16)
    z2 = jnp.zeros_like(w2)
    w2_first = jnp.concatenate([w2, z2], axis=1)
    w2p = jnp.stack([w2_first, jnp.concatenate([z2, w2], axis=1) if lane_by_group else w2_first]).astype(BF16)
    pe8 = jnp.pad(pos.reshape(1, CMP_BLOCK * NSA_HD), ((0, 7), (0, 0)))
    return w1a, w1b, pe8, w1, b1.reshape(1, CMP_HIDDEN), w2p


def _rope_tables(S):
    half = NSA_HD // 2
    inv = ROPE_THETA ** (-jnp.arange(half, dtype=F32) / half)
    ang = jnp.arange(S, dtype=F32)[:, None] * inv[None, :]
    cos, sin = jnp.cos(ang), jnp.sin(ang)
    reps = LANES // NSA_HD
    return jnp.tile(jnp.concatenate([cos, cos], axis=1), (1, reps)), jnp.tile(jnp.concatenate([-sin, sin], axis=1), (1, reps))


def _tiles(S):
    return dict(inproj=min(512, S), hgrn=min(512, S), cmpsel=min(512, S), slc_q=512, slc_k=512, moe=min(2048, S), final=min(512, S))


def kernel(x, c, w_ada, b_ada, norm1_g, w_in, hg_lb_logits, hg_norm_g, cmp_pos_k, cmp_w1_k, cmp_b1_k, cmp_w2_k,
           cmp_pos_v, cmp_w1_v, cmp_b1_v, cmp_w2_v, w_out, norm2_g, router_w, router_bias, w_exp_gu, w_exp_dn,
           w_sh_gu, w_sh_dn, final_g):
    B, S, D = x.shape
    assert D == D_MODEL and w_ada.shape[0] == 1 and S % 512 == 0
    tl = _tiles(S)
    l = 0
    lb = jnp.cumsum(jax.nn.softmax(hg_lb_logits.astype(F32), axis=0), axis=0)[l].reshape(1, HG_WIDTH)
    c8 = jnp.pad(c, ((0, 8 - B), (0, 0)))
    mod3 = _ada(c8, w_ada[l], b_ada[l].reshape(1, -1))[:B].reshape(B, 1, 6 * D)
    cos, sin = _rope_tables(S)
    hg, qraw, qrot, kc, vc, ks, kw, vaug, gate = _inproj(x, mod3, norm1_g[l].reshape(1, D), w_in[l],
                                                         cos, sin, tl["inproj"])
    ng = hg_norm_g[l].reshape(1, HG_WIDTH)
    mxu_safe = -HG_SUB * jnp.log(jnp.min(lb)) <= HG_SAFE_LOG_RANGE
    ohg = lax.cond(mxu_safe, lambda: _hgrn_mxu(hg, lb, ng, tl["hgrn"]), lambda: _hgrn(hg, lb, ng))
    ncb = S // CMP_STRIDE
    kcmp = _cmpmlp(kc.reshape(B, ncb, CMP_STRIDE * LANES),
                   *_pack_cmp(cmp_pos_k[l], cmp_w1_k[l], cmp_b1_k[l], cmp_w2_k[l], True))
    vcmp = _cmpmlp(vc.reshape(B, ncb, CMP_STRIDE * LANES),
                   *_pack_cmp(cmp_pos_v[l], cmp_w1_v[l], cmp_b1_v[l], cmp_w2_v[l], False))
    ocmp, sel = _cmpsel(qraw, kcmp, vcmp, gate, tl["cmpsel"])
    onsa, wgu_bf, wdn_bf = _slcwin(qrot, sel, ks, kw, vaug, gate, ocmp, w_exp_gu[l], w_exp_dn[l],
                                   tl["slc_q"], tl["slc_k"])
    x1, h2, sc, slot_t, w_t, cnt = _outproj(x, ohg, onsa, w_out[l].astype(BF16), mod3,
                                            norm2_g[l].reshape(1, D), _split_bf16(router_w[l].T),
                                            router_bias[l].reshape(N_EXPERTS, 1), ROUTE_TM)
    T = B * S
    tm = tl["moe"]
    ng = N_EXPERTS // MOE_EG
    slot_c = sc.reshape(T, N_EXPERTS)
    load = jnp.max(cnt[:, :, 0].reshape(T // tm, tm // MOE_SUB, N_EXPERTS), axis=1).astype(jnp.int32)
    per_expert = (load + MOE_ROUND - 1) // MOE_ROUND
    rounds = jnp.concatenate([per_expert, jnp.max(per_expert.reshape(T // tm, ng, MOE_EG), axis=2)], axis=1).reshape(-1)
    moe = _moe(rounds, h2.reshape(T, D), slot_t, w_t, slot_c,
               wgu_bf, wdn_bf, w_sh_gu[l].astype(BF16), w_sh_dn[l].astype(BF16), tm)
    return _final(x1, moe.reshape(B, S, D), mod3, final_g.reshape(1, D), tl["final"])
```

```python
import functools

import jax
import jax.numpy as jnp
from jax import lax
from jax.experimental import pallas as pl
from jax.experimental.pallas import tpu as pltpu

F32 = jnp.float32
BF16 = jnp.bfloat16
HIGHEST = lax.Precision.HIGHEST

D_MODEL = 1024
EPS = 1e-6
HG_HEADS = 4
HG_DK = 128
HG_DV = 128
HG_WIDTH = HG_HEADS * HG_DV
HG_CHUNK = 64
HG_SUB = 16
HG_SAFE_LOG_RANGE = 80.0
NSA_HEADS = 8
NSA_KV_GROUPS = 2
NSA_HG = NSA_HEADS // NSA_KV_GROUPS
NSA_HD = 64
NSA_WIDTH = NSA_HEADS * NSA_HD
NSA_KV = NSA_KV_GROUPS * NSA_HD
CMP_BLOCK = 32
CMP_STRIDE = 16
CMP_HIDDEN = 256
SEL_BLOCK = 64
N_SEL = 16
WINDOW = 512
ROPE_THETA = 10000.0
SLC_STACK = 4
WIN_Q = 256
N_EXPERTS = 64
TOP_K = 8
N_GROUPS = 8
TOPK_GROUPS = 4
EXPERT_HIDDEN = 256
ROUTED_SCALE = 2.5
ROUTE_TM = 512
MOE_SUB = 256
MOE_ROUND = 64
MOE_EG = 8
MOE_FIRST = 1

LANES = 128
NEG = -1e30
VMEM_LIMIT = 56 * 1024 * 1024

C_HG = 0
C_Q = 4 * HG_WIDTH
C_K = C_Q + NSA_WIDTH
C_GATE = C_K + 6 * NSA_KV
IN_COLS_P = C_GATE + LANES
LOG2E = 1.4426950408889634

NT = (((1,), (1,)), ((), ()))
TN = (((0,), (0,)), ((), ()))


def _cparams(sem):
    return pltpu.CompilerParams(dimension_semantics=sem, vmem_limit_bytes=VMEM_LIMIT)


def _sigmoid(x):
    return 1.0 / (1.0 + jnp.exp(-x))


def _silu(x):
    return x * _sigmoid(x)


def _gelu_tanh(x):
    return 0.5 * x * (1.0 + jnp.tanh(0.7978845608028654 * (x + 0.044715 * (x * x * x))))


def _dot3(a, w):
    n = a.shape[0]
    a_hi = a.astype(BF16)
    a_lo = (a - a_hi.astype(F32)).astype(BF16)
    w_hi = w.astype(BF16)
    w_lo = (w - w_hi.astype(F32)).astype(BF16)
    both = jnp.dot(jnp.concatenate([a_hi, a_lo], axis=0), w_hi, preferred_element_type=F32)
    return both[:n] + both[n:] + jnp.dot(a_hi, w_lo, preferred_element_type=F32)


def _ada_kernel(c_ref, w_ref, b_ref, o_ref):
    o_ref[...] = _dot3(_silu(c_ref[...]), w_ref[...]) + b_ref[...]


def _ada(c8, w, b):
    n = w.shape[1]
    tn = 1024
    return pl.pallas_call(
        _ada_kernel,
        out_shape=jax.ShapeDtypeStruct((8, n), F32),
        grid=(n // tn,),
        in_specs=[pl.BlockSpec((8, D_MODEL), lambda j: (0, 0)),
                  pl.BlockSpec((D_MODEL, tn), lambda j: (0, j)),
                  pl.BlockSpec((1, tn), lambda j: (0, j))],
        out_specs=pl.BlockSpec((8, tn), lambda j: (0, j)),
        compiler_params=_cparams(("arbitrary",)),
        name="ada",
    )(c8, w, b)


def _rope(t, cos, sin_signed, first_half):
    rot = jnp.where(first_half, pltpu.roll(t, 96, 1), pltpu.roll(t, 32, 1))
    return t * cos + rot * sin_signed


def _inproj_kernel(x_ref, sh_ref, sc_ref, g_ref, w_ref, cos_ref, sin_ref,
                   hg_ref, qraw_ref, qrot_ref, kc_ref, vc_ref, ks_ref, kw_ref, va_ref, gate_ref, h_scr):
    tm = x_ref.shape[1]
    x = x_ref[0]
    y = x * lax.rsqrt(jnp.mean(x * x, axis=-1, keepdims=True) + EPS) * g_ref[...]
    h_scr[...] = (y * (1.0 + sc_ref[0]) + sh_ref[0]).astype(BF16)

    def mm(lo, width):
        return jnp.dot(h_scr[...], w_ref[:, lo:lo + width], preferred_element_type=F32)

    cos = cos_ref[...]
    sin = sin_ref[...]
    first_half = (lax.broadcasted_iota(jnp.int32, cos.shape, 1) % NSA_HD) < (NSA_HD // 2)
    lane = lax.broadcasted_iota(jnp.int32, (tm, LANES), 1)
    half_of_lane = lane // NSA_HD

    def own_lanes(t, src_half, dst_half):
        moved = t if src_half == dst_half else pltpu.roll(t, NSA_HD, 1)
        return jnp.where(half_of_lane == dst_half, moved, 0.0)

    for j in range(4):
        hg_ref[0, :, j * HG_WIDTH:(j + 1) * HG_WIDTH] = mm(C_HG + j * HG_WIDTH, HG_WIDTH)
    for m in range(NSA_WIDTH // (2 * LANES)):
        qq = mm(C_Q + m * 2 * LANES, 2 * LANES) * (NSA_HD ** -0.5 * LOG2E)
        for c in range(2):
            q = qq[:, c * LANES:(c + 1) * LANES]
            qr = _rope(q, cos, sin, first_half)
            for half in range(2):
                n = (2 * m + c) * 2 + half
                g = n // NSA_HG
                qraw_ref[0, :, n * LANES:(n + 1) * LANES] = own_lanes(q, half, g).astype(BF16)
                qrot_ref[0, :, n * LANES:(n + 1) * LANES] = own_lanes(qr, half, g).astype(BF16)
    kv = mm(C_K, 2 * LANES)
    kc_ref[0] = kv[:, :LANES].astype(BF16)
    vc_ref[0] = kv[:, LANES:].astype(BF16)
    pos = pl.program_id(1) * tm + lax.broadcasted_iota(jnp.int32, (tm, LANES), 0)
    ks_ref[0, :, LANES:2 * LANES] = jnp.where(lane == pos // SEL_BLOCK, 1.0, 0.0).astype(BF16)
    for j in range(2):
        kv = mm(C_K + (j + 1) * 2 * LANES, 2 * LANES)
        k_rot = _rope(kv[:, :LANES], cos, sin, first_half).astype(BF16)
        if j == 0:
            ks_ref[0, :, 0:LANES] = k_rot
        else:
            kw_ref[0] = k_rot
        for g in range(NSA_KV_GROUPS):
            v = own_lanes(kv[:, LANES:], g, 0)
            col = (j * NSA_KV_GROUPS + g) * LANES
            va_ref[0, :, col:col + LANES] = jnp.where(lane == NSA_HD, 1.0, v).astype(BF16)
    gate = _sigmoid(mm(C_GATE, LANES))
    per_group = 3 * NSA_HG
    for g in range(NSA_KV_GROUPS):
        gate_ref[0, :, g * LANES:(g + 1) * LANES] = gate if g == 0 else pltpu.roll(gate, LANES - g * per_group, 1)


def _inproj(x, mod3, norm_g, w_p, cos, sin, tm):
    B, S, D = x.shape
    blk = lambda w: pl.BlockSpec((1, tm, w), lambda b, i: (b, i, 0))
    return pl.pallas_call(
        _inproj_kernel,
        out_shape=(jax.ShapeDtypeStruct((B, S, 4 * HG_WIDTH), F32),
                   jax.ShapeDtypeStruct((B, S, NSA_HEADS * LANES), BF16),
                   jax.ShapeDtypeStruct((B, S, NSA_HEADS * LANES), BF16),
                   jax.ShapeDtypeStruct((B, S, LANES), BF16),
                   jax.ShapeDtypeStruct((B, S, LANES), BF16),
                   jax.ShapeDtypeStruct((B, S, 2 * LANES), BF16),
                   jax.ShapeDtypeStruct((B, S, LANES), BF16),
                   jax.ShapeDtypeStruct((B, S, 4 * LANES), BF16),
                   jax.ShapeDtypeStruct((B, S, NSA_KV_GROUPS * LANES), F32)),
        grid=(B, S // tm),
        in_specs=[blk(D),
                  pl.BlockSpec((1, 1, D), lambda b, i: (b, 0, 0)),
                  pl.BlockSpec((1, 1, D), lambda b, i: (b, 0, 1)),
                  pl.BlockSpec((1, D), lambda b, i: (0, 0)),
                  pl.BlockSpec((D, IN_COLS_P), lambda b, i: (0, 0)),
                  pl.BlockSpec((tm, LANES), lambda b, i: (i, 0)),
                  pl.BlockSpec((tm, LANES), lambda b, i: (i, 0))],
        out_specs=(blk(4 * HG_WIDTH), blk(NSA_HEADS * LANES), blk(NSA_HEADS * LANES),
                   blk(LANES), blk(LANES), blk(2 * LANES), blk(LANES), blk(4 * LANES), blk(NSA_KV_GROUPS * LANES)),
        scratch_shapes=[pltpu.VMEM((tm, D), BF16)],
        compiler_params=_cparams(("arbitrary", "arbitrary")),
        name="inproj",
    )(x, mod3, mod3, norm_g, w_p, cos, sin)


def _hgrn_kernel(q_ref, f_ref, i_ref, gt_ref, lb_ref, ng_ref, o_ref):
    S = q_ref.shape[1]
    C, U = HG_CHUNK, HG_SUB
    lb = lb_ref[...]
    ng = ng_ref[...]
    ri = lax.broadcasted_iota(jnp.int32, (C, C), 0)
    ci = lax.broadcasted_iota(jnp.int32, (C, C), 1)
    tril = (ri >= ci).astype(F32)
    trow = lax.broadcasted_iota(jnp.int32, (U, 1), 0)

    def chunk(c, st_t):
        r0 = pl.multiple_of(c * C, C)
        rows = pl.ds(r0, C)
        f = lb + (1.0 - lb) * _sigmoid(f_ref[0, rows, :])
        kk = 1.0 - f
        b = jnp.dot(tril, jnp.log(f), precision=HIGHEST, preferred_element_type=F32)
        q = q_ref[0, rows, :] * (HG_DK ** -0.5)
        v = i_ref[0, rows, :]
        vb = v.astype(BF16)
        o_inter = lax.dot_general((q * jnp.exp(b)).astype(BF16), st_t.astype(BF16), NT,
                                  preferred_element_type=F32)
        parts = []
        for i in range(C // U):
            lo = i * U
            bi = b[lo:lo + U]
            qi = q[lo:lo + U]
            if i == 0:
                oi = jnp.zeros((U, HG_DV), F32)
            else:
                r = b[lo - 1:lo]
                qrel = (qi * jnp.exp(bi - r)).astype(BF16)
                kprev = (kk[:lo] * jnp.exp(r - b[:lo])).astype(BF16)
                a_off = lax.dot_general(qrel, kprev, NT, preferred_element_type=F32)
                oi = jnp.dot(a_off.astype(BF16), vb[:lo], preferred_element_type=F32)
            for s in range(U):
                valid = trow >= s
                e = jnp.exp(jnp.where(valid, bi - bi[s:s + 1], 0.0))
                a = jnp.sum(qi * e * kk[lo + s:lo + s + 1], axis=-1, keepdims=True)
                oi = oi + jnp.where(valid, a, 0.0) * v[lo + s:lo + s + 1]
            parts.append(oi)
        o = o_inter + jnp.concatenate(parts, axis=0)
        o = o * lax.rsqrt(jnp.mean(o * o, axis=-1, keepdims=True) + EPS) * ng
        o_ref[0, rows, :] = (o * _silu(gt_ref[0, rows, :])).astype(BF16)
        bl = b[C - 1:C]
        kv_t = lax.dot_general(vb, (kk * jnp.exp(bl - b)).astype(BF16), TN, preferred_element_type=F32)
        return jnp.exp(bl) * st_t + kv_t

    lax.fori_loop(0, S // C, chunk, jnp.zeros((HG_DV, HG_DK), F32))


def _cumsum_rows(x):
    n = x.shape[0]
    row = lax.broadcasted_iota(jnp.int32, x.shape, 0)
    d = 1
    while d < n:
        x = x + jnp.where(row >= d, pltpu.roll(x, d, 0), 0.0)
        d *= 2
    return x


def _hgrn_mxu_kernel(hg_ref, lb_ref, ng_ref, o_ref, st_scr):
    ts = hg_ref.shape[1]
    C, U = HG_CHUNK, HG_SUB

    @pl.when(pl.program_id(1) == 0)
    def _():
        st_scr[...] = jnp.zeros(st_scr.shape, F32)

    W = HG_WIDTH
    NH = HG_HEADS
    head_of_lane = lax.broadcasted_iota(jnp.int32, (1, W), 1) // HG_DK
    hcols = [slice(h * HG_DK, (h + 1) * HG_DK) for h in range(NH)]

    def chunk(c, carry):
        rows = pl.ds(pl.multiple_of(c * C, C), C)
        lb = lb_ref[...]
        f = lb + (1.0 - lb) * _sigmoid(hg_ref[0, rows, W:2 * W])
        kk = 1.0 - f
        b = _cumsum_rows(jnp.log(f))
        q = hg_ref[0, rows, 0:W] * (HG_DK ** -0.5)
        vb = hg_ref[0, rows, 2 * W:3 * W].astype(BF16)
        qe = (q * jnp.exp(b)).astype(BF16)
        o_inter = jnp.concatenate(
            [lax.dot_general(qe[:, hc], st_scr[h].astype(BF16), NT, preferred_element_type=F32)
             for h, hc in enumerate(hcols)], axis=1)
        parts = []
        for i in range(C // U):
            lo, hi = i * U, (i + 1) * U
            r = b[lo - 1:lo] if i else jnp.zeros((1, W), F32)
            qrel = q[lo:hi] * jnp.exp(b[lo:hi] - r)
            kall = (kk[:hi] * jnp.exp(r - b[:hi])).astype(BF16)
            qbd = jnp.concatenate([jnp.where(head_of_lane == h, qrel, 0.0) for h in range(NH)], axis=0)
            a = lax.dot_general(qbd.astype(BF16), kall, NT, preferred_element_type=F32)
            trow = lax.broadcasted_iota(jnp.int32, (NH * U, hi), 0) % U
            a = jnp.where(lax.broadcasted_iota(jnp.int32, (NH * U, hi), 1) <= lo + trow, a, 0.0)
            oa = jnp.dot(a.astype(BF16), vb[:hi], preferred_element_type=F32)
            oi = jnp.where(head_of_lane == 0, oa[0:U], 0.0)
            for h in range(1, NH):
                oi = jnp.where(head_of_lane == h, oa[h * U:(h + 1) * U], oi)
            parts.append(oi)
        o = o_inter + jnp.concatenate(parts, axis=0)
        o = jnp.concatenate(
            [o[:, hc] * lax.rsqrt(jnp.mean(o[:, hc] * o[:, hc], axis=-1, keepdims=True) + EPS) for hc in hcols], axis=1)
        o_ref[0, rows, :] = (o * ng_ref[...] * _silu(hg_ref[0, rows, 3 * W:4 * W])).astype(BF16)
        bl = b[C - 1:C]
        ke = (kk * jnp.exp(bl - b)).astype(BF16)
        decay = jnp.exp(bl)
        for h, hc in enumerate(hcols):
            kv_t = lax.dot_general(vb[:, hc], ke[:, hc], TN, preferred_element_type=F32)
            st_scr[h] = decay[:, hc] * st_scr[h] + kv_t
        return carry

    lax.fori_loop(0, ts // C, chunk, 0, unroll=8)


def _hgrn_mxu(hg, lb, ng, ts):
    B, S, _ = hg.shape
    vec = pl.BlockSpec((1, HG_WIDTH), lambda b, i: (0, 0))
    return pl.pallas_call(
        _hgrn_mxu_kernel,
        out_shape=jax.ShapeDtypeStruct((B, S, HG_WIDTH), BF16),
        grid=(B, S // ts),
        in_specs=[pl.BlockSpec((1, ts, 4 * HG_WIDTH), lambda b, i: (b, i, 0)), vec, vec],
        out_specs=pl.BlockSpec((1, ts, HG_WIDTH), lambda b, i: (b, i, 0)),
        scratch_shapes=[pltpu.VMEM((HG_HEADS, HG_DV, HG_DK), F32)],
        compiler_params=_cparams(("arbitrary", "arbitrary")),
        name="hgrn_mxu",
    )(hg, lb, ng)


def _hgrn(hg, lb, ng):
    B, S, _ = hg.shape
    col = lambda k: pl.BlockSpec((1, S, HG_DK), lambda b, h, k=k: (b, 0, k * HG_HEADS + h))
    vec = pl.BlockSpec((1, HG_DK), lambda b, h: (0, h))
    return pl.pallas_call(
        _hgrn_kernel,
        out_shape=jax.ShapeDtypeStruct((B, S, HG_WIDTH), BF16),
        grid=(B, HG_HEADS),
        in_specs=[col(0), col(1), col(2), col(3), vec, vec],
        out_specs=pl.BlockSpec((1, S, HG_DV), lambda b, h: (b, 0, h)),
        compiler_params=_cparams(("arbitrary", "arbitrary")),
        name="hgrn",
    )(hg, hg, hg, hg, lb, ng)


def _cmpmlp_kernel(x_ref, w1a_ref, w1b_ref, pe_ref, w1_ref, b1_ref, w2_ref, o_ref):
    x = x_ref[0]
    hb = _dot3(pe_ref[...], w1_ref[...])[0:1] + b1_ref[...]
    nrow = x.shape[0]
    for g in range(NSA_KV_GROUPS):
        a = jnp.dot(x, w1a_ref[g], preferred_element_type=F32)
        bm = jnp.dot(x, w1b_ref[g], preferred_element_type=F32)
        hdn = a + pltpu.roll(bm, nrow - 1, 0) + hb
        o_ref[0, g] = jnp.dot(_gelu_tanh(hdn).astype(BF16), w2_ref[g], preferred_element_type=F32).astype(BF16)


def _cmpmlp(x2, w1a, w1b, pe8, w1, b1, w2p):
    B, ncb, width = x2.shape
    full = lambda a: pl.BlockSpec(a.shape, lambda b: (0,) * a.ndim)
    return pl.pallas_call(
        _cmpmlp_kernel,
        out_shape=jax.ShapeDtypeStruct((B, NSA_KV_GROUPS, ncb, LANES), BF16),
        grid=(B,),
        in_specs=[pl.BlockSpec((1, ncb, width), lambda b: (b, 0, 0)),
                  full(w1a), full(w1b), full(pe8), full(w1), full(b1), full(w2p)],
        out_specs=pl.BlockSpec((1, NSA_KV_GROUPS, ncb, LANES), lambda b: (b, 0, 0, 0)),
        compiler_params=_cparams(("arbitrary",)),
        name="cmpmlp",
    )(x2, w1a, w1b, pe8, w1, b1, w2p)


def _rank_before(score, nrows):
    jrow = lax.broadcasted_iota(jnp.int32, score.shape, 0)
    rank = jnp.zeros(score.shape, F32)
    for k in range(nrows):
        rk = score[k:k + 1]
        beats = (rk > score) | ((rk == score) & (jrow > k))
        rank = rank + jnp.where(beats, 1.0, 0.0)
    return rank


def _topk_rows(score, k):
    n = score.shape[0]
    row = lax.broadcasted_iota(jnp.int32, score.shape, 0).astype(F32)
    keep = jnp.zeros(score.shape, F32)
    for _ in range(k):
        top = jnp.max(score, axis=0, keepdims=True)
        first = jnp.min(jnp.where(score == top, row, float(n)), axis=0, keepdims=True)
        pick = row == first
        keep = jnp.where(pick, 1.0, keep)
        score = jnp.where(pick, -jnp.inf, score)
    return keep


def _transpose_small_ints(xt):
    c = xt.shape[1]
    eye = (lax.broadcasted_iota(jnp.int32, (c, c), 0) == lax.broadcasted_iota(jnp.int32, (c, c), 1)).astype(BF16)
    return lax.dot_general(eye, xt.astype(BF16), NT, preferred_element_type=F32)


def _pack_heads(heads):
    low = lax.broadcasted_iota(jnp.int32, heads[0].shape, 1) < NSA_HD
    return jnp.concatenate([jnp.where(low, heads[k], pltpu.roll(heads[k + 1], NSA_HD, 1))
                            for k in range(0, len(heads), 2)], axis=1)


def _cmpsel_kernel(q_ref, kc_ref, vc_ref, gate_ref, o_ref, sel_ref):
    tq = q_ref.shape[1]
    ncb = kc_ref.shape[2]
    t = pl.program_id(2)
    kc = kc_ref[0, 0]
    vc = vc_ref[0, 0]
    pos = t * tq + lax.broadcasted_iota(jnp.int32, (tq, 1), 0)
    cblk = lax.broadcasted_iota(jnp.int32, (1, ncb), 1)
    vis = (cblk * CMP_STRIDE + CMP_BLOCK - 1) <= pos
    any_vis = pos >= CMP_BLOCK - 1
    psum = jnp.zeros((tq, ncb), F32)
    gate = gate_ref[0]
    heads = []
    for h in range(NSA_HG):
        s = lax.dot_general(q_ref[0, :, h * LANES:(h + 1) * LANES], kc, NT, preferred_element_type=F32)
        s = jnp.where(vis, s, NEG)
        p = jnp.exp2(s - jnp.max(s, axis=-1, keepdims=True))
        p = p * jnp.where(any_vis, 1.0 / jnp.sum(p, axis=-1, keepdims=True), 0.0)
        heads.append(jnp.dot(p.astype(BF16), vc, preferred_element_type=F32) * gate[:, 3 * h:3 * h + 1])
        psum = psum + p
    o_ref[0] = _pack_heads(heads)
    nsb = ncb * CMP_STRIDE // SEL_BLOCK
    jb = lax.broadcasted_iota(jnp.int32, (nsb, ncb), 0) * SEL_BLOCK
    cb = lax.broadcasted_iota(jnp.int32, (nsb, ncb), 1) * CMP_STRIDE
    ov = jnp.maximum(jnp.minimum(cb + CMP_BLOCK, jb + SEL_BLOCK) - jnp.maximum(cb, jb), 0).astype(F32) / CMP_BLOCK
    ps_hi = psum.astype(BF16)
    ps_lo = (psum - ps_hi.astype(F32)).astype(BF16)
    ovb = ov.astype(BF16)
    pslc_t = (lax.dot_general(ovb, ps_hi, NT, preferred_element_type=F32)
              + lax.dot_general(ovb, ps_lo, NT, preferred_element_type=F32))
    posl = t * tq + lax.broadcasted_iota(jnp.int32, (1, tq), 1)
    cur = posl // SEL_BLOCK
    jrow = lax.broadcasted_iota(jnp.int32, (nsb, tq), 0)
    forced = (jrow == 0) | (jrow == cur) | (jrow == cur - 1)
    score = jnp.where(forced, 1e30, jnp.where(jrow <= cur, pslc_t, NEG))
    chosen = _topk_rows(score, min(N_SEL, nsb)) > 0.5
    drop = jnp.where(chosen & (score > -1e29), 0.0, 1.0)
    drop = jnp.concatenate([drop, jnp.zeros((LANES - nsb, tq), F32)], axis=0)
    sel_ref[0, 0] = (_transpose_small_ints(drop) * NEG).astype(BF16)


def _cmpsel(qraw, kcmp, vcmp, gate, tq):
    B, S, _ = qraw.shape
    ncb = kcmp.shape[2]
    gw = NSA_HG * LANES
    assert S // SEL_BLOCK <= NSA_HD
    cmp_spec = pl.BlockSpec((1, 1, ncb, LANES), lambda b, g, t: (b, g, 0, 0))
    return pl.pallas_call(
        _cmpsel_kernel,
        out_shape=(jax.ShapeDtypeStruct((B, S, NSA_WIDTH), F32),
                   jax.ShapeDtypeStruct((B, NSA_KV_GROUPS, S, LANES), BF16)),
        grid=(B, NSA_KV_GROUPS, S // tq),
        in_specs=[pl.BlockSpec((1, tq, gw), lambda b, g, t: (b, t, g)), cmp_spec, cmp_spec,
                  pl.BlockSpec((1, tq, LANES), lambda b, g, t: (b, t, g))],
        out_specs=(pl.BlockSpec((1, tq, NSA_WIDTH // NSA_KV_GROUPS), lambda b, g, t: (b, t, g)),
                   pl.BlockSpec((1, 1, tq, LANES), lambda b, g, t: (b, g, t, 0))),
        compiler_params=_cparams(("arbitrary", "arbitrary", "arbitrary")),
        name="cmpsel",
    )(qraw, kcmp, vcmp, gate)


def _slcwin_kernel(q_ref, sel_ref, ks_ref, kw_ref, vs_ref, vw_ref, gate_ref, ocmp_ref, wgu_ref, wdn_ref,
                   o_ref, wgu_bf_ref, wdn_bf_ref, *, tk, wk):
    wgu_bf_ref[...] = wgu_ref[...].astype(BF16)
    wdn_bf_ref[...] = wdn_ref[...].astype(BF16)
    tq = q_ref.shape[1]
    S = ks_ref.shape[1]
    t = pl.program_id(2)
    q0 = t * tq
    qpos = q0 + lax.broadcasted_iota(jnp.int32, (tq, 1), 0)
    qw = [q_ref[0, :, h * LANES:(h + 1) * LANES] for h in range(NSA_HG)]
    sel = sel_ref[0, 0]
    qa = [jnp.concatenate([q, sel], axis=1) for q in qw]

    gate = gate_ref[0]

    def normalise(acc, g):
        return acc * (g / acc[:, NSA_HD:NSA_HD + 1])

    nchain = NSA_HG // SLC_STACK
    qst = [jnp.concatenate(qa[c * SLC_STACK:(c + 1) * SLC_STACK], axis=0) for c in range(nchain)]
    qpos_st = jnp.concatenate([qpos] * SLC_STACK, axis=0)

    def kstep(j, carry, diag):
        k0 = pl.multiple_of(j * tk, tk)
        ks = ks_ref[0, pl.ds(k0, tk), :]
        vs = vs_ref[0, pl.ds(k0, tk), :]
        if diag:
            vis = (k0 + lax.broadcasted_iota(jnp.int32, (1, tk), 1)) <= qpos_st
        out = []
        for c in range(nchain):
            m, acc = carry[c]
            s = lax.dot_general(qst[c], ks, NT, preferred_element_type=F32)
            if diag:
                s = jnp.where(vis, s, NEG)
            mn = jnp.maximum(m, jnp.max(s, axis=-1, keepdims=True))
            p = jnp.exp2(s - mn)
            acc = jnp.exp2(m - mn) * acc + jnp.dot(p.astype(BF16), vs, preferred_element_type=F32)
            out.append((mn, acc))
        return tuple(out)

    init = tuple((jnp.full((SLC_STACK * tq, 1), NEG, F32), jnp.zeros((SLC_STACK * tq, LANES), F32))
                 for _ in range(nchain))
    jdiag = q0 // tk
    carry = lax.fori_loop(0, jdiag, lambda j, c: kstep(j, c, False), init)
    carry = kstep(jdiag, carry, True)
    heads = [normalise(carry[h // SLC_STACK][1][(h % SLC_STACK) * tq:(h % SLC_STACK + 1) * tq],
                       gate[:, 3 * h + 1:3 * h + 2]) for h in range(NSA_HG)]

    wq = wk - WINDOW
    wins = [[] for _ in range(NSA_HG)]
    spos = lax.broadcasted_iota(jnp.int32, (NSA_HG * wq, 1), 0) % wq
    for i in range(tq // wq):
        rows = slice(i * wq, (i + 1) * wq)
        start = pl.multiple_of(jnp.maximum(q0 + (i + 1) * wq - wk, 0), wq)
        kw = kw_ref[0, pl.ds(start, wk), :]
        vw = vw_ref[0, pl.ds(start, wk), :]
        d = (q0 + i * wq + spos) - (start + lax.broadcasted_iota(jnp.int32, (1, wk), 1))
        q_st = jnp.concatenate([q[rows] for q in qw], axis=0)
        s = jnp.where((d >= 0) & (d < WINDOW), lax.dot_general(q_st, kw, NT, preferred_element_type=F32), NEG)
        p = jnp.exp2(s - jnp.max(s, axis=-1, keepdims=True))
        o_st = jnp.dot(p.astype(BF16), vw, preferred_element_type=F32)
        for h in range(NSA_HG):
            wins[h].append(o_st[h * wq:(h + 1) * wq])
    for h in range(NSA_HG):
        heads[h] = heads[h] + normalise(jnp.concatenate(wins[h], axis=0), gate[:, 3 * h + 2:3 * h + 3])
    o_ref[0] = (ocmp_ref[0] + _pack_heads(heads)).astype(BF16)


def _slcwin(qrot, sel, ks, kw, vaug, gate, ocmp, wgu, wdn, tq, tk):
    B, S, _ = qrot.shape
    gw = NSA_HG * LANES
    wk = WINDOW + WIN_Q
    nt = S // tq
    nstep = B * NSA_KV_GROUPS * nt
    assert S >= wk and tq % WIN_Q == 0 and N_EXPERTS % nstep == 0
    epb = N_EXPERTS // nstep
    ospec = pl.BlockSpec((1, tq, NSA_WIDTH // NSA_KV_GROUPS), lambda b, g, t: (b, t, g))
    wspec = lambda a: pl.BlockSpec((epb,) + a.shape[1:], lambda b, g, t: ((b * NSA_KV_GROUPS + g) * nt + t, 0, 0))
    return pl.pallas_call(
        functools.partial(_slcwin_kernel, tk=tk, wk=wk),
        out_shape=(jax.ShapeDtypeStruct((B, S, NSA_WIDTH), BF16),
                   jax.ShapeDtypeStruct(wgu.shape, BF16), jax.ShapeDtypeStruct(wdn.shape, BF16)),
        grid=(B, NSA_KV_GROUPS, nt),
        in_specs=[pl.BlockSpec((1, tq, gw), lambda b, g, t: (b, t, g)),
                  pl.BlockSpec((1, 1, tq, LANES), lambda b, g, t: (b, g, t, 0)),
                  pl.BlockSpec((1, S, 2 * LANES), lambda b, g, t: (b, 0, 0)),
                  pl.BlockSpec((1, S, LANES), lambda b, g, t: (b, 0, 0)),
                  pl.BlockSpec((1, S, LANES), lambda b, g, t: (b, 0, g)),
                  pl.BlockSpec((1, S, LANES), lambda b, g, t: (b, 0, NSA_KV_GROUPS + g)),
                  pl.BlockSpec((1, tq, LANES), lambda b, g, t: (b, t, g)),
                  ospec, wspec(wgu), wspec(wdn)],
        out_specs=(ospec, wspec(wgu), wspec(wdn)),
        compiler_params=_cparams(("arbitrary", "arbitrary", "arbitrary")),
        name="slcwin",
    )(qrot, sel, ks, kw, vaug, vaug, gate, ocmp, wgu, wdn)


def _outproj_kernel(x_ref, ohg_ref, onsa_ref, wout_ref,
                    g1_ref, sh2_ref, sc2_ref, n2_ref, rwt_ref, rb_ref,
                    x1_ref, h2_ref, sc_ref, slot_t_ref, w_t_ref, cnt_ref):
    tm = x_ref.shape[1]
    mix = jnp.dot(jnp.concatenate([ohg_ref[0], onsa_ref[0]], axis=1), wout_ref[...], preferred_element_type=F32)
    x1 = x_ref[0] + g1_ref[0] * mix
    x1_ref[0] = x1
    h2 = (x1 * lax.rsqrt(jnp.mean(x1 * x1, axis=-1, keepdims=True) + EPS) * n2_ref[...]) * (1.0 + sc2_ref[0]) + sh2_ref[0]
    h_hi = h2.astype(BF16)
    h2_ref[0] = h_hi
    h_lo = (h2 - h_hi.astype(F32)).astype(BF16)
    both = lax.dot_general(rwt_ref[...], h_hi, NT, preferred_element_type=F32)
    logits = (both[:N_EXPERTS] + both[N_EXPERTS:]
              + lax.dot_general(rwt_ref[:N_EXPERTS, :], h_lo, NT, preferred_element_type=F32))
    scores = _sigmoid(logits)
    choice = scores + rb_ref[...]
    per = N_EXPERTS // N_GROUPS
    c3 = choice.reshape(N_GROUPS, per, tm)
    erow = lax.broadcasted_iota(jnp.int32, c3.shape, 1)
    rank_in = jnp.zeros(c3.shape, F32)
    for k in range(per):
        ck = c3[:, k:k + 1, :]
        rank_in = rank_in + jnp.where((ck > c3) | ((ck == c3) & (erow > k)), 1.0, 0.0)
    grp_score = jnp.sum(jnp.where(rank_in < 2, c3, 0.0), axis=1)
    grp_keep = _rank_before(grp_score, N_GROUPS) < TOPK_GROUPS
    masked = jnp.where(grp_keep[:, None, :], c3, -jnp.inf).reshape(N_EXPERTS, tm)
    keep = _topk_rows(masked, TOP_K) > 0.5
    tw = jnp.where(keep, scores, 0.0)
    tw = tw / jnp.sum(tw, axis=0, keepdims=True) * ROUTED_SCALE
    before = lax.broadcasted_iota(jnp.int32, (tm, tm), 0)
    after = lax.broadcasted_iota(jnp.int32, (tm, tm), 1)
    earlier = (before < after) & (before // MOE_SUB == after // MOE_SUB)
    kept = jnp.where(keep, 1.0, 0.0)
    pref = jnp.dot(kept.astype(BF16), earlier.astype(BF16), preferred_element_type=F32)
    slot = jnp.where(keep, pref, -1.0)
    assert MOE_SUB <= 256
    sc_ref[0] = _transpose_small_ints(slot)
    for j in range(tm // MOE_SUB):
        sub = slice(j * MOE_SUB, (j + 1) * MOE_SUB)
        slot_t_ref[:, j] = slot[:, sub].reshape(N_EXPERTS // MOE_EG, MOE_EG, MOE_SUB)
        w_t_ref[:, j] = tw[:, sub].reshape(N_EXPERTS // MOE_EG, MOE_EG, MOE_SUB)
        cnt_ref[j] = jnp.broadcast_to(jnp.sum(kept[:, sub], axis=1, keepdims=True), (N_EXPERTS, LANES))


def _outproj(x, ohg, onsa, wout, mod3, n2, rwt, rb, tm):
    B, S, D = x.shape
    blk = lambda w: pl.BlockSpec((1, tm, w), lambda b, i: (b, i, 0))
    full = lambda a: pl.BlockSpec(a.shape, lambda b, i: (0,) * a.ndim)
    modc = lambda k: pl.BlockSpec((1, 1, D), lambda b, i, k=k: (b, 0, k))
    nt = S // tm
    ng = N_EXPERTS // MOE_EG
    nh = tm // MOE_SUB
    tspec = pl.BlockSpec((ng, nh, MOE_EG, MOE_SUB), lambda b, i: (0, b * nt + i, 0, 0))
    return pl.pallas_call(
        _outproj_kernel,
        out_shape=(jax.ShapeDtypeStruct((B, S, D), F32),
                   jax.ShapeDtypeStruct((B, S, D), BF16),
                   jax.ShapeDtypeStruct((B, S, N_EXPERTS), F32),
                   jax.ShapeDtypeStruct((ng, B * nt * nh, MOE_EG, MOE_SUB), F32),
                   jax.ShapeDtypeStruct((ng, B * nt * nh, MOE_EG, MOE_SUB), F32),
                   jax.ShapeDtypeStruct((B * nt * nh, N_EXPERTS, LANES), F32)),
        grid=(B, nt),
        in_specs=[blk(D), blk(HG_WIDTH), blk(NSA_WIDTH), full(wout),
                  modc(2), modc(3), modc(4), full(n2), full(rwt), full(rb)],
        out_specs=(blk(D), blk(D), blk(N_EXPERTS), tspec, tspec,
                   pl.BlockSpec((nh, N_EXPERTS, LANES), lambda b, i: (b * nt + i, 0, 0))),
        compiler_params=_cparams(("arbitrary", "arbitrary")),
        name="outproj",
    )(x, ohg, onsa, wout, mod3, mod3, mod3, n2, rwt, rb)


def _swiglu_hidden(x, wgu):
    gu = jnp.dot(x, wgu, preferred_element_type=F32)
    return _silu(gu[:, :EXPERT_HIDDEN]) * gu[:, EXPERT_HIDDEN:]


def _moe_kernel(rounds_ref, h_ref, slot_t_ref, w_t_ref, slot_c_ref, wgu_ref, wdn_ref, sgu_ref, sdn_ref,
                o_ref, x_scr, p_scr):
    i = pl.program_id(0)
    g = pl.program_id(1)
    tm = h_ref.shape[0]
    nsub = tm // MOE_SUB
    rnd, eg = MOE_ROUND, MOE_EG
    ng = N_EXPERTS // eg
    base = i * (N_EXPERTS + ng)

    @pl.when(g == 0)
    def _():
        act = _swiglu_hidden(h_ref[...], sgu_ref[...]).astype(BF16)
        o_ref[...] = jnp.dot(act, sdn_ref[...], preferred_element_type=F32)

    blk = eg * rnd
    rslot = lax.broadcasted_iota(jnp.int32, (rnd, 1), 0).astype(F32)

    def gather(first, nround):
        for s in range(nsub):
            st = slot_t_ref[0, s] - first
            p = jnp.concatenate([jnp.where(rslot + float(w * rnd) == st[e:e + 1, :], 1.0, 0.0)
                                 for w in range(nround) for e in range(eg)], axis=0)
            p_scr[s, 0:nround * blk, :] = p.astype(BF16)
            x_scr[s, 0:nround * blk, :] = jnp.dot(p_scr[s, 0:nround * blk, :], h_ref[s * MOE_SUB:(s + 1) * MOE_SUB, :],
                                                  preferred_element_type=F32).astype(BF16)

    def run_expert(e, w):
        start = w * blk + e * rnd
        rows = pl.ds(start if isinstance(e, int) else pl.multiple_of(start, rnd), rnd)
        xe = jnp.concatenate([x_scr[s, rows, :] for s in range(nsub)], axis=0)
        wslot = jnp.concatenate(
            [jnp.sum(p_scr[s, rows, :].astype(F32) * w_t_ref[0, s, pl.ds(e, 1), :], axis=-1, keepdims=True)
             for s in range(nsub)], axis=0)
        y = jnp.dot((_swiglu_hidden(xe, wgu_ref[e]) * wslot).astype(BF16), wdn_ref[e], preferred_element_type=F32)
        for s in range(nsub):
            x_scr[s, rows, :] = y[s * rnd:(s + 1) * rnd].astype(BF16)

    def run_active(r, w):
        def body(e, carry):
            pl.when(rounds_ref[base + g * eg + e] > r)(functools.partial(run_expert, e, w))
            return carry
        lax.fori_loop(0, eg, body, 0)

    def scatter(first, nround):
        lane = lax.broadcasted_iota(jnp.int32, (1, nround * blk), 1)
        lane_slot = ((lane // blk) * rnd + lane % rnd).astype(F32)
        spread = ((lax.broadcasted_iota(jnp.int32, (N_EXPERTS, nround * blk), 1) % blk) // rnd + g * eg
                  == lax.broadcasted_iota(jnp.int32, (N_EXPERTS, nround * blk), 0)).astype(BF16)
        for s in range(nsub):
            toks = slice(s * MOE_SUB, (s + 1) * MOE_SUB)
            sc = jnp.dot(slot_c_ref[toks, :].astype(BF16), spread, preferred_element_type=F32) - first
            pt = jnp.where(sc == lane_slot, 1.0, 0.0).astype(BF16)
            o_ref[toks, :] += jnp.dot(pt, x_scr[s, 0:nround * blk, :], preferred_element_type=F32)

    gather(0.0, MOE_FIRST)
    for e in range(eg):
        run_expert(e, 0)
    for w in range(1, MOE_FIRST):
        run_active(w, w)
    scatter(0.0, MOE_FIRST)

    def later_round(r, carry):
        first = (r * rnd).astype(F32)
        gather(first, 1)
        run_active(r, 0)
        scatter(first, 1)
        return carry

    lax.fori_loop(MOE_FIRST, rounds_ref[base + N_EXPERTS + g], later_round, 0)


def _moe(rounds, h2, slot_t, w_t, slot_c, wgu, wdn, sgu, sdn, tm):
    T, D = h2.shape
    eg, nsub = MOE_EG, tm // MOE_SUB
    full = lambda a: pl.BlockSpec(a.shape, lambda i, e, o: (0,) * a.ndim)
    tspec = pl.BlockSpec((1, nsub, eg, MOE_SUB), lambda i, e, o: (e, i, 0, 0))
    return pl.pallas_call(
        _moe_kernel,
        out_shape=jax.ShapeDtypeStruct((T, D), F32),
        grid_spec=pltpu.PrefetchScalarGridSpec(
            num_scalar_prefetch=1,
            grid=(T // tm, N_EXPERTS // eg),
            in_specs=[pl.BlockSpec((tm, D), lambda i, e, o: (i, 0), pipeline_mode=pl.Buffered(1)), tspec, tspec,
                      pl.BlockSpec((tm, N_EXPERTS), lambda i, e, o: (i, 0)),
                      pl.BlockSpec((eg, D, 2 * EXPERT_HIDDEN), lambda i, e, o: (e, 0, 0)),
                      pl.BlockSpec((eg, EXPERT_HIDDEN, D), lambda i, e, o: (e, 0, 0)),
                      full(sgu), full(sdn)],
            out_specs=pl.BlockSpec((tm, D), lambda i, e, o: (i, 0), pipeline_mode=pl.Buffered(1)),
            scratch_shapes=[pltpu.VMEM((nsub, MOE_FIRST * eg * MOE_ROUND, D), BF16),
                            pltpu.VMEM((nsub, MOE_FIRST * eg * MOE_ROUND, MOE_SUB), BF16)]),
        compiler_params=_cparams(("arbitrary", "arbitrary")),
        name="moe",
    )(rounds, h2, slot_t, w_t, slot_c, wgu, wdn, sgu, sdn)


def _final_kernel(x1_ref, moe_ref, g2_ref, fg_ref, o_ref):
    x2 = x1_ref[0] + g2_ref[0] * moe_ref[0]
    o_ref[0] = x2 * lax.rsqrt(jnp.mean(x2 * x2, axis=-1, keepdims=True) + EPS) * fg_ref[...]


def _final(x1, moe, mod3, fg, tm):
    B, S, D = x1.shape
    blk = pl.BlockSpec((1, tm, D), lambda b, i: (b, i, 0))
    return pl.pallas_call(
        _final_kernel,
        out_shape=jax.ShapeDtypeStruct((B, S, D), F32),
        grid=(B, S // tm),
        in_specs=[blk, blk, pl.BlockSpec((1, 1, D), lambda b, i: (b, 0, 5)), pl.BlockSpec((1, D), lambda b, i: (0, 0))],
        out_specs=blk,
        compiler_params=_cparams(("arbitrary", "arbitrary")),
        name="final",
    )(x1, moe, mod3, fg)


def _split_bf16(w):
    hi = w.astype(BF16)
    return jnp.concatenate([hi, (w - hi.astype(F32)).astype(BF16)], axis=0)


def _pack_w_in(w_in):
    return jnp.pad(w_in, ((0, 0), (0, IN_COLS_P - w_in.shape[1]))).astype(BF16)


def _pack_cmp(pos, w1, b1, w2, lane_by_group):
    half = CMP_STRIDE * NSA_HD
    def rows_for(wh):
        w3 = wh.reshape(CMP_STRIDE, NSA_HD, CMP_HIDDEN)
        z = jnp.zeros_like(w3)
        return jnp.stack([jnp.concatenate([w3, z], axis=1).reshape(CMP_STRIDE * LANES, CMP_HIDDEN),
                          jnp.concatenate([z, w3], axis=1).reshape(CMP_STRIDE * LANES, CMP_HIDDEN)])
    w1a = rows_for(w1[:half]).astype(BF16)
    w1b = rows_for(w1[half:]).astype(BF16)
    z2 = jnp.zeros_like(w2)
    w2_first = jnp.concatenate([w2, z2], axis=1)
    w2p = jnp.stack([w2_first, jnp.concatenate([z2, w2], axis=1) if lane_by_group else w2_first]).astype(BF16)
    pe8 = jnp.pad(pos.reshape(1, CMP_BLOCK * NSA_HD), ((0, 7), (0, 0)))
    return w1a, w1b, pe8, w1, b1.reshape(1, CMP_HIDDEN), w2p


def _rope_tables(S):
    half = NSA_HD // 2
    inv = ROPE_THETA ** (-jnp.arange(half, dtype=F32) / half)
    ang = jnp.arange(S, dtype=F32)[:, None] * inv[None, :]
    cos, sin = jnp.cos(ang), jnp.sin(ang)
    reps = LANES // NSA_HD
    return jnp.tile(jnp.concatenate([cos, cos], axis=1), (1, reps)), jnp.tile(jnp.concatenate([-sin, sin], axis=1), (1, reps))


def _tiles(S):
    return dict(inproj=min(512, S), hgrn=min(512, S), cmpsel=min(512, S), slc_q=512, slc_k=512, moe=min(2048, S), final=min(512, S))


def kernel(x, c, w_ada, b_ada, norm1_g, w_in, hg_lb_logits, hg_norm_g, cmp_pos_k, cmp_w1_k, cmp_b1_k, cmp_w2_k,
           cmp_pos_v, cmp_w1_v, cmp_b1_v, cmp_w2_v, w_out, norm2_g, router_w, router_bias, w_exp_gu, w_exp_dn,
           w_sh_gu, w_sh_dn, final_g):
    B, S, D = x.shape
    assert D == D_MODEL and w_ada.shape[0] == 1 and S % 512 == 0
    tl = _tiles(S)
    l = 0
    lb = jnp.cumsum(jax.nn.softmax(hg_lb_logits.astype(F32), axis=0), axis=0)[l].reshape(1, HG_WIDTH)
    c8 = jnp.pad(c, ((0, 8 - B), (0, 0)))
    mod3 = _ada(c8, w_ada[l], b_ada[l].reshape(1, -1))[:B].reshape(B, 1, 6 * D)
    cos, sin = _rope_tables(S)
    hg, qraw, qrot, kc, vc, ks, kw, vaug, gate = _inproj(x, mod3, norm1_g[l].reshape(1, D), _pack_w_in(w_in[l]),
                                                         cos, sin, tl["inproj"])
    ng = hg_norm_g[l].reshape(1, HG_WIDTH)
    mxu_safe = -HG_SUB * jnp.log(jnp.min(lb)) <= HG_SAFE_LOG_RANGE
    ohg = lax.cond(mxu_safe, lambda: _hgrn_mxu(hg, lb, ng, tl["hgrn"]), lambda: _hgrn(hg, lb, ng))
    ncb = S // CMP_STRIDE
    kcmp = _cmpmlp(kc.reshape(B, ncb, CMP_STRIDE * LANES),
                   *_pack_cmp(cmp_pos_k[l], cmp_w1_k[l], cmp_b1_k[l], cmp_w2_k[l], True))
    vcmp = _cmpmlp(vc.reshape(B, ncb, CMP_STRIDE * LANES),
                   *_pack_cmp(cmp_pos_v[l], cmp_w1_v[l], cmp_b1_v[l], cmp_w2_v[l], False))
    ocmp, sel = _cmpsel(qraw, kcmp, vcmp, gate, tl["cmpsel"])
    onsa, wgu_bf, wdn_bf = _slcwin(qrot, sel, ks, kw, vaug, gate, ocmp, w_exp_gu[l], w_exp_dn[l],
                                   tl["slc_q"], tl["slc_k"])
    x1, h2, sc, slot_t, w_t, cnt = _outproj(x, ohg, onsa, w_out[l].astype(BF16), mod3,
                                            norm2_g[l].reshape(1, D), _split_bf16(router_w[l].T),
                                            router_bias[l].reshape(N_EXPERTS, 1), ROUTE_TM)
    T = B * S
    tm = tl["moe"]
    ng = N_EXPERTS // MOE_EG
    slot_c = sc.reshape(T, N_EXPERTS)
    load = jnp.max(cnt[:, :, 0].reshape(T // tm, tm // MOE_SUB, N_EXPERTS), axis=1).astype(jnp.int32)
    per_expert = (load + MOE_ROUND - 1) // MOE_ROUND
    rounds = jnp.concatenate([per_expert, jnp.max(per_expert.reshape(T // tm, ng, MOE_EG), axis=2)], axis=1).reshape(-1)
    moe = _moe(rounds, h2.reshape(T, D), slot_t, w_t, slot_c,
               wgu_bf, wdn_bf, w_sh_gu[l].astype(BF16), w_sh_dn[l].astype(BF16), tm)
    return _final(x1, moe.reshape(B, S, D), mod3, final_g.reshape(1, D), tl["final"])
```
